```python
import jax, jax.numpy as jnp
from jax import lax
import numpy as np

D_MODEL = 2048
BATCH = 8
SEQ = 4096
DEPTH = 2

N_MIXERS = 2
HGRN_EXPAND = 128
HGRN_HEADS = D_MODEL // HGRN_EXPAND
HGRN_F_DIM = HGRN_HEADS * HGRN_EXPAND
HGRN_I_DIM = D_MODEL
HGRN_HEAD_I = HGRN_I_DIM // HGRN_HEADS
CHUNK = 64
CONV_WIDTH = 3
D_FF = 5632
EPS = 1e-6
N_HGRN = (DEPTH + 1) // 2
N_SC = DEPTH // 2

kernel_name = "hgrn2_shortconv_interleaved_trunk"


def rms_norm(x, w):
    x32 = x.astype(jnp.float32)
    y = x32 * lax.rsqrt(jnp.mean(x32 * x32, axis=-1, keepdims=True) + EPS)
    return (y * w.astype(jnp.float32)).astype(x.dtype)


def causal_dwconv(u, w):
    K = w.shape[0]
    T = u.shape[1]
    up = jnp.pad(u, ((0, 0), (K - 1, 0), (0, 0)))
    return sum(up[:, k:k + T] * w[k] for k in range(K))


def chunked_gated_recurrence(q, k, v, logf):
    B, T, H, dk = q.shape
    dv = v.shape[-1]
    n = T // CHUNK

    def to_chunks(a):
        return a.reshape(B, n, CHUNK, H, a.shape[-1]).transpose(1, 0, 3, 2, 4)

    causal = jnp.tril(jnp.ones((CHUNK, CHUNK), dtype=bool))

    def step(S, inp):
        qc, kc, vc, gc = inp
        b = jnp.cumsum(gc, axis=2)
        rel = b[:, :, :, None, :] - b[:, :, None, :, :]
        rel = jnp.where(causal[:, :, None], rel, -jnp.inf)
        A = jnp.einsum('bhtd,bhsd,bhtsd->bhts', qc, kc, jnp.exp(rel))
        o = jnp.einsum('bhts,bhsv->bhtv', A, vc) + jnp.einsum('bhtd,bhdv->bhtv', qc * jnp.exp(b), S)
        b_last = b[:, :, -1:, :]
        S = jnp.exp(b_last[:, :, 0, :])[..., None] * S + jnp.einsum(
            'bhsd,bhsv->bhdv', kc * jnp.exp(b_last - b), vc)
        return S, o

    S0 = jnp.zeros((B, H, dk, dv), jnp.float32)
    _, o = lax.scan(step, S0, (to_chunks(q), to_chunks(k), to_chunks(v), to_chunks(logf)))
    return o.transpose(1, 0, 3, 2, 4).reshape(B, T, H * dv)


def hgrn2_mixer(xn, w_in, lb, out_gain, w_out):
    B, T, _ = xn.shape
    proj = xn @ w_in
    q, fr, v, g = jnp.split(proj, [HGRN_F_DIM, 2 * HGRN_F_DIM, 2 * HGRN_F_DIM + HGRN_I_DIM], axis=-1)
    q = jax.nn.silu(q.astype(jnp.float32)) * (HGRN_EXPAND ** -0.5)
    f = lb + (1.0 - lb) * jax.nn.sigmoid(fr.astype(jnp.float32))
    k = 1.0 - f
    logf = jnp.log(f)

    def heads(a, d):
        return a.reshape(B, T, HGRN_HEADS, d).astype(jnp.float32)

    o = chunked_gated_recurrence(heads(q, HGRN_EXPAND), heads(k, HGRN_EXPAND),
                                 heads(v, HGRN_HEAD_I), heads(logf, HGRN_EXPAND))
    o = rms_norm(o, out_gain) * jax.nn.silu(g.astype(jnp.float32))
    return (o @ w_out.astype(jnp.float32)).astype(xn.dtype)


def short_conv_mixer(xn, w_in, conv_w, w_out):
    proj = xn @ w_in
    gb, gc, h = jnp.split(proj, 3, axis=-1)
    y = gb * causal_dwconv(gc * h, conv_w)
    return y @ w_out


def conv_glu_ffn(xn, w_up, conv_w, w_down):
    u = causal_dwconv(xn @ w_up, conv_w)
    gate, val = jnp.split(u, 2, axis=-1)
    return (jax.nn.silu(gate) * val) @ w_down


def _fwd_setup_inputs(seed: int = 0) -> dict:
    key = jax.random.key(seed)
    ks = jax.random.split(key, 16)
    f32 = jnp.float32

    def w(k, shape, fan_in):
        return jax.random.normal(k, shape, f32) * (fan_in ** -0.5)

    def gain(k, shape):
        return 1.0 + 0.02 * jax.random.normal(k, shape, f32)

    return {
        "x": jax.random.normal(ks[0], (BATCH, SEQ, D_MODEL), f32),
        "norm_mix": gain(ks[1], (DEPTH, D_MODEL)),
        "norm_ffn": gain(ks[2], (DEPTH, D_MODEL)),
        "hgrn_w_in": w(ks[3], (N_HGRN, D_MODEL, 2 * HGRN_F_DIM + 2 * HGRN_I_DIM), D_MODEL),
        "hgrn_lb_table": 0.1 * jax.random.normal(ks[4], (DEPTH + 1, HGRN_F_DIM), f32),
        "hgrn_out_norm": gain(ks[5], (N_HGRN, HGRN_I_DIM)),
        "hgrn_w_out": w(ks[6], (N_HGRN, HGRN_I_DIM, D_MODEL), HGRN_I_DIM),
        "sc_w_in": w(ks[7], (N_SC, D_MODEL, 3 * D_MODEL), D_MODEL),
        "sc_conv": w(ks[8], (N_SC, CONV_WIDTH, D_MODEL), CONV_WIDTH),
        "sc_w_out": w(ks[9], (N_SC, D_MODEL, D_MODEL), D_MODEL),
        "ffn_w_up": w(ks[10], (DEPTH, D_MODEL, 2 * D_FF), D_MODEL),
        "ffn_conv": w(ks[11], (DEPTH, CONV_WIDTH, 2 * D_FF), CONV_WIDTH),
        "ffn_w_down": w(ks[12], (DEPTH, D_FF, D_MODEL), D_FF),
        "final_norm": gain(ks[13], (D_MODEL,)),
    }


def _fwd_reference(x, norm_mix, norm_ffn, hgrn_w_in, hgrn_lb_table, hgrn_out_norm, hgrn_w_out,
              sc_w_in, sc_conv, sc_w_out, ffn_w_up, ffn_conv, ffn_w_down, final_norm):
    lb_all = jnp.cumsum(jax.nn.softmax(hgrn_lb_table.astype(jnp.float32), axis=0), axis=0)
    h = x
    for i in range(DEPTH):
        j = i // N_MIXERS
        xn = rms_norm(h, norm_mix[i])
        if i % N_MIXERS == 0:
            mix = hgrn2_mixer(xn, hgrn_w_in[j], lb_all[i], hgrn_out_norm[j], hgrn_w_out[j])
        else:
            mix = short_conv_mixer(xn, sc_w_in[j], sc_conv[j], sc_w_out[j])
        h = h + mix.astype(h.dtype)
        h = h + conv_glu_ffn(rms_norm(h, norm_ffn[i]), ffn_w_up[i], ffn_conv[i], ffn_w_down[i]).astype(h.dtype)
    return rms_norm(h, final_norm)


import jax as _jax
import jax.numpy as _jnp

TWIN_FORMAT = 'train_step'
FWD_PARAMS = ['x', 'norm_mix', 'norm_ffn', 'hgrn_w_in', 'hgrn_lb_table', 'hgrn_out_norm', 'hgrn_w_out', 'sc_w_in', 'sc_conv', 'sc_w_out', 'ffn_w_up', 'ffn_conv', 'ffn_w_down', 'final_norm']
TWIN_WEIGHTS = ['norm_mix', 'norm_ffn', 'hgrn_w_in', 'hgrn_lb_table', 'hgrn_out_norm', 'hgrn_w_out', 'sc_w_in', 'sc_conv', 'sc_w_out', 'ffn_w_up', 'ffn_conv', 'ffn_w_down', 'final_norm']
TWIN_DIFF_INPUT = 'x'
TWIN_INPUTS = ['x', 'norm_mix', 'norm_ffn', 'hgrn_w_in', 'hgrn_lb_table', 'hgrn_out_norm', 'hgrn_w_out', 'sc_w_in', 'sc_conv', 'sc_w_out', 'ffn_w_up', 'ffn_conv', 'ffn_w_down', 'final_norm', 'loss_target', 'm_norm_mix', 'm_norm_ffn', 'm_hgrn_w_in', 'm_hgrn_lb_table', 'm_hgrn_out_norm', 'm_hgrn_w_out', 'm_sc_w_in', 'm_sc_conv', 'm_sc_w_out', 'm_ffn_w_up', 'm_ffn_conv', 'm_ffn_w_down', 'm_final_norm', 'v_norm_mix', 'v_norm_ffn', 'v_hgrn_w_in', 'v_hgrn_lb_table', 'v_hgrn_out_norm', 'v_hgrn_w_out', 'v_sc_w_in', 'v_sc_conv', 'v_sc_w_out', 'v_ffn_w_up', 'v_ffn_conv', 'v_ffn_w_down', 'v_final_norm']
TWIN_OUTPUTS = ['loss', 'grad_x', 'grad_norm_mix', 'grad_norm_ffn', 'grad_hgrn_w_in', 'grad_hgrn_lb_table', 'grad_hgrn_out_norm', 'grad_hgrn_w_out', 'grad_sc_w_in', 'grad_sc_conv', 'grad_sc_w_out', 'grad_ffn_w_up', 'grad_ffn_conv', 'grad_ffn_w_down', 'grad_final_norm', 'delta_norm_mix', 'delta_norm_ffn', 'delta_hgrn_w_in', 'delta_hgrn_lb_table', 'delta_hgrn_out_norm', 'delta_hgrn_w_out', 'delta_sc_w_in', 'delta_sc_conv', 'delta_sc_w_out', 'delta_ffn_w_up', 'delta_ffn_conv', 'delta_ffn_w_down', 'delta_final_norm', 'new_m_norm_mix', 'new_m_norm_ffn', 'new_m_hgrn_w_in', 'new_m_hgrn_lb_table', 'new_m_hgrn_out_norm', 'new_m_hgrn_w_out', 'new_m_sc_w_in', 'new_m_sc_conv', 'new_m_sc_w_out', 'new_m_ffn_w_up', 'new_m_ffn_conv', 'new_m_ffn_w_down', 'new_m_final_norm', 'new_v_norm_mix', 'new_v_norm_ffn', 'new_v_hgrn_w_in', 'new_v_hgrn_lb_table', 'new_v_hgrn_out_norm', 'new_v_hgrn_w_out', 'new_v_sc_w_in', 'new_v_sc_conv', 'new_v_sc_w_out', 'new_v_ffn_w_up', 'new_v_ffn_conv', 'new_v_ffn_w_down', 'new_v_final_norm']
TWIN_LEAF_KINDS = {'loss': 'loss', 'grad_x': 'grad_x', 'grad_norm_mix': 'grad_w', 'grad_norm_ffn': 'grad_w', 'grad_hgrn_w_in': 'grad_w', 'grad_hgrn_lb_table': 'grad_w', 'grad_hgrn_out_norm': 'grad_w', 'grad_hgrn_w_out': 'grad_w', 'grad_sc_w_in': 'grad_w', 'grad_sc_conv': 'grad_w', 'grad_sc_w_out': 'grad_w', 'grad_ffn_w_up': 'grad_w', 'grad_ffn_conv': 'grad_w', 'grad_ffn_w_down': 'grad_w', 'grad_final_norm': 'grad_w', 'delta_norm_mix': 'delta_w', 'delta_norm_ffn': 'delta_w', 'delta_hgrn_w_in': 'delta_w', 'delta_hgrn_lb_table': 'delta_w', 'delta_hgrn_out_norm': 'delta_w', 'delta_hgrn_w_out': 'delta_w', 'delta_sc_w_in': 'delta_w', 'delta_sc_conv': 'delta_w', 'delta_sc_w_out': 'delta_w', 'delta_ffn_w_up': 'delta_w', 'delta_ffn_conv': 'delta_w', 'delta_ffn_w_down': 'delta_w', 'delta_final_norm': 'delta_w', 'new_m_norm_mix': 'new_m', 'new_m_norm_ffn': 'new_m', 'new_m_hgrn_w_in': 'new_m', 'new_m_hgrn_lb_table': 'new_m', 'new_m_hgrn_out_norm': 'new_m', 'new_m_hgrn_w_out': 'new_m', 'new_m_sc_w_in': 'new_m', 'new_m_sc_conv': 'new_m', 'new_m_sc_w_out': 'new_m', 'new_m_ffn_w_up': 'new_m', 'new_m_ffn_conv': 'new_m', 'new_m_ffn_w_down': 'new_m', 'new_m_final_norm': 'new_m', 'new_v_norm_mix': 'new_v', 'new_v_norm_ffn': 'new_v', 'new_v_hgrn_w_in': 'new_v', 'new_v_hgrn_lb_table': 'new_v', 'new_v_hgrn_out_norm': 'new_v', 'new_v_hgrn_w_out': 'new_v', 'new_v_sc_w_in': 'new_v', 'new_v_sc_conv': 'new_v', 'new_v_sc_w_out': 'new_v', 'new_v_ffn_w_up': 'new_v', 'new_v_ffn_conv': 'new_v', 'new_v_ffn_w_down': 'new_v', 'new_v_final_norm': 'new_v'}


def _forward(args):
    return _fwd_reference(*[args[k] for k in FWD_PARAMS])


def _output_shape():
    def fwd():
        inp = _fwd_setup_inputs(0)
        return _fwd_reference(*[inp[k] for k in FWD_PARAMS])
    out = _jax.eval_shape(fwd)
    return out.shape, out.dtype

N_MICROBATCH = 1
ADAM_LR = 0.001
ADAM_B1 = 0.9
ADAM_B2 = 0.999
ADAM_EPS = 1e-08
ADAM_WD = 0.01
ADAM_STEP = 10
PER_EXAMPLE_BATCH_AXIS = {'x': 0, 'loss_target': 0}
SHARED_INPUTS = []
_WEIGHT_DTYPES = {'norm_mix': _jnp.float32, 'norm_ffn': _jnp.float32, 'hgrn_w_in': _jnp.float32, 'hgrn_lb_table': _jnp.float32, 'hgrn_out_norm': _jnp.float32, 'hgrn_w_out': _jnp.float32, 'sc_w_in': _jnp.float32, 'sc_conv': _jnp.float32, 'sc_w_out': _jnp.float32, 'ffn_w_up': _jnp.float32, 'ffn_conv': _jnp.float32, 'ffn_w_down': _jnp.float32, 'final_norm': _jnp.float32}
MOMENT_SCALE = {'norm_mix': 9.920315e-02, 'norm_ffn': 6.817722e-02, 'hgrn_w_in': 5.174459e-02, 'hgrn_lb_table': 3.884170e-03, 'hgrn_out_norm': 7.124582e-02, 'hgrn_w_out': 7.039653e-02, 'sc_w_in': 5.634442e-02, 'sc_conv': 5.710802e-02, 'sc_w_out': 5.648479e-02, 'ffn_w_up': 2.851388e-02, 'ffn_conv': 2.799057e-02, 'ffn_w_down': 4.652178e-02, 'final_norm': 1.599934e+01}


def _to_microbatches(a, axis):
    t = _jnp.moveaxis(a, axis, 0)
    t = t.reshape((N_MICROBATCH, t.shape[0] // N_MICROBATCH) + t.shape[1:])
    return _jnp.moveaxis(t, 1, axis + 1)


def setup_inputs(seed: int = 0) -> dict:
    inp = _fwd_setup_inputs(seed)
    key = _jax.random.fold_in(_jax.random.key(seed), 7919)
    shape, _ = _output_shape()
    out = dict(inp)
    out["loss_target"] = _jax.random.normal(_jax.random.fold_in(key, 0), shape, _jnp.float32)
    for i, name in enumerate(TWIN_WEIGHTS):
        w = inp[name].astype(_jnp.float32)
        if MOMENT_SCALE is None:
            s = _jnp.sqrt(_jnp.mean(_jnp.square(w)) + 1e-30)
        else:
            s = MOMENT_SCALE[name]
        km, kv = _jax.random.split(_jax.random.fold_in(key, i + 1))
        out[name] = w
        out["m_" + name] = s * _jax.random.normal(km, w.shape, _jnp.float32)
        out["v_" + name] = (s * s) * _jax.random.uniform(kv, w.shape, _jnp.float32, 0.5, 1.5)
    if N_MICROBATCH > 1:
        for name, axis in PER_EXAMPLE_BATCH_AXIS.items():
            out[name] = _to_microbatches(out[name], axis)
    return {'x': out['x'], 'norm_mix': out['norm_mix'], 'norm_ffn': out['norm_ffn'], 'hgrn_w_in': out['hgrn_w_in'], 'hgrn_lb_table': out['hgrn_lb_table'], 'hgrn_out_norm': out['hgrn_out_norm'], 'hgrn_w_out': out['hgrn_w_out'], 'sc_w_in': out['sc_w_in'], 'sc_conv': out['sc_conv'], 'sc_w_out': out['sc_w_out'], 'ffn_w_up': out['ffn_w_up'], 'ffn_conv': out['ffn_conv'], 'ffn_w_down': out['ffn_w_down'], 'final_norm': out['final_norm'], 'loss_target': out['loss_target'], 'm_norm_mix': out['m_norm_mix'], 'm_norm_ffn': out['m_norm_ffn'], 'm_hgrn_w_in': out['m_hgrn_w_in'], 'm_hgrn_lb_table': out['m_hgrn_lb_table'], 'm_hgrn_out_norm': out['m_hgrn_out_norm'], 'm_hgrn_w_out': out['m_hgrn_w_out'], 'm_sc_w_in': out['m_sc_w_in'], 'm_sc_conv': out['m_sc_conv'], 'm_sc_w_out': out['m_sc_w_out'], 'm_ffn_w_up': out['m_ffn_w_up'], 'm_ffn_conv': out['m_ffn_conv'], 'm_ffn_w_down': out['m_ffn_w_down'], 'm_final_norm': out['m_final_norm'], 'v_norm_mix': out['v_norm_mix'], 'v_norm_ffn': out['v_norm_ffn'], 'v_hgrn_w_in': out['v_hgrn_w_in'], 'v_hgrn_lb_table': out['v_hgrn_lb_table'], 'v_hgrn_out_norm': out['v_hgrn_out_norm'], 'v_hgrn_w_out': out['v_hgrn_w_out'], 'v_sc_w_in': out['v_sc_w_in'], 'v_sc_conv': out['v_sc_conv'], 'v_sc_w_out': out['v_sc_w_out'], 'v_ffn_w_up': out['v_ffn_w_up'], 'v_ffn_conv': out['v_ffn_conv'], 'v_ffn_w_down': out['v_ffn_w_down'], 'v_final_norm': out['v_final_norm']}


def _loss(weights, diff, rest, loss_target):
    with _jax.named_scope("forward"):
        args = {**rest, TWIN_DIFF_INPUT: diff, **{k: w.astype(_WEIGHT_DTYPES[k]) for k, w in weights.items()}}
        y = _forward(args)
    with _jax.named_scope("loss_head"):
        err = _jnp.square(y.astype(_jnp.float32) - loss_target)
        return 0.5 * _jnp.sum(_jnp.mean(err, axis=-1)) if err.ndim else 0.5 * err


def _adamw(w, g, m, v):
    m = ADAM_B1 * m + (1.0 - ADAM_B1) * g
    v = ADAM_B2 * v + (1.0 - ADAM_B2) * _jnp.square(g)
    m_hat = m / (1.0 - ADAM_B1 ** ADAM_STEP)
    v_hat = v / (1.0 - ADAM_B2 ** ADAM_STEP)
    delta = -ADAM_LR * (m_hat / (_jnp.sqrt(v_hat) + ADAM_EPS) + ADAM_WD * w)
    return delta, m, v


def reference(x, norm_mix, norm_ffn, hgrn_w_in, hgrn_lb_table, hgrn_out_norm, hgrn_w_out, sc_w_in, sc_conv, sc_w_out, ffn_w_up, ffn_conv, ffn_w_down, final_norm, loss_target, m_norm_mix, m_norm_ffn, m_hgrn_w_in, m_hgrn_lb_table, m_hgrn_out_norm, m_hgrn_w_out, m_sc_w_in, m_sc_conv, m_sc_w_out, m_ffn_w_up, m_ffn_conv, m_ffn_w_down, m_final_norm, v_norm_mix, v_norm_ffn, v_hgrn_w_in, v_hgrn_lb_table, v_hgrn_out_norm, v_hgrn_w_out, v_sc_w_in, v_sc_conv, v_sc_w_out, v_ffn_w_up, v_ffn_conv, v_ffn_w_down, v_final_norm):
    given = dict(x=x, norm_mix=norm_mix, norm_ffn=norm_ffn, hgrn_w_in=hgrn_w_in, hgrn_lb_table=hgrn_lb_table, hgrn_out_norm=hgrn_out_norm, hgrn_w_out=hgrn_w_out, sc_w_in=sc_w_in, sc_conv=sc_conv, sc_w_out=sc_w_out, ffn_w_up=ffn_w_up, ffn_conv=ffn_conv, ffn_w_down=ffn_w_down, final_norm=final_norm, loss_target=loss_target, m_norm_mix=m_norm_mix, m_norm_ffn=m_norm_ffn, m_hgrn_w_in=m_hgrn_w_in, m_hgrn_lb_table=m_hgrn_lb_table, m_hgrn_out_norm=m_hgrn_out_norm, m_hgrn_w_out=m_hgrn_w_out, m_sc_w_in=m_sc_w_in, m_sc_conv=m_sc_conv, m_sc_w_out=m_sc_w_out, m_ffn_w_up=m_ffn_w_up, m_ffn_conv=m_ffn_conv, m_ffn_w_down=m_ffn_w_down, m_final_norm=m_final_norm, v_norm_mix=v_norm_mix, v_norm_ffn=v_norm_ffn, v_hgrn_w_in=v_hgrn_w_in, v_hgrn_lb_table=v_hgrn_lb_table, v_hgrn_out_norm=v_hgrn_out_norm, v_hgrn_w_out=v_hgrn_w_out, v_sc_w_in=v_sc_w_in, v_sc_conv=v_sc_conv, v_sc_w_out=v_sc_w_out, v_ffn_w_up=v_ffn_w_up, v_ffn_conv=v_ffn_conv, v_ffn_w_down=v_ffn_w_down, v_final_norm=v_final_norm)
    weights = {n: given[n] for n in TWIN_WEIGHTS}
    shared = {n: given[n] for n in SHARED_INPUTS}
    per_example = {n: given[n] for n in ['x']}
    grad_fn = _jax.value_and_grad(_loss, argnums=(0, 1))

    def one_microbatch(ex, loss_target):
        ex = dict(ex)
        diff = ex.pop(TWIN_DIFF_INPUT)
        return grad_fn(weights, diff, {**shared, **ex}, loss_target)

    if N_MICROBATCH == 1:
        loss, (grad_w, grad_x) = one_microbatch(per_example, given["loss_target"])
    else:
        def body(carry, xs):
            loss_sum, grad_sum = carry
            l_k, (gw_k, gx_k) = one_microbatch(xs[0], xs[1])
            with _jax.named_scope("update"):
                return (loss_sum + l_k, _jax.tree.map(_jnp.add, grad_sum, gw_k)), gx_k

        init = (_jnp.zeros((), _jnp.float32), _jax.tree.map(_jnp.zeros_like, weights))
        (loss, grad_w), grad_x = _jax.lax.scan(body, init, (per_example, given["loss_target"]))
    with _jax.named_scope("update"):
        delta_w, new_m, new_v = {}, {}, {}
        for n in TWIN_WEIGHTS:
            delta_w[n], new_m[n], new_v[n] = _adamw(weights[n], grad_w[n], given["m_" + n], given["v_" + n])
    return (loss, grad_x, *[grad_w[n] for n in TWIN_WEIGHTS], *[delta_w[n] for n in TWIN_WEIGHTS],
            *[new_m[n] for n in TWIN_WEIGHTS], *[new_v[n] for n in TWIN_WEIGHTS])
```

```python
import functools

import jax
import jax.numpy as jnp
from jax import lax
from jax.experimental import pallas as pl
from jax.experimental.pallas import tpu as pltpu

F32 = jnp.float32
BF16 = jnp.bfloat16
MESH = pl.DeviceIdType.MESH

EPS = 1e-6
CHUNK = 64
HEAD = 128
N_CHIPS = 4
ADAM_LR, ADAM_B1, ADAM_B2, ADAM_EPS, ADAM_WD, ADAM_STEP = 0.001, 0.9, 0.999, 1e-08, 0.01, 10
VMEM_LIMIT = 56 * 1024 * 1024
SUBLANES = 8
LANES = 128


def _pcall(body, **kw):
    return pl.pallas_call(body, **kw)


def _params(sem, vmem=VMEM_LIMIT):
    return pltpu.CompilerParams(dimension_semantics=sem, vmem_limit_bytes=vmem)


def _pick(dim, prefs):
    for p in prefs:
        if p <= dim and dim % p == 0:
            return p
    return dim


def _sigmoid(x):
    return 1.0 / (1.0 + jnp.exp(-x))


def _wmap_col(cw, tn, r0):
    bps = cw // tn
    return lambda kb, nb: (nb // bps, r0 + kb, nb % bps)


def _wmap_row(kp, tk, r0):
    bps = kp // tk
    return lambda kb, nb: (kb // bps, r0 + kb % bps, nb)


def _seg_spec(rows, cols, nseg_blocks, s, row_of, col_of):
    def imap(*idx):
        c = jnp.clip(col_of(*idx) - s * nseg_blocks, 0, nseg_blocks - 1)
        return (row_of(*idx), c)
    return pl.BlockSpec((rows, cols), imap)


def _mm_nn(a, w3, wmap, K, N, tm, tk, tn, name, res=None, out_dtype=F32):
    M = a.shape[0]
    grid = (M // tm, N // tn, K // tk)
    nk = grid[2]

    def body(*refs):
        if res is None:
            a_ref, w_ref, o_ref, acc = refs
        else:
            a_ref, w_ref, r_ref, o_ref, acc = refs
        k = pl.program_id(2)

        @pl.when(k == 0)
        def _():
            acc[...] = jnp.zeros_like(acc)

        acc[...] += jnp.dot(a_ref[...], w_ref[...], preferred_element_type=F32)

        @pl.when(k == nk - 1)
        def _():
            r = acc[...]
            if res is not None:
                r = r + r_ref[...]
            o_ref[...] = r.astype(o_ref.dtype)

    in_specs = [pl.BlockSpec((tm, tk), lambda i, j, k: (i, k)),
                pl.BlockSpec((None, tk, tn), lambda i, j, k: wmap(k, j))]
    args = [a, w3]
    if res is not None:
        in_specs.append(pl.BlockSpec((tm, tn), lambda i, j, k: (i, j)))
        args.append(res)
    return _pcall(
        body, name=name, grid=grid, in_specs=in_specs,
        out_specs=pl.BlockSpec((tm, tn), lambda i, j, k: (i, j)),
        out_shape=jax.ShapeDtypeStruct((M, N), out_dtype),
        scratch_shapes=[pltpu.VMEM((tm, tn), F32)],
        compiler_params=_params(("parallel", "parallel", "arbitrary")),
    )(*args)


def _mm_nt(dys, w3, wmap, K, N, tm, tk, tn, name):
    M = dys[0].shape[0]
    nseg = len(dys)
    bps = (N // nseg) // tn
    grid = (M // tm, K // tk, N // tn)
    nn = grid[2]

    def body(*refs):
        dy_refs = refs[:nseg]
        w_ref, o_ref, acc = refs[nseg:]
        n = pl.program_id(2)

        @pl.when(n == 0)
        def _():
            acc[...] = jnp.zeros_like(acc)

        for s in range(nseg):
            @pl.when(n // bps == s)
            def _(s=s):
                acc[...] += lax.dot_general(dy_refs[s][...], w_ref[...], (((1,), (1,)), ((), ())),
                                            preferred_element_type=F32)

        @pl.when(n == nn - 1)
        def _():
            o_ref[...] = acc[...]

    in_specs = [_seg_spec(tm, tn, bps, s, lambda i, j, n: i, lambda i, j, n: n) for s in range(nseg)]
    in_specs.append(pl.BlockSpec((None, tk, tn), lambda i, j, n: wmap(j, n)))
    return _pcall(
        body, name=name, grid=grid, in_specs=in_specs,
        out_specs=pl.BlockSpec((tm, tk), lambda i, j, n: (i, j)),
        out_shape=jax.ShapeDtypeStruct((M, K), F32),
        scratch_shapes=[pltpu.VMEM((tm, tk), F32)],
        compiler_params=_params(("parallel", "parallel", "arbitrary")),
    )(*dys, w3)


def _mm_tn(x, dys, w3_shape, wmap, K, N, tm, tk, tn, name):
    M = x.shape[0]
    nseg = len(dys)
    bps = (N // nseg) // tn
    grid = (K // tk, N // tn, M // tm)
    nm = grid[2]

    def body(*refs):
        x_ref = refs[0]
        dy_refs = refs[1:1 + nseg]
        o_ref, acc = refs[1 + nseg:]
        j = pl.program_id(1)
        m = pl.program_id(2)

        @pl.when(m == 0)
        def _():
            acc[...] = jnp.zeros_like(acc)

        for s in range(nseg):
            @pl.when(j // bps == s)
            def _(s=s):
                acc[...] += lax.dot_general(x_ref[...], dy_refs[s][...], (((0,), (0,)), ((), ())),
                                            preferred_element_type=F32)

        @pl.when(m == nm - 1)
        def _():
            o_ref[...] = acc[...].astype(o_ref.dtype)

    in_specs = [pl.BlockSpec((tm, tk), lambda i, j, m: (m, i))]
    in_specs += [_seg_spec(tm, tn, bps, s, lambda i, j, m: m, lambda i, j, m: j) for s in range(nseg)]
    return _pcall(
        body, name=name, grid=grid, in_specs=in_specs,
        out_specs=pl.BlockSpec((None, tk, tn), lambda i, j, m: wmap(i, j)),
        out_shape=jax.ShapeDtypeStruct(w3_shape, BF16),
        scratch_shapes=[pltpu.VMEM((tk, tn), F32)],
        compiler_params=_params(("parallel", "parallel", "arbitrary")),
    )(x, *dys)


def _row_call(fn, rows, vecs, outs, n_acc, name, t_rows=256, sub=16):
    T = rows[0][0].shape[0]
    t_rows = min(t_rows, T)
    nsub = t_rows // sub
    n_r, n_v, n_o = len(rows), len(vecs), len(outs)
    width = rows[0][2]

    def body(*refs):
        r_refs = refs[:n_r]
        v_refs = refs[n_r:n_r + n_v]
        o_refs = refs[n_r + n_v:n_r + n_v + n_o]
        a_refs = refs[n_r + n_v + n_o:]

        @pl.when(pl.program_id(0) == 0)
        def _():
            for a in a_refs:
                a[...] = jnp.zeros_like(a)

        vv = [v[...] for v in v_refs]

        def step(i, carry):
            sl = pl.ds(pl.multiple_of(i * sub, sub), sub)
            o_vals, a_vals = fn([r[sl, :] for r in r_refs], vv)
            for o, val in zip(o_refs, o_vals):
                o[sl, :] = val.astype(o.dtype)
            for a, val in zip(a_refs, a_vals):
                a[...] += val.reshape(sub // SUBLANES, SUBLANES, val.shape[-1]).sum(axis=0)
            return carry

        lax.fori_loop(0, nsub, step, 0)

    in_specs = [pl.BlockSpec((t_rows, w), functools.partial(lambda i, cb: (i, cb), cb=cb)) for _, cb, w in rows]
    in_specs += [pl.BlockSpec(v.shape, lambda i: (0, 0)) for v in vecs]
    out_specs = [pl.BlockSpec((t_rows, w), lambda i: (i, 0)) for w, _ in outs]
    out_specs += [pl.BlockSpec((SUBLANES, width), lambda i: (0, 0)) for _ in range(n_acc)]
    out_shape = [jax.ShapeDtypeStruct((T, w), dt) for w, dt in outs]
    out_shape += [jax.ShapeDtypeStruct((SUBLANES, width), F32) for _ in range(n_acc)]
    return _pcall(
        body, name=name, grid=(T // t_rows,), in_specs=in_specs, out_specs=out_specs, out_shape=out_shape,
        compiler_params=_params(("arbitrary",)),
    )(*[r[0] for r in rows], *vecs)


def _rms_fwd_fn(rv, vv):
    h, = rv
    w, = vv
    r = lax.rsqrt(jnp.mean(h * h, axis=-1, keepdims=True) + EPS)
    return [h * r * w], []


def _rms_bwd_fn(rv, vv):
    h, dxn, dh_in = rv
    w, = vv
    d = h.shape[-1]
    r = lax.rsqrt(jnp.mean(h * h, axis=-1, keepdims=True) + EPS)
    gy = dxn * w
    dh = r * gy - h * ((r * r * r) * (1.0 / d) * jnp.sum(gy * h, axis=-1, keepdims=True))
    return [dh_in + dh], [dxn * h * r]


def _final_fn(rv, vv):
    h, tgt = rv
    w, = vv
    d = h.shape[-1]
    r = lax.rsqrt(jnp.mean(h * h, axis=-1, keepdims=True) + EPS)
    hn = h * r
    e = hn * w - tgt
    dy = e * (1.0 / d)
    gy = dy * w
    dh = r * gy - h * ((r * r * r) * (1.0 / d) * jnp.sum(gy * h, axis=-1, keepdims=True))
    return [dh], [e * e, dy * hn]


def _onorm_fwd_fn(rv, vv):
    o, g = rv
    gain, = vv
    r = lax.rsqrt(jnp.mean(o * o, axis=-1, keepdims=True) + EPS)
    return [o * r * gain * (g * _sigmoid(g))], []


def _onorm_bwd_fn(rv, vv):
    o, g, don = rv
    gain, = vv
    d = o.shape[-1]
    r = lax.rsqrt(jnp.mean(o * o, axis=-1, keepdims=True) + EPS)
    sg = _sigmoid(g)
    sl = g * sg
    n = o * r
    dg = don * n * gain * (sg * (1.0 + g * (1.0 - sg)))
    gy = don * sl * gain
    do = r * gy - o * ((r * r * r) * (1.0 / d) * jnp.sum(gy * o, axis=-1, keepdims=True))
    return [do, dg], [don * sl * n]


HALO = SUBLANES


def _col_call(fn, cols, vecs, outs, n_acc, name, before, after, tc=LANES, chunk=128):
    T = cols[0][0].shape[0]
    chunk = min(chunk, T)
    nch = T // chunk
    ncol = outs[0][0] // tc
    n_c, n_v, n_o = len(cols), len(vecs), len(outs)
    hb = HALO if before else 0
    rw = chunk + hb + (HALO if after else 0)

    def body(*refs):
        c_refs = refs[:n_c]
        v_refs = refs[n_c:n_c + n_v]
        o_refs = refs[n_c + n_v:n_c + n_v + n_o]
        a_refs = refs[n_c + n_v + n_o:]
        vv = [v[...] for v in v_refs]
        wrow = lax.broadcasted_iota(jnp.int32, (rw, tc), 0)
        inside = (wrow >= hb) & (wrow < hb + chunk)

        def step(i, carry):
            r0 = pl.multiple_of(i * chunk, chunk)
            wins = []
            for ref in c_refs:
                parts = []
                if before:
                    pb = ref[pl.ds(pl.multiple_of(jnp.maximum(r0 - HALO, 0), HALO), HALO), :]
                    parts.append(jnp.where(i > 0, pb, 0.0))
                parts.append(ref[pl.ds(r0, chunk), :])
                if after:
                    pa = ref[pl.ds(pl.multiple_of(jnp.minimum(r0 + chunk, T - HALO), HALO), HALO), :]
                    parts.append(jnp.where(i < nch - 1, pa, 0.0))
                wins.append(jnp.concatenate(parts, axis=0) if len(parts) > 1 else parts[0])
            o_vals, a_vals = fn(wins, vv, inside)
            for o, val in zip(o_refs, o_vals):
                o[pl.ds(r0, chunk), :] = val[hb:hb + chunk].astype(o.dtype)
            return tuple(c + a for c, a in zip(carry, a_vals))

        taps = [v.shape[0] for v, _ in vecs][:n_acc]
        init = tuple(jnp.zeros((1, tc), F32) for k in taps for _ in range(k))
        sums = lax.fori_loop(0, nch, step, init)
        arow = lax.broadcasted_iota(jnp.int32, (SUBLANES, tc), 0)
        p = 0
        for a, k in zip(a_refs, taps):
            acc = jnp.zeros((SUBLANES, tc), F32)
            for t in range(k):
                acc = jnp.where(arow == t, sums[p], acc)
                p += 1
            a[...] = acc

    in_specs = [pl.BlockSpec((T, tc), functools.partial(lambda j, off: (0, off + j), off=off)) for _, off in cols]
    in_specs += [pl.BlockSpec((v.shape[0], tc), functools.partial(lambda j, off: (0, off + j), off=off))
                 for v, off in vecs]
    out_specs = [pl.BlockSpec((T, tc), lambda j: (0, j)) for _ in outs]
    out_specs += [pl.BlockSpec((SUBLANES, tc), lambda j: (0, j)) for _ in range(n_acc)]
    out_shape = [jax.ShapeDtypeStruct((T, w), dt) for w, dt in outs]
    out_shape += [jax.ShapeDtypeStruct((SUBLANES, ncol * tc), F32) for _ in range(n_acc)]
    return _pcall(
        body, name=name, grid=(ncol,), in_specs=in_specs, out_specs=out_specs, out_shape=out_shape,
        compiler_params=_params(("parallel",)),
    )(*[c[0] for c in cols], *[v[0] for v in vecs])


def _down(x, k):
    return x if k == 0 else pltpu.roll(x, k, 0)


def _up(x, k):
    return x if k == 0 else pltpu.roll(x, x.shape[0] - k, 0)


def _conv(x, w):
    return w[0:1] * _down(x, 2) + w[1:2] * _down(x, 1) + w[2:3] * x


def _conv_t(d, w):
    return w[2:3] * d + w[1:2] * _up(d, 1) + w[0:1] * _up(d, 2)


def _tap_sums(d, x, inside):
    dm = jnp.where(inside, d, 0.0)
    return [jnp.sum(dm * _down(x, 2 - k), axis=0, keepdims=True) for k in range(3)]


def _glu_fwd_fn(wins, vv, inside):
    xg, xv = wins
    wg, wv = vv
    ug = _conv(xg, wg)
    uv = _conv(xv, wv)
    return [ug * _sigmoid(ug) * uv], []


def _glu_bwd_fn(wins, vv, inside):
    xg, xv, da = wins
    wg, wv = vv
    ug = _conv(xg, wg)
    uv = _conv(xv, wv)
    sg = _sigmoid(ug)
    dug = da * uv * (sg * (1.0 + ug * (1.0 - sg)))
    duv = da * (ug * sg)
    return [_conv_t(dug, wg), _conv_t(duv, wv)], _tap_sums(dug, xg, inside) + _tap_sums(duv, xv, inside)


def _sc_fwd_fn(wins, vv, inside):
    gb, gc, hh = wins
    w, = vv
    return [gb * _conv(gc * hh, w)], []


def _sc_bwd_fn(wins, vv, inside):
    gb, gc, hh, dy = wins
    w, = vv
    z = gc * hh
    dcv = dy * gb
    dz = _conv_t(dcv, w)
    return [dy * _conv(z, w), dz * hh, dz * gc], _tap_sums(dcv, z, inside)


def _gates(qr, fr, lb):
    sg = _sigmoid(fr)
    f = lb + (1.0 - lb) * sg
    sq = _sigmoid(qr)
    q = qr * sq * (HEAD ** -0.5)
    return q, 1.0 - f, jnp.log(f), f, sg, sq


def _chunk_decays(gl, row):
    c = gl.shape[0]

    def seg_prefix(x, g):
        r = row & (g - 1)
        d = 1
        while d < g:
            x = x + jnp.where(r >= d, pltpu.roll(x, d, 0), 0.0)
            d *= 2
        return x

    def seg_suffix(x, g):
        r = row & (g - 1)
        d = 1
        while d < g:
            x = x + jnp.where(r < g - d, pltpu.roll(x, c - d, 0), 0.0)
            d *= 2
        return x

    eq, ek = [], []
    g = c // 2
    while g >= 1:
        right = (row & g) != 0
        eq.append(jnp.where(right, jnp.exp(seg_prefix(gl, g)), 0.0))
        ek.append(jnp.where(right, 0.0, jnp.exp(seg_suffix(gl, g) - gl)))
        g //= 2
    return seg_prefix(gl, c), eq, ek


def _intra(q, k, eq, ek, tt, ss):
    c = q.shape[0]
    qs, ks = [], []
    a = jnp.where(tt == ss, jnp.sum(q * k, axis=1, keepdims=True), 0.0)
    g = c // 2
    for e_q, e_k in zip(eq, ek):
        qg = (q * e_q).astype(BF16)
        kg = (k * e_k).astype(BF16)
        p = lax.dot_general(qg, kg, (((1,), (1,)), ((), ())), preferred_element_type=F32)
        a = a + (p if 2 * g >= c else jnp.where((tt ^ ss) < 2 * g, p, 0.0))
        qs.append(qg)
        ks.append(kg)
        g //= 2
    return a, qs, ks


def _hgrn_fwd(proj, lb, d_model):
    T = proj.shape[0]
    H = d_model // HEAD
    nch = T // CHUNK

    def body(q_ref, f_ref, v_ref, lb_ref, o_ref, s_ref):
        lbv = lb_ref[...]
        row = lax.broadcasted_iota(jnp.int32, (CHUNK, HEAD), 0)
        tt = lax.broadcasted_iota(jnp.int32, (CHUNK, CHUNK), 0)
        ss = lax.broadcasted_iota(jnp.int32, (CHUNK, CHUNK), 1)

        def step(i, st):
            sl = pl.ds(pl.multiple_of(i * CHUNK, CHUNK), CHUNK)
            q, k, gl, _, _, _ = _gates(q_ref[sl, :], f_ref[sl, :], lbv)
            v = v_ref[sl, :].astype(BF16)
            b, eq, ek = _chunk_decays(gl, row)
            a, _, _ = _intra(q, k, eq, ek, tt, ss)
            bl = b[CHUNK - 1:CHUNK, :]
            q0 = (q * jnp.exp(b)).astype(BF16)
            kh = (k * jnp.exp(bl - b)).astype(BF16)
            s_ref[i] = st
            o = jnp.dot(a.astype(BF16), v, preferred_element_type=F32)
            o = o + lax.dot_general(q0, st.astype(BF16), (((1,), (1,)), ((), ())), preferred_element_type=F32)
            o_ref[sl, :] = o
            return jnp.exp(bl) * st + lax.dot_general(v, kh, (((0,), (0,)), ((), ())), preferred_element_type=F32)

        lax.fori_loop(0, nch, step, jnp.zeros((HEAD, HEAD), F32))

    col = lambda off: pl.BlockSpec((T, HEAD), functools.partial(lambda h, off: (0, off + h), off=off))
    return _pcall(
        body, name="hgrn_fwd", grid=(H,),
        in_specs=[col(0), col(H), col(2 * H), pl.BlockSpec((1, HEAD), lambda h: (0, h))],
        out_specs=[pl.BlockSpec((T, HEAD), lambda h: (0, h)),
                   pl.BlockSpec((None, nch, HEAD, HEAD), lambda h: (h, 0, 0, 0))],
        out_shape=[jax.ShapeDtypeStruct((T, d_model), F32), jax.ShapeDtypeStruct((H, nch, HEAD, HEAD), F32)],
        compiler_params=_params(("parallel",)),
    )(proj, proj, proj, lb)


def _hgrn_bwd(proj, lb, states, do, d_model):
    T = proj.shape[0]
    H = d_model // HEAD
    nch = T // CHUNK

    def body(q_ref, f_ref, v_ref, lb_ref, s_ref, do_ref, dq_ref, df_ref, dv_ref, dlb_ref):
        lbv = lb_ref[...]
        row = lax.broadcasted_iota(jnp.int32, (CHUNK, HEAD), 0)
        tt = lax.broadcasted_iota(jnp.int32, (CHUNK, CHUNK), 0)
        ss = lax.broadcasted_iota(jnp.int32, (CHUNK, CHUNK), 1)
        last = row == CHUNK - 1
        nt = (((1,), (1,)), ((), ()))
        tn = (((0,), (0,)), ((), ()))

        def step(j, carry):
            dst, dlb = carry
            i = nch - 1 - j
            sl = pl.ds(pl.multiple_of(i * CHUNK, CHUNK), CHUNK)
            qr = q_ref[sl, :]
            q, k, gl, f, sg, sq = _gates(qr, f_ref[sl, :], lbv)
            v = v_ref[sl, :].astype(BF16)
            d_o = do_ref[sl, :].astype(BF16)
            st = s_ref[i]
            st16 = st.astype(BF16)
            dst16 = dst.astype(BF16)
            b, eq, ek = _chunk_decays(gl, row)
            a, qs, ks = _intra(q, k, eq, ek, tt, ss)
            bl = b[CHUNK - 1:CHUNK, :]
            e0 = jnp.exp(b)
            eh = jnp.exp(bl - b)
            ebl = jnp.exp(bl)
            q0 = q * e0
            kh = k * eh
            q016 = q0.astype(BF16)
            kh16 = kh.astype(BF16)
            dv = lax.dot_general(a.astype(BF16), d_o, tn, preferred_element_type=F32)
            dv = dv + lax.dot_general(kh16, dst16, nt, preferred_element_type=F32)
            dv_ref[sl, :] = dv.astype(dv_ref.dtype)
            da = lax.dot_general(d_o, v, nt, preferred_element_type=F32)
            da = jnp.where(tt >= ss, da, 0.0)
            dd = jnp.sum(jnp.where(tt == ss, da, 0.0), axis=1, keepdims=True)
            dq0 = jnp.dot(d_o, st16, preferred_element_type=F32)
            dkh = jnp.dot(v, dst16, preferred_element_type=F32)
            dq = dq0 * e0 + dd * k
            dk = dkh * eh + dd * q
            db = dq0 * q016.astype(F32) - dkh * kh16.astype(F32)
            g = CHUNK // 2
            for e_q, e_k, qg, kg in zip(eq, ek, qs, ks):
                dag = (da if 2 * g >= CHUNK else jnp.where((tt ^ ss) < 2 * g, da, 0.0)).astype(BF16)
                dqg = jnp.dot(dag, kg, preferred_element_type=F32)
                dkg = lax.dot_general(dag, qg, tn, preferred_element_type=F32)
                dq = dq + dqg * e_q
                dk = dk + dkg * e_k
                db = db + (dqg * qg.astype(F32) - dkg * kg.astype(F32))
                g //= 2
            dbl = jnp.sum(dkh * kh16.astype(F32), axis=0, keepdims=True) + ebl * jnp.sum(dst * st, axis=0, keepdims=True)
            db = db + jnp.where(last, dbl, 0.0)
            d = 1
            while d < CHUNK:
                db = db + jnp.where(row < CHUNK - d, pltpu.roll(db, CHUNK - d, 0), 0.0)
                d *= 2
            dfg = db / f - dk
            df_ref[sl, :] = (dfg * (1.0 - lbv) * sg * (1.0 - sg)).astype(df_ref.dtype)
            dq_ref[sl, :] = (dq * (HEAD ** -0.5) * (sq * (1.0 + qr * (1.0 - sq)))).astype(dq_ref.dtype)
            dlb = dlb + jnp.sum(dfg * (1.0 - sg), axis=0, keepdims=True)
            dst = ebl * dst + lax.dot_general(d_o, q016, tn, preferred_element_type=F32)
            return dst, dlb

        _, dlb = lax.fori_loop(0, nch, step, (jnp.zeros((HEAD, HEAD), F32), jnp.zeros((1, HEAD), F32)))
        arow = lax.broadcasted_iota(jnp.int32, (SUBLANES, HEAD), 0)
        dlb_ref[...] = jnp.where(arow == 0, dlb, 0.0)

    col = lambda off: pl.BlockSpec((T, HEAD), functools.partial(lambda h, off: (0, off + h), off=off))
    return _pcall(
        body, name="hgrn_bwd", grid=(H,),
        in_specs=[col(0), col(H), col(2 * H), pl.BlockSpec((1, HEAD), lambda h: (0, h)),
                  pl.BlockSpec((None, nch, HEAD, HEAD), lambda h: (h, 0, 0, 0)), col(0)],
        out_specs=[col(0), col(0), col(0), pl.BlockSpec((SUBLANES, HEAD), lambda h: (0, h))],
        out_shape=[jax.ShapeDtypeStruct((T, d_model), BF16)] * 3 + [jax.ShapeDtypeStruct((SUBLANES, d_model), F32)],
        compiler_params=_params(("parallel",)),
    )(proj, proj, proj, lb, states, do)


def _lb_softmax(table):
    n, f = table.shape

    def body(t_ref, p_ref):
        t = t_ref[...]
        e = jnp.exp(t - jnp.max(t, axis=0, keepdims=True))
        p_ref[...] = e / jnp.sum(e, axis=0, keepdims=True)

    padded = jnp.pad(table, ((0, SUBLANES - n), (0, 0)), constant_values=-jnp.inf)
    return _pcall(body, name="lb_softmax", out_shape=jax.ShapeDtypeStruct((SUBLANES, f), F32))(padded)


def _adamw_math(w, g, m, v):
    m = ADAM_B1 * m + (1.0 - ADAM_B1) * g
    v = ADAM_B2 * v + (1.0 - ADAM_B2) * (g * g)
    m_hat = m / (1.0 - ADAM_B1 ** ADAM_STEP)
    v_hat = v / (1.0 - ADAM_B2 ** ADAM_STEP)
    delta = -ADAM_LR * (m_hat / (jnp.sqrt(v_hat) + ADAM_EPS) + ADAM_WD * w)
    return delta, m, v


def _adamw(w, g, m, v, name):
    R, C = w.shape
    tr = _pick(R, (128, 64, 32, 16, 8))

    def body(w_ref, g_ref, m_ref, v_ref, d_ref, nm_ref, nv_ref):
        d, nm, nv = _adamw_math(w_ref[...], g_ref[...], m_ref[...], v_ref[...])
        d_ref[...] = d
        nm_ref[...] = nm
        nv_ref[...] = nv

    spec = pl.BlockSpec((tr, C), lambda i: (i, 0))
    return _pcall(
        body, name=name, grid=(R // tr,), in_specs=[spec] * 4, out_specs=[spec] * 3,
        out_shape=[jax.ShapeDtypeStruct((R, C), F32)] * 3, compiler_params=_params(("parallel",)),
    )(w, g, m, v)


def _lb_table_grad(p8, dlb, n):
    f = p8.shape[1]

    def body(p_ref, d_ref, o_ref):
        p = p_ref[...]
        d = d_ref[...]
        p0 = p[0:1, :]
        first = lax.broadcasted_iota(jnp.int32, p.shape, 0) == 0
        o_ref[...] = p * (jnp.where(first, d, 0.0) - d * p0)

    return _pcall(body, name="lb_table_grad", out_shape=jax.ShapeDtypeStruct((SUBLANES, f), F32))(p8, dlb)[:n]


def _place():
    x, y, c = lax.axis_index("x"), lax.axis_index("y"), lax.axis_index("c")
    chips = [(1 - x, y), (x, 1 - y), (1 - x, 1 - y)]
    return x, y, c, chips


HBM_SPEC = pl.BlockSpec(memory_space=pltpu.HBM)


def _gather_weights(big, small):
    nb, ns = len(big), len(small)
    n = nb + ns

    def body(*refs):
        ins, outs = refs[:n], refs[n:2 * n]
        send_sems, recv_sems, loc_sems = refs[2 * n:]
        x, y, c, chips = _place()
        me = 2 * x + y
        sib = (x, y, 1 - c)
        local = [pltpu.make_async_copy(ins[t], outs[t].at[me], loc_sems.at[t]) for t in range(n)]
        for cp in local:
            cp.start()

        def half(t, h):
            rh = big[t].shape[0] // 2
            return pl.ds(pl.multiple_of(h * rh, rh), rh)

        sends = []
        for t in range(n):
            for j, chip in enumerate(chips):
                k = 6 * t + j
                if t < nb:
                    src, dst = ins[t].at[half(t, c)], outs[t].at[me, half(t, c)]
                else:
                    src, dst = ins[t], outs[t].at[me]
                sends.append(pltpu.make_async_remote_copy(
                    src_ref=src, dst_ref=dst, send_sem=send_sems.at[k], recv_sem=recv_sems.at[k],
                    device_id=(*chip, c), device_id_type=MESH))
        for cp in sends:
            cp.start()
        passed = []
        for t in range(n):
            for j, (cx, cy) in enumerate(chips):
                k = 6 * t + j
                s = 2 * cx + cy
                if t < nb:
                    landed = outs[t].at[s, half(t, c)]
                    pltpu.make_async_remote_copy(
                        src_ref=landed, dst_ref=landed, send_sem=send_sems.at[k], recv_sem=recv_sems.at[k],
                        device_id=sib, device_id_type=MESH).wait_recv()
                    fwd = pltpu.make_async_remote_copy(
                        src_ref=landed, dst_ref=landed, send_sem=send_sems.at[k + 3], recv_sem=recv_sems.at[k + 3],
                        device_id=sib, device_id_type=MESH)
                    fwd.start()
                    passed.append(fwd)
                else:
                    landed = outs[t].at[s]
                    pltpu.make_async_remote_copy(
                        src_ref=landed, dst_ref=landed, send_sem=send_sems.at[k], recv_sem=recv_sems.at[k],
                        device_id=sib, device_id_type=MESH).wait_recv()
        for t in range(nb):
            for j, (cx, cy) in enumerate(chips):
                k = 6 * t + j
                other = outs[t].at[2 * cx + cy, half(t, 1 - c)]
                pltpu.make_async_remote_copy(
                    src_ref=other, dst_ref=other, send_sem=send_sems.at[k + 3], recv_sem=recv_sems.at[k + 3],
                    device_id=sib, device_id_type=MESH).wait_recv()
        for cp in sends + passed:
            cp.wait_send()
        for cp in local:
            cp.wait()

    arrs = list(big) + list(small)
    return _pcall(
        body, name="gather_weights", in_specs=[HBM_SPEC] * n, out_specs=[HBM_SPEC] * n,
        out_shape=[jax.ShapeDtypeStruct((N_CHIPS,) + a.shape, a.dtype) for a in arrs],
        scratch_shapes=[pltpu.SemaphoreType.DMA((6 * n,)), pltpu.SemaphoreType.DMA((6 * n,)),
                        pltpu.SemaphoreType.DMA((n,))],
    )(*arrs)


def _pair_exchange(grads):
    n = len(grads)

    def body(*refs):
        ins, outs = refs[:n], refs[n:2 * n]
        send_sems, recv_sems = refs[2 * n:]
        x, y, c, _ = _place()
        cps = [pltpu.make_async_remote_copy(
            src_ref=ins[t].at[:, 1 - c], dst_ref=outs[t], send_sem=send_sems.at[t], recv_sem=recv_sems.at[t],
            device_id=(x, y, 1 - c), device_id_type=MESH) for t in range(n)]
        for cp in cps:
            cp.start()
        for cp in cps:
            cp.wait()

    return _pcall(
        body, name="grad_pair_exchange", in_specs=[HBM_SPEC] * n, out_specs=[HBM_SPEC] * n,
        out_shape=[jax.ShapeDtypeStruct((g.shape[0],) + g.shape[2:], g.dtype) for g in grads],
        scratch_shapes=[pltpu.SemaphoreType.DMA((n,)), pltpu.SemaphoreType.DMA((n,))],
    )(*grads)


def _chip_exchange(parts):
    n = len(parts)

    def body(*refs):
        ins, outs = refs[:n], refs[n:2 * n]
        send_sems, recv_sems = refs[2 * n:]
        x, y, c, chips = _place()
        cps = []
        for t in range(n):
            for j, (cx, cy) in enumerate(chips):
                cps.append(pltpu.make_async_remote_copy(
                    src_ref=ins[t].at[2 * cx + cy], dst_ref=outs[t].at[j],
                    send_sem=send_sems.at[3 * t + j], recv_sem=recv_sems.at[3 * t + j],
                    device_id=(cx, cy, c), device_id_type=MESH))
        for cp in cps:
            cp.start()
        for cp in cps:
            cp.wait()

    return _pcall(
        body, name="grad_chip_exchange", in_specs=[HBM_SPEC] * n, out_specs=[HBM_SPEC] * n,
        out_shape=[jax.ShapeDtypeStruct((3,) + p.shape[1:], p.dtype) for p in parts],
        scratch_shapes=[pltpu.SemaphoreType.DMA((3 * n,)), pltpu.SemaphoreType.DMA((3 * n,))],
    )(*parts)


def _pair_share(halves):
    n = len(halves)

    def body(*refs):
        ins, outs = refs[:n], refs[n:2 * n]
        send_sems, recv_sems, loc_sems = refs[2 * n:]
        x, y, c, _ = _place()
        local = [pltpu.make_async_copy(ins[t], outs[t].at[c], loc_sems.at[t]) for t in range(n)]
        cps = [pltpu.make_async_remote_copy(
            src_ref=ins[t], dst_ref=outs[t].at[c], send_sem=send_sems.at[t], recv_sem=recv_sems.at[t],
            device_id=(x, y, 1 - c), device_id_type=MESH) for t in range(n)]
        for cp in local + cps:
            cp.start()
        for t in range(n):
            pltpu.make_async_remote_copy(
                src_ref=ins[t], dst_ref=outs[t].at[1 - c], send_sem=send_sems.at[t], recv_sem=recv_sems.at[t],
                device_id=(x, y, 1 - c), device_id_type=MESH).wait_recv()
        for cp in cps:
            cp.wait_send()
        for cp in local:
            cp.wait()

    return _pcall(
        body, name="grad_pair_share", in_specs=[HBM_SPEC] * n, out_specs=[HBM_SPEC] * n,
        out_shape=[jax.ShapeDtypeStruct((2,) + h.shape, h.dtype) for h in halves],
        scratch_shapes=[pltpu.SemaphoreType.DMA((n,)), pltpu.SemaphoreType.DMA((n,)), pltpu.SemaphoreType.DMA((n,))],
    )(*halves)


def _add_pair(grad, recv, c, name):
    s, _, rh, cc = grad.shape
    tr = _pick(rh, (256, 128, 64, 32, 16))

    def body(c_ref, g_ref, r_ref, o_ref):
        o_ref[...] = (g_ref[...].astype(F32) + r_ref[...].astype(F32)).astype(o_ref.dtype)

    return _pcall(
        body, name=name,
        grid_spec=pltpu.PrefetchScalarGridSpec(
            num_scalar_prefetch=1, grid=(s, rh // tr),
            in_specs=[pl.BlockSpec((None, None, tr, cc), lambda a, i, cr: (a, cr[0], i, 0)),
                      pl.BlockSpec((None, tr, cc), lambda a, i, cr: (a, i, 0))],
            out_specs=pl.BlockSpec((None, tr, cc), lambda a, i, cr: (a, i, 0))),
        out_shape=jax.ShapeDtypeStruct((s, rh, cc), BF16),
        compiler_params=_params(("parallel", "parallel")),
    )(c, grad, recv)


def _add_chips(part, recv, me, name):
    _, rh, cc = part.shape
    tr = _pick(rh, (256, 128, 64, 32, 16))

    def body(m_ref, p_ref, r_ref, o_ref):
        o_ref[...] = ((p_ref[...].astype(F32) + r_ref[0].astype(F32)) + r_ref[1].astype(F32)) + r_ref[2].astype(F32)

    return _pcall(
        body, name=name,
        grid_spec=pltpu.PrefetchScalarGridSpec(
            num_scalar_prefetch=1, grid=(rh // tr,),
            in_specs=[pl.BlockSpec((None, tr, cc), lambda i, mr: (mr[0], i, 0)),
                      pl.BlockSpec((3, tr, cc), lambda i, mr: (0, i, 0))],
            out_specs=pl.BlockSpec((tr, cc), lambda i, mr: (i, 0))),
        out_shape=jax.ShapeDtypeStruct((rh, cc), F32),
        compiler_params=_params(("parallel",)),
    )(me, part, recv)


def _all_sum(vec):
    rows = vec.shape[0]

    def body(v_ref, o_ref, buf, send_sems, recv_sems):
        x, y, c, _ = _place()
        me = 4 * x + 2 * y + c
        buf[me] = v_ref[...]
        cps = []
        for r in range(1, 8):
            fx, fy, fc = (r >> 2) & 1, (r >> 1) & 1, r & 1
            peer = (x ^ fx, y ^ fy, c ^ fc)
            cps.append(pltpu.make_async_remote_copy(
                src_ref=v_ref, dst_ref=buf.at[me], send_sem=send_sems.at[r - 1], recv_sem=recv_sems.at[r - 1],
                device_id=peer, device_id_type=MESH))
        for cp in cps:
            cp.start()
        for r in range(1, 8):
            src = me ^ r
            pltpu.make_async_remote_copy(
                src_ref=v_ref, dst_ref=buf.at[src], send_sem=send_sems.at[r - 1], recv_sem=recv_sems.at[r - 1],
                device_id=(x, y, c), device_id_type=MESH).wait_recv()
        for cp in cps:
            cp.wait_send()
        acc = buf[0]
        for d in range(1, 8):
            acc = acc + buf[d]
        o_ref[...] = acc

    return _pcall(
        body, name="all_sum_small",
        in_specs=[pl.BlockSpec(memory_space=pltpu.VMEM)], out_specs=pl.BlockSpec(memory_space=pltpu.VMEM),
        out_shape=jax.ShapeDtypeStruct((rows, LANES), F32),
        scratch_shapes=[pltpu.VMEM((8, rows, LANES), F32), pltpu.SemaphoreType.DMA((7,)), pltpu.SemaphoreType.DMA((7,))],
    )(vec)


def _pack(parts):
    flat = jnp.concatenate([p.reshape(-1) for p in parts])
    tile = SUBLANES * LANES
    pad = (-flat.shape[0]) % tile
    return jnp.pad(flat, (0, pad)).reshape(-1, LANES)


def _unpack(vec, shapes):
    flat = vec.reshape(-1)
    out, p = [], 0
    for s in shapes:
        n = 1
        for d in s:
            n *= d
        out.append(flat[p:p + n].reshape(s))
        p += n
    return out


def _local_step(x, tgt, norm_mix, norm_ffn, lb8, out_norm, final_norm, sc_conv, ffn_conv,
                w_hin, w_hout, w_sin, w_sout, w_up, w_down):
    T, D = x.shape
    F2 = ffn_conv.shape[-1]
    FF = F2 // 2
    tm = _pick(T, (1024, 512, 256, 128))
    t_d = _pick(D // N_CHIPS, (512, 256, 128))
    t_ff = (FF // N_CHIPS) if (FF // N_CHIPS) % LANES == 0 else LANES
    tn_h = _pick(w_hin.shape[2], (1024, 512, 256, 128))
    tn_s = _pick(D // N_CHIPS, (512, 256, 128))
    tn_u = w_up.shape[2] // 2 if (w_up.shape[2] // 2) % LANES == 0 else _pick(w_up.shape[2], (256, 128))
    tk_d = _pick(D, (512, 256, 128))
    lb = lb8[0:1]

    def mix_in(h, w):
        return _row_call(_rms_fwd_fn, [(h, 0, D)], [w], [(D, BF16)], 0, "rms_fwd")[0]

    def ffn_fwd(h, i):
        xn = mix_in(h, norm_ffn[i:i + 1])
        up = _mm_nn(xn, w_up, _wmap_col(w_up.shape[2], tn_u, i * (D // tk_d)), D, F2, tm, tk_d, tn_u, "ffn_up")
        nb = FF // LANES
        a = _col_call(_glu_fwd_fn, [(up, 0), (up, nb)], [(ffn_conv[i], 0), (ffn_conv[i], nb)], [(FF, BF16)], 0,
                      "glu_fwd", before=True, after=False)[0]
        kp = w_down.shape[1] // 2
        h2 = _mm_nn(a, w_down, _wmap_row(kp, t_ff, i * (kp // t_ff)), FF, D, tm, t_ff, _pick(D, (1024, 512, 256, 128)),
                    "ffn_down", res=h)
        return h2, (xn, up, a)

    def ffn_bwd(dh, h, saved, i):
        xn, up, a = saved
        kp = w_down.shape[1] // 2
        dh16 = dh.astype(BF16)
        wm = _wmap_row(kp, t_ff, i * (kp // t_ff))
        tn_o = _pick(D, (1024, 512, 256, 128))
        da = _mm_nt([dh16], w_down, wm, FF, D, tm, t_ff, tn_o, "ffn_down_dx")
        nb = FF // LANES
        dg, dv, cg, cv = _col_call(_glu_bwd_fn, [(up, 0), (up, nb), (da, 0)], [(ffn_conv[i], 0), (ffn_conv[i], nb)],
                                   [(FF, BF16), (FF, BF16)], 2, "glu_bwd", before=True, after=True)
        wmu = _wmap_col(w_up.shape[2], tn_u, i * (D // tk_d))
        dxn = _mm_nt([dg, dv], w_up, wmu, D, F2, tm, tk_d, tn_u, "ffn_up_dx")
        dh2, dnw = _row_call(_rms_bwd_fn, [(h, 0, D), (dxn, 0, D), (dh, 0, D)], [norm_ffn[i:i + 1]], [(D, F32)], 1,
                             "rms_bwd")
        return dh2, dnw, jnp.concatenate([cg[:3], cv[:3]], axis=1), (a, dh16, wm, t_ff, tn_o), (xn, [dg, dv], wmu, tk_d, tn_u)

    h0 = x
    xn0 = mix_in(h0, norm_mix[0:1])
    proj = _mm_nn(xn0, w_hin, _wmap_col(w_hin.shape[2], tn_h, 0), D, 4 * D, tm, tk_d, tn_h, "hgrn_in")
    o, states = _hgrn_fwd(proj, lb, D)
    on = _row_call(_onorm_fwd_fn, [(o, 0, D), (proj, 3, D)], [out_norm], [(D, BF16)], 0, "onorm_fwd")[0]
    tn_o = _pick(D, (1024, 512, 256, 128))
    h1 = _mm_nn(on, w_hout, _wmap_row(w_hout.shape[1], t_d, 0), D, D, tm, t_d, tn_o, "hgrn_out", res=h0)
    h2, ffn0 = ffn_fwd(h1, 0)
    xn1 = mix_in(h2, norm_mix[1:2])
    sproj = _mm_nn(xn1, w_sin, _wmap_col(w_sin.shape[2], tn_s, 0), D, 3 * D, tm, tk_d, tn_s, "sc_in")
    nd = D // LANES
    ysc = _col_call(_sc_fwd_fn, [(sproj, 0), (sproj, nd), (sproj, 2 * nd)], [(sc_conv, 0)], [(D, BF16)], 0,
                    "sc_fwd", before=True, after=False)[0]
    h3 = _mm_nn(ysc, w_sout, _wmap_row(w_sout.shape[1], t_d, 0), D, D, tm, t_d, tn_o, "sc_out", res=h2)
    h4, ffn1 = ffn_fwd(h3, 1)

    dh, esq, dfinal = _row_call(_final_fn, [(h4, 0, D), (tgt, 0, D)], [final_norm], [(D, F32)], 2, "final_loss")
    loss = 0.5 / D * jnp.sum(esq)
    dh, dnf1, dconv1, down1, up1 = ffn_bwd(dh, h3, ffn1, 1)
    dh16 = dh.astype(BF16)
    wm_so = _wmap_row(w_sout.shape[1], t_d, 0)
    dy = _mm_nt([dh16], w_sout, wm_so, D, D, tm, t_d, tn_o, "sc_out_dx")
    dgb, dgc, dhh, dscc = _col_call(_sc_bwd_fn, [(sproj, 0), (sproj, nd), (sproj, 2 * nd), (dy, 0)], [(sc_conv, 0)],
                                    [(D, BF16)] * 3, 1, "sc_bwd", before=True, after=True)
    wm_si = _wmap_col(w_sin.shape[2], tn_s, 0)
    dxn = _mm_nt([dgb, dgc, dhh], w_sin, wm_si, D, 3 * D, tm, tk_d, tn_s, "sc_in_dx")
    dh_s, dnm1 = _row_call(_rms_bwd_fn, [(h2, 0, D), (dxn, 0, D), (dh, 0, D)], [norm_mix[1:2]], [(D, F32)], 1, "rms_bwd")
    g_sout = _mm_tn(ysc, [dh16], w_sout.shape, wm_so, D, D, tm, t_d, tn_o, "sc_out_dw")
    g_sin = _mm_tn(xn1, [dgb, dgc, dhh], w_sin.shape, wm_si, D, 3 * D, tm, tk_d, tn_s, "sc_in_dw")
    dh = dh_s
    dh, dnf0, dconv0, down0, up0 = ffn_bwd(dh, h1, ffn0, 0)
    dh16 = dh.astype(BF16)
    wm_ho = _wmap_row(w_hout.shape[1], t_d, 0)
    don = _mm_nt([dh16], w_hout, wm_ho, D, D, tm, t_d, tn_o, "hgrn_out_dx")
    do, dgate, dgain = _row_call(_onorm_bwd_fn, [(o, 0, D), (proj, 3, D), (don, 0, D)], [out_norm],
                                 [(D, F32), (D, BF16)], 1, "onorm_bwd")
    dq, df, dv, dlb = _hgrn_bwd(proj, lb, states, do, D)
    wm_hi = _wmap_col(w_hin.shape[2], tn_h, 0)
    dxn = _mm_nt([dq, df, dv, dgate], w_hin, wm_hi, D, 4 * D, tm, tk_d, tn_h, "hgrn_in_dx")
    grad_x, dnm0 = _row_call(_rms_bwd_fn, [(h0, 0, D), (dxn, 0, D), (dh, 0, D)], [norm_mix[0:1]], [(D, F32)], 1, "rms_bwd")
    g_hout = _mm_tn(on, [dh16], w_hout.shape, wm_ho, D, D, tm, t_d, tn_o, "hgrn_out_dw")
    g_hin = _mm_tn(xn0, [dq, df, dv, dgate], w_hin.shape, wm_hi, D, 4 * D, tm, tk_d, tn_h, "hgrn_in_dw")

    def ffn_dw(down, up, i):
        a, d16, wm, tk, tn = down
        kp = w_down.shape[1] // 2
        gd = _mm_tn(a, [d16], (N_CHIPS, kp, D), _wmap_row(kp, tk, 0), FF, D, tm, tk, tn, "ffn_down_dw")
        xn, dys, wmu, tku, tnu = up
        gu = _mm_tn(xn, dys, (N_CHIPS, D, w_up.shape[2]), _wmap_col(w_up.shape[2], tnu, 0), D, F2, tm, tku, tnu, "ffn_up_dw")
        return gd, gu

    gd0, gu0 = ffn_dw(down0, up0, 0)
    gd1, gu1 = ffn_dw(down1, up1, 1)
    g_down = jnp.stack([gd0, gd1], axis=1)
    g_up = jnp.stack([gu0, gu1], axis=1)

    small = dict(
        loss=loss,
        norm_mix=jnp.stack([jnp.sum(dnm0, axis=0), jnp.sum(dnm1, axis=0)]),
        norm_ffn=jnp.stack([jnp.sum(dnf0, axis=0), jnp.sum(dnf1, axis=0)]),
        lb=dlb[0:1],
        out_norm=jnp.sum(dgain, axis=0)[None],
        final_norm=jnp.sum(dfinal, axis=0),
        sc_conv=dscc[:3],
        ffn_conv=jnp.stack([dconv0, dconv1]),
    )
    return grad_x, small, (g_hin, g_hout, g_sin, g_sout, g_up, g_down)


def kernel(x, norm_mix, norm_ffn, hgrn_w_in, hgrn_lb_table, hgrn_out_norm, hgrn_w_out, sc_w_in, sc_conv, sc_w_out, ffn_w_up, ffn_conv, ffn_w_down, final_norm, loss_target, m_norm_mix, m_norm_ffn, m_hgrn_w_in, m_hgrn_lb_table, m_hgrn_out_norm, m_hgrn_w_out, m_sc_w_in, m_sc_conv, m_sc_w_out, m_ffn_w_up, m_ffn_conv, m_ffn_w_down, m_final_norm, v_norm_mix, v_norm_ffn, v_hgrn_w_in, v_hgrn_lb_table, v_hgrn_out_norm, v_hgrn_w_out, v_sc_w_in, v_sc_conv, v_sc_w_out, v_ffn_w_up, v_ffn_conv, v_ffn_w_down, v_final_norm):
    D = x.shape[-1]
    xi, yi, ci = lax.axis_index("x"), lax.axis_index("y"), lax.axis_index("c")
    me_chip = (2 * xi + yi).astype(jnp.int32).reshape(1)
    me_core = ci.astype(jnp.int32).reshape(1)

    big_names = ["hgrn_w_in", "hgrn_w_out", "sc_w_in", "sc_w_out", "ffn_w_up", "ffn_w_down"]
    big_w = dict(hgrn_w_in=hgrn_w_in, hgrn_w_out=hgrn_w_out, sc_w_in=sc_w_in, sc_w_out=sc_w_out,
                 ffn_w_up=ffn_w_up, ffn_w_down=ffn_w_down)
    big_m = dict(hgrn_w_in=m_hgrn_w_in, hgrn_w_out=m_hgrn_w_out, sc_w_in=m_sc_w_in, sc_w_out=m_sc_w_out,
                 ffn_w_up=m_ffn_w_up, ffn_w_down=m_ffn_w_down)
    big_v = dict(hgrn_w_in=v_hgrn_w_in, hgrn_w_out=v_hgrn_w_out, sc_w_in=v_sc_w_in, sc_w_out=v_sc_w_out,
                 ffn_w_up=v_ffn_w_up, ffn_w_down=v_ffn_w_down)
    flat2 = lambda a: a.reshape(-1, a.shape[-1])

    shards16 = [flat2(big_w[n]).astype(BF16) for n in big_names]
    conv_shards = [flat2(sc_conv), flat2(ffn_conv)]
    gathered = _gather_weights(shards16, conv_shards)
    w_hin, w_hout, w_sin, w_sout, w_up, w_down = gathered[:6]
    scc = jnp.moveaxis(gathered[6], 0, 1).reshape(3, D)
    f2 = ffn_conv.shape[-1] * N_CHIPS
    fcc = jnp.moveaxis(gathered[7].reshape(N_CHIPS, 2, 3, -1), 0, 2).reshape(2, 3, f2)

    lb8 = _lb_softmax(hgrn_lb_table)
    grad_x, small, bigs = _local_step(
        x[0], loss_target[0], norm_mix, norm_ffn, lb8, hgrn_out_norm, final_norm[None], scc, fcc,
        w_hin, w_hout, w_sin, w_sout, w_up, w_down)

    small_names = ["loss", "norm_mix", "norm_ffn", "lb", "out_norm", "final_norm", "sc_conv", "ffn_conv"]
    parts = [small[n].astype(F32) for n in small_names]
    shapes = [p.shape for p in parts]
    tot = dict(zip(small_names, _unpack(_all_sum(_pack(parts)), shapes)))
    loss = tot["loss"].reshape(())
    g_lb_table = _lb_table_grad(lb8, tot["lb"], hgrn_lb_table.shape[0])
    cw = sc_conv.shape[-1]
    g_sc_conv = lax.dynamic_slice_in_dim(tot["sc_conv"], me_chip[0] * cw, cw, axis=1)[None]
    cf = ffn_conv.shape[-1]
    g_ffn_conv = lax.dynamic_slice_in_dim(tot["ffn_conv"], me_chip[0] * cf, cf, axis=2)
    g_small = dict(norm_mix=tot["norm_mix"], norm_ffn=tot["norm_ffn"], hgrn_lb_table=g_lb_table,
                   hgrn_out_norm=tot["out_norm"], sc_conv=g_sc_conv, ffn_conv=g_ffn_conv, final_norm=tot["final_norm"])
    w_small = dict(norm_mix=norm_mix, norm_ffn=norm_ffn, hgrn_lb_table=hgrn_lb_table, hgrn_out_norm=hgrn_out_norm,
                   sc_conv=sc_conv, ffn_conv=ffn_conv, final_norm=final_norm)
    m_small = dict(norm_mix=m_norm_mix, norm_ffn=m_norm_ffn, hgrn_lb_table=m_hgrn_lb_table, hgrn_out_norm=m_hgrn_out_norm,
                   sc_conv=m_sc_conv, ffn_conv=m_ffn_conv, final_norm=m_final_norm)
    v_small = dict(norm_mix=v_norm_mix, norm_ffn=v_norm_ffn, hgrn_lb_table=v_hgrn_lb_table, hgrn_out_norm=v_hgrn_out_norm,
                   sc_conv=v_sc_conv, ffn_conv=v_ffn_conv, final_norm=v_final_norm)
    sm_names = list(g_small)
    sm_shapes = [w_small[n].shape for n in sm_names]
    d_s, m_s, v_s = _adamw(_pack([w_small[n] for n in sm_names]), _pack([g_small[n] for n in sm_names]),
                           _pack([m_small[n] for n in sm_names]), _pack([v_small[n] for n in sm_names]), "adamw_small")
    out_g, out_d, out_m, out_v = dict(g_small), {}, {}, {}
    for n, d_, m_, v_ in zip(sm_names, _unpack(d_s, sm_shapes), _unpack(m_s, sm_shapes), _unpack(v_s, sm_shapes)):
        out_d[n], out_m[n], out_v[n] = d_, m_, v_

    halves = [g.reshape(N_CHIPS, 2, -1, g.shape[-1]) for g in bigs]
    recv1 = _pair_exchange(halves)
    pair = [_add_pair(g, r, me_core, "grad_add_pair") for g, r in zip(halves, recv1)]
    recv2 = _chip_exchange(pair)
    mine = [_add_chips(p, r, me_chip, "grad_add_chips") for p, r in zip(pair, recv2)]
    full = _pair_share(mine)
    for n, g in zip(big_names, full):
        w = big_w[n]
        g2 = g.reshape(-1, g.shape[-1])
        d_, m_, v_ = _adamw(flat2(w), g2, flat2(big_m[n]), flat2(big_v[n]), "adamw_" + n)
        out_g[n], out_d[n], out_m[n], out_v[n] = (a.reshape(w.shape) for a in (g2, d_, m_, v_))

    order = ["norm_mix", "norm_ffn", "hgrn_w_in", "hgrn_lb_table", "hgrn_out_norm", "hgrn_w_out", "sc_w_in", "sc_conv",
             "sc_w_out", "ffn_w_up", "ffn_conv", "ffn_w_down", "final_norm"]
    return (loss, grad_x[None], *[out_g[n] for n in order], *[out_d[n] for n in order],
            *[out_m[n] for n in order], *[out_v[n] for n in order])
```

```python
import functools

import jax
import jax.numpy as jnp
from jax import lax
from jax.experimental import pallas as pl
from jax.experimental.pallas import tpu as pltpu

F32 = jnp.float32
BF16 = jnp.bfloat16
MESH = pl.DeviceIdType.MESH

EPS = 1e-6
CHUNK = 64
HEAD = 128
N_CHIPS = 4
ADAM_LR, ADAM_B1, ADAM_B2, ADAM_EPS, ADAM_WD, ADAM_STEP = 0.001, 0.9, 0.999, 1e-08, 0.01, 10
VMEM_LIMIT = 56 * 1024 * 1024
SUBLANES = 8
LANES = 128


def _pcall(body, **kw):
    return pl.pallas_call(body, **kw)


def _params(sem, vmem=VMEM_LIMIT):
    return pltpu.CompilerParams(dimension_semantics=sem, vmem_limit_bytes=vmem)


def _pick(dim, prefs):
    for p in prefs:
        if p <= dim and dim % p == 0:
            return p
    return dim


def _sigmoid(x):
    return 1.0 / (1.0 + jnp.exp(-x))


def _wmap_col(cw, tn, r0):
    bps = cw // tn
    return lambda kb, nb: (nb // bps, r0 + kb, nb % bps)


def _wmap_row(kp, tk, r0):
    bps = kp // tk
    return lambda kb, nb: (kb // bps, r0 + kb % bps, nb)


def _mm_nn(a, w3, wmap, K, N, tm, tk, tn, name, res=None):
    M = a.shape[0]
    nk = K // tk

    def body(*refs):
        if res is None:
            a_ref, w_ref, o_ref = refs[:3]
        else:
            a_ref, w_ref, r_ref, o_ref = refs[:4]
        p = jnp.dot(a_ref[...], w_ref[...], preferred_element_type=F32)
        if nk == 1:
            o_ref[...] = p if res is None else p + r_ref[...]
            return
        acc = refs[-1]
        k = pl.program_id(2)

        @pl.when(k == 0)
        def _():
            acc[...] = p

        @pl.when(k > 0)
        def _():
            acc[...] += p

        @pl.when(k == nk - 1)
        def _():
            o_ref[...] = acc[...] if res is None else acc[...] + r_ref[...]

    if nk == 1:
        grid = (M // tm, N // tn)
        ix = lambda f: (lambda i, j: f(i, j, 0))
        sem = ("parallel", "parallel")
        scratch = []
    else:
        grid = (M // tm, N // tn, nk)
        ix = lambda f: f
        sem = ("parallel", "parallel", "arbitrary")
        scratch = [pltpu.VMEM((tm, tn), F32)]
    in_specs = [pl.BlockSpec((tm, tk), ix(lambda i, j, k: (i, k))),
                pl.BlockSpec((None, tk, tn), ix(lambda i, j, k: wmap(k, j)))]
    args = [a, w3]
    if res is not None:
        in_specs.append(pl.BlockSpec((tm, tn), ix(lambda i, j, k: (i, j))))
        args.append(res)
    return _pcall(
        body, name=name, grid=grid, in_specs=in_specs,
        out_specs=pl.BlockSpec((tm, tn), ix(lambda i, j, k: (i, j))),
        out_shape=jax.ShapeDtypeStruct((M, N), F32), scratch_shapes=scratch, compiler_params=_params(sem),
    )(*args)


def _mm_nt(dy3, w3, wmap, K, N, tm, tk, tn, name):
    M = dy3.shape[1]
    bps = dy3.shape[2] // tn
    grid = (M // tm, K // tk, N // tn)
    nn = grid[2]

    def body(dy_ref, w_ref, o_ref):
        p = lax.dot_general(dy_ref[...], w_ref[...], (((1,), (1,)), ((), ())), preferred_element_type=F32)
        if nn == 1:
            o_ref[...] = p
            return
        n = pl.program_id(2)

        @pl.when(n == 0)
        def _():
            o_ref[...] = p

        @pl.when(n > 0)
        def _():
            o_ref[...] += p

    return _pcall(
        body, name=name, grid=grid,
        in_specs=[pl.BlockSpec((None, tm, tn), lambda i, j, n: (n // bps, i, n % bps)),
                  pl.BlockSpec((None, tk, tn), lambda i, j, n: wmap(j, n))],
        out_specs=pl.BlockSpec((tm, tk), lambda i, j, n: (i, j)),
        out_shape=jax.ShapeDtypeStruct((M, K), F32),
        compiler_params=_params(("parallel", "parallel", "arbitrary")),
    )(dy3, w3)


def _mm_tn(x, dy3, shape4, layer, wmap, K, N, tk, tn, name, into=None, tm=None):
    M = x.shape[0]
    tm = M if tm is None else tm
    nm = M // tm
    bps = dy3.shape[2] // tn

    def body(*refs):
        x_ref, dy_ref = refs[:2]
        p = lax.dot_general(x_ref[...], dy_ref[...], (((0,), (0,)), ((), ())), preferred_element_type=F32)
        if nm == 1:
            o_ref = refs[-1]
            o_ref[...] = p.astype(o_ref.dtype)
            return
        o_ref, acc = refs[-2:]
        m = pl.program_id(2)

        @pl.when(m == 0)
        def _():
            acc[...] = p

        @pl.when(m > 0)
        def _():
            acc[...] += p

        @pl.when(m == nm - 1)
        def _():
            o_ref[...] = acc[...].astype(o_ref.dtype)

    def omap(i, j, m):
        s, rb, cb = wmap(i, j)
        return (s, layer, rb, cb)

    in_specs = [pl.BlockSpec((tm, tk), lambda i, j, m: (m, i)),
                pl.BlockSpec((None, tm, tn), lambda i, j, m: (j // bps, m, j % bps))]
    args = [x, dy3]
    alias = {}
    if into is not None:
        in_specs.append(pl.BlockSpec(memory_space=pl.ANY))
        args.append(into)
        alias = {2: 0}
    return _pcall(
        body, name=name, grid=(K // tk, N // tn, nm), in_specs=in_specs,
        out_specs=pl.BlockSpec((None, None, tk, tn), omap),
        out_shape=jax.ShapeDtypeStruct(shape4, BF16), input_output_aliases=alias,
        scratch_shapes=[] if nm == 1 else [pltpu.VMEM((tk, tn), F32)],
        compiler_params=_params(("parallel", "parallel", "arbitrary")),
    )(*args)


def _row_call(fn, rows, vecs, outs, n_acc, name, t_rows=256, sub=16):
    T = rows[0][0].shape[0]
    t_rows = min(t_rows, T)
    nsub = t_rows // sub
    n_r, n_v, n_o = len(rows), len(vecs), len(outs)
    width = rows[0][2]

    def body(*refs):
        r_refs = refs[:n_r]
        v_refs = refs[n_r:n_r + n_v]
        o_refs = refs[n_r + n_v:n_r + n_v + n_o]
        a_refs = refs[n_r + n_v + n_o:]

        @pl.when(pl.program_id(0) == 0)
        def _():
            for a in a_refs:
                a[...] = jnp.zeros_like(a)

        vv = [v[...] for v in v_refs]

        def step(i, carry):
            sl = pl.ds(pl.multiple_of(i * sub, sub), sub)
            o_vals, a_vals = fn([r[sl, :] for r in r_refs], vv)
            for o, val in zip(o_refs, o_vals):
                o[sl, :] = val.astype(o.dtype)
            for a, val in zip(a_refs, a_vals):
                a[...] += val.reshape(sub // SUBLANES, SUBLANES, val.shape[-1]).sum(axis=0)
            return carry

        lax.fori_loop(0, nsub, step, 0)

    in_specs = [pl.BlockSpec((t_rows, w), functools.partial(lambda i, cb: (i, cb), cb=cb)) for _, cb, w in rows]
    in_specs += [pl.BlockSpec(v.shape, lambda i: (0, 0)) for v in vecs]
    out_specs = [pl.BlockSpec((t_rows, w), lambda i: (i, 0)) for w, _ in outs]
    out_specs += [pl.BlockSpec((SUBLANES, width), lambda i: (0, 0)) for _ in range(n_acc)]
    out_shape = [jax.ShapeDtypeStruct((T, w), dt) for w, dt in outs]
    out_shape += [jax.ShapeDtypeStruct((SUBLANES, width), F32) for _ in range(n_acc)]
    return _pcall(
        body, name=name, grid=(T // t_rows,), in_specs=in_specs, out_specs=out_specs, out_shape=out_shape,
        compiler_params=_params(("arbitrary",)),
    )(*[r[0] for r in rows], *vecs)


def _rms_fwd_fn(rv, vv):
    h, = rv
    w, = vv
    r = lax.rsqrt(jnp.mean(h * h, axis=-1, keepdims=True) + EPS)
    return [h * r * w], []


def _rms_bwd_fn(rv, vv):
    h, dxn, dh_in = rv
    w, = vv
    d = h.shape[-1]
    r = lax.rsqrt(jnp.mean(h * h, axis=-1, keepdims=True) + EPS)
    gy = dxn * w
    dh = r * gy - h * ((r * r * r) * (1.0 / d) * jnp.sum(gy * h, axis=-1, keepdims=True))
    return [dh_in + dh] * 2, [dxn * h * r]


def _final_fn(rv, vv):
    h, tgt = rv
    w, = vv
    d = h.shape[-1]
    r = lax.rsqrt(jnp.mean(h * h, axis=-1, keepdims=True) + EPS)
    hn = h * r
    e = hn * w - tgt
    dy = e * (1.0 / d)
    gy = dy * w
    dh = r * gy - h * ((r * r * r) * (1.0 / d) * jnp.sum(gy * h, axis=-1, keepdims=True))
    return [dh] * 2, [e * e, dy * hn]


def _onorm_fwd_fn(rv, vv):
    o, g = rv
    gain, = vv
    r = lax.rsqrt(jnp.mean(o * o, axis=-1, keepdims=True) + EPS)
    return [o * r * gain * (g * _sigmoid(g))], []


def _onorm_bwd_fn(rv, vv):
    o, g, don = rv
    gain, = vv
    d = o.shape[-1]
    r = lax.rsqrt(jnp.mean(o * o, axis=-1, keepdims=True) + EPS)
    sg = _sigmoid(g)
    sl = g * sg
    n = o * r
    dg = don * n * gain * (sg * (1.0 + g * (1.0 - sg)))
    gy = don * sl * gain
    do = r * gy - o * ((r * r * r) * (1.0 / d) * jnp.sum(gy * o, axis=-1, keepdims=True))
    return [do, dg], [don * sl * n]


HALO = SUBLANES


def _col_call(fn, cols, vecs, outs, n_acc, name, before, after, tc=LANES, chunk=128):
    T = cols[0][0].shape[0]
    chunk = min(chunk, T)
    nch = T // chunk
    ncol = outs[0][1] // tc
    n_c, n_v, n_o = len(cols), len(vecs), len(outs)
    hb = HALO if before else 0
    rw = chunk + hb + (HALO if after else 0)

    def body(*refs):
        c_refs = refs[:n_c]
        v_refs = refs[n_c:n_c + n_v]
        o_refs = refs[n_c + n_v:n_c + n_v + n_o]
        a_refs = refs[n_c + n_v + n_o:]
        vv = [v[...] for v in v_refs]
        wrow = lax.broadcasted_iota(jnp.int32, (rw, tc), 0)
        inside = (wrow >= hb) & (wrow < hb + chunk)

        def step(i, carry):
            r0 = pl.multiple_of(i * chunk, chunk)
            wins = []
            for ref in c_refs:
                parts = []
                if before:
                    pb = ref[pl.ds(pl.multiple_of(jnp.maximum(r0 - HALO, 0), HALO), HALO), :]
                    parts.append(jnp.where(i > 0, pb, 0.0))
                parts.append(ref[pl.ds(r0, chunk), :])
                if after:
                    pa = ref[pl.ds(pl.multiple_of(jnp.minimum(r0 + chunk, T - HALO), HALO), HALO), :]
                    parts.append(jnp.where(i < nch - 1, pa, 0.0))
                wins.append(jnp.concatenate(parts, axis=0) if len(parts) > 1 else parts[0])
            o_vals, a_vals = fn(wins, vv, inside)
            p = 0
            for o, (nseg, _, _) in zip(o_refs, outs):
                for s in range(nseg):
                    o[s, pl.ds(r0, chunk), :] = o_vals[p][hb:hb + chunk].astype(o.dtype)
                    p += 1
            return tuple(c + a for c, a in zip(carry, a_vals))

        taps = [v.shape[0] for v, _ in vecs][:n_acc]
        init = tuple(jnp.zeros((1, tc), F32) for k in taps for _ in range(k))
        sums = lax.fori_loop(0, nch, step, init)
        arow = lax.broadcasted_iota(jnp.int32, (SUBLANES, tc), 0)
        p = 0
        for a, k in zip(a_refs, taps):
            acc = jnp.zeros((SUBLANES, tc), F32)
            for t in range(k):
                acc = jnp.where(arow == t, sums[p], acc)
                p += 1
            a[...] = acc

    in_specs = [pl.BlockSpec((T, tc), functools.partial(lambda j, off: (0, off + j), off=off)) for _, off in cols]
    in_specs += [pl.BlockSpec((v.shape[0], tc), functools.partial(lambda j, off: (0, off + j), off=off))
                 for v, off in vecs]
    out_specs = [pl.BlockSpec((nseg, T, tc), lambda j: (0, 0, j)) for nseg, _, _ in outs]
    out_specs += [pl.BlockSpec((SUBLANES, tc), lambda j: (0, j)) for _ in range(n_acc)]
    out_shape = [jax.ShapeDtypeStruct((nseg, T, w), dt) for nseg, w, dt in outs]
    out_shape += [jax.ShapeDtypeStruct((SUBLANES, ncol * tc), F32) for _ in range(n_acc)]
    return _pcall(
        body, name=name, grid=(ncol,), in_specs=in_specs, out_specs=out_specs, out_shape=out_shape,
        compiler_params=_params(("parallel",)),
    )(*[c[0] for c in cols], *[v[0] for v in vecs])


def _down(x, k):
    return x if k == 0 else pltpu.roll(x, k, 0)


def _up(x, k):
    return x if k == 0 else pltpu.roll(x, x.shape[0] - k, 0)


def _conv(x, w):
    return w[0:1] * _down(x, 2) + w[1:2] * _down(x, 1) + w[2:3] * x


def _conv_t(d, w):
    return w[2:3] * d + w[1:2] * _up(d, 1) + w[0:1] * _up(d, 2)


def _tap_sums(d, x, inside):
    dm = jnp.where(inside, d, 0.0)
    return [jnp.sum(dm * _down(x, 2 - k), axis=0, keepdims=True) for k in range(3)]


def _glu_fwd_fn(wins, vv, inside):
    xg, xv = wins
    wg, wv = vv
    ug = _conv(xg, wg)
    uv = _conv(xv, wv)
    return [ug * _sigmoid(ug) * uv], []


def _glu_bwd_fn(wins, vv, inside):
    xg, xv, da = wins
    wg, wv = vv
    ug = _conv(xg, wg)
    uv = _conv(xv, wv)
    sg = _sigmoid(ug)
    dug = da * uv * (sg * (1.0 + ug * (1.0 - sg)))
    duv = da * (ug * sg)
    return [_conv_t(dug, wg), _conv_t(duv, wv)], _tap_sums(dug, xg, inside) + _tap_sums(duv, xv, inside)


def _sc_fwd_fn(wins, vv, inside):
    gb, gc, hh = wins
    w, = vv
    return [gb * _conv(gc * hh, w)], []


def _sc_bwd_fn(wins, vv, inside):
    gb, gc, hh, dy = wins
    w, = vv
    z = gc * hh
    dcv = dy * gb
    dz = _conv_t(dcv, w)
    return [dy * _conv(z, w), dz * hh, dz * gc], _tap_sums(dcv, z, inside)


def _gates(qr, fr, lb):
    sg = _sigmoid(fr)
    f = lb + (1.0 - lb) * sg
    sq = _sigmoid(qr)
    q = qr * sq * (HEAD ** -0.5)
    return q, 1.0 - f, jnp.log(f), f, sg, sq


def _chunk_decays(gl, row):
    c = gl.shape[0]

    def seg_prefix(x, g):
        r = row & (g - 1)
        d = 1
        while d < g:
            x = x + jnp.where(r >= d, pltpu.roll(x, d, 0), 0.0)
            d *= 2
        return x

    def seg_suffix(x, g):
        r = row & (g - 1)
        d = 1
        while d < g:
            x = x + jnp.where(r < g - d, pltpu.roll(x, c - d, 0), 0.0)
            d *= 2
        return x

    eq, ek = [], []
    g = c // 2
    while g >= 1:
        right = (row & g) != 0
        eq.append(jnp.where(right, jnp.exp(seg_prefix(gl, g)), 0.0))
        ek.append(jnp.where(right, 0.0, jnp.exp(seg_suffix(gl, g) - gl)))
        g //= 2
    return seg_prefix(gl, c), eq, ek


def _intra(q, k, eq, ek, tt, ss):
    c = q.shape[0]
    qs, ks = [], []
    a = jnp.where(tt == ss, jnp.sum(q * k, axis=1, keepdims=True), 0.0)
    g = c // 2
    for e_q, e_k in zip(eq, ek):
        qg = (q * e_q).astype(BF16)
        kg = (k * e_k).astype(BF16)
        p = lax.dot_general(qg, kg, (((1,), (1,)), ((), ())), preferred_element_type=F32)
        a = a + (p if 2 * g >= c else jnp.where((tt ^ ss) < 2 * g, p, 0.0))
        qs.append(qg)
        ks.append(kg)
        g //= 2
    return a, qs, ks


def _hgrn_fwd(proj, lb, d_model):
    T = proj.shape[0]
    H = d_model // HEAD
    nch = T // CHUNK

    def body(q_ref, f_ref, v_ref, lb_ref, o_ref, s_ref):
        lbv = lb_ref[...]
        row = lax.broadcasted_iota(jnp.int32, (CHUNK, HEAD), 0)
        tt = lax.broadcasted_iota(jnp.int32, (CHUNK, CHUNK), 0)
        ss = lax.broadcasted_iota(jnp.int32, (CHUNK, CHUNK), 1)

        def step(i, st):
            sl = pl.ds(pl.multiple_of(i * CHUNK, CHUNK), CHUNK)
            q, k, gl, _, _, _ = _gates(q_ref[sl, :], f_ref[sl, :], lbv)
            v = v_ref[sl, :].astype(BF16)
            b, eq, ek = _chunk_decays(gl, row)
            a, _, _ = _intra(q, k, eq, ek, tt, ss)
            bl = b[CHUNK - 1:CHUNK, :]
            q0 = (q * jnp.exp(b)).astype(BF16)
            kh = (k * jnp.exp(bl - b)).astype(BF16)
            s_ref[i] = st
            o = jnp.dot(a.astype(BF16), v, preferred_element_type=F32)
            o = o + lax.dot_general(q0, st.astype(BF16), (((1,), (1,)), ((), ())), preferred_element_type=F32)
            o_ref[sl, :] = o
            return jnp.exp(bl) * st + lax.dot_general(v, kh, (((0,), (0,)), ((), ())), preferred_element_type=F32)

        lax.fori_loop(0, nch, step, jnp.zeros((HEAD, HEAD), F32))

    col = lambda off: pl.BlockSpec((T, HEAD), functools.partial(lambda h, off: (0, off + h), off=off))
    return _pcall(
        body, name="hgrn_fwd", grid=(H,),
        in_specs=[col(0), col(H), col(2 * H), pl.BlockSpec((1, HEAD), lambda h: (0, h))],
        out_specs=[pl.BlockSpec((T, HEAD), lambda h: (0, h)),
                   pl.BlockSpec((None, nch, HEAD, HEAD), lambda h: (h, 0, 0, 0))],
        out_shape=[jax.ShapeDtypeStruct((T, d_model), F32), jax.ShapeDtypeStruct((H, nch, HEAD, HEAD), F32)],
        compiler_params=_params(("parallel",)),
    )(proj, proj, proj, lb)


def _hgrn_bwd(proj, lb, states, do, dgate, d_model):
    T = proj.shape[0]
    H = d_model // HEAD
    nch = T // CHUNK

    def body(q_ref, f_ref, v_ref, lb_ref, s_ref, do_ref, dg_ref, dp_ref, dlb_ref):
        dq_ref, df_ref, dv_ref = dp_ref.at[0], dp_ref.at[1], dp_ref.at[2]
        dp_ref[3] = dg_ref[...]
        lbv = lb_ref[...]
        row = lax.broadcasted_iota(jnp.int32, (CHUNK, HEAD), 0)
        tt = lax.broadcasted_iota(jnp.int32, (CHUNK, CHUNK), 0)
        ss = lax.broadcasted_iota(jnp.int32, (CHUNK, CHUNK), 1)
        last = row == CHUNK - 1
        nt = (((1,), (1,)), ((), ()))
        tn = (((0,), (0,)), ((), ()))

        def step(j, carry):
            dst, dlb = carry
            i = nch - 1 - j
            sl = pl.ds(pl.multiple_of(i * CHUNK, CHUNK), CHUNK)
            qr = q_ref[sl, :]
            q, k, gl, f, sg, sq = _gates(qr, f_ref[sl, :], lbv)
            v = v_ref[sl, :].astype(BF16)
            d_o = do_ref[sl, :].astype(BF16)
            st = s_ref[i]
            st16 = st.astype(BF16)
            dst16 = dst.astype(BF16)
            b, eq, ek = _chunk_decays(gl, row)
            a, qs, ks = _intra(q, k, eq, ek, tt, ss)
            bl = b[CHUNK - 1:CHUNK, :]
            e0 = jnp.exp(b)
            eh = jnp.exp(bl - b)
            ebl = jnp.exp(bl)
            q0 = q * e0
            kh = k * eh
            q016 = q0.astype(BF16)
            kh16 = kh.astype(BF16)
            dv = lax.dot_general(a.astype(BF16), d_o, tn, preferred_element_type=F32)
            dv = dv + lax.dot_general(kh16, dst16, nt, preferred_element_type=F32)
            dv_ref[sl, :] = dv.astype(dv_ref.dtype)
            da = lax.dot_general(d_o, v, nt, preferred_element_type=F32)
            da = jnp.where(tt >= ss, da, 0.0)
            dd = jnp.sum(jnp.where(tt == ss, da, 0.0), axis=1, keepdims=True)
            dq0 = jnp.dot(d_o, st16, preferred_element_type=F32)
            dkh = jnp.dot(v, dst16, preferred_element_type=F32)
            dq = dq0 * e0 + dd * k
            dk = dkh * eh + dd * q
            db = dq0 * q016.astype(F32) - dkh * kh16.astype(F32)
            g = CHUNK // 2
            for e_q, e_k, qg, kg in zip(eq, ek, qs, ks):
                dag = (da if 2 * g >= CHUNK else jnp.where((tt ^ ss) < 2 * g, da, 0.0)).astype(BF16)
                dqg = jnp.dot(dag, kg, preferred_element_type=F32)
                dkg = lax.dot_general(dag, qg, tn, preferred_element_type=F32)
                dq = dq + dqg * e_q
                dk = dk + dkg * e_k
                db = db + (dqg * qg.astype(F32) - dkg * kg.astype(F32))
                g //= 2
            dbl = jnp.sum(dkh * kh16.astype(F32), axis=0, keepdims=True) + ebl * jnp.sum(dst * st, axis=0, keepdims=True)
            db = db + jnp.where(last, dbl, 0.0)
            d = 1
            while d < CHUNK:
                db = db + jnp.where(row < CHUNK - d, pltpu.roll(db, CHUNK - d, 0), 0.0)
                d *= 2
            dfg = db / f - dk
            df_ref[sl, :] = (dfg * (1.0 - lbv) * sg * (1.0 - sg)).astype(df_ref.dtype)
            dq_ref[sl, :] = (dq * (HEAD ** -0.5) * (sq * (1.0 + qr * (1.0 - sq)))).astype(dq_ref.dtype)
            dlb = dlb + jnp.sum(dfg * (1.0 - sg), axis=0, keepdims=True)
            dst = ebl * dst + lax.dot_general(d_o, q016, tn, preferred_element_type=F32)
            return dst, dlb

        _, dlb = lax.fori_loop(0, nch, step, (jnp.zeros((HEAD, HEAD), F32), jnp.zeros((1, HEAD), F32)))
        arow = lax.broadcasted_iota(jnp.int32, (SUBLANES, HEAD), 0)
        dlb_ref[...] = jnp.where(arow == 0, dlb, 0.0)

    col = lambda off: pl.BlockSpec((T, HEAD), functools.partial(lambda h, off: (0, off + h), off=off))
    return _pcall(
        body, name="hgrn_bwd", grid=(H,),
        in_specs=[col(0), col(H), col(2 * H), pl.BlockSpec((1, HEAD), lambda h: (0, h)),
                  pl.BlockSpec((None, nch, HEAD, HEAD), lambda h: (h, 0, 0, 0)), col(0), col(0)],
        out_specs=[pl.BlockSpec((4, T, HEAD), lambda h: (0, 0, h)), pl.BlockSpec((SUBLANES, HEAD), lambda h: (0, h))],
        out_shape=[jax.ShapeDtypeStruct((4, T, d_model), BF16), jax.ShapeDtypeStruct((SUBLANES, d_model), F32)],
        compiler_params=_params(("parallel",)),
    )(proj, proj, proj, lb, states, do, dgate)


def _lb_softmax(table):
    n, f = table.shape

    def body(t_ref, p_ref):
        t = t_ref[...]
        e = jnp.exp(t - jnp.max(t, axis=0, keepdims=True))
        p_ref[...] = e / jnp.sum(e, axis=0, keepdims=True)

    padded = jnp.pad(table, ((0, SUBLANES - n), (0, 0)), constant_values=-jnp.inf)
    return _pcall(body, name="lb_softmax", out_shape=jax.ShapeDtypeStruct((SUBLANES, f), F32))(padded)


def _adamw_math(w, g, m, v):
    m = ADAM_B1 * m + (1.0 - ADAM_B1) * g
    v = ADAM_B2 * v + (1.0 - ADAM_B2) * (g * g)
    m_hat = m / (1.0 - ADAM_B1 ** ADAM_STEP)
    v_hat = v / (1.0 - ADAM_B2 ** ADAM_STEP)
    delta = -ADAM_LR * (m_hat / (jnp.sqrt(v_hat) + ADAM_EPS) + ADAM_WD * w)
    return delta, m, v


def _adamw(w, g, m, v, name):
    R, C = w.shape
    tr = _pick(R, (128, 64, 32, 16, 8))

    def body(w_ref, g_ref, m_ref, v_ref, d_ref, nm_ref, nv_ref):
        d, nm, nv = _adamw_math(w_ref[...], g_ref[...], m_ref[...], v_ref[...])
        d_ref[...] = d
        nm_ref[...] = nm
        nv_ref[...] = nv

    spec = pl.BlockSpec((tr, C), lambda i: (i, 0))
    return _pcall(
        body, name=name, grid=(R // tr,), in_specs=[spec] * 4, out_specs=[spec] * 3,
        out_shape=[jax.ShapeDtypeStruct((R, C), F32)] * 3, compiler_params=_params(("parallel",)),
    )(w, g, m, v)


def _adamw_halves(w, m, v, g_mine, g_recv, c, name):
    R, C = w.shape
    rh = R // 2
    tr = _pick(rh, (128, 64, 32, 16, 8))
    nb = rh // tr

    def body(c_ref, w_ref, m_ref, v_ref, gm_ref, gr_ref, g_ref, d_ref, nm_ref, nv_ref):
        g = jnp.where(pl.program_id(0) == c_ref[0], gm_ref[...], gr_ref[...])
        d, nm, nv = _adamw_math(w_ref[...], g, m_ref[...], v_ref[...])
        g_ref[...] = g
        d_ref[...] = d
        nm_ref[...] = nm
        nv_ref[...] = nv

    full = pl.BlockSpec((tr, C), lambda h, i, cr: (h * nb + i, 0))
    half = pl.BlockSpec((tr, C), lambda h, i, cr: (i, 0))
    return _pcall(
        body, name=name,
        grid_spec=pltpu.PrefetchScalarGridSpec(
            num_scalar_prefetch=1, grid=(2, nb), in_specs=[full, full, full, half, half], out_specs=[full] * 4),
        out_shape=[jax.ShapeDtypeStruct((R, C), F32)] * 4,
        compiler_params=_params(("parallel", "parallel")),
    )(c, w, m, v, g_mine, g_recv)


def _lb_table_grad(p8, dlb, n):
    f = p8.shape[1]

    def body(p_ref, d_ref, o_ref):
        p = p_ref[...]
        d = d_ref[...]
        p0 = p[0:1, :]
        first = lax.broadcasted_iota(jnp.int32, p.shape, 0) == 0
        o_ref[...] = p * (jnp.where(first, d, 0.0) - d * p0)

    return _pcall(body, name="lb_table_grad", out_shape=jax.ShapeDtypeStruct((SUBLANES, f), F32))(p8, dlb)[:n]


def _place():
    x, y, c = lax.axis_index("x"), lax.axis_index("y"), lax.axis_index("c")
    chips = [(1 - x, y), (x, 1 - y), (1 - x, 1 - y)]
    return x, y, c, chips


HBM_SPEC = pl.BlockSpec(memory_space=pltpu.HBM)


def _gather_weights(big, small):
    nb, ns = len(big), len(small)
    n = nb + ns

    def body(*refs):
        ins, outs = refs[:n], refs[n:2 * n]
        send_sems, recv_sems, own_send, own_recv = refs[2 * n:]
        x, y, c, chips = _place()
        me = 2 * x + y
        sib = (x, y, 1 - c)
        own = [pltpu.make_async_remote_copy(
            src_ref=ins[t], dst_ref=outs[t].at[me], send_sem=own_send.at[t], recv_sem=own_recv.at[t],
            device_id=sib, device_id_type=MESH) for t in range(n)]
        for cp in own:
            cp.start()

        def half(t, h):
            rh = big[t].shape[0] // 2
            return pl.ds(pl.multiple_of(h * rh, rh), rh)

        sends = []
        for t in range(n):
            for j, chip in enumerate(chips):
                k = 6 * t + j
                if t < nb:
                    src, dst = ins[t].at[half(t, c)], outs[t].at[me, half(t, c)]
                else:
                    src, dst = ins[t], outs[t].at[me]
                sends.append(pltpu.make_async_remote_copy(
                    src_ref=src, dst_ref=dst, send_sem=send_sems.at[k], recv_sem=recv_sems.at[k],
                    device_id=(*chip, c), device_id_type=MESH))
        for cp in sends:
            cp.start()
        passed = []
        for t in range(n):
            for j, (cx, cy) in enumerate(chips):
                k = 6 * t + j
                s = 2 * cx + cy
                if t < nb:
                    landed = outs[t].at[s, half(t, c)]
                    pltpu.make_async_remote_copy(
                        src_ref=landed, dst_ref=landed, send_sem=send_sems.at[k], recv_sem=recv_sems.at[k],
                        device_id=sib, device_id_type=MESH).wait_recv()
                    fwd = pltpu.make_async_remote_copy(
                        src_ref=landed, dst_ref=landed, send_sem=send_sems.at[k + 3], recv_sem=recv_sems.at[k + 3],
                        device_id=sib, device_id_type=MESH)
                    fwd.start()
                    passed.append(fwd)
                else:
                    landed = outs[t].at[s]
                    pltpu.make_async_remote_copy(
                        src_ref=landed, dst_ref=landed, send_sem=send_sems.at[k], recv_sem=recv_sems.at[k],
                        device_id=sib, device_id_type=MESH).wait_recv()
        for t in range(nb):
            for j, (cx, cy) in enumerate(chips):
                k = 6 * t + j
                other = outs[t].at[2 * cx + cy, half(t, 1 - c)]
                pltpu.make_async_remote_copy(
                    src_ref=other, dst_ref=other, send_sem=send_sems.at[k + 3], recv_sem=recv_sems.at[k + 3],
                    device_id=sib, device_id_type=MESH).wait_recv()
        for cp in sends + passed:
            cp.wait_send()
        for cp in own:
            cp.wait()

    arrs = list(big) + list(small)
    return _pcall(
        body, name="gather_weights", in_specs=[HBM_SPEC] * n, out_specs=[HBM_SPEC] * n,
        out_shape=[jax.ShapeDtypeStruct((N_CHIPS,) + a.shape, a.dtype) for a in arrs],
        scratch_shapes=[pltpu.SemaphoreType.DMA((6 * n,)), pltpu.SemaphoreType.DMA((6 * n,)),
                        pltpu.SemaphoreType.DMA((n,)), pltpu.SemaphoreType.DMA((n,))],
    )(*arrs)


def _pair_exchange(grads):
    n = len(grads)

    def body(*refs):
        ins, outs = refs[:n], refs[n:2 * n]
        send_sems, recv_sems = refs[2 * n:]
        x, y, c, _ = _place()
        cps = [pltpu.make_async_remote_copy(
            src_ref=ins[t].at[:, 1 - c], dst_ref=outs[t], send_sem=send_sems.at[t], recv_sem=recv_sems.at[t],
            device_id=(x, y, 1 - c), device_id_type=MESH) for t in range(n)]
        for cp in cps:
            cp.start()
        for cp in cps:
            cp.wait()

    return _pcall(
        body, name="grad_pair_exchange", in_specs=[HBM_SPEC] * n, out_specs=[HBM_SPEC] * n,
        out_shape=[jax.ShapeDtypeStruct((g.shape[0],) + g.shape[2:], g.dtype) for g in grads],
        scratch_shapes=[pltpu.SemaphoreType.DMA((n,)), pltpu.SemaphoreType.DMA((n,))],
    )(*grads)


def _chip_exchange(parts):
    n = len(parts)

    def body(*refs):
        ins, outs = refs[:n], refs[n:2 * n]
        send_sems, recv_sems = refs[2 * n:]
        x, y, c, chips = _place()
        cps = []
        for t in range(n):
            for j, (cx, cy) in enumerate(chips):
                cps.append(pltpu.make_async_remote_copy(
                    src_ref=ins[t].at[2 * cx + cy], dst_ref=outs[t].at[j],
                    send_sem=send_sems.at[3 * t + j], recv_sem=recv_sems.at[3 * t + j],
                    device_id=(cx, cy, c), device_id_type=MESH))
        for cp in cps:
            cp.start()
        for cp in cps:
            cp.wait()

    return _pcall(
        body, name="grad_chip_exchange", in_specs=[HBM_SPEC] * n, out_specs=[HBM_SPEC] * n,
        out_shape=[jax.ShapeDtypeStruct((3,) + p.shape[1:], p.dtype) for p in parts],
        scratch_shapes=[pltpu.SemaphoreType.DMA((3 * n,)), pltpu.SemaphoreType.DMA((3 * n,))],
    )(*parts)


def _pair_share(halves):
    n = len(halves)

    def body(*refs):
        ins, outs = refs[:n], refs[n:2 * n]
        send_sems, recv_sems = refs[2 * n:]
        x, y, c, _ = _place()
        cps = [pltpu.make_async_remote_copy(
            src_ref=ins[t], dst_ref=outs[t], send_sem=send_sems.at[t], recv_sem=recv_sems.at[t],
            device_id=(x, y, 1 - c), device_id_type=MESH) for t in range(n)]
        for cp in cps:
            cp.start()
        for cp in cps:
            cp.wait()

    return _pcall(
        body, name="grad_pair_share", in_specs=[HBM_SPEC] * n, out_specs=[HBM_SPEC] * n,
        out_shape=[jax.ShapeDtypeStruct(h.shape, h.dtype) for h in halves],
        scratch_shapes=[pltpu.SemaphoreType.DMA((n,)), pltpu.SemaphoreType.DMA((n,))],
    )(*halves)


def _add_pair(grad, recv, c, name):
    s, _, rh, cc = grad.shape
    tr = _pick(rh, (256, 128, 64, 32, 16))

    def body(c_ref, g_ref, r_ref, o_ref):
        o_ref[...] = (g_ref[...].astype(F32) + r_ref[...].astype(F32)).astype(o_ref.dtype)

    return _pcall(
        body, name=name,
        grid_spec=pltpu.PrefetchScalarGridSpec(
            num_scalar_prefetch=1, grid=(s, rh // tr),
            in_specs=[pl.BlockSpec((None, None, tr, cc), lambda a, i, cr: (a, cr[0], i, 0)),
                      pl.BlockSpec((None, tr, cc), lambda a, i, cr: (a, i, 0))],
            out_specs=pl.BlockSpec((None, tr, cc), lambda a, i, cr: (a, i, 0))),
        out_shape=jax.ShapeDtypeStruct((s, rh, cc), BF16),
        compiler_params=_params(("parallel", "parallel")),
    )(c, grad, recv)


def _add_chips(part, recv, me, name):
    _, rh, cc = part.shape
    tr = _pick(rh, (256, 128, 64, 32, 16))

    def body(m_ref, p_ref, r_ref, o_ref):
        o_ref[...] = ((p_ref[...].astype(F32) + r_ref[0].astype(F32)) + r_ref[1].astype(F32)) + r_ref[2].astype(F32)

    return _pcall(
        body, name=name,
        grid_spec=pltpu.PrefetchScalarGridSpec(
            num_scalar_prefetch=1, grid=(rh // tr,),
            in_specs=[pl.BlockSpec((None, tr, cc), lambda i, mr: (mr[0], i, 0)),
                      pl.BlockSpec((3, tr, cc), lambda i, mr: (0, i, 0))],
            out_specs=pl.BlockSpec((tr, cc), lambda i, mr: (i, 0))),
        out_shape=jax.ShapeDtypeStruct((rh, cc), F32),
        compiler_params=_params(("parallel",)),
    )(me, part, recv)


def _all_sum(vec):
    rows = vec.shape[0]

    def body(v_ref, o_ref, buf, send_sems, recv_sems):
        x, y, c, _ = _place()
        me = 4 * x + 2 * y + c
        buf[me] = v_ref[...]
        cps = []
        for r in range(1, 8):
            fx, fy, fc = (r >> 2) & 1, (r >> 1) & 1, r & 1
            peer = (x ^ fx, y ^ fy, c ^ fc)
            cps.append(pltpu.make_async_remote_copy(
                src_ref=v_ref, dst_ref=buf.at[me], send_sem=send_sems.at[r - 1], recv_sem=recv_sems.at[r - 1],
                device_id=peer, device_id_type=MESH))
        for cp in cps:
            cp.start()
        for r in range(1, 8):
            src = me ^ r
            pltpu.make_async_remote_copy(
                src_ref=v_ref, dst_ref=buf.at[src], send_sem=send_sems.at[r - 1], recv_sem=recv_sems.at[r - 1],
                device_id=(x, y, c), device_id_type=MESH).wait_recv()
        for cp in cps:
            cp.wait_send()
        acc = buf[0]
        for d in range(1, 8):
            acc = acc + buf[d]
        o_ref[...] = acc

    return _pcall(
        body, name="all_sum_small",
        in_specs=[pl.BlockSpec(memory_space=pltpu.VMEM)], out_specs=pl.BlockSpec(memory_space=pltpu.VMEM),
        out_shape=jax.ShapeDtypeStruct((rows, LANES), F32),
        scratch_shapes=[pltpu.VMEM((8, rows, LANES), F32), pltpu.SemaphoreType.DMA((7,)), pltpu.SemaphoreType.DMA((7,))],
    )(vec)


def _pack(parts):
    flat = jnp.concatenate([p.reshape(-1) for p in parts])
    tile = SUBLANES * LANES
    pad = (-flat.shape[0]) % tile
    return jnp.pad(flat, (0, pad)).reshape(-1, LANES)


def _unpack(vec, shapes):
    flat = vec.reshape(-1)
    out, p = [], 0
    for s in shapes:
        n = 1
        for d in s:
            n *= d
        out.append(flat[p:p + n].reshape(s))
        p += n
    return out


def _local_step(x, tgt, norm_mix, norm_ffn, lb8, out_norm, final_norm, sc_conv, ffn_conv,
                w_hin, w_hout, w_sin, w_sout, w_up, w_down):
    T, D = x.shape
    F2 = ffn_conv.shape[-1]
    FF = F2 // 2
    tm = _pick(T, (1024, 512, 256, 128))
    wide = (1536, 1408, 1024, 768, 512, 384, 256, 128)
    cw_h, cw_s, cw_u = w_hin.shape[2], w_sin.shape[2], w_up.shape[2]
    kp = w_down.shape[1] // 2
    tk_ff = kp if kp % LANES == 0 else LANES
    tn_d = _pick(D, (1024, 512, 256, 128))
    tk_w = _pick(D, (512, 256, 128))
    tn_h = _pick(cw_h, (1024, 512, 256, 128))
    tn_s = _pick(D // N_CHIPS, (512, 256, 128))
    tn_u = _pick(cw_u, wide)
    lb = lb8[0:1]
    w_hout1, w_sout1 = w_hout.reshape(1, D, D), w_sout.reshape(1, D, D)
    wm_sq = _wmap_col(D, tn_d, 0)
    wm_sq1 = _wmap_col(D, D, 0)
    seg1 = lambda a: a.reshape((1,) + a.shape)

    def mix_in(h, w):
        return _row_call(_rms_fwd_fn, [(h, 0, D)], [w], [(D, BF16)], 0, "rms_fwd")[0]

    def rms_bwd(h, dxn, dh, w):
        return _row_call(_rms_bwd_fn, [(h, 0, D), (dxn, 0, D), (dh, 0, D)], [w], [(D, F32), (D, BF16)], 1, "rms_bwd")

    def ffn_fwd(h, i):
        xn = mix_in(h, norm_ffn[i:i + 1])
        tn = _pick(cw_u, wide)
        up = _mm_nn(xn, w_up, _wmap_col(cw_u, tn, i), D, F2, tm, D, tn, "ffn_up")
        nb = FF // LANES
        a = _col_call(_glu_fwd_fn, [(up, 0), (up, nb)], [(ffn_conv[i], 0), (ffn_conv[i], nb)], [(1, FF, BF16)], 0,
                      "glu_fwd", before=True, after=False)[0][0]
        h2 = _mm_nn(a, w_down, _wmap_row(kp, tk_ff, i * (kp // tk_ff)), FF, D, tm, tk_ff, tn_d, "ffn_down", res=h)
        return h2, (xn, up, a)

    def ffn_bwd(dh, dh16, h, saved, i):
        xn, up, a = saved
        da = _mm_nt(seg1(dh16), w_down, _wmap_row(kp, tk_ff, i * (kp // tk_ff)), FF, D, tm, tk_ff, D, "ffn_down_dx")
        nb = FF // LANES
        dgv, cg, cv = _col_call(_glu_bwd_fn, [(up, 0), (up, nb), (da, 0)], [(ffn_conv[i], 0), (ffn_conv[i], nb)],
                                [(2, FF, BF16)], 2, "glu_bwd", before=True, after=True)
        dxn = _mm_nt(dgv, w_up, _wmap_col(cw_u, tn_u, i), D, F2, tm, D, tn_u, "ffn_up_dx")
        dh2, dh2_16, dnw = rms_bwd(h, dxn, dh, norm_ffn[i:i + 1])
        return dh2, dh2_16, dnw, jnp.concatenate([cg[:3], cv[:3]], axis=1), (a, dh16, xn, dgv)

    h0 = x
    xn0 = mix_in(h0, norm_mix[0:1])
    proj = _mm_nn(xn0, w_hin, _wmap_col(cw_h, tn_h, 0), D, 4 * D, tm, D, tn_h, "hgrn_in")
    o, states = _hgrn_fwd(proj, lb, D)
    on = _row_call(_onorm_fwd_fn, [(o, 0, D), (proj, 3, D)], [out_norm], [(D, BF16)], 0, "onorm_fwd")[0]
    h1 = _mm_nn(on, w_hout1, wm_sq, D, D, tm, D, tn_d, "hgrn_out", res=h0)
    h2, ffn0 = ffn_fwd(h1, 0)
    xn1 = mix_in(h2, norm_mix[1:2])
    tn_si = _pick(cw_s, wide)
    sproj = _mm_nn(xn1, w_sin, _wmap_col(cw_s, tn_si, 0), D, 3 * D, tm, D, tn_si, "sc_in")
    nd = D // LANES
    ysc = _col_call(_sc_fwd_fn, [(sproj, 0), (sproj, nd), (sproj, 2 * nd)], [(sc_conv, 0)], [(1, D, BF16)], 0,
                    "sc_fwd", before=True, after=False)[0][0]
    h3 = _mm_nn(ysc, w_sout1, wm_sq, D, D, tm, D, tn_d, "sc_out", res=h2)
    h4, ffn1 = ffn_fwd(h3, 1)

    dh, dh16, esq, dfinal = _row_call(_final_fn, [(h4, 0, D), (tgt, 0, D)], [final_norm], [(D, F32), (D, BF16)], 2,
                                      "final_loss")
    loss = 0.5 / D * jnp.sum(esq)
    dh, dh16, dnf1, dconv1, ffn1_dw = ffn_bwd(dh, dh16, h3, ffn1, 1)
    dy = _mm_nt(seg1(dh16), w_sout1, wm_sq1, D, D, tm, D, D, "sc_out_dx")
    dsp, dscc = _col_call(_sc_bwd_fn, [(sproj, 0), (sproj, nd), (sproj, 2 * nd), (dy, 0)], [(sc_conv, 0)],
                          [(3, D, BF16)], 1, "sc_bwd", before=True, after=True)
    dxn = _mm_nt(dsp, w_sin, _wmap_col(cw_s, tn_s, 0), D, 3 * D, tm, D, tn_s, "sc_in_dx")
    g_sout = _mm_tn(ysc, seg1(dh16), (1, 1, D, D), 0, wm_sq, D, D, tk_w, tn_d, "sc_out_dw")
    g_sin = _mm_tn(xn1, dsp, (N_CHIPS, 1, D, cw_s), 0, _wmap_col(cw_s, tn_s, 0), D, 3 * D, tk_w, tn_s, "sc_in_dw")
    dh, dh16, dnm1 = rms_bwd(h2, dxn, dh, norm_mix[1:2])
    dh, dh16, dnf0, dconv0, ffn0_dw = ffn_bwd(dh, dh16, h1, ffn0, 0)
    don = _mm_nt(seg1(dh16), w_hout1, wm_sq1, D, D, tm, D, D, "hgrn_out_dx")
    do, dgate, dgain = _row_call(_onorm_bwd_fn, [(o, 0, D), (proj, 3, D), (don, 0, D)], [out_norm],
                                 [(D, F32), (D, BF16)], 1, "onorm_bwd")
    dproj, dlb = _hgrn_bwd(proj, lb, states, do, dgate, D)
    dxn = _mm_nt(dproj, w_hin, _wmap_col(cw_h, tn_h, 0), D, 4 * D, tm, D, tn_h, "hgrn_in_dx")
    g_hout = _mm_tn(on, seg1(dh16), (1, 1, D, D), 0, wm_sq, D, D, tk_w, tn_d, "hgrn_out_dw")
    g_hin = _mm_tn(xn0, dproj, (N_CHIPS, 1, D, cw_h), 0, _wmap_col(cw_h, tn_h, 0), D, 4 * D, tk_w, tn_h, "hgrn_in_dw")
    grad_x, _, dnm0 = rms_bwd(h0, dxn, dh, norm_mix[0:1])

    g_down = g_up = None
    for i, (a, d16, xn, dgv) in enumerate((ffn0_dw, ffn1_dw)):
        g_down = _mm_tn(a, seg1(d16), (N_CHIPS, 2, kp, D), i, _wmap_row(kp, tk_ff, 0), FF, D, tk_ff, tn_d,
                        "ffn_down_dw", into=g_down, tm=_pick(T, (2048, 1024, 512, 256, 128)))
        g_up = _mm_tn(xn, dgv, (N_CHIPS, 2, D, cw_u), i, _wmap_col(cw_u, tn_u, 0), D, F2, tk_w, tn_u,
                      "ffn_up_dw", into=g_up)

    small = dict(
        loss=loss,
        norm_mix=jnp.stack([jnp.sum(dnm0, axis=0), jnp.sum(dnm1, axis=0)]),
        norm_ffn=jnp.stack([jnp.sum(dnf0, axis=0), jnp.sum(dnf1, axis=0)]),
        lb=dlb[0:1],
        out_norm=jnp.sum(dgain, axis=0)[None],
        final_norm=jnp.sum(dfinal, axis=0),
        sc_conv=dscc[:3],
        ffn_conv=jnp.stack([dconv0, dconv1]),
    )
    return grad_x, small, (g_hin, g_hout, g_sin, g_sout, g_up, g_down)


def kernel(x, norm_mix, norm_ffn, hgrn_w_in, hgrn_lb_table, hgrn_out_norm, hgrn_w_out, sc_w_in, sc_conv, sc_w_out, ffn_w_up, ffn_conv, ffn_w_down, final_norm, loss_target, m_norm_mix, m_norm_ffn, m_hgrn_w_in, m_hgrn_lb_table, m_hgrn_out_norm, m_hgrn_w_out, m_sc_w_in, m_sc_conv, m_sc_w_out, m_ffn_w_up, m_ffn_conv, m_ffn_w_down, m_final_norm, v_norm_mix, v_norm_ffn, v_hgrn_w_in, v_hgrn_lb_table, v_hgrn_out_norm, v_hgrn_w_out, v_sc_w_in, v_sc_conv, v_sc_w_out, v_ffn_w_up, v_ffn_conv, v_ffn_w_down, v_final_norm):
    D = x.shape[-1]
    xi, yi, ci = lax.axis_index("x"), lax.axis_index("y"), lax.axis_index("c")
    me_chip = (2 * xi + yi).astype(jnp.int32).reshape(1)
    me_core = ci.astype(jnp.int32).reshape(1)

    big_names = ["hgrn_w_in", "hgrn_w_out", "sc_w_in", "sc_w_out", "ffn_w_up", "ffn_w_down"]
    big_w = dict(hgrn_w_in=hgrn_w_in, hgrn_w_out=hgrn_w_out, sc_w_in=sc_w_in, sc_w_out=sc_w_out,
                 ffn_w_up=ffn_w_up, ffn_w_down=ffn_w_down)
    big_m = dict(hgrn_w_in=m_hgrn_w_in, hgrn_w_out=m_hgrn_w_out, sc_w_in=m_sc_w_in, sc_w_out=m_sc_w_out,
                 ffn_w_up=m_ffn_w_up, ffn_w_down=m_ffn_w_down)
    big_v = dict(hgrn_w_in=v_hgrn_w_in, hgrn_w_out=v_hgrn_w_out, sc_w_in=v_sc_w_in, sc_w_out=v_sc_w_out,
                 ffn_w_up=v_ffn_w_up, ffn_w_down=v_ffn_w_down)
    flat2 = lambda a: a.reshape(-1, a.shape[-1])

    shards16 = [flat2(big_w[n]).astype(BF16) for n in big_names]
    conv_shards = [flat2(sc_conv), flat2(ffn_conv)]
    gathered = _gather_weights(shards16, conv_shards)
    w_hin, w_hout, w_sin, w_sout, w_up, w_down = gathered[:6]
    scc = jnp.moveaxis(gathered[6], 0, 1).reshape(3, D)
    f2 = ffn_conv.shape[-1] * N_CHIPS
    fcc = jnp.moveaxis(gathered[7].reshape(N_CHIPS, 2, 3, -1), 0, 2).reshape(2, 3, f2)

    lb8 = _lb_softmax(hgrn_lb_table)
    grad_x, small, bigs = _local_step(
        x[0], loss_target[0], norm_mix, norm_ffn, lb8, hgrn_out_norm, final_norm[None], scc, fcc,
        w_hin, w_hout, w_sin, w_sout, w_up, w_down)

    small_names = ["loss", "norm_mix", "norm_ffn", "lb", "out_norm", "final_norm", "sc_conv", "ffn_conv"]
    parts = [small[n].astype(F32) for n in small_names]
    shapes = [p.shape for p in parts]
    tot = dict(zip(small_names, _unpack(_all_sum(_pack(parts)), shapes)))
    loss = tot["loss"].reshape(())
    g_lb_table = _lb_table_grad(lb8, tot["lb"], hgrn_lb_table.shape[0])
    cw = sc_conv.shape[-1]
    g_sc_conv = lax.dynamic_slice_in_dim(tot["sc_conv"], me_chip[0] * cw, cw, axis=1)[None]
    cf = ffn_conv.shape[-1]
    g_ffn_conv = lax.dynamic_slice_in_dim(tot["ffn_conv"], me_chip[0] * cf, cf, axis=2)
    g_small = dict(norm_mix=tot["norm_mix"], norm_ffn=tot["norm_ffn"], hgrn_lb_table=g_lb_table,
                   hgrn_out_norm=tot["out_norm"], sc_conv=g_sc_conv, ffn_conv=g_ffn_conv, final_norm=tot["final_norm"])
    w_small = dict(norm_mix=norm_mix, norm_ffn=norm_ffn, hgrn_lb_table=hgrn_lb_table, hgrn_out_norm=hgrn_out_norm,
                   sc_conv=sc_conv, ffn_conv=ffn_conv, final_norm=final_norm)
    m_small = dict(norm_mix=m_norm_mix, norm_ffn=m_norm_ffn, hgrn_lb_table=m_hgrn_lb_table, hgrn_out_norm=m_hgrn_out_norm,
                   sc_conv=m_sc_conv, ffn_conv=m_ffn_conv, final_norm=m_final_norm)
    v_small = dict(norm_mix=v_norm_mix, norm_ffn=v_norm_ffn, hgrn_lb_table=v_hgrn_lb_table, hgrn_out_norm=v_hgrn_out_norm,
                   sc_conv=v_sc_conv, ffn_conv=v_ffn_conv, final_norm=v_final_norm)
    sm_names = list(g_small)
    sm_shapes = [w_small[n].shape for n in sm_names]
    d_s, m_s, v_s = _adamw(_pack([w_small[n] for n in sm_names]), _pack([g_small[n] for n in sm_names]),
                           _pack([m_small[n] for n in sm_names]), _pack([v_small[n] for n in sm_names]), "adamw_small")
    out_g, out_d, out_m, out_v = dict(g_small), {}, {}, {}
    for n, d_, m_, v_ in zip(sm_names, _unpack(d_s, sm_shapes), _unpack(m_s, sm_shapes), _unpack(v_s, sm_shapes)):
        out_d[n], out_m[n], out_v[n] = d_, m_, v_

    halves = [g.reshape(N_CHIPS, 2, -1, g.shape[-1]) for g in bigs]
    recv1 = _pair_exchange(halves)
    pair = [_add_pair(g, r, me_core, "grad_add_pair") for g, r in zip(halves, recv1)]
    recv2 = _chip_exchange(pair)
    mine = [_add_chips(p, r, me_chip, "grad_add_chips") for p, r in zip(pair, recv2)]
    theirs = _pair_share(mine)
    for n, gm, gr in zip(big_names, mine, theirs):
        w = big_w[n]
        res = _adamw_halves(flat2(w), flat2(big_m[n]), flat2(big_v[n]), gm, gr, me_core, "adamw_" + n)
        out_g[n], out_d[n], out_m[n], out_v[n] = (a.reshape(w.shape) for a in res)

    order = ["norm_mix", "norm_ffn", "hgrn_w_in", "hgrn_lb_table", "hgrn_out_norm", "hgrn_w_out", "sc_w_in", "sc_conv",
             "sc_w_out", "ffn_w_up", "ffn_conv", "ffn_w_down", "final_norm"]
    return (loss, grad_x[None], *[out_g[n] for n in order], *[out_d[n] for n in order],
            *[out_m[n] for n in order], *[out_v[n] for n in order])
```

```python
import functools

import jax
import jax.numpy as jnp
from jax import lax
from jax.experimental import pallas as pl
from jax.experimental.pallas import tpu as pltpu

F32 = jnp.float32
BF16 = jnp.bfloat16
MESH = pl.DeviceIdType.MESH

EPS = 1e-6
CHUNK = 64
HEAD = 128
N_CHIPS = 4
ADAM_LR, ADAM_B1, ADAM_B2, ADAM_EPS, ADAM_WD, ADAM_STEP = 0.001, 0.9, 0.999, 1e-08, 0.01, 10
VMEM_LIMIT = 56 * 1024 * 1024
SUBLANES = 8
LANES = 128


def _pcall(body, **kw):
    return pl.pallas_call(body, **kw)


def _params(sem, vmem=VMEM_LIMIT):
    return pltpu.CompilerParams(dimension_semantics=sem, vmem_limit_bytes=vmem)


def _pick(dim, prefs):
    for p in prefs:
        if p <= dim and dim % p == 0:
            return p
    return dim


def _sigmoid(x):
    return 1.0 / (1.0 + jnp.exp(-x))


def _wmap_col(cw, tn, r0):
    bps = cw // tn
    return lambda kb, nb: (nb // bps, r0 + kb, nb % bps)


def _wmap_row(kp, tk, r0):
    bps = kp // tk
    return lambda kb, nb: (kb // bps, r0 + kb % bps, nb)


def _mm_nn(a, w3, wmap, K, N, tm, tk, tn, name, res=None):
    M = a.shape[0]
    nk = K // tk

    def body(*refs):
        if res is None:
            a_ref, w_ref, o_ref = refs[:3]
        else:
            a_ref, w_ref, r_ref, o_ref = refs[:4]
        p = jnp.dot(a_ref[...], w_ref[...], preferred_element_type=F32)
        if nk == 1:
            o_ref[...] = p if res is None else p + r_ref[...]
            return
        acc = refs[-1]
        k = pl.program_id(2)

        @pl.when(k == 0)
        def _():
            acc[...] = p

        @pl.when(k > 0)
        def _():
            acc[...] += p

        @pl.when(k == nk - 1)
        def _():
            o_ref[...] = acc[...] if res is None else acc[...] + r_ref[...]

    if nk == 1:
        grid = (M // tm, N // tn)
        ix = lambda f: (lambda i, j: f(i, j, 0))
        sem = ("parallel", "parallel")
        scratch = []
    else:
        grid = (M // tm, N // tn, nk)
        ix = lambda f: f
        sem = ("parallel", "parallel", "arbitrary")
        scratch = [pltpu.VMEM((tm, tn), F32)]
    in_specs = [pl.BlockSpec((tm, tk), ix(lambda i, j, k: (i, k))),
                pl.BlockSpec((None, tk, tn), ix(lambda i, j, k: wmap(k, j)))]
    args = [a, w3]
    if res is not None:
        in_specs.append(pl.BlockSpec((tm, tn), ix(lambda i, j, k: (i, j))))
        args.append(res)
    return _pcall(
        body, name=name, grid=grid, in_specs=in_specs,
        out_specs=pl.BlockSpec((tm, tn), ix(lambda i, j, k: (i, j))),
        out_shape=jax.ShapeDtypeStruct((M, N), F32), scratch_shapes=scratch, compiler_params=_params(sem),
    )(*args)


def _mm_nt(dy3, w3, wmap, K, N, tm, tk, tn, name):
    M = dy3.shape[1]
    bps = dy3.shape[2] // tn
    grid = (M // tm, K // tk, N // tn)
    nn = grid[2]

    def body(dy_ref, w_ref, o_ref):
        p = lax.dot_general(dy_ref[...], w_ref[...], (((1,), (1,)), ((), ())), preferred_element_type=F32)
        if nn == 1:
            o_ref[...] = p
            return
        n = pl.program_id(2)

        @pl.when(n == 0)
        def _():
            o_ref[...] = p

        @pl.when(n > 0)
        def _():
            o_ref[...] += p

    return _pcall(
        body, name=name, grid=grid,
        in_specs=[pl.BlockSpec((None, tm, tn), lambda i, j, n: (n // bps, i, n % bps)),
                  pl.BlockSpec((None, tk, tn), lambda i, j, n: wmap(j, n))],
        out_specs=pl.BlockSpec((tm, tk), lambda i, j, n: (i, j)),
        out_shape=jax.ShapeDtypeStruct((M, K), F32),
        compiler_params=_params(("parallel", "parallel", "arbitrary")),
    )(dy3, w3)


def _mm_tn(x, dy3, shape4, wmap, K, N, tk, tn, name, tm=None):
    M = x.shape[0]
    tm = M if tm is None else tm
    nm = M // tm
    bps = dy3.shape[2] // tn

    def body(*refs):
        x_ref, dy_ref = refs[:2]
        p = lax.dot_general(x_ref[...], dy_ref[...], (((0,), (0,)), ((), ())), preferred_element_type=F32)
        if nm == 1:
            o_ref = refs[-1]
            o_ref[...] = p.astype(o_ref.dtype)
            return
        o_ref, acc = refs[-2:]
        m = pl.program_id(2)

        @pl.when(m == 0)
        def _():
            acc[...] = p

        @pl.when(m > 0)
        def _():
            acc[...] += p

        @pl.when(m == nm - 1)
        def _():
            o_ref[...] = acc[...].astype(o_ref.dtype)

    def omap(i, j, m):
        s, rb, cb = wmap(i, j)
        return (s, 0, rb, cb)

    return _pcall(
        body, name=name, grid=(K // tk, N // tn, nm),
        in_specs=[pl.BlockSpec((tm, tk), lambda i, j, m: (m, i)),
                  pl.BlockSpec((None, tm, tn), lambda i, j, m: (j // bps, m, j % bps))],
        out_specs=pl.BlockSpec((None, None, tk, tn), omap),
        out_shape=jax.ShapeDtypeStruct(shape4, BF16),
        scratch_shapes=[] if nm == 1 else [pltpu.VMEM((tk, tn), F32)],
        compiler_params=_params(("parallel", "parallel", "arbitrary")),
    )(x, dy3)


def _row_call(fn, rows, vecs, outs, n_acc, name, t_rows=256, sub=16):
    T = rows[0][0].shape[0]
    t_rows = min(t_rows, T)
    nsub = t_rows // sub
    n_r, n_v, n_o = len(rows), len(vecs), len(outs)
    width = rows[0][2]

    def body(*refs):
        r_refs = refs[:n_r]
        v_refs = refs[n_r:n_r + n_v]
        o_refs = refs[n_r + n_v:n_r + n_v + n_o]
        a_refs = refs[n_r + n_v + n_o:]

        @pl.when(pl.program_id(0) == 0)
        def _():
            for a in a_refs:
                a[...] = jnp.zeros_like(a)

        vv = [v[...] for v in v_refs]

        def step(i, carry):
            sl = pl.ds(pl.multiple_of(i * sub, sub), sub)
            o_vals, a_vals = fn([r[sl, :] for r in r_refs], vv)
            for o, val in zip(o_refs, o_vals):
                o[sl, :] = val.astype(o.dtype)
            for a, val in zip(a_refs, a_vals):
                a[...] += val.reshape(sub // SUBLANES, SUBLANES, val.shape[-1]).sum(axis=0)
            return carry

        lax.fori_loop(0, nsub, step, 0)

    in_specs = [pl.BlockSpec((t_rows, w), functools.partial(lambda i, cb: (i, cb), cb=cb)) for _, cb, w in rows]
    in_specs += [pl.BlockSpec(v.shape, lambda i: (0, 0)) for v in vecs]
    out_specs = [pl.BlockSpec((t_rows, w), lambda i: (i, 0)) for w, _ in outs]
    out_specs += [pl.BlockSpec((SUBLANES, width), lambda i: (0, 0)) for _ in range(n_acc)]
    out_shape = [jax.ShapeDtypeStruct((T, w), dt) for w, dt in outs]
    out_shape += [jax.ShapeDtypeStruct((SUBLANES, width), F32) for _ in range(n_acc)]
    return _pcall(
        body, name=name, grid=(T // t_rows,), in_specs=in_specs, out_specs=out_specs, out_shape=out_shape,
        compiler_params=_params(("arbitrary",)),
    )(*[r[0] for r in rows], *vecs)


def _rms_fwd_fn(rv, vv):
    h, = rv
    w, = vv
    r = lax.rsqrt(jnp.mean(h * h, axis=-1, keepdims=True) + EPS)
    return [h * r * w], []


def _rms_bwd_fn(rv, vv):
    h, dxn, dh_in = rv
    w, = vv
    d = h.shape[-1]
    r = lax.rsqrt(jnp.mean(h * h, axis=-1, keepdims=True) + EPS)
    gy = dxn * w
    dh = r * gy - h * ((r * r * r) * (1.0 / d) * jnp.sum(gy * h, axis=-1, keepdims=True))
    return [dh_in + dh] * 2, [dxn * h * r]


def _final_fn(rv, vv):
    h, tgt = rv
    w, = vv
    d = h.shape[-1]
    r = lax.rsqrt(jnp.mean(h * h, axis=-1, keepdims=True) + EPS)
    hn = h * r
    e = hn * w - tgt
    dy = e * (1.0 / d)
    gy = dy * w
    dh = r * gy - h * ((r * r * r) * (1.0 / d) * jnp.sum(gy * h, axis=-1, keepdims=True))
    return [dh] * 2, [e * e, dy * hn]


def _onorm_fwd_fn(rv, vv):
    o, g = rv
    gain, = vv
    r = lax.rsqrt(jnp.mean(o * o, axis=-1, keepdims=True) + EPS)
    return [o * r * gain * (g * _sigmoid(g))], []


def _onorm_bwd_fn(rv, vv):
    o, g, don = rv
    gain, = vv
    d = o.shape[-1]
    r = lax.rsqrt(jnp.mean(o * o, axis=-1, keepdims=True) + EPS)
    sg = _sigmoid(g)
    sl = g * sg
    n = o * r
    dg = don * n * gain * (sg * (1.0 + g * (1.0 - sg)))
    gy = don * sl * gain
    do = r * gy - o * ((r * r * r) * (1.0 / d) * jnp.sum(gy * o, axis=-1, keepdims=True))
    return [do, dg], [don * sl * n]


HALO = SUBLANES


def _col_call(fn, cols, vecs, outs, n_acc, name, before, after, tc=LANES, chunk=128):
    T = cols[0][0].shape[0]
    chunk = min(chunk, T)
    nch = T // chunk
    ncol = outs[0][1] // tc
    n_c, n_v, n_o = len(cols), len(vecs), len(outs)
    hb = HALO if before else 0
    rw = chunk + hb + (HALO if after else 0)

    def body(*refs):
        c_refs = refs[:n_c]
        v_refs = refs[n_c:n_c + n_v]
        o_refs = refs[n_c + n_v:n_c + n_v + n_o]
        a_refs = refs[n_c + n_v + n_o:]
        vv = [v[...] for v in v_refs]
        wrow = lax.broadcasted_iota(jnp.int32, (rw, tc), 0)
        inside = (wrow >= hb) & (wrow < hb + chunk)

        def step(i, carry):
            r0 = pl.multiple_of(i * chunk, chunk)
            wins = []
            for ref in c_refs:
                parts = []
                if before:
                    pb = ref[pl.ds(pl.multiple_of(jnp.maximum(r0 - HALO, 0), HALO), HALO), :]
                    parts.append(jnp.where(i > 0, pb, 0.0))
                parts.append(ref[pl.ds(r0, chunk), :])
                if after:
                    pa = ref[pl.ds(pl.multiple_of(jnp.minimum(r0 + chunk, T - HALO), HALO), HALO), :]
                    parts.append(jnp.where(i < nch - 1, pa, 0.0))
                wins.append(jnp.concatenate(parts, axis=0) if len(parts) > 1 else parts[0])
            o_vals, a_vals = fn(wins, vv, inside)
            p = 0
            for o, (nseg, _, _) in zip(o_refs, outs):
                for s in range(nseg):
                    o[s, pl.ds(r0, chunk), :] = o_vals[p][hb:hb + chunk].astype(o.dtype)
                    p += 1
            return tuple(c + a for c, a in zip(carry, a_vals))

        taps = [v.shape[0] for v, _ in vecs][:n_acc]
        init = tuple(jnp.zeros((1, tc), F32) for k in taps for _ in range(k))
        sums = lax.fori_loop(0, nch, step, init)
        arow = lax.broadcasted_iota(jnp.int32, (SUBLANES, tc), 0)
        p = 0
        for a, k in zip(a_refs, taps):
            acc = jnp.zeros((SUBLANES, tc), F32)
            for t in range(k):
                acc = jnp.where(arow == t, sums[p], acc)
                p += 1
            a[...] = acc

    in_specs = [pl.BlockSpec((T, tc), functools.partial(lambda j, off: (0, off + j), off=off)) for _, off in cols]
    in_specs += [pl.BlockSpec((v.shape[0], tc), functools.partial(lambda j, off: (0, off + j), off=off))
                 for v, off in vecs]
    out_specs = [pl.BlockSpec((nseg, T, tc), lambda j: (0, 0, j)) for nseg, _, _ in outs]
    out_specs += [pl.BlockSpec((SUBLANES, tc), lambda j: (0, j)) for _ in range(n_acc)]
    out_shape = [jax.ShapeDtypeStruct((nseg, T, w), dt) for nseg, w, dt in outs]
    out_shape += [jax.ShapeDtypeStruct((SUBLANES, ncol * tc), F32) for _ in range(n_acc)]
    return _pcall(
        body, name=name, grid=(ncol,), in_specs=in_specs, out_specs=out_specs, out_shape=out_shape,
        compiler_params=_params(("parallel",)),
    )(*[c[0] for c in cols], *[v[0] for v in vecs])


def _down(x, k):
    return x if k == 0 else pltpu.roll(x, k, 0)


def _up(x, k):
    return x if k == 0 else pltpu.roll(x, x.shape[0] - k, 0)


def _conv(x, w):
    return w[0:1] * _down(x, 2) + w[1:2] * _down(x, 1) + w[2:3] * x


def _conv_t(d, w):
    return w[2:3] * d + w[1:2] * _up(d, 1) + w[0:1] * _up(d, 2)


def _tap_sums(d, x, inside):
    dm = jnp.where(inside, d, 0.0)
    return [jnp.sum(dm * _down(x, 2 - k), axis=0, keepdims=True) for k in range(3)]


def _glu_fwd_fn(wins, vv, inside):
    xg, xv = wins
    wg, wv = vv
    ug = _conv(xg, wg)
    uv = _conv(xv, wv)
    return [ug * _sigmoid(ug) * uv], []


def _glu_bwd_fn(wins, vv, inside):
    xg, xv, da = wins
    wg, wv = vv
    ug = _conv(xg, wg)
    uv = _conv(xv, wv)
    sg = _sigmoid(ug)
    dug = da * uv * (sg * (1.0 + ug * (1.0 - sg)))
    duv = da * (ug * sg)
    return [_conv_t(dug, wg), _conv_t(duv, wv)], _tap_sums(dug, xg, inside) + _tap_sums(duv, xv, inside)


def _sc_fwd_fn(wins, vv, inside):
    gb, gc, hh = wins
    w, = vv
    return [gb * _conv(gc * hh, w)], []


def _sc_bwd_fn(wins, vv, inside):
    gb, gc, hh, dy = wins
    w, = vv
    z = gc * hh
    dcv = dy * gb
    dz = _conv_t(dcv, w)
    return [dy * _conv(z, w), dz * hh, dz * gc], _tap_sums(dcv, z, inside)


def _gates(qr, fr, lb):
    sg = _sigmoid(fr)
    f = lb + (1.0 - lb) * sg
    sq = _sigmoid(qr)
    q = qr * sq * (HEAD ** -0.5)
    return q, 1.0 - f, jnp.log(f), f, sg, sq


def _chunk_decays(gl, row):
    c = gl.shape[0]

    def seg_prefix(x, g):
        r = row & (g - 1)
        d = 1
        while d < g:
            x = x + jnp.where(r >= d, pltpu.roll(x, d, 0), 0.0)
            d *= 2
        return x

    def seg_suffix(x, g):
        r = row & (g - 1)
        d = 1
        while d < g:
            x = x + jnp.where(r < g - d, pltpu.roll(x, c - d, 0), 0.0)
            d *= 2
        return x

    eq, ek = [], []
    g = c // 2
    while g >= 1:
        right = (row & g) != 0
        eq.append(jnp.where(right, jnp.exp(seg_prefix(gl, g)), 0.0))
        ek.append(jnp.where(right, 0.0, jnp.exp(seg_suffix(gl, g) - gl)))
        g //= 2
    return seg_prefix(gl, c), eq, ek


def _intra(q, k, eq, ek, tt, ss):
    c = q.shape[0]
    qs, ks = [], []
    a = jnp.where(tt == ss, jnp.sum(q * k, axis=1, keepdims=True), 0.0)
    g = c // 2
    for e_q, e_k in zip(eq, ek):
        qg = (q * e_q).astype(BF16)
        kg = (k * e_k).astype(BF16)
        p = lax.dot_general(qg, kg, (((1,), (1,)), ((), ())), preferred_element_type=F32)
        a = a + (p if 2 * g >= c else jnp.where((tt ^ ss) < 2 * g, p, 0.0))
        qs.append(qg)
        ks.append(kg)
        g //= 2
    return a, qs, ks


def _hgrn_fwd(proj, lb, d_model):
    T = proj.shape[0]
    H = d_model // HEAD
    nch = T // CHUNK

    def body(q_ref, f_ref, v_ref, lb_ref, o_ref, s_ref):
        lbv = lb_ref[...]
        row = lax.broadcasted_iota(jnp.int32, (CHUNK, HEAD), 0)
        tt = lax.broadcasted_iota(jnp.int32, (CHUNK, CHUNK), 0)
        ss = lax.broadcasted_iota(jnp.int32, (CHUNK, CHUNK), 1)

        def step(i, st):
            sl = pl.ds(pl.multiple_of(i * CHUNK, CHUNK), CHUNK)
            q, k, gl, _, _, _ = _gates(q_ref[sl, :], f_ref[sl, :], lbv)
            v = v_ref[sl, :].astype(BF16)
            b, eq, ek = _chunk_decays(gl, row)
            a, _, _ = _intra(q, k, eq, ek, tt, ss)
            bl = b[CHUNK - 1:CHUNK, :]
            q0 = (q * jnp.exp(b)).astype(BF16)
            kh = (k * jnp.exp(bl - b)).astype(BF16)
            s_ref[i] = st
            o = jnp.dot(a.astype(BF16), v, preferred_element_type=F32)
            o = o + lax.dot_general(q0, st.astype(BF16), (((1,), (1,)), ((), ())), preferred_element_type=F32)
            o_ref[sl, :] = o
            return jnp.exp(bl) * st + lax.dot_general(v, kh, (((0,), (0,)), ((), ())), preferred_element_type=F32)

        lax.fori_loop(0, nch, step, jnp.zeros((HEAD, HEAD), F32))

    col = lambda off: pl.BlockSpec((T, HEAD), functools.partial(lambda h, off: (0, off + h), off=off))
    return _pcall(
        body, name="hgrn_fwd", grid=(H,),
        in_specs=[col(0), col(H), col(2 * H), pl.BlockSpec((1, HEAD), lambda h: (0, h))],
        out_specs=[pl.BlockSpec((T, HEAD), lambda h: (0, h)),
                   pl.BlockSpec((None, nch, HEAD, HEAD), lambda h: (h, 0, 0, 0))],
        out_shape=[jax.ShapeDtypeStruct((T, d_model), F32), jax.ShapeDtypeStruct((H, nch, HEAD, HEAD), F32)],
        compiler_params=_params(("parallel",)),
    )(proj, proj, proj, lb)


def _hgrn_bwd(proj, lb, states, do, dgate, d_model):
    T = proj.shape[0]
    H = d_model // HEAD
    nch = T // CHUNK

    def body(q_ref, f_ref, v_ref, lb_ref, s_ref, do_ref, dg_ref, dp_ref, dlb_ref):
        dq_ref, df_ref, dv_ref = dp_ref.at[0], dp_ref.at[1], dp_ref.at[2]
        dp_ref[3] = dg_ref[...]
        lbv = lb_ref[...]
        row = lax.broadcasted_iota(jnp.int32, (CHUNK, HEAD), 0)
        tt = lax.broadcasted_iota(jnp.int32, (CHUNK, CHUNK), 0)
        ss = lax.broadcasted_iota(jnp.int32, (CHUNK, CHUNK), 1)
        last = row == CHUNK - 1
        nt = (((1,), (1,)), ((), ()))
        tn = (((0,), (0,)), ((), ()))

        def step(j, carry):
            dst, dlb = carry
            i = nch - 1 - j
            sl = pl.ds(pl.multiple_of(i * CHUNK, CHUNK), CHUNK)
            qr = q_ref[sl, :]
            q, k, gl, f, sg, sq = _gates(qr, f_ref[sl, :], lbv)
            v = v_ref[sl, :].astype(BF16)
            d_o = do_ref[sl, :].astype(BF16)
            st = s_ref[i]
            st16 = st.astype(BF16)
            dst16 = dst.astype(BF16)
            b, eq, ek = _chunk_decays(gl, row)
            a, qs, ks = _intra(q, k, eq, ek, tt, ss)
            bl = b[CHUNK - 1:CHUNK, :]
            e0 = jnp.exp(b)
            eh = jnp.exp(bl - b)
            ebl = jnp.exp(bl)
            q0 = q * e0
            kh = k * eh
            q016 = q0.astype(BF16)
            kh16 = kh.astype(BF16)
            dv = lax.dot_general(a.astype(BF16), d_o, tn, preferred_element_type=F32)
            dv = dv + lax.dot_general(kh16, dst16, nt, preferred_element_type=F32)
            dv_ref[sl, :] = dv.astype(dv_ref.dtype)
            da = lax.dot_general(d_o, v, nt, preferred_element_type=F32)
            da = jnp.where(tt >= ss, da, 0.0)
            dd = jnp.sum(jnp.where(tt == ss, da, 0.0), axis=1, keepdims=True)
            dq0 = jnp.dot(d_o, st16, preferred_element_type=F32)
            dkh = jnp.dot(v, dst16, preferred_element_type=F32)
            dq = dq0 * e0 + dd * k
            dk = dkh * eh + dd * q
            db = dq0 * q016.astype(F32) - dkh * kh16.astype(F32)
            g = CHUNK // 2
            for e_q, e_k, qg, kg in zip(eq, ek, qs, ks):
                dag = (da if 2 * g >= CHUNK else jnp.where((tt ^ ss) < 2 * g, da, 0.0)).astype(BF16)
                dqg = jnp.dot(dag, kg, preferred_element_type=F32)
                dkg = lax.dot_general(dag, qg, tn, preferred_element_type=F32)
                dq = dq + dqg * e_q
                dk = dk + dkg * e_k
                db = db + (dqg * qg.astype(F32) - dkg * kg.astype(F32))
                g //= 2
            dbl = jnp.sum(dkh * kh16.astype(F32), axis=0, keepdims=True) + ebl * jnp.sum(dst * st, axis=0, keepdims=True)
            db = db + jnp.where(last, dbl, 0.0)
            d = 1
            while d < CHUNK:
                db = db + jnp.where(row < CHUNK - d, pltpu.roll(db, CHUNK - d, 0), 0.0)
                d *= 2
            dfg = db / f - dk
            df_ref[sl, :] = (dfg * (1.0 - lbv) * sg * (1.0 - sg)).astype(df_ref.dtype)
            dq_ref[sl, :] = (dq * (HEAD ** -0.5) * (sq * (1.0 + qr * (1.0 - sq)))).astype(dq_ref.dtype)
            dlb = dlb + jnp.sum(dfg * (1.0 - sg), axis=0, keepdims=True)
            dst = ebl * dst + lax.dot_general(d_o, q016, tn, preferred_element_type=F32)
            return dst, dlb

        _, dlb = lax.fori_loop(0, nch, step, (jnp.zeros((HEAD, HEAD), F32), jnp.zeros((1, HEAD), F32)))
        arow = lax.broadcasted_iota(jnp.int32, (SUBLANES, HEAD), 0)
        dlb_ref[...] = jnp.where(arow == 0, dlb, 0.0)

    col = lambda off: pl.BlockSpec((T, HEAD), functools.partial(lambda h, off: (0, off + h), off=off))
    return _pcall(
        body, name="hgrn_bwd", grid=(H,),
        in_specs=[col(0), col(H), col(2 * H), pl.BlockSpec((1, HEAD), lambda h: (0, h)),
                  pl.BlockSpec((None, nch, HEAD, HEAD), lambda h: (h, 0, 0, 0)), col(0), col(0)],
        out_specs=[pl.BlockSpec((4, T, HEAD), lambda h: (0, 0, h)), pl.BlockSpec((SUBLANES, HEAD), lambda h: (0, h))],
        out_shape=[jax.ShapeDtypeStruct((4, T, d_model), BF16), jax.ShapeDtypeStruct((SUBLANES, d_model), F32)],
        compiler_params=_params(("parallel",)),
    )(proj, proj, proj, lb, states, do, dgate)


def _lb_softmax(table):
    n, f = table.shape

    def body(t_ref, p_ref):
        t = t_ref[...]
        e = jnp.exp(t - jnp.max(t, axis=0, keepdims=True))
        p_ref[...] = e / jnp.sum(e, axis=0, keepdims=True)

    padded = jnp.pad(table, ((0, SUBLANES - n), (0, 0)), constant_values=-jnp.inf)
    return _pcall(body, name="lb_softmax", out_shape=jax.ShapeDtypeStruct((SUBLANES, f), F32))(padded)


def _adamw_math(w, g, m, v):
    m = ADAM_B1 * m + (1.0 - ADAM_B1) * g
    v = ADAM_B2 * v + (1.0 - ADAM_B2) * (g * g)
    m_hat = m / (1.0 - ADAM_B1 ** ADAM_STEP)
    v_hat = v / (1.0 - ADAM_B2 ** ADAM_STEP)
    delta = -ADAM_LR * (m_hat / (jnp.sqrt(v_hat) + ADAM_EPS) + ADAM_WD * w)
    return delta, m, v


def _adamw(w, g, m, v, name):
    R, C = w.shape
    tr = _pick(R, (128, 64, 32, 16, 8))

    def body(w_ref, g_ref, m_ref, v_ref, d_ref, nm_ref, nv_ref):
        d, nm, nv = _adamw_math(w_ref[...], g_ref[...], m_ref[...], v_ref[...])
        d_ref[...] = d
        nm_ref[...] = nm
        nv_ref[...] = nv

    spec = pl.BlockSpec((tr, C), lambda i: (i, 0))
    return _pcall(
        body, name=name, grid=(R // tr,), in_specs=[spec] * 4, out_specs=[spec] * 3,
        out_shape=[jax.ShapeDtypeStruct((R, C), F32)] * 3, compiler_params=_params(("parallel",)),
    )(w, g, m, v)


def _adamw_halves(w, m, v, g_mine, g_recv, c, name, layer=0, prev=None):
    C = w.shape[1]
    rh = g_mine.shape[0]
    tr = _pick(rh, (128, 64, 32, 16, 8))
    nb = rh // tr
    r0 = layer * 2 * nb

    def body(c_ref, w_ref, m_ref, v_ref, gm_ref, gr_ref, *rest):
        g_ref, d_ref, nm_ref, nv_ref = rest[-4:]
        g = jnp.where(pl.program_id(0) == c_ref[0], gm_ref[...], gr_ref[...])
        d, nm, nv = _adamw_math(w_ref[...], g, m_ref[...], v_ref[...])
        g_ref[...] = g
        d_ref[...] = d
        nm_ref[...] = nm
        nv_ref[...] = nv

    full = pl.BlockSpec((tr, C), lambda h, i, cr: (r0 + h * nb + i, 0))
    half = pl.BlockSpec((tr, C), lambda h, i, cr: (i, 0))
    in_specs = [full, full, full, half, half]
    args = [c, w, m, v, g_mine, g_recv]
    alias = {}
    if prev is not None:
        in_specs += [pl.BlockSpec(memory_space=pl.ANY)] * 4
        args += list(prev)
        alias = {6 + k: k for k in range(4)}
    return _pcall(
        body, name=name,
        grid_spec=pltpu.PrefetchScalarGridSpec(
            num_scalar_prefetch=1, grid=(2, nb), in_specs=in_specs, out_specs=[full] * 4),
        out_shape=[jax.ShapeDtypeStruct(w.shape, F32)] * 4, input_output_aliases=alias,
        compiler_params=_params(("parallel", "parallel")),
    )(*args)


def _lb_table_grad(p8, dlb, n):
    f = p8.shape[1]

    def body(p_ref, d_ref, o_ref):
        p = p_ref[...]
        d = d_ref[...]
        p0 = p[0:1, :]
        first = lax.broadcasted_iota(jnp.int32, p.shape, 0) == 0
        o_ref[...] = p * (jnp.where(first, d, 0.0) - d * p0)

    return _pcall(body, name="lb_table_grad", out_shape=jax.ShapeDtypeStruct((SUBLANES, f), F32))(p8, dlb)[:n]


def _place():
    x, y, c = lax.axis_index("x"), lax.axis_index("y"), lax.axis_index("c")
    chips = [(1 - x, y), (x, 1 - y), (1 - x, 1 - y)]
    return x, y, c, chips


HBM_SPEC = pl.BlockSpec(memory_space=pltpu.HBM)


def _gather_weights(big, small):
    nb, ns = len(big), len(small)
    n = nb + ns

    def body(*refs):
        ins, outs = refs[:n], refs[n:2 * n]
        send_sems, recv_sems, own_send, own_recv = refs[2 * n:]
        x, y, c, chips = _place()
        me = 2 * x + y
        sib = (x, y, 1 - c)
        own = [pltpu.make_async_remote_copy(
            src_ref=ins[t], dst_ref=outs[t].at[me], send_sem=own_send.at[t], recv_sem=own_recv.at[t],
            device_id=sib, device_id_type=MESH) for t in range(n)]
        for cp in own:
            cp.start()

        def half(t, h):
            rh = big[t].shape[0] // 2
            return pl.ds(pl.multiple_of(h * rh, rh), rh)

        sends = []
        for t in range(n):
            for j, chip in enumerate(chips):
                k = 6 * t + j
                if t < nb:
                    src, dst = ins[t].at[half(t, c)], outs[t].at[me, half(t, c)]
                else:
                    src, dst = ins[t], outs[t].at[me]
                sends.append(pltpu.make_async_remote_copy(
                    src_ref=src, dst_ref=dst, send_sem=send_sems.at[k], recv_sem=recv_sems.at[k],
                    device_id=(*chip, c), device_id_type=MESH))
        for cp in sends:
            cp.start()
        passed = []
        for t in range(n):
            for j, (cx, cy) in enumerate(chips):
                k = 6 * t + j
                s = 2 * cx + cy
                if t < nb:
                    landed = outs[t].at[s, half(t, c)]
                    pltpu.make_async_remote_copy(
                        src_ref=landed, dst_ref=landed, send_sem=send_sems.at[k], recv_sem=recv_sems.at[k],
                        device_id=sib, device_id_type=MESH).wait_recv()
                    fwd = pltpu.make_async_remote_copy(
                        src_ref=landed, dst_ref=landed, send_sem=send_sems.at[k + 3], recv_sem=recv_sems.at[k + 3],
                        device_id=sib, device_id_type=MESH)
                    fwd.start()
                    passed.append(fwd)
                else:
                    landed = outs[t].at[s]
                    pltpu.make_async_remote_copy(
                        src_ref=landed, dst_ref=landed, send_sem=send_sems.at[k], recv_sem=recv_sems.at[k],
                        device_id=sib, device_id_type=MESH).wait_recv()
        for t in range(nb):
            for j, (cx, cy) in enumerate(chips):
                k = 6 * t + j
                other = outs[t].at[2 * cx + cy, half(t, 1 - c)]
                pltpu.make_async_remote_copy(
                    src_ref=other, dst_ref=other, send_sem=send_sems.at[k + 3], recv_sem=recv_sems.at[k + 3],
                    device_id=sib, device_id_type=MESH).wait_recv()
        for cp in sends + passed:
            cp.wait_send()
        for cp in own:
            cp.wait()

    arrs = list(big) + list(small)
    return _pcall(
        body, name="gather_weights", in_specs=[HBM_SPEC] * n, out_specs=[HBM_SPEC] * n,
        out_shape=[jax.ShapeDtypeStruct((N_CHIPS,) + a.shape, a.dtype) for a in arrs],
        scratch_shapes=[pltpu.SemaphoreType.DMA((6 * n,)), pltpu.SemaphoreType.DMA((6 * n,)),
                        pltpu.SemaphoreType.DMA((n,)), pltpu.SemaphoreType.DMA((n,))],
    )(*arrs)


SEM_SPEC = pl.BlockSpec(memory_space=pltpu.SEMAPHORE)
DATAFLOW = pltpu.SideEffectType.DATAFLOW_SIDE_EFFECTING
COPIES_PER_SHARD = 4


def _shard_copies(ins, lands, send_sems, recv_sems):
    x, y, c, chips = _place()
    me = 2 * x + y
    cps = []
    for t in range(len(ins)):
        rh = ins[t].shape[0] // 2
        half = pl.ds(pl.multiple_of(c * rh, rh), rh)
        for j, chip in enumerate(chips):
            k = COPIES_PER_SHARD * t + j
            cps.append(pltpu.make_async_remote_copy(
                src_ref=ins[t].at[half], dst_ref=lands[t].at[me, half], send_sem=send_sems.at[k],
                recv_sem=recv_sems.at[k], device_id=(*chip, c), device_id_type=MESH))
        k = COPIES_PER_SHARD * t + 3
        cps.append(pltpu.make_async_remote_copy(
            src_ref=ins[t], dst_ref=lands[t].at[me], send_sem=send_sems.at[k], recv_sem=recv_sems.at[k],
            device_id=(x, y, 1 - c), device_id_type=MESH))
    return cps


def _gather_start(shards, thru, name):
    n = len(shards)

    def body(*refs):
        ins, lands = refs[:n], refs[n:2 * n]
        send_sems, recv_sems = refs[2 * n + 1], refs[2 * n + 2]
        for cp in _shard_copies(ins, lands, send_sems, recv_sems):
            cp.start()

    lands = [pltpu.with_memory_space_constraint(lax.empty((N_CHIPS,) + s.shape, s.dtype), pltpu.HBM) for s in shards]
    ops = [pltpu.with_memory_space_constraint(s, pltpu.HBM) for s in shards] + lands + [thru]
    nsem = COPIES_PER_SHARD * n
    res = _pcall(
        body, name=name, in_specs=[HBM_SPEC] * (2 * n + 1),
        out_specs=[SEM_SPEC, SEM_SPEC] + [HBM_SPEC] * (2 * n + 1),
        out_shape=[pltpu.SemaphoreType.DMA((nsem,)), pltpu.SemaphoreType.DMA((nsem,))]
        + [pltpu.HBM(o.shape, o.dtype) for o in ops],
        input_output_aliases={i: 2 + i for i in range(2 * n + 1)},
        compiler_params=pltpu.CompilerParams(has_side_effects=DATAFLOW),
    )(*ops)
    return res[0], res[1], res[2:2 + n], res[2 + n:2 + 2 * n], res[2 + 2 * n]


def _gather_wait(send_sems, recv_sems, shards, lands, after, name):
    n = len(shards)

    def body(*refs):
        ins, lnd = refs[:n], refs[n:2 * n]
        ssem, rsem = refs[2 * n], refs[2 * n + 1]
        for cp in _shard_copies(ins, lnd, ssem, rsem):
            cp.wait_send()
            cp.wait_recv()

    res = _pcall(
        body, name=name,
        in_specs=[HBM_SPEC] * (2 * n) + [SEM_SPEC, SEM_SPEC, pl.BlockSpec(memory_space=pl.ANY)],
        out_specs=[HBM_SPEC] * (2 * n),
        out_shape=[pltpu.HBM(o.shape, o.dtype) for o in list(shards) + list(lands)],
        input_output_aliases={i: i for i in range(2 * n)},
        compiler_params=pltpu.CompilerParams(has_side_effects=DATAFLOW),
    )(*shards, *lands, send_sems, recv_sems, after)
    return res[n:]


def _gather_forward(lands, name):
    n = len(lands)

    def body(*refs):
        outs = refs[n:2 * n]
        send_sems, recv_sems = refs[2 * n:]
        x, y, c, chips = _place()
        sib = (x, y, 1 - c)
        cps = []
        for t in range(n):
            rh = lands[t].shape[1] // 2
            for j, (cx, cy) in enumerate(chips):
                mine = outs[t].at[2 * cx + cy, pl.ds(pl.multiple_of(c * rh, rh), rh)]
                cps.append(pltpu.make_async_remote_copy(
                    src_ref=mine, dst_ref=mine, send_sem=send_sems.at[3 * t + j], recv_sem=recv_sems.at[3 * t + j],
                    device_id=sib, device_id_type=MESH))
        for cp in cps:
            cp.start()
        for t in range(n):
            rh = lands[t].shape[1] // 2
            for j, (cx, cy) in enumerate(chips):
                theirs = outs[t].at[2 * cx + cy, pl.ds(pl.multiple_of((1 - c) * rh, rh), rh)]
                pltpu.make_async_remote_copy(
                    src_ref=theirs, dst_ref=theirs, send_sem=send_sems.at[3 * t + j], recv_sem=recv_sems.at[3 * t + j],
                    device_id=sib, device_id_type=MESH).wait_recv()
        for cp in cps:
            cp.wait_send()

    return _pcall(
        body, name=name, in_specs=[HBM_SPEC] * n, out_specs=[HBM_SPEC] * n,
        out_shape=[jax.ShapeDtypeStruct(a.shape, a.dtype) for a in lands],
        input_output_aliases={i: i for i in range(n)},
        scratch_shapes=[pltpu.SemaphoreType.DMA((3 * n,)), pltpu.SemaphoreType.DMA((3 * n,))],
    )(*lands)


def _pair_exchange(grads):
    n = len(grads)

    def body(*refs):
        ins, outs = refs[:n], refs[n:2 * n]
        send_sems, recv_sems = refs[2 * n:]
        x, y, c, _ = _place()
        cps = [pltpu.make_async_remote_copy(
            src_ref=ins[t].at[:, 1 - c], dst_ref=outs[t], send_sem=send_sems.at[t], recv_sem=recv_sems.at[t],
            device_id=(x, y, 1 - c), device_id_type=MESH) for t in range(n)]
        for cp in cps:
            cp.start()
        for cp in cps:
            cp.wait()

    return _pcall(
        body, name="grad_pair_exchange", in_specs=[HBM_SPEC] * n, out_specs=[HBM_SPEC] * n,
        out_shape=[jax.ShapeDtypeStruct((g.shape[0],) + g.shape[2:], g.dtype) for g in grads],
        scratch_shapes=[pltpu.SemaphoreType.DMA((n,)), pltpu.SemaphoreType.DMA((n,))],
    )(*grads)


def _chip_exchange(parts):
    n = len(parts)

    def body(*refs):
        ins, outs = refs[:n], refs[n:2 * n]
        send_sems, recv_sems = refs[2 * n:]
        x, y, c, chips = _place()
        cps = []
        for t in range(n):
            for j, (cx, cy) in enumerate(chips):
                cps.append(pltpu.make_async_remote_copy(
                    src_ref=ins[t].at[2 * cx + cy], dst_ref=outs[t].at[j],
                    send_sem=send_sems.at[3 * t + j], recv_sem=recv_sems.at[3 * t + j],
                    device_id=(cx, cy, c), device_id_type=MESH))
        for cp in cps:
            cp.start()
        for cp in cps:
            cp.wait()

    return _pcall(
        body, name="grad_chip_exchange", in_specs=[HBM_SPEC] * n, out_specs=[HBM_SPEC] * n,
        out_shape=[jax.ShapeDtypeStruct((3,) + p.shape[1:], p.dtype) for p in parts],
        scratch_shapes=[pltpu.SemaphoreType.DMA((3 * n,)), pltpu.SemaphoreType.DMA((3 * n,))],
    )(*parts)


def _pair_share(halves):
    n = len(halves)

    def body(*refs):
        ins, outs = refs[:n], refs[n:2 * n]
        send_sems, recv_sems = refs[2 * n:]
        x, y, c, _ = _place()
        cps = [pltpu.make_async_remote_copy(
            src_ref=ins[t], dst_ref=outs[t], send_sem=send_sems.at[t], recv_sem=recv_sems.at[t],
            device_id=(x, y, 1 - c), device_id_type=MESH) for t in range(n)]
        for cp in cps:
            cp.start()
        for cp in cps:
            cp.wait()

    return _pcall(
        body, name="grad_pair_share", in_specs=[HBM_SPEC] * n, out_specs=[HBM_SPEC] * n,
        out_shape=[jax.ShapeDtypeStruct(h.shape, h.dtype) for h in halves],
        scratch_shapes=[pltpu.SemaphoreType.DMA((n,)), pltpu.SemaphoreType.DMA((n,))],
    )(*halves)


def _add_pair(grad, recv, c, name):
    s, _, rh, cc = grad.shape
    tr = _pick(rh, (256, 128, 64, 32, 16))

    def body(c_ref, g_ref, r_ref, o_ref):
        o_ref[...] = (g_ref[...].astype(F32) + r_ref[...].astype(F32)).astype(o_ref.dtype)

    return _pcall(
        body, name=name,
        grid_spec=pltpu.PrefetchScalarGridSpec(
            num_scalar_prefetch=1, grid=(s, rh // tr),
            in_specs=[pl.BlockSpec((None, None, tr, cc), lambda a, i, cr: (a, cr[0], i, 0)),
                      pl.BlockSpec((None, tr, cc), lambda a, i, cr: (a, i, 0))],
            out_specs=pl.BlockSpec((None, tr, cc), lambda a, i, cr: (a, i, 0))),
        out_shape=jax.ShapeDtypeStruct((s, rh, cc), BF16),
        compiler_params=_params(("parallel", "parallel")),
    )(c, grad, recv)


def _add_chips(part, recv, me, name):
    _, rh, cc = part.shape
    tr = _pick(rh, (256, 128, 64, 32, 16))

    def body(m_ref, p_ref, r_ref, o_ref):
        o_ref[...] = ((p_ref[...].astype(F32) + r_ref[0].astype(F32)) + r_ref[1].astype(F32)) + r_ref[2].astype(F32)

    return _pcall(
        body, name=name,
        grid_spec=pltpu.PrefetchScalarGridSpec(
            num_scalar_prefetch=1, grid=(rh // tr,),
            in_specs=[pl.BlockSpec((None, tr, cc), lambda i, mr: (mr[0], i, 0)),
                      pl.BlockSpec((3, tr, cc), lambda i, mr: (0, i, 0))],
            out_specs=pl.BlockSpec((tr, cc), lambda i, mr: (i, 0))),
        out_shape=jax.ShapeDtypeStruct((rh, cc), F32),
        compiler_params=_params(("parallel",)),
    )(me, part, recv)


def _all_sum(vec):
    rows = vec.shape[0]

    def body(v_ref, o_ref, buf, send_sems, recv_sems):
        x, y, c, _ = _place()
        me = 4 * x + 2 * y + c
        buf[me] = v_ref[...]
        cps = []
        for r in range(1, 8):
            fx, fy, fc = (r >> 2) & 1, (r >> 1) & 1, r & 1
            peer = (x ^ fx, y ^ fy, c ^ fc)
            cps.append(pltpu.make_async_remote_copy(
                src_ref=v_ref, dst_ref=buf.at[me], send_sem=send_sems.at[r - 1], recv_sem=recv_sems.at[r - 1],
                device_id=peer, device_id_type=MESH))
        for cp in cps:
            cp.start()
        for r in range(1, 8):
            src = me ^ r
            pltpu.make_async_remote_copy(
                src_ref=v_ref, dst_ref=buf.at[src], send_sem=send_sems.at[r - 1], recv_sem=recv_sems.at[r - 1],
                device_id=(x, y, c), device_id_type=MESH).wait_recv()
        for cp in cps:
            cp.wait_send()
        acc = buf[0]
        for d in range(1, 8):
            acc = acc + buf[d]
        o_ref[...] = acc

    return _pcall(
        body, name="all_sum_small",
        in_specs=[pl.BlockSpec(memory_space=pltpu.VMEM)], out_specs=pl.BlockSpec(memory_space=pltpu.VMEM),
        out_shape=jax.ShapeDtypeStruct((rows, LANES), F32),
        scratch_shapes=[pltpu.VMEM((8, rows, LANES), F32), pltpu.SemaphoreType.DMA((7,)), pltpu.SemaphoreType.DMA((7,))],
    )(vec)


def _pack(parts):
    flat = jnp.concatenate([p.reshape(-1) for p in parts])
    tile = SUBLANES * LANES
    pad = (-flat.shape[0]) % tile
    return jnp.pad(flat, (0, pad)).reshape(-1, LANES)


def _unpack(vec, shapes):
    flat = vec.reshape(-1)
    out, p = [], 0
    for s in shapes:
        n = 1
        for d in s:
            n *= d
        out.append(flat[p:p + n].reshape(s))
        p += n
    return out


def _local_step(x, tgt, norm_mix, norm_ffn, lb8, out_norm, final_norm, sc_conv, ffn_conv, weights):
    T, D = x.shape
    F2 = ffn_conv.shape[-1]
    FF = F2 // 2
    tm = _pick(T, (1024, 512, 256, 128))
    wide = (1536, 1408, 1024, 768, 512, 384, 256, 128)
    cw_h, cw_s, cw_u = 4 * D // N_CHIPS, 3 * D // N_CHIPS, F2 // N_CHIPS
    kp = FF // N_CHIPS
    tk_ff = kp if kp % LANES == 0 else LANES
    tn_d = _pick(D, (1024, 512, 256, 128))
    tk_w = _pick(D, (512, 256, 128))
    tn_h = _pick(cw_h, (1024, 512, 256, 128))
    tn_s = _pick(D // N_CHIPS, (512, 256, 128))
    tn_u = _pick(cw_u, wide)
    lb = lb8[0:1]
    wm_sq = _wmap_col(D, tn_d, 0)
    wm_sq1 = _wmap_col(D, D, 0)
    seg1 = lambda a: a.reshape((1,) + a.shape)

    def mix_in(h, w):
        return _row_call(_rms_fwd_fn, [(h, 0, D)], [w], [(D, BF16)], 0, "rms_fwd")[0]

    def rms_bwd(h, dxn, dh, w):
        return _row_call(_rms_bwd_fn, [(h, 0, D), (dxn, 0, D), (dh, 0, D)], [w], [(D, F32), (D, BF16)], 1, "rms_bwd")

    def ffn_fwd(h, i, w_up, w_down):
        xn = mix_in(h, norm_ffn[i:i + 1])
        tn = _pick(cw_u, wide)
        up = _mm_nn(xn, w_up, _wmap_col(cw_u, tn, 0), D, F2, tm, D, tn, "ffn_up")
        nb = FF // LANES
        a = _col_call(_glu_fwd_fn, [(up, 0), (up, nb)], [(ffn_conv[i], 0), (ffn_conv[i], nb)], [(1, FF, BF16)], 0,
                      "glu_fwd", before=True, after=False)[0][0]
        h2 = _mm_nn(a, w_down, _wmap_row(kp, tk_ff, 0), FF, D, tm, tk_ff, tn_d, "ffn_down", res=h)
        return h2, (xn, up, a)

    def ffn_bwd(dh, dh16, h, saved, i, w_up, w_down):
        xn, up, a = saved
        da = _mm_nt(seg1(dh16), w_down, _wmap_row(kp, tk_ff, 0), FF, D, tm, tk_ff, D, "ffn_down_dx")
        nb = FF // LANES
        dgv, cg, cv = _col_call(_glu_bwd_fn, [(up, 0), (up, nb), (da, 0)], [(ffn_conv[i], 0), (ffn_conv[i], nb)],
                                [(2, FF, BF16)], 2, "glu_bwd", before=True, after=True)
        dxn = _mm_nt(dgv, w_up, _wmap_col(cw_u, tn_u, 0), D, F2, tm, D, tn_u, "ffn_up_dx")
        g_down = _mm_tn(a, seg1(dh16), (N_CHIPS, 1, kp, D), _wmap_row(kp, tk_ff, 0), FF, D, tk_ff, tn_d,
                        "ffn_down_dw", tm=_pick(T, (2048, 1024, 512, 256, 128)))
        g_up = _mm_tn(xn, dgv, (N_CHIPS, 1, D, cw_u), _wmap_col(cw_u, tn_u, 0), D, F2, tk_w, tn_u, "ffn_up_dw")
        dh2, dh2_16, dnw = rms_bwd(h, dxn, dh, norm_ffn[i:i + 1])
        return dh2, dh2_16, dnw, jnp.concatenate([cg[:3], cv[:3]], axis=1), g_up, g_down

    h0 = x
    xn0 = mix_in(h0, norm_mix[0:1])
    w_hin, = weights(0, xn0)
    proj = _mm_nn(xn0, w_hin, _wmap_col(cw_h, tn_h, 0), D, 4 * D, tm, D, tn_h, "hgrn_in")
    o, states = _hgrn_fwd(proj, lb, D)
    on = _row_call(_onorm_fwd_fn, [(o, 0, D), (proj, 3, D)], [out_norm], [(D, BF16)], 0, "onorm_fwd")[0]
    w_hout, w_up0, w_down0 = weights(1, on)
    w_hout1 = w_hout.reshape(1, D, D)
    h1 = _mm_nn(on, w_hout1, wm_sq, D, D, tm, D, tn_d, "hgrn_out", res=h0)
    h2, ffn0 = ffn_fwd(h1, 0, w_up0, w_down0)
    xn1 = mix_in(h2, norm_mix[1:2])
    w_sin, w_sout, w_up1, w_down1 = weights(2, xn1)
    w_sout1 = w_sout.reshape(1, D, D)
    tn_si = _pick(cw_s, wide)
    sproj = _mm_nn(xn1, w_sin, _wmap_col(cw_s, tn_si, 0), D, 3 * D, tm, D, tn_si, "sc_in")
    nd = D // LANES
    ysc = _col_call(_sc_fwd_fn, [(sproj, 0), (sproj, nd), (sproj, 2 * nd)], [(sc_conv, 0)], [(1, D, BF16)], 0,
                    "sc_fwd", before=True, after=False)[0][0]
    h3 = _mm_nn(ysc, w_sout1, wm_sq, D, D, tm, D, tn_d, "sc_out", res=h2)
    h4, ffn1 = ffn_fwd(h3, 1, w_up1, w_down1)

    dh, dh16, esq, dfinal = _row_call(_final_fn, [(h4, 0, D), (tgt, 0, D)], [final_norm], [(D, F32), (D, BF16)], 2,
                                      "final_loss")
    loss = 0.5 / D * jnp.sum(esq)
    dh, dh16, dnf1, dconv1, g_up1, g_down1 = ffn_bwd(dh, dh16, h3, ffn1, 1, w_up1, w_down1)
    dy = _mm_nt(seg1(dh16), w_sout1, wm_sq1, D, D, tm, D, D, "sc_out_dx")
    dsp, dscc = _col_call(_sc_bwd_fn, [(sproj, 0), (sproj, nd), (sproj, 2 * nd), (dy, 0)], [(sc_conv, 0)],
                          [(3, D, BF16)], 1, "sc_bwd", before=True, after=True)
    dxn = _mm_nt(dsp, w_sin, _wmap_col(cw_s, tn_s, 0), D, 3 * D, tm, D, tn_s, "sc_in_dx")
    g_sout = _mm_tn(ysc, seg1(dh16), (1, 1, D, D), wm_sq, D, D, tk_w, tn_d, "sc_out_dw")
    g_sin = _mm_tn(xn1, dsp, (N_CHIPS, 1, D, cw_s), _wmap_col(cw_s, tn_s, 0), D, 3 * D, tk_w, tn_s, "sc_in_dw")
    dh, dh16, dnm1 = rms_bwd(h2, dxn, dh, norm_mix[1:2])
    dh, dh16, dnf0, dconv0, g_up0, g_down0 = ffn_bwd(dh, dh16, h1, ffn0, 0, w_up0, w_down0)
    don = _mm_nt(seg1(dh16), w_hout1, wm_sq1, D, D, tm, D, D, "hgrn_out_dx")
    do, dgate, dgain = _row_call(_onorm_bwd_fn, [(o, 0, D), (proj, 3, D), (don, 0, D)], [out_norm],
                                 [(D, F32), (D, BF16)], 1, "onorm_bwd")
    dproj, dlb = _hgrn_bwd(proj, lb, states, do, dgate, D)
    dxn = _mm_nt(dproj, w_hin, _wmap_col(cw_h, tn_h, 0), D, 4 * D, tm, D, tn_h, "hgrn_in_dx")
    g_hout = _mm_tn(on, seg1(dh16), (1, 1, D, D), wm_sq, D, D, tk_w, tn_d, "hgrn_out_dw")
    g_hin = _mm_tn(xn0, dproj, (N_CHIPS, 1, D, cw_h), _wmap_col(cw_h, tn_h, 0), D, 4 * D, tk_w, tn_h, "hgrn_in_dw")
    grad_x, _, dnm0 = rms_bwd(h0, dxn, dh, norm_mix[0:1])

    small = dict(
        loss=loss,
        norm_mix=jnp.stack([jnp.sum(dnm0, axis=0), jnp.sum(dnm1, axis=0)]),
        norm_ffn=jnp.stack([jnp.sum(dnf0, axis=0), jnp.sum(dnf1, axis=0)]),
        lb=dlb[0:1],
        out_norm=jnp.sum(dgain, axis=0)[None],
        final_norm=jnp.sum(dfinal, axis=0),
        sc_conv=dscc[:3],
        ffn_conv=jnp.stack([dconv0, dconv1]),
    )
    return grad_x, small, (g_hin, g_hout, g_sin, g_sout, g_up0, g_down0, g_up1, g_down1)


def kernel(x, norm_mix, norm_ffn, hgrn_w_in, hgrn_lb_table, hgrn_out_norm, hgrn_w_out, sc_w_in, sc_conv, sc_w_out, ffn_w_up, ffn_conv, ffn_w_down, final_norm, loss_target, m_norm_mix, m_norm_ffn, m_hgrn_w_in, m_hgrn_lb_table, m_hgrn_out_norm, m_hgrn_w_out, m_sc_w_in, m_sc_conv, m_sc_w_out, m_ffn_w_up, m_ffn_conv, m_ffn_w_down, m_final_norm, v_norm_mix, v_norm_ffn, v_hgrn_w_in, v_hgrn_lb_table, v_hgrn_out_norm, v_hgrn_w_out, v_sc_w_in, v_sc_conv, v_sc_w_out, v_ffn_w_up, v_ffn_conv, v_ffn_w_down, v_final_norm):
    D = x.shape[-1]
    xi, yi, ci = lax.axis_index("x"), lax.axis_index("y"), lax.axis_index("c")
    me_chip = (2 * xi + yi).astype(jnp.int32).reshape(1)
    me_core = ci.astype(jnp.int32).reshape(1)

    big_names = ["hgrn_w_in", "hgrn_w_out", "sc_w_in", "sc_w_out", "ffn_w_up", "ffn_w_down"]
    big_w = dict(hgrn_w_in=hgrn_w_in, hgrn_w_out=hgrn_w_out, sc_w_in=sc_w_in, sc_w_out=sc_w_out,
                 ffn_w_up=ffn_w_up, ffn_w_down=ffn_w_down)
    big_m = dict(hgrn_w_in=m_hgrn_w_in, hgrn_w_out=m_hgrn_w_out, sc_w_in=m_sc_w_in, sc_w_out=m_sc_w_out,
                 ffn_w_up=m_ffn_w_up, ffn_w_down=m_ffn_w_down)
    big_v = dict(hgrn_w_in=v_hgrn_w_in, hgrn_w_out=v_hgrn_w_out, sc_w_in=v_sc_w_in, sc_w_out=v_sc_w_out,
                 ffn_w_up=v_ffn_w_up, ffn_w_down=v_ffn_w_down)
    flat2 = lambda a: a.reshape(-1, a.shape[-1])

    sh = lambda a: a.reshape(-1, a.shape[-1]).astype(BF16)
    conv_shards = [flat2(sc_conv), flat2(ffn_conv)]
    w_hin, scc4, fcc4 = _gather_weights([sh(hgrn_w_in)], conv_shards)
    scc = jnp.moveaxis(scc4, 0, 1).reshape(3, D)
    f2 = ffn_conv.shape[-1] * N_CHIPS
    fcc = jnp.moveaxis(fcc4.reshape(N_CHIPS, 2, 3, -1), 0, 2).reshape(2, 3, f2)
    stage1 = [sh(hgrn_w_out), sh(ffn_w_up[0]), sh(ffn_w_down[0])]
    stage2 = [sh(sc_w_in), sh(sc_w_out), sh(ffn_w_up[1]), sh(ffn_w_down[1])]
    ss1, rs1, src1, land1, w_hin = _gather_start(stage1, w_hin, "gather_start_1")
    ss2, rs2, src2, land2, w_hin = _gather_start(stage2, w_hin, "gather_start_2")

    def weights(stage, after):
        if stage == 0:
            return (w_hin,)
        if stage == 1:
            return _gather_forward(_gather_wait(ss1, rs1, src1, land1, after, "gather_wait_1"), "gather_forward_1")
        return _gather_forward(_gather_wait(ss2, rs2, src2, land2, after, "gather_wait_2"), "gather_forward_2")

    lb8 = _lb_softmax(hgrn_lb_table)
    grad_x, small, bigs = _local_step(
        x[0], loss_target[0], norm_mix, norm_ffn, lb8, hgrn_out_norm, final_norm[None], scc, fcc, weights)

    small_names = ["loss", "norm_mix", "norm_ffn", "lb", "out_norm", "final_norm", "sc_conv", "ffn_conv"]
    parts = [small[n].astype(F32) for n in small_names]
    shapes = [p.shape for p in parts]
    tot = dict(zip(small_names, _unpack(_all_sum(_pack(parts)), shapes)))
    loss = tot["loss"].reshape(())
    g_lb_table = _lb_table_grad(lb8, tot["lb"], hgrn_lb_table.shape[0])
    cw = sc_conv.shape[-1]
    g_sc_conv = lax.dynamic_slice_in_dim(tot["sc_conv"], me_chip[0] * cw, cw, axis=1)[None]
    cf = ffn_conv.shape[-1]
    g_ffn_conv = lax.dynamic_slice_in_dim(tot["ffn_conv"], me_chip[0] * cf, cf, axis=2)
    g_small = dict(norm_mix=tot["norm_mix"], norm_ffn=tot["norm_ffn"], hgrn_lb_table=g_lb_table,
                   hgrn_out_norm=tot["out_norm"], sc_conv=g_sc_conv, ffn_conv=g_ffn_conv, final_norm=tot["final_norm"])
    w_small = dict(norm_mix=norm_mix, norm_ffn=norm_ffn, hgrn_lb_table=hgrn_lb_table, hgrn_out_norm=hgrn_out_norm,
                   sc_conv=sc_conv, ffn_conv=ffn_conv, final_norm=final_norm)
    m_small = dict(norm_mix=m_norm_mix, norm_ffn=m_norm_ffn, hgrn_lb_table=m_hgrn_lb_table, hgrn_out_norm=m_hgrn_out_norm,
                   sc_conv=m_sc_conv, ffn_conv=m_ffn_conv, final_norm=m_final_norm)
    v_small = dict(norm_mix=v_norm_mix, norm_ffn=v_norm_ffn, hgrn_lb_table=v_hgrn_lb_table, hgrn_out_norm=v_hgrn_out_norm,
                   sc_conv=v_sc_conv, ffn_conv=v_ffn_conv, final_norm=v_final_norm)
    sm_names = list(g_small)
    sm_shapes = [w_small[n].shape for n in sm_names]
    d_s, m_s, v_s = _adamw(_pack([w_small[n] for n in sm_names]), _pack([g_small[n] for n in sm_names]),
                           _pack([m_small[n] for n in sm_names]), _pack([v_small[n] for n in sm_names]), "adamw_small")
    out_g, out_d, out_m, out_v = dict(g_small), {}, {}, {}
    for n, d_, m_, v_ in zip(sm_names, _unpack(d_s, sm_shapes), _unpack(m_s, sm_shapes), _unpack(v_s, sm_shapes)):
        out_d[n], out_m[n], out_v[n] = d_, m_, v_

    halves = [g.reshape(N_CHIPS, 2, -1, g.shape[-1]) for g in bigs]
    recv1 = _pair_exchange(halves)
    pair = [_add_pair(g, r, me_core, "grad_add_pair") for g, r in zip(halves, recv1)]
    recv2 = _chip_exchange(pair)
    mine = [_add_chips(p, r, me_chip, "grad_add_chips") for p, r in zip(pair, recv2)]
    theirs = _pair_share(mine)
    slots = [("hgrn_w_in", 0), ("hgrn_w_out", 0), ("sc_w_in", 0), ("sc_w_out", 0),
             ("ffn_w_up", 0), ("ffn_w_down", 0), ("ffn_w_up", 1), ("ffn_w_down", 1)]
    done = {}
    for (n, layer), gm, gr in zip(slots, mine, theirs):
        w = big_w[n]
        done[n] = _adamw_halves(flat2(w), flat2(big_m[n]), flat2(big_v[n]), gm, gr, me_core, "adamw_" + n,
                                layer=layer, prev=done.get(n))
    for n in big_names:
        out_g[n], out_d[n], out_m[n], out_v[n] = (a.reshape(big_w[n].shape) for a in done[n])

    order = ["norm_mix", "norm_ffn", "hgrn_w_in", "hgrn_lb_table", "hgrn_out_norm", "hgrn_w_out", "sc_w_in", "sc_conv",
             "sc_w_out", "ffn_w_up", "ffn_conv", "ffn_w_down", "final_norm"]
    return (loss, grad_x[None], *[out_g[n] for n in order], *[out_d[n] for n in order],
            *[out_m[n] for n in order], *[out_v[n] for n in order])
```

```python
import functools

import jax
import jax.numpy as jnp
from jax import lax
from jax.experimental import pallas as pl
from jax.experimental.pallas import tpu as pltpu

F32 = jnp.float32
BF16 = jnp.bfloat16
MESH = pl.DeviceIdType.MESH

EPS = 1e-6
CHUNK = 64
HEAD = 128
N_CHIPS = 4
ADAM_LR, ADAM_B1, ADAM_B2, ADAM_EPS, ADAM_WD, ADAM_STEP = 0.001, 0.9, 0.999, 1e-08, 0.01, 10
VMEM_LIMIT = 56 * 1024 * 1024
SUBLANES = 8
LANES = 128


def _pcall(body, **kw):
    return pl.pallas_call(body, **kw)


def _params(sem, vmem=VMEM_LIMIT):
    return pltpu.CompilerParams(dimension_semantics=sem, vmem_limit_bytes=vmem)


def _pick(dim, prefs):
    for p in prefs:
        if p <= dim and dim % p == 0:
            return p
    return dim


def _sigmoid(x):
    return 1.0 / (1.0 + jnp.exp(-x))


def _wmap_col(cw, tn, r0):
    bps = cw // tn
    return lambda kb, nb: (nb // bps, r0 + kb, nb % bps)


def _wmap_row(kp, tk, r0):
    bps = kp // tk
    return lambda kb, nb: (kb // bps, r0 + kb % bps, nb)


def _mm_nn(a, w3, wmap, K, N, tm, tk, tn, name, res=None):
    M = a.shape[0]
    nk = K // tk

    def body(*refs):
        if res is None:
            a_ref, w_ref, o_ref = refs[:3]
        else:
            a_ref, w_ref, r_ref, o_ref = refs[:4]
        p = jnp.dot(a_ref[...], w_ref[...], preferred_element_type=F32)
        if nk == 1:
            o_ref[...] = p if res is None else p + r_ref[...]
            return
        acc = refs[-1]
        k = pl.program_id(2)

        @pl.when(k == 0)
        def _():
            acc[...] = p

        @pl.when(k > 0)
        def _():
            acc[...] += p

        @pl.when(k == nk - 1)
        def _():
            o_ref[...] = acc[...] if res is None else acc[...] + r_ref[...]

    if nk == 1:
        grid = (M // tm, N // tn)
        ix = lambda f: (lambda i, j: f(i, j, 0))
        sem = ("parallel", "parallel")
        scratch = []
    else:
        grid = (M // tm, N // tn, nk)
        ix = lambda f: f
        sem = ("parallel", "parallel", "arbitrary")
        scratch = [pltpu.VMEM((tm, tn), F32)]
    in_specs = [pl.BlockSpec((tm, tk), ix(lambda i, j, k: (i, k))),
                pl.BlockSpec((None, tk, tn), ix(lambda i, j, k: wmap(k, j)))]
    args = [a, w3]
    if res is not None:
        in_specs.append(pl.BlockSpec((tm, tn), ix(lambda i, j, k: (i, j))))
        args.append(res)
    return _pcall(
        body, name=name, grid=grid, in_specs=in_specs,
        out_specs=pl.BlockSpec((tm, tn), ix(lambda i, j, k: (i, j))),
        out_shape=jax.ShapeDtypeStruct((M, N), F32), scratch_shapes=scratch, compiler_params=_params(sem),
    )(*args)


def _mm_nt(dy3, w3, wmap, K, N, tm, tk, tn, name):
    M = dy3.shape[1]
    bps = dy3.shape[2] // tn
    grid = (M // tm, K // tk, N // tn)
    nn = grid[2]

    def body(dy_ref, w_ref, o_ref):
        p = lax.dot_general(dy_ref[...], w_ref[...], (((1,), (1,)), ((), ())), preferred_element_type=F32)
        if nn == 1:
            o_ref[...] = p
            return
        n = pl.program_id(2)

        @pl.when(n == 0)
        def _():
            o_ref[...] = p

        @pl.when(n > 0)
        def _():
            o_ref[...] += p

    return _pcall(
        body, name=name, grid=grid,
        in_specs=[pl.BlockSpec((None, tm, tn), lambda i, j, n: (n // bps, i, n % bps)),
                  pl.BlockSpec((None, tk, tn), lambda i, j, n: wmap(j, n))],
        out_specs=pl.BlockSpec((tm, tk), lambda i, j, n: (i, j)),
        out_shape=jax.ShapeDtypeStruct((M, K), F32),
        compiler_params=_params(("parallel", "parallel", "arbitrary")),
    )(dy3, w3)


def _mm_tn(x, dy3, shape4, wmap, K, N, tk, tn, name, tm=None):
    M = x.shape[0]
    tm = M if tm is None else tm
    nm = M // tm
    bps = dy3.shape[2] // tn

    def body(*refs):
        x_ref, dy_ref = refs[:2]
        p = lax.dot_general(x_ref[...], dy_ref[...], (((0,), (0,)), ((), ())), preferred_element_type=F32)
        if nm == 1:
            o_ref = refs[-1]
            o_ref[...] = p.astype(o_ref.dtype)
            return
        o_ref, acc = refs[-2:]
        m = pl.program_id(2)

        @pl.when(m == 0)
        def _():
            acc[...] = p

        @pl.when(m > 0)
        def _():
            acc[...] += p

        @pl.when(m == nm - 1)
        def _():
            o_ref[...] = acc[...].astype(o_ref.dtype)

    def omap(i, j, m):
        s, rb, cb = wmap(i, j)
        return (s, 0, rb, cb)

    return _pcall(
        body, name=name, grid=(K // tk, N // tn, nm),
        in_specs=[pl.BlockSpec((tm, tk), lambda i, j, m: (m, i)),
                  pl.BlockSpec((None, tm, tn), lambda i, j, m: (j // bps, m, j % bps))],
        out_specs=pl.BlockSpec((None, None, tk, tn), omap),
        out_shape=jax.ShapeDtypeStruct(shape4, BF16),
        scratch_shapes=[] if nm == 1 else [pltpu.VMEM((tk, tn), F32)],
        compiler_params=_params(("parallel", "parallel", "arbitrary")),
    )(x, dy3)


def _row_call(fn, rows, vecs, outs, n_acc, name, t_rows=256, sub=16):
    T = rows[0][0].shape[0]
    t_rows = min(t_rows, T)
    nsub = t_rows // sub
    n_r, n_v, n_o = len(rows), len(vecs), len(outs)
    width = rows[0][2]

    def body(*refs):
        r_refs = refs[:n_r]
        v_refs = refs[n_r:n_r + n_v]
        o_refs = refs[n_r + n_v:n_r + n_v + n_o]
        a_refs = refs[n_r + n_v + n_o:]

        @pl.when(pl.program_id(0) == 0)
        def _():
            for a in a_refs:
                a[...] = jnp.zeros_like(a)

        vv = [v[...] for v in v_refs]

        def step(i, carry):
            sl = pl.ds(pl.multiple_of(i * sub, sub), sub)
            o_vals, a_vals = fn([r[sl, :] for r in r_refs], vv)
            for o, val in zip(o_refs, o_vals):
                o[sl, :] = val.astype(o.dtype)
            for a, val in zip(a_refs, a_vals):
                a[...] += val.reshape(sub // SUBLANES, SUBLANES, val.shape[-1]).sum(axis=0)
            return carry

        lax.fori_loop(0, nsub, step, 0)

    in_specs = [pl.BlockSpec((t_rows, w), functools.partial(lambda i, cb: (i, cb), cb=cb)) for _, cb, w in rows]
    in_specs += [pl.BlockSpec(v.shape, lambda i: (0, 0)) for v in vecs]
    out_specs = [pl.BlockSpec((t_rows, w), lambda i: (i, 0)) for w, _ in outs]
    out_specs += [pl.BlockSpec((SUBLANES, width), lambda i: (0, 0)) for _ in range(n_acc)]
    out_shape = [jax.ShapeDtypeStruct((T, w), dt) for w, dt in outs]
    out_shape += [jax.ShapeDtypeStruct((SUBLANES, width), F32) for _ in range(n_acc)]
    return _pcall(
        body, name=name, grid=(T // t_rows,), in_specs=in_specs, out_specs=out_specs, out_shape=out_shape,
        compiler_params=_params(("arbitrary",)),
    )(*[r[0] for r in rows], *vecs)


def _rms_fwd_fn(rv, vv):
    h, = rv
    w, = vv
    r = lax.rsqrt(jnp.mean(h * h, axis=-1, keepdims=True) + EPS)
    return [h * r * w], []


def _rms_bwd_fn(rv, vv):
    h, dxn, dh_in = rv
    w, = vv
    d = h.shape[-1]
    r = lax.rsqrt(jnp.mean(h * h, axis=-1, keepdims=True) + EPS)
    gy = dxn * w
    dh = r * gy - h * ((r * r * r) * (1.0 / d) * jnp.sum(gy * h, axis=-1, keepdims=True))
    return [dh_in + dh] * 2, [dxn * h * r]


def _final_fn(rv, vv):
    h, tgt = rv
    w, = vv
    d = h.shape[-1]
    r = lax.rsqrt(jnp.mean(h * h, axis=-1, keepdims=True) + EPS)
    hn = h * r
    e = hn * w - tgt
    dy = e * (1.0 / d)
    gy = dy * w
    dh = r * gy - h * ((r * r * r) * (1.0 / d) * jnp.sum(gy * h, axis=-1, keepdims=True))
    return [dh] * 2, [e * e, dy * hn]


def _onorm_fwd_fn(rv, vv):
    o, g = rv
    gain, = vv
    r = lax.rsqrt(jnp.mean(o * o, axis=-1, keepdims=True) + EPS)
    return [o * r * gain * (g * _sigmoid(g))], []


def _onorm_bwd_fn(rv, vv):
    o, g, don = rv
    gain, = vv
    d = o.shape[-1]
    r = lax.rsqrt(jnp.mean(o * o, axis=-1, keepdims=True) + EPS)
    sg = _sigmoid(g)
    sl = g * sg
    n = o * r
    dg = don * n * gain * (sg * (1.0 + g * (1.0 - sg)))
    gy = don * sl * gain
    do = r * gy - o * ((r * r * r) * (1.0 / d) * jnp.sum(gy * o, axis=-1, keepdims=True))
    return [do, dg], [don * sl * n]


HALO = SUBLANES


def _col_call(fn, cols, vecs, outs, n_acc, name, before, after, tc=LANES, chunk=128):
    T = cols[0][0].shape[0]
    chunk = min(chunk, T)
    nch = T // chunk
    ncol = outs[0][1] // tc
    n_c, n_v, n_o = len(cols), len(vecs), len(outs)
    hb = HALO if before else 0
    rw = chunk + hb + (HALO if after else 0)

    def body(*refs):
        c_refs = refs[:n_c]
        v_refs = refs[n_c:n_c + n_v]
        o_refs = refs[n_c + n_v:n_c + n_v + n_o]
        a_refs = refs[n_c + n_v + n_o:]
        vv = [v[...] for v in v_refs]
        wrow = lax.broadcasted_iota(jnp.int32, (rw, tc), 0)
        inside = (wrow >= hb) & (wrow < hb + chunk)

        def step(i, carry):
            r0 = pl.multiple_of(i * chunk, chunk)
            wins = []
            for ref in c_refs:
                parts = []
                if before:
                    pb = ref[pl.ds(pl.multiple_of(jnp.maximum(r0 - HALO, 0), HALO), HALO), :]
                    parts.append(jnp.where(i > 0, pb, 0.0))
                parts.append(ref[pl.ds(r0, chunk), :])
                if after:
                    pa = ref[pl.ds(pl.multiple_of(jnp.minimum(r0 + chunk, T - HALO), HALO), HALO), :]
                    parts.append(jnp.where(i < nch - 1, pa, 0.0))
                wins.append(jnp.concatenate(parts, axis=0) if len(parts) > 1 else parts[0])
            o_vals, a_vals = fn(wins, vv, inside)
            p = 0
            for o, (nseg, _, _) in zip(o_refs, outs):
                for s in range(nseg):
                    o[s, pl.ds(r0, chunk), :] = o_vals[p][hb:hb + chunk].astype(o.dtype)
                    p += 1
            return tuple(c + a for c, a in zip(carry, a_vals))

        taps = [v.shape[0] for v, _ in vecs][:n_acc]
        init = tuple(jnp.zeros((1, tc), F32) for k in taps for _ in range(k))
        sums = lax.fori_loop(0, nch, step, init)
        arow = lax.broadcasted_iota(jnp.int32, (SUBLANES, tc), 0)
        p = 0
        for a, k in zip(a_refs, taps):
            acc = jnp.zeros((SUBLANES, tc), F32)
            for t in range(k):
                acc = jnp.where(arow == t, sums[p], acc)
                p += 1
            a[...] = acc

    in_specs = [pl.BlockSpec((T, tc), functools.partial(lambda j, off: (0, off + j), off=off)) for _, off in cols]
    in_specs += [pl.BlockSpec((v.shape[0], tc), functools.partial(lambda j, off: (0, off + j), off=off))
                 for v, off in vecs]
    out_specs = [pl.BlockSpec((nseg, T, tc), lambda j: (0, 0, j)) for nseg, _, _ in outs]
    out_specs += [pl.BlockSpec((SUBLANES, tc), lambda j: (0, j)) for _ in range(n_acc)]
    out_shape = [jax.ShapeDtypeStruct((nseg, T, w), dt) for nseg, w, dt in outs]
    out_shape += [jax.ShapeDtypeStruct((SUBLANES, ncol * tc), F32) for _ in range(n_acc)]
    return _pcall(
        body, name=name, grid=(ncol,), in_specs=in_specs, out_specs=out_specs, out_shape=out_shape,
        compiler_params=_params(("parallel",)),
    )(*[c[0] for c in cols], *[v[0] for v in vecs])


def _down(x, k):
    return x if k == 0 else pltpu.roll(x, k, 0)


def _up(x, k):
    return x if k == 0 else pltpu.roll(x, x.shape[0] - k, 0)


def _conv(x, w):
    return w[0:1] * _down(x, 2) + w[1:2] * _down(x, 1) + w[2:3] * x


def _conv_t(d, w):
    return w[2:3] * d + w[1:2] * _up(d, 1) + w[0:1] * _up(d, 2)


def _tap_sums(d, x, inside):
    dm = jnp.where(inside, d, 0.0)
    return [jnp.sum(dm * _down(x, 2 - k), axis=0, keepdims=True) for k in range(3)]


def _glu_fwd_fn(wins, vv, inside):
    xg, xv = wins
    wg, wv = vv
    ug = _conv(xg, wg)
    uv = _conv(xv, wv)
    return [ug * _sigmoid(ug) * uv], []


def _glu_bwd_fn(wins, vv, inside):
    xg, xv, da = wins
    wg, wv = vv
    ug = _conv(xg, wg)
    uv = _conv(xv, wv)
    sg = _sigmoid(ug)
    dug = da * uv * (sg * (1.0 + ug * (1.0 - sg)))
    duv = da * (ug * sg)
    return [_conv_t(dug, wg), _conv_t(duv, wv)], _tap_sums(dug, xg, inside) + _tap_sums(duv, xv, inside)


def _sc_fwd_fn(wins, vv, inside):
    gb, gc, hh = wins
    w, = vv
    return [gb * _conv(gc * hh, w)], []


def _sc_bwd_fn(wins, vv, inside):
    gb, gc, hh, dy = wins
    w, = vv
    z = gc * hh
    dcv = dy * gb
    dz = _conv_t(dcv, w)
    return [dy * _conv(z, w), dz * hh, dz * gc], _tap_sums(dcv, z, inside)


def _gates(qr, fr, lb):
    sg = _sigmoid(fr)
    f = lb + (1.0 - lb) * sg
    sq = _sigmoid(qr)
    q = qr * sq * (HEAD ** -0.5)
    return q, 1.0 - f, jnp.log(f), f, sg, sq


def _chunk_decays(gl, row):
    c = gl.shape[0]

    def seg_prefix(x, g):
        r = row & (g - 1)
        d = 1
        while d < g:
            x = x + jnp.where(r >= d, pltpu.roll(x, d, 0), 0.0)
            d *= 2
        return x

    def seg_suffix(x, g):
        r = row & (g - 1)
        d = 1
        while d < g:
            x = x + jnp.where(r < g - d, pltpu.roll(x, c - d, 0), 0.0)
            d *= 2
        return x

    eq, ek = [], []
    g = c // 2
    while g >= 1:
        right = (row & g) != 0
        eq.append(jnp.where(right, jnp.exp(seg_prefix(gl, g)), 0.0))
        ek.append(jnp.where(right, 0.0, jnp.exp(seg_suffix(gl, g) - gl)))
        g //= 2
    return seg_prefix(gl, c), eq, ek


def _intra(q, k, eq, ek, tt, ss):
    c = q.shape[0]
    qs, ks = [], []
    a = jnp.where(tt == ss, jnp.sum(q * k, axis=1, keepdims=True), 0.0)
    g = c // 2
    for e_q, e_k in zip(eq, ek):
        qg = (q * e_q).astype(BF16)
        kg = (k * e_k).astype(BF16)
        p = lax.dot_general(qg, kg, (((1,), (1,)), ((), ())), preferred_element_type=F32)
        a = a + (p if 2 * g >= c else jnp.where((tt ^ ss) < 2 * g, p, 0.0))
        qs.append(qg)
        ks.append(kg)
        g //= 2
    return a, qs, ks


def _hgrn_fwd(proj, lb, d_model):
    T = proj.shape[0]
    H = d_model // HEAD
    nch = T // CHUNK

    def body(q_ref, f_ref, v_ref, lb_ref, o_ref, s_ref):
        lbv = lb_ref[...]
        row = lax.broadcasted_iota(jnp.int32, (CHUNK, HEAD), 0)
        tt = lax.broadcasted_iota(jnp.int32, (CHUNK, CHUNK), 0)
        ss = lax.broadcasted_iota(jnp.int32, (CHUNK, CHUNK), 1)

        def step(i, st):
            sl = pl.ds(pl.multiple_of(i * CHUNK, CHUNK), CHUNK)
            q, k, gl, _, _, _ = _gates(q_ref[sl, :], f_ref[sl, :], lbv)
            v = v_ref[sl, :].astype(BF16)
            b, eq, ek = _chunk_decays(gl, row)
            a, _, _ = _intra(q, k, eq, ek, tt, ss)
            bl = b[CHUNK - 1:CHUNK, :]
            q0 = (q * jnp.exp(b)).astype(BF16)
            kh = (k * jnp.exp(bl - b)).astype(BF16)
            s_ref[i] = st
            o = jnp.dot(a.astype(BF16), v, preferred_element_type=F32)
            o = o + lax.dot_general(q0, st.astype(BF16), (((1,), (1,)), ((), ())), preferred_element_type=F32)
            o_ref[sl, :] = o
            return jnp.exp(bl) * st + lax.dot_general(v, kh, (((0,), (0,)), ((), ())), preferred_element_type=F32)

        lax.fori_loop(0, nch, step, jnp.zeros((HEAD, HEAD), F32))

    col = lambda off: pl.BlockSpec((T, HEAD), functools.partial(lambda h, off: (0, off + h), off=off))
    return _pcall(
        body, name="hgrn_fwd", grid=(H,),
        in_specs=[col(0), col(H), col(2 * H), pl.BlockSpec((1, HEAD), lambda h: (0, h))],
        out_specs=[pl.BlockSpec((T, HEAD), lambda h: (0, h)),
                   pl.BlockSpec((None, nch, HEAD, HEAD), lambda h: (h, 0, 0, 0))],
        out_shape=[jax.ShapeDtypeStruct((T, d_model), F32), jax.ShapeDtypeStruct((H, nch, HEAD, HEAD), F32)],
        compiler_params=_params(("parallel",)),
    )(proj, proj, proj, lb)


def _hgrn_bwd(proj, lb, states, do, dgate, d_model):
    T = proj.shape[0]
    H = d_model // HEAD
    nch = T // CHUNK

    def body(q_ref, f_ref, v_ref, lb_ref, s_ref, do_ref, dg_ref, dp_ref, dlb_ref):
        dq_ref, df_ref, dv_ref = dp_ref.at[0], dp_ref.at[1], dp_ref.at[2]
        dp_ref[3] = dg_ref[...]
        lbv = lb_ref[...]
        row = lax.broadcasted_iota(jnp.int32, (CHUNK, HEAD), 0)
        tt = lax.broadcasted_iota(jnp.int32, (CHUNK, CHUNK), 0)
        ss = lax.broadcasted_iota(jnp.int32, (CHUNK, CHUNK), 1)
        last = row == CHUNK - 1
        nt = (((1,), (1,)), ((), ()))
        tn = (((0,), (0,)), ((), ()))

        def step(j, carry):
            dst, dlb = carry
            i = nch - 1 - j
            sl = pl.ds(pl.multiple_of(i * CHUNK, CHUNK), CHUNK)
            qr = q_ref[sl, :]
            q, k, gl, f, sg, sq = _gates(qr, f_ref[sl, :], lbv)
            v = v_ref[sl, :].astype(BF16)
            d_o = do_ref[sl, :].astype(BF16)
            st = s_ref[i]
            st16 = st.astype(BF16)
            dst16 = dst.astype(BF16)
            b, eq, ek = _chunk_decays(gl, row)
            a, qs, ks = _intra(q, k, eq, ek, tt, ss)
            bl = b[CHUNK - 1:CHUNK, :]
            e0 = jnp.exp(b)
            eh = jnp.exp(bl - b)
            ebl = jnp.exp(bl)
            q0 = q * e0
            kh = k * eh
            q016 = q0.astype(BF16)
            kh16 = kh.astype(BF16)
            dv = lax.dot_general(a.astype(BF16), d_o, tn, preferred_element_type=F32)
            dv = dv + lax.dot_general(kh16, dst16, nt, preferred_element_type=F32)
            dv_ref[sl, :] = dv.astype(dv_ref.dtype)
            da = lax.dot_general(d_o, v, nt, preferred_element_type=F32)
            da = jnp.where(tt >= ss, da, 0.0)
            dd = jnp.sum(jnp.where(tt == ss, da, 0.0), axis=1, keepdims=True)
            dq0 = jnp.dot(d_o, st16, preferred_element_type=F32)
            dkh = jnp.dot(v, dst16, preferred_element_type=F32)
            dq = dq0 * e0 + dd * k
            dk = dkh * eh + dd * q
            db = dq0 * q016.astype(F32) - dkh * kh16.astype(F32)
            g = CHUNK // 2
            for e_q, e_k, qg, kg in zip(eq, ek, qs, ks):
                dag = (da if 2 * g >= CHUNK else jnp.where((tt ^ ss) < 2 * g, da, 0.0)).astype(BF16)
                dqg = jnp.dot(dag, kg, preferred_element_type=F32)
                dkg = lax.dot_general(dag, qg, tn, preferred_element_type=F32)
                dq = dq + dqg * e_q
                dk = dk + dkg * e_k
                db = db + (dqg * qg.astype(F32) - dkg * kg.astype(F32))
                g //= 2
            dbl = jnp.sum(dkh * kh16.astype(F32), axis=0, keepdims=True) + ebl * jnp.sum(dst * st, axis=0, keepdims=True)
            db = db + jnp.where(last, dbl, 0.0)
            d = 1
            while d < CHUNK:
                db = db + jnp.where(row < CHUNK - d, pltpu.roll(db, CHUNK - d, 0), 0.0)
                d *= 2
            dfg = db / f - dk
            df_ref[sl, :] = (dfg * (1.0 - lbv) * sg * (1.0 - sg)).astype(df_ref.dtype)
            dq_ref[sl, :] = (dq * (HEAD ** -0.5) * (sq * (1.0 + qr * (1.0 - sq)))).astype(dq_ref.dtype)
            dlb = dlb + jnp.sum(dfg * (1.0 - sg), axis=0, keepdims=True)
            dst = ebl * dst + lax.dot_general(d_o, q016, tn, preferred_element_type=F32)
            return dst, dlb

        _, dlb = lax.fori_loop(0, nch, step, (jnp.zeros((HEAD, HEAD), F32), jnp.zeros((1, HEAD), F32)))
        arow = lax.broadcasted_iota(jnp.int32, (SUBLANES, HEAD), 0)
        dlb_ref[...] = jnp.where(arow == 0, dlb, 0.0)

    col = lambda off: pl.BlockSpec((T, HEAD), functools.partial(lambda h, off: (0, off + h), off=off))
    return _pcall(
        body, name="hgrn_bwd", grid=(H,),
        in_specs=[col(0), col(H), col(2 * H), pl.BlockSpec((1, HEAD), lambda h: (0, h)),
                  pl.BlockSpec((None, nch, HEAD, HEAD), lambda h: (h, 0, 0, 0)), col(0), col(0)],
        out_specs=[pl.BlockSpec((4, T, HEAD), lambda h: (0, 0, h)), pl.BlockSpec((SUBLANES, HEAD), lambda h: (0, h))],
        out_shape=[jax.ShapeDtypeStruct((4, T, d_model), BF16), jax.ShapeDtypeStruct((SUBLANES, d_model), F32)],
        compiler_params=_params(("parallel",)),
    )(proj, proj, proj, lb, states, do, dgate)


def _lb_softmax(table):
    n, f = table.shape

    def body(t_ref, p_ref):
        t = t_ref[...]
        e = jnp.exp(t - jnp.max(t, axis=0, keepdims=True))
        p_ref[...] = e / jnp.sum(e, axis=0, keepdims=True)

    padded = jnp.pad(table, ((0, SUBLANES - n), (0, 0)), constant_values=-jnp.inf)
    return _pcall(body, name="lb_softmax", out_shape=jax.ShapeDtypeStruct((SUBLANES, f), F32))(padded)


def _adamw_math(w, g, m, v):
    m = ADAM_B1 * m + (1.0 - ADAM_B1) * g
    v = ADAM_B2 * v + (1.0 - ADAM_B2) * (g * g)
    m_hat = m / (1.0 - ADAM_B1 ** ADAM_STEP)
    v_hat = v / (1.0 - ADAM_B2 ** ADAM_STEP)
    delta = -ADAM_LR * (m_hat / (jnp.sqrt(v_hat) + ADAM_EPS) + ADAM_WD * w)
    return delta, m, v


def _adamw(w, g, m, v, name):
    R, C = w.shape
    tr = _pick(R, (128, 64, 32, 16, 8))

    def body(w_ref, g_ref, m_ref, v_ref, d_ref, nm_ref, nv_ref):
        d, nm, nv = _adamw_math(w_ref[...], g_ref[...], m_ref[...], v_ref[...])
        d_ref[...] = d
        nm_ref[...] = nm
        nv_ref[...] = nv

    spec = pl.BlockSpec((tr, C), lambda i: (i, 0))
    return _pcall(
        body, name=name, grid=(R // tr,), in_specs=[spec] * 4, out_specs=[spec] * 3,
        out_shape=[jax.ShapeDtypeStruct((R, C), F32)] * 3, compiler_params=_params(("parallel",)),
    )(w, g, m, v)


def _adamw_halves(w, m, v, g_mine, g_recv, c, name, layer=0, prev=None):
    C = w.shape[1]
    rh = g_mine.shape[0]
    tr = _pick(rh, (128, 64, 32, 16, 8))
    nb = rh // tr
    r0 = layer * 2 * nb

    def body(c_ref, w_ref, m_ref, v_ref, gm_ref, gr_ref, *rest):
        g_ref, d_ref, nm_ref, nv_ref = rest[-4:]
        g = jnp.where(pl.program_id(0) == c_ref[0], gm_ref[...], gr_ref[...])
        d, nm, nv = _adamw_math(w_ref[...], g, m_ref[...], v_ref[...])
        g_ref[...] = g
        d_ref[...] = d
        nm_ref[...] = nm
        nv_ref[...] = nv

    full = pl.BlockSpec((tr, C), lambda h, i, cr: (r0 + h * nb + i, 0))
    half = pl.BlockSpec((tr, C), lambda h, i, cr: (i, 0))
    in_specs = [full, full, full, half, half]
    args = [c, w, m, v, g_mine, g_recv]
    alias = {}
    if prev is not None:
        in_specs += [pl.BlockSpec(memory_space=pl.ANY)] * 4
        args += list(prev)
        alias = {6 + k: k for k in range(4)}
    return _pcall(
        body, name=name,
        grid_spec=pltpu.PrefetchScalarGridSpec(
            num_scalar_prefetch=1, grid=(2, nb), in_specs=in_specs, out_specs=[full] * 4),
        out_shape=[jax.ShapeDtypeStruct(w.shape, F32)] * 4, input_output_aliases=alias,
        compiler_params=_params(("parallel", "parallel")),
    )(*args)


def _lb_table_grad(p8, dlb, n):
    f = p8.shape[1]

    def body(p_ref, d_ref, o_ref):
        p = p_ref[...]
        d = d_ref[...]
        p0 = p[0:1, :]
        first = lax.broadcasted_iota(jnp.int32, p.shape, 0) == 0
        o_ref[...] = p * (jnp.where(first, d, 0.0) - d * p0)

    return _pcall(body, name="lb_table_grad", out_shape=jax.ShapeDtypeStruct((SUBLANES, f), F32))(p8, dlb)[:n]


def _place():
    x, y, c = lax.axis_index("x"), lax.axis_index("y"), lax.axis_index("c")
    chips = [(1 - x, y), (x, 1 - y), (1 - x, 1 - y)]
    return x, y, c, chips


HBM_SPEC = pl.BlockSpec(memory_space=pltpu.HBM)


def _gather_weights(big, small):
    nb, ns = len(big), len(small)
    n = nb + ns

    def body(*refs):
        ins, outs = refs[:n], refs[n:2 * n]
        send_sems, recv_sems, own_send, own_recv = refs[2 * n:]
        x, y, c, chips = _place()
        me = 2 * x + y
        sib = (x, y, 1 - c)
        own = [pltpu.make_async_remote_copy(
            src_ref=ins[t], dst_ref=outs[t].at[me], send_sem=own_send.at[t], recv_sem=own_recv.at[t],
            device_id=sib, device_id_type=MESH) for t in range(n)]
        for cp in own:
            cp.start()

        def half(t, h):
            rh = big[t].shape[0] // 2
            return pl.ds(pl.multiple_of(h * rh, rh), rh)

        sends = []
        for t in range(n):
            for j, chip in enumerate(chips):
                k = 6 * t + j
                if t < nb:
                    src, dst = ins[t].at[half(t, c)], outs[t].at[me, half(t, c)]
                else:
                    src, dst = ins[t], outs[t].at[me]
                sends.append(pltpu.make_async_remote_copy(
                    src_ref=src, dst_ref=dst, send_sem=send_sems.at[k], recv_sem=recv_sems.at[k],
                    device_id=(*chip, c), device_id_type=MESH))
        for cp in sends:
            cp.start()
        passed = []
        for t in range(n):
            for j, (cx, cy) in enumerate(chips):
                k = 6 * t + j
                s = 2 * cx + cy
                if t < nb:
                    landed = outs[t].at[s, half(t, c)]
                    pltpu.make_async_remote_copy(
                        src_ref=landed, dst_ref=landed, send_sem=send_sems.at[k], recv_sem=recv_sems.at[k],
                        device_id=sib, device_id_type=MESH).wait_recv()
                    fwd = pltpu.make_async_remote_copy(
                        src_ref=landed, dst_ref=landed, send_sem=send_sems.at[k + 3], recv_sem=recv_sems.at[k + 3],
                        device_id=sib, device_id_type=MESH)
                    fwd.start()
                    passed.append(fwd)
                else:
                    landed = outs[t].at[s]
                    pltpu.make_async_remote_copy(
                        src_ref=landed, dst_ref=landed, send_sem=send_sems.at[k], recv_sem=recv_sems.at[k],
                        device_id=sib, device_id_type=MESH).wait_recv()
        for t in range(nb):
            for j, (cx, cy) in enumerate(chips):
                k = 6 * t + j
                other = outs[t].at[2 * cx + cy, half(t, 1 - c)]
                pltpu.make_async_remote_copy(
                    src_ref=other, dst_ref=other, send_sem=send_sems.at[k + 3], recv_sem=recv_sems.at[k + 3],
                    device_id=sib, device_id_type=MESH).wait_recv()
        for cp in sends + passed:
            cp.wait_send()
        for cp in own:
            cp.wait()

    arrs = list(big) + list(small)
    return _pcall(
        body, name="gather_weights", in_specs=[HBM_SPEC] * n, out_specs=[HBM_SPEC] * n,
        out_shape=[jax.ShapeDtypeStruct((N_CHIPS,) + a.shape, a.dtype) for a in arrs],
        scratch_shapes=[pltpu.SemaphoreType.DMA((6 * n,)), pltpu.SemaphoreType.DMA((6 * n,)),
                        pltpu.SemaphoreType.DMA((n,)), pltpu.SemaphoreType.DMA((n,))],
    )(*arrs)


SEM_SPEC = pl.BlockSpec(memory_space=pltpu.SEMAPHORE)
DATAFLOW = pltpu.SideEffectType.DATAFLOW_SIDE_EFFECTING
COPIES_PER_SHARD = 4


def _shard_copies(ins, lands, send_sems, recv_sems):
    x, y, c, chips = _place()
    me = 2 * x + y
    cps = []
    for t in range(len(ins)):
        rh = ins[t].shape[0] // 2
        half = pl.ds(pl.multiple_of(c * rh, rh), rh)
        for j, chip in enumerate(chips):
            k = COPIES_PER_SHARD * t + j
            cps.append(pltpu.make_async_remote_copy(
                src_ref=ins[t].at[half], dst_ref=lands[t].at[me, half], send_sem=send_sems.at[k],
                recv_sem=recv_sems.at[k], device_id=(*chip, c), device_id_type=MESH))
        k = COPIES_PER_SHARD * t + 3
        cps.append(pltpu.make_async_remote_copy(
            src_ref=ins[t], dst_ref=lands[t].at[me], send_sem=send_sems.at[k], recv_sem=recv_sems.at[k],
            device_id=(x, y, 1 - c), device_id_type=MESH))
    return cps


def _gather_start(shards, thru, name):
    n = len(shards)

    def body(*refs):
        ins, lands = refs[:n], refs[n:2 * n]
        send_sems, recv_sems = refs[2 * n + 1], refs[2 * n + 2]
        for cp in _shard_copies(ins, lands, send_sems, recv_sems):
            cp.start()

    lands = [pltpu.with_memory_space_constraint(lax.empty((N_CHIPS,) + s.shape, s.dtype), pltpu.HBM) for s in shards]
    ops = [pltpu.with_memory_space_constraint(s, pltpu.HBM) for s in shards] + lands + [thru]
    nsem = COPIES_PER_SHARD * n
    res = _pcall(
        body, name=name, in_specs=[HBM_SPEC] * (2 * n + 1),
        out_specs=[SEM_SPEC, SEM_SPEC] + [HBM_SPEC] * (2 * n + 1),
        out_shape=[pltpu.SemaphoreType.DMA((nsem,)), pltpu.SemaphoreType.DMA((nsem,))]
        + [pltpu.HBM(o.shape, o.dtype) for o in ops],
        input_output_aliases={i: 2 + i for i in range(2 * n + 1)},
        compiler_params=pltpu.CompilerParams(has_side_effects=DATAFLOW),
    )(*ops)
    return res[0], res[1], res[2:2 + n], res[2 + n:2 + 2 * n], res[2 + 2 * n]


def _gather_wait(send_sems, recv_sems, shards, lands, after, name):
    n = len(shards)

    def body(*refs):
        ins, lnd = refs[:n], refs[n:2 * n]
        ssem, rsem = refs[2 * n], refs[2 * n + 1]
        for cp in _shard_copies(ins, lnd, ssem, rsem):
            cp.wait_send()
            cp.wait_recv()

    res = _pcall(
        body, name=name,
        in_specs=[HBM_SPEC] * (2 * n) + [SEM_SPEC, SEM_SPEC, pl.BlockSpec(memory_space=pl.ANY)],
        out_specs=[HBM_SPEC] * (2 * n),
        out_shape=[pltpu.HBM(o.shape, o.dtype) for o in list(shards) + list(lands)],
        input_output_aliases={i: i for i in range(2 * n)},
        compiler_params=pltpu.CompilerParams(has_side_effects=DATAFLOW),
    )(*shards, *lands, send_sems, recv_sems, after)
    return res[n:]


def _gather_forward(lands, name):
    n = len(lands)

    def body(*refs):
        outs = refs[n:2 * n]
        send_sems, recv_sems = refs[2 * n:]
        x, y, c, chips = _place()
        sib = (x, y, 1 - c)
        cps = []
        for t in range(n):
            rh = lands[t].shape[1] // 2
            for j, (cx, cy) in enumerate(chips):
                mine = outs[t].at[2 * cx + cy, pl.ds(pl.multiple_of(c * rh, rh), rh)]
                cps.append(pltpu.make_async_remote_copy(
                    src_ref=mine, dst_ref=mine, send_sem=send_sems.at[3 * t + j], recv_sem=recv_sems.at[3 * t + j],
                    device_id=sib, device_id_type=MESH))
        for cp in cps:
            cp.start()
        for t in range(n):
            rh = lands[t].shape[1] // 2
            for j, (cx, cy) in enumerate(chips):
                theirs = outs[t].at[2 * cx + cy, pl.ds(pl.multiple_of((1 - c) * rh, rh), rh)]
                pltpu.make_async_remote_copy(
                    src_ref=theirs, dst_ref=theirs, send_sem=send_sems.at[3 * t + j], recv_sem=recv_sems.at[3 * t + j],
                    device_id=sib, device_id_type=MESH).wait_recv()
        for cp in cps:
            cp.wait_send()

    return _pcall(
        body, name=name, in_specs=[HBM_SPEC] * n, out_specs=[HBM_SPEC] * n,
        out_shape=[jax.ShapeDtypeStruct(a.shape, a.dtype) for a in lands],
        input_output_aliases={i: i for i in range(n)},
        scratch_shapes=[pltpu.SemaphoreType.DMA((3 * n,)), pltpu.SemaphoreType.DMA((3 * n,))],
    )(*lands)


def _pair_exchange(grads, name):
    n = len(grads)

    def body(*refs):
        ins, outs = refs[:n], refs[n:2 * n]
        send_sems, recv_sems = refs[2 * n:]
        x, y, c, _ = _place()
        cps = [pltpu.make_async_remote_copy(
            src_ref=ins[t].at[:, 1 - c], dst_ref=outs[t], send_sem=send_sems.at[t], recv_sem=recv_sems.at[t],
            device_id=(x, y, 1 - c), device_id_type=MESH) for t in range(n)]
        for cp in cps:
            cp.start()
        for cp in cps:
            cp.wait()

    return _pcall(
        body, name=name, in_specs=[HBM_SPEC] * n, out_specs=[HBM_SPEC] * n,
        out_shape=[jax.ShapeDtypeStruct((g.shape[0],) + g.shape[2:], g.dtype) for g in grads],
        scratch_shapes=[pltpu.SemaphoreType.DMA((n,)), pltpu.SemaphoreType.DMA((n,))],
    )(*grads)


def _chip_copies(ins, lands, send_sems, recv_sems):
    x, y, c, chips = _place()
    cps = []
    for t in range(len(ins)):
        for j, (cx, cy) in enumerate(chips):
            cps.append(pltpu.make_async_remote_copy(
                src_ref=ins[t].at[2 * cx + cy], dst_ref=lands[t].at[j],
                send_sem=send_sems.at[3 * t + j], recv_sem=recv_sems.at[3 * t + j],
                device_id=(cx, cy, c), device_id_type=MESH))
    return cps


def _chip_start(parts, thru, name):
    n = len(parts)

    def body(*refs):
        ins, lands = refs[:n], refs[n:2 * n]
        send_sems, recv_sems = refs[2 * n + 1], refs[2 * n + 2]
        for cp in _chip_copies(ins, lands, send_sems, recv_sems):
            cp.start()

    lands = [pltpu.with_memory_space_constraint(lax.empty((3,) + p.shape[1:], p.dtype), pltpu.HBM) for p in parts]
    ops = [pltpu.with_memory_space_constraint(p, pltpu.HBM) for p in parts] + lands + [thru]
    res = _pcall(
        body, name=name, in_specs=[HBM_SPEC] * (2 * n + 1),
        out_specs=[SEM_SPEC, SEM_SPEC] + [HBM_SPEC] * (2 * n + 1),
        out_shape=[pltpu.SemaphoreType.DMA((3 * n,)), pltpu.SemaphoreType.DMA((3 * n,))]
        + [pltpu.HBM(o.shape, o.dtype) for o in ops],
        input_output_aliases={i: 2 + i for i in range(2 * n + 1)},
        compiler_params=pltpu.CompilerParams(has_side_effects=DATAFLOW),
    )(*ops)
    return res[0], res[1], res[2:2 + n], res[2 + n:2 + 2 * n], res[2 + 2 * n]


def _chip_wait(send_sems, recv_sems, parts, lands, after, name):
    n = len(parts)

    def body(*refs):
        ins, lnd = refs[:n], refs[n:2 * n]
        ssem, rsem = refs[2 * n], refs[2 * n + 1]
        for cp in _chip_copies(ins, lnd, ssem, rsem):
            cp.wait_send()
            cp.wait_recv()

    res = _pcall(
        body, name=name,
        in_specs=[HBM_SPEC] * (2 * n) + [SEM_SPEC, SEM_SPEC, pl.BlockSpec(memory_space=pl.ANY)],
        out_specs=[HBM_SPEC] * (2 * n),
        out_shape=[pltpu.HBM(o.shape, o.dtype) for o in list(parts) + list(lands)],
        input_output_aliases={i: i for i in range(2 * n)},
        compiler_params=pltpu.CompilerParams(has_side_effects=DATAFLOW),
    )(*parts, *lands, send_sems, recv_sems, after)
    return res[:n], res[n:]


def _pair_share(halves, name):
    n = len(halves)

    def body(*refs):
        ins, outs = refs[:n], refs[n:2 * n]
        send_sems, recv_sems = refs[2 * n:]
        x, y, c, _ = _place()
        cps = [pltpu.make_async_remote_copy(
            src_ref=ins[t], dst_ref=outs[t], send_sem=send_sems.at[t], recv_sem=recv_sems.at[t],
            device_id=(x, y, 1 - c), device_id_type=MESH) for t in range(n)]
        for cp in cps:
            cp.start()
        for cp in cps:
            cp.wait()

    return _pcall(
        body, name=name, in_specs=[HBM_SPEC] * n, out_specs=[HBM_SPEC] * n,
        out_shape=[jax.ShapeDtypeStruct(h.shape, h.dtype) for h in halves],
        scratch_shapes=[pltpu.SemaphoreType.DMA((n,)), pltpu.SemaphoreType.DMA((n,))],
    )(*halves)


def _add_pair(grad, recv, c, name):
    s, _, rh, cc = grad.shape
    tr = _pick(rh, (256, 128, 64, 32, 16))

    def body(c_ref, g_ref, r_ref, o_ref):
        o_ref[...] = (g_ref[...].astype(F32) + r_ref[...].astype(F32)).astype(o_ref.dtype)

    return _pcall(
        body, name=name,
        grid_spec=pltpu.PrefetchScalarGridSpec(
            num_scalar_prefetch=1, grid=(s, rh // tr),
            in_specs=[pl.BlockSpec((None, None, tr, cc), lambda a, i, cr: (a, cr[0], i, 0)),
                      pl.BlockSpec((None, tr, cc), lambda a, i, cr: (a, i, 0))],
            out_specs=pl.BlockSpec((None, tr, cc), lambda a, i, cr: (a, i, 0))),
        out_shape=jax.ShapeDtypeStruct((s, rh, cc), BF16),
        compiler_params=_params(("parallel", "parallel")),
    )(c, grad, recv)


def _add_chips(part, recv, me, name):
    _, rh, cc = part.shape
    tr = _pick(rh, (256, 128, 64, 32, 16))

    def body(m_ref, p_ref, r_ref, o_ref):
        o_ref[...] = ((p_ref[...].astype(F32) + r_ref[0].astype(F32)) + r_ref[1].astype(F32)) + r_ref[2].astype(F32)

    return _pcall(
        body, name=name,
        grid_spec=pltpu.PrefetchScalarGridSpec(
            num_scalar_prefetch=1, grid=(rh // tr,),
            in_specs=[pl.BlockSpec((None, tr, cc), lambda i, mr: (mr[0], i, 0)),
                      pl.BlockSpec((3, tr, cc), lambda i, mr: (0, i, 0))],
            out_specs=pl.BlockSpec((tr, cc), lambda i, mr: (i, 0))),
        out_shape=jax.ShapeDtypeStruct((rh, cc), F32),
        compiler_params=_params(("parallel",)),
    )(me, part, recv)


def _all_sum(vec):
    rows = vec.shape[0]

    def body(v_ref, o_ref, buf, send_sems, recv_sems):
        x, y, c, _ = _place()
        me = 4 * x + 2 * y + c
        buf[me] = v_ref[...]
        cps = []
        for r in range(1, 8):
            fx, fy, fc = (r >> 2) & 1, (r >> 1) & 1, r & 1
            peer = (x ^ fx, y ^ fy, c ^ fc)
            cps.append(pltpu.make_async_remote_copy(
                src_ref=v_ref, dst_ref=buf.at[me], send_sem=send_sems.at[r - 1], recv_sem=recv_sems.at[r - 1],
                device_id=peer, device_id_type=MESH))
        for cp in cps:
            cp.start()
        for r in range(1, 8):
            src = me ^ r
            pltpu.make_async_remote_copy(
                src_ref=v_ref, dst_ref=buf.at[src], send_sem=send_sems.at[r - 1], recv_sem=recv_sems.at[r - 1],
                device_id=(x, y, c), device_id_type=MESH).wait_recv()
        for cp in cps:
            cp.wait_send()
        acc = buf[0]
        for d in range(1, 8):
            acc = acc + buf[d]
        o_ref[...] = acc

    return _pcall(
        body, name="all_sum_small",
        in_specs=[pl.BlockSpec(memory_space=pltpu.VMEM)], out_specs=pl.BlockSpec(memory_space=pltpu.VMEM),
        out_shape=jax.ShapeDtypeStruct((rows, LANES), F32),
        scratch_shapes=[pltpu.VMEM((8, rows, LANES), F32), pltpu.SemaphoreType.DMA((7,)), pltpu.SemaphoreType.DMA((7,))],
    )(vec)


def _pack(parts):
    flat = jnp.concatenate([p.reshape(-1) for p in parts])
    tile = SUBLANES * LANES
    pad = (-flat.shape[0]) % tile
    return jnp.pad(flat, (0, pad)).reshape(-1, LANES)


def _unpack(vec, shapes):
    flat = vec.reshape(-1)
    out, p = [], 0
    for s in shapes:
        n = 1
        for d in s:
            n *= d
        out.append(flat[p:p + n].reshape(s))
        p += n
    return out


def _local_step(x, tgt, norm_mix, norm_ffn, lb8, out_norm, final_norm, sc_conv, ffn_conv, weights, reduce):
    T, D = x.shape
    F2 = ffn_conv.shape[-1]
    FF = F2 // 2
    tm = _pick(T, (1024, 512, 256, 128))
    wide = (1536, 1408, 1024, 768, 512, 384, 256, 128)
    cw_h, cw_s, cw_u = 4 * D // N_CHIPS, 3 * D // N_CHIPS, F2 // N_CHIPS
    kp = FF // N_CHIPS
    tk_ff = kp if kp % LANES == 0 else LANES
    tn_d = _pick(D, (1024, 512, 256, 128))
    tk_w = _pick(D, (512, 256, 128))
    tn_h = _pick(cw_h, (1024, 512, 256, 128))
    tn_s = _pick(D // N_CHIPS, (512, 256, 128))
    tn_u = _pick(cw_u, wide)
    lb = lb8[0:1]
    wm_sq = _wmap_col(D, tn_d, 0)
    wm_sq1 = _wmap_col(D, D, 0)
    seg1 = lambda a: a.reshape((1,) + a.shape)

    def mix_in(h, w):
        return _row_call(_rms_fwd_fn, [(h, 0, D)], [w], [(D, BF16)], 0, "rms_fwd")[0]

    def rms_bwd(h, dxn, dh, w):
        return _row_call(_rms_bwd_fn, [(h, 0, D), (dxn, 0, D), (dh, 0, D)], [w], [(D, F32), (D, BF16)], 1, "rms_bwd")

    def ffn_fwd(h, i, w_up, w_down):
        xn = mix_in(h, norm_ffn[i:i + 1])
        tn = _pick(cw_u, wide)
        up = _mm_nn(xn, w_up, _wmap_col(cw_u, tn, 0), D, F2, tm, D, tn, "ffn_up")
        nb = FF // LANES
        a = _col_call(_glu_fwd_fn, [(up, 0), (up, nb)], [(ffn_conv[i], 0), (ffn_conv[i], nb)], [(1, FF, BF16)], 0,
                      "glu_fwd", before=True, after=False)[0][0]
        h2 = _mm_nn(a, w_down, _wmap_row(kp, tk_ff, 0), FF, D, tm, tk_ff, tn_d, "ffn_down", res=h)
        return h2, (xn, up, a)

    def ffn_bwd(dh, dh16, h, saved, i, w_up, w_down):
        xn, up, a = saved
        da = _mm_nt(seg1(dh16), w_down, _wmap_row(kp, tk_ff, 0), FF, D, tm, tk_ff, D, "ffn_down_dx")
        nb = FF // LANES
        dgv, cg, cv = _col_call(_glu_bwd_fn, [(up, 0), (up, nb), (da, 0)], [(ffn_conv[i], 0), (ffn_conv[i], nb)],
                                [(2, FF, BF16)], 2, "glu_bwd", before=True, after=True)
        dxn = _mm_nt(dgv, w_up, _wmap_col(cw_u, tn_u, 0), D, F2, tm, D, tn_u, "ffn_up_dx")
        g_down = _mm_tn(a, seg1(dh16), (N_CHIPS, 1, kp, D), _wmap_row(kp, tk_ff, 0), FF, D, tk_ff, tn_d,
                        "ffn_down_dw", tm=_pick(T, (2048, 1024, 512, 256, 128)))
        g_up = _mm_tn(xn, dgv, (N_CHIPS, 1, D, cw_u), _wmap_col(cw_u, tn_u, 0), D, F2, tk_w, tn_u, "ffn_up_dw")
        dh2, dh2_16, dnw = rms_bwd(h, dxn, dh, norm_ffn[i:i + 1])
        return dh2, dh2_16, dnw, jnp.concatenate([cg[:3], cv[:3]], axis=1), g_up, g_down

    h0 = x
    xn0 = mix_in(h0, norm_mix[0:1])
    w_hin, = weights(0, xn0)
    proj = _mm_nn(xn0, w_hin, _wmap_col(cw_h, tn_h, 0), D, 4 * D, tm, D, tn_h, "hgrn_in")
    o, states = _hgrn_fwd(proj, lb, D)
    on = _row_call(_onorm_fwd_fn, [(o, 0, D), (proj, 3, D)], [out_norm], [(D, BF16)], 0, "onorm_fwd")[0]
    w_hout, w_up0, w_down0 = weights(1, on)
    w_hout1 = w_hout.reshape(1, D, D)
    h1 = _mm_nn(on, w_hout1, wm_sq, D, D, tm, D, tn_d, "hgrn_out", res=h0)
    h2, ffn0 = ffn_fwd(h1, 0, w_up0, w_down0)
    xn1 = mix_in(h2, norm_mix[1:2])
    w_sin, w_sout, w_up1, w_down1 = weights(2, xn1)
    w_sout1 = w_sout.reshape(1, D, D)
    tn_si = _pick(cw_s, wide)
    sproj = _mm_nn(xn1, w_sin, _wmap_col(cw_s, tn_si, 0), D, 3 * D, tm, D, tn_si, "sc_in")
    nd = D // LANES
    ysc = _col_call(_sc_fwd_fn, [(sproj, 0), (sproj, nd), (sproj, 2 * nd)], [(sc_conv, 0)], [(1, D, BF16)], 0,
                    "sc_fwd", before=True, after=False)[0][0]
    h3 = _mm_nn(ysc, w_sout1, wm_sq, D, D, tm, D, tn_d, "sc_out", res=h2)
    h4, ffn1 = ffn_fwd(h3, 1, w_up1, w_down1)

    dh, dh16, esq, dfinal = _row_call(_final_fn, [(h4, 0, D), (tgt, 0, D)], [final_norm], [(D, F32), (D, BF16)], 2,
                                      "final_loss")
    loss = 0.5 / D * jnp.sum(esq)
    dh, dh16, dnf1, dconv1, g_up1, g_down1 = ffn_bwd(dh, dh16, h3, ffn1, 1, w_up1, w_down1)
    dh16 = reduce([("ffn_w_up", 1), ("ffn_w_down", 1)], [g_up1, g_down1], dh16)
    dy = _mm_nt(seg1(dh16), w_sout1, wm_sq1, D, D, tm, D, D, "sc_out_dx")
    dsp, dscc = _col_call(_sc_bwd_fn, [(sproj, 0), (sproj, nd), (sproj, 2 * nd), (dy, 0)], [(sc_conv, 0)],
                          [(3, D, BF16)], 1, "sc_bwd", before=True, after=True)
    dxn = _mm_nt(dsp, w_sin, _wmap_col(cw_s, tn_s, 0), D, 3 * D, tm, D, tn_s, "sc_in_dx")
    g_sout = _mm_tn(ysc, seg1(dh16), (1, 1, D, D), wm_sq, D, D, tk_w, tn_d, "sc_out_dw")
    g_sin = _mm_tn(xn1, dsp, (N_CHIPS, 1, D, cw_s), _wmap_col(cw_s, tn_s, 0), D, 3 * D, tk_w, tn_s, "sc_in_dw")
    dh, dh16, dnm1 = rms_bwd(h2, dxn, dh, norm_mix[1:2])
    dh16 = reduce([("sc_w_in", 0), ("sc_w_out", 0)], [g_sin, g_sout], dh16)
    dh, dh16, dnf0, dconv0, g_up0, g_down0 = ffn_bwd(dh, dh16, h1, ffn0, 0, w_up0, w_down0)
    dh16 = reduce([("ffn_w_up", 0), ("ffn_w_down", 0)], [g_up0, g_down0], dh16)
    don = _mm_nt(seg1(dh16), w_hout1, wm_sq1, D, D, tm, D, D, "hgrn_out_dx")
    do, dgate, dgain = _row_call(_onorm_bwd_fn, [(o, 0, D), (proj, 3, D), (don, 0, D)], [out_norm],
                                 [(D, F32), (D, BF16)], 1, "onorm_bwd")
    dproj, dlb = _hgrn_bwd(proj, lb, states, do, dgate, D)
    dxn = _mm_nt(dproj, w_hin, _wmap_col(cw_h, tn_h, 0), D, 4 * D, tm, D, tn_h, "hgrn_in_dx")
    g_hout = _mm_tn(on, seg1(dh16), (1, 1, D, D), wm_sq, D, D, tk_w, tn_d, "hgrn_out_dw")
    g_hin = _mm_tn(xn0, dproj, (N_CHIPS, 1, D, cw_h), _wmap_col(cw_h, tn_h, 0), D, 4 * D, tk_w, tn_h, "hgrn_in_dw")
    grad_x, _, dnm0 = rms_bwd(h0, dxn, dh, norm_mix[0:1])
    grad_x = reduce([("hgrn_w_in", 0), ("hgrn_w_out", 0)], [g_hin, g_hout], grad_x)

    small = dict(
        loss=loss,
        norm_mix=jnp.stack([jnp.sum(dnm0, axis=0), jnp.sum(dnm1, axis=0)]),
        norm_ffn=jnp.stack([jnp.sum(dnf0, axis=0), jnp.sum(dnf1, axis=0)]),
        lb=dlb[0:1],
        out_norm=jnp.sum(dgain, axis=0)[None],
        final_norm=jnp.sum(dfinal, axis=0),
        sc_conv=dscc[:3],
        ffn_conv=jnp.stack([dconv0, dconv1]),
    )
    return grad_x, small


def kernel(x, norm_mix, norm_ffn, hgrn_w_in, hgrn_lb_table, hgrn_out_norm, hgrn_w_out, sc_w_in, sc_conv, sc_w_out, ffn_w_up, ffn_conv, ffn_w_down, final_norm, loss_target, m_norm_mix, m_norm_ffn, m_hgrn_w_in, m_hgrn_lb_table, m_hgrn_out_norm, m_hgrn_w_out, m_sc_w_in, m_sc_conv, m_sc_w_out, m_ffn_w_up, m_ffn_conv, m_ffn_w_down, m_final_norm, v_norm_mix, v_norm_ffn, v_hgrn_w_in, v_hgrn_lb_table, v_hgrn_out_norm, v_hgrn_w_out, v_sc_w_in, v_sc_conv, v_sc_w_out, v_ffn_w_up, v_ffn_conv, v_ffn_w_down, v_final_norm):
    D = x.shape[-1]
    xi, yi, ci = lax.axis_index("x"), lax.axis_index("y"), lax.axis_index("c")
    me_chip = (2 * xi + yi).astype(jnp.int32).reshape(1)
    me_core = ci.astype(jnp.int32).reshape(1)

    big_names = ["hgrn_w_in", "hgrn_w_out", "sc_w_in", "sc_w_out", "ffn_w_up", "ffn_w_down"]
    big_w = dict(hgrn_w_in=hgrn_w_in, hgrn_w_out=hgrn_w_out, sc_w_in=sc_w_in, sc_w_out=sc_w_out,
                 ffn_w_up=ffn_w_up, ffn_w_down=ffn_w_down)
    big_m = dict(hgrn_w_in=m_hgrn_w_in, hgrn_w_out=m_hgrn_w_out, sc_w_in=m_sc_w_in, sc_w_out=m_sc_w_out,
                 ffn_w_up=m_ffn_w_up, ffn_w_down=m_ffn_w_down)
    big_v = dict(hgrn_w_in=v_hgrn_w_in, hgrn_w_out=v_hgrn_w_out, sc_w_in=v_sc_w_in, sc_w_out=v_sc_w_out,
                 ffn_w_up=v_ffn_w_up, ffn_w_down=v_ffn_w_down)
    flat2 = lambda a: a.reshape(-1, a.shape[-1])

    sh = lambda a: a.reshape(-1, a.shape[-1]).astype(BF16)
    conv_shards = [flat2(sc_conv), flat2(ffn_conv)]
    w_hin, scc4, fcc4 = _gather_weights([sh(hgrn_w_in)], conv_shards)
    scc = jnp.moveaxis(scc4, 0, 1).reshape(3, D)
    f2 = ffn_conv.shape[-1] * N_CHIPS
    fcc = jnp.moveaxis(fcc4.reshape(N_CHIPS, 2, 3, -1), 0, 2).reshape(2, 3, f2)
    stage1 = [sh(hgrn_w_out), sh(ffn_w_up[0]), sh(ffn_w_down[0])]
    stage2 = [sh(sc_w_in), sh(sc_w_out), sh(ffn_w_up[1]), sh(ffn_w_down[1])]
    ss1, rs1, src1, land1, w_hin = _gather_start(stage1, w_hin, "gather_start_1")
    ss2, rs2, src2, land2, w_hin = _gather_start(stage2, w_hin, "gather_start_2")

    def weights(stage, after):
        if stage == 0:
            return (w_hin,)
        if stage == 1:
            return _gather_forward(_gather_wait(ss1, rs1, src1, land1, after, "gather_wait_1"), "gather_forward_1")
        return _gather_forward(_gather_wait(ss2, rs2, src2, land2, after, "gather_wait_2"), "gather_forward_2")

    pending = []

    def reduce(slots, grads, thru):
        k = len(pending)
        halves = [g.reshape(N_CHIPS, 2, -1, g.shape[-1]) for g in grads]
        recv = _pair_exchange(halves, "grad_pair_exchange_%d" % k)
        pair = [_add_pair(g, r, me_core, "grad_add_pair") for g, r in zip(halves, recv)]
        ss, rs, pair, land, thru = _chip_start(pair, thru, "grad_chip_start_%d" % k)
        pending.append((slots, ss, rs, pair, land))
        return thru

    lb8 = _lb_softmax(hgrn_lb_table)
    grad_x, small = _local_step(
        x[0], loss_target[0], norm_mix, norm_ffn, lb8, hgrn_out_norm, final_norm[None], scc, fcc, weights, reduce)

    small_names = ["loss", "norm_mix", "norm_ffn", "lb", "out_norm", "final_norm", "sc_conv", "ffn_conv"]
    parts = [small[n].astype(F32) for n in small_names]
    shapes = [p.shape for p in parts]
    tot = dict(zip(small_names, _unpack(_all_sum(_pack(parts)), shapes)))
    loss = tot["loss"].reshape(())
    g_lb_table = _lb_table_grad(lb8, tot["lb"], hgrn_lb_table.shape[0])
    cw = sc_conv.shape[-1]
    g_sc_conv = lax.dynamic_slice_in_dim(tot["sc_conv"], me_chip[0] * cw, cw, axis=1)[None]
    cf = ffn_conv.shape[-1]
    g_ffn_conv = lax.dynamic_slice_in_dim(tot["ffn_conv"], me_chip[0] * cf, cf, axis=2)
    g_small = dict(norm_mix=tot["norm_mix"], norm_ffn=tot["norm_ffn"], hgrn_lb_table=g_lb_table,
                   hgrn_out_norm=tot["out_norm"], sc_conv=g_sc_conv, ffn_conv=g_ffn_conv, final_norm=tot["final_norm"])
    w_small = dict(norm_mix=norm_mix, norm_ffn=norm_ffn, hgrn_lb_table=hgrn_lb_table, hgrn_out_norm=hgrn_out_norm,
                   sc_conv=sc_conv, ffn_conv=ffn_conv, final_norm=final_norm)
    m_small = dict(norm_mix=m_norm_mix, norm_ffn=m_norm_ffn, hgrn_lb_table=m_hgrn_lb_table, hgrn_out_norm=m_hgrn_out_norm,
                   sc_conv=m_sc_conv, ffn_conv=m_ffn_conv, final_norm=m_final_norm)
    v_small = dict(norm_mix=v_norm_mix, norm_ffn=v_norm_ffn, hgrn_lb_table=v_hgrn_lb_table, hgrn_out_norm=v_hgrn_out_norm,
                   sc_conv=v_sc_conv, ffn_conv=v_ffn_conv, final_norm=v_final_norm)
    sm_names = list(g_small)
    sm_shapes = [w_small[n].shape for n in sm_names]
    d_s, m_s, v_s = _adamw(_pack([w_small[n] for n in sm_names]), _pack([g_small[n] for n in sm_names]),
                           _pack([m_small[n] for n in sm_names]), _pack([v_small[n] for n in sm_names]), "adamw_small")
    out_g, out_d, out_m, out_v = dict(g_small), {}, {}, {}
    for n, d_, m_, v_ in zip(sm_names, _unpack(d_s, sm_shapes), _unpack(m_s, sm_shapes), _unpack(v_s, sm_shapes)):
        out_d[n], out_m[n], out_v[n] = d_, m_, v_

    done = {}
    after = grad_x
    for k, (slots, ss, rs, pair, land) in enumerate(pending):
        pair, recv = _chip_wait(ss, rs, pair, land, after, "grad_chip_wait_%d" % k)
        mine = [_add_chips(p, r, me_chip, "grad_add_chips") for p, r in zip(pair, recv)]
        theirs = _pair_share(mine, "grad_pair_share_%d" % k)
        for (n, layer), gm, gr in zip(slots, mine, theirs):
            done[n] = _adamw_halves(flat2(big_w[n]), flat2(big_m[n]), flat2(big_v[n]), gm, gr, me_core, "adamw_" + n,
                                    layer=layer, prev=done.get(n))
        after = theirs[0]
    for n in big_names:
        out_g[n], out_d[n], out_m[n], out_v[n] = (a.reshape(big_w[n].shape) for a in done[n])

    order = ["norm_mix", "norm_ffn", "hgrn_w_in", "hgrn_lb_table", "hgrn_out_norm", "hgrn_w_out", "sc_w_in", "sc_conv",
             "sc_w_out", "ffn_w_up", "ffn_conv", "ffn_w_down", "final_norm"]
    return (loss, grad_x[None], *[out_g[n] for n in order], *[out_d[n] for n in order],
            *[out_m[n] for n in order], *[out_v[n] for n in order])
```

```python
import functools

import jax
import jax.numpy as jnp
from jax import lax
from jax.experimental import pallas as pl
from jax.experimental.pallas import tpu as pltpu

F32 = jnp.float32
BF16 = jnp.bfloat16
MESH = pl.DeviceIdType.MESH

EPS = 1e-6
CHUNK = 64
HEAD = 128
N_CHIPS = 4
ADAM_LR, ADAM_B1, ADAM_B2, ADAM_EPS, ADAM_WD, ADAM_STEP = 0.001, 0.9, 0.999, 1e-08, 0.01, 10
VMEM_LIMIT = 56 * 1024 * 1024
SUBLANES = 8
LANES = 128


def _pcall(body, **kw):
    return pl.pallas_call(body, **kw)


def _params(sem, vmem=VMEM_LIMIT):
    return pltpu.CompilerParams(dimension_semantics=sem, vmem_limit_bytes=vmem)


def _pick(dim, prefs):
    for p in prefs:
        if p <= dim and dim % p == 0:
            return p
    return dim


def _sigmoid(x):
    return 1.0 / (1.0 + jnp.exp(-x))


def _wmap_col(cw, tn, r0):
    bps = cw // tn
    return lambda kb, nb: (nb // bps, r0 + kb, nb % bps)


def _wmap_row(kp, tk, r0):
    bps = kp // tk
    return lambda kb, nb: (kb // bps, r0 + kb % bps, nb)


def _mm_nn(a, w3, wmap, K, N, tm, tk, tn, name, res=None):
    M = a.shape[0]
    nk = K // tk

    def body(*refs):
        if res is None:
            a_ref, w_ref, o_ref = refs[:3]
        else:
            a_ref, w_ref, r_ref, o_ref = refs[:4]
        p = jnp.dot(a_ref[...], w_ref[...], preferred_element_type=F32)
        if nk == 1:
            o_ref[...] = p if res is None else p + r_ref[...]
            return
        acc = refs[-1]
        k = pl.program_id(2)

        @pl.when(k == 0)
        def _():
            acc[...] = p

        @pl.when(k > 0)
        def _():
            acc[...] += p

        @pl.when(k == nk - 1)
        def _():
            o_ref[...] = acc[...] if res is None else acc[...] + r_ref[...]

    if nk == 1:
        grid = (M // tm, N // tn)
        ix = lambda f: (lambda i, j: f(i, j, 0))
        sem = ("parallel", "parallel")
        scratch = []
    else:
        grid = (M // tm, N // tn, nk)
        ix = lambda f: f
        sem = ("parallel", "parallel", "arbitrary")
        scratch = [pltpu.VMEM((tm, tn), F32)]
    in_specs = [pl.BlockSpec((tm, tk), ix(lambda i, j, k: (i, k))),
                pl.BlockSpec((None, tk, tn), ix(lambda i, j, k: wmap(k, j)))]
    args = [a, w3]
    if res is not None:
        in_specs.append(pl.BlockSpec((tm, tn), ix(lambda i, j, k: (i, j))))
        args.append(res)
    return _pcall(
        body, name=name, grid=grid, in_specs=in_specs,
        out_specs=pl.BlockSpec((tm, tn), ix(lambda i, j, k: (i, j))),
        out_shape=jax.ShapeDtypeStruct((M, N), F32), scratch_shapes=scratch, compiler_params=_params(sem),
    )(*args)


def _mm_nt(dy3, w3, wmap, K, N, tm, tk, tn, name, per_step=1):
    M = dy3.shape[1]
    bps = dy3.shape[2] // tn
    u = per_step
    grid = (M // tm, K // tk, N // (tn * u))
    nn = grid[2]

    def body(*refs):
        o_ref = refs[-1]
        p = None
        for r in range(u):
            d = lax.dot_general(refs[r][...], refs[u + r][...], (((1,), (1,)), ((), ())), preferred_element_type=F32)
            p = d if p is None else p + d
        if nn == 1:
            o_ref[...] = p
            return
        n = pl.program_id(2)

        @pl.when(n == 0)
        def _():
            o_ref[...] = p

        @pl.when(n > 0)
        def _():
            o_ref[...] += p

    def dy_spec(r):
        return pl.BlockSpec((None, tm, tn), lambda i, j, n: ((n * u + r) // bps, i, (n * u + r) % bps))

    def w_spec(r):
        return pl.BlockSpec((None, tk, tn), lambda i, j, n: wmap(j, n * u + r))

    return _pcall(
        body, name=name, grid=grid,
        in_specs=[dy_spec(r) for r in range(u)] + [w_spec(r) for r in range(u)],
        out_specs=pl.BlockSpec((tm, tk), lambda i, j, n: (i, j)),
        out_shape=jax.ShapeDtypeStruct((M, K), F32),
        compiler_params=_params(("parallel", "parallel", "arbitrary")),
    )(*([dy3] * u), *([w3] * u))


def _mm_tn(x, dy3, shape4, wmap, K, N, tk, tn, name, tm=None):
    M = x.shape[0]
    tm = M if tm is None else tm
    nm = M // tm
    bps = dy3.shape[2] // tn

    def body(*refs):
        x_ref, dy_ref = refs[:2]
        p = lax.dot_general(x_ref[...], dy_ref[...], (((0,), (0,)), ((), ())), preferred_element_type=F32)
        if nm == 1:
            o_ref = refs[-1]
            o_ref[...] = p.astype(o_ref.dtype)
            return
        o_ref, acc = refs[-2:]
        m = pl.program_id(2)

        @pl.when(m == 0)
        def _():
            acc[...] = p

        @pl.when(m > 0)
        def _():
            acc[...] += p

        @pl.when(m == nm - 1)
        def _():
            o_ref[...] = acc[...].astype(o_ref.dtype)

    def omap(i, j, m):
        s, rb, cb = wmap(i, j)
        return (s, 0, rb, cb)

    return _pcall(
        body, name=name, grid=(K // tk, N // tn, nm),
        in_specs=[pl.BlockSpec((tm, tk), lambda i, j, m: (m, i)),
                  pl.BlockSpec((None, tm, tn), lambda i, j, m: (j // bps, m, j % bps))],
        out_specs=pl.BlockSpec((None, None, tk, tn), omap),
        out_shape=jax.ShapeDtypeStruct(shape4, BF16),
        scratch_shapes=[] if nm == 1 else [pltpu.VMEM((tk, tn), F32)],
        compiler_params=_params(("parallel", "parallel", "arbitrary")),
    )(x, dy3)


def _row_call(fn, rows, vecs, outs, n_acc, name, t_rows=256, sub=16):
    T = rows[0][0].shape[0]
    t_rows = min(t_rows, T)
    nsub = t_rows // sub
    n_r, n_v, n_o = len(rows), len(vecs), len(outs)
    width = rows[0][2]

    def body(*refs):
        r_refs = refs[:n_r]
        v_refs = refs[n_r:n_r + n_v]
        o_refs = refs[n_r + n_v:n_r + n_v + n_o]
        a_refs = refs[n_r + n_v + n_o:]

        @pl.when(pl.program_id(0) == 0)
        def _():
            for a in a_refs:
                a[...] = jnp.zeros_like(a)

        vv = [v[...] for v in v_refs]

        def step(i, carry):
            sl = pl.ds(pl.multiple_of(i * sub, sub), sub)
            o_vals, a_vals = fn([r[sl, :] for r in r_refs], vv)
            for o, val in zip(o_refs, o_vals):
                o[sl, :] = val.astype(o.dtype)
            for a, val in zip(a_refs, a_vals):
                a[...] += val.reshape(sub // SUBLANES, SUBLANES, val.shape[-1]).sum(axis=0)
            return carry

        lax.fori_loop(0, nsub, step, 0)

    in_specs = [pl.BlockSpec((t_rows, w), functools.partial(lambda i, cb: (i, cb), cb=cb)) for _, cb, w in rows]
    in_specs += [pl.BlockSpec(v.shape, lambda i: (0, 0)) for v in vecs]
    out_specs = [pl.BlockSpec((t_rows, w), lambda i: (i, 0)) for w, _ in outs]
    out_specs += [pl.BlockSpec((SUBLANES, width), lambda i: (0, 0)) for _ in range(n_acc)]
    out_shape = [jax.ShapeDtypeStruct((T, w), dt) for w, dt in outs]
    out_shape += [jax.ShapeDtypeStruct((SUBLANES, width), F32) for _ in range(n_acc)]
    return _pcall(
        body, name=name, grid=(T // t_rows,), in_specs=in_specs, out_specs=out_specs, out_shape=out_shape,
        compiler_params=_params(("arbitrary",)),
    )(*[r[0] for r in rows], *vecs)


def _rms_fwd_fn(rv, vv):
    h, = rv
    w, = vv
    r = lax.rsqrt(jnp.mean(h * h, axis=-1, keepdims=True) + EPS)
    return [h * r * w], []


def _rms_bwd_fn(rv, vv):
    h, dxn, dh_in = rv
    w, = vv
    d = h.shape[-1]
    r = lax.rsqrt(jnp.mean(h * h, axis=-1, keepdims=True) + EPS)
    gy = dxn * w
    dh = r * gy - h * ((r * r * r) * (1.0 / d) * jnp.sum(gy * h, axis=-1, keepdims=True))
    return [dh_in + dh] * 2, [dxn * h * r]


def _final_fn(rv, vv):
    h, tgt = rv
    w, = vv
    d = h.shape[-1]
    r = lax.rsqrt(jnp.mean(h * h, axis=-1, keepdims=True) + EPS)
    hn = h * r
    e = hn * w - tgt
    dy = e * (1.0 / d)
    gy = dy * w
    dh = r * gy - h * ((r * r * r) * (1.0 / d) * jnp.sum(gy * h, axis=-1, keepdims=True))
    return [dh] * 2, [e * e, dy * hn]


def _onorm_fwd_fn(rv, vv):
    o, g = rv
    gain, = vv
    r = lax.rsqrt(jnp.mean(o * o, axis=-1, keepdims=True) + EPS)
    return [o * r * gain * (g * _sigmoid(g))], []


def _onorm_bwd_fn(rv, vv):
    o, g, don = rv
    gain, = vv
    d = o.shape[-1]
    r = lax.rsqrt(jnp.mean(o * o, axis=-1, keepdims=True) + EPS)
    sg = _sigmoid(g)
    sl = g * sg
    n = o * r
    dg = don * n * gain * (sg * (1.0 + g * (1.0 - sg)))
    gy = don * sl * gain
    do = r * gy - o * ((r * r * r) * (1.0 / d) * jnp.sum(gy * o, axis=-1, keepdims=True))
    return [do, dg], [don * sl * n]


HALO = SUBLANES


def _col_call(fn, cols, vecs, outs, n_acc, name, before, after, tc=LANES, chunk=128):
    T = cols[0][0].shape[0]
    chunk = min(chunk, T)
    nch = T // chunk
    ncol = outs[0][1] // tc
    n_c, n_v, n_o = len(cols), len(vecs), len(outs)
    hb = HALO if before else 0
    rw = chunk + hb + (HALO if after else 0)

    def body(*refs):
        c_refs = refs[:n_c]
        v_refs = refs[n_c:n_c + n_v]
        o_refs = refs[n_c + n_v:n_c + n_v + n_o]
        a_refs = refs[n_c + n_v + n_o:]
        vv = [v[...] for v in v_refs]
        wrow = lax.broadcasted_iota(jnp.int32, (rw, tc), 0)
        inside = (wrow >= hb) & (wrow < hb + chunk)

        def step(i, carry):
            r0 = pl.multiple_of(i * chunk, chunk)
            wins = []
            for ref in c_refs:
                parts = []
                if before:
                    pb = ref[pl.ds(pl.multiple_of(jnp.maximum(r0 - HALO, 0), HALO), HALO), :]
                    parts.append(jnp.where(i > 0, pb, 0.0))
                parts.append(ref[pl.ds(r0, chunk), :])
                if after:
                    pa = ref[pl.ds(pl.multiple_of(jnp.minimum(r0 + chunk, T - HALO), HALO), HALO), :]
                    parts.append(jnp.where(i < nch - 1, pa, 0.0))
                wins.append(jnp.concatenate(parts, axis=0) if len(parts) > 1 else parts[0])
            o_vals, a_vals = fn(wins, vv, inside)
            p = 0
            for o, (nseg, _, _) in zip(o_refs, outs):
                for s in range(nseg):
                    o[s, pl.ds(r0, chunk), :] = o_vals[p][hb:hb + chunk].astype(o.dtype)
                    p += 1
            return tuple(c + a for c, a in zip(carry, a_vals))

        taps = [v.shape[0] for v, _ in vecs][:n_acc]
        init = tuple(jnp.zeros((1, tc), F32) for k in taps for _ in range(k))
        sums = lax.fori_loop(0, nch, step, init)
        arow = lax.broadcasted_iota(jnp.int32, (SUBLANES, tc), 0)
        p = 0
        for a, k in zip(a_refs, taps):
            acc = jnp.zeros((SUBLANES, tc), F32)
            for t in range(k):
                acc = jnp.where(arow == t, sums[p], acc)
                p += 1
            a[...] = acc

    in_specs = [pl.BlockSpec((T, tc), functools.partial(lambda j, off: (0, off + j), off=off)) for _, off in cols]
    in_specs += [pl.BlockSpec((v.shape[0], tc), functools.partial(lambda j, off: (0, off + j), off=off))
                 for v, off in vecs]
    out_specs = [pl.BlockSpec((nseg, T, tc), lambda j: (0, 0, j)) for nseg, _, _ in outs]
    out_specs += [pl.BlockSpec((SUBLANES, tc), lambda j: (0, j)) for _ in range(n_acc)]
    out_shape = [jax.ShapeDtypeStruct((nseg, T, w), dt) for nseg, w, dt in outs]
    out_shape += [jax.ShapeDtypeStruct((SUBLANES, ncol * tc), F32) for _ in range(n_acc)]
    return _pcall(
        body, name=name, grid=(ncol,), in_specs=in_specs, out_specs=out_specs, out_shape=out_shape,
        compiler_params=_params(("parallel",)),
    )(*[c[0] for c in cols], *[v[0] for v in vecs])


def _down(x, k):
    return x if k == 0 else pltpu.roll(x, k, 0)


def _up(x, k):
    return x if k == 0 else pltpu.roll(x, x.shape[0] - k, 0)


def _lags(x):
    return _down(x, 2), _down(x, 1), x


def _conv(lags, w):
    return w[0:1] * lags[0] + w[1:2] * lags[1] + w[2:3] * lags[2]


def _conv_t(d, w):
    return w[2:3] * d + w[1:2] * _up(d, 1) + w[0:1] * _up(d, 2)


def _tap_sums(d, lags, inside):
    dm = jnp.where(inside, d, 0.0)
    return [jnp.sum(dm * lag, axis=0, keepdims=True) for lag in lags]


def _glu_fwd_fn(wins, vv, inside):
    xg, xv = wins
    wg, wv = vv
    ug = _conv(_lags(xg), wg)
    uv = _conv(_lags(xv), wv)
    return [ug * _sigmoid(ug) * uv], []


def _glu_bwd_fn(wins, vv, inside):
    xg, xv, da = wins
    wg, wv = vv
    lg, lv = _lags(xg), _lags(xv)
    ug = _conv(lg, wg)
    uv = _conv(lv, wv)
    sg = _sigmoid(ug)
    dug = da * uv * (sg * (1.0 + ug * (1.0 - sg)))
    duv = da * (ug * sg)
    return [_conv_t(dug, wg), _conv_t(duv, wv)], _tap_sums(dug, lg, inside) + _tap_sums(duv, lv, inside)


def _sc_fwd_fn(wins, vv, inside):
    gb, gc, hh = wins
    w, = vv
    return [gb * _conv(_lags(gc * hh), w)], []


def _sc_bwd_fn(wins, vv, inside):
    gb, gc, hh, dy = wins
    w, = vv
    lz = _lags(gc * hh)
    dcv = dy * gb
    dz = _conv_t(dcv, w)
    return [dy * _conv(lz, w), dz * hh, dz * gc], _tap_sums(dcv, lz, inside)


def _gates(qr, fr, lb):
    sg = _sigmoid(fr)
    f = lb + (1.0 - lb) * sg
    sq = _sigmoid(qr)
    q = qr * sq * (HEAD ** -0.5)
    return q, 1.0 - f, jnp.log(f), f, sg, sq


def _boundary_rows(b, g, row):
    c = b.shape[0]
    if 2 * g >= SUBLANES:
        x = b.reshape(c // (2 * g), 2 * g, LANES)
        return jnp.broadcast_to(x[:, g - 1:g, :], x.shape).reshape(c, LANES)
    x = b.reshape(c // SUBLANES, SUBLANES, LANES)
    lo = jnp.broadcast_to(x[:, 1:2, :], x.shape).reshape(c, LANES)
    hi = jnp.broadcast_to(x[:, 5:6, :], x.shape).reshape(c, LANES)
    return jnp.where((row & 4) == 0, lo, hi)


def _chunk_decays(gl, f, row):
    c = gl.shape[0]
    b = gl
    d = 1
    while d < c:
        b = b + jnp.where(row >= d, pltpu.roll(b, d, 0), 0.0)
        d *= 2
    eq, ek = [], []
    g = c // 2
    while g >= 2:
        right = (row & g) != 0
        m = _boundary_rows(b, g, row)
        z = jnp.exp(jnp.where(right, b - m, m - b))
        eq.append(jnp.where(right, z, 0.0))
        ek.append(jnp.where(right, 0.0, z))
        g //= 2
    odd = (row & 1) != 0
    eq.append(jnp.where(odd, f, 0.0))
    ek.append(jnp.where(odd, 0.0, 1.0))
    return b, eq, ek


def _intra(q, k, eq, ek, tt, ss):
    c = q.shape[0]
    qs, ks = [], []
    a = jnp.where(tt == ss, jnp.sum(q * k, axis=1, keepdims=True), 0.0)
    g = c // 2
    for e_q, e_k in zip(eq, ek):
        qg = (q * e_q).astype(BF16)
        kg = (k * e_k).astype(BF16)
        p = lax.dot_general(qg, kg, (((1,), (1,)), ((), ())), preferred_element_type=F32)
        a = a + (p if 2 * g >= c else jnp.where((tt ^ ss) < 2 * g, p, 0.0))
        qs.append(qg)
        ks.append(kg)
        g //= 2
    return a, qs, ks


def _hgrn_fwd(proj, lb, d_model):
    T = proj.shape[0]
    H = d_model // HEAD
    nch = T // CHUNK

    def body(q_ref, f_ref, v_ref, lb_ref, o_ref, s_ref):
        lbv = lb_ref[...]
        row = lax.broadcasted_iota(jnp.int32, (CHUNK, HEAD), 0)
        tt = lax.broadcasted_iota(jnp.int32, (CHUNK, CHUNK), 0)
        ss = lax.broadcasted_iota(jnp.int32, (CHUNK, CHUNK), 1)

        def step(i, st):
            sl = pl.ds(pl.multiple_of(i * CHUNK, CHUNK), CHUNK)
            q, k, gl, f, _, _ = _gates(q_ref[sl, :], f_ref[sl, :], lbv)
            v = v_ref[sl, :].astype(BF16)
            b, eq, ek = _chunk_decays(gl, f, row)
            a, _, _ = _intra(q, k, eq, ek, tt, ss)
            bl = b[CHUNK - 1:CHUNK, :]
            q0 = (q * jnp.exp(b)).astype(BF16)
            kh = (k * jnp.exp(bl - b)).astype(BF16)
            s_ref[i] = st
            o = jnp.dot(a.astype(BF16), v, preferred_element_type=F32)
            o = o + lax.dot_general(q0, st.astype(BF16), (((1,), (1,)), ((), ())), preferred_element_type=F32)
            o_ref[sl, :] = o
            return jnp.exp(bl) * st + lax.dot_general(v, kh, (((0,), (0,)), ((), ())), preferred_element_type=F32)

        lax.fori_loop(0, nch // 2, lambda i, st: step(2 * i + 1, step(2 * i, st)), jnp.zeros((HEAD, HEAD), F32))

    col = lambda off: pl.BlockSpec((T, HEAD), functools.partial(lambda h, off: (0, off + h), off=off))
    return _pcall(
        body, name="hgrn_fwd", grid=(H,),
        in_specs=[col(0), col(H), col(2 * H), pl.BlockSpec((1, HEAD), lambda h: (0, h))],
        out_specs=[pl.BlockSpec((T, HEAD), lambda h: (0, h)),
                   pl.BlockSpec((None, nch, HEAD, HEAD), lambda h: (h, 0, 0, 0))],
        out_shape=[jax.ShapeDtypeStruct((T, d_model), F32), jax.ShapeDtypeStruct((H, nch, HEAD, HEAD), F32)],
        compiler_params=_params(("parallel",)),
    )(proj, proj, proj, lb)


def _hgrn_bwd(proj, lb, states, do, dgate, d_model):
    T = proj.shape[0]
    H = d_model // HEAD
    nch = T // CHUNK

    def body(q_ref, f_ref, v_ref, lb_ref, s_ref, do_ref, dg_ref, dp_ref, dlb_ref):
        dq_ref, df_ref, dv_ref = dp_ref.at[0], dp_ref.at[1], dp_ref.at[2]
        dp_ref[3] = dg_ref[...]
        lbv = lb_ref[...]
        row = lax.broadcasted_iota(jnp.int32, (CHUNK, HEAD), 0)
        tt = lax.broadcasted_iota(jnp.int32, (CHUNK, CHUNK), 0)
        ss = lax.broadcasted_iota(jnp.int32, (CHUNK, CHUNK), 1)
        last = row == CHUNK - 1
        nt = (((1,), (1,)), ((), ()))
        tn = (((0,), (0,)), ((), ()))

        def step(j, carry):
            dst, dlb = carry
            i = nch - 1 - j
            sl = pl.ds(pl.multiple_of(i * CHUNK, CHUNK), CHUNK)
            qr = q_ref[sl, :]
            q, k, gl, f, sg, sq = _gates(qr, f_ref[sl, :], lbv)
            v = v_ref[sl, :].astype(BF16)
            d_o = do_ref[sl, :].astype(BF16)
            st = s_ref[i]
            st16 = st.astype(BF16)
            dst16 = dst.astype(BF16)
            b, eq, ek = _chunk_decays(gl, f, row)
            a, qs, ks = _intra(q, k, eq, ek, tt, ss)
            bl = b[CHUNK - 1:CHUNK, :]
            e0 = jnp.exp(b)
            eh = jnp.exp(bl - b)
            ebl = jnp.exp(bl)
            q0 = q * e0
            kh = k * eh
            q016 = q0.astype(BF16)
            kh16 = kh.astype(BF16)
            dv = lax.dot_general(a.astype(BF16), d_o, tn, preferred_element_type=F32)
            dv = dv + lax.dot_general(kh16, dst16, nt, preferred_element_type=F32)
            dv_ref[sl, :] = dv.astype(dv_ref.dtype)
            da = lax.dot_general(d_o, v, nt, preferred_element_type=F32)
            da = jnp.where(tt >= ss, da, 0.0)
            dd = jnp.sum(jnp.where(tt == ss, da, 0.0), axis=1, keepdims=True)
            dq0 = jnp.dot(d_o, st16, preferred_element_type=F32)
            dkh = jnp.dot(v, dst16, preferred_element_type=F32)
            dq = dq0 * e0 + dd * k
            dk = dkh * eh + dd * q
            db = dq0 * q016.astype(F32) - dkh * kh16.astype(F32)
            g = CHUNK // 2
            for e_q, e_k, qg, kg in zip(eq, ek, qs, ks):
                dag = (da if 2 * g >= CHUNK else jnp.where((tt ^ ss) < 2 * g, da, 0.0)).astype(BF16)
                dqg = jnp.dot(dag, kg, preferred_element_type=F32)
                dkg = lax.dot_general(dag, qg, tn, preferred_element_type=F32)
                dq = dq + dqg * e_q
                dk = dk + dkg * e_k
                db = db + (dqg * qg.astype(F32) - dkg * kg.astype(F32))
                g //= 2
            dbl = jnp.sum(dkh * kh16.astype(F32), axis=0, keepdims=True) + ebl * jnp.sum(dst * st, axis=0, keepdims=True)
            db = db + jnp.where(last, dbl, 0.0)
            d = 1
            while d < CHUNK:
                db = db + jnp.where(row < CHUNK - d, pltpu.roll(db, CHUNK - d, 0), 0.0)
                d *= 2
            dfg = db / f - dk
            df_ref[sl, :] = (dfg * (1.0 - lbv) * sg * (1.0 - sg)).astype(df_ref.dtype)
            dq_ref[sl, :] = (dq * (HEAD ** -0.5) * (sq * (1.0 + qr * (1.0 - sq)))).astype(dq_ref.dtype)
            dlb = dlb + jnp.sum(dfg * (1.0 - sg), axis=0, keepdims=True)
            dst = ebl * dst + lax.dot_general(d_o, q016, tn, preferred_element_type=F32)
            return dst, dlb

        _, dlb = lax.fori_loop(0, nch // 2, lambda j, cr: step(2 * j + 1, step(2 * j, cr)),
                               (jnp.zeros((HEAD, HEAD), F32), jnp.zeros((1, HEAD), F32)))
        arow = lax.broadcasted_iota(jnp.int32, (SUBLANES, HEAD), 0)
        dlb_ref[...] = jnp.where(arow == 0, dlb, 0.0)

    col = lambda off: pl.BlockSpec((T, HEAD), functools.partial(lambda h, off: (0, off + h), off=off))
    return _pcall(
        body, name="hgrn_bwd", grid=(H,),
        in_specs=[col(0), col(H), col(2 * H), pl.BlockSpec((1, HEAD), lambda h: (0, h)),
                  pl.BlockSpec((None, nch, HEAD, HEAD), lambda h: (h, 0, 0, 0)), col(0), col(0)],
        out_specs=[pl.BlockSpec((4, T, HEAD), lambda h: (0, 0, h)), pl.BlockSpec((SUBLANES, HEAD), lambda h: (0, h))],
        out_shape=[jax.ShapeDtypeStruct((4, T, d_model), BF16), jax.ShapeDtypeStruct((SUBLANES, d_model), F32)],
        compiler_params=_params(("parallel",)),
    )(proj, proj, proj, lb, states, do, dgate)


def _lb_softmax(table):
    n, f = table.shape

    def body(t_ref, p_ref):
        t = t_ref[...]
        e = jnp.exp(t - jnp.max(t, axis=0, keepdims=True))
        p_ref[...] = e / jnp.sum(e, axis=0, keepdims=True)

    padded = jnp.pad(table, ((0, SUBLANES - n), (0, 0)), constant_values=-jnp.inf)
    return _pcall(body, name="lb_softmax", out_shape=jax.ShapeDtypeStruct((SUBLANES, f), F32))(padded)


def _adamw_math(w, g, m, v):
    m = ADAM_B1 * m + (1.0 - ADAM_B1) * g
    v = ADAM_B2 * v + (1.0 - ADAM_B2) * (g * g)
    m_hat = m / (1.0 - ADAM_B1 ** ADAM_STEP)
    v_hat = v / (1.0 - ADAM_B2 ** ADAM_STEP)
    delta = -ADAM_LR * (m_hat / (jnp.sqrt(v_hat) + ADAM_EPS) + ADAM_WD * w)
    return delta, m, v


def _adamw(w, g, m, v, name):
    R, C = w.shape
    tr = _pick(R, (128, 64, 32, 16, 8))

    def body(w_ref, g_ref, m_ref, v_ref, d_ref, nm_ref, nv_ref):
        d, nm, nv = _adamw_math(w_ref[...], g_ref[...], m_ref[...], v_ref[...])
        d_ref[...] = d
        nm_ref[...] = nm
        nv_ref[...] = nv

    spec = pl.BlockSpec((tr, C), lambda i: (i, 0))
    return _pcall(
        body, name=name, grid=(R // tr,), in_specs=[spec] * 4, out_specs=[spec] * 3,
        out_shape=[jax.ShapeDtypeStruct((R, C), F32)] * 3, compiler_params=_params(("parallel",)),
    )(w, g, m, v)


def _adamw_halves(w, m, v, g_mine, g_recv, c, name, layer=0, prev=None):
    C = w.shape[1]
    rh = g_mine.shape[0]
    tr = _pick(rh, (128, 64, 32, 16, 8))
    nb = rh // tr
    r0 = layer * 2 * nb

    def body(c_ref, w_ref, m_ref, v_ref, gm_ref, gr_ref, *rest):
        g_ref, d_ref, nm_ref, nv_ref = rest[-4:]
        g = jnp.where(pl.program_id(0) == c_ref[0], gm_ref[...], gr_ref[...])
        d, nm, nv = _adamw_math(w_ref[...], g, m_ref[...], v_ref[...])
        g_ref[...] = g
        d_ref[...] = d
        nm_ref[...] = nm
        nv_ref[...] = nv

    full = pl.BlockSpec((tr, C), lambda h, i, cr: (r0 + h * nb + i, 0))
    half = pl.BlockSpec((tr, C), lambda h, i, cr: (i, 0))
    in_specs = [full, full, full, half, half]
    args = [c, w, m, v, g_mine, g_recv]
    alias = {}
    if prev is not None:
        in_specs += [pl.BlockSpec(memory_space=pl.ANY)] * 4
        args += list(prev)
        alias = {6 + k: k for k in range(4)}
    return _pcall(
        body, name=name,
        grid_spec=pltpu.PrefetchScalarGridSpec(
            num_scalar_prefetch=1, grid=(2, nb), in_specs=in_specs, out_specs=[full] * 4),
        out_shape=[jax.ShapeDtypeStruct(w.shape, F32)] * 4, input_output_aliases=alias,
        compiler_params=_params(("parallel", "parallel")),
    )(*args)


def _lb_table_grad(p8, dlb, n):
    f = p8.shape[1]

    def body(p_ref, d_ref, o_ref):
        p = p_ref[...]
        d = d_ref[...]
        p0 = p[0:1, :]
        first = lax.broadcasted_iota(jnp.int32, p.shape, 0) == 0
        o_ref[...] = p * (jnp.where(first, d, 0.0) - d * p0)

    return _pcall(body, name="lb_table_grad", out_shape=jax.ShapeDtypeStruct((SUBLANES, f), F32))(p8, dlb)[:n]


def _place():
    x, y, c = lax.axis_index("x"), lax.axis_index("y"), lax.axis_index("c")
    chips = [(1 - x, y), (x, 1 - y), (1 - x, 1 - y)]
    return x, y, c, chips


HBM_SPEC = pl.BlockSpec(memory_space=pltpu.HBM)


def _gather_weights(big, small):
    nb, ns = len(big), len(small)
    n = nb + ns

    def body(*refs):
        ins, outs = refs[:n], refs[n:2 * n]
        send_sems, recv_sems, own_send, own_recv = refs[2 * n:]
        x, y, c, chips = _place()
        me = 2 * x + y
        sib = (x, y, 1 - c)
        own = [pltpu.make_async_remote_copy(
            src_ref=ins[t], dst_ref=outs[t].at[me], send_sem=own_send.at[t], recv_sem=own_recv.at[t],
            device_id=sib, device_id_type=MESH) for t in range(n)]
        for cp in own:
            cp.start()

        def half(t, h):
            rh = big[t].shape[0] // 2
            return pl.ds(pl.multiple_of(h * rh, rh), rh)

        sends = []
        for t in range(n):
            for j, chip in enumerate(chips):
                k = 6 * t + j
                if t < nb:
                    src, dst = ins[t].at[half(t, c)], outs[t].at[me, half(t, c)]
                else:
                    src, dst = ins[t], outs[t].at[me]
                sends.append(pltpu.make_async_remote_copy(
                    src_ref=src, dst_ref=dst, send_sem=send_sems.at[k], recv_sem=recv_sems.at[k],
                    device_id=(*chip, c), device_id_type=MESH))
        for cp in sends:
            cp.start()
        passed = []
        for t in range(n):
            for j, (cx, cy) in enumerate(chips):
                k = 6 * t + j
                s = 2 * cx + cy
                if t < nb:
                    landed = outs[t].at[s, half(t, c)]
                    pltpu.make_async_remote_copy(
                        src_ref=landed, dst_ref=landed, send_sem=send_sems.at[k], recv_sem=recv_sems.at[k],
                        device_id=sib, device_id_type=MESH).wait_recv()
                    fwd = pltpu.make_async_remote_copy(
                        src_ref=landed, dst_ref=landed, send_sem=send_sems.at[k + 3], recv_sem=recv_sems.at[k + 3],
                        device_id=sib, device_id_type=MESH)
                    fwd.start()
                    passed.append(fwd)
                else:
                    landed = outs[t].at[s]
                    pltpu.make_async_remote_copy(
                        src_ref=landed, dst_ref=landed, send_sem=send_sems.at[k], recv_sem=recv_sems.at[k],
                        device_id=sib, device_id_type=MESH).wait_recv()
        for t in range(nb):
            for j, (cx, cy) in enumerate(chips):
                k = 6 * t + j
                other = outs[t].at[2 * cx + cy, half(t, 1 - c)]
                pltpu.make_async_remote_copy(
                    src_ref=other, dst_ref=other, send_sem=send_sems.at[k + 3], recv_sem=recv_sems.at[k + 3],
                    device_id=sib, device_id_type=MESH).wait_recv()
        for cp in sends + passed:
            cp.wait_send()
        for cp in own:
            cp.wait()

    arrs = list(big) + list(small)
    return _pcall(
        body, name="gather_weights", in_specs=[HBM_SPEC] * n, out_specs=[HBM_SPEC] * n,
        out_shape=[jax.ShapeDtypeStruct((N_CHIPS,) + a.shape, a.dtype) for a in arrs],
        scratch_shapes=[pltpu.SemaphoreType.DMA((6 * n,)), pltpu.SemaphoreType.DMA((6 * n,)),
                        pltpu.SemaphoreType.DMA((n,)), pltpu.SemaphoreType.DMA((n,))],
    )(*arrs)


SEM_SPEC = pl.BlockSpec(memory_space=pltpu.SEMAPHORE)
DATAFLOW = pltpu.SideEffectType.DATAFLOW_SIDE_EFFECTING
COPIES_PER_SHARD = 4


def _shard_copies(ins, lands, send_sems, recv_sems):
    x, y, c, chips = _place()
    me = 2 * x + y
    cps = []
    for t in range(len(ins)):
        rh = ins[t].shape[0] // 2
        half = pl.ds(pl.multiple_of(c * rh, rh), rh)
        for j, chip in enumerate(chips):
            k = COPIES_PER_SHARD * t + j
            cps.append(pltpu.make_async_remote_copy(
                src_ref=ins[t].at[half], dst_ref=lands[t].at[me, half], send_sem=send_sems.at[k],
                recv_sem=recv_sems.at[k], device_id=(*chip, c), device_id_type=MESH))
        k = COPIES_PER_SHARD * t + 3
        cps.append(pltpu.make_async_remote_copy(
            src_ref=ins[t], dst_ref=lands[t].at[me], send_sem=send_sems.at[k], recv_sem=recv_sems.at[k],
            device_id=(x, y, 1 - c), device_id_type=MESH))
    return cps


def _gather_start(shards, thru, name):
    n = len(shards)

    def body(*refs):
        ins, lands = refs[:n], refs[n:2 * n]
        send_sems, recv_sems = refs[2 * n + 1], refs[2 * n + 2]
        for cp in _shard_copies(ins, lands, send_sems, recv_sems):
            cp.start()

    lands = [pltpu.with_memory_space_constraint(lax.empty((N_CHIPS,) + s.shape, s.dtype), pltpu.HBM) for s in shards]
    ops = [pltpu.with_memory_space_constraint(s, pltpu.HBM) for s in shards] + lands + [thru]
    nsem = COPIES_PER_SHARD * n
    res = _pcall(
        body, name=name, in_specs=[HBM_SPEC] * (2 * n + 1),
        out_specs=[SEM_SPEC, SEM_SPEC] + [HBM_SPEC] * (2 * n + 1),
        out_shape=[pltpu.SemaphoreType.DMA((nsem,)), pltpu.SemaphoreType.DMA((nsem,))]
        + [pltpu.HBM(o.shape, o.dtype) for o in ops],
        input_output_aliases={i: 2 + i for i in range(2 * n + 1)},
        compiler_params=pltpu.CompilerParams(has_side_effects=DATAFLOW),
    )(*ops)
    return res[0], res[1], res[2:2 + n], res[2 + n:2 + 2 * n], res[2 + 2 * n]


def _gather_wait(send_sems, recv_sems, shards, lands, after, name):
    n = len(shards)

    def body(*refs):
        ins, lnd = refs[:n], refs[n:2 * n]
        ssem, rsem = refs[2 * n], refs[2 * n + 1]
        for cp in _shard_copies(ins, lnd, ssem, rsem):
            cp.wait_send()
            cp.wait_recv()

    res = _pcall(
        body, name=name,
        in_specs=[HBM_SPEC] * (2 * n) + [SEM_SPEC, SEM_SPEC, pl.BlockSpec(memory_space=pl.ANY)],
        out_specs=[HBM_SPEC] * (2 * n),
        out_shape=[pltpu.HBM(o.shape, o.dtype) for o in list(shards) + list(lands)],
        input_output_aliases={i: i for i in range(2 * n)},
        compiler_params=pltpu.CompilerParams(has_side_effects=DATAFLOW),
    )(*shards, *lands, send_sems, recv_sems, after)
    return res[n:]


def _gather_forward(lands, name):
    n = len(lands)

    def body(*refs):
        outs = refs[n:2 * n]
        send_sems, recv_sems = refs[2 * n:]
        x, y, c, chips = _place()
        sib = (x, y, 1 - c)
        cps = []
        for t in range(n):
            rh = lands[t].shape[1] // 2
            for j, (cx, cy) in enumerate(chips):
                mine = outs[t].at[2 * cx + cy, pl.ds(pl.multiple_of(c * rh, rh), rh)]
                cps.append(pltpu.make_async_remote_copy(
                    src_ref=mine, dst_ref=mine, send_sem=send_sems.at[3 * t + j], recv_sem=recv_sems.at[3 * t + j],
                    device_id=sib, device_id_type=MESH))
        for cp in cps:
            cp.start()
        for t in range(n):
            rh = lands[t].shape[1] // 2
            for j, (cx, cy) in enumerate(chips):
                theirs = outs[t].at[2 * cx + cy, pl.ds(pl.multiple_of((1 - c) * rh, rh), rh)]
                pltpu.make_async_remote_copy(
                    src_ref=theirs, dst_ref=theirs, send_sem=send_sems.at[3 * t + j], recv_sem=recv_sems.at[3 * t + j],
                    device_id=sib, device_id_type=MESH).wait_recv()
        for cp in cps:
            cp.wait_send()

    return _pcall(
        body, name=name, in_specs=[HBM_SPEC] * n, out_specs=[HBM_SPEC] * n,
        out_shape=[jax.ShapeDtypeStruct(a.shape, a.dtype) for a in lands],
        input_output_aliases={i: i for i in range(n)},
        scratch_shapes=[pltpu.SemaphoreType.DMA((3 * n,)), pltpu.SemaphoreType.DMA((3 * n,))],
    )(*lands)


def _pair_exchange(grads, name):
    n = len(grads)

    def body(*refs):
        ins, outs = refs[:n], refs[n:2 * n]
        send_sems, recv_sems = refs[2 * n:]
        x, y, c, _ = _place()
        cps = [pltpu.make_async_remote_copy(
            src_ref=ins[t].at[:, 1 - c], dst_ref=outs[t], send_sem=send_sems.at[t], recv_sem=recv_sems.at[t],
            device_id=(x, y, 1 - c), device_id_type=MESH) for t in range(n)]
        for cp in cps:
            cp.start()
        for cp in cps:
            cp.wait()

    return _pcall(
        body, name=name, in_specs=[HBM_SPEC] * n, out_specs=[HBM_SPEC] * n,
        out_shape=[jax.ShapeDtypeStruct((g.shape[0],) + g.shape[2:], g.dtype) for g in grads],
        scratch_shapes=[pltpu.SemaphoreType.DMA((n,)), pltpu.SemaphoreType.DMA((n,))],
    )(*grads)


def _chip_copies(ins, lands, send_sems, recv_sems):
    x, y, c, chips = _place()
    cps = []
    for t in range(len(ins)):
        for j, (cx, cy) in enumerate(chips):
            cps.append(pltpu.make_async_remote_copy(
                src_ref=ins[t].at[2 * cx + cy], dst_ref=lands[t].at[j],
                send_sem=send_sems.at[3 * t + j], recv_sem=recv_sems.at[3 * t + j],
                device_id=(cx, cy, c), device_id_type=MESH))
    return cps


def _chip_start(parts, thru, name):
    n = len(parts)

    def body(*refs):
        ins, lands = refs[:n], refs[n:2 * n]
        send_sems, recv_sems = refs[2 * n + 1], refs[2 * n + 2]
        for cp in _chip_copies(ins, lands, send_sems, recv_sems):
            cp.start()

    lands = [pltpu.with_memory_space_constraint(lax.empty((3,) + p.shape[1:], p.dtype), pltpu.HBM) for p in parts]
    ops = [pltpu.with_memory_space_constraint(p, pltpu.HBM) for p in parts] + lands + [thru]
    res = _pcall(
        body, name=name, in_specs=[HBM_SPEC] * (2 * n + 1),
        out_specs=[SEM_SPEC, SEM_SPEC] + [HBM_SPEC] * (2 * n + 1),
        out_shape=[pltpu.SemaphoreType.DMA((3 * n,)), pltpu.SemaphoreType.DMA((3 * n,))]
        + [pltpu.HBM(o.shape, o.dtype) for o in ops],
        input_output_aliases={i: 2 + i for i in range(2 * n + 1)},
        compiler_params=pltpu.CompilerParams(has_side_effects=DATAFLOW),
    )(*ops)
    return res[0], res[1], res[2:2 + n], res[2 + n:2 + 2 * n], res[2 + 2 * n]


def _chip_wait(send_sems, recv_sems, parts, lands, after, name):
    n = len(parts)

    def body(*refs):
        ins, lnd = refs[:n], refs[n:2 * n]
        ssem, rsem = refs[2 * n], refs[2 * n + 1]
        for cp in _chip_copies(ins, lnd, ssem, rsem):
            cp.wait_send()
            cp.wait_recv()

    res = _pcall(
        body, name=name,
        in_specs=[HBM_SPEC] * (2 * n) + [SEM_SPEC, SEM_SPEC, pl.BlockSpec(memory_space=pl.ANY)],
        out_specs=[HBM_SPEC] * (2 * n),
        out_shape=[pltpu.HBM(o.shape, o.dtype) for o in list(parts) + list(lands)],
        input_output_aliases={i: i for i in range(2 * n)},
        compiler_params=pltpu.CompilerParams(has_side_effects=DATAFLOW),
    )(*parts, *lands, send_sems, recv_sems, after)
    return res[:n], res[n:]


def _pair_share(halves, name):
    n = len(halves)

    def body(*refs):
        ins, outs = refs[:n], refs[n:2 * n]
        send_sems, recv_sems = refs[2 * n:]
        x, y, c, _ = _place()
        cps = [pltpu.make_async_remote_copy(
            src_ref=ins[t], dst_ref=outs[t], send_sem=send_sems.at[t], recv_sem=recv_sems.at[t],
            device_id=(x, y, 1 - c), device_id_type=MESH) for t in range(n)]
        for cp in cps:
            cp.start()
        for cp in cps:
            cp.wait()

    return _pcall(
        body, name=name, in_specs=[HBM_SPEC] * n, out_specs=[HBM_SPEC] * n,
        out_shape=[jax.ShapeDtypeStruct(h.shape, h.dtype) for h in halves],
        scratch_shapes=[pltpu.SemaphoreType.DMA((n,)), pltpu.SemaphoreType.DMA((n,))],
    )(*halves)


def _add_pair(grad, recv, c, name):
    s, _, rh, cc = grad.shape
    tr = _pick(rh, (256, 128, 64, 32, 16))

    def body(c_ref, g_ref, r_ref, o_ref):
        o_ref[...] = (g_ref[...].astype(F32) + r_ref[...].astype(F32)).astype(o_ref.dtype)

    return _pcall(
        body, name=name,
        grid_spec=pltpu.PrefetchScalarGridSpec(
            num_scalar_prefetch=1, grid=(s, rh // tr),
            in_specs=[pl.BlockSpec((None, None, tr, cc), lambda a, i, cr: (a, cr[0], i, 0)),
                      pl.BlockSpec((None, tr, cc), lambda a, i, cr: (a, i, 0))],
            out_specs=pl.BlockSpec((None, tr, cc), lambda a, i, cr: (a, i, 0))),
        out_shape=jax.ShapeDtypeStruct((s, rh, cc), BF16),
        compiler_params=_params(("parallel", "parallel")),
    )(c, grad, recv)


def _add_chips(part, recv, me, name):
    _, rh, cc = part.shape
    tr = _pick(rh, (256, 128, 64, 32, 16))

    def body(m_ref, p_ref, r_ref, o_ref):
        o_ref[...] = ((p_ref[...].astype(F32) + r_ref[0].astype(F32)) + r_ref[1].astype(F32)) + r_ref[2].astype(F32)

    return _pcall(
        body, name=name,
        grid_spec=pltpu.PrefetchScalarGridSpec(
            num_scalar_prefetch=1, grid=(rh // tr,),
            in_specs=[pl.BlockSpec((None, tr, cc), lambda i, mr: (mr[0], i, 0)),
                      pl.BlockSpec((3, tr, cc), lambda i, mr: (0, i, 0))],
            out_specs=pl.BlockSpec((tr, cc), lambda i, mr: (i, 0))),
        out_shape=jax.ShapeDtypeStruct((rh, cc), F32),
        compiler_params=_params(("parallel",)),
    )(me, part, recv)


def _all_sum(vec):
    rows = vec.shape[0]

    def body(v_ref, o_ref, buf, send_sems, recv_sems):
        x, y, c, _ = _place()
        me = 4 * x + 2 * y + c
        buf[me] = v_ref[...]
        cps = []
        for r in range(1, 8):
            fx, fy, fc = (r >> 2) & 1, (r >> 1) & 1, r & 1
            peer = (x ^ fx, y ^ fy, c ^ fc)
            cps.append(pltpu.make_async_remote_copy(
                src_ref=v_ref, dst_ref=buf.at[me], send_sem=send_sems.at[r - 1], recv_sem=recv_sems.at[r - 1],
                device_id=peer, device_id_type=MESH))
        for cp in cps:
            cp.start()
        for r in range(1, 8):
            src = me ^ r
            pltpu.make_async_remote_copy(
                src_ref=v_ref, dst_ref=buf.at[src], send_sem=send_sems.at[r - 1], recv_sem=recv_sems.at[r - 1],
                device_id=(x, y, c), device_id_type=MESH).wait_recv()
        for cp in cps:
            cp.wait_send()
        acc = buf[0]
        for d in range(1, 8):
            acc = acc + buf[d]
        o_ref[...] = acc

    return _pcall(
        body, name="all_sum_small",
        in_specs=[pl.BlockSpec(memory_space=pltpu.VMEM)], out_specs=pl.BlockSpec(memory_space=pltpu.VMEM),
        out_shape=jax.ShapeDtypeStruct((rows, LANES), F32),
        scratch_shapes=[pltpu.VMEM((8, rows, LANES), F32), pltpu.SemaphoreType.DMA((7,)), pltpu.SemaphoreType.DMA((7,))],
    )(vec)


def _pack(parts):
    flat = jnp.concatenate([p.reshape(-1) for p in parts])
    tile = SUBLANES * LANES
    pad = (-flat.shape[0]) % tile
    return jnp.pad(flat, (0, pad)).reshape(-1, LANES)


def _unpack(vec, shapes):
    flat = vec.reshape(-1)
    out, p = [], 0
    for s in shapes:
        n = 1
        for d in s:
            n *= d
        out.append(flat[p:p + n].reshape(s))
        p += n
    return out


def _local_step(x, tgt, norm_mix, norm_ffn, lb8, out_norm, final_norm, sc_conv, ffn_conv, weights, reduce):
    T, D = x.shape
    F2 = ffn_conv.shape[-1]
    FF = F2 // 2
    tm = _pick(T, (1024, 512, 256, 128))
    wide = (1536, 1408, 1024, 768, 512, 384, 256, 128)
    cw_h, cw_s, cw_u = 4 * D // N_CHIPS, 3 * D // N_CHIPS, F2 // N_CHIPS
    kp = FF // N_CHIPS
    tk_ff = kp if kp % LANES == 0 else LANES
    tn_d = _pick(D, (1024, 512, 256, 128))
    tk_w = _pick(D, (512, 256, 128))
    tn_h = _pick(cw_h, (1024, 512, 256, 128))
    tn_s = _pick(D // N_CHIPS, (512, 256, 128))
    tn_u = _pick(cw_u, wide)
    lb = lb8[0:1]
    wm_sq = _wmap_col(D, tn_d, 0)
    wm_sq1 = _wmap_col(D, D, 0)
    seg1 = lambda a: a.reshape((1,) + a.shape)

    def mix_in(h, w):
        return _row_call(_rms_fwd_fn, [(h, 0, D)], [w], [(D, BF16)], 0, "rms_fwd")[0]

    def rms_bwd(h, dxn, dh, w):
        return _row_call(_rms_bwd_fn, [(h, 0, D), (dxn, 0, D), (dh, 0, D)], [w], [(D, F32), (D, BF16)], 1, "rms_bwd")

    def ffn_fwd(h, i, w_up, w_down):
        xn = mix_in(h, norm_ffn[i:i + 1])
        tn = _pick(cw_u, wide)
        up = _mm_nn(xn, w_up, _wmap_col(cw_u, tn, 0), D, F2, tm, D, tn, "ffn_up")
        nb = FF // LANES
        a = _col_call(_glu_fwd_fn, [(up, 0), (up, nb)], [(ffn_conv[i], 0), (ffn_conv[i], nb)], [(1, FF, BF16)], 0,
                      "glu_fwd", before=True, after=False)[0][0]
        h2 = _mm_nn(a, w_down, _wmap_row(kp, tk_ff, 0), FF, D, tm, tk_ff, tn_d, "ffn_down", res=h)
        return h2, (xn, up, a)

    def ffn_bwd(dh, dh16, h, saved, i, w_up, w_down):
        xn, up, a = saved
        da = _mm_nt(seg1(dh16), w_down, _wmap_row(kp, tk_ff, 0), FF, D, tm, tk_ff, D, "ffn_down_dx")
        nb = FF // LANES
        dgv, cg, cv = _col_call(_glu_bwd_fn, [(up, 0), (up, nb), (da, 0)], [(ffn_conv[i], 0), (ffn_conv[i], nb)],
                                [(2, FF, BF16)], 2, "glu_bwd", before=True, after=True)
        dxn = _mm_nt(dgv, w_up, _wmap_col(cw_u, tn_u, 0), D, F2, tm, D, tn_u, "ffn_up_dx")
        g_down = _mm_tn(a, seg1(dh16), (N_CHIPS, 1, kp, D), _wmap_row(kp, tk_ff, 0), FF, D, tk_ff, tn_d,
                        "ffn_down_dw", tm=_pick(T, (2048, 1024, 512, 256, 128)))
        g_up = _mm_tn(xn, dgv, (N_CHIPS, 1, D, cw_u), _wmap_col(cw_u, tn_u, 0), D, F2, tk_w, tn_u, "ffn_up_dw")
        dh2, dh2_16, dnw = rms_bwd(h, dxn, dh, norm_ffn[i:i + 1])
        return dh2, dh2_16, dnw, jnp.concatenate([cg[:3], cv[:3]], axis=1), g_up, g_down

    h0 = x
    xn0 = mix_in(h0, norm_mix[0:1])
    w_hin, = weights(0, xn0)
    proj = _mm_nn(xn0, w_hin, _wmap_col(cw_h, tn_h, 0), D, 4 * D, tm, D, tn_h, "hgrn_in")
    o, states = _hgrn_fwd(proj, lb, D)
    on = _row_call(_onorm_fwd_fn, [(o, 0, D), (proj, 3, D)], [out_norm], [(D, BF16)], 0, "onorm_fwd")[0]
    w_hout, w_up0, w_down0 = weights(1, on)
    w_hout1 = w_hout.reshape(1, D, D)
    h1 = _mm_nn(on, w_hout1, wm_sq, D, D, tm, D, tn_d, "hgrn_out", res=h0)
    h2, ffn0 = ffn_fwd(h1, 0, w_up0, w_down0)
    xn1 = mix_in(h2, norm_mix[1:2])
    w_sin, w_sout, w_up1, w_down1 = weights(2, xn1)
    w_sout1 = w_sout.reshape(1, D, D)
    tn_si = _pick(cw_s, wide)
    sproj = _mm_nn(xn1, w_sin, _wmap_col(cw_s, tn_si, 0), D, 3 * D, tm, D, tn_si, "sc_in")
    nd = D // LANES
    ysc = _col_call(_sc_fwd_fn, [(sproj, 0), (sproj, nd), (sproj, 2 * nd)], [(sc_conv, 0)], [(1, D, BF16)], 0,
                    "sc_fwd", before=True, after=False)[0][0]
    h3 = _mm_nn(ysc, w_sout1, wm_sq, D, D, tm, D, tn_d, "sc_out", res=h2)
    h4, ffn1 = ffn_fwd(h3, 1, w_up1, w_down1)

    dh, dh16, esq, dfinal = _row_call(_final_fn, [(h4, 0, D), (tgt, 0, D)], [final_norm], [(D, F32), (D, BF16)], 2,
                                      "final_loss")
    loss = 0.5 / D * jnp.sum(esq)
    dh, dh16, dnf1, dconv1, g_up1, g_down1 = ffn_bwd(dh, dh16, h3, ffn1, 1, w_up1, w_down1)
    dh16 = reduce([("ffn_w_up", 1), ("ffn_w_down", 1)], [g_up1, g_down1], dh16)
    dy = _mm_nt(seg1(dh16), w_sout1, wm_sq1, D, D, tm, D, D, "sc_out_dx")
    dsp, dscc = _col_call(_sc_bwd_fn, [(sproj, 0), (sproj, nd), (sproj, 2 * nd), (dy, 0)], [(sc_conv, 0)],
                          [(3, D, BF16)], 1, "sc_bwd", before=True, after=True)
    dxn = _mm_nt(dsp, w_sin, _wmap_col(cw_s, tn_s, 0), D, 3 * D, tm, D, tn_s, "sc_in_dx", per_step=3)
    g_sout = _mm_tn(ysc, seg1(dh16), (1, 1, D, D), wm_sq, D, D, tk_w, tn_d, "sc_out_dw")
    g_sin = _mm_tn(xn1, dsp, (N_CHIPS, 1, D, cw_s), _wmap_col(cw_s, tn_s, 0), D, 3 * D, tk_w, tn_s, "sc_in_dw")
    dh, dh16, dnm1 = rms_bwd(h2, dxn, dh, norm_mix[1:2])
    dh16 = reduce([("sc_w_in", 0), ("sc_w_out", 0)], [g_sin, g_sout], dh16)
    dh, dh16, dnf0, dconv0, g_up0, g_down0 = ffn_bwd(dh, dh16, h1, ffn0, 0, w_up0, w_down0)
    dh16 = reduce([("ffn_w_up", 0), ("ffn_w_down", 0)], [g_up0, g_down0], dh16)
    don = _mm_nt(seg1(dh16), w_hout1, wm_sq1, D, D, tm, D, D, "hgrn_out_dx")
    do, dgate, dgain = _row_call(_onorm_bwd_fn, [(o, 0, D), (proj, 3, D), (don, 0, D)], [out_norm],
                                 [(D, F32), (D, BF16)], 1, "onorm_bwd")
    dproj, dlb = _hgrn_bwd(proj, lb, states, do, dgate, D)
    dxn = _mm_nt(dproj, w_hin, _wmap_col(cw_h, tn_h, 0), D, 4 * D, tm, D, tn_h, "hgrn_in_dx", per_step=2)
    g_hout = _mm_tn(on, seg1(dh16), (1, 1, D, D), wm_sq, D, D, tk_w, tn_d, "hgrn_out_dw")
    g_hin = _mm_tn(xn0, dproj, (N_CHIPS, 1, D, cw_h), _wmap_col(cw_h, tn_h, 0), D, 4 * D, tk_w, tn_h, "hgrn_in_dw")
    grad_x, _, dnm0 = rms_bwd(h0, dxn, dh, norm_mix[0:1])
    grad_x = reduce([("hgrn_w_in", 0), ("hgrn_w_out", 0)], [g_hin, g_hout], grad_x)

    small = dict(
        loss=loss,
        norm_mix=jnp.stack([jnp.sum(dnm0, axis=0), jnp.sum(dnm1, axis=0)]),
        norm_ffn=jnp.stack([jnp.sum(dnf0, axis=0), jnp.sum(dnf1, axis=0)]),
        lb=dlb[0:1],
        out_norm=jnp.sum(dgain, axis=0)[None],
        final_norm=jnp.sum(dfinal, axis=0),
        sc_conv=dscc[:3],
        ffn_conv=jnp.stack([dconv0, dconv1]),
    )
    return grad_x, small


def kernel(x, norm_mix, norm_ffn, hgrn_w_in, hgrn_lb_table, hgrn_out_norm, hgrn_w_out, sc_w_in, sc_conv, sc_w_out, ffn_w_up, ffn_conv, ffn_w_down, final_norm, loss_target, m_norm_mix, m_norm_ffn, m_hgrn_w_in, m_hgrn_lb_table, m_hgrn_out_norm, m_hgrn_w_out, m_sc_w_in, m_sc_conv, m_sc_w_out, m_ffn_w_up, m_ffn_conv, m_ffn_w_down, m_final_norm, v_norm_mix, v_norm_ffn, v_hgrn_w_in, v_hgrn_lb_table, v_hgrn_out_norm, v_hgrn_w_out, v_sc_w_in, v_sc_conv, v_sc_w_out, v_ffn_w_up, v_ffn_conv, v_ffn_w_down, v_final_norm):
    D = x.shape[-1]
    xi, yi, ci = lax.axis_index("x"), lax.axis_index("y"), lax.axis_index("c")
    me_chip = (2 * xi + yi).astype(jnp.int32).reshape(1)
    me_core = ci.astype(jnp.int32).reshape(1)

    big_names = ["hgrn_w_in", "hgrn_w_out", "sc_w_in", "sc_w_out", "ffn_w_up", "ffn_w_down"]
    big_w = dict(hgrn_w_in=hgrn_w_in, hgrn_w_out=hgrn_w_out, sc_w_in=sc_w_in, sc_w_out=sc_w_out,
                 ffn_w_up=ffn_w_up, ffn_w_down=ffn_w_down)
    big_m = dict(hgrn_w_in=m_hgrn_w_in, hgrn_w_out=m_hgrn_w_out, sc_w_in=m_sc_w_in, sc_w_out=m_sc_w_out,
                 ffn_w_up=m_ffn_w_up, ffn_w_down=m_ffn_w_down)
    big_v = dict(hgrn_w_in=v_hgrn_w_in, hgrn_w_out=v_hgrn_w_out, sc_w_in=v_sc_w_in, sc_w_out=v_sc_w_out,
                 ffn_w_up=v_ffn_w_up, ffn_w_down=v_ffn_w_down)
    flat2 = lambda a: a.reshape(-1, a.shape[-1])

    sh = lambda a: a.reshape(-1, a.shape[-1]).astype(BF16)
    conv_shards = [flat2(sc_conv), flat2(ffn_conv)]
    w_hin, scc4, fcc4 = _gather_weights([sh(hgrn_w_in)], conv_shards)
    scc = jnp.moveaxis(scc4, 0, 1).reshape(3, D)
    f2 = ffn_conv.shape[-1] * N_CHIPS
    fcc = jnp.moveaxis(fcc4.reshape(N_CHIPS, 2, 3, -1), 0, 2).reshape(2, 3, f2)
    stage1 = [sh(hgrn_w_out), sh(ffn_w_up[0]), sh(ffn_w_down[0])]
    stage2 = [sh(sc_w_in), sh(sc_w_out), sh(ffn_w_up[1]), sh(ffn_w_down[1])]
    ss1, rs1, src1, land1, w_hin = _gather_start(stage1, w_hin, "gather_start_1")
    ss2, rs2, src2, land2, w_hin = _gather_start(stage2, w_hin, "gather_start_2")

    def weights(stage, after):
        if stage == 0:
            return (w_hin,)
        if stage == 1:
            return _gather_forward(_gather_wait(ss1, rs1, src1, land1, after, "gather_wait_1"), "gather_forward_1")
        return _gather_forward(_gather_wait(ss2, rs2, src2, land2, after, "gather_wait_2"), "gather_forward_2")

    pending = []

    def reduce(slots, grads, thru):
        k = len(pending)
        halves = [g.reshape(N_CHIPS, 2, -1, g.shape[-1]) for g in grads]
        recv = _pair_exchange(halves, "grad_pair_exchange_%d" % k)
        pair = [_add_pair(g, r, me_core, "grad_add_pair") for g, r in zip(halves, recv)]
        ss, rs, pair, land, thru = _chip_start(pair, thru, "grad_chip_start_%d" % k)
        pending.append((slots, ss, rs, pair, land))
        return thru

    lb8 = _lb_softmax(hgrn_lb_table)
    grad_x, small = _local_step(
        x[0], loss_target[0], norm_mix, norm_ffn, lb8, hgrn_out_norm, final_norm[None], scc, fcc, weights, reduce)

    small_names = ["loss", "norm_mix", "norm_ffn", "lb", "out_norm", "final_norm", "sc_conv", "ffn_conv"]
    parts = [small[n].astype(F32) for n in small_names]
    shapes = [p.shape for p in parts]
    tot = dict(zip(small_names, _unpack(_all_sum(_pack(parts)), shapes)))
    loss = tot["loss"].reshape(())
    g_lb_table = _lb_table_grad(lb8, tot["lb"], hgrn_lb_table.shape[0])
    cw = sc_conv.shape[-1]
    g_sc_conv = lax.dynamic_slice_in_dim(tot["sc_conv"], me_chip[0] * cw, cw, axis=1)[None]
    cf = ffn_conv.shape[-1]
    g_ffn_conv = lax.dynamic_slice_in_dim(tot["ffn_conv"], me_chip[0] * cf, cf, axis=2)
    g_small = dict(norm_mix=tot["norm_mix"], norm_ffn=tot["norm_ffn"], hgrn_lb_table=g_lb_table,
                   hgrn_out_norm=tot["out_norm"], sc_conv=g_sc_conv, ffn_conv=g_ffn_conv, final_norm=tot["final_norm"])
    w_small = dict(norm_mix=norm_mix, norm_ffn=norm_ffn, hgrn_lb_table=hgrn_lb_table, hgrn_out_norm=hgrn_out_norm,
                   sc_conv=sc_conv, ffn_conv=ffn_conv, final_norm=final_norm)
    m_small = dict(norm_mix=m_norm_mix, norm_ffn=m_norm_ffn, hgrn_lb_table=m_hgrn_lb_table, hgrn_out_norm=m_hgrn_out_norm,
                   sc_conv=m_sc_conv, ffn_conv=m_ffn_conv, final_norm=m_final_norm)
    v_small = dict(norm_mix=v_norm_mix, norm_ffn=v_norm_ffn, hgrn_lb_table=v_hgrn_lb_table, hgrn_out_norm=v_hgrn_out_norm,
                   sc_conv=v_sc_conv, ffn_conv=v_ffn_conv, final_norm=v_final_norm)
    sm_names = list(g_small)
    sm_shapes = [w_small[n].shape for n in sm_names]
    d_s, m_s, v_s = _adamw(_pack([w_small[n] for n in sm_names]), _pack([g_small[n] for n in sm_names]),
                           _pack([m_small[n] for n in sm_names]), _pack([v_small[n] for n in sm_names]), "adamw_small")
    out_g, out_d, out_m, out_v = dict(g_small), {}, {}, {}
    for n, d_, m_, v_ in zip(sm_names, _unpack(d_s, sm_shapes), _unpack(m_s, sm_shapes), _unpack(v_s, sm_shapes)):
        out_d[n], out_m[n], out_v[n] = d_, m_, v_

    done = {}
    after = grad_x
    for k, (slots, ss, rs, pair, land) in enumerate(pending):
        pair, recv = _chip_wait(ss, rs, pair, land, after, "grad_chip_wait_%d" % k)
        mine = [_add_chips(p, r, me_chip, "grad_add_chips") for p, r in zip(pair, recv)]
        theirs = _pair_share(mine, "grad_pair_share_%d" % k)
        for (n, layer), gm, gr in zip(slots, mine, theirs):
            done[n] = _adamw_halves(flat2(big_w[n]), flat2(big_m[n]), flat2(big_v[n]), gm, gr, me_core, "adamw_" + n,
                                    layer=layer, prev=done.get(n))
        after = theirs[0]
    for n in big_names:
        out_g[n], out_d[n], out_m[n], out_v[n] = (a.reshape(big_w[n].shape) for a in done[n])

    order = ["norm_mix", "norm_ffn", "hgrn_w_in", "hgrn_lb_table", "hgrn_out_norm", "hgrn_w_out", "sc_w_in", "sc_conv",
             "sc_w_out", "ffn_w_up", "ffn_conv", "ffn_w_down", "final_norm"]
    return (loss, grad_x[None], *[out_g[n] for n in order], *[out_d[n] for n in order],
            *[out_m[n] for n in order], *[out_v[n] for n in order])
```

```python
import functools

import jax
import jax.numpy as jnp
from jax import lax
from jax.experimental import pallas as pl
from jax.experimental.pallas import tpu as pltpu

F32 = jnp.float32
BF16 = jnp.bfloat16
MESH = pl.DeviceIdType.MESH

EPS = 1e-6
CHUNK = 64
HEAD = 128
N_CHIPS = 4
ADAM_LR, ADAM_B1, ADAM_B2, ADAM_EPS, ADAM_WD, ADAM_STEP = 0.001, 0.9, 0.999, 1e-08, 0.01, 10
VMEM_LIMIT = 56 * 1024 * 1024
SUBLANES = 8
LANES = 128


def _pcall(body, **kw):
    return pl.pallas_call(body, **kw)


def _params(sem, vmem=VMEM_LIMIT):
    return pltpu.CompilerParams(dimension_semantics=sem, vmem_limit_bytes=vmem)


def _pick(dim, prefs):
    for p in prefs:
        if p <= dim and dim % p == 0:
            return p
    return dim


def _sigmoid(x):
    return 1.0 / (1.0 + jnp.exp(-x))


def _wmap_col(cw, tn, r0):
    bps = cw // tn
    return lambda kb, nb: (nb // bps, r0 + kb, nb % bps)


def _wmap_row(kp, tk, r0):
    bps = kp // tk
    return lambda kb, nb: (kb // bps, r0 + kb % bps, nb)


def _mm_nn(a, w3, wmap, K, N, tm, tk, tn, name, res=None):
    M = a.shape[0]
    nk = K // tk

    def body(*refs):
        if res is None:
            a_ref, w_ref, o_ref = refs[:3]
        else:
            a_ref, w_ref, r_ref, o_ref = refs[:4]
        p = jnp.dot(a_ref[...], w_ref[...], preferred_element_type=F32)
        if nk == 1:
            o_ref[...] = p if res is None else p + r_ref[...]
            return
        acc = refs[-1]
        k = pl.program_id(2)

        @pl.when(k == 0)
        def _():
            acc[...] = p

        @pl.when(k > 0)
        def _():
            acc[...] += p

        @pl.when(k == nk - 1)
        def _():
            o_ref[...] = acc[...] if res is None else acc[...] + r_ref[...]

    if nk == 1:
        grid = (M // tm, N // tn)
        ix = lambda f: (lambda i, j: f(i, j, 0))
        sem = ("parallel", "parallel")
        scratch = []
    else:
        grid = (M // tm, N // tn, nk)
        ix = lambda f: f
        sem = ("parallel", "parallel", "arbitrary")
        scratch = [pltpu.VMEM((tm, tn), F32)]
    in_specs = [pl.BlockSpec((tm, tk), ix(lambda i, j, k: (i, k))),
                pl.BlockSpec((None, tk, tn), ix(lambda i, j, k: wmap(k, j)))]
    args = [a, w3]
    if res is not None:
        in_specs.append(pl.BlockSpec((tm, tn), ix(lambda i, j, k: (i, j))))
        args.append(res)
    return _pcall(
        body, name=name, grid=grid, in_specs=in_specs,
        out_specs=pl.BlockSpec((tm, tn), ix(lambda i, j, k: (i, j))),
        out_shape=jax.ShapeDtypeStruct((M, N), F32), scratch_shapes=scratch, compiler_params=_params(sem),
    )(*args)


def _mm_nt(dy3, w3, wmap, K, N, tm, tk, tn, name, per_step=1):
    M = dy3.shape[1]
    bps = dy3.shape[2] // tn
    u = per_step
    grid = (M // tm, K // tk, N // (tn * u))
    nn = grid[2]

    def body(*refs):
        o_ref = refs[-1]
        p = None
        for r in range(u):
            d = lax.dot_general(refs[r][...], refs[u + r][...], (((1,), (1,)), ((), ())), preferred_element_type=F32)
            p = d if p is None else p + d
        if nn == 1:
            o_ref[...] = p
            return
        n = pl.program_id(2)

        @pl.when(n == 0)
        def _():
            o_ref[...] = p

        @pl.when(n > 0)
        def _():
            o_ref[...] += p

    def dy_spec(r):
        return pl.BlockSpec((None, tm, tn), lambda i, j, n: ((n * u + r) // bps, i, (n * u + r) % bps))

    def w_spec(r):
        return pl.BlockSpec((None, tk, tn), lambda i, j, n: wmap(j, n * u + r))

    return _pcall(
        body, name=name, grid=grid,
        in_specs=[dy_spec(r) for r in range(u)] + [w_spec(r) for r in range(u)],
        out_specs=pl.BlockSpec((tm, tk), lambda i, j, n: (i, j)),
        out_shape=jax.ShapeDtypeStruct((M, K), F32),
        compiler_params=_params(("parallel", "parallel", "arbitrary")),
    )(*([dy3] * u), *([w3] * u))


def _mm_tn(x, dy3, shape4, wmap, K, N, tk, tn, name, tm=None):
    M = x.shape[0]
    tm = M if tm is None else tm
    nm = M // tm
    bps = dy3.shape[2] // tn

    def body(*refs):
        x_ref, dy_ref = refs[:2]
        p = lax.dot_general(x_ref[...], dy_ref[...], (((0,), (0,)), ((), ())), preferred_element_type=F32)
        if nm == 1:
            o_ref = refs[-1]
            o_ref[...] = p.astype(o_ref.dtype)
            return
        o_ref, acc = refs[-2:]
        m = pl.program_id(2)

        @pl.when(m == 0)
        def _():
            acc[...] = p

        @pl.when(m > 0)
        def _():
            acc[...] += p

        @pl.when(m == nm - 1)
        def _():
            o_ref[...] = acc[...].astype(o_ref.dtype)

    def omap(i, j, m):
        s, rb, cb = wmap(i, j)
        return (s, 0, rb, cb)

    return _pcall(
        body, name=name, grid=(K // tk, N // tn, nm),
        in_specs=[pl.BlockSpec((tm, tk), lambda i, j, m: (m, i)),
                  pl.BlockSpec((None, tm, tn), lambda i, j, m: (j // bps, m, j % bps))],
        out_specs=pl.BlockSpec((None, None, tk, tn), omap),
        out_shape=jax.ShapeDtypeStruct(shape4, BF16),
        scratch_shapes=[] if nm == 1 else [pltpu.VMEM((tk, tn), F32)],
        compiler_params=_params(("parallel", "parallel", "arbitrary")),
    )(x, dy3)


def _row_call(fn, rows, vecs, outs, n_acc, name, t_rows=256, sub=16, per_trip=4):
    T = rows[0][0].shape[0]
    t_rows = min(t_rows, T)
    nsub = t_rows // sub
    n_r, n_v, n_o = len(rows), len(vecs), len(outs)
    width = rows[0][2]

    def body(*refs):
        r_refs = refs[:n_r]
        v_refs = refs[n_r:n_r + n_v]
        o_refs = refs[n_r + n_v:n_r + n_v + n_o]
        a_refs = refs[n_r + n_v + n_o:]

        @pl.when(pl.program_id(0) == 0)
        def _():
            for a in a_refs:
                a[...] = jnp.zeros_like(a)

        vv = [v[...] for v in v_refs]

        def step(i, carry):
            done = []
            for u in range(per_trip):
                sl = pl.ds(pl.multiple_of((i * per_trip + u) * sub, sub), sub)
                done.append((sl,) + tuple(fn([r[sl, :] for r in r_refs], vv)))
            for sl, o_vals, a_vals in done:
                for o, val in zip(o_refs, o_vals):
                    o[sl, :] = val.astype(o.dtype)
            for a_i, a in enumerate(a_refs):
                tot = None
                for _, _, a_vals in done:
                    part = a_vals[a_i].reshape(sub // SUBLANES, SUBLANES, a_vals[a_i].shape[-1]).sum(axis=0)
                    tot = part if tot is None else tot + part
                a[...] += tot
            return carry

        lax.fori_loop(0, nsub // per_trip, step, 0)

    in_specs = [pl.BlockSpec((t_rows, w), functools.partial(lambda i, cb: (i, cb), cb=cb)) for _, cb, w in rows]
    in_specs += [pl.BlockSpec(v.shape, lambda i: (0, 0)) for v in vecs]
    out_specs = [pl.BlockSpec((t_rows, w), lambda i: (i, 0)) for w, _ in outs]
    out_specs += [pl.BlockSpec((SUBLANES, width), lambda i: (0, 0)) for _ in range(n_acc)]
    out_shape = [jax.ShapeDtypeStruct((T, w), dt) for w, dt in outs]
    out_shape += [jax.ShapeDtypeStruct((SUBLANES, width), F32) for _ in range(n_acc)]
    return _pcall(
        body, name=name, grid=(T // t_rows,), in_specs=in_specs, out_specs=out_specs, out_shape=out_shape,
        compiler_params=_params(("arbitrary",)),
    )(*[r[0] for r in rows], *vecs)


def _rms_fwd_fn(rv, vv):
    h, = rv
    w, = vv
    r = lax.rsqrt(jnp.mean(h * h, axis=-1, keepdims=True) + EPS)
    return [h * r * w], []


def _rms_bwd_fn(rv, vv):
    h, dxn, dh_in = rv
    w, = vv
    d = h.shape[-1]
    r = lax.rsqrt(jnp.mean(h * h, axis=-1, keepdims=True) + EPS)
    gy = dxn * w
    dh = r * gy - h * ((r * r * r) * (1.0 / d) * jnp.sum(gy * h, axis=-1, keepdims=True))
    return [dh_in + dh] * 2, [dxn * h * r]


def _final_fn(rv, vv):
    h, tgt = rv
    w, = vv
    d = h.shape[-1]
    r = lax.rsqrt(jnp.mean(h * h, axis=-1, keepdims=True) + EPS)
    hn = h * r
    e = hn * w - tgt
    dy = e * (1.0 / d)
    gy = dy * w
    dh = r * gy - h * ((r * r * r) * (1.0 / d) * jnp.sum(gy * h, axis=-1, keepdims=True))
    return [dh] * 2, [e * e, dy * hn]


def _onorm_fwd_fn(rv, vv):
    o, g = rv
    gain, = vv
    r = lax.rsqrt(jnp.mean(o * o, axis=-1, keepdims=True) + EPS)
    return [o * r * gain * (g * _sigmoid(g))], []


def _onorm_bwd_fn(rv, vv):
    o, g, don = rv
    gain, = vv
    d = o.shape[-1]
    r = lax.rsqrt(jnp.mean(o * o, axis=-1, keepdims=True) + EPS)
    sg = _sigmoid(g)
    sl = g * sg
    n = o * r
    dg = don * n * gain * (sg * (1.0 + g * (1.0 - sg)))
    gy = don * sl * gain
    do = r * gy - o * ((r * r * r) * (1.0 / d) * jnp.sum(gy * o, axis=-1, keepdims=True))
    return [do, dg], [don * sl * n]


HALO = SUBLANES


def _col_call(fn, cols, vecs, outs, n_acc, name, before, after, tc=LANES, chunk=128):
    T = cols[0][0].shape[0]
    chunk = min(chunk, T)
    nch = T // chunk
    ncol = outs[0][1] // tc
    n_c, n_v, n_o = len(cols), len(vecs), len(outs)
    hb = HALO if before else 0
    rw = chunk + hb + (HALO if after else 0)

    def body(*refs):
        c_refs = refs[:n_c]
        v_refs = refs[n_c:n_c + n_v]
        o_refs = refs[n_c + n_v:n_c + n_v + n_o]
        a_refs = refs[n_c + n_v + n_o:]
        vv = [v[...] for v in v_refs]
        wrow = lax.broadcasted_iota(jnp.int32, (rw, tc), 0)
        inside = (wrow >= hb) & (wrow < hb + chunk)

        def step(i, carry):
            r0 = pl.multiple_of(i * chunk, chunk)
            wins = []
            for ref in c_refs:
                parts = []
                if before:
                    pb = ref[pl.ds(pl.multiple_of(jnp.maximum(r0 - HALO, 0), HALO), HALO), :]
                    parts.append(jnp.where(i > 0, pb, 0.0))
                parts.append(ref[pl.ds(r0, chunk), :])
                if after:
                    pa = ref[pl.ds(pl.multiple_of(jnp.minimum(r0 + chunk, T - HALO), HALO), HALO), :]
                    parts.append(jnp.where(i < nch - 1, pa, 0.0))
                wins.append(jnp.concatenate(parts, axis=0) if len(parts) > 1 else parts[0])
            o_vals, a_vals = fn(wins, vv, inside)
            p = 0
            for o, (nseg, _, _) in zip(o_refs, outs):
                for s in range(nseg):
                    o[s, pl.ds(r0, chunk), :] = o_vals[p][hb:hb + chunk].astype(o.dtype)
                    p += 1
            return tuple(c + a for c, a in zip(carry, a_vals))

        taps = [v.shape[0] for v, _ in vecs][:n_acc]
        init = tuple(jnp.zeros((1, tc), F32) for k in taps for _ in range(k))
        sums = lax.fori_loop(0, nch, step, init)
        arow = lax.broadcasted_iota(jnp.int32, (SUBLANES, tc), 0)
        p = 0
        for a, k in zip(a_refs, taps):
            acc = jnp.zeros((SUBLANES, tc), F32)
            for t in range(k):
                acc = jnp.where(arow == t, sums[p], acc)
                p += 1
            a[...] = acc

    in_specs = [pl.BlockSpec((T, tc), functools.partial(lambda j, off: (0, off + j), off=off)) for _, off in cols]
    in_specs += [pl.BlockSpec((v.shape[0], tc), functools.partial(lambda j, off: (0, off + j), off=off))
                 for v, off in vecs]
    out_specs = [pl.BlockSpec((nseg, T, tc), lambda j: (0, 0, j)) for nseg, _, _ in outs]
    out_specs += [pl.BlockSpec((SUBLANES, tc), lambda j: (0, j)) for _ in range(n_acc)]
    out_shape = [jax.ShapeDtypeStruct((nseg, T, w), dt) for nseg, w, dt in outs]
    out_shape += [jax.ShapeDtypeStruct((SUBLANES, ncol * tc), F32) for _ in range(n_acc)]
    return _pcall(
        body, name=name, grid=(ncol,), in_specs=in_specs, out_specs=out_specs, out_shape=out_shape,
        compiler_params=_params(("parallel",)),
    )(*[c[0] for c in cols], *[v[0] for v in vecs])


def _down(x, k):
    return x if k == 0 else pltpu.roll(x, k, 0)


def _up(x, k):
    return x if k == 0 else pltpu.roll(x, x.shape[0] - k, 0)


def _lags(x):
    return _down(x, 2), _down(x, 1), x


def _conv(lags, w):
    return w[0:1] * lags[0] + w[1:2] * lags[1] + w[2:3] * lags[2]


def _conv_t(d, w):
    return w[2:3] * d + w[1:2] * _up(d, 1) + w[0:1] * _up(d, 2)


def _tap_sums(d, lags, inside):
    dm = jnp.where(inside, d, 0.0)
    return [jnp.sum(dm * lag, axis=0, keepdims=True) for lag in lags]


def _glu_fwd_fn(wins, vv, inside):
    xg, xv = wins
    wg, wv = vv
    ug = _conv(_lags(xg), wg)
    uv = _conv(_lags(xv), wv)
    return [ug * _sigmoid(ug) * uv], []


def _glu_bwd_fn(wins, vv, inside):
    xg, xv, da = wins
    wg, wv = vv
    lg, lv = _lags(xg), _lags(xv)
    ug = _conv(lg, wg)
    uv = _conv(lv, wv)
    sg = _sigmoid(ug)
    dug = da * uv * (sg * (1.0 + ug * (1.0 - sg)))
    duv = da * (ug * sg)
    return [_conv_t(dug, wg), _conv_t(duv, wv)], _tap_sums(dug, lg, inside) + _tap_sums(duv, lv, inside)


def _sc_fwd_fn(wins, vv, inside):
    gb, gc, hh = wins
    w, = vv
    return [gb * _conv(_lags(gc * hh), w)], []


def _sc_bwd_fn(wins, vv, inside):
    gb, gc, hh, dy = wins
    w, = vv
    lz = _lags(gc * hh)
    dcv = dy * gb
    dz = _conv_t(dcv, w)
    return [dy * _conv(lz, w), dz * hh, dz * gc], _tap_sums(dcv, lz, inside)


def _gates(qr, fr, lb):
    sg = _sigmoid(fr)
    f = lb + (1.0 - lb) * sg
    sq = _sigmoid(qr)
    q = qr * sq * (HEAD ** -0.5)
    return q, 1.0 - f, jnp.log(f), f, sg, sq


def _boundary_rows(b, g, row):
    c = b.shape[0]
    if 2 * g >= SUBLANES:
        x = b.reshape(c // (2 * g), 2 * g, LANES)
        return jnp.broadcast_to(x[:, g - 1:g, :], x.shape).reshape(c, LANES)
    x = b.reshape(c // SUBLANES, SUBLANES, LANES)
    lo = jnp.broadcast_to(x[:, 1:2, :], x.shape).reshape(c, LANES)
    hi = jnp.broadcast_to(x[:, 5:6, :], x.shape).reshape(c, LANES)
    return jnp.where((row & 4) == 0, lo, hi)


def _chunk_decays(gl, f, row):
    c = gl.shape[0]
    b = gl
    d = 1
    while d < c:
        b = b + jnp.where(row >= d, pltpu.roll(b, d, 0), 0.0)
        d *= 2
    eq, ek = [], []
    g = c // 2
    while g >= 2:
        right = (row & g) != 0
        m = _boundary_rows(b, g, row)
        z = jnp.exp(jnp.where(right, b - m, m - b))
        eq.append(jnp.where(right, z, 0.0))
        ek.append(jnp.where(right, 0.0, z))
        g //= 2
    odd = (row & 1) != 0
    eq.append(jnp.where(odd, f, 0.0))
    ek.append(jnp.where(odd, 0.0, 1.0))
    return b, eq, ek


def _intra(q, k, eq, ek, tt, ss):
    c = q.shape[0]
    qs, ks = [], []
    a = jnp.where(tt == ss, jnp.sum(q * k, axis=1, keepdims=True), 0.0)
    g = c // 2
    for e_q, e_k in zip(eq, ek):
        qg = (q * e_q).astype(BF16)
        kg = (k * e_k).astype(BF16)
        p = lax.dot_general(qg, kg, (((1,), (1,)), ((), ())), preferred_element_type=F32)
        a = a + (p if 2 * g >= c else jnp.where((tt ^ ss) < 2 * g, p, 0.0))
        qs.append(qg)
        ks.append(kg)
        g //= 2
    return a, qs, ks


def _hgrn_fwd(proj, lb, d_model):
    T = proj.shape[0]
    H = d_model // HEAD
    nch = T // CHUNK

    def body(q_ref, f_ref, v_ref, lb_ref, o_ref, s_ref):
        lbv = lb_ref[...]
        row = lax.broadcasted_iota(jnp.int32, (CHUNK, HEAD), 0)
        tt = lax.broadcasted_iota(jnp.int32, (CHUNK, CHUNK), 0)
        ss = lax.broadcasted_iota(jnp.int32, (CHUNK, CHUNK), 1)

        def step(i, st):
            sl = pl.ds(pl.multiple_of(i * CHUNK, CHUNK), CHUNK)
            q, k, gl, f, _, _ = _gates(q_ref[sl, :], f_ref[sl, :], lbv)
            v = v_ref[sl, :].astype(BF16)
            b, eq, ek = _chunk_decays(gl, f, row)
            a, _, _ = _intra(q, k, eq, ek, tt, ss)
            bl = b[CHUNK - 1:CHUNK, :]
            q0 = (q * jnp.exp(b)).astype(BF16)
            kh = (k * jnp.exp(bl - b)).astype(BF16)
            s_ref[i] = st
            o = jnp.dot(a.astype(BF16), v, preferred_element_type=F32)
            o = o + lax.dot_general(q0, st.astype(BF16), (((1,), (1,)), ((), ())), preferred_element_type=F32)
            o_ref[sl, :] = o
            return jnp.exp(bl) * st + lax.dot_general(v, kh, (((0,), (0,)), ((), ())), preferred_element_type=F32)

        per = 4 if nch % 4 == 0 else 2

        def trip(i, st):
            for u in range(per):
                st = step(per * i + u, st)
            return st

        lax.fori_loop(0, nch // per, trip, jnp.zeros((HEAD, HEAD), F32))

    col = lambda off: pl.BlockSpec((T, HEAD), functools.partial(lambda h, off: (0, off + h), off=off))
    return _pcall(
        body, name="hgrn_fwd", grid=(H,),
        in_specs=[col(0), col(H), col(2 * H), pl.BlockSpec((1, HEAD), lambda h: (0, h))],
        out_specs=[pl.BlockSpec((T, HEAD), lambda h: (0, h)),
                   pl.BlockSpec((None, nch, HEAD, HEAD), lambda h: (h, 0, 0, 0))],
        out_shape=[jax.ShapeDtypeStruct((T, d_model), F32), jax.ShapeDtypeStruct((H, nch, HEAD, HEAD), F32)],
        compiler_params=_params(("parallel",)),
    )(proj, proj, proj, lb)


def _hgrn_bwd(proj, lb, states, do, dgate, d_model):
    T = proj.shape[0]
    H = d_model // HEAD
    nch = T // CHUNK

    def body(q_ref, f_ref, v_ref, lb_ref, s_ref, do_ref, dg_ref, dp_ref, dlb_ref):
        dq_ref, df_ref, dv_ref = dp_ref.at[0], dp_ref.at[1], dp_ref.at[2]
        dp_ref[3] = dg_ref[...]
        lbv = lb_ref[...]
        row = lax.broadcasted_iota(jnp.int32, (CHUNK, HEAD), 0)
        tt = lax.broadcasted_iota(jnp.int32, (CHUNK, CHUNK), 0)
        ss = lax.broadcasted_iota(jnp.int32, (CHUNK, CHUNK), 1)
        last = row == CHUNK - 1
        nt = (((1,), (1,)), ((), ()))
        tn = (((0,), (0,)), ((), ()))

        def step(j, carry):
            dst, dlb = carry
            i = nch - 1 - j
            sl = pl.ds(pl.multiple_of(i * CHUNK, CHUNK), CHUNK)
            qr = q_ref[sl, :]
            q, k, gl, f, sg, sq = _gates(qr, f_ref[sl, :], lbv)
            v = v_ref[sl, :].astype(BF16)
            d_o = do_ref[sl, :].astype(BF16)
            st = s_ref[i]
            st16 = st.astype(BF16)
            dst16 = dst.astype(BF16)
            b, eq, ek = _chunk_decays(gl, f, row)
            a, qs, ks = _intra(q, k, eq, ek, tt, ss)
            bl = b[CHUNK - 1:CHUNK, :]
            e0 = jnp.exp(b)
            eh = jnp.exp(bl - b)
            ebl = jnp.exp(bl)
            q0 = q * e0
            kh = k * eh
            q016 = q0.astype(BF16)
            kh16 = kh.astype(BF16)
            dv = lax.dot_general(a.astype(BF16), d_o, tn, preferred_element_type=F32)
            dv = dv + lax.dot_general(kh16, dst16, nt, preferred_element_type=F32)
            dv_ref[sl, :] = dv.astype(dv_ref.dtype)
            da = lax.dot_general(d_o, v, nt, preferred_element_type=F32)
            da = jnp.where(tt >= ss, da, 0.0)
            dd = jnp.sum(jnp.where(tt == ss, da, 0.0), axis=1, keepdims=True)
            dq0 = jnp.dot(d_o, st16, preferred_element_type=F32)
            dkh = jnp.dot(v, dst16, preferred_element_type=F32)
            dq = dq0 * e0 + dd * k
            dk = dkh * eh + dd * q
            db = dq0 * q016.astype(F32) - dkh * kh16.astype(F32)
            g = CHUNK // 2
            for e_q, e_k, qg, kg in zip(eq, ek, qs, ks):
                dag = (da if 2 * g >= CHUNK else jnp.where((tt ^ ss) < 2 * g, da, 0.0)).astype(BF16)
                dqg = jnp.dot(dag, kg, preferred_element_type=F32)
                dkg = lax.dot_general(dag, qg, tn, preferred_element_type=F32)
                dq = dq + dqg * e_q
                dk = dk + dkg * e_k
                db = db + (dqg * qg.astype(F32) - dkg * kg.astype(F32))
                g //= 2
            dbl = jnp.sum(dkh * kh16.astype(F32), axis=0, keepdims=True) + ebl * jnp.sum(dst * st, axis=0, keepdims=True)
            db = db + jnp.where(last, dbl, 0.0)
            d = 1
            while d < CHUNK:
                db = db + jnp.where(row < CHUNK - d, pltpu.roll(db, CHUNK - d, 0), 0.0)
                d *= 2
            dfg = db / f - dk
            df_ref[sl, :] = (dfg * (1.0 - lbv) * sg * (1.0 - sg)).astype(df_ref.dtype)
            dq_ref[sl, :] = (dq * (HEAD ** -0.5) * (sq * (1.0 + qr * (1.0 - sq)))).astype(dq_ref.dtype)
            dlb = dlb + jnp.sum(dfg * (1.0 - sg), axis=0, keepdims=True)
            dst = ebl * dst + lax.dot_general(d_o, q016, tn, preferred_element_type=F32)
            return dst, dlb

        _, dlb = lax.fori_loop(0, nch // 2, lambda j, cr: step(2 * j + 1, step(2 * j, cr)),
                               (jnp.zeros((HEAD, HEAD), F32), jnp.zeros((1, HEAD), F32)))
        arow = lax.broadcasted_iota(jnp.int32, (SUBLANES, HEAD), 0)
        dlb_ref[...] = jnp.where(arow == 0, dlb, 0.0)

    col = lambda off: pl.BlockSpec((T, HEAD), functools.partial(lambda h, off: (0, off + h), off=off))
    return _pcall(
        body, name="hgrn_bwd", grid=(H,),
        in_specs=[col(0), col(H), col(2 * H), pl.BlockSpec((1, HEAD), lambda h: (0, h)),
                  pl.BlockSpec((None, nch, HEAD, HEAD), lambda h: (h, 0, 0, 0)), col(0), col(0)],
        out_specs=[pl.BlockSpec((4, T, HEAD), lambda h: (0, 0, h)), pl.BlockSpec((SUBLANES, HEAD), lambda h: (0, h))],
        out_shape=[jax.ShapeDtypeStruct((4, T, d_model), BF16), jax.ShapeDtypeStruct((SUBLANES, d_model), F32)],
        compiler_params=_params(("parallel",)),
    )(proj, proj, proj, lb, states, do, dgate)


def _lb_softmax(table):
    n, f = table.shape

    def body(t_ref, p_ref):
        t = t_ref[...]
        e = jnp.exp(t - jnp.max(t, axis=0, keepdims=True))
        p_ref[...] = e / jnp.sum(e, axis=0, keepdims=True)

    padded = jnp.pad(table, ((0, SUBLANES - n), (0, 0)), constant_values=-jnp.inf)
    return _pcall(body, name="lb_softmax", out_shape=jax.ShapeDtypeStruct((SUBLANES, f), F32))(padded)


def _adamw_math(w, g, m, v):
    m = ADAM_B1 * m + (1.0 - ADAM_B1) * g
    v = ADAM_B2 * v + (1.0 - ADAM_B2) * (g * g)
    m_hat = m / (1.0 - ADAM_B1 ** ADAM_STEP)
    v_hat = v / (1.0 - ADAM_B2 ** ADAM_STEP)
    delta = -ADAM_LR * (m_hat / (jnp.sqrt(v_hat) + ADAM_EPS) + ADAM_WD * w)
    return delta, m, v


def _adamw(w, g, m, v, name):
    R, C = w.shape
    tr = _pick(R, (128, 64, 32, 16, 8))

    def body(w_ref, g_ref, m_ref, v_ref, d_ref, nm_ref, nv_ref):
        d, nm, nv = _adamw_math(w_ref[...], g_ref[...], m_ref[...], v_ref[...])
        d_ref[...] = d
        nm_ref[...] = nm
        nv_ref[...] = nv

    spec = pl.BlockSpec((tr, C), lambda i: (i, 0))
    return _pcall(
        body, name=name, grid=(R // tr,), in_specs=[spec] * 4, out_specs=[spec] * 3,
        out_shape=[jax.ShapeDtypeStruct((R, C), F32)] * 3, compiler_params=_params(("parallel",)),
    )(w, g, m, v)


def _adamw_halves(w, m, v, g_mine, g_recv, c, name, layer=0, prev=None):
    C = w.shape[1]
    rh = g_mine.shape[0]
    tr = _pick(rh, (128, 64, 32, 16, 8))
    nb = rh // tr
    r0 = layer * 2 * nb

    def body(c_ref, w_ref, m_ref, v_ref, gm_ref, gr_ref, *rest):
        g_ref, d_ref, nm_ref, nv_ref = rest[-4:]
        g = jnp.where(pl.program_id(0) == c_ref[0], gm_ref[...], gr_ref[...])
        d, nm, nv = _adamw_math(w_ref[...], g, m_ref[...], v_ref[...])
        g_ref[...] = g
        d_ref[...] = d
        nm_ref[...] = nm
        nv_ref[...] = nv

    full = pl.BlockSpec((tr, C), lambda h, i, cr: (r0 + h * nb + i, 0))
    half = pl.BlockSpec((tr, C), lambda h, i, cr: (i, 0))
    in_specs = [full, full, full, half, half]
    args = [c, w, m, v, g_mine, g_recv]
    alias = {}
    if prev is not None:
        in_specs += [pl.BlockSpec(memory_space=pl.ANY)] * 4
        args += list(prev)
        alias = {6 + k: k for k in range(4)}
    return _pcall(
        body, name=name,
        grid_spec=pltpu.PrefetchScalarGridSpec(
            num_scalar_prefetch=1, grid=(2, nb), in_specs=in_specs, out_specs=[full] * 4),
        out_shape=[jax.ShapeDtypeStruct(w.shape, F32)] * 4, input_output_aliases=alias,
        compiler_params=_params(("parallel", "parallel")),
    )(*args)


def _lb_table_grad(p8, dlb, n):
    f = p8.shape[1]

    def body(p_ref, d_ref, o_ref):
        p = p_ref[...]
        d = d_ref[...]
        p0 = p[0:1, :]
        first = lax.broadcasted_iota(jnp.int32, p.shape, 0) == 0
        o_ref[...] = p * (jnp.where(first, d, 0.0) - d * p0)

    return _pcall(body, name="lb_table_grad", out_shape=jax.ShapeDtypeStruct((SUBLANES, f), F32))(p8, dlb)[:n]


def _place():
    x, y, c = lax.axis_index("x"), lax.axis_index("y"), lax.axis_index("c")
    chips = [(1 - x, y), (x, 1 - y), (1 - x, 1 - y)]
    return x, y, c, chips


HBM_SPEC = pl.BlockSpec(memory_space=pltpu.HBM)


def _gather_weights(big, small):
    nb, ns = len(big), len(small)
    n = nb + ns

    def body(*refs):
        ins, outs = refs[:n], refs[n:2 * n]
        send_sems, recv_sems, own_send, own_recv = refs[2 * n:]
        x, y, c, chips = _place()
        me = 2 * x + y
        sib = (x, y, 1 - c)
        own = [pltpu.make_async_remote_copy(
            src_ref=ins[t], dst_ref=outs[t].at[me], send_sem=own_send.at[t], recv_sem=own_recv.at[t],
            device_id=sib, device_id_type=MESH) for t in range(n)]
        for cp in own:
            cp.start()

        def half(t, h):
            rh = big[t].shape[0] // 2
            return pl.ds(pl.multiple_of(h * rh, rh), rh)

        sends = []
        for t in range(n):
            for j, chip in enumerate(chips):
                k = 6 * t + j
                if t < nb:
                    src, dst = ins[t].at[half(t, c)], outs[t].at[me, half(t, c)]
                else:
                    src, dst = ins[t], outs[t].at[me]
                sends.append(pltpu.make_async_remote_copy(
                    src_ref=src, dst_ref=dst, send_sem=send_sems.at[k], recv_sem=recv_sems.at[k],
                    device_id=(*chip, c), device_id_type=MESH))
        for cp in sends:
            cp.start()
        passed = []
        for t in range(n):
            for j, (cx, cy) in enumerate(chips):
                k = 6 * t + j
                s = 2 * cx + cy
                if t < nb:
                    landed = outs[t].at[s, half(t, c)]
                    pltpu.make_async_remote_copy(
                        src_ref=landed, dst_ref=landed, send_sem=send_sems.at[k], recv_sem=recv_sems.at[k],
                        device_id=sib, device_id_type=MESH).wait_recv()
                    fwd = pltpu.make_async_remote_copy(
                        src_ref=landed, dst_ref=landed, send_sem=send_sems.at[k + 3], recv_sem=recv_sems.at[k + 3],
                        device_id=sib, device_id_type=MESH)
                    fwd.start()
                    passed.append(fwd)
                else:
                    landed = outs[t].at[s]
                    pltpu.make_async_remote_copy(
                        src_ref=landed, dst_ref=landed, send_sem=send_sems.at[k], recv_sem=recv_sems.at[k],
                        device_id=sib, device_id_type=MESH).wait_recv()
        for t in range(nb):
            for j, (cx, cy) in enumerate(chips):
                k = 6 * t + j
                other = outs[t].at[2 * cx + cy, half(t, 1 - c)]
                pltpu.make_async_remote_copy(
                    src_ref=other, dst_ref=other, send_sem=send_sems.at[k + 3], recv_sem=recv_sems.at[k + 3],
                    device_id=sib, device_id_type=MESH).wait_recv()
        for cp in sends + passed:
            cp.wait_send()
        for cp in own:
            cp.wait()

    arrs = list(big) + list(small)
    return _pcall(
        body, name="gather_weights", in_specs=[HBM_SPEC] * n, out_specs=[HBM_SPEC] * n,
        out_shape=[jax.ShapeDtypeStruct((N_CHIPS,) + a.shape, a.dtype) for a in arrs],
        scratch_shapes=[pltpu.SemaphoreType.DMA((6 * n,)), pltpu.SemaphoreType.DMA((6 * n,)),
                        pltpu.SemaphoreType.DMA((n,)), pltpu.SemaphoreType.DMA((n,))],
    )(*arrs)


SEM_SPEC = pl.BlockSpec(memory_space=pltpu.SEMAPHORE)
DATAFLOW = pltpu.SideEffectType.DATAFLOW_SIDE_EFFECTING
COPIES_PER_SHARD = 4


def _shard_copies(ins, lands, send_sems, recv_sems):
    x, y, c, chips = _place()
    me = 2 * x + y
    cps = []
    for t in range(len(ins)):
        rh = ins[t].shape[0] // 2
        half = pl.ds(pl.multiple_of(c * rh, rh), rh)
        for j, chip in enumerate(chips):
            k = COPIES_PER_SHARD * t + j
            cps.append(pltpu.make_async_remote_copy(
                src_ref=ins[t].at[half], dst_ref=lands[t].at[me, half], send_sem=send_sems.at[k],
                recv_sem=recv_sems.at[k], device_id=(*chip, c), device_id_type=MESH))
        k = COPIES_PER_SHARD * t + 3
        cps.append(pltpu.make_async_remote_copy(
            src_ref=ins[t], dst_ref=lands[t].at[me], send_sem=send_sems.at[k], recv_sem=recv_sems.at[k],
            device_id=(x, y, 1 - c), device_id_type=MESH))
    return cps


def _gather_start(shards, thru, name):
    n = len(shards)

    def body(*refs):
        ins, lands = refs[:n], refs[n:2 * n]
        send_sems, recv_sems = refs[2 * n + 1], refs[2 * n + 2]
        for cp in _shard_copies(ins, lands, send_sems, recv_sems):
            cp.start()

    lands = [pltpu.with_memory_space_constraint(lax.empty((N_CHIPS,) + s.shape, s.dtype), pltpu.HBM) for s in shards]
    ops = [pltpu.with_memory_space_constraint(s, pltpu.HBM) for s in shards] + lands + [thru]
    nsem = COPIES_PER_SHARD * n
    res = _pcall(
        body, name=name, in_specs=[HBM_SPEC] * (2 * n + 1),
        out_specs=[SEM_SPEC, SEM_SPEC] + [HBM_SPEC] * (2 * n + 1),
        out_shape=[pltpu.SemaphoreType.DMA((nsem,)), pltpu.SemaphoreType.DMA((nsem,))]
        + [pltpu.HBM(o.shape, o.dtype) for o in ops],
        input_output_aliases={i: 2 + i for i in range(2 * n + 1)},
        compiler_params=pltpu.CompilerParams(has_side_effects=DATAFLOW),
    )(*ops)
    return res[0], res[1], res[2:2 + n], res[2 + n:2 + 2 * n], res[2 + 2 * n]


def _gather_wait(send_sems, recv_sems, shards, lands, after, name):
    n = len(shards)

    def body(*refs):
        ins, lnd = refs[:n], refs[n:2 * n]
        ssem, rsem = refs[2 * n], refs[2 * n + 1]
        for cp in _shard_copies(ins, lnd, ssem, rsem):
            cp.wait_send()
            cp.wait_recv()

    res = _pcall(
        body, name=name,
        in_specs=[HBM_SPEC] * (2 * n) + [SEM_SPEC, SEM_SPEC, pl.BlockSpec(memory_space=pl.ANY)],
        out_specs=[HBM_SPEC] * (2 * n),
        out_shape=[pltpu.HBM(o.shape, o.dtype) for o in list(shards) + list(lands)],
        input_output_aliases={i: i for i in range(2 * n)},
        compiler_params=pltpu.CompilerParams(has_side_effects=DATAFLOW),
    )(*shards, *lands, send_sems, recv_sems, after)
    return res[n:]


def _gather_forward(lands, name):
    n = len(lands)

    def body(*refs):
        outs = refs[n:2 * n]
        send_sems, recv_sems = refs[2 * n:]
        x, y, c, chips = _place()
        sib = (x, y, 1 - c)
        cps = []
        for t in range(n):
            rh = lands[t].shape[1] // 2
            for j, (cx, cy) in enumerate(chips):
                mine = outs[t].at[2 * cx + cy, pl.ds(pl.multiple_of(c * rh, rh), rh)]
                cps.append(pltpu.make_async_remote_copy(
                    src_ref=mine, dst_ref=mine, send_sem=send_sems.at[3 * t + j], recv_sem=recv_sems.at[3 * t + j],
                    device_id=sib, device_id_type=MESH))
        for cp in cps:
            cp.start()
        for t in range(n):
            rh = lands[t].shape[1] // 2
            for j, (cx, cy) in enumerate(chips):
                theirs = outs[t].at[2 * cx + cy, pl.ds(pl.multiple_of((1 - c) * rh, rh), rh)]
                pltpu.make_async_remote_copy(
                    src_ref=theirs, dst_ref=theirs, send_sem=send_sems.at[3 * t + j], recv_sem=recv_sems.at[3 * t + j],
                    device_id=sib, device_id_type=MESH).wait_recv()
        for cp in cps:
            cp.wait_send()

    return _pcall(
        body, name=name, in_specs=[HBM_SPEC] * n, out_specs=[HBM_SPEC] * n,
        out_shape=[jax.ShapeDtypeStruct(a.shape, a.dtype) for a in lands],
        input_output_aliases={i: i for i in range(n)},
        scratch_shapes=[pltpu.SemaphoreType.DMA((3 * n,)), pltpu.SemaphoreType.DMA((3 * n,))],
    )(*lands)


def _sibling_copies(ins, lands, send_sems, recv_sems, other_half):
    x, y, c, _ = _place()
    return [pltpu.make_async_remote_copy(
        src_ref=ins[t].at[:, 1 - c] if other_half else ins[t], dst_ref=lands[t], send_sem=send_sems.at[t],
        recv_sem=recv_sems.at[t], device_id=(x, y, 1 - c), device_id_type=MESH) for t in range(len(ins))]


def _sibling_start(srcs, other_half, thru, name):
    n = len(srcs)
    nthru = 0 if thru is None else 1

    def body(*refs):
        ins, lands = refs[:n], refs[n:2 * n]
        send_sems, recv_sems = refs[2 * n + nthru], refs[2 * n + nthru + 1]
        for cp in _sibling_copies(ins, lands, send_sems, recv_sems, other_half):
            cp.start()

    shapes = [(s.shape[0],) + s.shape[2:] if other_half else s.shape for s in srcs]
    lands = [pltpu.with_memory_space_constraint(lax.empty(sh, s.dtype), pltpu.HBM) for sh, s in zip(shapes, srcs)]
    ops = [pltpu.with_memory_space_constraint(s, pltpu.HBM) for s in srcs] + lands + ([] if thru is None else [thru])
    res = _pcall(
        body, name=name, in_specs=[HBM_SPEC] * len(ops),
        out_specs=[SEM_SPEC, SEM_SPEC] + [HBM_SPEC] * len(ops),
        out_shape=[pltpu.SemaphoreType.DMA((n,)), pltpu.SemaphoreType.DMA((n,))]
        + [pltpu.HBM(o.shape, o.dtype) for o in ops],
        input_output_aliases={i: 2 + i for i in range(len(ops))},
        compiler_params=pltpu.CompilerParams(has_side_effects=DATAFLOW),
    )(*ops)
    return res[0], res[1], res[2:2 + n], res[2 + n:2 + 2 * n], (None if thru is None else res[2 + 2 * n])


def _sibling_wait(send_sems, recv_sems, srcs, lands, other_half, after, name):
    n = len(srcs)

    def body(*refs):
        ins, lnd = refs[:n], refs[n:2 * n]
        ssem, rsem = refs[2 * n], refs[2 * n + 1]
        for cp in _sibling_copies(ins, lnd, ssem, rsem, other_half):
            cp.wait_send()
            cp.wait_recv()

    res = _pcall(
        body, name=name,
        in_specs=[HBM_SPEC] * (2 * n) + [SEM_SPEC, SEM_SPEC, pl.BlockSpec(memory_space=pl.ANY)],
        out_specs=[HBM_SPEC] * (2 * n),
        out_shape=[pltpu.HBM(o.shape, o.dtype) for o in list(srcs) + list(lands)],
        input_output_aliases={i: i for i in range(2 * n)},
        compiler_params=pltpu.CompilerParams(has_side_effects=DATAFLOW),
    )(*srcs, *lands, send_sems, recv_sems, after)
    return res[:n], res[n:]


def _chip_copies(ins, lands, send_sems, recv_sems):
    x, y, c, chips = _place()
    cps = []
    for t in range(len(ins)):
        for j, (cx, cy) in enumerate(chips):
            cps.append(pltpu.make_async_remote_copy(
                src_ref=ins[t].at[2 * cx + cy], dst_ref=lands[t].at[j],
                send_sem=send_sems.at[3 * t + j], recv_sem=recv_sems.at[3 * t + j],
                device_id=(cx, cy, c), device_id_type=MESH))
    return cps


def _chip_start(parts, thru, name):
    n = len(parts)

    def body(*refs):
        ins, lands = refs[:n], refs[n:2 * n]
        send_sems, recv_sems = refs[2 * n + 1], refs[2 * n + 2]
        for cp in _chip_copies(ins, lands, send_sems, recv_sems):
            cp.start()

    lands = [pltpu.with_memory_space_constraint(lax.empty((3,) + p.shape[1:], p.dtype), pltpu.HBM) for p in parts]
    ops = [pltpu.with_memory_space_constraint(p, pltpu.HBM) for p in parts] + lands + [thru]
    res = _pcall(
        body, name=name, in_specs=[HBM_SPEC] * (2 * n + 1),
        out_specs=[SEM_SPEC, SEM_SPEC] + [HBM_SPEC] * (2 * n + 1),
        out_shape=[pltpu.SemaphoreType.DMA((3 * n,)), pltpu.SemaphoreType.DMA((3 * n,))]
        + [pltpu.HBM(o.shape, o.dtype) for o in ops],
        input_output_aliases={i: 2 + i for i in range(2 * n + 1)},
        compiler_params=pltpu.CompilerParams(has_side_effects=DATAFLOW),
    )(*ops)
    return res[0], res[1], res[2:2 + n], res[2 + n:2 + 2 * n], res[2 + 2 * n]


def _chip_wait(send_sems, recv_sems, parts, lands, after, name):
    n = len(parts)

    def body(*refs):
        ins, lnd = refs[:n], refs[n:2 * n]
        ssem, rsem = refs[2 * n], refs[2 * n + 1]
        for cp in _chip_copies(ins, lnd, ssem, rsem):
            cp.wait_send()
            cp.wait_recv()

    res = _pcall(
        body, name=name,
        in_specs=[HBM_SPEC] * (2 * n) + [SEM_SPEC, SEM_SPEC, pl.BlockSpec(memory_space=pl.ANY)],
        out_specs=[HBM_SPEC] * (2 * n),
        out_shape=[pltpu.HBM(o.shape, o.dtype) for o in list(parts) + list(lands)],
        input_output_aliases={i: i for i in range(2 * n)},
        compiler_params=pltpu.CompilerParams(has_side_effects=DATAFLOW),
    )(*parts, *lands, send_sems, recv_sems, after)
    return res[:n], res[n:]


def _add_pair(grad, recv, c, name):
    s, _, rh, cc = grad.shape
    tr = _pick(rh, (256, 128, 64, 32, 16))

    def body(c_ref, g_ref, r_ref, o_ref):
        o_ref[...] = (g_ref[...].astype(F32) + r_ref[...].astype(F32)).astype(o_ref.dtype)

    return _pcall(
        body, name=name,
        grid_spec=pltpu.PrefetchScalarGridSpec(
            num_scalar_prefetch=1, grid=(s, rh // tr),
            in_specs=[pl.BlockSpec((None, None, tr, cc), lambda a, i, cr: (a, cr[0], i, 0)),
                      pl.BlockSpec((None, tr, cc), lambda a, i, cr: (a, i, 0))],
            out_specs=pl.BlockSpec((None, tr, cc), lambda a, i, cr: (a, i, 0))),
        out_shape=jax.ShapeDtypeStruct((s, rh, cc), BF16),
        compiler_params=_params(("parallel", "parallel")),
    )(c, grad, recv)


def _add_chips(part, recv, me, name):
    _, rh, cc = part.shape
    tr = _pick(rh, (256, 128, 64, 32, 16))

    def body(m_ref, p_ref, r_ref, o_ref):
        o_ref[...] = ((p_ref[...].astype(F32) + r_ref[0].astype(F32)) + r_ref[1].astype(F32)) + r_ref[2].astype(F32)

    return _pcall(
        body, name=name,
        grid_spec=pltpu.PrefetchScalarGridSpec(
            num_scalar_prefetch=1, grid=(rh // tr,),
            in_specs=[pl.BlockSpec((None, tr, cc), lambda i, mr: (mr[0], i, 0)),
                      pl.BlockSpec((3, tr, cc), lambda i, mr: (0, i, 0))],
            out_specs=pl.BlockSpec((tr, cc), lambda i, mr: (i, 0))),
        out_shape=jax.ShapeDtypeStruct((rh, cc), F32),
        compiler_params=_params(("parallel",)),
    )(me, part, recv)


def _all_sum(vec):
    rows = vec.shape[0]

    def body(v_ref, o_ref, buf, send_sems, recv_sems):
        x, y, c, _ = _place()
        me = 4 * x + 2 * y + c
        buf[me] = v_ref[...]
        cps = []
        for r in range(1, 8):
            fx, fy, fc = (r >> 2) & 1, (r >> 1) & 1, r & 1
            peer = (x ^ fx, y ^ fy, c ^ fc)
            cps.append(pltpu.make_async_remote_copy(
                src_ref=v_ref, dst_ref=buf.at[me], send_sem=send_sems.at[r - 1], recv_sem=recv_sems.at[r - 1],
                device_id=peer, device_id_type=MESH))
        for cp in cps:
            cp.start()
        for r in range(1, 8):
            src = me ^ r
            pltpu.make_async_remote_copy(
                src_ref=v_ref, dst_ref=buf.at[src], send_sem=send_sems.at[r - 1], recv_sem=recv_sems.at[r - 1],
                device_id=(x, y, c), device_id_type=MESH).wait_recv()
        for cp in cps:
            cp.wait_send()
        acc = buf[0]
        for d in range(1, 8):
            acc = acc + buf[d]
        o_ref[...] = acc

    return _pcall(
        body, name="all_sum_small",
        in_specs=[pl.BlockSpec(memory_space=pltpu.VMEM)], out_specs=pl.BlockSpec(memory_space=pltpu.VMEM),
        out_shape=jax.ShapeDtypeStruct((rows, LANES), F32),
        scratch_shapes=[pltpu.VMEM((8, rows, LANES), F32), pltpu.SemaphoreType.DMA((7,)), pltpu.SemaphoreType.DMA((7,))],
    )(vec)


def _pack(parts):
    flat = jnp.concatenate([p.reshape(-1) for p in parts])
    tile = SUBLANES * LANES
    pad = (-flat.shape[0]) % tile
    return jnp.pad(flat, (0, pad)).reshape(-1, LANES)


def _unpack(vec, shapes):
    flat = vec.reshape(-1)
    out, p = [], 0
    for s in shapes:
        n = 1
        for d in s:
            n *= d
        out.append(flat[p:p + n].reshape(s))
        p += n
    return out


def _local_step(x, tgt, norm_mix, norm_ffn, lb8, out_norm, final_norm, sc_conv, ffn_conv, weights, reduce_start,
                reduce_finish):
    T, D = x.shape
    F2 = ffn_conv.shape[-1]
    FF = F2 // 2
    tm = _pick(T, (1024, 512, 256, 128))
    wide = (1536, 1408, 1024, 768, 512, 384, 256, 128)
    cw_h, cw_s, cw_u = 4 * D // N_CHIPS, 3 * D // N_CHIPS, F2 // N_CHIPS
    kp = FF // N_CHIPS
    tk_ff = kp if kp % LANES == 0 else LANES
    tn_d = _pick(D, (1024, 512, 256, 128))
    tk_w = _pick(D, (512, 256, 128))
    tn_h = _pick(cw_h, (1024, 512, 256, 128))
    tn_s = _pick(D // N_CHIPS, (512, 256, 128))
    tn_u = _pick(cw_u, wide)
    lb = lb8[0:1]
    wm_sq = _wmap_col(D, tn_d, 0)
    wm_sq1 = _wmap_col(D, D, 0)
    seg1 = lambda a: a.reshape((1,) + a.shape)

    def mix_in(h, w):
        return _row_call(_rms_fwd_fn, [(h, 0, D)], [w], [(D, BF16)], 0, "rms_fwd")[0]

    def rms_bwd(h, dxn, dh, w):
        return _row_call(_rms_bwd_fn, [(h, 0, D), (dxn, 0, D), (dh, 0, D)], [w], [(D, F32), (D, BF16)], 1, "rms_bwd")

    def ffn_fwd(h, i, w_up, w_down):
        xn = mix_in(h, norm_ffn[i:i + 1])
        tn = _pick(cw_u, wide)
        up = _mm_nn(xn, w_up, _wmap_col(cw_u, tn, 0), D, F2, tm, D, tn, "ffn_up")
        nb = FF // LANES
        a = _col_call(_glu_fwd_fn, [(up, 0), (up, nb)], [(ffn_conv[i], 0), (ffn_conv[i], nb)], [(1, FF, BF16)], 0,
                      "glu_fwd", before=True, after=False)[0][0]
        h2 = _mm_nn(a, w_down, _wmap_row(kp, tk_ff, 0), FF, D, tm, tk_ff, tn_d, "ffn_down", res=h)
        return h2, (xn, up, a)

    def ffn_bwd(dh, dh16, h, saved, i, w_up, w_down):
        xn, up, a = saved
        g_down = _mm_tn(a, seg1(dh16), (N_CHIPS, 1, kp, D), _wmap_row(kp, tk_ff, 0), FF, D, tk_ff, tn_d,
                        "ffn_down_dw", tm=_pick(T, (2048, 1024, 512, 256, 128)))
        dh16 = reduce_start(("ffn_w_down", i), g_down, dh16)
        da = _mm_nt(seg1(dh16), w_down, _wmap_row(kp, tk_ff, 0), FF, D, tm, tk_ff, D, "ffn_down_dx")
        nb = FF // LANES
        dgv, cg, cv = _col_call(_glu_bwd_fn, [(up, 0), (up, nb), (da, 0)], [(ffn_conv[i], 0), (ffn_conv[i], nb)],
                                [(2, FF, BF16)], 2, "glu_bwd", before=True, after=True)
        g_up = _mm_tn(xn, dgv, (N_CHIPS, 1, D, cw_u), _wmap_col(cw_u, tn_u, 0), D, F2, tk_w, tn_u, "ffn_up_dw")
        dgv = reduce_start(("ffn_w_up", i), g_up, dgv)
        dxn = _mm_nt(dgv, w_up, _wmap_col(cw_u, tn_u, 0), D, F2, tm, D, tn_u, "ffn_up_dx")
        dh2, dh2_16, dnw = rms_bwd(h, dxn, dh, norm_ffn[i:i + 1])
        return dh2, reduce_finish(dh2_16), dnw, jnp.concatenate([cg[:3], cv[:3]], axis=1)

    h0 = x
    xn0 = mix_in(h0, norm_mix[0:1])
    w_hin, = weights(0, xn0)
    proj = _mm_nn(xn0, w_hin, _wmap_col(cw_h, tn_h, 0), D, 4 * D, tm, D, tn_h, "hgrn_in")
    o, states = _hgrn_fwd(proj, lb, D)
    on = _row_call(_onorm_fwd_fn, [(o, 0, D), (proj, 3, D)], [out_norm], [(D, BF16)], 0, "onorm_fwd")[0]
    w_hout, w_up0, w_down0 = weights(1, on)
    w_hout1 = w_hout.reshape(1, D, D)
    h1 = _mm_nn(on, w_hout1, wm_sq, D, D, tm, D, tn_d, "hgrn_out", res=h0)
    h2, ffn0 = ffn_fwd(h1, 0, w_up0, w_down0)
    xn1 = mix_in(h2, norm_mix[1:2])
    w_sin, w_sout, w_up1, w_down1 = weights(2, xn1)
    w_sout1 = w_sout.reshape(1, D, D)
    tn_si = _pick(cw_s, wide)
    sproj = _mm_nn(xn1, w_sin, _wmap_col(cw_s, tn_si, 0), D, 3 * D, tm, D, tn_si, "sc_in")
    nd = D // LANES
    ysc = _col_call(_sc_fwd_fn, [(sproj, 0), (sproj, nd), (sproj, 2 * nd)], [(sc_conv, 0)], [(1, D, BF16)], 0,
                    "sc_fwd", before=True, after=False)[0][0]
    h3 = _mm_nn(ysc, w_sout1, wm_sq, D, D, tm, D, tn_d, "sc_out", res=h2)
    h4, ffn1 = ffn_fwd(h3, 1, w_up1, w_down1)

    dh, dh16, esq, dfinal = _row_call(_final_fn, [(h4, 0, D), (tgt, 0, D)], [final_norm], [(D, F32), (D, BF16)], 2,
                                      "final_loss")
    loss = 0.5 / D * jnp.sum(esq)
    dh, dh16, dnf1, dconv1 = ffn_bwd(dh, dh16, h3, ffn1, 1, w_up1, w_down1)
    g_sout = _mm_tn(ysc, seg1(dh16), (1, 1, D, D), wm_sq, D, D, tk_w, tn_d, "sc_out_dw")
    dh16 = reduce_start(("sc_w_out", 0), g_sout, dh16)
    dy = _mm_nt(seg1(dh16), w_sout1, wm_sq1, D, D, tm, D, D, "sc_out_dx")
    dsp, dscc = _col_call(_sc_bwd_fn, [(sproj, 0), (sproj, nd), (sproj, 2 * nd), (dy, 0)], [(sc_conv, 0)],
                          [(3, D, BF16)], 1, "sc_bwd", before=True, after=True)
    g_sin = _mm_tn(xn1, dsp, (N_CHIPS, 1, D, cw_s), _wmap_col(cw_s, tn_s, 0), D, 3 * D, tk_w, tn_s, "sc_in_dw")
    dsp = reduce_start(("sc_w_in", 0), g_sin, dsp)
    dxn = _mm_nt(dsp, w_sin, _wmap_col(cw_s, tn_s, 0), D, 3 * D, tm, D, tn_s, "sc_in_dx", per_step=3)
    dh, dh16, dnm1 = rms_bwd(h2, dxn, dh, norm_mix[1:2])
    dh16 = reduce_finish(dh16)
    dh, dh16, dnf0, dconv0 = ffn_bwd(dh, dh16, h1, ffn0, 0, w_up0, w_down0)
    g_hout = _mm_tn(on, seg1(dh16), (1, 1, D, D), wm_sq, D, D, tk_w, tn_d, "hgrn_out_dw")
    dh16 = reduce_start(("hgrn_w_out", 0), g_hout, dh16)
    don = _mm_nt(seg1(dh16), w_hout1, wm_sq1, D, D, tm, D, D, "hgrn_out_dx")
    do, dgate, dgain = _row_call(_onorm_bwd_fn, [(o, 0, D), (proj, 3, D), (don, 0, D)], [out_norm],
                                 [(D, F32), (D, BF16)], 1, "onorm_bwd")
    dproj, dlb = _hgrn_bwd(proj, lb, states, do, dgate, D)
    g_hin = _mm_tn(xn0, dproj, (N_CHIPS, 1, D, cw_h), _wmap_col(cw_h, tn_h, 0), D, 4 * D, tk_w, tn_h, "hgrn_in_dw")
    dproj = reduce_start(("hgrn_w_in", 0), g_hin, dproj)
    dxn = _mm_nt(dproj, w_hin, _wmap_col(cw_h, tn_h, 0), D, 4 * D, tm, D, tn_h, "hgrn_in_dx", per_step=2)
    grad_x, _, dnm0 = rms_bwd(h0, dxn, dh, norm_mix[0:1])
    grad_x = reduce_finish(grad_x)

    small = dict(
        loss=loss,
        norm_mix=jnp.stack([jnp.sum(dnm0, axis=0), jnp.sum(dnm1, axis=0)]),
        norm_ffn=jnp.stack([jnp.sum(dnf0, axis=0), jnp.sum(dnf1, axis=0)]),
        lb=dlb[0:1],
        out_norm=jnp.sum(dgain, axis=0)[None],
        final_norm=jnp.sum(dfinal, axis=0),
        sc_conv=dscc[:3],
        ffn_conv=jnp.stack([dconv0, dconv1]),
    )
    return grad_x, small


def kernel(x, norm_mix, norm_ffn, hgrn_w_in, hgrn_lb_table, hgrn_out_norm, hgrn_w_out, sc_w_in, sc_conv, sc_w_out, ffn_w_up, ffn_conv, ffn_w_down, final_norm, loss_target, m_norm_mix, m_norm_ffn, m_hgrn_w_in, m_hgrn_lb_table, m_hgrn_out_norm, m_hgrn_w_out, m_sc_w_in, m_sc_conv, m_sc_w_out, m_ffn_w_up, m_ffn_conv, m_ffn_w_down, m_final_norm, v_norm_mix, v_norm_ffn, v_hgrn_w_in, v_hgrn_lb_table, v_hgrn_out_norm, v_hgrn_w_out, v_sc_w_in, v_sc_conv, v_sc_w_out, v_ffn_w_up, v_ffn_conv, v_ffn_w_down, v_final_norm):
    D = x.shape[-1]
    xi, yi, ci = lax.axis_index("x"), lax.axis_index("y"), lax.axis_index("c")
    me_chip = (2 * xi + yi).astype(jnp.int32).reshape(1)
    me_core = ci.astype(jnp.int32).reshape(1)

    big_names = ["hgrn_w_in", "hgrn_w_out", "sc_w_in", "sc_w_out", "ffn_w_up", "ffn_w_down"]
    big_w = dict(hgrn_w_in=hgrn_w_in, hgrn_w_out=hgrn_w_out, sc_w_in=sc_w_in, sc_w_out=sc_w_out,
                 ffn_w_up=ffn_w_up, ffn_w_down=ffn_w_down)
    big_m = dict(hgrn_w_in=m_hgrn_w_in, hgrn_w_out=m_hgrn_w_out, sc_w_in=m_sc_w_in, sc_w_out=m_sc_w_out,
                 ffn_w_up=m_ffn_w_up, ffn_w_down=m_ffn_w_down)
    big_v = dict(hgrn_w_in=v_hgrn_w_in, hgrn_w_out=v_hgrn_w_out, sc_w_in=v_sc_w_in, sc_w_out=v_sc_w_out,
                 ffn_w_up=v_ffn_w_up, ffn_w_down=v_ffn_w_down)
    flat2 = lambda a: a.reshape(-1, a.shape[-1])

    sh = lambda a: a.reshape(-1, a.shape[-1]).astype(BF16)
    conv_shards = [flat2(sc_conv), flat2(ffn_conv)]
    w_hin, scc4, fcc4 = _gather_weights([sh(hgrn_w_in)], conv_shards)
    scc = jnp.moveaxis(scc4, 0, 1).reshape(3, D)
    f2 = ffn_conv.shape[-1] * N_CHIPS
    fcc = jnp.moveaxis(fcc4.reshape(N_CHIPS, 2, 3, -1), 0, 2).reshape(2, 3, f2)
    stage1 = [sh(hgrn_w_out), sh(ffn_w_up[0]), sh(ffn_w_down[0])]
    stage2 = [sh(sc_w_in), sh(sc_w_out), sh(ffn_w_up[1]), sh(ffn_w_down[1])]
    ss1, rs1, src1, land1, w_hin = _gather_start(stage1, w_hin, "gather_start_1")
    ss2, rs2, src2, land2, w_hin = _gather_start(stage2, w_hin, "gather_start_2")

    def weights(stage, after):
        if stage == 0:
            return (w_hin,)
        if stage == 1:
            return _gather_forward(_gather_wait(ss1, rs1, src1, land1, after, "gather_wait_1"), "gather_forward_1")
        return _gather_forward(_gather_wait(ss2, rs2, src2, land2, after, "gather_wait_2"), "gather_forward_2")

    pending = []
    started = []

    def reduce_start(slot, grad, thru):
        t = sum(len(b[0]) for b in pending) + len(started)
        halves = grad.reshape(N_CHIPS, 2, -1, grad.shape[-1])
        ss, rs, src, land, thru = _sibling_start([halves], True, thru, "grad_pair_start_%d" % t)
        started.append((slot, t, ss, rs, src, land))
        return thru

    def reduce_finish(thru):
        k = len(pending)
        pair = []
        for slot, t, ss, rs, src, land in started:
            src, recv = _sibling_wait(ss, rs, src, land, True, thru, "grad_pair_wait_%d" % t)
            pair.append(_add_pair(src[0], recv[0], me_core, "grad_add_pair"))
        ss, rs, pair, land, thru = _chip_start(pair, thru, "grad_chip_start_%d" % k)
        pending.append(([s[0] for s in started], ss, rs, pair, land))
        started.clear()
        return thru

    lb8 = _lb_softmax(hgrn_lb_table)
    grad_x, small = _local_step(
        x[0], loss_target[0], norm_mix, norm_ffn, lb8, hgrn_out_norm, final_norm[None], scc, fcc, weights,
        reduce_start, reduce_finish)

    small_names = ["loss", "norm_mix", "norm_ffn", "lb", "out_norm", "final_norm", "sc_conv", "ffn_conv"]
    parts = [small[n].astype(F32) for n in small_names]
    shapes = [p.shape for p in parts]
    tot = dict(zip(small_names, _unpack(_all_sum(_pack(parts)), shapes)))
    loss = tot["loss"].reshape(())
    g_lb_table = _lb_table_grad(lb8, tot["lb"], hgrn_lb_table.shape[0])
    cw = sc_conv.shape[-1]
    g_sc_conv = lax.dynamic_slice_in_dim(tot["sc_conv"], me_chip[0] * cw, cw, axis=1)[None]
    cf = ffn_conv.shape[-1]
    g_ffn_conv = lax.dynamic_slice_in_dim(tot["ffn_conv"], me_chip[0] * cf, cf, axis=2)
    g_small = dict(norm_mix=tot["norm_mix"], norm_ffn=tot["norm_ffn"], hgrn_lb_table=g_lb_table,
                   hgrn_out_norm=tot["out_norm"], sc_conv=g_sc_conv, ffn_conv=g_ffn_conv, final_norm=tot["final_norm"])
    w_small = dict(norm_mix=norm_mix, norm_ffn=norm_ffn, hgrn_lb_table=hgrn_lb_table, hgrn_out_norm=hgrn_out_norm,
                   sc_conv=sc_conv, ffn_conv=ffn_conv, final_norm=final_norm)
    m_small = dict(norm_mix=m_norm_mix, norm_ffn=m_norm_ffn, hgrn_lb_table=m_hgrn_lb_table, hgrn_out_norm=m_hgrn_out_norm,
                   sc_conv=m_sc_conv, ffn_conv=m_ffn_conv, final_norm=m_final_norm)
    v_small = dict(norm_mix=v_norm_mix, norm_ffn=v_norm_ffn, hgrn_lb_table=v_hgrn_lb_table, hgrn_out_norm=v_hgrn_out_norm,
                   sc_conv=v_sc_conv, ffn_conv=v_ffn_conv, final_norm=v_final_norm)
    sm_names = list(g_small)
    sm_shapes = [w_small[n].shape for n in sm_names]
    d_s, m_s, v_s = _adamw(_pack([w_small[n] for n in sm_names]), _pack([g_small[n] for n in sm_names]),
                           _pack([m_small[n] for n in sm_names]), _pack([v_small[n] for n in sm_names]), "adamw_small")
    out_g, out_d, out_m, out_v = dict(g_small), {}, {}, {}
    for n, d_, m_, v_ in zip(sm_names, _unpack(d_s, sm_shapes), _unpack(m_s, sm_shapes), _unpack(v_s, sm_shapes)):
        out_d[n], out_m[n], out_v[n] = d_, m_, v_

    done = {}
    after = grad_x
    shares = []
    for k, (slots, ss, rs, pair, land) in enumerate(pending):
        pair, recv = _chip_wait(ss, rs, pair, land, after, "grad_chip_wait_%d" % k)
        mine = [_add_chips(p, r, me_chip, "grad_add_chips") for p, r in zip(pair, recv)]
        ss, rs, mine, land, _ = _sibling_start(mine, False, None, "grad_share_start_%d" % k)
        shares.append((slots, ss, rs, mine, land))
        after = mine[0]
    for k, (slots, ss, rs, mine, land) in enumerate(shares):
        mine, theirs = _sibling_wait(ss, rs, mine, land, False, after, "grad_share_wait_%d" % k)
        for (n, layer), gm, gr in zip(slots, mine, theirs):
            done[n] = _adamw_halves(flat2(big_w[n]), flat2(big_m[n]), flat2(big_v[n]), gm, gr, me_core, "adamw_" + n,
                                    layer=layer, prev=done.get(n))
        after = done[slots[-1][0]][0]
    for n in big_names:
        out_g[n], out_d[n], out_m[n], out_v[n] = (a.reshape(big_w[n].shape) for a in done[n])

    order = ["norm_mix", "norm_ffn", "hgrn_w_in", "hgrn_lb_table", "hgrn_out_norm", "hgrn_w_out", "sc_w_in", "sc_conv",
             "sc_w_out", "ffn_w_up", "ffn_conv", "ffn_w_down", "final_norm"]
    return (loss, grad_x[None], *[out_g[n] for n in order], *[out_d[n] for n in order],
            *[out_m[n] for n in order], *[out_v[n] for n in order])
```

```python
import functools

import jax
import jax.numpy as jnp
from jax import lax
from jax.experimental import pallas as pl
from jax.experimental.pallas import tpu as pltpu

F32 = jnp.float32
BF16 = jnp.bfloat16
MESH = pl.DeviceIdType.MESH

EPS = 1e-6
CHUNK = 64
HEAD = 128
N_CHIPS = 4
ADAM_LR, ADAM_B1, ADAM_B2, ADAM_EPS, ADAM_WD, ADAM_STEP = 0.001, 0.9, 0.999, 1e-08, 0.01, 10
VMEM_LIMIT = 56 * 1024 * 1024
SUBLANES = 8
LANES = 128


def _pcall(body, **kw):
    return pl.pallas_call(body, **kw)


def _params(sem, vmem=VMEM_LIMIT):
    return pltpu.CompilerParams(dimension_semantics=sem, vmem_limit_bytes=vmem)


def _pick(dim, prefs):
    for p in prefs:
        if p <= dim and dim % p == 0:
            return p
    return dim


def _sigmoid(x):
    return 1.0 / (1.0 + jnp.exp(-x))


def _wmap_col(cw, tn, r0):
    bps = cw // tn
    return lambda kb, nb: (nb // bps, r0 + kb, nb % bps)


def _wmap_row(kp, tk, r0):
    bps = kp // tk
    return lambda kb, nb: (kb // bps, r0 + kb % bps, nb)


def _mm_nn(a, w3, wmap, K, N, tm, tk, tn, name, res=None):
    M = a.shape[0]
    nk = K // tk

    def body(*refs):
        if res is None:
            a_ref, w_ref, o_ref = refs[:3]
        else:
            a_ref, w_ref, r_ref, o_ref = refs[:4]
        p = jnp.dot(a_ref[...], w_ref[...], preferred_element_type=F32)
        if nk == 1:
            o_ref[...] = p if res is None else p + r_ref[...]
            return
        acc = refs[-1]
        k = pl.program_id(2)

        @pl.when(k == 0)
        def _():
            acc[...] = p

        @pl.when(k > 0)
        def _():
            acc[...] += p

        @pl.when(k == nk - 1)
        def _():
            o_ref[...] = acc[...] if res is None else acc[...] + r_ref[...]

    if nk == 1:
        grid = (M // tm, N // tn)
        ix = lambda f: (lambda i, j: f(i, j, 0))
        sem = ("parallel", "parallel")
        scratch = []
    else:
        grid = (M // tm, N // tn, nk)
        ix = lambda f: f
        sem = ("parallel", "parallel", "arbitrary")
        scratch = [pltpu.VMEM((tm, tn), F32)]
    in_specs = [pl.BlockSpec((tm, tk), ix(lambda i, j, k: (i, k))),
                pl.BlockSpec((None, tk, tn), ix(lambda i, j, k: wmap(k, j)))]
    args = [a, w3]
    if res is not None:
        in_specs.append(pl.BlockSpec((tm, tn), ix(lambda i, j, k: (i, j))))
        args.append(res)
    return _pcall(
        body, name=name, grid=grid, in_specs=in_specs,
        out_specs=pl.BlockSpec((tm, tn), ix(lambda i, j, k: (i, j))),
        out_shape=jax.ShapeDtypeStruct((M, N), F32), scratch_shapes=scratch, compiler_params=_params(sem),
    )(*args)


def _mm_nt(dy3, w3, wmap, K, N, tm, tk, tn, name, per_step=1):
    M = dy3.shape[1]
    bps = dy3.shape[2] // tn
    u = per_step
    grid = (M // tm, K // tk, N // (tn * u))
    nn = grid[2]

    def body(*refs):
        o_ref = refs[-1]
        p = None
        for r in range(u):
            d = lax.dot_general(refs[r][...], refs[u + r][...], (((1,), (1,)), ((), ())), preferred_element_type=F32)
            p = d if p is None else p + d
        if nn == 1:
            o_ref[...] = p
            return
        n = pl.program_id(2)

        @pl.when(n == 0)
        def _():
            o_ref[...] = p

        @pl.when(n > 0)
        def _():
            o_ref[...] += p

    def dy_spec(r):
        return pl.BlockSpec((None, tm, tn), lambda i, j, n: ((n * u + r) // bps, i, (n * u + r) % bps))

    def w_spec(r):
        return pl.BlockSpec((None, tk, tn), lambda i, j, n: wmap(j, n * u + r))

    return _pcall(
        body, name=name, grid=grid,
        in_specs=[dy_spec(r) for r in range(u)] + [w_spec(r) for r in range(u)],
        out_specs=pl.BlockSpec((tm, tk), lambda i, j, n: (i, j)),
        out_shape=jax.ShapeDtypeStruct((M, K), F32),
        compiler_params=_params(("parallel", "parallel", "arbitrary")),
    )(*([dy3] * u), *([w3] * u))


def _mm_tn(x, dy3, shape4, wmap, K, N, tk, tn, name, tm=None):
    M = x.shape[0]
    tm = M if tm is None else tm
    nm = M // tm
    bps = dy3.shape[2] // tn

    def body(*refs):
        x_ref, dy_ref = refs[:2]
        p = lax.dot_general(x_ref[...], dy_ref[...], (((0,), (0,)), ((), ())), preferred_element_type=F32)
        if nm == 1:
            o_ref = refs[-1]
            o_ref[...] = p.astype(o_ref.dtype)
            return
        o_ref, acc = refs[-2:]
        m = pl.program_id(2)

        @pl.when(m == 0)
        def _():
            acc[...] = p

        @pl.when(m > 0)
        def _():
            acc[...] += p

        @pl.when(m == nm - 1)
        def _():
            o_ref[...] = acc[...].astype(o_ref.dtype)

    def omap(i, j, m):
        s, rb, cb = wmap(i, j)
        return (s, 0, rb, cb)

    return _pcall(
        body, name=name, grid=(K // tk, N // tn, nm),
        in_specs=[pl.BlockSpec((tm, tk), lambda i, j, m: (m, i)),
                  pl.BlockSpec((None, tm, tn), lambda i, j, m: (j // bps, m, j % bps))],
        out_specs=pl.BlockSpec((None, None, tk, tn), omap),
        out_shape=jax.ShapeDtypeStruct(shape4, BF16),
        scratch_shapes=[] if nm == 1 else [pltpu.VMEM((tk, tn), F32)],
        compiler_params=_params(("parallel", "parallel", "arbitrary")),
    )(x, dy3)


def _row_call(fn, rows, vecs, outs, n_acc, name, t_rows=256, sub=16, per_trip=4):
    T = rows[0][0].shape[0]
    t_rows = min(t_rows, T)
    nsub = t_rows // sub
    n_r, n_v, n_o = len(rows), len(vecs), len(outs)
    width = rows[0][2]

    def body(*refs):
        r_refs = refs[:n_r]
        v_refs = refs[n_r:n_r + n_v]
        o_refs = refs[n_r + n_v:n_r + n_v + n_o]
        a_refs = refs[n_r + n_v + n_o:]

        @pl.when(pl.program_id(0) == 0)
        def _():
            for a in a_refs:
                a[...] = jnp.zeros_like(a)

        vv = [v[...] for v in v_refs]

        def step(i, carry):
            done = []
            for u in range(per_trip):
                sl = pl.ds(pl.multiple_of((i * per_trip + u) * sub, sub), sub)
                done.append((sl,) + tuple(fn([r[sl, :] for r in r_refs], vv)))
            for sl, o_vals, a_vals in done:
                for o, val in zip(o_refs, o_vals):
                    o[sl, :] = val.astype(o.dtype)
            for a_i, a in enumerate(a_refs):
                tot = None
                for _, _, a_vals in done:
                    part = a_vals[a_i].reshape(sub // SUBLANES, SUBLANES, a_vals[a_i].shape[-1]).sum(axis=0)
                    tot = part if tot is None else tot + part
                a[...] += tot
            return carry

        lax.fori_loop(0, nsub // per_trip, step, 0)

    in_specs = [pl.BlockSpec((t_rows, w), functools.partial(lambda i, cb: (i, cb), cb=cb)) for _, cb, w in rows]
    in_specs += [pl.BlockSpec(v.shape, lambda i: (0, 0)) for v in vecs]
    out_specs = [pl.BlockSpec((t_rows, w), lambda i: (i, 0)) for w, _ in outs]
    out_specs += [pl.BlockSpec((SUBLANES, width), lambda i: (0, 0)) for _ in range(n_acc)]
    out_shape = [jax.ShapeDtypeStruct((T, w), dt) for w, dt in outs]
    out_shape += [jax.ShapeDtypeStruct((SUBLANES, width), F32) for _ in range(n_acc)]
    return _pcall(
        body, name=name, grid=(T // t_rows,), in_specs=in_specs, out_specs=out_specs, out_shape=out_shape,
        compiler_params=_params(("arbitrary",)),
    )(*[r[0] for r in rows], *vecs)


def _rms_fwd_fn(rv, vv):
    h, = rv
    w, = vv
    r = lax.rsqrt(jnp.mean(h * h, axis=-1, keepdims=True) + EPS)
    return [h * r * w], []


def _rms_bwd_fn(rv, vv):
    h, dxn, dh_in = rv
    w, = vv
    d = h.shape[-1]
    r = lax.rsqrt(jnp.mean(h * h, axis=-1, keepdims=True) + EPS)
    gy = dxn * w
    dh = r * gy - h * ((r * r * r) * (1.0 / d) * jnp.sum(gy * h, axis=-1, keepdims=True))
    return [dh_in + dh] * 2, [dxn * h * r]


def _final_fn(rv, vv):
    h, tgt = rv
    w, = vv
    d = h.shape[-1]
    r = lax.rsqrt(jnp.mean(h * h, axis=-1, keepdims=True) + EPS)
    hn = h * r
    e = hn * w - tgt
    dy = e * (1.0 / d)
    gy = dy * w
    dh = r * gy - h * ((r * r * r) * (1.0 / d) * jnp.sum(gy * h, axis=-1, keepdims=True))
    return [dh] * 2, [e * e, dy * hn]


def _onorm_fwd_fn(rv, vv):
    o, g = rv
    gain, = vv
    r = lax.rsqrt(jnp.mean(o * o, axis=-1, keepdims=True) + EPS)
    return [o * r * gain * (g * _sigmoid(g))], []


def _onorm_bwd_fn(rv, vv):
    o, g, don = rv
    gain, = vv
    d = o.shape[-1]
    r = lax.rsqrt(jnp.mean(o * o, axis=-1, keepdims=True) + EPS)
    sg = _sigmoid(g)
    sl = g * sg
    n = o * r
    dg = don * n * gain * (sg * (1.0 + g * (1.0 - sg)))
    gy = don * sl * gain
    do = r * gy - o * ((r * r * r) * (1.0 / d) * jnp.sum(gy * o, axis=-1, keepdims=True))
    return [do, dg], [don * sl * n]


HALO = SUBLANES


def _col_call(fn, cols, vecs, outs, n_acc, name, before, after, tc=LANES, chunk=128):
    T = cols[0][0].shape[0]
    chunk = min(chunk, T)
    nch = T // chunk
    ncol = outs[0][1] // tc
    n_c, n_v, n_o = len(cols), len(vecs), len(outs)
    hb = HALO if before else 0
    rw = chunk + hb + (HALO if after else 0)

    def body(*refs):
        c_refs = refs[:n_c]
        v_refs = refs[n_c:n_c + n_v]
        o_refs = refs[n_c + n_v:n_c + n_v + n_o]
        a_refs = refs[n_c + n_v + n_o:]
        vv = [v[...] for v in v_refs]
        wrow = lax.broadcasted_iota(jnp.int32, (rw, tc), 0)
        inside = (wrow >= hb) & (wrow < hb + chunk)

        def step(i, carry):
            r0 = pl.multiple_of(i * chunk, chunk)
            wins = []
            for ref in c_refs:
                parts = []
                if before:
                    pb = ref[pl.ds(pl.multiple_of(jnp.maximum(r0 - HALO, 0), HALO), HALO), :]
                    parts.append(jnp.where(i > 0, pb, 0.0))
                parts.append(ref[pl.ds(r0, chunk), :])
                if after:
                    pa = ref[pl.ds(pl.multiple_of(jnp.minimum(r0 + chunk, T - HALO), HALO), HALO), :]
                    parts.append(jnp.where(i < nch - 1, pa, 0.0))
                wins.append(jnp.concatenate(parts, axis=0) if len(parts) > 1 else parts[0])
            o_vals, a_vals = fn(wins, vv, inside)
            p = 0
            for o, (nseg, _, _) in zip(o_refs, outs):
                for s in range(nseg):
                    o[s, pl.ds(r0, chunk), :] = o_vals[p][hb:hb + chunk].astype(o.dtype)
                    p += 1
            return tuple(c + a for c, a in zip(carry, a_vals))

        taps = [v.shape[0] for v, _ in vecs][:n_acc]
        init = tuple(jnp.zeros((1, tc), F32) for k in taps for _ in range(k))
        sums = lax.fori_loop(0, nch, step, init)
        arow = lax.broadcasted_iota(jnp.int32, (SUBLANES, tc), 0)
        p = 0
        for a, k in zip(a_refs, taps):
            acc = jnp.zeros((SUBLANES, tc), F32)
            for t in range(k):
                acc = jnp.where(arow == t, sums[p], acc)
                p += 1
            a[...] = acc

    in_specs = [pl.BlockSpec((T, tc), functools.partial(lambda j, off: (0, off + j), off=off)) for _, off in cols]
    in_specs += [pl.BlockSpec((v.shape[0], tc), functools.partial(lambda j, off: (0, off + j), off=off))
                 for v, off in vecs]
    out_specs = [pl.BlockSpec((nseg, T, tc), lambda j: (0, 0, j)) for nseg, _, _ in outs]
    out_specs += [pl.BlockSpec((SUBLANES, tc), lambda j: (0, j)) for _ in range(n_acc)]
    out_shape = [jax.ShapeDtypeStruct((nseg, T, w), dt) for nseg, w, dt in outs]
    out_shape += [jax.ShapeDtypeStruct((SUBLANES, ncol * tc), F32) for _ in range(n_acc)]
    return _pcall(
        body, name=name, grid=(ncol,), in_specs=in_specs, out_specs=out_specs, out_shape=out_shape,
        compiler_params=_params(("parallel",)),
    )(*[c[0] for c in cols], *[v[0] for v in vecs])


def _down(x, k):
    return x if k == 0 else pltpu.roll(x, k, 0)


def _up(x, k):
    return x if k == 0 else pltpu.roll(x, x.shape[0] - k, 0)


def _lags(x):
    return _down(x, 2), _down(x, 1), x


def _conv(lags, w):
    return w[0:1] * lags[0] + w[1:2] * lags[1] + w[2:3] * lags[2]


def _conv_t(d, w):
    return w[2:3] * d + w[1:2] * _up(d, 1) + w[0:1] * _up(d, 2)


def _tap_sums(d, lags, inside):
    dm = jnp.where(inside, d, 0.0)
    return [jnp.sum(dm * lag, axis=0, keepdims=True) for lag in lags]


def _glu_fwd_fn(wins, vv, inside):
    xg, xv = wins
    wg, wv = vv
    ug = _conv(_lags(xg), wg)
    uv = _conv(_lags(xv), wv)
    return [ug * _sigmoid(ug) * uv], []


def _glu_bwd_fn(wins, vv, inside):
    xg, xv, da = wins
    wg, wv = vv
    lg, lv = _lags(xg), _lags(xv)
    ug = _conv(lg, wg)
    uv = _conv(lv, wv)
    sg = _sigmoid(ug)
    dug = da * uv * (sg * (1.0 + ug * (1.0 - sg)))
    duv = da * (ug * sg)
    return [_conv_t(dug, wg), _conv_t(duv, wv)], _tap_sums(dug, lg, inside) + _tap_sums(duv, lv, inside)


def _sc_fwd_fn(wins, vv, inside):
    gb, gc, hh = wins
    w, = vv
    return [gb * _conv(_lags(gc * hh), w)], []


def _sc_bwd_fn(wins, vv, inside):
    gb, gc, hh, dy = wins
    w, = vv
    lz = _lags(gc * hh)
    dcv = dy * gb
    dz = _conv_t(dcv, w)
    return [dy * _conv(lz, w), dz * hh, dz * gc], _tap_sums(dcv, lz, inside)


def _gates(qr, fr, lb):
    sg = _sigmoid(fr)
    f = lb + (1.0 - lb) * sg
    sq = _sigmoid(qr)
    q = qr * sq * (HEAD ** -0.5)
    return q, 1.0 - f, jnp.log(f), f, sg, sq


def _boundary_rows(b, g, row):
    c = b.shape[0]
    if 2 * g >= SUBLANES:
        x = b.reshape(c // (2 * g), 2 * g, LANES)
        return jnp.broadcast_to(x[:, g - 1:g, :], x.shape).reshape(c, LANES)
    x = b.reshape(c // SUBLANES, SUBLANES, LANES)
    lo = jnp.broadcast_to(x[:, 1:2, :], x.shape).reshape(c, LANES)
    hi = jnp.broadcast_to(x[:, 5:6, :], x.shape).reshape(c, LANES)
    return jnp.where((row & 4) == 0, lo, hi)


def _chunk_decays(gl, f, row):
    c = gl.shape[0]
    b = gl
    d = 1
    while d < c:
        b = b + jnp.where(row >= d, pltpu.roll(b, d, 0), 0.0)
        d *= 2
    eq, ek = [], []
    g = c // 2
    while g >= 2:
        right = (row & g) != 0
        m = _boundary_rows(b, g, row)
        z = jnp.exp(jnp.where(right, b - m, m - b))
        eq.append(jnp.where(right, z, 0.0))
        ek.append(jnp.where(right, 0.0, z))
        g //= 2
    odd = (row & 1) != 0
    eq.append(jnp.where(odd, f, 0.0))
    ek.append(jnp.where(odd, 0.0, 1.0))
    return b, eq, ek


def _intra(q, k, eq, ek, tt, ss):
    c = q.shape[0]
    qs, ks = [], []
    a = jnp.where(tt == ss, jnp.sum(q * k, axis=1, keepdims=True), 0.0)
    g = c // 2
    for e_q, e_k in zip(eq, ek):
        qg = (q * e_q).astype(BF16)
        kg = (k * e_k).astype(BF16)
        p = lax.dot_general(qg, kg, (((1,), (1,)), ((), ())), preferred_element_type=F32)
        a = a + (p if 2 * g >= c else jnp.where((tt ^ ss) < 2 * g, p, 0.0))
        qs.append(qg)
        ks.append(kg)
        g //= 2
    return a, qs, ks


def _hgrn_fwd(proj, lb, d_model):
    T = proj.shape[0]
    H = d_model // HEAD
    nch = T // CHUNK

    def body(q_ref, f_ref, v_ref, lb_ref, o_ref, s_ref):
        lbv = lb_ref[...]
        row = lax.broadcasted_iota(jnp.int32, (CHUNK, HEAD), 0)
        tt = lax.broadcasted_iota(jnp.int32, (CHUNK, CHUNK), 0)
        ss = lax.broadcasted_iota(jnp.int32, (CHUNK, CHUNK), 1)

        def step(i, st):
            sl = pl.ds(pl.multiple_of(i * CHUNK, CHUNK), CHUNK)
            q, k, gl, f, _, _ = _gates(q_ref[sl, :], f_ref[sl, :], lbv)
            v = v_ref[sl, :].astype(BF16)
            b, eq, ek = _chunk_decays(gl, f, row)
            a, _, _ = _intra(q, k, eq, ek, tt, ss)
            bl = b[CHUNK - 1:CHUNK, :]
            q0 = (q * jnp.exp(b)).astype(BF16)
            kh = (k * jnp.exp(bl - b)).astype(BF16)
            s_ref[i] = st
            o = jnp.dot(a.astype(BF16), v, preferred_element_type=F32)
            o = o + lax.dot_general(q0, st.astype(BF16), (((1,), (1,)), ((), ())), preferred_element_type=F32)
            o_ref[sl, :] = o
            return jnp.exp(bl) * st + lax.dot_general(v, kh, (((0,), (0,)), ((), ())), preferred_element_type=F32)

        per = 4 if nch % 4 == 0 else 2

        def trip(i, st):
            for u in range(per):
                st = step(per * i + u, st)
            return st

        lax.fori_loop(0, nch // per, trip, jnp.zeros((HEAD, HEAD), F32))

    col = lambda off: pl.BlockSpec((T, HEAD), functools.partial(lambda h, off: (0, off + h), off=off))
    return _pcall(
        body, name="hgrn_fwd", grid=(H,),
        in_specs=[col(0), col(H), col(2 * H), pl.BlockSpec((1, HEAD), lambda h: (0, h))],
        out_specs=[pl.BlockSpec((T, HEAD), lambda h: (0, h)),
                   pl.BlockSpec((None, nch, HEAD, HEAD), lambda h: (h, 0, 0, 0))],
        out_shape=[jax.ShapeDtypeStruct((T, d_model), F32), jax.ShapeDtypeStruct((H, nch, HEAD, HEAD), F32)],
        compiler_params=_params(("parallel",)),
    )(proj, proj, proj, lb)


def _hgrn_bwd(proj, lb, states, do, dgate, d_model):
    T = proj.shape[0]
    H = d_model // HEAD
    nch = T // CHUNK

    def body(q_ref, f_ref, v_ref, lb_ref, s_ref, do_ref, dg_ref, dp_ref, dlb_ref):
        dq_ref, df_ref, dv_ref = dp_ref.at[0], dp_ref.at[1], dp_ref.at[2]
        dp_ref[3] = dg_ref[...]
        lbv = lb_ref[...]
        row = lax.broadcasted_iota(jnp.int32, (CHUNK, HEAD), 0)
        tt = lax.broadcasted_iota(jnp.int32, (CHUNK, CHUNK), 0)
        ss = lax.broadcasted_iota(jnp.int32, (CHUNK, CHUNK), 1)
        last = row == CHUNK - 1
        nt = (((1,), (1,)), ((), ()))
        tn = (((0,), (0,)), ((), ()))

        def step(j, carry):
            dst, dlb = carry
            i = nch - 1 - j
            sl = pl.ds(pl.multiple_of(i * CHUNK, CHUNK), CHUNK)
            qr = q_ref[sl, :]
            q, k, gl, f, sg, sq = _gates(qr, f_ref[sl, :], lbv)
            v = v_ref[sl, :].astype(BF16)
            d_o = do_ref[sl, :].astype(BF16)
            st = s_ref[i]
            st16 = st.astype(BF16)
            dst16 = dst.astype(BF16)
            b, eq, ek = _chunk_decays(gl, f, row)
            a, qs, ks = _intra(q, k, eq, ek, tt, ss)
            bl = b[CHUNK - 1:CHUNK, :]
            e0 = jnp.exp(b)
            eh = jnp.exp(bl - b)
            ebl = jnp.exp(bl)
            q0 = q * e0
            kh = k * eh
            q016 = q0.astype(BF16)
            kh16 = kh.astype(BF16)
            dv = lax.dot_general(a.astype(BF16), d_o, tn, preferred_element_type=F32)
            dv = dv + lax.dot_general(kh16, dst16, nt, preferred_element_type=F32)
            dv_ref[sl, :] = dv.astype(dv_ref.dtype)
            da = lax.dot_general(d_o, v, nt, preferred_element_type=F32)
            da = jnp.where(tt >= ss, da, 0.0)
            dd = jnp.sum(jnp.where(tt == ss, da, 0.0), axis=1, keepdims=True)
            dq0 = jnp.dot(d_o, st16, preferred_element_type=F32)
            dkh = jnp.dot(v, dst16, preferred_element_type=F32)
            dq = dq0 * e0 + dd * k
            dk = dkh * eh + dd * q
            db = dq0 * q016.astype(F32) - dkh * kh16.astype(F32)
            g = CHUNK // 2
            for e_q, e_k, qg, kg in zip(eq, ek, qs, ks):
                dag = (da if 2 * g >= CHUNK else jnp.where((tt ^ ss) < 2 * g, da, 0.0)).astype(BF16)
                dqg = jnp.dot(dag, kg, preferred_element_type=F32)
                dkg = lax.dot_general(dag, qg, tn, preferred_element_type=F32)
                dq = dq + dqg * e_q
                dk = dk + dkg * e_k
                db = db + (dqg * qg.astype(F32) - dkg * kg.astype(F32))
                g //= 2
            dbl = jnp.sum(dkh * kh16.astype(F32), axis=0, keepdims=True) + ebl * jnp.sum(dst * st, axis=0, keepdims=True)
            db = db + jnp.where(last, dbl, 0.0)
            d = 1
            while d < CHUNK:
                db = db + jnp.where(row < CHUNK - d, pltpu.roll(db, CHUNK - d, 0), 0.0)
                d *= 2
            dfg = db / f - dk
            df_ref[sl, :] = (dfg * (1.0 - lbv) * sg * (1.0 - sg)).astype(df_ref.dtype)
            dq_ref[sl, :] = (dq * (HEAD ** -0.5) * (sq * (1.0 + qr * (1.0 - sq)))).astype(dq_ref.dtype)
            dlb = dlb + jnp.sum(dfg * (1.0 - sg), axis=0, keepdims=True)
            dst = ebl * dst + lax.dot_general(d_o, q016, tn, preferred_element_type=F32)
            return dst, dlb

        _, dlb = lax.fori_loop(0, nch // 2, lambda j, cr: step(2 * j + 1, step(2 * j, cr)),
                               (jnp.zeros((HEAD, HEAD), F32), jnp.zeros((1, HEAD), F32)))
        arow = lax.broadcasted_iota(jnp.int32, (SUBLANES, HEAD), 0)
        dlb_ref[...] = jnp.where(arow == 0, dlb, 0.0)

    col = lambda off: pl.BlockSpec((T, HEAD), functools.partial(lambda h, off: (0, off + h), off=off))
    return _pcall(
        body, name="hgrn_bwd", grid=(H,),
        in_specs=[col(0), col(H), col(2 * H), pl.BlockSpec((1, HEAD), lambda h: (0, h)),
                  pl.BlockSpec((None, nch, HEAD, HEAD), lambda h: (h, 0, 0, 0)), col(0), col(0)],
        out_specs=[pl.BlockSpec((4, T, HEAD), lambda h: (0, 0, h)), pl.BlockSpec((SUBLANES, HEAD), lambda h: (0, h))],
        out_shape=[jax.ShapeDtypeStruct((4, T, d_model), BF16), jax.ShapeDtypeStruct((SUBLANES, d_model), F32)],
        compiler_params=_params(("parallel",)),
    )(proj, proj, proj, lb, states, do, dgate)


def _lb_softmax(table):
    n, f = table.shape

    def body(t_ref, p_ref):
        t = t_ref[...]
        e = jnp.exp(t - jnp.max(t, axis=0, keepdims=True))
        p_ref[...] = e / jnp.sum(e, axis=0, keepdims=True)

    padded = jnp.pad(table, ((0, SUBLANES - n), (0, 0)), constant_values=-jnp.inf)
    return _pcall(body, name="lb_softmax", out_shape=jax.ShapeDtypeStruct((SUBLANES, f), F32))(padded)


def _adamw_math(w, g, m, v):
    m = ADAM_B1 * m + (1.0 - ADAM_B1) * g
    v = ADAM_B2 * v + (1.0 - ADAM_B2) * (g * g)
    m_hat = m / (1.0 - ADAM_B1 ** ADAM_STEP)
    v_hat = v / (1.0 - ADAM_B2 ** ADAM_STEP)
    delta = -ADAM_LR * (m_hat / (jnp.sqrt(v_hat) + ADAM_EPS) + ADAM_WD * w)
    return delta, m, v


def _adamw(w, g, m, v, name):
    R, C = w.shape
    tr = _pick(R, (128, 64, 32, 16, 8))

    def body(w_ref, g_ref, m_ref, v_ref, d_ref, nm_ref, nv_ref):
        d, nm, nv = _adamw_math(w_ref[...], g_ref[...], m_ref[...], v_ref[...])
        d_ref[...] = d
        nm_ref[...] = nm
        nv_ref[...] = nv

    spec = pl.BlockSpec((tr, C), lambda i: (i, 0))
    return _pcall(
        body, name=name, grid=(R // tr,), in_specs=[spec] * 4, out_specs=[spec] * 3,
        out_shape=[jax.ShapeDtypeStruct((R, C), F32)] * 3, compiler_params=_params(("parallel",)),
    )(w, g, m, v)


def _adamw_halves(w, m, v, g_mine, g_recv, c, name, layer=0, prev=None):
    C = w.shape[1]
    rh = g_mine.shape[0]
    tr = _pick(rh, (128, 64, 32, 16, 8))
    nb = rh // tr
    r0 = layer * 2 * nb

    def body(c_ref, w_ref, m_ref, v_ref, gm_ref, gr_ref, *rest):
        g_ref, d_ref, nm_ref, nv_ref = rest[-4:]
        g = jnp.where(pl.program_id(0) == c_ref[0], gm_ref[...], gr_ref[...])
        d, nm, nv = _adamw_math(w_ref[...], g, m_ref[...], v_ref[...])
        g_ref[...] = g
        d_ref[...] = d
        nm_ref[...] = nm
        nv_ref[...] = nv

    full = pl.BlockSpec((tr, C), lambda h, i, cr: (r0 + h * nb + i, 0))
    half = pl.BlockSpec((tr, C), lambda h, i, cr: (i, 0))
    in_specs = [full, full, full, half, half]
    args = [c, w, m, v, g_mine, g_recv]
    alias = {}
    if prev is not None:
        in_specs += [pl.BlockSpec(memory_space=pl.ANY)] * 4
        args += list(prev)
        alias = {6 + k: k for k in range(4)}
    return _pcall(
        body, name=name,
        grid_spec=pltpu.PrefetchScalarGridSpec(
            num_scalar_prefetch=1, grid=(2, nb), in_specs=in_specs, out_specs=[full] * 4),
        out_shape=[jax.ShapeDtypeStruct(w.shape, F32)] * 4, input_output_aliases=alias,
        compiler_params=_params(("parallel", "parallel")),
    )(*args)


def _lb_table_grad(p8, dlb, n):
    f = p8.shape[1]

    def body(p_ref, d_ref, o_ref):
        p = p_ref[...]
        d = d_ref[...]
        p0 = p[0:1, :]
        first = lax.broadcasted_iota(jnp.int32, p.shape, 0) == 0
        o_ref[...] = p * (jnp.where(first, d, 0.0) - d * p0)

    return _pcall(body, name="lb_table_grad", out_shape=jax.ShapeDtypeStruct((SUBLANES, f), F32))(p8, dlb)[:n]


def _place():
    x, y, c = lax.axis_index("x"), lax.axis_index("y"), lax.axis_index("c")
    chips = [(1 - x, y), (x, 1 - y), (1 - x, 1 - y)]
    return x, y, c, chips


HBM_SPEC = pl.BlockSpec(memory_space=pltpu.HBM)


def _gather_weights(big, small):
    nb, ns = len(big), len(small)
    n = nb + ns

    def body(*refs):
        ins, outs = refs[:n], refs[n:2 * n]
        send_sems, recv_sems, own_send, own_recv = refs[2 * n:]
        x, y, c, chips = _place()
        me = 2 * x + y
        sib = (x, y, 1 - c)
        own = [pltpu.make_async_remote_copy(
            src_ref=ins[t], dst_ref=outs[t].at[me], send_sem=own_send.at[t], recv_sem=own_recv.at[t],
            device_id=sib, device_id_type=MESH) for t in range(n)]
        for cp in own:
            cp.start()

        def half(t, h):
            rh = big[t].shape[0] // 2
            return pl.ds(pl.multiple_of(h * rh, rh), rh)

        sends = []
        for t in range(n):
            for j, chip in enumerate(chips):
                k = 6 * t + j
                if t < nb:
                    src, dst = ins[t].at[half(t, c)], outs[t].at[me, half(t, c)]
                else:
                    src, dst = ins[t], outs[t].at[me]
                sends.append(pltpu.make_async_remote_copy(
                    src_ref=src, dst_ref=dst, send_sem=send_sems.at[k], recv_sem=recv_sems.at[k],
                    device_id=(*chip, c), device_id_type=MESH))
        for cp in sends:
            cp.start()
        passed = []
        for t in range(n):
            for j, (cx, cy) in enumerate(chips):
                k = 6 * t + j
                s = 2 * cx + cy
                if t < nb:
                    landed = outs[t].at[s, half(t, c)]
                    pltpu.make_async_remote_copy(
                        src_ref=landed, dst_ref=landed, send_sem=send_sems.at[k], recv_sem=recv_sems.at[k],
                        device_id=sib, device_id_type=MESH).wait_recv()
                    fwd = pltpu.make_async_remote_copy(
                        src_ref=landed, dst_ref=landed, send_sem=send_sems.at[k + 3], recv_sem=recv_sems.at[k + 3],
                        device_id=sib, device_id_type=MESH)
                    fwd.start()
                    passed.append(fwd)
                else:
                    landed = outs[t].at[s]
                    pltpu.make_async_remote_copy(
                        src_ref=landed, dst_ref=landed, send_sem=send_sems.at[k], recv_sem=recv_sems.at[k],
                        device_id=sib, device_id_type=MESH).wait_recv()
        for t in range(nb):
            for j, (cx, cy) in enumerate(chips):
                k = 6 * t + j
                other = outs[t].at[2 * cx + cy, half(t, 1 - c)]
                pltpu.make_async_remote_copy(
                    src_ref=other, dst_ref=other, send_sem=send_sems.at[k + 3], recv_sem=recv_sems.at[k + 3],
                    device_id=sib, device_id_type=MESH).wait_recv()
        for cp in sends + passed:
            cp.wait_send()
        for cp in own:
            cp.wait()

    arrs = list(big) + list(small)
    return _pcall(
        body, name="gather_weights", in_specs=[HBM_SPEC] * n, out_specs=[HBM_SPEC] * n,
        out_shape=[jax.ShapeDtypeStruct((N_CHIPS,) + a.shape, a.dtype) for a in arrs],
        scratch_shapes=[pltpu.SemaphoreType.DMA((6 * n,)), pltpu.SemaphoreType.DMA((6 * n,)),
                        pltpu.SemaphoreType.DMA((n,)), pltpu.SemaphoreType.DMA((n,))],
    )(*arrs)


SEM_SPEC = pl.BlockSpec(memory_space=pltpu.SEMAPHORE)
DATAFLOW = pltpu.SideEffectType.DATAFLOW_SIDE_EFFECTING
COPIES_PER_SHARD = 4


def _shard_copies(ins, lands, send_sems, recv_sems):
    x, y, c, chips = _place()
    me = 2 * x + y
    cps = []
    for t in range(len(ins)):
        rh = ins[t].shape[0] // 2
        half = pl.ds(pl.multiple_of(c * rh, rh), rh)
        for j, chip in enumerate(chips):
            k = COPIES_PER_SHARD * t + j
            cps.append(pltpu.make_async_remote_copy(
                src_ref=ins[t].at[half], dst_ref=lands[t].at[me, half], send_sem=send_sems.at[k],
                recv_sem=recv_sems.at[k], device_id=(*chip, c), device_id_type=MESH))
        k = COPIES_PER_SHARD * t + 3
        cps.append(pltpu.make_async_remote_copy(
            src_ref=ins[t], dst_ref=lands[t].at[me], send_sem=send_sems.at[k], recv_sem=recv_sems.at[k],
            device_id=(x, y, 1 - c), device_id_type=MESH))
    return cps


def _gather_start(shards, thru, name):
    n = len(shards)

    def body(*refs):
        ins, lands = refs[:n], refs[n:2 * n]
        send_sems, recv_sems = refs[2 * n + 1], refs[2 * n + 2]
        for cp in _shard_copies(ins, lands, send_sems, recv_sems):
            cp.start()

    lands = [pltpu.with_memory_space_constraint(lax.empty((N_CHIPS,) + s.shape, s.dtype), pltpu.HBM) for s in shards]
    ops = [pltpu.with_memory_space_constraint(s, pltpu.HBM) for s in shards] + lands + [thru]
    nsem = COPIES_PER_SHARD * n
    res = _pcall(
        body, name=name, in_specs=[HBM_SPEC] * (2 * n + 1),
        out_specs=[SEM_SPEC, SEM_SPEC] + [HBM_SPEC] * (2 * n + 1),
        out_shape=[pltpu.SemaphoreType.DMA((nsem,)), pltpu.SemaphoreType.DMA((nsem,))]
        + [pltpu.HBM(o.shape, o.dtype) for o in ops],
        input_output_aliases={i: 2 + i for i in range(2 * n + 1)},
        compiler_params=pltpu.CompilerParams(has_side_effects=DATAFLOW),
    )(*ops)
    return res[0], res[1], res[2:2 + n], res[2 + n:2 + 2 * n], res[2 + 2 * n]


def _gather_wait(send_sems, recv_sems, shards, lands, after, name):
    n = len(shards)

    def body(*refs):
        ins, lnd = refs[:n], refs[n:2 * n]
        ssem, rsem = refs[2 * n], refs[2 * n + 1]
        for cp in _shard_copies(ins, lnd, ssem, rsem):
            cp.wait_send()
            cp.wait_recv()

    res = _pcall(
        body, name=name,
        in_specs=[HBM_SPEC] * (2 * n) + [SEM_SPEC, SEM_SPEC, pl.BlockSpec(memory_space=pl.ANY)],
        out_specs=[HBM_SPEC] * (2 * n),
        out_shape=[pltpu.HBM(o.shape, o.dtype) for o in list(shards) + list(lands)],
        input_output_aliases={i: i for i in range(2 * n)},
        compiler_params=pltpu.CompilerParams(has_side_effects=DATAFLOW),
    )(*shards, *lands, send_sems, recv_sems, after)
    return res[n:]


def _forward_copies(land, send_sems, recv_sems):
    x, y, c, chips = _place()
    rh = land.shape[1] // 2
    return [pltpu.make_async_remote_copy(
        src_ref=land.at[2 * cx + cy, pl.ds(pl.multiple_of(c * rh, rh), rh)],
        dst_ref=land.at[2 * cx + cy, pl.ds(pl.multiple_of(c * rh, rh), rh)],
        send_sem=send_sems.at[j], recv_sem=recv_sems.at[j], device_id=(x, y, 1 - c), device_id_type=MESH)
        for j, (cx, cy) in enumerate(chips)]


def _forward_start(land, name):
    def body(land_ref, send_sems, recv_sems, out_ref):
        for cp in _forward_copies(land_ref, send_sems, recv_sems):
            cp.start()

    return _pcall(
        body, name=name, in_specs=[HBM_SPEC], out_specs=[SEM_SPEC, SEM_SPEC, HBM_SPEC],
        out_shape=[pltpu.SemaphoreType.DMA((3,)), pltpu.SemaphoreType.DMA((3,)), pltpu.HBM(land.shape, land.dtype)],
        input_output_aliases={0: 2}, compiler_params=pltpu.CompilerParams(has_side_effects=DATAFLOW),
    )(land)


def _forward_wait(send_sems, recv_sems, land, after, name):
    def body(land_ref, ssem, rsem, after_ref, out_ref):
        for cp in _forward_copies(land_ref, ssem, rsem):
            cp.wait_send()
            cp.wait_recv()

    return _pcall(
        body, name=name, in_specs=[HBM_SPEC, SEM_SPEC, SEM_SPEC, pl.BlockSpec(memory_space=pl.ANY)],
        out_specs=HBM_SPEC, out_shape=pltpu.HBM(land.shape, land.dtype), input_output_aliases={0: 0},
        compiler_params=pltpu.CompilerParams(has_side_effects=DATAFLOW),
    )(land, send_sems, recv_sems, after)


def _sibling_copies(ins, lands, send_sems, recv_sems, other_half):
    x, y, c, _ = _place()
    return [pltpu.make_async_remote_copy(
        src_ref=ins[t].at[:, 1 - c] if other_half else ins[t], dst_ref=lands[t], send_sem=send_sems.at[t],
        recv_sem=recv_sems.at[t], device_id=(x, y, 1 - c), device_id_type=MESH) for t in range(len(ins))]


def _sibling_start(srcs, other_half, thru, name):
    n = len(srcs)
    nthru = 0 if thru is None else 1

    def body(*refs):
        ins, lands = refs[:n], refs[n:2 * n]
        send_sems, recv_sems = refs[2 * n + nthru], refs[2 * n + nthru + 1]
        for cp in _sibling_copies(ins, lands, send_sems, recv_sems, other_half):
            cp.start()

    shapes = [(s.shape[0],) + s.shape[2:] if other_half else s.shape for s in srcs]
    lands = [pltpu.with_memory_space_constraint(lax.empty(sh, s.dtype), pltpu.HBM) for sh, s in zip(shapes, srcs)]
    ops = [pltpu.with_memory_space_constraint(s, pltpu.HBM) for s in srcs] + lands + ([] if thru is None else [thru])
    res = _pcall(
        body, name=name, in_specs=[HBM_SPEC] * len(ops),
        out_specs=[SEM_SPEC, SEM_SPEC] + [HBM_SPEC] * len(ops),
        out_shape=[pltpu.SemaphoreType.DMA((n,)), pltpu.SemaphoreType.DMA((n,))]
        + [pltpu.HBM(o.shape, o.dtype) for o in ops],
        input_output_aliases={i: 2 + i for i in range(len(ops))},
        compiler_params=pltpu.CompilerParams(has_side_effects=DATAFLOW),
    )(*ops)
    return res[0], res[1], res[2:2 + n], res[2 + n:2 + 2 * n], (None if thru is None else res[2 + 2 * n])


def _sibling_wait(send_sems, recv_sems, srcs, lands, other_half, after, name):
    n = len(srcs)

    def body(*refs):
        ins, lnd = refs[:n], refs[n:2 * n]
        ssem, rsem = refs[2 * n], refs[2 * n + 1]
        for cp in _sibling_copies(ins, lnd, ssem, rsem, other_half):
            cp.wait_send()
            cp.wait_recv()

    res = _pcall(
        body, name=name,
        in_specs=[HBM_SPEC] * (2 * n) + [SEM_SPEC, SEM_SPEC, pl.BlockSpec(memory_space=pl.ANY)],
        out_specs=[HBM_SPEC] * (2 * n),
        out_shape=[pltpu.HBM(o.shape, o.dtype) for o in list(srcs) + list(lands)],
        input_output_aliases={i: i for i in range(2 * n)},
        compiler_params=pltpu.CompilerParams(has_side_effects=DATAFLOW),
    )(*srcs, *lands, send_sems, recv_sems, after)
    return res[:n], res[n:]


def _chip_copies(ins, lands, send_sems, recv_sems):
    x, y, c, chips = _place()
    cps = []
    for t in range(len(ins)):
        for j, (cx, cy) in enumerate(chips):
            cps.append(pltpu.make_async_remote_copy(
                src_ref=ins[t].at[2 * cx + cy], dst_ref=lands[t].at[j],
                send_sem=send_sems.at[3 * t + j], recv_sem=recv_sems.at[3 * t + j],
                device_id=(cx, cy, c), device_id_type=MESH))
    return cps


def _chip_start(parts, thru, name):
    n = len(parts)

    def body(*refs):
        ins, lands = refs[:n], refs[n:2 * n]
        send_sems, recv_sems = refs[2 * n + 1], refs[2 * n + 2]
        for cp in _chip_copies(ins, lands, send_sems, recv_sems):
            cp.start()

    lands = [pltpu.with_memory_space_constraint(lax.empty((3,) + p.shape[1:], p.dtype), pltpu.HBM) for p in parts]
    ops = [pltpu.with_memory_space_constraint(p, pltpu.HBM) for p in parts] + lands + [thru]
    res = _pcall(
        body, name=name, in_specs=[HBM_SPEC] * (2 * n + 1),
        out_specs=[SEM_SPEC, SEM_SPEC] + [HBM_SPEC] * (2 * n + 1),
        out_shape=[pltpu.SemaphoreType.DMA((3 * n,)), pltpu.SemaphoreType.DMA((3 * n,))]
        + [pltpu.HBM(o.shape, o.dtype) for o in ops],
        input_output_aliases={i: 2 + i for i in range(2 * n + 1)},
        compiler_params=pltpu.CompilerParams(has_side_effects=DATAFLOW),
    )(*ops)
    return res[0], res[1], res[2:2 + n], res[2 + n:2 + 2 * n], res[2 + 2 * n]


def _chip_wait(send_sems, recv_sems, parts, lands, after, name):
    n = len(parts)

    def body(*refs):
        ins, lnd = refs[:n], refs[n:2 * n]
        ssem, rsem = refs[2 * n], refs[2 * n + 1]
        for cp in _chip_copies(ins, lnd, ssem, rsem):
            cp.wait_send()
            cp.wait_recv()

    res = _pcall(
        body, name=name,
        in_specs=[HBM_SPEC] * (2 * n) + [SEM_SPEC, SEM_SPEC, pl.BlockSpec(memory_space=pl.ANY)],
        out_specs=[HBM_SPEC] * (2 * n),
        out_shape=[pltpu.HBM(o.shape, o.dtype) for o in list(parts) + list(lands)],
        input_output_aliases={i: i for i in range(2 * n)},
        compiler_params=pltpu.CompilerParams(has_side_effects=DATAFLOW),
    )(*parts, *lands, send_sems, recv_sems, after)
    return res[:n], res[n:]


def _add_pair(grad, recv, c, name):
    s, _, rh, cc = grad.shape
    tr = _pick(rh, (256, 128, 64, 32, 16))

    def body(c_ref, g_ref, r_ref, o_ref):
        o_ref[...] = (g_ref[...].astype(F32) + r_ref[...].astype(F32)).astype(o_ref.dtype)

    return _pcall(
        body, name=name,
        grid_spec=pltpu.PrefetchScalarGridSpec(
            num_scalar_prefetch=1, grid=(s, rh // tr),
            in_specs=[pl.BlockSpec((None, None, tr, cc), lambda a, i, cr: (a, cr[0], i, 0)),
                      pl.BlockSpec((None, tr, cc), lambda a, i, cr: (a, i, 0))],
            out_specs=pl.BlockSpec((None, tr, cc), lambda a, i, cr: (a, i, 0))),
        out_shape=jax.ShapeDtypeStruct((s, rh, cc), BF16),
        compiler_params=_params(("parallel", "parallel")),
    )(c, grad, recv)


def _add_chips(part, recv, me, name):
    _, rh, cc = part.shape
    tr = _pick(rh, (256, 128, 64, 32, 16))

    def body(m_ref, p_ref, r_ref, o_ref):
        o_ref[...] = ((p_ref[...].astype(F32) + r_ref[0].astype(F32)) + r_ref[1].astype(F32)) + r_ref[2].astype(F32)

    return _pcall(
        body, name=name,
        grid_spec=pltpu.PrefetchScalarGridSpec(
            num_scalar_prefetch=1, grid=(rh // tr,),
            in_specs=[pl.BlockSpec((None, tr, cc), lambda i, mr: (mr[0], i, 0)),
                      pl.BlockSpec((3, tr, cc), lambda i, mr: (0, i, 0))],
            out_specs=pl.BlockSpec((tr, cc), lambda i, mr: (i, 0))),
        out_shape=jax.ShapeDtypeStruct((rh, cc), F32),
        compiler_params=_params(("parallel",)),
    )(me, part, recv)


def _all_sum(vec):
    rows = vec.shape[0]

    def body(v_ref, o_ref, buf, send_sems, recv_sems):
        x, y, c, _ = _place()
        me = 4 * x + 2 * y + c
        buf[me] = v_ref[...]
        cps = []
        for r in range(1, 8):
            fx, fy, fc = (r >> 2) & 1, (r >> 1) & 1, r & 1
            peer = (x ^ fx, y ^ fy, c ^ fc)
            cps.append(pltpu.make_async_remote_copy(
                src_ref=v_ref, dst_ref=buf.at[me], send_sem=send_sems.at[r - 1], recv_sem=recv_sems.at[r - 1],
                device_id=peer, device_id_type=MESH))
        for cp in cps:
            cp.start()
        for r in range(1, 8):
            src = me ^ r
            pltpu.make_async_remote_copy(
                src_ref=v_ref, dst_ref=buf.at[src], send_sem=send_sems.at[r - 1], recv_sem=recv_sems.at[r - 1],
                device_id=(x, y, c), device_id_type=MESH).wait_recv()
        for cp in cps:
            cp.wait_send()
        acc = buf[0]
        for d in range(1, 8):
            acc = acc + buf[d]
        o_ref[...] = acc

    return _pcall(
        body, name="all_sum_small",
        in_specs=[pl.BlockSpec(memory_space=pltpu.VMEM)], out_specs=pl.BlockSpec(memory_space=pltpu.VMEM),
        out_shape=jax.ShapeDtypeStruct((rows, LANES), F32),
        scratch_shapes=[pltpu.VMEM((8, rows, LANES), F32), pltpu.SemaphoreType.DMA((7,)), pltpu.SemaphoreType.DMA((7,))],
    )(vec)


def _pack(parts):
    flat = jnp.concatenate([p.reshape(-1) for p in parts])
    tile = SUBLANES * LANES
    pad = (-flat.shape[0]) % tile
    return jnp.pad(flat, (0, pad)).reshape(-1, LANES)


def _unpack(vec, shapes):
    flat = vec.reshape(-1)
    out, p = [], 0
    for s in shapes:
        n = 1
        for d in s:
            n *= d
        out.append(flat[p:p + n].reshape(s))
        p += n
    return out


def _local_step(x, tgt, norm_mix, norm_ffn, lb8, out_norm, final_norm, sc_conv, ffn_conv, weights, reduce_start,
                reduce_finish):
    T, D = x.shape
    F2 = ffn_conv.shape[-1]
    FF = F2 // 2
    tm = _pick(T, (1024, 512, 256, 128))
    wide = (1536, 1408, 1024, 768, 512, 384, 256, 128)
    cw_h, cw_s, cw_u = 4 * D // N_CHIPS, 3 * D // N_CHIPS, F2 // N_CHIPS
    kp = FF // N_CHIPS
    tk_ff = kp if kp % LANES == 0 else LANES
    tn_d = _pick(D, (1024, 512, 256, 128))
    tk_w = _pick(D, (512, 256, 128))
    tn_h = _pick(cw_h, (1024, 512, 256, 128))
    tn_s = _pick(D // N_CHIPS, (512, 256, 128))
    tn_u = _pick(cw_u, wide)
    lb = lb8[0:1]
    wm_sq = _wmap_col(D, tn_d, 0)
    wm_sq1 = _wmap_col(D, D, 0)
    seg1 = lambda a: a.reshape((1,) + a.shape)

    def mix_in(h, w):
        return _row_call(_rms_fwd_fn, [(h, 0, D)], [w], [(D, BF16)], 0, "rms_fwd")[0]

    def rms_bwd(h, dxn, dh, w):
        return _row_call(_rms_bwd_fn, [(h, 0, D), (dxn, 0, D), (dh, 0, D)], [w], [(D, F32), (D, BF16)], 1, "rms_bwd")

    def ffn_fwd(h, i, fetch_up, fetch_down):
        xn = mix_in(h, norm_ffn[i:i + 1])
        tn = _pick(cw_u, wide)
        w_up = fetch_up(xn)
        up = _mm_nn(xn, w_up, _wmap_col(cw_u, tn, 0), D, F2, tm, D, tn, "ffn_up")
        nb = FF // LANES
        a = _col_call(_glu_fwd_fn, [(up, 0), (up, nb)], [(ffn_conv[i], 0), (ffn_conv[i], nb)], [(1, FF, BF16)], 0,
                      "glu_fwd", before=True, after=False)[0][0]
        w_down = fetch_down(a)
        h2 = _mm_nn(a, w_down, _wmap_row(kp, tk_ff, 0), FF, D, tm, tk_ff, tn_d, "ffn_down", res=h)
        return h2, (xn, up, a), w_up, w_down

    def ffn_bwd(dh, dh16, h, saved, i, w_up, w_down):
        xn, up, a = saved
        g_down = _mm_tn(a, seg1(dh16), (N_CHIPS, 1, kp, D), _wmap_row(kp, tk_ff, 0), FF, D, tk_ff, tn_d,
                        "ffn_down_dw", tm=_pick(T, (2048, 1024, 512, 256, 128)))
        dh16 = reduce_start(("ffn_w_down", i), g_down, dh16)
        da = _mm_nt(seg1(dh16), w_down, _wmap_row(kp, tk_ff, 0), FF, D, tm, tk_ff, D, "ffn_down_dx")
        nb = FF // LANES
        dgv, cg, cv = _col_call(_glu_bwd_fn, [(up, 0), (up, nb), (da, 0)], [(ffn_conv[i], 0), (ffn_conv[i], nb)],
                                [(2, FF, BF16)], 2, "glu_bwd", before=True, after=True)
        g_up = _mm_tn(xn, dgv, (N_CHIPS, 1, D, cw_u), _wmap_col(cw_u, tn_u, 0), D, F2, tk_w, tn_u, "ffn_up_dw")
        dgv = reduce_start(("ffn_w_up", i), g_up, dgv)
        dxn = _mm_nt(dgv, w_up, _wmap_col(cw_u, tn_u, 0), D, F2, tm, D, tn_u, "ffn_up_dx")
        dh2, dh2_16, dnw = rms_bwd(h, dxn, dh, norm_ffn[i:i + 1])
        return dh2, reduce_finish(dh2_16), dnw, jnp.concatenate([cg[:3], cv[:3]], axis=1)

    h0 = x
    xn0 = mix_in(h0, norm_mix[0:1])
    fetch_hin, = weights(0, xn0)
    w_hin = fetch_hin(xn0)
    proj = _mm_nn(xn0, w_hin, _wmap_col(cw_h, tn_h, 0), D, 4 * D, tm, D, tn_h, "hgrn_in")
    o, states = _hgrn_fwd(proj, lb, D)
    on = _row_call(_onorm_fwd_fn, [(o, 0, D), (proj, 3, D)], [out_norm], [(D, BF16)], 0, "onorm_fwd")[0]
    fetch_hout, fetch_up0, fetch_down0 = weights(1, on)
    w_hout1 = fetch_hout(on).reshape(1, D, D)
    h1 = _mm_nn(on, w_hout1, wm_sq, D, D, tm, D, tn_d, "hgrn_out", res=h0)
    h2, ffn0, w_up0, w_down0 = ffn_fwd(h1, 0, fetch_up0, fetch_down0)
    xn1 = mix_in(h2, norm_mix[1:2])
    fetch_sin, fetch_sout, fetch_up1, fetch_down1 = weights(2, xn1)
    w_sin = fetch_sin(xn1)
    tn_si = _pick(cw_s, wide)
    sproj = _mm_nn(xn1, w_sin, _wmap_col(cw_s, tn_si, 0), D, 3 * D, tm, D, tn_si, "sc_in")
    nd = D // LANES
    ysc = _col_call(_sc_fwd_fn, [(sproj, 0), (sproj, nd), (sproj, 2 * nd)], [(sc_conv, 0)], [(1, D, BF16)], 0,
                    "sc_fwd", before=True, after=False)[0][0]
    w_sout1 = fetch_sout(ysc).reshape(1, D, D)
    h3 = _mm_nn(ysc, w_sout1, wm_sq, D, D, tm, D, tn_d, "sc_out", res=h2)
    h4, ffn1, w_up1, w_down1 = ffn_fwd(h3, 1, fetch_up1, fetch_down1)

    dh, dh16, esq, dfinal = _row_call(_final_fn, [(h4, 0, D), (tgt, 0, D)], [final_norm], [(D, F32), (D, BF16)], 2,
                                      "final_loss")
    loss = 0.5 / D * jnp.sum(esq)
    dh, dh16, dnf1, dconv1 = ffn_bwd(dh, dh16, h3, ffn1, 1, w_up1, w_down1)
    g_sout = _mm_tn(ysc, seg1(dh16), (1, 1, D, D), wm_sq, D, D, tk_w, tn_d, "sc_out_dw")
    dh16 = reduce_start(("sc_w_out", 0), g_sout, dh16)
    dy = _mm_nt(seg1(dh16), w_sout1, wm_sq1, D, D, tm, D, D, "sc_out_dx")
    dsp, dscc = _col_call(_sc_bwd_fn, [(sproj, 0), (sproj, nd), (sproj, 2 * nd), (dy, 0)], [(sc_conv, 0)],
                          [(3, D, BF16)], 1, "sc_bwd", before=True, after=True)
    g_sin = _mm_tn(xn1, dsp, (N_CHIPS, 1, D, cw_s), _wmap_col(cw_s, tn_s, 0), D, 3 * D, tk_w, tn_s, "sc_in_dw")
    dsp = reduce_start(("sc_w_in", 0), g_sin, dsp)
    dxn = _mm_nt(dsp, w_sin, _wmap_col(cw_s, tn_s, 0), D, 3 * D, tm, D, tn_s, "sc_in_dx", per_step=3)
    dh, dh16, dnm1 = rms_bwd(h2, dxn, dh, norm_mix[1:2])
    dh16 = reduce_finish(dh16)
    dh, dh16, dnf0, dconv0 = ffn_bwd(dh, dh16, h1, ffn0, 0, w_up0, w_down0)
    g_hout = _mm_tn(on, seg1(dh16), (1, 1, D, D), wm_sq, D, D, tk_w, tn_d, "hgrn_out_dw")
    dh16 = reduce_start(("hgrn_w_out", 0), g_hout, dh16)
    don = _mm_nt(seg1(dh16), w_hout1, wm_sq1, D, D, tm, D, D, "hgrn_out_dx")
    do, dgate, dgain = _row_call(_onorm_bwd_fn, [(o, 0, D), (proj, 3, D), (don, 0, D)], [out_norm],
                                 [(D, F32), (D, BF16)], 1, "onorm_bwd")
    dproj, dlb = _hgrn_bwd(proj, lb, states, do, dgate, D)
    g_hin = _mm_tn(xn0, dproj, (N_CHIPS, 1, D, cw_h), _wmap_col(cw_h, tn_h, 0), D, 4 * D, tk_w, tn_h, "hgrn_in_dw")
    dproj = reduce_start(("hgrn_w_in", 0), g_hin, dproj)
    dxn = _mm_nt(dproj, w_hin, _wmap_col(cw_h, tn_h, 0), D, 4 * D, tm, D, tn_h, "hgrn_in_dx", per_step=2)
    grad_x, _, dnm0 = rms_bwd(h0, dxn, dh, norm_mix[0:1])
    grad_x = reduce_finish(grad_x)

    small = dict(
        loss=loss,
        norm_mix=jnp.stack([jnp.sum(dnm0, axis=0), jnp.sum(dnm1, axis=0)]),
        norm_ffn=jnp.stack([jnp.sum(dnf0, axis=0), jnp.sum(dnf1, axis=0)]),
        lb=dlb[0:1],
        out_norm=jnp.sum(dgain, axis=0)[None],
        final_norm=jnp.sum(dfinal, axis=0),
        sc_conv=dscc[:3],
        ffn_conv=jnp.stack([dconv0, dconv1]),
    )
    return grad_x, small


def kernel(x, norm_mix, norm_ffn, hgrn_w_in, hgrn_lb_table, hgrn_out_norm, hgrn_w_out, sc_w_in, sc_conv, sc_w_out, ffn_w_up, ffn_conv, ffn_w_down, final_norm, loss_target, m_norm_mix, m_norm_ffn, m_hgrn_w_in, m_hgrn_lb_table, m_hgrn_out_norm, m_hgrn_w_out, m_sc_w_in, m_sc_conv, m_sc_w_out, m_ffn_w_up, m_ffn_conv, m_ffn_w_down, m_final_norm, v_norm_mix, v_norm_ffn, v_hgrn_w_in, v_hgrn_lb_table, v_hgrn_out_norm, v_hgrn_w_out, v_sc_w_in, v_sc_conv, v_sc_w_out, v_ffn_w_up, v_ffn_conv, v_ffn_w_down, v_final_norm):
    D = x.shape[-1]
    xi, yi, ci = lax.axis_index("x"), lax.axis_index("y"), lax.axis_index("c")
    me_chip = (2 * xi + yi).astype(jnp.int32).reshape(1)
    me_core = ci.astype(jnp.int32).reshape(1)

    big_names = ["hgrn_w_in", "hgrn_w_out", "sc_w_in", "sc_w_out", "ffn_w_up", "ffn_w_down"]
    big_w = dict(hgrn_w_in=hgrn_w_in, hgrn_w_out=hgrn_w_out, sc_w_in=sc_w_in, sc_w_out=sc_w_out,
                 ffn_w_up=ffn_w_up, ffn_w_down=ffn_w_down)
    big_m = dict(hgrn_w_in=m_hgrn_w_in, hgrn_w_out=m_hgrn_w_out, sc_w_in=m_sc_w_in, sc_w_out=m_sc_w_out,
                 ffn_w_up=m_ffn_w_up, ffn_w_down=m_ffn_w_down)
    big_v = dict(hgrn_w_in=v_hgrn_w_in, hgrn_w_out=v_hgrn_w_out, sc_w_in=v_sc_w_in, sc_w_out=v_sc_w_out,
                 ffn_w_up=v_ffn_w_up, ffn_w_down=v_ffn_w_down)
    flat2 = lambda a: a.reshape(-1, a.shape[-1])

    sh = lambda a: a.reshape(-1, a.shape[-1]).astype(BF16)
    conv_shards = [flat2(sc_conv), flat2(ffn_conv)]
    stages = [[sh(hgrn_w_in)], [sh(hgrn_w_out), sh(ffn_w_up[0]), sh(ffn_w_down[0])],
              [sh(sc_w_in), sh(sc_w_out), sh(ffn_w_up[1]), sh(ffn_w_down[1])]]
    gathers = []
    for k, shards in enumerate(stages):
        ss, rs, src, land, norm_mix = _gather_start(shards, norm_mix, "gather_start_%d" % k)
        gathers.append((ss, rs, src, land))
    scc4, fcc4 = _gather_weights([], [flat2(sc_conv), flat2(ffn_conv)])
    scc = jnp.moveaxis(scc4, 0, 1).reshape(3, D)
    f2 = ffn_conv.shape[-1] * N_CHIPS
    fcc = jnp.moveaxis(fcc4.reshape(N_CHIPS, 2, 3, -1), 0, 2).reshape(2, 3, f2)

    def weights(stage, after):
        lands = _gather_wait(*gathers[stage], after, "gather_wait_%d" % stage)
        fetch = []
        for t, land in enumerate(lands):
            ss, rs, land = _forward_start(land, "gather_forward_start_%d_%d" % (stage, t))
            fetch.append(functools.partial(_forward_wait, ss, rs, land, name="gather_forward_wait_%d_%d" % (stage, t)))
        return fetch

    pending = []
    started = []

    def reduce_start(slot, grad, thru):
        t = sum(len(b[0]) for b in pending) + len(started)
        halves = grad.reshape(N_CHIPS, 2, -1, grad.shape[-1])
        ss, rs, src, land, thru = _sibling_start([halves], True, thru, "grad_pair_start_%d" % t)
        started.append((slot, t, ss, rs, src, land))
        return thru

    def reduce_finish(thru):
        k = len(pending)
        pair = []
        for slot, t, ss, rs, src, land in started:
            src, recv = _sibling_wait(ss, rs, src, land, True, thru, "grad_pair_wait_%d" % t)
            pair.append(_add_pair(src[0], recv[0], me_core, "grad_add_pair"))
        ss, rs, pair, land, thru = _chip_start(pair, thru, "grad_chip_start_%d" % k)
        pending.append(([s[0] for s in started], ss, rs, pair, land))
        started.clear()
        return thru

    lb8 = _lb_softmax(hgrn_lb_table)
    grad_x, small = _local_step(
        x[0], loss_target[0], norm_mix, norm_ffn, lb8, hgrn_out_norm, final_norm[None], scc, fcc, weights,
        reduce_start, reduce_finish)

    small_names = ["loss", "norm_mix", "norm_ffn", "lb", "out_norm", "final_norm", "sc_conv", "ffn_conv"]
    parts = [small[n].astype(F32) for n in small_names]
    shapes = [p.shape for p in parts]
    tot = dict(zip(small_names, _unpack(_all_sum(_pack(parts)), shapes)))
    loss = tot["loss"].reshape(())
    g_lb_table = _lb_table_grad(lb8, tot["lb"], hgrn_lb_table.shape[0])
    cw = sc_conv.shape[-1]
    g_sc_conv = lax.dynamic_slice_in_dim(tot["sc_conv"], me_chip[0] * cw, cw, axis=1)[None]
    cf = ffn_conv.shape[-1]
    g_ffn_conv = lax.dynamic_slice_in_dim(tot["ffn_conv"], me_chip[0] * cf, cf, axis=2)
    g_small = dict(norm_mix=tot["norm_mix"], norm_ffn=tot["norm_ffn"], hgrn_lb_table=g_lb_table,
                   hgrn_out_norm=tot["out_norm"], sc_conv=g_sc_conv, ffn_conv=g_ffn_conv, final_norm=tot["final_norm"])
    w_small = dict(norm_mix=norm_mix, norm_ffn=norm_ffn, hgrn_lb_table=hgrn_lb_table, hgrn_out_norm=hgrn_out_norm,
                   sc_conv=sc_conv, ffn_conv=ffn_conv, final_norm=final_norm)
    m_small = dict(norm_mix=m_norm_mix, norm_ffn=m_norm_ffn, hgrn_lb_table=m_hgrn_lb_table, hgrn_out_norm=m_hgrn_out_norm,
                   sc_conv=m_sc_conv, ffn_conv=m_ffn_conv, final_norm=m_final_norm)
    v_small = dict(norm_mix=v_norm_mix, norm_ffn=v_norm_ffn, hgrn_lb_table=v_hgrn_lb_table, hgrn_out_norm=v_hgrn_out_norm,
                   sc_conv=v_sc_conv, ffn_conv=v_ffn_conv, final_norm=v_final_norm)
    sm_names = list(g_small)
    sm_shapes = [w_small[n].shape for n in sm_names]
    d_s, m_s, v_s = _adamw(_pack([w_small[n] for n in sm_names]), _pack([g_small[n] for n in sm_names]),
                           _pack([m_small[n] for n in sm_names]), _pack([v_small[n] for n in sm_names]), "adamw_small")
    out_g, out_d, out_m, out_v = dict(g_small), {}, {}, {}
    for n, d_, m_, v_ in zip(sm_names, _unpack(d_s, sm_shapes), _unpack(m_s, sm_shapes), _unpack(v_s, sm_shapes)):
        out_d[n], out_m[n], out_v[n] = d_, m_, v_

    done = {}
    after = grad_x

    def add_and_share(k, after):
        slots, ss, rs, pair, land = pending[k]
        pair, recv = _chip_wait(ss, rs, pair, land, after, "grad_chip_wait_%d" % k)
        mine = [_add_chips(p, r, me_chip, "grad_add_chips") for p, r in zip(pair, recv)]
        ss, rs, mine, land, _ = _sibling_start(mine, False, None, "grad_share_start_%d" % k)
        return slots, ss, rs, mine, land

    def update(k, share, after):
        slots, ss, rs, mine, land = share
        mine, theirs = _sibling_wait(ss, rs, mine, land, False, after, "grad_share_wait_%d" % k)
        for (n, layer), gm, gr in zip(slots, mine, theirs):
            done[n] = _adamw_halves(flat2(big_w[n]), flat2(big_m[n]), flat2(big_v[n]), gm, gr, me_core, "adamw_" + n,
                                    layer=layer, prev=done.get(n))
        return done[slots[-1][0]][0]

    shares = []
    for k in range(len(pending) - 1):
        shares.append(add_and_share(k, after))
        after = shares[-1][3][0]
    for k, share in enumerate(shares):
        after = update(k, share, after)
    last = len(pending) - 1
    share = add_and_share(last, after)
    update(last, share, share[3][0])
    for n in big_names:
        out_g[n], out_d[n], out_m[n], out_v[n] = (a.reshape(big_w[n].shape) for a in done[n])

    order = ["norm_mix", "norm_ffn", "hgrn_w_in", "hgrn_lb_table", "hgrn_out_norm", "hgrn_w_out", "sc_w_in", "sc_conv",
             "sc_w_out", "ffn_w_up", "ffn_conv", "ffn_w_down", "final_norm"]
    return (loss, grad_x[None], *[out_g[n] for n in order], *[out_d[n] for n in order],
            *[out_m[n] for n in order], *[out_v[n] for n in order])
```

```python
import functools

import jax
import jax.numpy as jnp
from jax import lax
from jax.experimental import pallas as pl
from jax.experimental.pallas import tpu as pltpu

F32 = jnp.float32
BF16 = jnp.bfloat16
MESH = pl.DeviceIdType.MESH

EPS = 1e-6
CHUNK = 64
HEAD = 128
N_CHIPS = 4
ADAM_LR, ADAM_B1, ADAM_B2, ADAM_EPS, ADAM_WD, ADAM_STEP = 0.001, 0.9, 0.999, 1e-08, 0.01, 10
VMEM_LIMIT = 56 * 1024 * 1024
SUBLANES = 8
LANES = 128


def _pcall(body, **kw):
    return pl.pallas_call(body, **kw)


def _params(sem, vmem=VMEM_LIMIT):
    return pltpu.CompilerParams(dimension_semantics=sem, vmem_limit_bytes=vmem)


def _pick(dim, prefs):
    for p in prefs:
        if p <= dim and dim % p == 0:
            return p
    return dim


def _sigmoid(x):
    return 1.0 / (1.0 + jnp.exp(-x))


def _wmap_col(cw, tn, r0):
    bps = cw // tn
    return lambda kb, nb: (nb // bps, r0 + kb, nb % bps)


def _wmap_row(kp, tk, r0):
    bps = kp // tk
    return lambda kb, nb: (kb // bps, r0 + kb % bps, nb)


def _mm_nn(a, w3, wmap, K, N, tm, tk, tn, name, res=None):
    M = a.shape[0]
    nk = K // tk

    def body(*refs):
        if res is None:
            a_ref, w_ref, o_ref = refs[:3]
        else:
            a_ref, w_ref, r_ref, o_ref = refs[:4]
        p = jnp.dot(a_ref[...], w_ref[...], preferred_element_type=F32)
        if nk == 1:
            o_ref[...] = p if res is None else p + r_ref[...]
            return
        acc = refs[-1]
        k = pl.program_id(2)

        @pl.when(k == 0)
        def _():
            acc[...] = p

        @pl.when(k > 0)
        def _():
            acc[...] += p

        @pl.when(k == nk - 1)
        def _():
            o_ref[...] = acc[...] if res is None else acc[...] + r_ref[...]

    if nk == 1:
        grid = (M // tm, N // tn)
        ix = lambda f: (lambda i, j: f(i, j, 0))
        sem = ("parallel", "parallel")
        scratch = []
    else:
        grid = (M // tm, N // tn, nk)
        ix = lambda f: f
        sem = ("parallel", "parallel", "arbitrary")
        scratch = [pltpu.VMEM((tm, tn), F32)]
    in_specs = [pl.BlockSpec((tm, tk), ix(lambda i, j, k: (i, k))),
                pl.BlockSpec((None, tk, tn), ix(lambda i, j, k: wmap(k, j)))]
    args = [a, w3]
    if res is not None:
        in_specs.append(pl.BlockSpec((tm, tn), ix(lambda i, j, k: (i, j))))
        args.append(res)
    return _pcall(
        body, name=name, grid=grid, in_specs=in_specs,
        out_specs=pl.BlockSpec((tm, tn), ix(lambda i, j, k: (i, j))),
        out_shape=jax.ShapeDtypeStruct((M, N), F32), scratch_shapes=scratch, compiler_params=_params(sem),
    )(*args)


def _mm_nt(dy3, w3, wmap, K, N, tm, tk, tn, name, per_step=1):
    M = dy3.shape[1]
    bps = dy3.shape[2] // tn
    u = per_step
    grid = (M // tm, K // tk, N // (tn * u))
    nn = grid[2]

    def body(*refs):
        o_ref = refs[-1]
        p = None
        for r in range(u):
            d = lax.dot_general(refs[r][...], refs[u + r][...], (((1,), (1,)), ((), ())), preferred_element_type=F32)
            p = d if p is None else p + d
        if nn == 1:
            o_ref[...] = p
            return
        n = pl.program_id(2)

        @pl.when(n == 0)
        def _():
            o_ref[...] = p

        @pl.when(n > 0)
        def _():
            o_ref[...] += p

    def dy_spec(r):
        return pl.BlockSpec((None, tm, tn), lambda i, j, n: ((n * u + r) // bps, i, (n * u + r) % bps))

    def w_spec(r):
        return pl.BlockSpec((None, tk, tn), lambda i, j, n: wmap(j, n * u + r))

    return _pcall(
        body, name=name, grid=grid,
        in_specs=[dy_spec(r) for r in range(u)] + [w_spec(r) for r in range(u)],
        out_specs=pl.BlockSpec((tm, tk), lambda i, j, n: (i, j)),
        out_shape=jax.ShapeDtypeStruct((M, K), F32),
        compiler_params=_params(("parallel", "parallel", "arbitrary")),
    )(*([dy3] * u), *([w3] * u))


def _mm_tn(x, dy3, shape4, wmap, K, N, tk, tn, name, tm=None):
    M = x.shape[0]
    tm = M if tm is None else tm
    nm = M // tm
    bps = dy3.shape[2] // tn

    def body(*refs):
        x_ref, dy_ref = refs[:2]
        p = lax.dot_general(x_ref[...], dy_ref[...], (((0,), (0,)), ((), ())), preferred_element_type=F32)
        if nm == 1:
            o_ref = refs[-1]
            o_ref[...] = p.astype(o_ref.dtype)
            return
        o_ref, acc = refs[-2:]
        m = pl.program_id(2)

        @pl.when(m == 0)
        def _():
            acc[...] = p

        @pl.when(m > 0)
        def _():
            acc[...] += p

        @pl.when(m == nm - 1)
        def _():
            o_ref[...] = acc[...].astype(o_ref.dtype)

    def omap(i, j, m):
        s, rb, cb = wmap(i, j)
        return (s, 0, rb, cb)

    return _pcall(
        body, name=name, grid=(K // tk, N // tn, nm),
        in_specs=[pl.BlockSpec((tm, tk), lambda i, j, m: (m, i)),
                  pl.BlockSpec((None, tm, tn), lambda i, j, m: (j // bps, m, j % bps))],
        out_specs=pl.BlockSpec((None, None, tk, tn), omap),
        out_shape=jax.ShapeDtypeStruct(shape4, BF16),
        scratch_shapes=[] if nm == 1 else [pltpu.VMEM((tk, tn), F32)],
        compiler_params=_params(("parallel", "parallel", "arbitrary")),
    )(x, dy3)


def _row_call(fn, rows, vecs, outs, n_acc, name, t_rows=256, sub=16, per_trip=4):
    T = rows[0][0].shape[0]
    t_rows = min(t_rows, T)
    nsub = t_rows // sub
    n_r, n_v, n_o = len(rows), len(vecs), len(outs)
    width = rows[0][2]

    def body(*refs):
        r_refs = refs[:n_r]
        v_refs = refs[n_r:n_r + n_v]
        o_refs = refs[n_r + n_v:n_r + n_v + n_o]
        a_refs = refs[n_r + n_v + n_o:]

        @pl.when(pl.program_id(0) == 0)
        def _():
            for a in a_refs:
                a[...] = jnp.zeros_like(a)

        vv = [v[...] for v in v_refs]

        def step(i, carry):
            done = []
            for u in range(per_trip):
                sl = pl.ds(pl.multiple_of((i * per_trip + u) * sub, sub), sub)
                done.append((sl,) + tuple(fn([r[sl, :] for r in r_refs], vv)))
            for sl, o_vals, a_vals in done:
                for o, val in zip(o_refs, o_vals):
                    o[sl, :] = val.astype(o.dtype)
            for a_i, a in enumerate(a_refs):
                tot = None
                for _, _, a_vals in done:
                    part = a_vals[a_i].reshape(sub // SUBLANES, SUBLANES, a_vals[a_i].shape[-1]).sum(axis=0)
                    tot = part if tot is None else tot + part
                a[...] += tot
            return carry

        lax.fori_loop(0, nsub // per_trip, step, 0)

    in_specs = [pl.BlockSpec((t_rows, w), functools.partial(lambda i, cb: (i, cb), cb=cb)) for _, cb, w in rows]
    in_specs += [pl.BlockSpec(v.shape, lambda i: (0, 0)) for v in vecs]
    out_specs = [pl.BlockSpec((t_rows, w), lambda i: (i, 0)) for w, _ in outs]
    out_specs += [pl.BlockSpec((SUBLANES, width), lambda i: (0, 0)) for _ in range(n_acc)]
    out_shape = [jax.ShapeDtypeStruct((T, w), dt) for w, dt in outs]
    out_shape += [jax.ShapeDtypeStruct((SUBLANES, width), F32) for _ in range(n_acc)]
    return _pcall(
        body, name=name, grid=(T // t_rows,), in_specs=in_specs, out_specs=out_specs, out_shape=out_shape,
        compiler_params=_params(("arbitrary",)),
    )(*[r[0] for r in rows], *vecs)


def _rms_fwd_fn(rv, vv):
    h, = rv
    w, = vv
    r = lax.rsqrt(jnp.mean(h * h, axis=-1, keepdims=True) + EPS)
    return [h * r * w], []


def _rms_bwd_fn(rv, vv):
    h, dxn, dh_in = rv
    w, = vv
    d = h.shape[-1]
    r = lax.rsqrt(jnp.mean(h * h, axis=-1, keepdims=True) + EPS)
    gy = dxn * w
    dh = r * gy - h * ((r * r * r) * (1.0 / d) * jnp.sum(gy * h, axis=-1, keepdims=True))
    return [dh_in + dh] * 2, [dxn * h * r]


def _final_fn(rv, vv):
    h, tgt = rv
    w, = vv
    d = h.shape[-1]
    r = lax.rsqrt(jnp.mean(h * h, axis=-1, keepdims=True) + EPS)
    hn = h * r
    e = hn * w - tgt
    dy = e * (1.0 / d)
    gy = dy * w
    dh = r * gy - h * ((r * r * r) * (1.0 / d) * jnp.sum(gy * h, axis=-1, keepdims=True))
    return [dh] * 2, [e * e, dy * hn]


def _onorm_fwd_fn(rv, vv):
    o, g = rv
    gain, = vv
    r = lax.rsqrt(jnp.mean(o * o, axis=-1, keepdims=True) + EPS)
    return [o * r * gain * (g * _sigmoid(g))], []


def _onorm_bwd_fn(rv, vv):
    o, g, don = rv
    gain, = vv
    d = o.shape[-1]
    r = lax.rsqrt(jnp.mean(o * o, axis=-1, keepdims=True) + EPS)
    sg = _sigmoid(g)
    sl = g * sg
    n = o * r
    dg = don * n * gain * (sg * (1.0 + g * (1.0 - sg)))
    gy = don * sl * gain
    do = r * gy - o * ((r * r * r) * (1.0 / d) * jnp.sum(gy * o, axis=-1, keepdims=True))
    return [do, dg], [don * sl * n]


HALO = SUBLANES


def _col_call(fn, cols, vecs, outs, n_acc, name, before, after, tc=LANES, chunk=128):
    T = cols[0][0].shape[0]
    chunk = min(chunk, T)
    nch = T // chunk
    ncol = outs[0][1] // tc
    n_c, n_v, n_o = len(cols), len(vecs), len(outs)
    hb = HALO if before else 0
    rw = chunk + hb + (HALO if after else 0)

    def body(*refs):
        c_refs = refs[:n_c]
        v_refs = refs[n_c:n_c + n_v]
        o_refs = refs[n_c + n_v:n_c + n_v + n_o]
        a_refs = refs[n_c + n_v + n_o:]
        vv = [v[...] for v in v_refs]
        wrow = lax.broadcasted_iota(jnp.int32, (rw, tc), 0)
        inside = (wrow >= hb) & (wrow < hb + chunk)

        def step(i, carry):
            r0 = pl.multiple_of(i * chunk, chunk)
            wins = []
            for ref in c_refs:
                parts = []
                if before:
                    pb = ref[pl.ds(pl.multiple_of(jnp.maximum(r0 - HALO, 0), HALO), HALO), :]
                    parts.append(jnp.where(i > 0, pb, 0.0))
                parts.append(ref[pl.ds(r0, chunk), :])
                if after:
                    pa = ref[pl.ds(pl.multiple_of(jnp.minimum(r0 + chunk, T - HALO), HALO), HALO), :]
                    parts.append(jnp.where(i < nch - 1, pa, 0.0))
                wins.append(jnp.concatenate(parts, axis=0) if len(parts) > 1 else parts[0])
            o_vals, a_vals = fn(wins, vv, inside)
            p = 0
            for o, (nseg, _, _) in zip(o_refs, outs):
                for s in range(nseg):
                    o[s, pl.ds(r0, chunk), :] = o_vals[p][hb:hb + chunk].astype(o.dtype)
                    p += 1
            return tuple(c + a for c, a in zip(carry, a_vals))

        taps = [v.shape[0] for v, _ in vecs][:n_acc]
        init = tuple(jnp.zeros((1, tc), F32) for k in taps for _ in range(k))
        sums = lax.fori_loop(0, nch, step, init)
        arow = lax.broadcasted_iota(jnp.int32, (SUBLANES, tc), 0)
        p = 0
        for a, k in zip(a_refs, taps):
            acc = jnp.zeros((SUBLANES, tc), F32)
            for t in range(k):
                acc = jnp.where(arow == t, sums[p], acc)
                p += 1
            a[...] = acc

    in_specs = [pl.BlockSpec((T, tc), functools.partial(lambda j, off: (0, off + j), off=off)) for _, off in cols]
    in_specs += [pl.BlockSpec((v.shape[0], tc), functools.partial(lambda j, off: (0, off + j), off=off))
                 for v, off in vecs]
    out_specs = [pl.BlockSpec((nseg, T, tc), lambda j: (0, 0, j)) for nseg, _, _ in outs]
    out_specs += [pl.BlockSpec((SUBLANES, tc), lambda j: (0, j)) for _ in range(n_acc)]
    out_shape = [jax.ShapeDtypeStruct((nseg, T, w), dt) for nseg, w, dt in outs]
    out_shape += [jax.ShapeDtypeStruct((SUBLANES, ncol * tc), F32) for _ in range(n_acc)]
    return _pcall(
        body, name=name, grid=(ncol,), in_specs=in_specs, out_specs=out_specs, out_shape=out_shape,
        compiler_params=_params(("parallel",)),
    )(*[c[0] for c in cols], *[v[0] for v in vecs])


def _down(x, k):
    return x if k == 0 else pltpu.roll(x, k, 0)


def _up(x, k):
    return x if k == 0 else pltpu.roll(x, x.shape[0] - k, 0)


def _lags(x):
    return _down(x, 2), _down(x, 1), x


def _conv(lags, w):
    return w[0:1] * lags[0] + w[1:2] * lags[1] + w[2:3] * lags[2]


def _conv_t(d, w):
    return w[2:3] * d + w[1:2] * _up(d, 1) + w[0:1] * _up(d, 2)


def _tap_sums(d, lags, inside):
    dm = jnp.where(inside, d, 0.0)
    return [jnp.sum(dm * lag, axis=0, keepdims=True) for lag in lags]


def _glu_fwd_fn(wins, vv, inside):
    xg, xv = wins
    wg, wv = vv
    ug = _conv(_lags(xg), wg)
    uv = _conv(_lags(xv), wv)
    return [ug * _sigmoid(ug) * uv], []


def _glu_bwd_fn(wins, vv, inside):
    xg, xv, da = wins
    wg, wv = vv
    lg, lv = _lags(xg), _lags(xv)
    ug = _conv(lg, wg)
    uv = _conv(lv, wv)
    sg = _sigmoid(ug)
    dug = da * uv * (sg * (1.0 + ug * (1.0 - sg)))
    duv = da * (ug * sg)
    return [_conv_t(dug, wg), _conv_t(duv, wv)], _tap_sums(dug, lg, inside) + _tap_sums(duv, lv, inside)


def _sc_fwd_fn(wins, vv, inside):
    gb, gc, hh = wins
    w, = vv
    return [gb * _conv(_lags(gc * hh), w)], []


def _sc_bwd_fn(wins, vv, inside):
    gb, gc, hh, dy = wins
    w, = vv
    lz = _lags(gc * hh)
    dcv = dy * gb
    dz = _conv_t(dcv, w)
    return [dy * _conv(lz, w), dz * hh, dz * gc], _tap_sums(dcv, lz, inside)


def _gates(qr, fr, lb):
    sg = _sigmoid(fr)
    f = lb + (1.0 - lb) * sg
    sq = _sigmoid(qr)
    q = qr * sq * (HEAD ** -0.5)
    return q, 1.0 - f, jnp.log(f), f, sg, sq


def _boundary_rows(b, g, row):
    c = b.shape[0]
    if 2 * g >= SUBLANES:
        x = b.reshape(c // (2 * g), 2 * g, LANES)
        return jnp.broadcast_to(x[:, g - 1:g, :], x.shape).reshape(c, LANES)
    x = b.reshape(c // SUBLANES, SUBLANES, LANES)
    lo = jnp.broadcast_to(x[:, 1:2, :], x.shape).reshape(c, LANES)
    hi = jnp.broadcast_to(x[:, 5:6, :], x.shape).reshape(c, LANES)
    return jnp.where((row & 4) == 0, lo, hi)


def _chunk_decays(gl, f, row):
    c = gl.shape[0]
    b = gl
    d = 1
    while d < c:
        b = b + jnp.where(row >= d, pltpu.roll(b, d, 0), 0.0)
        d *= 2
    eq, ek = [], []
    g = c // 2
    while g >= 2:
        right = (row & g) != 0
        m = _boundary_rows(b, g, row)
        z = jnp.exp(jnp.where(right, b - m, m - b))
        eq.append(jnp.where(right, z, 0.0))
        ek.append(jnp.where(right, 0.0, z))
        g //= 2
    odd = (row & 1) != 0
    eq.append(jnp.where(odd, f, 0.0))
    ek.append(jnp.where(odd, 0.0, 1.0))
    return b, eq, ek


def _intra(q, k, eq, ek, tt, ss):
    c = q.shape[0]
    qs, ks = [], []
    a = jnp.where(tt == ss, jnp.sum(q * k, axis=1, keepdims=True), 0.0)
    g = c // 2
    for e_q, e_k in zip(eq, ek):
        qg = (q * e_q).astype(BF16)
        kg = (k * e_k).astype(BF16)
        p = lax.dot_general(qg, kg, (((1,), (1,)), ((), ())), preferred_element_type=F32)
        a = a + (p if 2 * g >= c else jnp.where((tt ^ ss) < 2 * g, p, 0.0))
        qs.append(qg)
        ks.append(kg)
        g //= 2
    return a, qs, ks


def _hgrn_fwd(proj, lb, d_model):
    T = proj.shape[0]
    H = d_model // HEAD
    nch = T // CHUNK

    def body(q_ref, f_ref, v_ref, lb_ref, o_ref, s_ref):
        lbv = lb_ref[...]
        row = lax.broadcasted_iota(jnp.int32, (CHUNK, HEAD), 0)
        tt = lax.broadcasted_iota(jnp.int32, (CHUNK, CHUNK), 0)
        ss = lax.broadcasted_iota(jnp.int32, (CHUNK, CHUNK), 1)

        def step(i, st):
            sl = pl.ds(pl.multiple_of(i * CHUNK, CHUNK), CHUNK)
            q, k, gl, f, _, _ = _gates(q_ref[sl, :], f_ref[sl, :], lbv)
            v = v_ref[sl, :].astype(BF16)
            b, eq, ek = _chunk_decays(gl, f, row)
            a, _, _ = _intra(q, k, eq, ek, tt, ss)
            bl = b[CHUNK - 1:CHUNK, :]
            q0 = (q * jnp.exp(b)).astype(BF16)
            kh = (k * jnp.exp(bl - b)).astype(BF16)
            s_ref[i] = st
            o = jnp.dot(a.astype(BF16), v, preferred_element_type=F32)
            o = o + lax.dot_general(q0, st.astype(BF16), (((1,), (1,)), ((), ())), preferred_element_type=F32)
            o_ref[sl, :] = o
            return jnp.exp(bl) * st + lax.dot_general(v, kh, (((0,), (0,)), ((), ())), preferred_element_type=F32)

        per = 4 if nch % 4 == 0 else 2

        def trip(i, st):
            for u in range(per):
                st = step(per * i + u, st)
            return st

        lax.fori_loop(0, nch // per, trip, jnp.zeros((HEAD, HEAD), F32))

    col = lambda off: pl.BlockSpec((T, HEAD), functools.partial(lambda h, off: (0, off + h), off=off))
    return _pcall(
        body, name="hgrn_fwd", grid=(H,),
        in_specs=[col(0), col(H), col(2 * H), pl.BlockSpec((1, HEAD), lambda h: (0, h))],
        out_specs=[pl.BlockSpec((T, HEAD), lambda h: (0, h)),
                   pl.BlockSpec((None, nch, HEAD, HEAD), lambda h: (h, 0, 0, 0))],
        out_shape=[jax.ShapeDtypeStruct((T, d_model), F32), jax.ShapeDtypeStruct((H, nch, HEAD, HEAD), F32)],
        compiler_params=_params(("parallel",)),
    )(proj, proj, proj, lb)


def _hgrn_bwd(proj, lb, states, do, dgate, d_model):
    T = proj.shape[0]
    H = d_model // HEAD
    nch = T // CHUNK

    def body(q_ref, f_ref, v_ref, lb_ref, s_ref, do_ref, dg_ref, dp_ref, dlb_ref):
        dq_ref, df_ref, dv_ref = dp_ref.at[0], dp_ref.at[1], dp_ref.at[2]
        dp_ref[3] = dg_ref[...]
        lbv = lb_ref[...]
        row = lax.broadcasted_iota(jnp.int32, (CHUNK, HEAD), 0)
        tt = lax.broadcasted_iota(jnp.int32, (CHUNK, CHUNK), 0)
        ss = lax.broadcasted_iota(jnp.int32, (CHUNK, CHUNK), 1)
        last = row == CHUNK - 1
        nt = (((1,), (1,)), ((), ()))
        tn = (((0,), (0,)), ((), ()))

        def step(j, carry):
            dst, dlb = carry
            i = nch - 1 - j
            sl = pl.ds(pl.multiple_of(i * CHUNK, CHUNK), CHUNK)
            qr = q_ref[sl, :]
            q, k, gl, f, sg, sq = _gates(qr, f_ref[sl, :], lbv)
            v = v_ref[sl, :].astype(BF16)
            d_o = do_ref[sl, :].astype(BF16)
            st = s_ref[i]
            st16 = st.astype(BF16)
            dst16 = dst.astype(BF16)
            b, eq, ek = _chunk_decays(gl, f, row)
            a, qs, ks = _intra(q, k, eq, ek, tt, ss)
            bl = b[CHUNK - 1:CHUNK, :]
            e0 = jnp.exp(b)
            eh = jnp.exp(bl - b)
            ebl = jnp.exp(bl)
            q0 = q * e0
            kh = k * eh
            q016 = q0.astype(BF16)
            kh16 = kh.astype(BF16)
            dv = lax.dot_general(a.astype(BF16), d_o, tn, preferred_element_type=F32)
            dv = dv + lax.dot_general(kh16, dst16, nt, preferred_element_type=F32)
            dv_ref[sl, :] = dv.astype(dv_ref.dtype)
            da = lax.dot_general(d_o, v, nt, preferred_element_type=F32)
            da = jnp.where(tt >= ss, da, 0.0)
            dd = jnp.sum(jnp.where(tt == ss, da, 0.0), axis=1, keepdims=True)
            dq0 = jnp.dot(d_o, st16, preferred_element_type=F32)
            dkh = jnp.dot(v, dst16, preferred_element_type=F32)
            dq = dq0 * e0 + dd * k
            dk = dkh * eh + dd * q
            db = dq0 * q016.astype(F32) - dkh * kh16.astype(F32)
            g = CHUNK // 2
            for e_q, e_k, qg, kg in zip(eq, ek, qs, ks):
                dag = (da if 2 * g >= CHUNK else jnp.where((tt ^ ss) < 2 * g, da, 0.0)).astype(BF16)
                dqg = jnp.dot(dag, kg, preferred_element_type=F32)
                dkg = lax.dot_general(dag, qg, tn, preferred_element_type=F32)
                dq = dq + dqg * e_q
                dk = dk + dkg * e_k
                db = db + (dqg * qg.astype(F32) - dkg * kg.astype(F32))
                g //= 2
            dbl = jnp.sum(dkh * kh16.astype(F32), axis=0, keepdims=True) + ebl * jnp.sum(dst * st, axis=0, keepdims=True)
            db = db + jnp.where(last, dbl, 0.0)
            d = 1
            while d < CHUNK:
                db = db + jnp.where(row < CHUNK - d, pltpu.roll(db, CHUNK - d, 0), 0.0)
                d *= 2
            dfg = db / f - dk
            df_ref[sl, :] = (dfg * (1.0 - lbv) * sg * (1.0 - sg)).astype(df_ref.dtype)
            dq_ref[sl, :] = (dq * (HEAD ** -0.5) * (sq * (1.0 + qr * (1.0 - sq)))).astype(dq_ref.dtype)
            dlb = dlb + jnp.sum(dfg * (1.0 - sg), axis=0, keepdims=True)
            dst = ebl * dst + lax.dot_general(d_o, q016, tn, preferred_element_type=F32)
            return dst, dlb

        _, dlb = lax.fori_loop(0, nch // 2, lambda j, cr: step(2 * j + 1, step(2 * j, cr)),
                               (jnp.zeros((HEAD, HEAD), F32), jnp.zeros((1, HEAD), F32)))
        arow = lax.broadcasted_iota(jnp.int32, (SUBLANES, HEAD), 0)
        dlb_ref[...] = jnp.where(arow == 0, dlb, 0.0)

    col = lambda off: pl.BlockSpec((T, HEAD), functools.partial(lambda h, off: (0, off + h), off=off))
    return _pcall(
        body, name="hgrn_bwd", grid=(H,),
        in_specs=[col(0), col(H), col(2 * H), pl.BlockSpec((1, HEAD), lambda h: (0, h)),
                  pl.BlockSpec((None, nch, HEAD, HEAD), lambda h: (h, 0, 0, 0)), col(0), col(0)],
        out_specs=[pl.BlockSpec((4, T, HEAD), lambda h: (0, 0, h)), pl.BlockSpec((SUBLANES, HEAD), lambda h: (0, h))],
        out_shape=[jax.ShapeDtypeStruct((4, T, d_model), BF16), jax.ShapeDtypeStruct((SUBLANES, d_model), F32)],
        compiler_params=_params(("parallel",)),
    )(proj, proj, proj, lb, states, do, dgate)


def _lb_softmax(table):
    n, f = table.shape

    def body(t_ref, p_ref):
        t = t_ref[...]
        e = jnp.exp(t - jnp.max(t, axis=0, keepdims=True))
        p_ref[...] = e / jnp.sum(e, axis=0, keepdims=True)

    padded = jnp.pad(table, ((0, SUBLANES - n), (0, 0)), constant_values=-jnp.inf)
    return _pcall(body, name="lb_softmax", out_shape=jax.ShapeDtypeStruct((SUBLANES, f), F32))(padded)


def _adamw_math(w, g, m, v):
    m = ADAM_B1 * m + (1.0 - ADAM_B1) * g
    v = ADAM_B2 * v + (1.0 - ADAM_B2) * (g * g)
    m_hat = m / (1.0 - ADAM_B1 ** ADAM_STEP)
    v_hat = v / (1.0 - ADAM_B2 ** ADAM_STEP)
    delta = -ADAM_LR * (m_hat / (jnp.sqrt(v_hat) + ADAM_EPS) + ADAM_WD * w)
    return delta, m, v


def _adamw(w, g, m, v, name):
    R, C = w.shape
    tr = _pick(R, (128, 64, 32, 16, 8))

    def body(w_ref, g_ref, m_ref, v_ref, d_ref, nm_ref, nv_ref):
        d, nm, nv = _adamw_math(w_ref[...], g_ref[...], m_ref[...], v_ref[...])
        d_ref[...] = d
        nm_ref[...] = nm
        nv_ref[...] = nv

    spec = pl.BlockSpec((tr, C), lambda i: (i, 0))
    return _pcall(
        body, name=name, grid=(R // tr,), in_specs=[spec] * 4, out_specs=[spec] * 3,
        out_shape=[jax.ShapeDtypeStruct((R, C), F32)] * 3, compiler_params=_params(("parallel",)),
    )(w, g, m, v)


def _adamw_halves(w, m, v, g_mine, g_recv, c, name, layer=0, prev=None):
    C = w.shape[1]
    rh = g_mine.shape[0]
    tr = _pick(rh, (128, 64, 32, 16, 8))
    nb = rh // tr
    r0 = layer * 2 * nb

    def body(c_ref, w_ref, m_ref, v_ref, gm_ref, gr_ref, *rest):
        g_ref, d_ref, nm_ref, nv_ref = rest[-4:]
        g = jnp.where(pl.program_id(0) == c_ref[0], gm_ref[...], gr_ref[...])
        d, nm, nv = _adamw_math(w_ref[...], g, m_ref[...], v_ref[...])
        g_ref[...] = g
        d_ref[...] = d
        nm_ref[...] = nm
        nv_ref[...] = nv

    full = pl.BlockSpec((tr, C), lambda h, i, cr: (r0 + h * nb + i, 0))
    half = pl.BlockSpec((tr, C), lambda h, i, cr: (i, 0))
    in_specs = [full, full, full, half, half]
    args = [c, w, m, v, g_mine, g_recv]
    alias = {}
    if prev is not None:
        in_specs += [pl.BlockSpec(memory_space=pl.ANY)] * 4
        args += list(prev)
        alias = {6 + k: k for k in range(4)}
    return _pcall(
        body, name=name,
        grid_spec=pltpu.PrefetchScalarGridSpec(
            num_scalar_prefetch=1, grid=(2, nb), in_specs=in_specs, out_specs=[full] * 4),
        out_shape=[jax.ShapeDtypeStruct(w.shape, F32)] * 4, input_output_aliases=alias,
        compiler_params=_params(("parallel", "parallel")),
    )(*args)


def _lb_table_grad(p8, dlb, n):
    f = p8.shape[1]

    def body(p_ref, d_ref, o_ref):
        p = p_ref[...]
        d = d_ref[...]
        p0 = p[0:1, :]
        first = lax.broadcasted_iota(jnp.int32, p.shape, 0) == 0
        o_ref[...] = p * (jnp.where(first, d, 0.0) - d * p0)

    return _pcall(body, name="lb_table_grad", out_shape=jax.ShapeDtypeStruct((SUBLANES, f), F32))(p8, dlb)[:n]


def _place():
    x, y, c = lax.axis_index("x"), lax.axis_index("y"), lax.axis_index("c")
    chips = [(1 - x, y), (x, 1 - y), (1 - x, 1 - y)]
    return x, y, c, chips


HBM_SPEC = pl.BlockSpec(memory_space=pltpu.HBM)


def _gather_weights(big, small):
    nb, ns = len(big), len(small)
    n = nb + ns

    def body(*refs):
        ins, outs = refs[:n], refs[n:2 * n]
        send_sems, recv_sems, own_send, own_recv = refs[2 * n:]
        x, y, c, chips = _place()
        me = 2 * x + y
        sib = (x, y, 1 - c)
        own = [pltpu.make_async_remote_copy(
            src_ref=ins[t], dst_ref=outs[t].at[me], send_sem=own_send.at[t], recv_sem=own_recv.at[t],
            device_id=sib, device_id_type=MESH) for t in range(n)]
        for cp in own:
            cp.start()

        def half(t, h):
            rh = big[t].shape[0] // 2
            return pl.ds(pl.multiple_of(h * rh, rh), rh)

        sends = []
        for t in range(n):
            for j, chip in enumerate(chips):
                k = 6 * t + j
                if t < nb:
                    src, dst = ins[t].at[half(t, c)], outs[t].at[me, half(t, c)]
                else:
                    src, dst = ins[t], outs[t].at[me]
                sends.append(pltpu.make_async_remote_copy(
                    src_ref=src, dst_ref=dst, send_sem=send_sems.at[k], recv_sem=recv_sems.at[k],
                    device_id=(*chip, c), device_id_type=MESH))
        for cp in sends:
            cp.start()
        passed = []
        for t in range(n):
            for j, (cx, cy) in enumerate(chips):
                k = 6 * t + j
                s = 2 * cx + cy
                if t < nb:
                    landed = outs[t].at[s, half(t, c)]
                    pltpu.make_async_remote_copy(
                        src_ref=landed, dst_ref=landed, send_sem=send_sems.at[k], recv_sem=recv_sems.at[k],
                        device_id=sib, device_id_type=MESH).wait_recv()
                    fwd = pltpu.make_async_remote_copy(
                        src_ref=landed, dst_ref=landed, send_sem=send_sems.at[k + 3], recv_sem=recv_sems.at[k + 3],
                        device_id=sib, device_id_type=MESH)
                    fwd.start()
                    passed.append(fwd)
                else:
                    landed = outs[t].at[s]
                    pltpu.make_async_remote_copy(
                        src_ref=landed, dst_ref=landed, send_sem=send_sems.at[k], recv_sem=recv_sems.at[k],
                        device_id=sib, device_id_type=MESH).wait_recv()
        for t in range(nb):
            for j, (cx, cy) in enumerate(chips):
                k = 6 * t + j
                other = outs[t].at[2 * cx + cy, half(t, 1 - c)]
                pltpu.make_async_remote_copy(
                    src_ref=other, dst_ref=other, send_sem=send_sems.at[k + 3], recv_sem=recv_sems.at[k + 3],
                    device_id=sib, device_id_type=MESH).wait_recv()
        for cp in sends + passed:
            cp.wait_send()
        for cp in own:
            cp.wait()

    arrs = list(big) + list(small)
    return _pcall(
        body, name="gather_weights", in_specs=[HBM_SPEC] * n, out_specs=[HBM_SPEC] * n,
        out_shape=[jax.ShapeDtypeStruct((N_CHIPS,) + a.shape, a.dtype) for a in arrs],
        scratch_shapes=[pltpu.SemaphoreType.DMA((6 * n,)), pltpu.SemaphoreType.DMA((6 * n,)),
                        pltpu.SemaphoreType.DMA((n,)), pltpu.SemaphoreType.DMA((n,))],
    )(*arrs)


SEM_SPEC = pl.BlockSpec(memory_space=pltpu.SEMAPHORE)
DATAFLOW = pltpu.SideEffectType.DATAFLOW_SIDE_EFFECTING
COPIES_PER_SHARD = 4


def _shard_copies(ins, lands, send_sems, recv_sems):
    x, y, c, chips = _place()
    me = 2 * x + y
    cps = []
    for t in range(len(ins)):
        rh = ins[t].shape[0] // 2
        half = pl.ds(pl.multiple_of(c * rh, rh), rh)
        for j, chip in enumerate(chips):
            k = COPIES_PER_SHARD * t + j
            cps.append(pltpu.make_async_remote_copy(
                src_ref=ins[t].at[half], dst_ref=lands[t].at[me, half], send_sem=send_sems.at[k],
                recv_sem=recv_sems.at[k], device_id=(*chip, c), device_id_type=MESH))
        k = COPIES_PER_SHARD * t + 3
        cps.append(pltpu.make_async_remote_copy(
            src_ref=ins[t], dst_ref=lands[t].at[me], send_sem=send_sems.at[k], recv_sem=recv_sems.at[k],
            device_id=(x, y, 1 - c), device_id_type=MESH))
    return cps


def _gather_start(shards, thru, name):
    n = len(shards)
    nops = 2 * n + len(thru)

    def body(*refs):
        ins, lands = refs[:n], refs[n:2 * n]
        send_sems, recv_sems = refs[nops], refs[nops + 1]
        for cp in _shard_copies(ins, lands, send_sems, recv_sems):
            cp.start()

    lands = [pltpu.with_memory_space_constraint(lax.empty((N_CHIPS,) + s.shape, s.dtype), pltpu.HBM) for s in shards]
    ops = [pltpu.with_memory_space_constraint(s, pltpu.HBM) for s in shards] + lands + list(thru)
    nsem = COPIES_PER_SHARD * n
    res = _pcall(
        body, name=name, in_specs=[HBM_SPEC] * nops,
        out_specs=[SEM_SPEC, SEM_SPEC] + [HBM_SPEC] * nops,
        out_shape=[pltpu.SemaphoreType.DMA((nsem,)), pltpu.SemaphoreType.DMA((nsem,))]
        + [pltpu.HBM(o.shape, o.dtype) for o in ops],
        input_output_aliases={i: 2 + i for i in range(nops)},
        compiler_params=pltpu.CompilerParams(has_side_effects=DATAFLOW),
    )(*ops)
    return res[0], res[1], res[2:2 + n], res[2 + n:2 + 2 * n], list(res[2 + 2 * n:])


def _gather_wait(send_sems, recv_sems, shards, lands, after, name):
    n = len(shards)

    def body(*refs):
        ins, lnd = refs[:n], refs[n:2 * n]
        ssem, rsem = refs[2 * n], refs[2 * n + 1]
        for cp in _shard_copies(ins, lnd, ssem, rsem):
            cp.wait_send()
            cp.wait_recv()

    res = _pcall(
        body, name=name,
        in_specs=[HBM_SPEC] * (2 * n) + [SEM_SPEC, SEM_SPEC, pl.BlockSpec(memory_space=pl.ANY)],
        out_specs=[HBM_SPEC] * (2 * n),
        out_shape=[pltpu.HBM(o.shape, o.dtype) for o in list(shards) + list(lands)],
        input_output_aliases={i: i for i in range(2 * n)},
        compiler_params=pltpu.CompilerParams(has_side_effects=DATAFLOW),
    )(*shards, *lands, send_sems, recv_sems, after)
    return res[n:]


def _forward_copies(land, send_sems, recv_sems):
    x, y, c, chips = _place()
    rh = land.shape[1] // 2
    return [pltpu.make_async_remote_copy(
        src_ref=land.at[2 * cx + cy, pl.ds(pl.multiple_of(c * rh, rh), rh)],
        dst_ref=land.at[2 * cx + cy, pl.ds(pl.multiple_of(c * rh, rh), rh)],
        send_sem=send_sems.at[j], recv_sem=recv_sems.at[j], device_id=(x, y, 1 - c), device_id_type=MESH)
        for j, (cx, cy) in enumerate(chips)]


def _forward_start(land, name):
    def body(land_ref, send_sems, recv_sems, out_ref):
        for cp in _forward_copies(land_ref, send_sems, recv_sems):
            cp.start()

    return _pcall(
        body, name=name, in_specs=[HBM_SPEC], out_specs=[SEM_SPEC, SEM_SPEC, HBM_SPEC],
        out_shape=[pltpu.SemaphoreType.DMA((3,)), pltpu.SemaphoreType.DMA((3,)), pltpu.HBM(land.shape, land.dtype)],
        input_output_aliases={0: 2}, compiler_params=pltpu.CompilerParams(has_side_effects=DATAFLOW),
    )(land)


def _forward_wait(send_sems, recv_sems, land, after, name):
    def body(land_ref, ssem, rsem, after_ref, out_ref):
        for cp in _forward_copies(land_ref, ssem, rsem):
            cp.wait_send()
            cp.wait_recv()

    return _pcall(
        body, name=name, in_specs=[HBM_SPEC, SEM_SPEC, SEM_SPEC, pl.BlockSpec(memory_space=pl.ANY)],
        out_specs=HBM_SPEC, out_shape=pltpu.HBM(land.shape, land.dtype), input_output_aliases={0: 0},
        compiler_params=pltpu.CompilerParams(has_side_effects=DATAFLOW),
    )(land, send_sems, recv_sems, after)


def _sibling_copies(ins, lands, send_sems, recv_sems, other_half):
    x, y, c, _ = _place()
    return [pltpu.make_async_remote_copy(
        src_ref=ins[t].at[:, 1 - c] if other_half else ins[t], dst_ref=lands[t], send_sem=send_sems.at[t],
        recv_sem=recv_sems.at[t], device_id=(x, y, 1 - c), device_id_type=MESH) for t in range(len(ins))]


def _sibling_start(srcs, other_half, thru, name):
    n = len(srcs)
    nthru = 0 if thru is None else 1

    def body(*refs):
        ins, lands = refs[:n], refs[n:2 * n]
        send_sems, recv_sems = refs[2 * n + nthru], refs[2 * n + nthru + 1]
        for cp in _sibling_copies(ins, lands, send_sems, recv_sems, other_half):
            cp.start()

    shapes = [(s.shape[0],) + s.shape[2:] if other_half else s.shape for s in srcs]
    lands = [pltpu.with_memory_space_constraint(lax.empty(sh, s.dtype), pltpu.HBM) for sh, s in zip(shapes, srcs)]
    ops = [pltpu.with_memory_space_constraint(s, pltpu.HBM) for s in srcs] + lands + ([] if thru is None else [thru])
    res = _pcall(
        body, name=name, in_specs=[HBM_SPEC] * len(ops),
        out_specs=[SEM_SPEC, SEM_SPEC] + [HBM_SPEC] * len(ops),
        out_shape=[pltpu.SemaphoreType.DMA((n,)), pltpu.SemaphoreType.DMA((n,))]
        + [pltpu.HBM(o.shape, o.dtype) for o in ops],
        input_output_aliases={i: 2 + i for i in range(len(ops))},
        compiler_params=pltpu.CompilerParams(has_side_effects=DATAFLOW),
    )(*ops)
    return res[0], res[1], res[2:2 + n], res[2 + n:2 + 2 * n], (None if thru is None else res[2 + 2 * n])


def _sibling_wait(send_sems, recv_sems, srcs, lands, other_half, after, name):
    n = len(srcs)

    def body(*refs):
        ins, lnd = refs[:n], refs[n:2 * n]
        ssem, rsem = refs[2 * n], refs[2 * n + 1]
        for cp in _sibling_copies(ins, lnd, ssem, rsem, other_half):
            cp.wait_send()
            cp.wait_recv()

    res = _pcall(
        body, name=name,
        in_specs=[HBM_SPEC] * (2 * n) + [SEM_SPEC, SEM_SPEC, pl.BlockSpec(memory_space=pl.ANY)],
        out_specs=[HBM_SPEC] * (2 * n),
        out_shape=[pltpu.HBM(o.shape, o.dtype) for o in list(srcs) + list(lands)],
        input_output_aliases={i: i for i in range(2 * n)},
        compiler_params=pltpu.CompilerParams(has_side_effects=DATAFLOW),
    )(*srcs, *lands, send_sems, recv_sems, after)
    return res[:n], res[n:]


def _chip_copies(ins, lands, send_sems, recv_sems):
    x, y, c, chips = _place()
    cps = []
    for t in range(len(ins)):
        for j, (cx, cy) in enumerate(chips):
            cps.append(pltpu.make_async_remote_copy(
                src_ref=ins[t].at[2 * cx + cy], dst_ref=lands[t].at[j],
                send_sem=send_sems.at[3 * t + j], recv_sem=recv_sems.at[3 * t + j],
                device_id=(cx, cy, c), device_id_type=MESH))
    return cps


def _chip_start(parts, thru, name):
    n = len(parts)

    def body(*refs):
        ins, lands = refs[:n], refs[n:2 * n]
        send_sems, recv_sems = refs[2 * n + 1], refs[2 * n + 2]
        for cp in _chip_copies(ins, lands, send_sems, recv_sems):
            cp.start()

    lands = [pltpu.with_memory_space_constraint(lax.empty((3,) + p.shape[1:], p.dtype), pltpu.HBM) for p in parts]
    ops = [pltpu.with_memory_space_constraint(p, pltpu.HBM) for p in parts] + lands + [thru]
    res = _pcall(
        body, name=name, in_specs=[HBM_SPEC] * (2 * n + 1),
        out_specs=[SEM_SPEC, SEM_SPEC] + [HBM_SPEC] * (2 * n + 1),
        out_shape=[pltpu.SemaphoreType.DMA((3 * n,)), pltpu.SemaphoreType.DMA((3 * n,))]
        + [pltpu.HBM(o.shape, o.dtype) for o in ops],
        input_output_aliases={i: 2 + i for i in range(2 * n + 1)},
        compiler_params=pltpu.CompilerParams(has_side_effects=DATAFLOW),
    )(*ops)
    return res[0], res[1], res[2:2 + n], res[2 + n:2 + 2 * n], res[2 + 2 * n]


def _chip_wait(send_sems, recv_sems, parts, lands, after, name):
    n = len(parts)

    def body(*refs):
        ins, lnd = refs[:n], refs[n:2 * n]
        ssem, rsem = refs[2 * n], refs[2 * n + 1]
        for cp in _chip_copies(ins, lnd, ssem, rsem):
            cp.wait_send()
            cp.wait_recv()

    res = _pcall(
        body, name=name,
        in_specs=[HBM_SPEC] * (2 * n) + [SEM_SPEC, SEM_SPEC, pl.BlockSpec(memory_space=pl.ANY)],
        out_specs=[HBM_SPEC] * (2 * n),
        out_shape=[pltpu.HBM(o.shape, o.dtype) for o in list(parts) + list(lands)],
        input_output_aliases={i: i for i in range(2 * n)},
        compiler_params=pltpu.CompilerParams(has_side_effects=DATAFLOW),
    )(*parts, *lands, send_sems, recv_sems, after)
    return res[:n], res[n:]


def _add_pair(grad, recv, c, name):
    s, _, rh, cc = grad.shape
    tr = _pick(rh, (256, 128, 64, 32, 16))

    def body(c_ref, g_ref, r_ref, o_ref):
        o_ref[...] = (g_ref[...].astype(F32) + r_ref[...].astype(F32)).astype(o_ref.dtype)

    return _pcall(
        body, name=name,
        grid_spec=pltpu.PrefetchScalarGridSpec(
            num_scalar_prefetch=1, grid=(s, rh // tr),
            in_specs=[pl.BlockSpec((None, None, tr, cc), lambda a, i, cr: (a, cr[0], i, 0)),
                      pl.BlockSpec((None, tr, cc), lambda a, i, cr: (a, i, 0))],
            out_specs=pl.BlockSpec((None, tr, cc), lambda a, i, cr: (a, i, 0))),
        out_shape=jax.ShapeDtypeStruct((s, rh, cc), BF16),
        compiler_params=_params(("parallel", "parallel")),
    )(c, grad, recv)


def _add_chips(part, recv, me, name):
    _, rh, cc = part.shape
    tr = _pick(rh, (256, 128, 64, 32, 16))

    def body(m_ref, p_ref, r_ref, o_ref):
        o_ref[...] = ((p_ref[...].astype(F32) + r_ref[0].astype(F32)) + r_ref[1].astype(F32)) + r_ref[2].astype(F32)

    return _pcall(
        body, name=name,
        grid_spec=pltpu.PrefetchScalarGridSpec(
            num_scalar_prefetch=1, grid=(rh // tr,),
            in_specs=[pl.BlockSpec((None, tr, cc), lambda i, mr: (mr[0], i, 0)),
                      pl.BlockSpec((3, tr, cc), lambda i, mr: (0, i, 0))],
            out_specs=pl.BlockSpec((tr, cc), lambda i, mr: (i, 0))),
        out_shape=jax.ShapeDtypeStruct((rh, cc), F32),
        compiler_params=_params(("parallel",)),
    )(me, part, recv)


def _all_sum(vec):
    rows = vec.shape[0]

    def body(v_ref, o_ref, buf, send_sems, recv_sems):
        x, y, c, _ = _place()
        me = 4 * x + 2 * y + c
        buf[me] = v_ref[...]
        cps = []
        for r in range(1, 8):
            fx, fy, fc = (r >> 2) & 1, (r >> 1) & 1, r & 1
            peer = (x ^ fx, y ^ fy, c ^ fc)
            cps.append(pltpu.make_async_remote_copy(
                src_ref=v_ref, dst_ref=buf.at[me], send_sem=send_sems.at[r - 1], recv_sem=recv_sems.at[r - 1],
                device_id=peer, device_id_type=MESH))
        for cp in cps:
            cp.start()
        for r in range(1, 8):
            src = me ^ r
            pltpu.make_async_remote_copy(
                src_ref=v_ref, dst_ref=buf.at[src], send_sem=send_sems.at[r - 1], recv_sem=recv_sems.at[r - 1],
                device_id=(x, y, c), device_id_type=MESH).wait_recv()
        for cp in cps:
            cp.wait_send()
        acc = buf[0]
        for d in range(1, 8):
            acc = acc + buf[d]
        o_ref[...] = acc

    return _pcall(
        body, name="all_sum_small",
        in_specs=[pl.BlockSpec(memory_space=pltpu.VMEM)], out_specs=pl.BlockSpec(memory_space=pltpu.VMEM),
        out_shape=jax.ShapeDtypeStruct((rows, LANES), F32),
        scratch_shapes=[pltpu.VMEM((8, rows, LANES), F32), pltpu.SemaphoreType.DMA((7,)), pltpu.SemaphoreType.DMA((7,))],
    )(vec)


def _pack(parts):
    flat = jnp.concatenate([p.reshape(-1) for p in parts])
    tile = SUBLANES * LANES
    pad = (-flat.shape[0]) % tile
    return jnp.pad(flat, (0, pad)).reshape(-1, LANES)


def _unpack(vec, shapes):
    flat = vec.reshape(-1)
    out, p = [], 0
    for s in shapes:
        n = 1
        for d in s:
            n *= d
        out.append(flat[p:p + n].reshape(s))
        p += n
    return out


def _local_step(x, tgt, norm_mix, norm_ffn, lb8, out_norm, final_norm, sc_conv, ffn_conv, weights, reduce_start,
                reduce_finish):
    T, D = x.shape
    F2 = ffn_conv.shape[-1]
    FF = F2 // 2
    tm = _pick(T, (1024, 512, 256, 128))
    wide = (1536, 1408, 1024, 768, 512, 384, 256, 128)
    cw_h, cw_s, cw_u = 4 * D // N_CHIPS, 3 * D // N_CHIPS, F2 // N_CHIPS
    kp = FF // N_CHIPS
    tk_ff = kp if kp % LANES == 0 else LANES
    tn_d = _pick(D, (1024, 512, 256, 128))
    tk_w = _pick(D, (512, 256, 128))
    tn_h = _pick(cw_h, (1024, 512, 256, 128))
    tn_s = _pick(D // N_CHIPS, (512, 256, 128))
    tn_u = _pick(cw_u, wide)
    lb = lb8[0:1]
    wm_sq = _wmap_col(D, tn_d, 0)
    wm_sq1 = _wmap_col(D, D, 0)
    seg1 = lambda a: a.reshape((1,) + a.shape)

    def mix_in(h, w):
        return _row_call(_rms_fwd_fn, [(h, 0, D)], [w], [(D, BF16)], 0, "rms_fwd")[0]

    def rms_bwd(h, dxn, dh, w):
        return _row_call(_rms_bwd_fn, [(h, 0, D), (dxn, 0, D), (dh, 0, D)], [w], [(D, F32), (D, BF16)], 1, "rms_bwd")

    def ffn_fwd(h, i, fetch_up, fetch_down, next_stage=None):
        xn = mix_in(h, norm_ffn[i:i + 1])
        tn = _pick(cw_u, wide)
        w_up = fetch_up(xn)
        up = _mm_nn(xn, w_up, _wmap_col(cw_u, tn, 0), D, F2, tm, D, tn, "ffn_up")
        nb = FF // LANES
        a = _col_call(_glu_fwd_fn, [(up, 0), (up, nb)], [(ffn_conv[i], 0), (ffn_conv[i], nb)], [(1, FF, BF16)], 0,
                      "glu_fwd", before=True, after=False)[0][0]
        later = None if next_stage is None else weights(next_stage, a)
        w_down = fetch_down(a)
        h2 = _mm_nn(a, w_down, _wmap_row(kp, tk_ff, 0), FF, D, tm, tk_ff, tn_d, "ffn_down", res=h)
        return h2, (xn, up, a), w_up, w_down, later

    def ffn_bwd(dh, dh16, h, saved, i, w_up, w_down):
        xn, up, a = saved
        g_down = _mm_tn(a, seg1(dh16), (N_CHIPS, 1, kp, D), _wmap_row(kp, tk_ff, 0), FF, D, tk_ff, tn_d,
                        "ffn_down_dw", tm=_pick(T, (2048, 1024, 512, 256, 128)))
        dh16 = reduce_start(("ffn_w_down", i), g_down, dh16)
        da = _mm_nt(seg1(dh16), w_down, _wmap_row(kp, tk_ff, 0), FF, D, tm, tk_ff, D, "ffn_down_dx")
        nb = FF // LANES
        dgv, cg, cv = _col_call(_glu_bwd_fn, [(up, 0), (up, nb), (da, 0)], [(ffn_conv[i], 0), (ffn_conv[i], nb)],
                                [(2, FF, BF16)], 2, "glu_bwd", before=True, after=True)
        g_up = _mm_tn(xn, dgv, (N_CHIPS, 1, D, cw_u), _wmap_col(cw_u, tn_u, 0), D, F2, tk_w, tn_u, "ffn_up_dw")
        dgv = reduce_start(("ffn_w_up", i), g_up, dgv)
        dxn = _mm_nt(dgv, w_up, _wmap_col(cw_u, tn_u, 0), D, F2, tm, D, tn_u, "ffn_up_dx")
        dh2, dh2_16, dnw = rms_bwd(h, dxn, dh, norm_ffn[i:i + 1])
        return dh2, reduce_finish(dh2_16), dnw, jnp.concatenate([cg[:3], cv[:3]], axis=1)

    h0 = x
    xn0 = mix_in(h0, norm_mix[0:1])
    fetch_hin, = weights(0, xn0)
    w_hin = fetch_hin(xn0)
    proj = _mm_nn(xn0, w_hin, _wmap_col(cw_h, tn_h, 0), D, 4 * D, tm, D, tn_h, "hgrn_in")
    o, states = _hgrn_fwd(proj, lb, D)
    fetch_hout, fetch_up0, fetch_down0 = weights(1, o)
    on = _row_call(_onorm_fwd_fn, [(o, 0, D), (proj, 3, D)], [out_norm], [(D, BF16)], 0, "onorm_fwd")[0]
    w_hout1 = fetch_hout(on).reshape(1, D, D)
    h1 = _mm_nn(on, w_hout1, wm_sq, D, D, tm, D, tn_d, "hgrn_out", res=h0)
    h2, ffn0, w_up0, w_down0, (fetch_sin, fetch_sout) = ffn_fwd(h1, 0, fetch_up0, fetch_down0, next_stage=2)
    xn1 = mix_in(h2, norm_mix[1:2])
    w_sin = fetch_sin(xn1)
    tn_si = _pick(cw_s, wide)
    sproj = _mm_nn(xn1, w_sin, _wmap_col(cw_s, tn_si, 0), D, 3 * D, tm, D, tn_si, "sc_in")
    fetch_up1, fetch_down1 = weights(3, sproj)
    nd = D // LANES
    ysc = _col_call(_sc_fwd_fn, [(sproj, 0), (sproj, nd), (sproj, 2 * nd)], [(sc_conv, 0)], [(1, D, BF16)], 0,
                    "sc_fwd", before=True, after=False)[0][0]
    w_sout1 = fetch_sout(ysc).reshape(1, D, D)
    h3 = _mm_nn(ysc, w_sout1, wm_sq, D, D, tm, D, tn_d, "sc_out", res=h2)
    h4, ffn1, w_up1, w_down1, _ = ffn_fwd(h3, 1, fetch_up1, fetch_down1)

    dh, dh16, esq, dfinal = _row_call(_final_fn, [(h4, 0, D), (tgt, 0, D)], [final_norm], [(D, F32), (D, BF16)], 2,
                                      "final_loss")
    loss = 0.5 / D * jnp.sum(esq)
    dh, dh16, dnf1, dconv1 = ffn_bwd(dh, dh16, h3, ffn1, 1, w_up1, w_down1)
    g_sout = _mm_tn(ysc, seg1(dh16), (1, 1, D, D), wm_sq, D, D, tk_w, tn_d, "sc_out_dw")
    dh16 = reduce_start(("sc_w_out", 0), g_sout, dh16)
    dy = _mm_nt(seg1(dh16), w_sout1, wm_sq1, D, D, tm, D, D, "sc_out_dx")
    dsp, dscc = _col_call(_sc_bwd_fn, [(sproj, 0), (sproj, nd), (sproj, 2 * nd), (dy, 0)], [(sc_conv, 0)],
                          [(3, D, BF16)], 1, "sc_bwd", before=True, after=True)
    g_sin = _mm_tn(xn1, dsp, (N_CHIPS, 1, D, cw_s), _wmap_col(cw_s, tn_s, 0), D, 3 * D, tk_w, tn_s, "sc_in_dw")
    dsp = reduce_start(("sc_w_in", 0), g_sin, dsp)
    dxn = _mm_nt(dsp, w_sin, _wmap_col(cw_s, tn_s, 0), D, 3 * D, tm, D, tn_s, "sc_in_dx", per_step=3)
    dh, dh16, dnm1 = rms_bwd(h2, dxn, dh, norm_mix[1:2])
    dh16 = reduce_finish(dh16)
    dh, dh16, dnf0, dconv0 = ffn_bwd(dh, dh16, h1, ffn0, 0, w_up0, w_down0)
    g_hout = _mm_tn(on, seg1(dh16), (1, 1, D, D), wm_sq, D, D, tk_w, tn_d, "hgrn_out_dw")
    dh16 = reduce_start(("hgrn_w_out", 0), g_hout, dh16)
    don = _mm_nt(seg1(dh16), w_hout1, wm_sq1, D, D, tm, D, D, "hgrn_out_dx")
    do, dgate, dgain = _row_call(_onorm_bwd_fn, [(o, 0, D), (proj, 3, D), (don, 0, D)], [out_norm],
                                 [(D, F32), (D, BF16)], 1, "onorm_bwd")
    dproj, dlb = _hgrn_bwd(proj, lb, states, do, dgate, D)
    g_hin = _mm_tn(xn0, dproj, (N_CHIPS, 1, D, cw_h), _wmap_col(cw_h, tn_h, 0), D, 4 * D, tk_w, tn_h, "hgrn_in_dw")
    dproj = reduce_start(("hgrn_w_in", 0), g_hin, dproj)
    dxn = _mm_nt(dproj, w_hin, _wmap_col(cw_h, tn_h, 0), D, 4 * D, tm, D, tn_h, "hgrn_in_dx", per_step=2)
    grad_x, _, dnm0 = rms_bwd(h0, dxn, dh, norm_mix[0:1])
    grad_x = reduce_finish(grad_x)

    small = dict(
        loss=loss,
        norm_mix=jnp.stack([jnp.sum(dnm0, axis=0), jnp.sum(dnm1, axis=0)]),
        norm_ffn=jnp.stack([jnp.sum(dnf0, axis=0), jnp.sum(dnf1, axis=0)]),
        lb=dlb[0:1],
        out_norm=jnp.sum(dgain, axis=0)[None],
        final_norm=jnp.sum(dfinal, axis=0),
        sc_conv=dscc[:3],
        ffn_conv=jnp.stack([dconv0, dconv1]),
    )
    return grad_x, small


def kernel(x, norm_mix, norm_ffn, hgrn_w_in, hgrn_lb_table, hgrn_out_norm, hgrn_w_out, sc_w_in, sc_conv, sc_w_out, ffn_w_up, ffn_conv, ffn_w_down, final_norm, loss_target, m_norm_mix, m_norm_ffn, m_hgrn_w_in, m_hgrn_lb_table, m_hgrn_out_norm, m_hgrn_w_out, m_sc_w_in, m_sc_conv, m_sc_w_out, m_ffn_w_up, m_ffn_conv, m_ffn_w_down, m_final_norm, v_norm_mix, v_norm_ffn, v_hgrn_w_in, v_hgrn_lb_table, v_hgrn_out_norm, v_hgrn_w_out, v_sc_w_in, v_sc_conv, v_sc_w_out, v_ffn_w_up, v_ffn_conv, v_ffn_w_down, v_final_norm):
    D = x.shape[-1]
    xi, yi, ci = lax.axis_index("x"), lax.axis_index("y"), lax.axis_index("c")
    me_chip = (2 * xi + yi).astype(jnp.int32).reshape(1)
    me_core = ci.astype(jnp.int32).reshape(1)

    big_names = ["hgrn_w_in", "hgrn_w_out", "sc_w_in", "sc_w_out", "ffn_w_up", "ffn_w_down"]
    big_w = dict(hgrn_w_in=hgrn_w_in, hgrn_w_out=hgrn_w_out, sc_w_in=sc_w_in, sc_w_out=sc_w_out,
                 ffn_w_up=ffn_w_up, ffn_w_down=ffn_w_down)
    big_m = dict(hgrn_w_in=m_hgrn_w_in, hgrn_w_out=m_hgrn_w_out, sc_w_in=m_sc_w_in, sc_w_out=m_sc_w_out,
                 ffn_w_up=m_ffn_w_up, ffn_w_down=m_ffn_w_down)
    big_v = dict(hgrn_w_in=v_hgrn_w_in, hgrn_w_out=v_hgrn_w_out, sc_w_in=v_sc_w_in, sc_w_out=v_sc_w_out,
                 ffn_w_up=v_ffn_w_up, ffn_w_down=v_ffn_w_down)
    flat2 = lambda a: a.reshape(-1, a.shape[-1])

    sh = lambda a: a.reshape(-1, a.shape[-1]).astype(BF16)
    conv_shards = [flat2(sc_conv), flat2(ffn_conv)]
    stages = [[sh(hgrn_w_in)], [sh(hgrn_w_out), sh(ffn_w_up[0]), sh(ffn_w_down[0])],
              [sh(sc_w_in), sh(sc_w_out)], [sh(ffn_w_up[1]), sh(ffn_w_down[1])]]
    scc4, fcc4 = _gather_weights([], [flat2(sc_conv), flat2(ffn_conv)])
    gathers = []
    thru = [scc4, norm_mix]
    for k, shards in enumerate(stages):
        ss, rs, src, land, thru = _gather_start(shards, thru, "gather_start_%d" % k)
        gathers.append((ss, rs, src, land))
    scc4, norm_mix = thru
    scc = jnp.moveaxis(scc4, 0, 1).reshape(3, D)
    f2 = ffn_conv.shape[-1] * N_CHIPS
    fcc = jnp.moveaxis(fcc4.reshape(N_CHIPS, 2, 3, -1), 0, 2).reshape(2, 3, f2)

    def weights(stage, after):
        lands = _gather_wait(*gathers[stage], after, "gather_wait_%d" % stage)
        fetch = []
        for t, land in enumerate(lands):
            ss, rs, land = _forward_start(land, "gather_forward_start_%d_%d" % (stage, t))
            fetch.append(functools.partial(_forward_wait, ss, rs, land, name="gather_forward_wait_%d_%d" % (stage, t)))
        return fetch

    pending = []
    started = []

    def reduce_start(slot, grad, thru):
        t = sum(len(b[0]) for b in pending) + len(started)
        halves = grad.reshape(N_CHIPS, 2, -1, grad.shape[-1])
        ss, rs, src, land, thru = _sibling_start([halves], True, thru, "grad_pair_start_%d" % t)
        started.append((slot, t, ss, rs, src, land))
        return thru

    def reduce_finish(thru):
        k = len(pending)
        pair = []
        for slot, t, ss, rs, src, land in started:
            src, recv = _sibling_wait(ss, rs, src, land, True, thru, "grad_pair_wait_%d" % t)
            pair.append(_add_pair(src[0], recv[0], me_core, "grad_add_pair"))
        ss, rs, pair, land, thru = _chip_start(pair, thru, "grad_chip_start_%d" % k)
        pending.append(([s[0] for s in started], ss, rs, pair, land))
        started.clear()
        return thru

    lb8 = _lb_softmax(hgrn_lb_table)
    grad_x, small = _local_step(
        x[0], loss_target[0], norm_mix, norm_ffn, lb8, hgrn_out_norm, final_norm[None], scc, fcc, weights,
        reduce_start, reduce_finish)

    small_names = ["loss", "norm_mix", "norm_ffn", "lb", "out_norm", "final_norm", "sc_conv", "ffn_conv"]
    parts = [small[n].astype(F32) for n in small_names]
    shapes = [p.shape for p in parts]
    tot = dict(zip(small_names, _unpack(_all_sum(_pack(parts)), shapes)))
    loss = tot["loss"].reshape(())
    g_lb_table = _lb_table_grad(lb8, tot["lb"], hgrn_lb_table.shape[0])
    cw = sc_conv.shape[-1]
    g_sc_conv = lax.dynamic_slice_in_dim(tot["sc_conv"], me_chip[0] * cw, cw, axis=1)[None]
    cf = ffn_conv.shape[-1]
    g_ffn_conv = lax.dynamic_slice_in_dim(tot["ffn_conv"], me_chip[0] * cf, cf, axis=2)
    g_small = dict(norm_mix=tot["norm_mix"], norm_ffn=tot["norm_ffn"], hgrn_lb_table=g_lb_table,
                   hgrn_out_norm=tot["out_norm"], sc_conv=g_sc_conv, ffn_conv=g_ffn_conv, final_norm=tot["final_norm"])
    w_small = dict(norm_mix=norm_mix, norm_ffn=norm_ffn, hgrn_lb_table=hgrn_lb_table, hgrn_out_norm=hgrn_out_norm,
                   sc_conv=sc_conv, ffn_conv=ffn_conv, final_norm=final_norm)
    m_small = dict(norm_mix=m_norm_mix, norm_ffn=m_norm_ffn, hgrn_lb_table=m_hgrn_lb_table, hgrn_out_norm=m_hgrn_out_norm,
                   sc_conv=m_sc_conv, ffn_conv=m_ffn_conv, final_norm=m_final_norm)
    v_small = dict(norm_mix=v_norm_mix, norm_ffn=v_norm_ffn, hgrn_lb_table=v_hgrn_lb_table, hgrn_out_norm=v_hgrn_out_norm,
                   sc_conv=v_sc_conv, ffn_conv=v_ffn_conv, final_norm=v_final_norm)
    sm_names = list(g_small)
    sm_shapes = [w_small[n].shape for n in sm_names]
    d_s, m_s, v_s = _adamw(_pack([w_small[n] for n in sm_names]), _pack([g_small[n] for n in sm_names]),
                           _pack([m_small[n] for n in sm_names]), _pack([v_small[n] for n in sm_names]), "adamw_small")
    out_g, out_d, out_m, out_v = dict(g_small), {}, {}, {}
    for n, d_, m_, v_ in zip(sm_names, _unpack(d_s, sm_shapes), _unpack(m_s, sm_shapes), _unpack(v_s, sm_shapes)):
        out_d[n], out_m[n], out_v[n] = d_, m_, v_

    done = {}
    after = grad_x

    def add_and_share(k, after):
        slots, ss, rs, pair, land = pending[k]
        pair, recv = _chip_wait(ss, rs, pair, land, after, "grad_chip_wait_%d" % k)
        mine = [_add_chips(p, r, me_chip, "grad_add_chips") for p, r in zip(pair, recv)]
        ss, rs, mine, land, _ = _sibling_start(mine, False, None, "grad_share_start_%d" % k)
        return slots, ss, rs, mine, land

    def update(k, share, after):
        slots, ss, rs, mine, land = share
        mine, theirs = _sibling_wait(ss, rs, mine, land, False, after, "grad_share_wait_%d" % k)
        for (n, layer), gm, gr in zip(slots, mine, theirs):
            done[n] = _adamw_halves(flat2(big_w[n]), flat2(big_m[n]), flat2(big_v[n]), gm, gr, me_core, "adamw_" + n,
                                    layer=layer, prev=done.get(n))
        return done[slots[-1][0]][0]

    shares = []
    for k in range(len(pending) - 1):
        shares.append(add_and_share(k, after))
        after = shares[-1][3][0]
    for k, share in enumerate(shares):
        after = update(k, share, after)
    last = len(pending) - 1
    share = add_and_share(last, after)
    update(last, share, share[3][0])
    for n in big_names:
        out_g[n], out_d[n], out_m[n], out_v[n] = (a.reshape(big_w[n].shape) for a in done[n])

    order = ["norm_mix", "norm_ffn", "hgrn_w_in", "hgrn_lb_table", "hgrn_out_norm", "hgrn_w_out", "sc_w_in", "sc_conv",
             "sc_w_out", "ffn_w_up", "ffn_conv", "ffn_w_down", "final_norm"]
    return (loss, grad_x[None], *[out_g[n] for n in order], *[out_d[n] for n in order],
            *[out_m[n] for n in order], *[out_v[n] for n in order])
```

```python
import functools

import jax
import jax.numpy as jnp
from jax import lax
from jax.experimental import pallas as pl
from jax.experimental.pallas import tpu as pltpu

F32 = jnp.float32
BF16 = jnp.bfloat16
MESH = pl.DeviceIdType.MESH

EPS = 1e-6
CHUNK = 64
HEAD = 128
N_CHIPS = 4
ADAM_LR, ADAM_B1, ADAM_B2, ADAM_EPS, ADAM_WD, ADAM_STEP = 0.001, 0.9, 0.999, 1e-08, 0.01, 10
VMEM_LIMIT = 56 * 1024 * 1024
SUBLANES = 8
LANES = 128


def _pcall(body, **kw):
    return pl.pallas_call(body, **kw)


def _params(sem, vmem=VMEM_LIMIT):
    return pltpu.CompilerParams(dimension_semantics=sem, vmem_limit_bytes=vmem)


def _pick(dim, prefs):
    for p in prefs:
        if p <= dim and dim % p == 0:
            return p
    return dim


def _sigmoid(x):
    return 1.0 / (1.0 + jnp.exp(-x))


def _wmap_col(cw, tn, r0):
    bps = cw // tn
    return lambda kb, nb: (nb // bps, r0 + kb, nb % bps)


def _wmap_row(kp, tk, r0):
    bps = kp // tk
    return lambda kb, nb: (kb // bps, r0 + kb % bps, nb)


def _mm_nn(a, w3, wmap, K, N, tm, tk, tn, name, res=None):
    M = a.shape[0]
    nk = K // tk

    def body(*refs):
        if res is None:
            a_ref, w_ref, o_ref = refs[:3]
        else:
            a_ref, w_ref, r_ref, o_ref = refs[:4]
        p = jnp.dot(a_ref[...], w_ref[...], preferred_element_type=F32)
        if nk == 1:
            o_ref[...] = p if res is None else p + r_ref[...]
            return
        acc = refs[-1]
        k = pl.program_id(2)

        @pl.when(k == 0)
        def _():
            acc[...] = p

        @pl.when(k > 0)
        def _():
            acc[...] += p

        @pl.when(k == nk - 1)
        def _():
            o_ref[...] = acc[...] if res is None else acc[...] + r_ref[...]

    if nk == 1:
        grid = (M // tm, N // tn)
        ix = lambda f: (lambda i, j: f(i, j, 0))
        sem = ("parallel", "parallel")
        scratch = []
    else:
        grid = (M // tm, N // tn, nk)
        ix = lambda f: f
        sem = ("parallel", "parallel", "arbitrary")
        scratch = [pltpu.VMEM((tm, tn), F32)]
    in_specs = [pl.BlockSpec((tm, tk), ix(lambda i, j, k: (i, k))),
                pl.BlockSpec((None, tk, tn), ix(lambda i, j, k: wmap(k, j)))]
    args = [a, w3]
    if res is not None:
        in_specs.append(pl.BlockSpec((tm, tn), ix(lambda i, j, k: (i, j))))
        args.append(res)
    return _pcall(
        body, name=name, grid=grid, in_specs=in_specs,
        out_specs=pl.BlockSpec((tm, tn), ix(lambda i, j, k: (i, j))),
        out_shape=jax.ShapeDtypeStruct((M, N), F32), scratch_shapes=scratch, compiler_params=_params(sem),
    )(*args)


def _mm_nt(dy3, w3, wmap, K, N, tm, tk, tn, name, per_step=1):
    M = dy3.shape[1]
    bps = dy3.shape[2] // tn
    u = per_step
    grid = (M // tm, K // tk, N // (tn * u))
    nn = grid[2]

    def body(*refs):
        o_ref = refs[-1]
        p = None
        for r in range(u):
            d = lax.dot_general(refs[r][...], refs[u + r][...], (((1,), (1,)), ((), ())), preferred_element_type=F32)
            p = d if p is None else p + d
        if nn == 1:
            o_ref[...] = p
            return
        n = pl.program_id(2)

        @pl.when(n == 0)
        def _():
            o_ref[...] = p

        @pl.when(n > 0)
        def _():
            o_ref[...] += p

    def dy_spec(r):
        return pl.BlockSpec((None, tm, tn), lambda i, j, n: ((n * u + r) // bps, i, (n * u + r) % bps))

    def w_spec(r):
        return pl.BlockSpec((None, tk, tn), lambda i, j, n: wmap(j, n * u + r))

    return _pcall(
        body, name=name, grid=grid,
        in_specs=[dy_spec(r) for r in range(u)] + [w_spec(r) for r in range(u)],
        out_specs=pl.BlockSpec((tm, tk), lambda i, j, n: (i, j)),
        out_shape=jax.ShapeDtypeStruct((M, K), F32),
        compiler_params=_params(("parallel", "parallel", "arbitrary")),
    )(*([dy3] * u), *([w3] * u))


def _mm_tn(x, dy3, shape4, wmap, K, N, tk, tn, name, tm=None):
    M = x.shape[0]
    tm = M if tm is None else tm
    nm = M // tm
    bps = dy3.shape[2] // tn

    def body(*refs):
        x_ref, dy_ref = refs[:2]
        p = lax.dot_general(x_ref[...], dy_ref[...], (((0,), (0,)), ((), ())), preferred_element_type=F32)
        if nm == 1:
            o_ref = refs[-1]
            o_ref[...] = p.astype(o_ref.dtype)
            return
        o_ref, acc = refs[-2:]
        m = pl.program_id(2)

        @pl.when(m == 0)
        def _():
            acc[...] = p

        @pl.when(m > 0)
        def _():
            acc[...] += p

        @pl.when(m == nm - 1)
        def _():
            o_ref[...] = acc[...].astype(o_ref.dtype)

    def omap(i, j, m):
        s, rb, cb = wmap(i, j)
        return (s, 0, rb, cb)

    return _pcall(
        body, name=name, grid=(K // tk, N // tn, nm),
        in_specs=[pl.BlockSpec((tm, tk), lambda i, j, m: (m, i)),
                  pl.BlockSpec((None, tm, tn), lambda i, j, m: (j // bps, m, j % bps))],
        out_specs=pl.BlockSpec((None, None, tk, tn), omap),
        out_shape=jax.ShapeDtypeStruct(shape4, BF16),
        scratch_shapes=[] if nm == 1 else [pltpu.VMEM((tk, tn), F32)],
        compiler_params=_params(("parallel", "parallel", "arbitrary")),
    )(x, dy3)


def _row_call(fn, rows, vecs, outs, n_acc, name, t_rows=256, sub=16, per_trip=4):
    T = rows[0][0].shape[0]
    t_rows = min(t_rows, T)
    nsub = t_rows // sub
    n_r, n_v, n_o = len(rows), len(vecs), len(outs)
    width = rows[0][2]

    def body(*refs):
        r_refs = refs[:n_r]
        v_refs = refs[n_r:n_r + n_v]
        o_refs = refs[n_r + n_v:n_r + n_v + n_o]
        a_refs = refs[n_r + n_v + n_o:]

        @pl.when(pl.program_id(0) == 0)
        def _():
            for a in a_refs:
                a[...] = jnp.zeros_like(a)

        vv = [v[...] for v in v_refs]

        def step(i, carry):
            done = []
            for u in range(per_trip):
                sl = pl.ds(pl.multiple_of((i * per_trip + u) * sub, sub), sub)
                done.append((sl,) + tuple(fn([r[sl, :] for r in r_refs], vv)))
            for sl, o_vals, a_vals in done:
                for o, val in zip(o_refs, o_vals):
                    o[sl, :] = val.astype(o.dtype)
            for a_i, a in enumerate(a_refs):
                tot = None
                for _, _, a_vals in done:
                    part = a_vals[a_i].reshape(sub // SUBLANES, SUBLANES, a_vals[a_i].shape[-1]).sum(axis=0)
                    tot = part if tot is None else tot + part
                a[...] += tot
            return carry

        lax.fori_loop(0, nsub // per_trip, step, 0)

    in_specs = [pl.BlockSpec((t_rows, w), functools.partial(lambda i, cb: (i, cb), cb=cb)) for _, cb, w in rows]
    in_specs += [pl.BlockSpec(v.shape, lambda i: (0, 0)) for v in vecs]
    out_specs = [pl.BlockSpec((t_rows, w), lambda i: (i, 0)) for w, _ in outs]
    out_specs += [pl.BlockSpec((SUBLANES, width), lambda i: (0, 0)) for _ in range(n_acc)]
    out_shape = [jax.ShapeDtypeStruct((T, w), dt) for w, dt in outs]
    out_shape += [jax.ShapeDtypeStruct((SUBLANES, width), F32) for _ in range(n_acc)]
    return _pcall(
        body, name=name, grid=(T // t_rows,), in_specs=in_specs, out_specs=out_specs, out_shape=out_shape,
        compiler_params=_params(("arbitrary",)),
    )(*[r[0] for r in rows], *vecs)


def _rms_fwd_fn(rv, vv):
    h, = rv
    w, = vv
    r = lax.rsqrt(jnp.mean(h * h, axis=-1, keepdims=True) + EPS)
    return [h * r * w], []


def _rms_bwd_fn(rv, vv):
    h, dxn, dh_in = rv
    w, = vv
    d = h.shape[-1]
    r = lax.rsqrt(jnp.mean(h * h, axis=-1, keepdims=True) + EPS)
    gy = dxn * w
    dh = r * gy - h * ((r * r * r) * (1.0 / d) * jnp.sum(gy * h, axis=-1, keepdims=True))
    return [dh_in + dh] * 2, [dxn * h * r]


def _final_fn(rv, vv):
    h, tgt = rv
    w, = vv
    d = h.shape[-1]
    r = lax.rsqrt(jnp.mean(h * h, axis=-1, keepdims=True) + EPS)
    hn = h * r
    e = hn * w - tgt
    dy = e * (1.0 / d)
    gy = dy * w
    dh = r * gy - h * ((r * r * r) * (1.0 / d) * jnp.sum(gy * h, axis=-1, keepdims=True))
    return [dh] * 2, [e * e, dy * hn]


def _onorm_fwd_fn(rv, vv):
    o, g = rv
    gain, = vv
    r = lax.rsqrt(jnp.mean(o * o, axis=-1, keepdims=True) + EPS)
    return [o * r * gain * (g * _sigmoid(g))], []


def _onorm_bwd_fn(rv, vv):
    o, g, don = rv
    gain, = vv
    d = o.shape[-1]
    r = lax.rsqrt(jnp.mean(o * o, axis=-1, keepdims=True) + EPS)
    sg = _sigmoid(g)
    sl = g * sg
    n = o * r
    dg = don * n * gain * (sg * (1.0 + g * (1.0 - sg)))
    gy = don * sl * gain
    do = r * gy - o * ((r * r * r) * (1.0 / d) * jnp.sum(gy * o, axis=-1, keepdims=True))
    return [do, dg], [don * sl * n]


HALO = SUBLANES


def _col_call(fn, cols, vecs, outs, n_acc, name, before, after, tc=LANES, chunk=128):
    T = cols[0][0].shape[0]
    chunk = min(chunk, T)
    nch = T // chunk
    ncol = outs[0][1] // tc
    n_c, n_v, n_o = len(cols), len(vecs), len(outs)
    hb = HALO if before else 0
    rw = chunk + hb + (HALO if after else 0)

    def body(*refs):
        c_refs = refs[:n_c]
        v_refs = refs[n_c:n_c + n_v]
        o_refs = refs[n_c + n_v:n_c + n_v + n_o]
        a_refs = refs[n_c + n_v + n_o:]
        vv = [v[...] for v in v_refs]
        wrow = lax.broadcasted_iota(jnp.int32, (rw, tc), 0)
        inside = (wrow >= hb) & (wrow < hb + chunk)

        def step(i, carry):
            r0 = pl.multiple_of(i * chunk, chunk)
            wins = []
            for ref in c_refs:
                parts = []
                if before:
                    pb = ref[pl.ds(pl.multiple_of(jnp.maximum(r0 - HALO, 0), HALO), HALO), :]
                    parts.append(jnp.where(i > 0, pb, 0.0))
                parts.append(ref[pl.ds(r0, chunk), :])
                if after:
                    pa = ref[pl.ds(pl.multiple_of(jnp.minimum(r0 + chunk, T - HALO), HALO), HALO), :]
                    parts.append(jnp.where(i < nch - 1, pa, 0.0))
                wins.append(jnp.concatenate(parts, axis=0) if len(parts) > 1 else parts[0])
            o_vals, a_vals = fn(wins, vv, inside)
            p = 0
            for o, (nseg, _, _) in zip(o_refs, outs):
                for s in range(nseg):
                    o[s, pl.ds(r0, chunk), :] = o_vals[p][hb:hb + chunk].astype(o.dtype)
                    p += 1
            return tuple(c + a for c, a in zip(carry, a_vals))

        taps = [v.shape[0] for v, _ in vecs][:n_acc]
        init = tuple(jnp.zeros((1, tc), F32) for k in taps for _ in range(k))
        sums = lax.fori_loop(0, nch, step, init)
        arow = lax.broadcasted_iota(jnp.int32, (SUBLANES, tc), 0)
        p = 0
        for a, k in zip(a_refs, taps):
            acc = jnp.zeros((SUBLANES, tc), F32)
            for t in range(k):
                acc = jnp.where(arow == t, sums[p], acc)
                p += 1
            a[...] = acc

    in_specs = [pl.BlockSpec((T, tc), functools.partial(lambda j, off: (0, off + j), off=off)) for _, off in cols]
    in_specs += [pl.BlockSpec((v.shape[0], tc), functools.partial(lambda j, off: (0, off + j), off=off))
                 for v, off in vecs]
    out_specs = [pl.BlockSpec((nseg, T, tc), lambda j: (0, 0, j)) for nseg, _, _ in outs]
    out_specs += [pl.BlockSpec((SUBLANES, tc), lambda j: (0, j)) for _ in range(n_acc)]
    out_shape = [jax.ShapeDtypeStruct((nseg, T, w), dt) for nseg, w, dt in outs]
    out_shape += [jax.ShapeDtypeStruct((SUBLANES, ncol * tc), F32) for _ in range(n_acc)]
    return _pcall(
        body, name=name, grid=(ncol,), in_specs=in_specs, out_specs=out_specs, out_shape=out_shape,
        compiler_params=_params(("parallel",)),
    )(*[c[0] for c in cols], *[v[0] for v in vecs])


def _down(x, k):
    return x if k == 0 else pltpu.roll(x, k, 0)


def _up(x, k):
    return x if k == 0 else pltpu.roll(x, x.shape[0] - k, 0)


def _lags(x):
    return _down(x, 2), _down(x, 1), x


def _conv(lags, w):
    return w[0:1] * lags[0] + w[1:2] * lags[1] + w[2:3] * lags[2]


def _conv_t(d, w):
    return w[2:3] * d + w[1:2] * _up(d, 1) + w[0:1] * _up(d, 2)


def _tap_sums(d, lags, inside):
    dm = jnp.where(inside, d, 0.0)
    return [jnp.sum(dm * lag, axis=0, keepdims=True) for lag in lags]


def _glu_fwd_fn(wins, vv, inside):
    xg, xv = wins
    wg, wv = vv
    ug = _conv(_lags(xg), wg)
    uv = _conv(_lags(xv), wv)
    return [ug * _sigmoid(ug) * uv], []


def _glu_bwd_fn(wins, vv, inside):
    xg, xv, da = wins
    wg, wv = vv
    lg, lv = _lags(xg), _lags(xv)
    ug = _conv(lg, wg)
    uv = _conv(lv, wv)
    sg = _sigmoid(ug)
    dug = da * uv * (sg * (1.0 + ug * (1.0 - sg)))
    duv = da * (ug * sg)
    return [_conv_t(dug, wg), _conv_t(duv, wv)], _tap_sums(dug, lg, inside) + _tap_sums(duv, lv, inside)


def _sc_fwd_fn(wins, vv, inside):
    gb, gc, hh = wins
    w, = vv
    return [gb * _conv(_lags(gc * hh), w)], []


def _sc_bwd_fn(wins, vv, inside):
    gb, gc, hh, dy = wins
    w, = vv
    lz = _lags(gc * hh)
    dcv = dy * gb
    dz = _conv_t(dcv, w)
    return [dy * _conv(lz, w), dz * hh, dz * gc], _tap_sums(dcv, lz, inside)


def _gates(qr, fr, lb):
    sg = _sigmoid(fr)
    f = lb + (1.0 - lb) * sg
    sq = _sigmoid(qr)
    q = qr * sq * (HEAD ** -0.5)
    return q, 1.0 - f, jnp.log(f), f, sg, sq


def _boundary_rows(b, g, row):
    c = b.shape[0]
    if 2 * g >= SUBLANES:
        x = b.reshape(c // (2 * g), 2 * g, LANES)
        return jnp.broadcast_to(x[:, g - 1:g, :], x.shape).reshape(c, LANES)
    x = b.reshape(c // SUBLANES, SUBLANES, LANES)
    lo = jnp.broadcast_to(x[:, 1:2, :], x.shape).reshape(c, LANES)
    hi = jnp.broadcast_to(x[:, 5:6, :], x.shape).reshape(c, LANES)
    return jnp.where((row & 4) == 0, lo, hi)


def _chunk_decays(gl, f, row):
    c = gl.shape[0]
    b = gl
    d = 1
    while d < c:
        b = b + jnp.where(row >= d, pltpu.roll(b, d, 0), 0.0)
        d *= 2
    eq, ek = [], []
    g = c // 2
    while g >= 2:
        right = (row & g) != 0
        m = _boundary_rows(b, g, row)
        z = jnp.exp(jnp.where(right, b - m, m - b))
        eq.append(jnp.where(right, z, 0.0))
        ek.append(jnp.where(right, 0.0, z))
        g //= 2
    odd = (row & 1) != 0
    eq.append(jnp.where(odd, f, 0.0))
    ek.append(jnp.where(odd, 0.0, 1.0))
    return b, eq, ek


def _intra(q, k, eq, ek, tt, ss):
    c = q.shape[0]
    qs, ks = [], []
    a = jnp.where(tt == ss, jnp.sum(q * k, axis=1, keepdims=True), 0.0)
    g = c // 2
    for e_q, e_k in zip(eq, ek):
        qg = (q * e_q).astype(BF16)
        kg = (k * e_k).astype(BF16)
        p = lax.dot_general(qg, kg, (((1,), (1,)), ((), ())), preferred_element_type=F32)
        a = a + (p if 2 * g >= c else jnp.where((tt ^ ss) < 2 * g, p, 0.0))
        qs.append(qg)
        ks.append(kg)
        g //= 2
    return a, qs, ks


def _hgrn_fwd(proj, lb, d_model):
    T = proj.shape[0]
    H = d_model // HEAD
    nch = T // CHUNK

    def body(q_ref, f_ref, v_ref, lb_ref, o_ref, s_ref):
        lbv = lb_ref[...]
        row = lax.broadcasted_iota(jnp.int32, (CHUNK, HEAD), 0)
        tt = lax.broadcasted_iota(jnp.int32, (CHUNK, CHUNK), 0)
        ss = lax.broadcasted_iota(jnp.int32, (CHUNK, CHUNK), 1)

        def step(i, st):
            sl = pl.ds(pl.multiple_of(i * CHUNK, CHUNK), CHUNK)
            q, k, gl, f, _, _ = _gates(q_ref[sl, :], f_ref[sl, :], lbv)
            v = v_ref[sl, :].astype(BF16)
            b, eq, ek = _chunk_decays(gl, f, row)
            a, _, _ = _intra(q, k, eq, ek, tt, ss)
            bl = b[CHUNK - 1:CHUNK, :]
            q0 = (q * jnp.exp(b)).astype(BF16)
            kh = (k * jnp.exp(bl - b)).astype(BF16)
            s_ref[i] = st
            o = jnp.dot(a.astype(BF16), v, preferred_element_type=F32)
            o = o + lax.dot_general(q0, st.astype(BF16), (((1,), (1,)), ((), ())), preferred_element_type=F32)
            o_ref[sl, :] = o
            return jnp.exp(bl) * st + lax.dot_general(v, kh, (((0,), (0,)), ((), ())), preferred_element_type=F32)

        per = 4 if nch % 4 == 0 else 2

        def trip(i, st):
            for u in range(per):
                st = step(per * i + u, st)
            return st

        lax.fori_loop(0, nch // per, trip, jnp.zeros((HEAD, HEAD), F32))

    col = lambda off: pl.BlockSpec((T, HEAD), functools.partial(lambda h, off: (0, off + h), off=off))
    return _pcall(
        body, name="hgrn_fwd", grid=(H,),
        in_specs=[col(0), col(H), col(2 * H), pl.BlockSpec((1, HEAD), lambda h: (0, h))],
        out_specs=[pl.BlockSpec((T, HEAD), lambda h: (0, h)),
                   pl.BlockSpec((None, nch, HEAD, HEAD), lambda h: (h, 0, 0, 0))],
        out_shape=[jax.ShapeDtypeStruct((T, d_model), F32), jax.ShapeDtypeStruct((H, nch, HEAD, HEAD), F32)],
        compiler_params=_params(("parallel",)),
    )(proj, proj, proj, lb)


def _hgrn_bwd(proj, lb, states, do, dgate, d_model):
    T = proj.shape[0]
    H = d_model // HEAD
    nch = T // CHUNK

    def body(q_ref, f_ref, v_ref, lb_ref, s_ref, do_ref, dg_ref, dp_ref, dlb_ref):
        dq_ref, df_ref, dv_ref = dp_ref.at[0], dp_ref.at[1], dp_ref.at[2]
        dp_ref[3] = dg_ref[...]
        lbv = lb_ref[...]
        row = lax.broadcasted_iota(jnp.int32, (CHUNK, HEAD), 0)
        tt = lax.broadcasted_iota(jnp.int32, (CHUNK, CHUNK), 0)
        ss = lax.broadcasted_iota(jnp.int32, (CHUNK, CHUNK), 1)
        last = row == CHUNK - 1
        nt = (((1,), (1,)), ((), ()))
        tn = (((0,), (0,)), ((), ()))

        def step(j, carry):
            dst, dlb = carry
            i = nch - 1 - j
            sl = pl.ds(pl.multiple_of(i * CHUNK, CHUNK), CHUNK)
            qr = q_ref[sl, :]
            q, k, gl, f, sg, sq = _gates(qr, f_ref[sl, :], lbv)
            v = v_ref[sl, :].astype(BF16)
            d_o = do_ref[sl, :].astype(BF16)
            st = s_ref[i]
            st16 = st.astype(BF16)
            dst16 = dst.astype(BF16)
            b, eq, ek = _chunk_decays(gl, f, row)
            a, qs, ks = _intra(q, k, eq, ek, tt, ss)
            bl = b[CHUNK - 1:CHUNK, :]
            e0 = jnp.exp(b)
            eh = jnp.exp(bl - b)
            ebl = jnp.exp(bl)
            q0 = q * e0
            kh = k * eh
            q016 = q0.astype(BF16)
            kh16 = kh.astype(BF16)
            dv = lax.dot_general(a.astype(BF16), d_o, tn, preferred_element_type=F32)
            dv = dv + lax.dot_general(kh16, dst16, nt, preferred_element_type=F32)
            dv_ref[sl, :] = dv.astype(dv_ref.dtype)
            da = lax.dot_general(d_o, v, nt, preferred_element_type=F32)
            da = jnp.where(tt >= ss, da, 0.0)
            dd = jnp.sum(jnp.where(tt == ss, da, 0.0), axis=1, keepdims=True)
            dq0 = jnp.dot(d_o, st16, preferred_element_type=F32)
            dkh = jnp.dot(v, dst16, preferred_element_type=F32)
            dq = dq0 * e0 + dd * k
            dk = dkh * eh + dd * q
            db = dq0 * q016.astype(F32) - dkh * kh16.astype(F32)
            g = CHUNK // 2
            for e_q, e_k, qg, kg in zip(eq, ek, qs, ks):
                dag = (da if 2 * g >= CHUNK else jnp.where((tt ^ ss) < 2 * g, da, 0.0)).astype(BF16)
                dqg = jnp.dot(dag, kg, preferred_element_type=F32)
                dkg = lax.dot_general(dag, qg, tn, preferred_element_type=F32)
                dq = dq + dqg * e_q
                dk = dk + dkg * e_k
                db = db + (dqg * qg.astype(F32) - dkg * kg.astype(F32))
                g //= 2
            dbl = jnp.sum(dkh * kh16.astype(F32), axis=0, keepdims=True) + ebl * jnp.sum(dst * st, axis=0, keepdims=True)
            db = db + jnp.where(last, dbl, 0.0)
            d = 1
            while d < CHUNK:
                db = db + jnp.where(row < CHUNK - d, pltpu.roll(db, CHUNK - d, 0), 0.0)
                d *= 2
            dfg = db / f - dk
            df_ref[sl, :] = (dfg * (1.0 - lbv) * sg * (1.0 - sg)).astype(df_ref.dtype)
            dq_ref[sl, :] = (dq * (HEAD ** -0.5) * (sq * (1.0 + qr * (1.0 - sq)))).astype(dq_ref.dtype)
            dlb = dlb + jnp.sum(dfg * (1.0 - sg), axis=0, keepdims=True)
            dst = ebl * dst + lax.dot_general(d_o, q016, tn, preferred_element_type=F32)
            return dst, dlb

        _, dlb = lax.fori_loop(0, nch // 2, lambda j, cr: step(2 * j + 1, step(2 * j, cr)),
                               (jnp.zeros((HEAD, HEAD), F32), jnp.zeros((1, HEAD), F32)))
        arow = lax.broadcasted_iota(jnp.int32, (SUBLANES, HEAD), 0)
        dlb_ref[...] = jnp.where(arow == 0, dlb, 0.0)

    col = lambda off: pl.BlockSpec((T, HEAD), functools.partial(lambda h, off: (0, off + h), off=off))
    return _pcall(
        body, name="hgrn_bwd", grid=(H,),
        in_specs=[col(0), col(H), col(2 * H), pl.BlockSpec((1, HEAD), lambda h: (0, h)),
                  pl.BlockSpec((None, nch, HEAD, HEAD), lambda h: (h, 0, 0, 0)), col(0), col(0)],
        out_specs=[pl.BlockSpec((4, T, HEAD), lambda h: (0, 0, h)), pl.BlockSpec((SUBLANES, HEAD), lambda h: (0, h))],
        out_shape=[jax.ShapeDtypeStruct((4, T, d_model), BF16), jax.ShapeDtypeStruct((SUBLANES, d_model), F32)],
        compiler_params=_params(("parallel",)),
    )(proj, proj, proj, lb, states, do, dgate)


def _lb_softmax(table):
    n, f = table.shape

    def body(t_ref, p_ref):
        t = t_ref[...]
        e = jnp.exp(t - jnp.max(t, axis=0, keepdims=True))
        p_ref[...] = e / jnp.sum(e, axis=0, keepdims=True)

    padded = jnp.pad(table, ((0, SUBLANES - n), (0, 0)), constant_values=-jnp.inf)
    return _pcall(body, name="lb_softmax", out_shape=jax.ShapeDtypeStruct((SUBLANES, f), F32))(padded)


def _adamw_math(w, g, m, v):
    m = ADAM_B1 * m + (1.0 - ADAM_B1) * g
    v = ADAM_B2 * v + (1.0 - ADAM_B2) * (g * g)
    m_hat = m / (1.0 - ADAM_B1 ** ADAM_STEP)
    v_hat = v / (1.0 - ADAM_B2 ** ADAM_STEP)
    delta = -ADAM_LR * (m_hat / (jnp.sqrt(v_hat) + ADAM_EPS) + ADAM_WD * w)
    return delta, m, v


def _adamw(w, g, m, v, name):
    R, C = w.shape
    tr = _pick(R, (128, 64, 32, 16, 8))

    def body(w_ref, g_ref, m_ref, v_ref, d_ref, nm_ref, nv_ref):
        d, nm, nv = _adamw_math(w_ref[...], g_ref[...], m_ref[...], v_ref[...])
        d_ref[...] = d
        nm_ref[...] = nm
        nv_ref[...] = nv

    spec = pl.BlockSpec((tr, C), lambda i: (i, 0))
    return _pcall(
        body, name=name, grid=(R // tr,), in_specs=[spec] * 4, out_specs=[spec] * 3,
        out_shape=[jax.ShapeDtypeStruct((R, C), F32)] * 3, compiler_params=_params(("parallel",)),
    )(w, g, m, v)


def _adamw_halves(w, m, v, g_mine, g_recv, c, name, layer=0, prev=None):
    C = w.shape[1]
    rh = g_mine.shape[0]
    tr = _pick(rh, (128, 64, 32, 16, 8))
    nb = rh // tr
    r0 = layer * 2 * nb

    def body(c_ref, w_ref, m_ref, v_ref, gm_ref, gr_ref, *rest):
        g_ref, d_ref, nm_ref, nv_ref = rest[-4:]
        g = jnp.where(pl.program_id(0) == c_ref[0], gm_ref[...], gr_ref[...])
        d, nm, nv = _adamw_math(w_ref[...], g, m_ref[...], v_ref[...])
        g_ref[...] = g
        d_ref[...] = d
        nm_ref[...] = nm
        nv_ref[...] = nv

    full = pl.BlockSpec((tr, C), lambda h, i, cr: (r0 + h * nb + i, 0))
    half = pl.BlockSpec((tr, C), lambda h, i, cr: (i, 0))
    in_specs = [full, full, full, half, half]
    args = [c, w, m, v, g_mine, g_recv]
    alias = {}
    if prev is not None:
        in_specs += [pl.BlockSpec(memory_space=pl.ANY)] * 4
        args += list(prev)
        alias = {6 + k: k for k in range(4)}
    return _pcall(
        body, name=name,
        grid_spec=pltpu.PrefetchScalarGridSpec(
            num_scalar_prefetch=1, grid=(2, nb), in_specs=in_specs, out_specs=[full] * 4),
        out_shape=[jax.ShapeDtypeStruct(w.shape, F32)] * 4, input_output_aliases=alias,
        compiler_params=_params(("parallel", "parallel")),
    )(*args)


def _lb_table_grad(p8, dlb, n):
    f = p8.shape[1]

    def body(p_ref, d_ref, o_ref):
        p = p_ref[...]
        d = d_ref[...]
        p0 = p[0:1, :]
        first = lax.broadcasted_iota(jnp.int32, p.shape, 0) == 0
        o_ref[...] = p * (jnp.where(first, d, 0.0) - d * p0)

    return _pcall(body, name="lb_table_grad", out_shape=jax.ShapeDtypeStruct((SUBLANES, f), F32))(p8, dlb)[:n]


def _place():
    x, y, c = lax.axis_index("x"), lax.axis_index("y"), lax.axis_index("c")
    chips = [(1 - x, y), (x, 1 - y), (1 - x, 1 - y)]
    return x, y, c, chips


HBM_SPEC = pl.BlockSpec(memory_space=pltpu.HBM)


def _gather_weights(big, small):
    nb, ns = len(big), len(small)
    n = nb + ns

    def body(*refs):
        ins, outs = refs[:n], refs[n:2 * n]
        send_sems, recv_sems, own_send, own_recv = refs[2 * n:]
        x, y, c, chips = _place()
        me = 2 * x + y
        sib = (x, y, 1 - c)
        own = [pltpu.make_async_remote_copy(
            src_ref=ins[t], dst_ref=outs[t].at[me], send_sem=own_send.at[t], recv_sem=own_recv.at[t],
            device_id=sib, device_id_type=MESH) for t in range(n)]
        for cp in own:
            cp.start()

        def half(t, h):
            rh = big[t].shape[0] // 2
            return pl.ds(pl.multiple_of(h * rh, rh), rh)

        sends = []
        for t in range(n):
            for j, chip in enumerate(chips):
                k = 6 * t + j
                if t < nb:
                    src, dst = ins[t].at[half(t, c)], outs[t].at[me, half(t, c)]
                else:
                    src, dst = ins[t], outs[t].at[me]
                sends.append(pltpu.make_async_remote_copy(
                    src_ref=src, dst_ref=dst, send_sem=send_sems.at[k], recv_sem=recv_sems.at[k],
                    device_id=(*chip, c), device_id_type=MESH))
        for cp in sends:
            cp.start()
        passed = []
        for t in range(n):
            for j, (cx, cy) in enumerate(chips):
                k = 6 * t + j
                s = 2 * cx + cy
                if t < nb:
                    landed = outs[t].at[s, half(t, c)]
                    pltpu.make_async_remote_copy(
                        src_ref=landed, dst_ref=landed, send_sem=send_sems.at[k], recv_sem=recv_sems.at[k],
                        device_id=sib, device_id_type=MESH).wait_recv()
                    fwd = pltpu.make_async_remote_copy(
                        src_ref=landed, dst_ref=landed, send_sem=send_sems.at[k + 3], recv_sem=recv_sems.at[k + 3],
                        device_id=sib, device_id_type=MESH)
                    fwd.start()
                    passed.append(fwd)
                else:
                    landed = outs[t].at[s]
                    pltpu.make_async_remote_copy(
                        src_ref=landed, dst_ref=landed, send_sem=send_sems.at[k], recv_sem=recv_sems.at[k],
                        device_id=sib, device_id_type=MESH).wait_recv()
        for t in range(nb):
            for j, (cx, cy) in enumerate(chips):
                k = 6 * t + j
                other = outs[t].at[2 * cx + cy, half(t, 1 - c)]
                pltpu.make_async_remote_copy(
                    src_ref=other, dst_ref=other, send_sem=send_sems.at[k + 3], recv_sem=recv_sems.at[k + 3],
                    device_id=sib, device_id_type=MESH).wait_recv()
        for cp in sends + passed:
            cp.wait_send()
        for cp in own:
            cp.wait()

    arrs = list(big) + list(small)
    return _pcall(
        body, name="gather_weights", in_specs=[HBM_SPEC] * n, out_specs=[HBM_SPEC] * n,
        out_shape=[jax.ShapeDtypeStruct((N_CHIPS,) + a.shape, a.dtype) for a in arrs],
        scratch_shapes=[pltpu.SemaphoreType.DMA((6 * n,)), pltpu.SemaphoreType.DMA((6 * n,)),
                        pltpu.SemaphoreType.DMA((n,)), pltpu.SemaphoreType.DMA((n,))],
    )(*arrs)


SEM_SPEC = pl.BlockSpec(memory_space=pltpu.SEMAPHORE)
DATAFLOW = pltpu.SideEffectType.DATAFLOW_SIDE_EFFECTING
COPIES_PER_SHARD = 4


def _shard_copies(ins, lands, send_sems, recv_sems):
    x, y, c, chips = _place()
    me = 2 * x + y
    cps = []
    for t in range(len(ins)):
        rh = ins[t].shape[0] // 2
        half = pl.ds(pl.multiple_of(c * rh, rh), rh)
        for j, chip in enumerate(chips):
            k = COPIES_PER_SHARD * t + j
            cps.append(pltpu.make_async_remote_copy(
                src_ref=ins[t].at[half], dst_ref=lands[t].at[me, half], send_sem=send_sems.at[k],
                recv_sem=recv_sems.at[k], device_id=(*chip, c), device_id_type=MESH))
        k = COPIES_PER_SHARD * t + 3
        cps.append(pltpu.make_async_remote_copy(
            src_ref=ins[t], dst_ref=lands[t].at[me], send_sem=send_sems.at[k], recv_sem=recv_sems.at[k],
            device_id=(x, y, 1 - c), device_id_type=MESH))
    return cps


def _gather_start(shards, thru, name):
    n = len(shards)
    nops = 2 * n + len(thru)

    def body(*refs):
        ins, lands = refs[:n], refs[n:2 * n]
        send_sems, recv_sems = refs[nops], refs[nops + 1]
        for cp in _shard_copies(ins, lands, send_sems, recv_sems):
            cp.start()

    lands = [pltpu.with_memory_space_constraint(lax.empty((N_CHIPS,) + s.shape, s.dtype), pltpu.HBM) for s in shards]
    ops = [pltpu.with_memory_space_constraint(s, pltpu.HBM) for s in shards] + lands + list(thru)
    nsem = COPIES_PER_SHARD * n
    res = _pcall(
        body, name=name, in_specs=[HBM_SPEC] * nops,
        out_specs=[SEM_SPEC, SEM_SPEC] + [HBM_SPEC] * nops,
        out_shape=[pltpu.SemaphoreType.DMA((nsem,)), pltpu.SemaphoreType.DMA((nsem,))]
        + [pltpu.HBM(o.shape, o.dtype) for o in ops],
        input_output_aliases={i: 2 + i for i in range(nops)},
        compiler_params=pltpu.CompilerParams(has_side_effects=DATAFLOW),
    )(*ops)
    return res[0], res[1], res[2:2 + n], res[2 + n:2 + 2 * n], list(res[2 + 2 * n:])


def _gather_wait(send_sems, recv_sems, shards, lands, after, name):
    n = len(shards)

    def body(*refs):
        ins, lnd = refs[:n], refs[n:2 * n]
        ssem, rsem = refs[2 * n], refs[2 * n + 1]
        for cp in _shard_copies(ins, lnd, ssem, rsem):
            cp.wait_send()
            cp.wait_recv()

    res = _pcall(
        body, name=name,
        in_specs=[HBM_SPEC] * (2 * n) + [SEM_SPEC, SEM_SPEC, pl.BlockSpec(memory_space=pl.ANY)],
        out_specs=[HBM_SPEC] * (2 * n),
        out_shape=[pltpu.HBM(o.shape, o.dtype) for o in list(shards) + list(lands)],
        input_output_aliases={i: i for i in range(2 * n)},
        compiler_params=pltpu.CompilerParams(has_side_effects=DATAFLOW),
    )(*shards, *lands, send_sems, recv_sems, after)
    return res[n:]


def _forward_copies(land, send_sems, recv_sems):
    x, y, c, chips = _place()
    rh = land.shape[1] // 2
    return [pltpu.make_async_remote_copy(
        src_ref=land.at[2 * cx + cy, pl.ds(pl.multiple_of(c * rh, rh), rh)],
        dst_ref=land.at[2 * cx + cy, pl.ds(pl.multiple_of(c * rh, rh), rh)],
        send_sem=send_sems.at[j], recv_sem=recv_sems.at[j], device_id=(x, y, 1 - c), device_id_type=MESH)
        for j, (cx, cy) in enumerate(chips)]


def _forward_start(land, thru, name):
    def body(land_ref, thru_ref, send_sems, recv_sems, out_ref, thru_out):
        for cp in _forward_copies(land_ref, send_sems, recv_sems):
            cp.start()

    return _pcall(
        body, name=name, in_specs=[HBM_SPEC, HBM_SPEC], out_specs=[SEM_SPEC, SEM_SPEC, HBM_SPEC, HBM_SPEC],
        out_shape=[pltpu.SemaphoreType.DMA((3,)), pltpu.SemaphoreType.DMA((3,)), pltpu.HBM(land.shape, land.dtype),
                   pltpu.HBM(thru.shape, thru.dtype)],
        input_output_aliases={0: 2, 1: 3}, compiler_params=pltpu.CompilerParams(has_side_effects=DATAFLOW),
    )(land, thru)


def _forward_wait(send_sems, recv_sems, land, after, name):
    def body(land_ref, ssem, rsem, after_ref, out_ref):
        for cp in _forward_copies(land_ref, ssem, rsem):
            cp.wait_send()
            cp.wait_recv()

    return _pcall(
        body, name=name, in_specs=[HBM_SPEC, SEM_SPEC, SEM_SPEC, pl.BlockSpec(memory_space=pl.ANY)],
        out_specs=HBM_SPEC, out_shape=pltpu.HBM(land.shape, land.dtype), input_output_aliases={0: 0},
        compiler_params=pltpu.CompilerParams(has_side_effects=DATAFLOW),
    )(land, send_sems, recv_sems, after)


def _sibling_copies(ins, lands, send_sems, recv_sems, other_half):
    x, y, c, _ = _place()
    return [pltpu.make_async_remote_copy(
        src_ref=ins[t].at[:, 1 - c] if other_half else ins[t], dst_ref=lands[t], send_sem=send_sems.at[t],
        recv_sem=recv_sems.at[t], device_id=(x, y, 1 - c), device_id_type=MESH) for t in range(len(ins))]


def _sibling_start(srcs, other_half, thru, name):
    n = len(srcs)
    nthru = 0 if thru is None else 1

    def body(*refs):
        ins, lands = refs[:n], refs[n:2 * n]
        send_sems, recv_sems = refs[2 * n + nthru], refs[2 * n + nthru + 1]
        for cp in _sibling_copies(ins, lands, send_sems, recv_sems, other_half):
            cp.start()

    shapes = [(s.shape[0],) + s.shape[2:] if other_half else s.shape for s in srcs]
    lands = [pltpu.with_memory_space_constraint(lax.empty(sh, s.dtype), pltpu.HBM) for sh, s in zip(shapes, srcs)]
    ops = [pltpu.with_memory_space_constraint(s, pltpu.HBM) for s in srcs] + lands + ([] if thru is None else [thru])
    res = _pcall(
        body, name=name, in_specs=[HBM_SPEC] * len(ops),
        out_specs=[SEM_SPEC, SEM_SPEC] + [HBM_SPEC] * len(ops),
        out_shape=[pltpu.SemaphoreType.DMA((n,)), pltpu.SemaphoreType.DMA((n,))]
        + [pltpu.HBM(o.shape, o.dtype) for o in ops],
        input_output_aliases={i: 2 + i for i in range(len(ops))},
        compiler_params=pltpu.CompilerParams(has_side_effects=DATAFLOW),
    )(*ops)
    return res[0], res[1], res[2:2 + n], res[2 + n:2 + 2 * n], (None if thru is None else res[2 + 2 * n])


def _sibling_wait(send_sems, recv_sems, srcs, lands, other_half, after, name):
    n = len(srcs)

    def body(*refs):
        ins, lnd = refs[:n], refs[n:2 * n]
        ssem, rsem = refs[2 * n], refs[2 * n + 1]
        for cp in _sibling_copies(ins, lnd, ssem, rsem, other_half):
            cp.wait_send()
            cp.wait_recv()

    res = _pcall(
        body, name=name,
        in_specs=[HBM_SPEC] * (2 * n) + [SEM_SPEC, SEM_SPEC, pl.BlockSpec(memory_space=pl.ANY)],
        out_specs=[HBM_SPEC] * (2 * n),
        out_shape=[pltpu.HBM(o.shape, o.dtype) for o in list(srcs) + list(lands)],
        input_output_aliases={i: i for i in range(2 * n)},
        compiler_params=pltpu.CompilerParams(has_side_effects=DATAFLOW),
    )(*srcs, *lands, send_sems, recv_sems, after)
    return res[:n], res[n:]


def _chip_copies(ins, lands, send_sems, recv_sems):
    x, y, c, chips = _place()
    cps = []
    for t in range(len(ins)):
        for j, (cx, cy) in enumerate(chips):
            cps.append(pltpu.make_async_remote_copy(
                src_ref=ins[t].at[2 * cx + cy], dst_ref=lands[t].at[j],
                send_sem=send_sems.at[3 * t + j], recv_sem=recv_sems.at[3 * t + j],
                device_id=(cx, cy, c), device_id_type=MESH))
    return cps


def _chip_start(parts, thru, name):
    n = len(parts)

    def body(*refs):
        ins, lands = refs[:n], refs[n:2 * n]
        send_sems, recv_sems = refs[2 * n + 1], refs[2 * n + 2]
        for cp in _chip_copies(ins, lands, send_sems, recv_sems):
            cp.start()

    lands = [pltpu.with_memory_space_constraint(lax.empty((3,) + p.shape[1:], p.dtype), pltpu.HBM) for p in parts]
    ops = [pltpu.with_memory_space_constraint(p, pltpu.HBM) for p in parts] + lands + [thru]
    res = _pcall(
        body, name=name, in_specs=[HBM_SPEC] * (2 * n + 1),
        out_specs=[SEM_SPEC, SEM_SPEC] + [HBM_SPEC] * (2 * n + 1),
        out_shape=[pltpu.SemaphoreType.DMA((3 * n,)), pltpu.SemaphoreType.DMA((3 * n,))]
        + [pltpu.HBM(o.shape, o.dtype) for o in ops],
        input_output_aliases={i: 2 + i for i in range(2 * n + 1)},
        compiler_params=pltpu.CompilerParams(has_side_effects=DATAFLOW),
    )(*ops)
    return res[0], res[1], res[2:2 + n], res[2 + n:2 + 2 * n], res[2 + 2 * n]


def _chip_wait(send_sems, recv_sems, parts, lands, after, name):
    n = len(parts)

    def body(*refs):
        ins, lnd = refs[:n], refs[n:2 * n]
        ssem, rsem = refs[2 * n], refs[2 * n + 1]
        for cp in _chip_copies(ins, lnd, ssem, rsem):
            cp.wait_send()
            cp.wait_recv()

    res = _pcall(
        body, name=name,
        in_specs=[HBM_SPEC] * (2 * n) + [SEM_SPEC, SEM_SPEC, pl.BlockSpec(memory_space=pl.ANY)],
        out_specs=[HBM_SPEC] * (2 * n),
        out_shape=[pltpu.HBM(o.shape, o.dtype) for o in list(parts) + list(lands)],
        input_output_aliases={i: i for i in range(2 * n)},
        compiler_params=pltpu.CompilerParams(has_side_effects=DATAFLOW),
    )(*parts, *lands, send_sems, recv_sems, after)
    return res[:n], res[n:]


def _add_pair(grad, recv, c, name):
    s, _, rh, cc = grad.shape
    tr = _pick(rh, (256, 128, 64, 32, 16))

    def body(c_ref, g_ref, r_ref, o_ref):
        o_ref[...] = (g_ref[...].astype(F32) + r_ref[...].astype(F32)).astype(o_ref.dtype)

    return _pcall(
        body, name=name,
        grid_spec=pltpu.PrefetchScalarGridSpec(
            num_scalar_prefetch=1, grid=(s, rh // tr),
            in_specs=[pl.BlockSpec((None, None, tr, cc), lambda a, i, cr: (a, cr[0], i, 0)),
                      pl.BlockSpec((None, tr, cc), lambda a, i, cr: (a, i, 0))],
            out_specs=pl.BlockSpec((None, tr, cc), lambda a, i, cr: (a, i, 0))),
        out_shape=jax.ShapeDtypeStruct((s, rh, cc), BF16),
        compiler_params=_params(("parallel", "parallel")),
    )(c, grad, recv)


def _add_chips(part, recv, me, name):
    _, rh, cc = part.shape
    tr = _pick(rh, (256, 128, 64, 32, 16))

    def body(m_ref, p_ref, r_ref, o_ref):
        o_ref[...] = ((p_ref[...].astype(F32) + r_ref[0].astype(F32)) + r_ref[1].astype(F32)) + r_ref[2].astype(F32)

    return _pcall(
        body, name=name,
        grid_spec=pltpu.PrefetchScalarGridSpec(
            num_scalar_prefetch=1, grid=(rh // tr,),
            in_specs=[pl.BlockSpec((None, tr, cc), lambda i, mr: (mr[0], i, 0)),
                      pl.BlockSpec((3, tr, cc), lambda i, mr: (0, i, 0))],
            out_specs=pl.BlockSpec((tr, cc), lambda i, mr: (i, 0))),
        out_shape=jax.ShapeDtypeStruct((rh, cc), F32),
        compiler_params=_params(("parallel",)),
    )(me, part, recv)


def _all_sum(vec):
    rows = vec.shape[0]

    def body(v_ref, o_ref, buf, send_sems, recv_sems):
        x, y, c, _ = _place()
        me = 4 * x + 2 * y + c
        buf[me] = v_ref[...]
        cps = []
        for r in range(1, 8):
            fx, fy, fc = (r >> 2) & 1, (r >> 1) & 1, r & 1
            peer = (x ^ fx, y ^ fy, c ^ fc)
            cps.append(pltpu.make_async_remote_copy(
                src_ref=v_ref, dst_ref=buf.at[me], send_sem=send_sems.at[r - 1], recv_sem=recv_sems.at[r - 1],
                device_id=peer, device_id_type=MESH))
        for cp in cps:
            cp.start()
        for r in range(1, 8):
            src = me ^ r
            pltpu.make_async_remote_copy(
                src_ref=v_ref, dst_ref=buf.at[src], send_sem=send_sems.at[r - 1], recv_sem=recv_sems.at[r - 1],
                device_id=(x, y, c), device_id_type=MESH).wait_recv()
        for cp in cps:
            cp.wait_send()
        acc = buf[0]
        for d in range(1, 8):
            acc = acc + buf[d]
        o_ref[...] = acc

    return _pcall(
        body, name="all_sum_small",
        in_specs=[pl.BlockSpec(memory_space=pltpu.VMEM)], out_specs=pl.BlockSpec(memory_space=pltpu.VMEM),
        out_shape=jax.ShapeDtypeStruct((rows, LANES), F32),
        scratch_shapes=[pltpu.VMEM((8, rows, LANES), F32), pltpu.SemaphoreType.DMA((7,)), pltpu.SemaphoreType.DMA((7,))],
    )(vec)


def _pack(parts):
    flat = jnp.concatenate([p.reshape(-1) for p in parts])
    tile = SUBLANES * LANES
    pad = (-flat.shape[0]) % tile
    return jnp.pad(flat, (0, pad)).reshape(-1, LANES)


def _unpack(vec, shapes):
    flat = vec.reshape(-1)
    out, p = [], 0
    for s in shapes:
        n = 1
        for d in s:
            n *= d
        out.append(flat[p:p + n].reshape(s))
        p += n
    return out


def _local_step(x, tgt, norm_mix, norm_ffn, lb8, out_norm, final_norm, sc_conv, ffn_conv, weights, reduce_start,
                reduce_finish):
    T, D = x.shape
    F2 = ffn_conv.shape[-1]
    FF = F2 // 2
    tm = _pick(T, (1024, 512, 256, 128))
    wide = (1536, 1408, 1024, 768, 512, 384, 256, 128)
    cw_h, cw_s, cw_u = 4 * D // N_CHIPS, 3 * D // N_CHIPS, F2 // N_CHIPS
    kp = FF // N_CHIPS
    tk_ff = kp if kp % LANES == 0 else LANES
    tn_d = _pick(D, (1024, 512, 256, 128))
    tk_w = _pick(D, (512, 256, 128))
    tn_h = _pick(cw_h, (1024, 512, 256, 128))
    tn_s = _pick(D // N_CHIPS, (512, 256, 128))
    tn_u = _pick(cw_u, wide)
    lb = lb8[0:1]
    wm_sq = _wmap_col(D, tn_d, 0)
    wm_sq1 = _wmap_col(D, D, 0)
    seg1 = lambda a: a.reshape((1,) + a.shape)

    def mix_in(h, w):
        return _row_call(_rms_fwd_fn, [(h, 0, D)], [w], [(D, BF16)], 0, "rms_fwd")[0]

    def rms_bwd(h, dxn, dh, w):
        return _row_call(_rms_bwd_fn, [(h, 0, D), (dxn, 0, D), (dh, 0, D)], [w], [(D, F32), (D, BF16)], 1, "rms_bwd")

    def ffn_fwd(h, i, fetch_up, fetch_down, next_stage=None):
        xn = mix_in(h, norm_ffn[i:i + 1])
        tn = _pick(cw_u, wide)
        w_up = fetch_up(xn)
        up = _mm_nn(xn, w_up, _wmap_col(cw_u, tn, 0), D, F2, tm, D, tn, "ffn_up")
        nb = FF // LANES
        a = _col_call(_glu_fwd_fn, [(up, 0), (up, nb)], [(ffn_conv[i], 0), (ffn_conv[i], nb)], [(1, FF, BF16)], 0,
                      "glu_fwd", before=True, after=False)[0][0]
        later = None
        if next_stage is not None:
            later, a = weights(next_stage, a)
        w_down = fetch_down(a)
        h2 = _mm_nn(a, w_down, _wmap_row(kp, tk_ff, 0), FF, D, tm, tk_ff, tn_d, "ffn_down", res=h)
        return h2, (xn, up, a), w_up, w_down, later

    def ffn_bwd(dh, dh16, h, saved, i, w_up, w_down):
        xn, up, a = saved
        g_down = _mm_tn(a, seg1(dh16), (N_CHIPS, 1, kp, D), _wmap_row(kp, tk_ff, 0), FF, D, tk_ff, tn_d,
                        "ffn_down_dw", tm=_pick(T, (2048, 1024, 512, 256, 128)))
        dh16 = reduce_start(("ffn_w_down", i), g_down, dh16)
        da = _mm_nt(seg1(dh16), w_down, _wmap_row(kp, tk_ff, 0), FF, D, tm, tk_ff, D, "ffn_down_dx")
        nb = FF // LANES
        dgv, cg, cv = _col_call(_glu_bwd_fn, [(up, 0), (up, nb), (da, 0)], [(ffn_conv[i], 0), (ffn_conv[i], nb)],
                                [(2, FF, BF16)], 2, "glu_bwd", before=True, after=True)
        g_up = _mm_tn(xn, dgv, (N_CHIPS, 1, D, cw_u), _wmap_col(cw_u, tn_u, 0), D, F2, tk_w, tn_u, "ffn_up_dw")
        dgv = reduce_start(("ffn_w_up", i), g_up, dgv)
        dxn = _mm_nt(dgv, w_up, _wmap_col(cw_u, tn_u, 0), D, F2, tm, D, tn_u, "ffn_up_dx")
        dh2, dh2_16, dnw = rms_bwd(h, dxn, dh, norm_ffn[i:i + 1])
        return dh2, reduce_finish(dh2_16), dnw, jnp.concatenate([cg[:3], cv[:3]], axis=1)

    h0 = x
    xn0 = mix_in(h0, norm_mix[0:1])
    (fetch_hin,), xn0 = weights(0, xn0)
    w_hin = fetch_hin(xn0)
    proj = _mm_nn(xn0, w_hin, _wmap_col(cw_h, tn_h, 0), D, 4 * D, tm, D, tn_h, "hgrn_in")
    o, states = _hgrn_fwd(proj, lb, D)
    (fetch_hout, fetch_up0, fetch_down0), o = weights(1, o)
    on = _row_call(_onorm_fwd_fn, [(o, 0, D), (proj, 3, D)], [out_norm], [(D, BF16)], 0, "onorm_fwd")[0]
    w_hout1 = fetch_hout(on).reshape(1, D, D)
    h1 = _mm_nn(on, w_hout1, wm_sq, D, D, tm, D, tn_d, "hgrn_out", res=h0)
    h2, ffn0, w_up0, w_down0, (fetch_sin, fetch_sout) = ffn_fwd(h1, 0, fetch_up0, fetch_down0, next_stage=2)
    xn1 = mix_in(h2, norm_mix[1:2])
    w_sin = fetch_sin(xn1)
    tn_si = _pick(cw_s, wide)
    sproj = _mm_nn(xn1, w_sin, _wmap_col(cw_s, tn_si, 0), D, 3 * D, tm, D, tn_si, "sc_in")
    (fetch_up1, fetch_down1), sproj = weights(3, sproj)
    nd = D // LANES
    ysc = _col_call(_sc_fwd_fn, [(sproj, 0), (sproj, nd), (sproj, 2 * nd)], [(sc_conv, 0)], [(1, D, BF16)], 0,
                    "sc_fwd", before=True, after=False)[0][0]
    w_sout1 = fetch_sout(ysc).reshape(1, D, D)
    h3 = _mm_nn(ysc, w_sout1, wm_sq, D, D, tm, D, tn_d, "sc_out", res=h2)
    h4, ffn1, w_up1, w_down1, _ = ffn_fwd(h3, 1, fetch_up1, fetch_down1)

    dh, dh16, esq, dfinal = _row_call(_final_fn, [(h4, 0, D), (tgt, 0, D)], [final_norm], [(D, F32), (D, BF16)], 2,
                                      "final_loss")
    loss = 0.5 / D * jnp.sum(esq)
    dh, dh16, dnf1, dconv1 = ffn_bwd(dh, dh16, h3, ffn1, 1, w_up1, w_down1)
    g_sout = _mm_tn(ysc, seg1(dh16), (1, 1, D, D), wm_sq, D, D, tk_w, tn_d, "sc_out_dw")
    dh16 = reduce_start(("sc_w_out", 0), g_sout, dh16)
    dy = _mm_nt(seg1(dh16), w_sout1, wm_sq1, D, D, tm, D, D, "sc_out_dx")
    dsp, dscc = _col_call(_sc_bwd_fn, [(sproj, 0), (sproj, nd), (sproj, 2 * nd), (dy, 0)], [(sc_conv, 0)],
                          [(3, D, BF16)], 1, "sc_bwd", before=True, after=True)
    g_sin = _mm_tn(xn1, dsp, (N_CHIPS, 1, D, cw_s), _wmap_col(cw_s, tn_s, 0), D, 3 * D, tk_w, tn_s, "sc_in_dw")
    dsp = reduce_start(("sc_w_in", 0), g_sin, dsp)
    dxn = _mm_nt(dsp, w_sin, _wmap_col(cw_s, tn_s, 0), D, 3 * D, tm, D, tn_s, "sc_in_dx", per_step=3)
    dh, dh16, dnm1 = rms_bwd(h2, dxn, dh, norm_mix[1:2])
    dh16 = reduce_finish(dh16)
    dh, dh16, dnf0, dconv0 = ffn_bwd(dh, dh16, h1, ffn0, 0, w_up0, w_down0)
    g_hout = _mm_tn(on, seg1(dh16), (1, 1, D, D), wm_sq, D, D, tk_w, tn_d, "hgrn_out_dw")
    dh16 = reduce_start(("hgrn_w_out", 0), g_hout, dh16)
    don = _mm_nt(seg1(dh16), w_hout1, wm_sq1, D, D, tm, D, D, "hgrn_out_dx")
    do, dgate, dgain = _row_call(_onorm_bwd_fn, [(o, 0, D), (proj, 3, D), (don, 0, D)], [out_norm],
                                 [(D, F32), (D, BF16)], 1, "onorm_bwd")
    dproj, dlb = _hgrn_bwd(proj, lb, states, do, dgate, D)
    g_hin = _mm_tn(xn0, dproj, (N_CHIPS, 1, D, cw_h), _wmap_col(cw_h, tn_h, 0), D, 4 * D, tk_w, tn_h, "hgrn_in_dw")
    dproj = reduce_start(("hgrn_w_in", 0), g_hin, dproj)
    dxn = _mm_nt(dproj, w_hin, _wmap_col(cw_h, tn_h, 0), D, 4 * D, tm, D, tn_h, "hgrn_in_dx", per_step=2)
    grad_x, _, dnm0 = rms_bwd(h0, dxn, dh, norm_mix[0:1])
    dnm0 = reduce_finish(dnm0)

    small = dict(
        loss=loss,
        norm_mix=jnp.stack([jnp.sum(dnm0, axis=0), jnp.sum(dnm1, axis=0)]),
        norm_ffn=jnp.stack([jnp.sum(dnf0, axis=0), jnp.sum(dnf1, axis=0)]),
        lb=dlb[0:1],
        out_norm=jnp.sum(dgain, axis=0)[None],
        final_norm=jnp.sum(dfinal, axis=0),
        sc_conv=dscc[:3],
        ffn_conv=jnp.stack([dconv0, dconv1]),
    )
    return grad_x, small


def kernel(x, norm_mix, norm_ffn, hgrn_w_in, hgrn_lb_table, hgrn_out_norm, hgrn_w_out, sc_w_in, sc_conv, sc_w_out, ffn_w_up, ffn_conv, ffn_w_down, final_norm, loss_target, m_norm_mix, m_norm_ffn, m_hgrn_w_in, m_hgrn_lb_table, m_hgrn_out_norm, m_hgrn_w_out, m_sc_w_in, m_sc_conv, m_sc_w_out, m_ffn_w_up, m_ffn_conv, m_ffn_w_down, m_final_norm, v_norm_mix, v_norm_ffn, v_hgrn_w_in, v_hgrn_lb_table, v_hgrn_out_norm, v_hgrn_w_out, v_sc_w_in, v_sc_conv, v_sc_w_out, v_ffn_w_up, v_ffn_conv, v_ffn_w_down, v_final_norm):
    D = x.shape[-1]
    xi, yi, ci = lax.axis_index("x"), lax.axis_index("y"), lax.axis_index("c")
    me_chip = (2 * xi + yi).astype(jnp.int32).reshape(1)
    me_core = ci.astype(jnp.int32).reshape(1)

    big_names = ["hgrn_w_in", "hgrn_w_out", "sc_w_in", "sc_w_out", "ffn_w_up", "ffn_w_down"]
    big_w = dict(hgrn_w_in=hgrn_w_in, hgrn_w_out=hgrn_w_out, sc_w_in=sc_w_in, sc_w_out=sc_w_out,
                 ffn_w_up=ffn_w_up, ffn_w_down=ffn_w_down)
    big_m = dict(hgrn_w_in=m_hgrn_w_in, hgrn_w_out=m_hgrn_w_out, sc_w_in=m_sc_w_in, sc_w_out=m_sc_w_out,
                 ffn_w_up=m_ffn_w_up, ffn_w_down=m_ffn_w_down)
    big_v = dict(hgrn_w_in=v_hgrn_w_in, hgrn_w_out=v_hgrn_w_out, sc_w_in=v_sc_w_in, sc_w_out=v_sc_w_out,
                 ffn_w_up=v_ffn_w_up, ffn_w_down=v_ffn_w_down)
    flat2 = lambda a: a.reshape(-1, a.shape[-1])

    sh = lambda a: a.reshape(-1, a.shape[-1]).astype(BF16)
    conv_shards = [flat2(sc_conv), flat2(ffn_conv)]
    stages = [[sh(hgrn_w_in)], [sh(hgrn_w_out), sh(ffn_w_up[0]), sh(ffn_w_down[0])],
              [sh(sc_w_in), sh(sc_w_out)], [sh(ffn_w_up[1]), sh(ffn_w_down[1])]]
    scc4, fcc4 = _gather_weights([], [flat2(sc_conv), flat2(ffn_conv)])
    gathers = []
    thru = [scc4, norm_mix]
    for k, shards in enumerate(stages):
        ss, rs, src, land, thru = _gather_start(shards, thru, "gather_start_%d" % k)
        gathers.append((ss, rs, src, land))
    scc4, norm_mix = thru
    scc = jnp.moveaxis(scc4, 0, 1).reshape(3, D)
    f2 = ffn_conv.shape[-1] * N_CHIPS
    fcc = jnp.moveaxis(fcc4.reshape(N_CHIPS, 2, 3, -1), 0, 2).reshape(2, 3, f2)

    def weights(stage, after):
        lands = _gather_wait(*gathers[stage], after, "gather_wait_%d" % stage)
        fetch = []
        for t, land in enumerate(lands):
            ss, rs, land, after = _forward_start(land, after, "gather_forward_start_%d_%d" % (stage, t))
            fetch.append(functools.partial(_forward_wait, ss, rs, land, name="gather_forward_wait_%d_%d" % (stage, t)))
        return fetch, after

    pending = []
    started = []

    def reduce_start(slot, grad, thru):
        t = sum(len(b[0]) for b in pending) + len(started)
        halves = grad.reshape(N_CHIPS, 2, -1, grad.shape[-1])
        ss, rs, src, land, thru = _sibling_start([halves], True, thru, "grad_pair_start_%d" % t)
        started.append((slot, t, ss, rs, src, land))
        return thru

    def reduce_finish(thru):
        k = len(pending)
        pair = []
        for slot, t, ss, rs, src, land in started:
            src, recv = _sibling_wait(ss, rs, src, land, True, thru, "grad_pair_wait_%d" % t)
            pair.append(_add_pair(src[0], recv[0], me_core, "grad_add_pair"))
        ss, rs, pair, land, thru = _chip_start(pair, thru, "grad_chip_start_%d" % k)
        pending.append(([s[0] for s in started], ss, rs, pair, land))
        started.clear()
        return thru

    lb8 = _lb_softmax(hgrn_lb_table)
    grad_x, small = _local_step(
        x[0], loss_target[0], norm_mix, norm_ffn, lb8, hgrn_out_norm, final_norm[None], scc, fcc, weights,
        reduce_start, reduce_finish)

    small_names = ["loss", "norm_mix", "norm_ffn", "lb", "out_norm", "final_norm", "sc_conv", "ffn_conv"]
    parts = [small[n].astype(F32) for n in small_names]
    shapes = [p.shape for p in parts]
    tot = dict(zip(small_names, _unpack(_all_sum(_pack(parts)), shapes)))
    loss = tot["loss"].reshape(())
    g_lb_table = _lb_table_grad(lb8, tot["lb"], hgrn_lb_table.shape[0])
    cw = sc_conv.shape[-1]
    g_sc_conv = lax.dynamic_slice_in_dim(tot["sc_conv"], me_chip[0] * cw, cw, axis=1)[None]
    cf = ffn_conv.shape[-1]
    g_ffn_conv = lax.dynamic_slice_in_dim(tot["ffn_conv"], me_chip[0] * cf, cf, axis=2)
    g_small = dict(norm_mix=tot["norm_mix"], norm_ffn=tot["norm_ffn"], hgrn_lb_table=g_lb_table,
                   hgrn_out_norm=tot["out_norm"], sc_conv=g_sc_conv, ffn_conv=g_ffn_conv, final_norm=tot["final_norm"])
    w_small = dict(norm_mix=norm_mix, norm_ffn=norm_ffn, hgrn_lb_table=hgrn_lb_table, hgrn_out_norm=hgrn_out_norm,
                   sc_conv=sc_conv, ffn_conv=ffn_conv, final_norm=final_norm)
    m_small = dict(norm_mix=m_norm_mix, norm_ffn=m_norm_ffn, hgrn_lb_table=m_hgrn_lb_table, hgrn_out_norm=m_hgrn_out_norm,
                   sc_conv=m_sc_conv, ffn_conv=m_ffn_conv, final_norm=m_final_norm)
    v_small = dict(norm_mix=v_norm_mix, norm_ffn=v_norm_ffn, hgrn_lb_table=v_hgrn_lb_table, hgrn_out_norm=v_hgrn_out_norm,
                   sc_conv=v_sc_conv, ffn_conv=v_ffn_conv, final_norm=v_final_norm)
    sm_names = list(g_small)
    sm_shapes = [w_small[n].shape for n in sm_names]
    d_s, m_s, v_s = _adamw(_pack([w_small[n] for n in sm_names]), _pack([g_small[n] for n in sm_names]),
                           _pack([m_small[n] for n in sm_names]), _pack([v_small[n] for n in sm_names]), "adamw_small")
    out_g, out_d, out_m, out_v = dict(g_small), {}, {}, {}
    for n, d_, m_, v_ in zip(sm_names, _unpack(d_s, sm_shapes), _unpack(m_s, sm_shapes), _unpack(v_s, sm_shapes)):
        out_d[n], out_m[n], out_v[n] = d_, m_, v_

    done = {}
    after = grad_x

    def add_and_share(k, after):
        slots, ss, rs, pair, land = pending[k]
        pair, recv = _chip_wait(ss, rs, pair, land, after, "grad_chip_wait_%d" % k)
        mine = [_add_chips(p, r, me_chip, "grad_add_chips") for p, r in zip(pair, recv)]
        ss, rs, mine, land, _ = _sibling_start(mine, False, None, "grad_share_start_%d" % k)
        return slots, ss, rs, mine, land

    def update(k, share, after):
        slots, ss, rs, mine, land = share
        mine, theirs = _sibling_wait(ss, rs, mine, land, False, after, "grad_share_wait_%d" % k)
        for (n, layer), gm, gr in zip(slots, mine, theirs):
            done[n] = _adamw_halves(flat2(big_w[n]), flat2(big_m[n]), flat2(big_v[n]), gm, gr, me_core, "adamw_" + n,
                                    layer=layer, prev=done.get(n))
        return done[slots[-1][0]][0]

    shares = []
    for k in range(len(pending) - 1):
        shares.append(add_and_share(k, after))
        after = shares[-1][3][0]
    for k, share in enumerate(shares):
        after = update(k, share, after)
    last = len(pending) - 1
    share = add_and_share(last, after)
    update(last, share, share[3][0])
    for n in big_names:
        out_g[n], out_d[n], out_m[n], out_v[n] = (a.reshape(big_w[n].shape) for a in done[n])

    order = ["norm_mix", "norm_ffn", "hgrn_w_in", "hgrn_lb_table", "hgrn_out_norm", "hgrn_w_out", "sc_w_in", "sc_conv",
             "sc_w_out", "ffn_w_up", "ffn_conv", "ffn_w_down", "final_norm"]
    return (loss, grad_x[None], *[out_g[n] for n in order], *[out_d[n] for n in order],
            *[out_m[n] for n in order], *[out_v[n] for n in order])
```

```python
import functools

import jax
import jax.numpy as jnp
from jax import lax
from jax.experimental import pallas as pl
from jax.experimental.pallas import tpu as pltpu

F32 = jnp.float32
BF16 = jnp.bfloat16
MESH = pl.DeviceIdType.MESH

EPS = 1e-6
CHUNK = 64
HEAD = 128
N_CHIPS = 4
ADAM_LR, ADAM_B1, ADAM_B2, ADAM_EPS, ADAM_WD, ADAM_STEP = 0.001, 0.9, 0.999, 1e-08, 0.01, 10
VMEM_LIMIT = 56 * 1024 * 1024
SUBLANES = 8
LANES = 128


def _pcall(body, **kw):
    return pl.pallas_call(body, **kw)


def _params(sem, vmem=VMEM_LIMIT):
    return pltpu.CompilerParams(dimension_semantics=sem, vmem_limit_bytes=vmem)


def _pick(dim, prefs):
    for p in prefs:
        if p <= dim and dim % p == 0:
            return p
    return dim


def _sigmoid(x):
    return 1.0 / (1.0 + jnp.exp(-x))


def _wmap_col(cw, tn, r0):
    bps = cw // tn
    return lambda kb, nb: (nb // bps, r0 + kb, nb % bps)


def _wmap_row(kp, tk, r0):
    bps = kp // tk
    return lambda kb, nb: (kb // bps, r0 + kb % bps, nb)


def _mm_nn(a, w3, wmap, K, N, tm, tk, tn, name, res=None):
    M = a.shape[0]
    nk = K // tk

    def body(*refs):
        if res is None:
            a_ref, w_ref, o_ref = refs[:3]
        else:
            a_ref, w_ref, r_ref, o_ref = refs[:4]
        p = jnp.dot(a_ref[...], w_ref[...], preferred_element_type=F32)
        if nk == 1:
            o_ref[...] = p if res is None else p + r_ref[...]
            return
        acc = refs[-1]
        k = pl.program_id(2)

        @pl.when(k == 0)
        def _():
            acc[...] = p

        @pl.when(k > 0)
        def _():
            acc[...] += p

        @pl.when(k == nk - 1)
        def _():
            o_ref[...] = acc[...] if res is None else acc[...] + r_ref[...]

    if nk == 1:
        grid = (M // tm, N // tn)
        ix = lambda f: (lambda i, j: f(i, j, 0))
        sem = ("parallel", "parallel")
        scratch = []
    else:
        grid = (M // tm, N // tn, nk)
        ix = lambda f: f
        sem = ("parallel", "parallel", "arbitrary")
        scratch = [pltpu.VMEM((tm, tn), F32)]
    in_specs = [pl.BlockSpec((tm, tk), ix(lambda i, j, k: (i, k))),
                pl.BlockSpec((None, tk, tn), ix(lambda i, j, k: wmap(k, j)))]
    args = [a, w3]
    if res is not None:
        in_specs.append(pl.BlockSpec((tm, tn), ix(lambda i, j, k: (i, j))))
        args.append(res)
    return _pcall(
        body, name=name, grid=grid, in_specs=in_specs,
        out_specs=pl.BlockSpec((tm, tn), ix(lambda i, j, k: (i, j))),
        out_shape=jax.ShapeDtypeStruct((M, N), F32), scratch_shapes=scratch, compiler_params=_params(sem),
    )(*args)


def _mm_nt(dy3, w3, wmap, K, N, tm, tk, tn, name, per_step=1):
    M = dy3.shape[1]
    bps = dy3.shape[2] // tn
    u = per_step
    grid = (M // tm, K // tk, N // (tn * u))
    nn = grid[2]

    def body(*refs):
        o_ref = refs[-1]
        p = None
        for r in range(u):
            d = lax.dot_general(refs[r][...], refs[u + r][...], (((1,), (1,)), ((), ())), preferred_element_type=F32)
            p = d if p is None else p + d
        if nn == 1:
            o_ref[...] = p
            return
        n = pl.program_id(2)

        @pl.when(n == 0)
        def _():
            o_ref[...] = p

        @pl.when(n > 0)
        def _():
            o_ref[...] += p

    def dy_spec(r):
        return pl.BlockSpec((None, tm, tn), lambda i, j, n: ((n * u + r) // bps, i, (n * u + r) % bps))

    def w_spec(r):
        return pl.BlockSpec((None, tk, tn), lambda i, j, n: wmap(j, n * u + r))

    return _pcall(
        body, name=name, grid=grid,
        in_specs=[dy_spec(r) for r in range(u)] + [w_spec(r) for r in range(u)],
        out_specs=pl.BlockSpec((tm, tk), lambda i, j, n: (i, j)),
        out_shape=jax.ShapeDtypeStruct((M, K), F32),
        compiler_params=_params(("parallel", "parallel", "arbitrary")),
    )(*([dy3] * u), *([w3] * u))


def _mm_tn(x, dy3, shape4, wmap, K, N, tk, tn, name, tm=None):
    M = x.shape[0]
    tm = M if tm is None else tm
    nm = M // tm
    bps = dy3.shape[2] // tn

    def body(*refs):
        x_ref, dy_ref = refs[:2]
        p = lax.dot_general(x_ref[...], dy_ref[...], (((0,), (0,)), ((), ())), preferred_element_type=F32)
        if nm == 1:
            o_ref = refs[-1]
            o_ref[...] = p.astype(o_ref.dtype)
            return
        o_ref, acc = refs[-2:]
        m = pl.program_id(2)

        @pl.when(m == 0)
        def _():
            acc[...] = p

        @pl.when(m > 0)
        def _():
            acc[...] += p

        @pl.when(m == nm - 1)
        def _():
            o_ref[...] = acc[...].astype(o_ref.dtype)

    def omap(i, j, m):
        s, rb, cb = wmap(i, j)
        return (s, 0, rb, cb)

    return _pcall(
        body, name=name, grid=(K // tk, N // tn, nm),
        in_specs=[pl.BlockSpec((tm, tk), lambda i, j, m: (m, i)),
                  pl.BlockSpec((None, tm, tn), lambda i, j, m: (j // bps, m, j % bps))],
        out_specs=pl.BlockSpec((None, None, tk, tn), omap),
        out_shape=jax.ShapeDtypeStruct(shape4, BF16),
        scratch_shapes=[] if nm == 1 else [pltpu.VMEM((tk, tn), F32)],
        compiler_params=_params(("parallel", "parallel", "arbitrary")),
    )(x, dy3)


def _row_call(fn, rows, vecs, outs, n_acc, name, t_rows=256, sub=16, per_trip=4):
    T = rows[0][0].shape[0]
    t_rows = min(t_rows, T)
    nsub = t_rows // sub
    n_r, n_v, n_o = len(rows), len(vecs), len(outs)
    width = rows[0][2]

    def body(*refs):
        r_refs = refs[:n_r]
        v_refs = refs[n_r:n_r + n_v]
        o_refs = refs[n_r + n_v:n_r + n_v + n_o]
        a_refs = refs[n_r + n_v + n_o:]

        @pl.when(pl.program_id(0) == 0)
        def _():
            for a in a_refs:
                a[...] = jnp.zeros_like(a)

        vv = [v[...] for v in v_refs]

        def step(i, carry):
            done = []
            for u in range(per_trip):
                sl = pl.ds(pl.multiple_of((i * per_trip + u) * sub, sub), sub)
                done.append((sl,) + tuple(fn([r[sl, :] for r in r_refs], vv)))
            for sl, o_vals, a_vals in done:
                for o, val in zip(o_refs, o_vals):
                    o[sl, :] = val.astype(o.dtype)
            for a_i, a in enumerate(a_refs):
                tot = None
                for _, _, a_vals in done:
                    part = a_vals[a_i].reshape(sub // SUBLANES, SUBLANES, a_vals[a_i].shape[-1]).sum(axis=0)
                    tot = part if tot is None else tot + part
                a[...] += tot
            return carry

        lax.fori_loop(0, nsub // per_trip, step, 0)

    in_specs = [pl.BlockSpec((t_rows, w), functools.partial(lambda i, cb: (i, cb), cb=cb)) for _, cb, w in rows]
    in_specs += [pl.BlockSpec(v.shape, lambda i: (0, 0)) for v in vecs]
    out_specs = [pl.BlockSpec((t_rows, w), lambda i: (i, 0)) for w, _ in outs]
    out_specs += [pl.BlockSpec((SUBLANES, width), lambda i: (0, 0)) for _ in range(n_acc)]
    out_shape = [jax.ShapeDtypeStruct((T, w), dt) for w, dt in outs]
    out_shape += [jax.ShapeDtypeStruct((SUBLANES, width), F32) for _ in range(n_acc)]
    return _pcall(
        body, name=name, grid=(T // t_rows,), in_specs=in_specs, out_specs=out_specs, out_shape=out_shape,
        compiler_params=_params(("arbitrary",)),
    )(*[r[0] for r in rows], *vecs)


def _rms_fwd_fn(rv, vv):
    h, = rv
    w, = vv
    r = lax.rsqrt(jnp.mean(h * h, axis=-1, keepdims=True) + EPS)
    return [h * r * w], []


def _rms_bwd_fn(rv, vv):
    h, dxn, dh_in = rv
    w, = vv
    d = h.shape[-1]
    r = lax.rsqrt(jnp.mean(h * h, axis=-1, keepdims=True) + EPS)
    gy = dxn * w
    dh = r * gy - h * ((r * r * r) * (1.0 / d) * jnp.sum(gy * h, axis=-1, keepdims=True))
    return [dh_in + dh] * 2, [dxn * h * r]


def _final_fn(rv, vv):
    h, tgt = rv
    w, = vv
    d = h.shape[-1]
    r = lax.rsqrt(jnp.mean(h * h, axis=-1, keepdims=True) + EPS)
    hn = h * r
    e = hn * w - tgt
    dy = e * (1.0 / d)
    gy = dy * w
    dh = r * gy - h * ((r * r * r) * (1.0 / d) * jnp.sum(gy * h, axis=-1, keepdims=True))
    return [dh] * 2, [e * e, dy * hn]


def _onorm_fwd_fn(rv, vv):
    o, g = rv
    gain, = vv
    r = lax.rsqrt(jnp.mean(o * o, axis=-1, keepdims=True) + EPS)
    return [o * r * gain * (g * _sigmoid(g))], []


def _onorm_bwd_fn(rv, vv):
    o, g, don = rv
    gain, = vv
    d = o.shape[-1]
    r = lax.rsqrt(jnp.mean(o * o, axis=-1, keepdims=True) + EPS)
    sg = _sigmoid(g)
    sl = g * sg
    n = o * r
    dg = don * n * gain * (sg * (1.0 + g * (1.0 - sg)))
    gy = don * sl * gain
    do = r * gy - o * ((r * r * r) * (1.0 / d) * jnp.sum(gy * o, axis=-1, keepdims=True))
    return [do, dg], [don * sl * n]


HALO = SUBLANES


def _col_call(fn, cols, vecs, outs, n_acc, name, before, after, tc=LANES, chunk=128):
    T = cols[0][0].shape[0]
    chunk = min(chunk, T)
    nch = T // chunk
    ncol = outs[0][1] // tc
    n_c, n_v, n_o = len(cols), len(vecs), len(outs)
    hb = HALO if before else 0
    rw = chunk + hb + (HALO if after else 0)

    def body(*refs):
        c_refs = refs[:n_c]
        v_refs = refs[n_c:n_c + n_v]
        o_refs = refs[n_c + n_v:n_c + n_v + n_o]
        a_refs = refs[n_c + n_v + n_o:]
        vv = [v[...] for v in v_refs]
        wrow = lax.broadcasted_iota(jnp.int32, (rw, tc), 0)
        inside = (wrow >= hb) & (wrow < hb + chunk)

        def step(i, carry):
            r0 = pl.multiple_of(i * chunk, chunk)
            wins = []
            for ref in c_refs:
                parts = []
                if before:
                    pb = ref[pl.ds(pl.multiple_of(jnp.maximum(r0 - HALO, 0), HALO), HALO), :]
                    parts.append(jnp.where(i > 0, pb, 0.0))
                parts.append(ref[pl.ds(r0, chunk), :])
                if after:
                    pa = ref[pl.ds(pl.multiple_of(jnp.minimum(r0 + chunk, T - HALO), HALO), HALO), :]
                    parts.append(jnp.where(i < nch - 1, pa, 0.0))
                wins.append(jnp.concatenate(parts, axis=0) if len(parts) > 1 else parts[0])
            o_vals, a_vals = fn(wins, vv, inside)
            p = 0
            for o, (nseg, _, _) in zip(o_refs, outs):
                for s in range(nseg):
                    o[s, pl.ds(r0, chunk), :] = o_vals[p][hb:hb + chunk].astype(o.dtype)
                    p += 1
            return tuple(c + a for c, a in zip(carry, a_vals))

        taps = [v.shape[0] for v, _ in vecs][:n_acc]
        init = tuple(jnp.zeros((1, tc), F32) for k in taps for _ in range(k))
        sums = lax.fori_loop(0, nch, step, init)
        arow = lax.broadcasted_iota(jnp.int32, (SUBLANES, tc), 0)
        p = 0
        for a, k in zip(a_refs, taps):
            acc = jnp.zeros((SUBLANES, tc), F32)
            for t in range(k):
                acc = jnp.where(arow == t, sums[p], acc)
                p += 1
            a[...] = acc

    in_specs = [pl.BlockSpec((T, tc), functools.partial(lambda j, off: (0, off + j), off=off)) for _, off in cols]
    in_specs += [pl.BlockSpec((v.shape[0], tc), functools.partial(lambda j, off: (0, off + j), off=off))
                 for v, off in vecs]
    out_specs = [pl.BlockSpec((nseg, T, tc), lambda j: (0, 0, j)) for nseg, _, _ in outs]
    out_specs += [pl.BlockSpec((SUBLANES, tc), lambda j: (0, j)) for _ in range(n_acc)]
    out_shape = [jax.ShapeDtypeStruct((nseg, T, w), dt) for nseg, w, dt in outs]
    out_shape += [jax.ShapeDtypeStruct((SUBLANES, ncol * tc), F32) for _ in range(n_acc)]
    return _pcall(
        body, name=name, grid=(ncol,), in_specs=in_specs, out_specs=out_specs, out_shape=out_shape,
        compiler_params=_params(("parallel",)),
    )(*[c[0] for c in cols], *[v[0] for v in vecs])


def _down(x, k):
    return x if k == 0 else pltpu.roll(x, k, 0)


def _up(x, k):
    return x if k == 0 else pltpu.roll(x, x.shape[0] - k, 0)


def _lags(x):
    return _down(x, 2), _down(x, 1), x


def _conv(lags, w):
    return w[0:1] * lags[0] + w[1:2] * lags[1] + w[2:3] * lags[2]


def _conv_t(d, w):
    return w[2:3] * d + w[1:2] * _up(d, 1) + w[0:1] * _up(d, 2)


def _tap_sums(d, lags, inside):
    dm = jnp.where(inside, d, 0.0)
    return [jnp.sum(dm * lag, axis=0, keepdims=True) for lag in lags]


def _glu_fwd_fn(wins, vv, inside):
    xg, xv = wins
    wg, wv = vv
    ug = _conv(_lags(xg), wg)
    uv = _conv(_lags(xv), wv)
    return [ug * _sigmoid(ug) * uv], []


def _glu_bwd_fn(wins, vv, inside):
    xg, xv, da = wins
    wg, wv = vv
    lg, lv = _lags(xg), _lags(xv)
    ug = _conv(lg, wg)
    uv = _conv(lv, wv)
    sg = _sigmoid(ug)
    dug = da * uv * (sg * (1.0 + ug * (1.0 - sg)))
    duv = da * (ug * sg)
    return [_conv_t(dug, wg), _conv_t(duv, wv)], _tap_sums(dug, lg, inside) + _tap_sums(duv, lv, inside)


def _sc_fwd_fn(wins, vv, inside):
    gb, gc, hh = wins
    w, = vv
    return [gb * _conv(_lags(gc * hh), w)], []


def _sc_bwd_fn(wins, vv, inside):
    gb, gc, hh, dy = wins
    w, = vv
    lz = _lags(gc * hh)
    dcv = dy * gb
    dz = _conv_t(dcv, w)
    return [dy * _conv(lz, w), dz * hh, dz * gc], _tap_sums(dcv, lz, inside)


def _gates(qr, fr, lb):
    sg = _sigmoid(fr)
    f = lb + (1.0 - lb) * sg
    sq = _sigmoid(qr)
    q = qr * sq * (HEAD ** -0.5)
    return q, 1.0 - f, jnp.log(f), f, sg, sq


def _boundary_rows(b, g, row):
    c = b.shape[0]
    if 2 * g >= SUBLANES:
        x = b.reshape(c // (2 * g), 2 * g, LANES)
        return jnp.broadcast_to(x[:, g - 1:g, :], x.shape).reshape(c, LANES)
    x = b.reshape(c // SUBLANES, SUBLANES, LANES)
    lo = jnp.broadcast_to(x[:, 1:2, :], x.shape).reshape(c, LANES)
    hi = jnp.broadcast_to(x[:, 5:6, :], x.shape).reshape(c, LANES)
    return jnp.where((row & 4) == 0, lo, hi)


def _chunk_decays(gl, f, row):
    c = gl.shape[0]
    b = gl
    d = 1
    while d < c:
        b = b + jnp.where(row >= d, pltpu.roll(b, d, 0), 0.0)
        d *= 2
    eq, ek = [], []
    g = c // 2
    while g >= 2:
        right = (row & g) != 0
        m = _boundary_rows(b, g, row)
        z = jnp.exp(jnp.where(right, b - m, m - b))
        eq.append(jnp.where(right, z, 0.0))
        ek.append(jnp.where(right, 0.0, z))
        g //= 2
    odd = (row & 1) != 0
    eq.append(jnp.where(odd, f, 0.0))
    ek.append(jnp.where(odd, 0.0, 1.0))
    return b, eq, ek


def _intra(q, k, eq, ek, tt, ss):
    c = q.shape[0]
    qs, ks = [], []
    a = jnp.where(tt == ss, jnp.sum(q * k, axis=1, keepdims=True), 0.0)
    g = c // 2
    for e_q, e_k in zip(eq, ek):
        qg = (q * e_q).astype(BF16)
        kg = (k * e_k).astype(BF16)
        p = lax.dot_general(qg, kg, (((1,), (1,)), ((), ())), preferred_element_type=F32)
        a = a + (p if 2 * g >= c else jnp.where((tt ^ ss) < 2 * g, p, 0.0))
        qs.append(qg)
        ks.append(kg)
        g //= 2
    return a, qs, ks


def _hgrn_fwd(proj, lb, d_model):
    T = proj.shape[0]
    H = d_model // HEAD
    nch = T // CHUNK

    def body(q_ref, f_ref, v_ref, lb_ref, o_ref, s_ref):
        lbv = lb_ref[...]
        row = lax.broadcasted_iota(jnp.int32, (CHUNK, HEAD), 0)
        tt = lax.broadcasted_iota(jnp.int32, (CHUNK, CHUNK), 0)
        ss = lax.broadcasted_iota(jnp.int32, (CHUNK, CHUNK), 1)

        def step(i, st):
            sl = pl.ds(pl.multiple_of(i * CHUNK, CHUNK), CHUNK)
            q, k, gl, f, _, _ = _gates(q_ref[sl, :], f_ref[sl, :], lbv)
            v = v_ref[sl, :].astype(BF16)
            b, eq, ek = _chunk_decays(gl, f, row)
            a, _, _ = _intra(q, k, eq, ek, tt, ss)
            bl = b[CHUNK - 1:CHUNK, :]
            q0 = (q * jnp.exp(b)).astype(BF16)
            kh = (k * jnp.exp(bl - b)).astype(BF16)
            s_ref[i] = st
            o = jnp.dot(a.astype(BF16), v, preferred_element_type=F32)
            o = o + lax.dot_general(q0, st.astype(BF16), (((1,), (1,)), ((), ())), preferred_element_type=F32)
            o_ref[sl, :] = o
            return jnp.exp(bl) * st + lax.dot_general(v, kh, (((0,), (0,)), ((), ())), preferred_element_type=F32)

        per = 4 if nch % 4 == 0 else 2

        def trip(i, st):
            for u in range(per):
                st = step(per * i + u, st)
            return st

        lax.fori_loop(0, nch // per, trip, jnp.zeros((HEAD, HEAD), F32))

    col = lambda off: pl.BlockSpec((T, HEAD), functools.partial(lambda h, off: (0, off + h), off=off))
    return _pcall(
        body, name="hgrn_fwd", grid=(H,),
        in_specs=[col(0), col(H), col(2 * H), pl.BlockSpec((1, HEAD), lambda h: (0, h))],
        out_specs=[pl.BlockSpec((T, HEAD), lambda h: (0, h)),
                   pl.BlockSpec((None, nch, HEAD, HEAD), lambda h: (h, 0, 0, 0))],
        out_shape=[jax.ShapeDtypeStruct((T, d_model), F32), jax.ShapeDtypeStruct((H, nch, HEAD, HEAD), F32)],
        compiler_params=_params(("parallel",)),
    )(proj, proj, proj, lb)


def _hgrn_bwd(proj, lb, states, do, dgate, d_model):
    T = proj.shape[0]
    H = d_model // HEAD
    nch = T // CHUNK

    def body(q_ref, f_ref, v_ref, lb_ref, s_ref, do_ref, dg_ref, dp_ref, dlb_ref):
        dq_ref, df_ref, dv_ref = dp_ref.at[0], dp_ref.at[1], dp_ref.at[2]
        dp_ref[3] = dg_ref[...]
        lbv = lb_ref[...]
        row = lax.broadcasted_iota(jnp.int32, (CHUNK, HEAD), 0)
        tt = lax.broadcasted_iota(jnp.int32, (CHUNK, CHUNK), 0)
        ss = lax.broadcasted_iota(jnp.int32, (CHUNK, CHUNK), 1)
        last = row == CHUNK - 1
        nt = (((1,), (1,)), ((), ()))
        tn = (((0,), (0,)), ((), ()))

        def step(j, carry):
            dst, dlb = carry
            i = nch - 1 - j
            sl = pl.ds(pl.multiple_of(i * CHUNK, CHUNK), CHUNK)
            qr = q_ref[sl, :]
            q, k, gl, f, sg, sq = _gates(qr, f_ref[sl, :], lbv)
            v = v_ref[sl, :].astype(BF16)
            d_o = do_ref[sl, :].astype(BF16)
            st = s_ref[i]
            st16 = st.astype(BF16)
            dst16 = dst.astype(BF16)
            b, eq, ek = _chunk_decays(gl, f, row)
            a, qs, ks = _intra(q, k, eq, ek, tt, ss)
            bl = b[CHUNK - 1:CHUNK, :]
            e0 = jnp.exp(b)
            eh = jnp.exp(bl - b)
            ebl = jnp.exp(bl)
            q0 = q * e0
            kh = k * eh
            q016 = q0.astype(BF16)
            kh16 = kh.astype(BF16)
            dv = lax.dot_general(a.astype(BF16), d_o, tn, preferred_element_type=F32)
            dv = dv + lax.dot_general(kh16, dst16, nt, preferred_element_type=F32)
            dv_ref[sl, :] = dv.astype(dv_ref.dtype)
            da = lax.dot_general(d_o, v, nt, preferred_element_type=F32)
            da = jnp.where(tt >= ss, da, 0.0)
            dd = jnp.sum(jnp.where(tt == ss, da, 0.0), axis=1, keepdims=True)
            dq0 = jnp.dot(d_o, st16, preferred_element_type=F32)
            dkh = jnp.dot(v, dst16, preferred_element_type=F32)
            dq = dq0 * e0 + dd * k
            dk = dkh * eh + dd * q
            db = dq0 * q016.astype(F32) - dkh * kh16.astype(F32)
            g = CHUNK // 2
            for e_q, e_k, qg, kg in zip(eq, ek, qs, ks):
                dag = (da if 2 * g >= CHUNK else jnp.where((tt ^ ss) < 2 * g, da, 0.0)).astype(BF16)
                dqg = jnp.dot(dag, kg, preferred_element_type=F32)
                dkg = lax.dot_general(dag, qg, tn, preferred_element_type=F32)
                dq = dq + dqg * e_q
                dk = dk + dkg * e_k
                db = db + (dqg * qg.astype(F32) - dkg * kg.astype(F32))
                g //= 2
            dbl = jnp.sum(dkh * kh16.astype(F32), axis=0, keepdims=True) + ebl * jnp.sum(dst * st, axis=0, keepdims=True)
            db = db + jnp.where(last, dbl, 0.0)
            d = 1
            while d < CHUNK:
                db = db + jnp.where(row < CHUNK - d, pltpu.roll(db, CHUNK - d, 0), 0.0)
                d *= 2
            dfg = db / f - dk
            df_ref[sl, :] = (dfg * (1.0 - lbv) * sg * (1.0 - sg)).astype(df_ref.dtype)
            dq_ref[sl, :] = (dq * (HEAD ** -0.5) * (sq * (1.0 + qr * (1.0 - sq)))).astype(dq_ref.dtype)
            dlb = dlb + jnp.sum(dfg * (1.0 - sg), axis=0, keepdims=True)
            dst = ebl * dst + lax.dot_general(d_o, q016, tn, preferred_element_type=F32)
            return dst, dlb

        _, dlb = lax.fori_loop(0, nch // 2, lambda j, cr: step(2 * j + 1, step(2 * j, cr)),
                               (jnp.zeros((HEAD, HEAD), F32), jnp.zeros((1, HEAD), F32)))
        arow = lax.broadcasted_iota(jnp.int32, (SUBLANES, HEAD), 0)
        dlb_ref[...] = jnp.where(arow == 0, dlb, 0.0)

    col = lambda off: pl.BlockSpec((T, HEAD), functools.partial(lambda h, off: (0, off + h), off=off))
    return _pcall(
        body, name="hgrn_bwd", grid=(H,),
        in_specs=[col(0), col(H), col(2 * H), pl.BlockSpec((1, HEAD), lambda h: (0, h)),
                  pl.BlockSpec((None, nch, HEAD, HEAD), lambda h: (h, 0, 0, 0)), col(0), col(0)],
        out_specs=[pl.BlockSpec((4, T, HEAD), lambda h: (0, 0, h)), pl.BlockSpec((SUBLANES, HEAD), lambda h: (0, h))],
        out_shape=[jax.ShapeDtypeStruct((4, T, d_model), BF16), jax.ShapeDtypeStruct((SUBLANES, d_model), F32)],
        compiler_params=_params(("parallel",)),
    )(proj, proj, proj, lb, states, do, dgate)


def _lb_softmax(table):
    n, f = table.shape

    def body(t_ref, p_ref):
        t = t_ref[...]
        e = jnp.exp(t - jnp.max(t, axis=0, keepdims=True))
        p_ref[...] = e / jnp.sum(e, axis=0, keepdims=True)

    padded = jnp.pad(table, ((0, SUBLANES - n), (0, 0)), constant_values=-jnp.inf)
    return _pcall(body, name="lb_softmax", out_shape=jax.ShapeDtypeStruct((SUBLANES, f), F32))(padded)


def _adamw_math(w, g, m, v):
    m = ADAM_B1 * m + (1.0 - ADAM_B1) * g
    v = ADAM_B2 * v + (1.0 - ADAM_B2) * (g * g)
    m_hat = m / (1.0 - ADAM_B1 ** ADAM_STEP)
    v_hat = v / (1.0 - ADAM_B2 ** ADAM_STEP)
    delta = -ADAM_LR * (m_hat / (jnp.sqrt(v_hat) + ADAM_EPS) + ADAM_WD * w)
    return delta, m, v


def _adamw(w, g, m, v, name):
    R, C = w.shape
    tr = _pick(R, (128, 64, 32, 16, 8))

    def body(w_ref, g_ref, m_ref, v_ref, d_ref, nm_ref, nv_ref):
        d, nm, nv = _adamw_math(w_ref[...], g_ref[...], m_ref[...], v_ref[...])
        d_ref[...] = d
        nm_ref[...] = nm
        nv_ref[...] = nv

    spec = pl.BlockSpec((tr, C), lambda i: (i, 0))
    return _pcall(
        body, name=name, grid=(R // tr,), in_specs=[spec] * 4, out_specs=[spec] * 3,
        out_shape=[jax.ShapeDtypeStruct((R, C), F32)] * 3, compiler_params=_params(("parallel",)),
    )(w, g, m, v)


def _adamw_halves(w, m, v, g_mine, g_recv, c, name, layer=0, prev=None):
    C = w.shape[1]
    rh = g_mine.shape[0]
    tr = _pick(rh, (128, 64, 32, 16, 8))
    nb = rh // tr
    r0 = layer * 2 * nb

    def body(c_ref, w_ref, m_ref, v_ref, gm_ref, gr_ref, *rest):
        g_ref, d_ref, nm_ref, nv_ref = rest[-4:]
        g = jnp.where(pl.program_id(0) == c_ref[0], gm_ref[...], gr_ref[...])
        d, nm, nv = _adamw_math(w_ref[...], g, m_ref[...], v_ref[...])
        g_ref[...] = g
        d_ref[...] = d
        nm_ref[...] = nm
        nv_ref[...] = nv

    full = pl.BlockSpec((tr, C), lambda h, i, cr: (r0 + h * nb + i, 0))
    half = pl.BlockSpec((tr, C), lambda h, i, cr: (i, 0))
    in_specs = [full, full, full, half, half]
    args = [c, w, m, v, g_mine, g_recv]
    alias = {}
    if prev is not None:
        in_specs += [pl.BlockSpec(memory_space=pl.ANY)] * 4
        args += list(prev)
        alias = {6 + k: k for k in range(4)}
    return _pcall(
        body, name=name,
        grid_spec=pltpu.PrefetchScalarGridSpec(
            num_scalar_prefetch=1, grid=(2, nb), in_specs=in_specs, out_specs=[full] * 4),
        out_shape=[jax.ShapeDtypeStruct(w.shape, F32)] * 4, input_output_aliases=alias,
        compiler_params=_params(("parallel", "parallel")),
    )(*args)


def _lb_table_grad(p8, dlb, n):
    f = p8.shape[1]

    def body(p_ref, d_ref, o_ref):
        p = p_ref[...]
        d = d_ref[...]
        p0 = p[0:1, :]
        first = lax.broadcasted_iota(jnp.int32, p.shape, 0) == 0
        o_ref[...] = p * (jnp.where(first, d, 0.0) - d * p0)

    return _pcall(body, name="lb_table_grad", out_shape=jax.ShapeDtypeStruct((SUBLANES, f), F32))(p8, dlb)[:n]


def _place():
    x, y, c = lax.axis_index("x"), lax.axis_index("y"), lax.axis_index("c")
    chips = [(1 - x, y), (x, 1 - y), (1 - x, 1 - y)]
    return x, y, c, chips


HBM_SPEC = pl.BlockSpec(memory_space=pltpu.HBM)


def _gather_weights(big, small):
    nb, ns = len(big), len(small)
    n = nb + ns

    def body(*refs):
        ins, outs = refs[:n], refs[n:2 * n]
        send_sems, recv_sems, own_send, own_recv = refs[2 * n:]
        x, y, c, chips = _place()
        me = 2 * x + y
        sib = (x, y, 1 - c)
        own = [pltpu.make_async_remote_copy(
            src_ref=ins[t], dst_ref=outs[t].at[me], send_sem=own_send.at[t], recv_sem=own_recv.at[t],
            device_id=sib, device_id_type=MESH) for t in range(n)]
        for cp in own:
            cp.start()

        def half(t, h):
            rh = big[t].shape[0] // 2
            return pl.ds(pl.multiple_of(h * rh, rh), rh)

        sends = []
        for t in range(n):
            for j, chip in enumerate(chips):
                k = 6 * t + j
                if t < nb:
                    src, dst = ins[t].at[half(t, c)], outs[t].at[me, half(t, c)]
                else:
                    src, dst = ins[t], outs[t].at[me]
                sends.append(pltpu.make_async_remote_copy(
                    src_ref=src, dst_ref=dst, send_sem=send_sems.at[k], recv_sem=recv_sems.at[k],
                    device_id=(*chip, c), device_id_type=MESH))
        for cp in sends:
            cp.start()
        passed = []
        for t in range(n):
            for j, (cx, cy) in enumerate(chips):
                k = 6 * t + j
                s = 2 * cx + cy
                if t < nb:
                    landed = outs[t].at[s, half(t, c)]
                    pltpu.make_async_remote_copy(
                        src_ref=landed, dst_ref=landed, send_sem=send_sems.at[k], recv_sem=recv_sems.at[k],
                        device_id=sib, device_id_type=MESH).wait_recv()
                    fwd = pltpu.make_async_remote_copy(
                        src_ref=landed, dst_ref=landed, send_sem=send_sems.at[k + 3], recv_sem=recv_sems.at[k + 3],
                        device_id=sib, device_id_type=MESH)
                    fwd.start()
                    passed.append(fwd)
                else:
                    landed = outs[t].at[s]
                    pltpu.make_async_remote_copy(
                        src_ref=landed, dst_ref=landed, send_sem=send_sems.at[k], recv_sem=recv_sems.at[k],
                        device_id=sib, device_id_type=MESH).wait_recv()
        for t in range(nb):
            for j, (cx, cy) in enumerate(chips):
                k = 6 * t + j
                other = outs[t].at[2 * cx + cy, half(t, 1 - c)]
                pltpu.make_async_remote_copy(
                    src_ref=other, dst_ref=other, send_sem=send_sems.at[k + 3], recv_sem=recv_sems.at[k + 3],
                    device_id=sib, device_id_type=MESH).wait_recv()
        for cp in sends + passed:
            cp.wait_send()
        for cp in own:
            cp.wait()

    arrs = list(big) + list(small)
    return _pcall(
        body, name="gather_weights", in_specs=[HBM_SPEC] * n, out_specs=[HBM_SPEC] * n,
        out_shape=[jax.ShapeDtypeStruct((N_CHIPS,) + a.shape, a.dtype) for a in arrs],
        scratch_shapes=[pltpu.SemaphoreType.DMA((6 * n,)), pltpu.SemaphoreType.DMA((6 * n,)),
                        pltpu.SemaphoreType.DMA((n,)), pltpu.SemaphoreType.DMA((n,))],
    )(*arrs)


SEM_SPEC = pl.BlockSpec(memory_space=pltpu.SEMAPHORE)
DATAFLOW = pltpu.SideEffectType.DATAFLOW_SIDE_EFFECTING
COPIES_PER_SHARD = 4


def _shard_copies(ins, lands, send_sems, recv_sems):
    x, y, c, chips = _place()
    me = 2 * x + y
    cps = []
    for t in range(len(ins)):
        rh = ins[t].shape[0] // 2
        half = pl.ds(pl.multiple_of(c * rh, rh), rh)
        for j, chip in enumerate(chips):
            k = COPIES_PER_SHARD * t + j
            cps.append(pltpu.make_async_remote_copy(
                src_ref=ins[t].at[half], dst_ref=lands[t].at[me, half], send_sem=send_sems.at[k],
                recv_sem=recv_sems.at[k], device_id=(*chip, c), device_id_type=MESH))
        k = COPIES_PER_SHARD * t + 3
        cps.append(pltpu.make_async_remote_copy(
            src_ref=ins[t], dst_ref=lands[t].at[me], send_sem=send_sems.at[k], recv_sem=recv_sems.at[k],
            device_id=(x, y, 1 - c), device_id_type=MESH))
    return cps


def _gather_start(shards, thru, name):
    n = len(shards)
    nops = 2 * n + len(thru)

    def body(*refs):
        ins, lands = refs[:n], refs[n:2 * n]
        send_sems, recv_sems = refs[nops], refs[nops + 1]
        for cp in _shard_copies(ins, lands, send_sems, recv_sems):
            cp.start()

    lands = [pltpu.with_memory_space_constraint(lax.empty((N_CHIPS,) + s.shape, s.dtype), pltpu.HBM) for s in shards]
    ops = [pltpu.with_memory_space_constraint(s, pltpu.HBM) for s in shards] + lands + list(thru)
    nsem = COPIES_PER_SHARD * n
    res = _pcall(
        body, name=name, in_specs=[HBM_SPEC] * nops,
        out_specs=[SEM_SPEC, SEM_SPEC] + [HBM_SPEC] * nops,
        out_shape=[pltpu.SemaphoreType.DMA((nsem,)), pltpu.SemaphoreType.DMA((nsem,))]
        + [pltpu.HBM(o.shape, o.dtype) for o in ops],
        input_output_aliases={i: 2 + i for i in range(nops)},
        compiler_params=pltpu.CompilerParams(has_side_effects=DATAFLOW),
    )(*ops)
    return res[0], res[1], res[2:2 + n], res[2 + n:2 + 2 * n], list(res[2 + 2 * n:])


def _gather_wait(send_sems, recv_sems, shards, lands, after, name):
    n = len(shards)

    def body(*refs):
        ins, lnd = refs[:n], refs[n:2 * n]
        ssem, rsem = refs[2 * n], refs[2 * n + 1]
        for cp in _shard_copies(ins, lnd, ssem, rsem):
            cp.wait_send()
            cp.wait_recv()

    res = _pcall(
        body, name=name,
        in_specs=[HBM_SPEC] * (2 * n) + [SEM_SPEC, SEM_SPEC, pl.BlockSpec(memory_space=pl.ANY)],
        out_specs=[HBM_SPEC] * (2 * n),
        out_shape=[pltpu.HBM(o.shape, o.dtype) for o in list(shards) + list(lands)],
        input_output_aliases={i: i for i in range(2 * n)},
        compiler_params=pltpu.CompilerParams(has_side_effects=DATAFLOW),
    )(*shards, *lands, send_sems, recv_sems, after)
    return res[n:]


def _forward_copies(land, send_sems, recv_sems):
    x, y, c, chips = _place()
    rh = land.shape[1] // 2
    return [pltpu.make_async_remote_copy(
        src_ref=land.at[2 * cx + cy, pl.ds(pl.multiple_of(c * rh, rh), rh)],
        dst_ref=land.at[2 * cx + cy, pl.ds(pl.multiple_of(c * rh, rh), rh)],
        send_sem=send_sems.at[j], recv_sem=recv_sems.at[j], device_id=(x, y, 1 - c), device_id_type=MESH)
        for j, (cx, cy) in enumerate(chips)]


def _forward_start(land, thru, name):
    def body(land_ref, thru_ref, send_sems, recv_sems, out_ref, thru_out):
        for cp in _forward_copies(land_ref, send_sems, recv_sems):
            cp.start()

    return _pcall(
        body, name=name, in_specs=[HBM_SPEC, HBM_SPEC], out_specs=[SEM_SPEC, SEM_SPEC, HBM_SPEC, HBM_SPEC],
        out_shape=[pltpu.SemaphoreType.DMA((3,)), pltpu.SemaphoreType.DMA((3,)), pltpu.HBM(land.shape, land.dtype),
                   pltpu.HBM(thru.shape, thru.dtype)],
        input_output_aliases={0: 2, 1: 3}, compiler_params=pltpu.CompilerParams(has_side_effects=DATAFLOW),
    )(land, thru)


def _forward_wait(send_sems, recv_sems, land, after, name):
    def body(land_ref, ssem, rsem, after_ref, out_ref):
        for cp in _forward_copies(land_ref, ssem, rsem):
            cp.wait_send()
            cp.wait_recv()

    return _pcall(
        body, name=name, in_specs=[HBM_SPEC, SEM_SPEC, SEM_SPEC, pl.BlockSpec(memory_space=pl.ANY)],
        out_specs=HBM_SPEC, out_shape=pltpu.HBM(land.shape, land.dtype), input_output_aliases={0: 0},
        compiler_params=pltpu.CompilerParams(has_side_effects=DATAFLOW),
    )(land, send_sems, recv_sems, after)


def _sibling_copies(ins, lands, send_sems, recv_sems, other_half):
    x, y, c, _ = _place()
    return [pltpu.make_async_remote_copy(
        src_ref=ins[t].at[:, 1 - c] if other_half else ins[t], dst_ref=lands[t], send_sem=send_sems.at[t],
        recv_sem=recv_sems.at[t], device_id=(x, y, 1 - c), device_id_type=MESH) for t in range(len(ins))]


def _sibling_start(srcs, other_half, thru, name):
    n = len(srcs)
    nthru = 0 if thru is None else 1

    def body(*refs):
        ins, lands = refs[:n], refs[n:2 * n]
        send_sems, recv_sems = refs[2 * n + nthru], refs[2 * n + nthru + 1]
        for cp in _sibling_copies(ins, lands, send_sems, recv_sems, other_half):
            cp.start()

    shapes = [(s.shape[0],) + s.shape[2:] if other_half else s.shape for s in srcs]
    lands = [pltpu.with_memory_space_constraint(lax.empty(sh, s.dtype), pltpu.HBM) for sh, s in zip(shapes, srcs)]
    ops = [pltpu.with_memory_space_constraint(s, pltpu.HBM) for s in srcs] + lands + ([] if thru is None else [thru])
    res = _pcall(
        body, name=name, in_specs=[HBM_SPEC] * len(ops),
        out_specs=[SEM_SPEC, SEM_SPEC] + [HBM_SPEC] * len(ops),
        out_shape=[pltpu.SemaphoreType.DMA((n,)), pltpu.SemaphoreType.DMA((n,))]
        + [pltpu.HBM(o.shape, o.dtype) for o in ops],
        input_output_aliases={i: 2 + i for i in range(len(ops))},
        compiler_params=pltpu.CompilerParams(has_side_effects=DATAFLOW),
    )(*ops)
    return res[0], res[1], res[2:2 + n], res[2 + n:2 + 2 * n], (None if thru is None else res[2 + 2 * n])


def _sibling_wait(send_sems, recv_sems, srcs, lands, other_half, after, name):
    n = len(srcs)

    def body(*refs):
        ins, lnd = refs[:n], refs[n:2 * n]
        ssem, rsem = refs[2 * n], refs[2 * n + 1]
        for cp in _sibling_copies(ins, lnd, ssem, rsem, other_half):
            cp.wait_send()
            cp.wait_recv()

    res = _pcall(
        body, name=name,
        in_specs=[HBM_SPEC] * (2 * n) + [SEM_SPEC, SEM_SPEC, pl.BlockSpec(memory_space=pl.ANY)],
        out_specs=[HBM_SPEC] * (2 * n),
        out_shape=[pltpu.HBM(o.shape, o.dtype) for o in list(srcs) + list(lands)],
        input_output_aliases={i: i for i in range(2 * n)},
        compiler_params=pltpu.CompilerParams(has_side_effects=DATAFLOW),
    )(*srcs, *lands, send_sems, recv_sems, after)
    return res[:n], res[n:]


def _chip_copies(ins, lands, send_sems, recv_sems):
    x, y, c, chips = _place()
    cps = []
    for t in range(len(ins)):
        for j, (cx, cy) in enumerate(chips):
            cps.append(pltpu.make_async_remote_copy(
                src_ref=ins[t].at[2 * cx + cy], dst_ref=lands[t].at[j],
                send_sem=send_sems.at[3 * t + j], recv_sem=recv_sems.at[3 * t + j],
                device_id=(cx, cy, c), device_id_type=MESH))
    return cps


def _chip_start(parts, thru, name):
    n = len(parts)

    def body(*refs):
        ins, lands = refs[:n], refs[n:2 * n]
        send_sems, recv_sems = refs[2 * n + 1], refs[2 * n + 2]
        for cp in _chip_copies(ins, lands, send_sems, recv_sems):
            cp.start()

    lands = [pltpu.with_memory_space_constraint(lax.empty((3,) + p.shape[1:], p.dtype), pltpu.HBM) for p in parts]
    ops = [pltpu.with_memory_space_constraint(p, pltpu.HBM) for p in parts] + lands + [thru]
    res = _pcall(
        body, name=name, in_specs=[HBM_SPEC] * (2 * n + 1),
        out_specs=[SEM_SPEC, SEM_SPEC] + [HBM_SPEC] * (2 * n + 1),
        out_shape=[pltpu.SemaphoreType.DMA((3 * n,)), pltpu.SemaphoreType.DMA((3 * n,))]
        + [pltpu.HBM(o.shape, o.dtype) for o in ops],
        input_output_aliases={i: 2 + i for i in range(2 * n + 1)},
        compiler_params=pltpu.CompilerParams(has_side_effects=DATAFLOW),
    )(*ops)
    return res[0], res[1], res[2:2 + n], res[2 + n:2 + 2 * n], res[2 + 2 * n]


def _chip_wait(send_sems, recv_sems, parts, lands, after, name):
    n = len(parts)

    def body(*refs):
        ins, lnd = refs[:n], refs[n:2 * n]
        ssem, rsem = refs[2 * n], refs[2 * n + 1]
        for cp in _chip_copies(ins, lnd, ssem, rsem):
            cp.wait_send()
            cp.wait_recv()

    res = _pcall(
        body, name=name,
        in_specs=[HBM_SPEC] * (2 * n) + [SEM_SPEC, SEM_SPEC, pl.BlockSpec(memory_space=pl.ANY)],
        out_specs=[HBM_SPEC] * (2 * n),
        out_shape=[pltpu.HBM(o.shape, o.dtype) for o in list(parts) + list(lands)],
        input_output_aliases={i: i for i in range(2 * n)},
        compiler_params=pltpu.CompilerParams(has_side_effects=DATAFLOW),
    )(*parts, *lands, send_sems, recv_sems, after)
    return res[:n], res[n:]


def _add_pair(grad, recv, c, name):
    s, _, rh, cc = grad.shape
    tr = _pick(rh, (256, 128, 64, 32, 16))

    def body(c_ref, g_ref, r_ref, o_ref):
        o_ref[...] = (g_ref[...].astype(F32) + r_ref[...].astype(F32)).astype(o_ref.dtype)

    return _pcall(
        body, name=name,
        grid_spec=pltpu.PrefetchScalarGridSpec(
            num_scalar_prefetch=1, grid=(s, rh // tr),
            in_specs=[pl.BlockSpec((None, None, tr, cc), lambda a, i, cr: (a, cr[0], i, 0)),
                      pl.BlockSpec((None, tr, cc), lambda a, i, cr: (a, i, 0))],
            out_specs=pl.BlockSpec((None, tr, cc), lambda a, i, cr: (a, i, 0))),
        out_shape=jax.ShapeDtypeStruct((s, rh, cc), BF16),
        compiler_params=_params(("parallel", "parallel")),
    )(c, grad, recv)


def _add_chips(part, recv, me, name):
    _, rh, cc = part.shape
    tr = _pick(rh, (256, 128, 64, 32, 16))

    def body(m_ref, p_ref, r_ref, o_ref):
        o_ref[...] = ((p_ref[...].astype(F32) + r_ref[0].astype(F32)) + r_ref[1].astype(F32)) + r_ref[2].astype(F32)

    return _pcall(
        body, name=name,
        grid_spec=pltpu.PrefetchScalarGridSpec(
            num_scalar_prefetch=1, grid=(rh // tr,),
            in_specs=[pl.BlockSpec((None, tr, cc), lambda i, mr: (mr[0], i, 0)),
                      pl.BlockSpec((3, tr, cc), lambda i, mr: (0, i, 0))],
            out_specs=pl.BlockSpec((tr, cc), lambda i, mr: (i, 0))),
        out_shape=jax.ShapeDtypeStruct((rh, cc), F32),
        compiler_params=_params(("parallel",)),
    )(me, part, recv)


def _all_sum(vec):
    rows = vec.shape[0]

    def body(v_ref, o_ref, buf, send_sems, recv_sems):
        x, y, c, _ = _place()
        me = 4 * x + 2 * y + c
        buf[me] = v_ref[...]
        cps = []
        for r in range(1, 8):
            fx, fy, fc = (r >> 2) & 1, (r >> 1) & 1, r & 1
            peer = (x ^ fx, y ^ fy, c ^ fc)
            cps.append(pltpu.make_async_remote_copy(
                src_ref=v_ref, dst_ref=buf.at[me], send_sem=send_sems.at[r - 1], recv_sem=recv_sems.at[r - 1],
                device_id=peer, device_id_type=MESH))
        for cp in cps:
            cp.start()
        for r in range(1, 8):
            src = me ^ r
            pltpu.make_async_remote_copy(
                src_ref=v_ref, dst_ref=buf.at[src], send_sem=send_sems.at[r - 1], recv_sem=recv_sems.at[r - 1],
                device_id=(x, y, c), device_id_type=MESH).wait_recv()
        for cp in cps:
            cp.wait_send()
        acc = buf[0]
        for d in range(1, 8):
            acc = acc + buf[d]
        o_ref[...] = acc

    return _pcall(
        body, name="all_sum_small",
        in_specs=[pl.BlockSpec(memory_space=pltpu.VMEM)], out_specs=pl.BlockSpec(memory_space=pltpu.VMEM),
        out_shape=jax.ShapeDtypeStruct((rows, LANES), F32),
        scratch_shapes=[pltpu.VMEM((8, rows, LANES), F32), pltpu.SemaphoreType.DMA((7,)), pltpu.SemaphoreType.DMA((7,))],
    )(vec)


def _pack(parts):
    flat = jnp.concatenate([p.reshape(-1) for p in parts])
    tile = SUBLANES * LANES
    pad = (-flat.shape[0]) % tile
    return jnp.pad(flat, (0, pad)).reshape(-1, LANES)


def _unpack(vec, shapes):
    flat = vec.reshape(-1)
    out, p = [], 0
    for s in shapes:
        n = 1
        for d in s:
            n *= d
        out.append(flat[p:p + n].reshape(s))
        p += n
    return out


def _local_step(x, tgt, norm_mix, norm_ffn, lb8, out_norm, final_norm, sc_conv, ffn_conv, weights, reduce_start,
                reduce_finish):
    T, D = x.shape
    F2 = ffn_conv.shape[-1]
    FF = F2 // 2
    tm = _pick(T, (1024, 512, 256, 128))
    wide = (1536, 1408, 1024, 768, 512, 384, 256, 128)
    cw_h, cw_s, cw_u = 4 * D // N_CHIPS, 3 * D // N_CHIPS, F2 // N_CHIPS
    kp = FF // N_CHIPS
    tk_ff = kp if kp % LANES == 0 else LANES
    tn_d = _pick(D, (1024, 512, 256, 128))
    tk_w = _pick(D, (512, 256, 128))
    tn_h = _pick(cw_h, (1024, 512, 256, 128))
    tn_s = _pick(D // N_CHIPS, (512, 256, 128))
    tn_u = _pick(cw_u, wide)
    lb = lb8[0:1]
    wm_sq = _wmap_col(D, tn_d, 0)
    wm_sq1 = _wmap_col(D, D, 0)
    seg1 = lambda a: a.reshape((1,) + a.shape)

    def mix_in(h, w):
        return _row_call(_rms_fwd_fn, [(h, 0, D)], [w], [(D, BF16)], 0, "rms_fwd")[0]

    def rms_bwd(h, dxn, dh, w):
        return _row_call(_rms_bwd_fn, [(h, 0, D), (dxn, 0, D), (dh, 0, D)], [w], [(D, F32), (D, BF16)], 1, "rms_bwd")

    def ffn_fwd(h, i, fetch_up, fetch_down, next_stage=None):
        xn = mix_in(h, norm_ffn[i:i + 1])
        tn = _pick(cw_u, wide)
        w_up = fetch_up(xn)
        up = _mm_nn(xn, w_up, _wmap_col(cw_u, tn, 0), D, F2, tm, D, tn, "ffn_up")
        nb = FF // LANES
        a = _col_call(_glu_fwd_fn, [(up, 0), (up, nb)], [(ffn_conv[i], 0), (ffn_conv[i], nb)], [(1, FF, BF16)], 0,
                      "glu_fwd", before=True, after=False)[0][0]
        later = None
        if next_stage is not None:
            later, a = weights(next_stage, a)
        w_down = fetch_down(a)
        h2 = _mm_nn(a, w_down, _wmap_row(kp, tk_ff, 0), FF, D, tm, tk_ff, tn_d, "ffn_down", res=h)
        return h2, (xn, up, a), w_up, w_down, later

    def ffn_bwd(dh, dh16, h, saved, i, w_up, w_down):
        xn, up, a = saved
        g_down = _mm_tn(a, seg1(dh16), (N_CHIPS, 1, kp, D), _wmap_row(kp, tk_ff, 0), FF, D, tk_ff, tn_d,
                        "ffn_down_dw", tm=_pick(T, (2048, 1024, 512, 256, 128)))
        dh16 = reduce_start(("ffn_w_down", i), g_down, dh16)
        da = _mm_nt(seg1(dh16), w_down, _wmap_row(kp, tk_ff, 0), FF, D, tm, tk_ff, D, "ffn_down_dx")
        nb = FF // LANES
        dgv, cg, cv = _col_call(_glu_bwd_fn, [(up, 0), (up, nb), (da, 0)], [(ffn_conv[i], 0), (ffn_conv[i], nb)],
                                [(2, FF, BF16)], 2, "glu_bwd", before=True, after=True)
        g_up = _mm_tn(xn, dgv, (N_CHIPS, 1, D, cw_u), _wmap_col(cw_u, tn_u, 0), D, F2, tk_w, tn_u, "ffn_up_dw")
        dgv = reduce_start(("ffn_w_up", i), g_up, dgv)
        dxn = _mm_nt(dgv, w_up, _wmap_col(cw_u, tn_u, 0), D, F2, tm, D, tn_u, "ffn_up_dx")
        dh2, dh2_16, dnw = rms_bwd(h, dxn, dh, norm_ffn[i:i + 1])
        return dh2, reduce_finish(dh2_16), dnw, jnp.concatenate([cg[:3], cv[:3]], axis=1)

    h0 = x
    xn0 = mix_in(h0, norm_mix[0:1])
    (fetch_hin,), xn0 = weights(0, xn0)
    w_hin = fetch_hin(xn0)
    proj = _mm_nn(xn0, w_hin, _wmap_col(cw_h, tn_h, 0), D, 4 * D, tm, D, tn_h, "hgrn_in")
    o, states = _hgrn_fwd(proj, lb, D)
    (fetch_hout, fetch_up0, fetch_down0), o = weights(1, o)
    on = _row_call(_onorm_fwd_fn, [(o, 0, D), (proj, 3, D)], [out_norm], [(D, BF16)], 0, "onorm_fwd")[0]
    w_hout1 = fetch_hout(on).reshape(1, D, D)
    h1 = _mm_nn(on, w_hout1, wm_sq, D, D, tm, D, tn_d, "hgrn_out", res=h0)
    h2, ffn0, w_up0, w_down0, (fetch_sin, fetch_sout) = ffn_fwd(h1, 0, fetch_up0, fetch_down0, next_stage=2)
    xn1 = mix_in(h2, norm_mix[1:2])
    w_sin = fetch_sin(xn1)
    tn_si = _pick(cw_s, wide)
    sproj = _mm_nn(xn1, w_sin, _wmap_col(cw_s, tn_si, 0), D, 3 * D, tm, D, tn_si, "sc_in")
    (fetch_up1, fetch_down1), sproj = weights(3, sproj)
    nd = D // LANES
    ysc = _col_call(_sc_fwd_fn, [(sproj, 0), (sproj, nd), (sproj, 2 * nd)], [(sc_conv, 0)], [(1, D, BF16)], 0,
                    "sc_fwd", before=True, after=False)[0][0]
    w_sout1 = fetch_sout(ysc).reshape(1, D, D)
    h3 = _mm_nn(ysc, w_sout1, wm_sq, D, D, tm, D, tn_d, "sc_out", res=h2)
    h4, ffn1, w_up1, w_down1, _ = ffn_fwd(h3, 1, fetch_up1, fetch_down1)

    dh, dh16, esq, dfinal = _row_call(_final_fn, [(h4, 0, D), (tgt, 0, D)], [final_norm], [(D, F32), (D, BF16)], 2,
                                      "final_loss")
    loss = 0.5 / D * jnp.sum(esq)
    dh, dh16, dnf1, dconv1 = ffn_bwd(dh, dh16, h3, ffn1, 1, w_up1, w_down1)
    g_sout = _mm_tn(ysc, seg1(dh16), (1, 1, D, D), wm_sq, D, D, tk_w, tn_d, "sc_out_dw")
    dh16 = reduce_start(("sc_w_out", 0), g_sout, dh16)
    dy = _mm_nt(seg1(dh16), w_sout1, wm_sq1, D, D, tm, D, D, "sc_out_dx")
    dsp, dscc = _col_call(_sc_bwd_fn, [(sproj, 0), (sproj, nd), (sproj, 2 * nd), (dy, 0)], [(sc_conv, 0)],
                          [(3, D, BF16)], 1, "sc_bwd", before=True, after=True)
    g_sin = _mm_tn(xn1, dsp, (N_CHIPS, 1, D, cw_s), _wmap_col(cw_s, tn_s, 0), D, 3 * D, tk_w, tn_s, "sc_in_dw")
    dsp = reduce_start(("sc_w_in", 0), g_sin, dsp)
    dxn = _mm_nt(dsp, w_sin, _wmap_col(cw_s, tn_s, 0), D, 3 * D, tm, D, tn_s, "sc_in_dx", per_step=3)
    dh, dh16, dnm1 = rms_bwd(h2, dxn, dh, norm_mix[1:2])
    dh16 = reduce_finish(dh16)
    dh, dh16, dnf0, dconv0 = ffn_bwd(dh, dh16, h1, ffn0, 0, w_up0, w_down0)
    g_hout = _mm_tn(on, seg1(dh16), (1, 1, D, D), wm_sq, D, D, tk_w, tn_d, "hgrn_out_dw")
    dh16 = reduce_start(("hgrn_w_out", 0), g_hout, dh16)
    don = _mm_nt(seg1(dh16), w_hout1, wm_sq1, D, D, tm, D, D, "hgrn_out_dx")
    do, dgate, dgain = _row_call(_onorm_bwd_fn, [(o, 0, D), (proj, 3, D), (don, 0, D)], [out_norm],
                                 [(D, F32), (D, BF16)], 1, "onorm_bwd")
    dproj, dlb = _hgrn_bwd(proj, lb, states, do, dgate, D)
    g_hin = _mm_tn(xn0, dproj, (N_CHIPS, 1, D, cw_h), _wmap_col(cw_h, tn_h, 0), D, 4 * D, tk_w, tn_h, "hgrn_in_dw")
    dproj = reduce_start(("hgrn_w_in", 0), g_hin, dproj)
    dxn = _mm_nt(dproj, w_hin, _wmap_col(cw_h, tn_h, 0), D, 4 * D, tm, D, tn_h, "hgrn_in_dx", per_step=2)
    grad_x, _, dnm0 = rms_bwd(h0, dxn, dh, norm_mix[0:1])

    small = dict(
        loss=loss,
        norm_mix=jnp.stack([jnp.sum(dnm0, axis=0), jnp.sum(dnm1, axis=0)]),
        norm_ffn=jnp.stack([jnp.sum(dnf0, axis=0), jnp.sum(dnf1, axis=0)]),
        lb=dlb[0:1],
        out_norm=jnp.sum(dgain, axis=0)[None],
        final_norm=jnp.sum(dfinal, axis=0),
        sc_conv=dscc[:3],
        ffn_conv=jnp.stack([dconv0, dconv1]),
    )
    return grad_x, small


def kernel(x, norm_mix, norm_ffn, hgrn_w_in, hgrn_lb_table, hgrn_out_norm, hgrn_w_out, sc_w_in, sc_conv, sc_w_out, ffn_w_up, ffn_conv, ffn_w_down, final_norm, loss_target, m_norm_mix, m_norm_ffn, m_hgrn_w_in, m_hgrn_lb_table, m_hgrn_out_norm, m_hgrn_w_out, m_sc_w_in, m_sc_conv, m_sc_w_out, m_ffn_w_up, m_ffn_conv, m_ffn_w_down, m_final_norm, v_norm_mix, v_norm_ffn, v_hgrn_w_in, v_hgrn_lb_table, v_hgrn_out_norm, v_hgrn_w_out, v_sc_w_in, v_sc_conv, v_sc_w_out, v_ffn_w_up, v_ffn_conv, v_ffn_w_down, v_final_norm):
    D = x.shape[-1]
    xi, yi, ci = lax.axis_index("x"), lax.axis_index("y"), lax.axis_index("c")
    me_chip = (2 * xi + yi).astype(jnp.int32).reshape(1)
    me_core = ci.astype(jnp.int32).reshape(1)

    big_names = ["hgrn_w_in", "hgrn_w_out", "sc_w_in", "sc_w_out", "ffn_w_up", "ffn_w_down"]
    big_w = dict(hgrn_w_in=hgrn_w_in, hgrn_w_out=hgrn_w_out, sc_w_in=sc_w_in, sc_w_out=sc_w_out,
                 ffn_w_up=ffn_w_up, ffn_w_down=ffn_w_down)
    big_m = dict(hgrn_w_in=m_hgrn_w_in, hgrn_w_out=m_hgrn_w_out, sc_w_in=m_sc_w_in, sc_w_out=m_sc_w_out,
                 ffn_w_up=m_ffn_w_up, ffn_w_down=m_ffn_w_down)
    big_v = dict(hgrn_w_in=v_hgrn_w_in, hgrn_w_out=v_hgrn_w_out, sc_w_in=v_sc_w_in, sc_w_out=v_sc_w_out,
                 ffn_w_up=v_ffn_w_up, ffn_w_down=v_ffn_w_down)
    flat2 = lambda a: a.reshape(-1, a.shape[-1])

    sh = lambda a: a.reshape(-1, a.shape[-1]).astype(BF16)
    conv_shards = [flat2(sc_conv), flat2(ffn_conv)]
    stages = [[sh(hgrn_w_in)], [sh(hgrn_w_out), sh(ffn_w_up[0]), sh(ffn_w_down[0])],
              [sh(sc_w_in), sh(sc_w_out)], [sh(ffn_w_up[1]), sh(ffn_w_down[1])]]
    scc4, fcc4 = _gather_weights([], [flat2(sc_conv), flat2(ffn_conv)])
    gathers = []
    thru = [scc4, norm_mix]
    for k, shards in enumerate(stages):
        ss, rs, src, land, thru = _gather_start(shards, thru, "gather_start_%d" % k)
        gathers.append((ss, rs, src, land))
    scc4, norm_mix = thru
    scc = jnp.moveaxis(scc4, 0, 1).reshape(3, D)
    f2 = ffn_conv.shape[-1] * N_CHIPS
    fcc = jnp.moveaxis(fcc4.reshape(N_CHIPS, 2, 3, -1), 0, 2).reshape(2, 3, f2)

    def weights(stage, after):
        lands = _gather_wait(*gathers[stage], after, "gather_wait_%d" % stage)
        fetch = []
        for t, land in enumerate(lands):
            ss, rs, land, after = _forward_start(land, after, "gather_forward_start_%d_%d" % (stage, t))
            fetch.append(functools.partial(_forward_wait, ss, rs, land, name="gather_forward_wait_%d_%d" % (stage, t)))
        return fetch, after

    pending = []
    started = []

    def reduce_start(slot, grad, thru):
        t = sum(len(b[0]) for b in pending) + len(started)
        halves = grad.reshape(N_CHIPS, 2, -1, grad.shape[-1])
        ss, rs, src, land, thru = _sibling_start([halves], True, thru, "grad_pair_start_%d" % t)
        started.append((slot, t, ss, rs, src, land))
        return thru

    def reduce_finish(thru):
        k = len(pending)
        pair = []
        for slot, t, ss, rs, src, land in started:
            src, recv = _sibling_wait(ss, rs, src, land, True, thru, "grad_pair_wait_%d" % t)
            pair.append(_add_pair(src[0], recv[0], me_core, "grad_add_pair"))
        ss, rs, pair, land, thru = _chip_start(pair, thru, "grad_chip_start_%d" % k)
        pending.append(([s[0] for s in started], ss, rs, pair, land))
        started.clear()
        return thru

    lb8 = _lb_softmax(hgrn_lb_table)
    grad_x, small = _local_step(
        x[0], loss_target[0], norm_mix, norm_ffn, lb8, hgrn_out_norm, final_norm[None], scc, fcc, weights,
        reduce_start, reduce_finish)

    small_names = ["loss", "norm_mix", "norm_ffn", "lb", "out_norm", "final_norm", "sc_conv", "ffn_conv"]
    parts = [small[n].astype(F32) for n in small_names]
    shapes = [p.shape for p in parts]
    tot = dict(zip(small_names, _unpack(reduce_finish(_all_sum(_pack(parts))), shapes)))
    loss = tot["loss"].reshape(())
    g_lb_table = _lb_table_grad(lb8, tot["lb"], hgrn_lb_table.shape[0])
    cw = sc_conv.shape[-1]
    g_sc_conv = lax.dynamic_slice_in_dim(tot["sc_conv"], me_chip[0] * cw, cw, axis=1)[None]
    cf = ffn_conv.shape[-1]
    g_ffn_conv = lax.dynamic_slice_in_dim(tot["ffn_conv"], me_chip[0] * cf, cf, axis=2)
    g_small = dict(norm_mix=tot["norm_mix"], norm_ffn=tot["norm_ffn"], hgrn_lb_table=g_lb_table,
                   hgrn_out_norm=tot["out_norm"], sc_conv=g_sc_conv, ffn_conv=g_ffn_conv, final_norm=tot["final_norm"])
    w_small = dict(norm_mix=norm_mix, norm_ffn=norm_ffn, hgrn_lb_table=hgrn_lb_table, hgrn_out_norm=hgrn_out_norm,
                   sc_conv=sc_conv, ffn_conv=ffn_conv, final_norm=final_norm)
    m_small = dict(norm_mix=m_norm_mix, norm_ffn=m_norm_ffn, hgrn_lb_table=m_hgrn_lb_table, hgrn_out_norm=m_hgrn_out_norm,
                   sc_conv=m_sc_conv, ffn_conv=m_ffn_conv, final_norm=m_final_norm)
    v_small = dict(norm_mix=v_norm_mix, norm_ffn=v_norm_ffn, hgrn_lb_table=v_hgrn_lb_table, hgrn_out_norm=v_hgrn_out_norm,
                   sc_conv=v_sc_conv, ffn_conv=v_ffn_conv, final_norm=v_final_norm)
    sm_names = list(g_small)
    sm_shapes = [w_small[n].shape for n in sm_names]
    d_s, m_s, v_s = _adamw(_pack([w_small[n] for n in sm_names]), _pack([g_small[n] for n in sm_names]),
                           _pack([m_small[n] for n in sm_names]), _pack([v_small[n] for n in sm_names]), "adamw_small")
    out_g, out_d, out_m, out_v = dict(g_small), {}, {}, {}
    for n, d_, m_, v_ in zip(sm_names, _unpack(d_s, sm_shapes), _unpack(m_s, sm_shapes), _unpack(v_s, sm_shapes)):
        out_d[n], out_m[n], out_v[n] = d_, m_, v_

    done = {}
    after = grad_x

    def add_and_share(k, after):
        slots, ss, rs, pair, land = pending[k]
        pair, recv = _chip_wait(ss, rs, pair, land, after, "grad_chip_wait_%d" % k)
        mine = [_add_chips(p, r, me_chip, "grad_add_chips") for p, r in zip(pair, recv)]
        ss, rs, mine, land, _ = _sibling_start(mine, False, None, "grad_share_start_%d" % k)
        return slots, ss, rs, mine, land

    def update(k, share, after):
        slots, ss, rs, mine, land = share
        mine, theirs = _sibling_wait(ss, rs, mine, land, False, after, "grad_share_wait_%d" % k)
        for (n, layer), gm, gr in zip(slots, mine, theirs):
            done[n] = _adamw_halves(flat2(big_w[n]), flat2(big_m[n]), flat2(big_v[n]), gm, gr, me_core, "adamw_" + n,
                                    layer=layer, prev=done.get(n))
        return done[slots[-1][0]][0]

    shares = []
    for k in range(len(pending) - 1):
        shares.append(add_and_share(k, after))
        after = shares[-1][3][0]
    for k, share in enumerate(shares):
        after = update(k, share, after)
    last = len(pending) - 1
    share = add_and_share(last, after)
    update(last, share, share[3][0])
    for n in big_names:
        out_g[n], out_d[n], out_m[n], out_v[n] = (a.reshape(big_w[n].shape) for a in done[n])

    order = ["norm_mix", "norm_ffn", "hgrn_w_in", "hgrn_lb_table", "hgrn_out_norm", "hgrn_w_out", "sc_w_in", "sc_conv",
             "sc_w_out", "ffn_w_up", "ffn_conv", "ffn_w_down", "final_norm"]
    return (loss, grad_x[None], *[out_g[n] for n in order], *[out_d[n] for n in order],
            *[out_m[n] for n in order], *[out_v[n] for n in order])
```

```python
import functools

import jax
import jax.numpy as jnp
from jax import lax
from jax.experimental import pallas as pl
from jax.experimental.pallas import tpu as pltpu

F32 = jnp.float32
BF16 = jnp.bfloat16
MESH = pl.DeviceIdType.MESH

EPS = 1e-6
CHUNK = 64
HEAD = 128
N_CHIPS = 4
ADAM_LR, ADAM_B1, ADAM_B2, ADAM_EPS, ADAM_WD, ADAM_STEP = 0.001, 0.9, 0.999, 1e-08, 0.01, 10
VMEM_LIMIT = 56 * 1024 * 1024
SUBLANES = 8
LANES = 128


def _pcall(body, **kw):
    return pl.pallas_call(body, **kw)


def _params(sem, vmem=VMEM_LIMIT):
    return pltpu.CompilerParams(dimension_semantics=sem, vmem_limit_bytes=vmem)


def _pick(dim, prefs):
    for p in prefs:
        if p <= dim and dim % p == 0:
            return p
    return dim


def _sigmoid(x):
    return 1.0 / (1.0 + jnp.exp(-x))


def _wmap_col(cw, tn, r0):
    bps = cw // tn
    return lambda kb, nb: (nb // bps, r0 + kb, nb % bps)


def _wmap_row(kp, tk, r0):
    bps = kp // tk
    return lambda kb, nb: (kb // bps, r0 + kb % bps, nb)


def _mm_nn(a, w3, wmap, K, N, tm, tk, tn, name, res=None):
    M = a.shape[0]
    nk = K // tk

    def body(*refs):
        if res is None:
            a_ref, w_ref, o_ref = refs[:3]
        else:
            a_ref, w_ref, r_ref, o_ref = refs[:4]
        p = jnp.dot(a_ref[...], w_ref[...], preferred_element_type=F32)
        if nk == 1:
            o_ref[...] = p if res is None else p + r_ref[...]
            return
        acc = refs[-1]
        k = pl.program_id(2)

        @pl.when(k == 0)
        def _():
            acc[...] = p

        @pl.when(k > 0)
        def _():
            acc[...] += p

        @pl.when(k == nk - 1)
        def _():
            o_ref[...] = acc[...] if res is None else acc[...] + r_ref[...]

    if nk == 1:
        grid = (M // tm, N // tn)
        ix = lambda f: (lambda i, j: f(i, j, 0))
        sem = ("parallel", "parallel")
        scratch = []
    else:
        grid = (M // tm, N // tn, nk)
        ix = lambda f: f
        sem = ("parallel", "parallel", "arbitrary")
        scratch = [pltpu.VMEM((tm, tn), F32)]
    in_specs = [pl.BlockSpec((tm, tk), ix(lambda i, j, k: (i, k))),
                pl.BlockSpec((None, tk, tn), ix(lambda i, j, k: wmap(k, j)))]
    args = [a, w3]
    if res is not None:
        in_specs.append(pl.BlockSpec((tm, tn), ix(lambda i, j, k: (i, j))))
        args.append(res)
    return _pcall(
        body, name=name, grid=grid, in_specs=in_specs,
        out_specs=pl.BlockSpec((tm, tn), ix(lambda i, j, k: (i, j))),
        out_shape=jax.ShapeDtypeStruct((M, N), F32), scratch_shapes=scratch, compiler_params=_params(sem),
    )(*args)


def _mm_nn_shard(a, w3, s, tm, tn, name, prev=None):
    M, K = a.shape
    S, _, cw = w3.shape
    bps = cw // tn

    def body(s_ref, a_ref, w_ref, *rest):
        o_ref = rest[-1]
        o_ref[...] = jnp.dot(a_ref[...], w_ref[...], preferred_element_type=F32)

    in_specs = [pl.BlockSpec((tm, K), lambda i, j, sr: (i, 0)),
                pl.BlockSpec((None, K, tn), lambda i, j, sr: (sr[0], 0, j))]
    args = [s, a, w3]
    alias = {}
    if prev is not None:
        in_specs.append(pl.BlockSpec(memory_space=pl.ANY))
        args.append(prev)
        alias = {3: 0}
    return _pcall(
        body, name=name,
        grid_spec=pltpu.PrefetchScalarGridSpec(
            num_scalar_prefetch=1, grid=(M // tm, bps), in_specs=in_specs,
            out_specs=pl.BlockSpec((tm, tn), lambda i, j, sr: (i, sr[0] * bps + j))),
        out_shape=jax.ShapeDtypeStruct((M, S * cw), F32), input_output_aliases=alias,
        compiler_params=_params(("parallel", "parallel")),
    )(*args)


def _mm_nt(dy3, w3, wmap, K, N, tm, tk, tn, name, per_step=1):
    M = dy3.shape[1]
    bps = dy3.shape[2] // tn
    u = per_step
    grid = (M // tm, K // tk, N // (tn * u))
    nn = grid[2]

    def body(*refs):
        o_ref = refs[-1]
        p = None
        for r in range(u):
            d = lax.dot_general(refs[r][...], refs[u + r][...], (((1,), (1,)), ((), ())), preferred_element_type=F32)
            p = d if p is None else p + d
        if nn == 1:
            o_ref[...] = p
            return
        n = pl.program_id(2)

        @pl.when(n == 0)
        def _():
            o_ref[...] = p

        @pl.when(n > 0)
        def _():
            o_ref[...] += p

    def dy_spec(r):
        return pl.BlockSpec((None, tm, tn), lambda i, j, n: ((n * u + r) // bps, i, (n * u + r) % bps))

    def w_spec(r):
        return pl.BlockSpec((None, tk, tn), lambda i, j, n: wmap(j, n * u + r))

    return _pcall(
        body, name=name, grid=grid,
        in_specs=[dy_spec(r) for r in range(u)] + [w_spec(r) for r in range(u)],
        out_specs=pl.BlockSpec((tm, tk), lambda i, j, n: (i, j)),
        out_shape=jax.ShapeDtypeStruct((M, K), F32),
        compiler_params=_params(("parallel", "parallel", "arbitrary")),
    )(*([dy3] * u), *([w3] * u))


def _mm_tn(x, dy3, shape4, wmap, K, N, tk, tn, name, tm=None):
    M = x.shape[0]
    tm = M if tm is None else tm
    nm = M // tm
    bps = dy3.shape[2] // tn

    def body(*refs):
        x_ref, dy_ref = refs[:2]
        p = lax.dot_general(x_ref[...], dy_ref[...], (((0,), (0,)), ((), ())), preferred_element_type=F32)
        if nm == 1:
            o_ref = refs[-1]
            o_ref[...] = p.astype(o_ref.dtype)
            return
        o_ref, acc = refs[-2:]
        m = pl.program_id(2)

        @pl.when(m == 0)
        def _():
            acc[...] = p

        @pl.when(m > 0)
        def _():
            acc[...] += p

        @pl.when(m == nm - 1)
        def _():
            o_ref[...] = acc[...].astype(o_ref.dtype)

    def omap(i, j, m):
        s, rb, cb = wmap(i, j)
        return (s, 0, rb, cb)

    return _pcall(
        body, name=name, grid=(K // tk, N // tn, nm),
        in_specs=[pl.BlockSpec((tm, tk), lambda i, j, m: (m, i)),
                  pl.BlockSpec((None, tm, tn), lambda i, j, m: (j // bps, m, j % bps))],
        out_specs=pl.BlockSpec((None, None, tk, tn), omap),
        out_shape=jax.ShapeDtypeStruct(shape4, BF16),
        scratch_shapes=[] if nm == 1 else [pltpu.VMEM((tk, tn), F32)],
        compiler_params=_params(("parallel", "parallel", "arbitrary")),
    )(x, dy3)


def _row_call(fn, rows, vecs, outs, n_acc, name, t_rows=256, sub=16, per_trip=4):
    T = rows[0][0].shape[0]
    t_rows = min(t_rows, T)
    nsub = t_rows // sub
    n_r, n_v, n_o = len(rows), len(vecs), len(outs)
    width = rows[0][2]

    def body(*refs):
        r_refs = refs[:n_r]
        v_refs = refs[n_r:n_r + n_v]
        o_refs = refs[n_r + n_v:n_r + n_v + n_o]
        a_refs = refs[n_r + n_v + n_o:]

        @pl.when(pl.program_id(0) == 0)
        def _():
            for a in a_refs:
                a[...] = jnp.zeros_like(a)

        vv = [v[...] for v in v_refs]

        def step(i, carry):
            done = []
            for u in range(per_trip):
                sl = pl.ds(pl.multiple_of((i * per_trip + u) * sub, sub), sub)
                done.append((sl,) + tuple(fn([r[sl, :] for r in r_refs], vv)))
            for sl, o_vals, a_vals in done:
                for o, val in zip(o_refs, o_vals):
                    o[sl, :] = val.astype(o.dtype)
            for a_i, a in enumerate(a_refs):
                tot = None
                for _, _, a_vals in done:
                    part = a_vals[a_i].reshape(sub // SUBLANES, SUBLANES, a_vals[a_i].shape[-1]).sum(axis=0)
                    tot = part if tot is None else tot + part
                a[...] += tot
            return carry

        lax.fori_loop(0, nsub // per_trip, step, 0)

    in_specs = [pl.BlockSpec((t_rows, w), functools.partial(lambda i, cb: (i, cb), cb=cb)) for _, cb, w in rows]
    in_specs += [pl.BlockSpec(v.shape, lambda i: (0, 0)) for v in vecs]
    out_specs = [pl.BlockSpec((t_rows, w), lambda i: (i, 0)) for w, _ in outs]
    out_specs += [pl.BlockSpec((SUBLANES, width), lambda i: (0, 0)) for _ in range(n_acc)]
    out_shape = [jax.ShapeDtypeStruct((T, w), dt) for w, dt in outs]
    out_shape += [jax.ShapeDtypeStruct((SUBLANES, width), F32) for _ in range(n_acc)]
    return _pcall(
        body, name=name, grid=(T // t_rows,), in_specs=in_specs, out_specs=out_specs, out_shape=out_shape,
        compiler_params=_params(("arbitrary",)),
    )(*[r[0] for r in rows], *vecs)


def _rms_fwd_fn(rv, vv):
    h, = rv
    w, = vv
    r = lax.rsqrt(jnp.mean(h * h, axis=-1, keepdims=True) + EPS)
    return [h * r * w], []


def _rms_bwd_fn(rv, vv):
    h, dxn, dh_in = rv
    w, = vv
    d = h.shape[-1]
    r = lax.rsqrt(jnp.mean(h * h, axis=-1, keepdims=True) + EPS)
    gy = dxn * w
    dh = r * gy - h * ((r * r * r) * (1.0 / d) * jnp.sum(gy * h, axis=-1, keepdims=True))
    return [dh_in + dh] * 2, [dxn * h * r]


def _final_fn(rv, vv):
    h, tgt = rv
    w, = vv
    d = h.shape[-1]
    r = lax.rsqrt(jnp.mean(h * h, axis=-1, keepdims=True) + EPS)
    hn = h * r
    e = hn * w - tgt
    dy = e * (1.0 / d)
    gy = dy * w
    dh = r * gy - h * ((r * r * r) * (1.0 / d) * jnp.sum(gy * h, axis=-1, keepdims=True))
    return [dh] * 2, [e * e, dy * hn]


def _onorm_fwd_fn(rv, vv):
    o, g = rv
    gain, = vv
    r = lax.rsqrt(jnp.mean(o * o, axis=-1, keepdims=True) + EPS)
    return [o * r * gain * (g * _sigmoid(g))], []


def _onorm_bwd_fn(rv, vv):
    o, g, don = rv
    gain, = vv
    d = o.shape[-1]
    r = lax.rsqrt(jnp.mean(o * o, axis=-1, keepdims=True) + EPS)
    sg = _sigmoid(g)
    sl = g * sg
    n = o * r
    dg = don * n * gain * (sg * (1.0 + g * (1.0 - sg)))
    gy = don * sl * gain
    do = r * gy - o * ((r * r * r) * (1.0 / d) * jnp.sum(gy * o, axis=-1, keepdims=True))
    return [do, dg], [don * sl * n]


HALO = SUBLANES


def _col_call(fn, cols, vecs, outs, n_acc, name, before, after, tc=LANES, chunk=128):
    T = cols[0][0].shape[0]
    chunk = min(chunk, T)
    nch = T // chunk
    ncol = outs[0][1] // tc
    n_c, n_v, n_o = len(cols), len(vecs), len(outs)
    hb = HALO if before else 0
    rw = chunk + hb + (HALO if after else 0)

    def body(*refs):
        c_refs = refs[:n_c]
        v_refs = refs[n_c:n_c + n_v]
        o_refs = refs[n_c + n_v:n_c + n_v + n_o]
        a_refs = refs[n_c + n_v + n_o:]
        vv = [v[...] for v in v_refs]
        wrow = lax.broadcasted_iota(jnp.int32, (rw, tc), 0)
        inside = (wrow >= hb) & (wrow < hb + chunk)

        def step(i, carry):
            r0 = pl.multiple_of(i * chunk, chunk)
            wins = []
            for ref in c_refs:
                parts = []
                if before:
                    pb = ref[pl.ds(pl.multiple_of(jnp.maximum(r0 - HALO, 0), HALO), HALO), :]
                    parts.append(jnp.where(i > 0, pb, 0.0))
                parts.append(ref[pl.ds(r0, chunk), :])
                if after:
                    pa = ref[pl.ds(pl.multiple_of(jnp.minimum(r0 + chunk, T - HALO), HALO), HALO), :]
                    parts.append(jnp.where(i < nch - 1, pa, 0.0))
                wins.append(jnp.concatenate(parts, axis=0) if len(parts) > 1 else parts[0])
            o_vals, a_vals = fn(wins, vv, inside)
            p = 0
            for o, (nseg, _, _) in zip(o_refs, outs):
                for s in range(nseg):
                    o[s, pl.ds(r0, chunk), :] = o_vals[p][hb:hb + chunk].astype(o.dtype)
                    p += 1
            return tuple(c + a for c, a in zip(carry, a_vals))

        taps = [v.shape[0] for v, _ in vecs][:n_acc]
        init = tuple(jnp.zeros((1, tc), F32) for k in taps for _ in range(k))
        sums = lax.fori_loop(0, nch, step, init)
        arow = lax.broadcasted_iota(jnp.int32, (SUBLANES, tc), 0)
        p = 0
        for a, k in zip(a_refs, taps):
            acc = jnp.zeros((SUBLANES, tc), F32)
            for t in range(k):
                acc = jnp.where(arow == t, sums[p], acc)
                p += 1
            a[...] = acc

    in_specs = [pl.BlockSpec((T, tc), functools.partial(lambda j, off: (0, off + j), off=off)) for _, off in cols]
    in_specs += [pl.BlockSpec((v.shape[0], tc), functools.partial(lambda j, off: (0, off + j), off=off))
                 for v, off in vecs]
    out_specs = [pl.BlockSpec((nseg, T, tc), lambda j: (0, 0, j)) for nseg, _, _ in outs]
    out_specs += [pl.BlockSpec((SUBLANES, tc), lambda j: (0, j)) for _ in range(n_acc)]
    out_shape = [jax.ShapeDtypeStruct((nseg, T, w), dt) for nseg, w, dt in outs]
    out_shape += [jax.ShapeDtypeStruct((SUBLANES, ncol * tc), F32) for _ in range(n_acc)]
    return _pcall(
        body, name=name, grid=(ncol,), in_specs=in_specs, out_specs=out_specs, out_shape=out_shape,
        compiler_params=_params(("parallel",)),
    )(*[c[0] for c in cols], *[v[0] for v in vecs])


def _down(x, k):
    return x if k == 0 else pltpu.roll(x, k, 0)


def _up(x, k):
    return x if k == 0 else pltpu.roll(x, x.shape[0] - k, 0)


def _lags(x):
    return _down(x, 2), _down(x, 1), x


def _conv(lags, w):
    return w[0:1] * lags[0] + w[1:2] * lags[1] + w[2:3] * lags[2]


def _conv_t(d, w):
    return w[2:3] * d + w[1:2] * _up(d, 1) + w[0:1] * _up(d, 2)


def _tap_sums(d, lags, inside):
    dm = jnp.where(inside, d, 0.0)
    return [jnp.sum(dm * lag, axis=0, keepdims=True) for lag in lags]


def _glu_fwd_fn(wins, vv, inside):
    xg, xv = wins
    wg, wv = vv
    ug = _conv(_lags(xg), wg)
    uv = _conv(_lags(xv), wv)
    return [ug * _sigmoid(ug) * uv], []


def _glu_bwd_fn(wins, vv, inside):
    xg, xv, da = wins
    wg, wv = vv
    lg, lv = _lags(xg), _lags(xv)
    ug = _conv(lg, wg)
    uv = _conv(lv, wv)
    sg = _sigmoid(ug)
    dug = da * uv * (sg * (1.0 + ug * (1.0 - sg)))
    duv = da * (ug * sg)
    return [_conv_t(dug, wg), _conv_t(duv, wv)], _tap_sums(dug, lg, inside) + _tap_sums(duv, lv, inside)


def _sc_fwd_fn(wins, vv, inside):
    gb, gc, hh = wins
    w, = vv
    return [gb * _conv(_lags(gc * hh), w)], []


def _sc_bwd_fn(wins, vv, inside):
    gb, gc, hh, dy = wins
    w, = vv
    lz = _lags(gc * hh)
    dcv = dy * gb
    dz = _conv_t(dcv, w)
    return [dy * _conv(lz, w), dz * hh, dz * gc], _tap_sums(dcv, lz, inside)


def _gates(qr, fr, lb):
    sg = _sigmoid(fr)
    f = lb + (1.0 - lb) * sg
    sq = _sigmoid(qr)
    q = qr * sq * (HEAD ** -0.5)
    return q, 1.0 - f, jnp.log(f), f, sg, sq


def _boundary_rows(b, g, row):
    c = b.shape[0]
    if 2 * g >= SUBLANES:
        x = b.reshape(c // (2 * g), 2 * g, LANES)
        return jnp.broadcast_to(x[:, g - 1:g, :], x.shape).reshape(c, LANES)
    x = b.reshape(c // SUBLANES, SUBLANES, LANES)
    lo = jnp.broadcast_to(x[:, 1:2, :], x.shape).reshape(c, LANES)
    hi = jnp.broadcast_to(x[:, 5:6, :], x.shape).reshape(c, LANES)
    return jnp.where((row & 4) == 0, lo, hi)


def _chunk_decays(gl, f, row):
    c = gl.shape[0]
    b = gl
    d = 1
    while d < c:
        b = b + jnp.where(row >= d, pltpu.roll(b, d, 0), 0.0)
        d *= 2
    eq, ek = [], []
    g = c // 2
    while g >= 2:
        right = (row & g) != 0
        m = _boundary_rows(b, g, row)
        z = jnp.exp(jnp.where(right, b - m, m - b))
        eq.append(jnp.where(right, z, 0.0))
        ek.append(jnp.where(right, 0.0, z))
        g //= 2
    odd = (row & 1) != 0
    eq.append(jnp.where(odd, f, 0.0))
    ek.append(jnp.where(odd, 0.0, 1.0))
    return b, eq, ek


def _intra(q, k, eq, ek, tt, ss):
    c = q.shape[0]
    qs, ks = [], []
    a = jnp.where(tt == ss, jnp.sum(q * k, axis=1, keepdims=True), 0.0)
    g = c // 2
    for e_q, e_k in zip(eq, ek):
        qg = (q * e_q).astype(BF16)
        kg = (k * e_k).astype(BF16)
        p = lax.dot_general(qg, kg, (((1,), (1,)), ((), ())), preferred_element_type=F32)
        a = a + (p if 2 * g >= c else jnp.where((tt ^ ss) < 2 * g, p, 0.0))
        qs.append(qg)
        ks.append(kg)
        g //= 2
    return a, qs, ks


def _hgrn_fwd(proj, lb, d_model):
    T = proj.shape[0]
    H = d_model // HEAD
    nch = T // CHUNK

    def body(q_ref, f_ref, v_ref, lb_ref, o_ref, s_ref):
        lbv = lb_ref[...]
        row = lax.broadcasted_iota(jnp.int32, (CHUNK, HEAD), 0)
        tt = lax.broadcasted_iota(jnp.int32, (CHUNK, CHUNK), 0)
        ss = lax.broadcasted_iota(jnp.int32, (CHUNK, CHUNK), 1)

        def step(i, st):
            sl = pl.ds(pl.multiple_of(i * CHUNK, CHUNK), CHUNK)
            q, k, gl, f, _, _ = _gates(q_ref[sl, :], f_ref[sl, :], lbv)
            v = v_ref[sl, :].astype(BF16)
            b, eq, ek = _chunk_decays(gl, f, row)
            a, _, _ = _intra(q, k, eq, ek, tt, ss)
            bl = b[CHUNK - 1:CHUNK, :]
            q0 = (q * jnp.exp(b)).astype(BF16)
            kh = (k * jnp.exp(bl - b)).astype(BF16)
            s_ref[i] = st
            o = jnp.dot(a.astype(BF16), v, preferred_element_type=F32)
            o = o + lax.dot_general(q0, st.astype(BF16), (((1,), (1,)), ((), ())), preferred_element_type=F32)
            o_ref[sl, :] = o
            return jnp.exp(bl) * st + lax.dot_general(v, kh, (((0,), (0,)), ((), ())), preferred_element_type=F32)

        per = 4 if nch % 4 == 0 else 2

        def trip(i, st):
            for u in range(per):
                st = step(per * i + u, st)
            return st

        lax.fori_loop(0, nch // per, trip, jnp.zeros((HEAD, HEAD), F32))

    col = lambda off: pl.BlockSpec((T, HEAD), functools.partial(lambda h, off: (0, off + h), off=off))
    return _pcall(
        body, name="hgrn_fwd", grid=(H,),
        in_specs=[col(0), col(H), col(2 * H), pl.BlockSpec((1, HEAD), lambda h: (0, h))],
        out_specs=[pl.BlockSpec((T, HEAD), lambda h: (0, h)),
                   pl.BlockSpec((None, nch, HEAD, HEAD), lambda h: (h, 0, 0, 0))],
        out_shape=[jax.ShapeDtypeStruct((T, d_model), F32), jax.ShapeDtypeStruct((H, nch, HEAD, HEAD), F32)],
        compiler_params=_params(("parallel",)),
    )(proj, proj, proj, lb)


def _hgrn_bwd(proj, lb, states, do, dgate, d_model):
    T = proj.shape[0]
    H = d_model // HEAD
    nch = T // CHUNK

    def body(q_ref, f_ref, v_ref, lb_ref, s_ref, do_ref, dg_ref, dp_ref, dlb_ref):
        dq_ref, df_ref, dv_ref = dp_ref.at[0], dp_ref.at[1], dp_ref.at[2]
        dp_ref[3] = dg_ref[...]
        lbv = lb_ref[...]
        row = lax.broadcasted_iota(jnp.int32, (CHUNK, HEAD), 0)
        tt = lax.broadcasted_iota(jnp.int32, (CHUNK, CHUNK), 0)
        ss = lax.broadcasted_iota(jnp.int32, (CHUNK, CHUNK), 1)
        last = row == CHUNK - 1
        nt = (((1,), (1,)), ((), ()))
        tn = (((0,), (0,)), ((), ()))

        def step(j, carry):
            dst, dlb = carry
            i = nch - 1 - j
            sl = pl.ds(pl.multiple_of(i * CHUNK, CHUNK), CHUNK)
            qr = q_ref[sl, :]
            q, k, gl, f, sg, sq = _gates(qr, f_ref[sl, :], lbv)
            v = v_ref[sl, :].astype(BF16)
            d_o = do_ref[sl, :].astype(BF16)
            st = s_ref[i]
            st16 = st.astype(BF16)
            dst16 = dst.astype(BF16)
            b, eq, ek = _chunk_decays(gl, f, row)
            a, qs, ks = _intra(q, k, eq, ek, tt, ss)
            bl = b[CHUNK - 1:CHUNK, :]
            e0 = jnp.exp(b)
            eh = jnp.exp(bl - b)
            ebl = jnp.exp(bl)
            q0 = q * e0
            kh = k * eh
            q016 = q0.astype(BF16)
            kh16 = kh.astype(BF16)
            dv = lax.dot_general(a.astype(BF16), d_o, tn, preferred_element_type=F32)
            dv = dv + lax.dot_general(kh16, dst16, nt, preferred_element_type=F32)
            dv_ref[sl, :] = dv.astype(dv_ref.dtype)
            da = lax.dot_general(d_o, v, nt, preferred_element_type=F32)
            da = jnp.where(tt >= ss, da, 0.0)
            dd = jnp.sum(jnp.where(tt == ss, da, 0.0), axis=1, keepdims=True)
            dq0 = jnp.dot(d_o, st16, preferred_element_type=F32)
            dkh = jnp.dot(v, dst16, preferred_element_type=F32)
            dq = dq0 * e0 + dd * k
            dk = dkh * eh + dd * q
            db = dq0 * q016.astype(F32) - dkh * kh16.astype(F32)
            g = CHUNK // 2
            for e_q, e_k, qg, kg in zip(eq, ek, qs, ks):
                dag = (da if 2 * g >= CHUNK else jnp.where((tt ^ ss) < 2 * g, da, 0.0)).astype(BF16)
                dqg = jnp.dot(dag, kg, preferred_element_type=F32)
                dkg = lax.dot_general(dag, qg, tn, preferred_element_type=F32)
                dq = dq + dqg * e_q
                dk = dk + dkg * e_k
                db = db + (dqg * qg.astype(F32) - dkg * kg.astype(F32))
                g //= 2
            dbl = jnp.sum(dkh * kh16.astype(F32), axis=0, keepdims=True) + ebl * jnp.sum(dst * st, axis=0, keepdims=True)
            db = db + jnp.where(last, dbl, 0.0)
            d = 1
            while d < CHUNK:
                db = db + jnp.where(row < CHUNK - d, pltpu.roll(db, CHUNK - d, 0), 0.0)
                d *= 2
            dfg = db / f - dk
            df_ref[sl, :] = (dfg * (1.0 - lbv) * sg * (1.0 - sg)).astype(df_ref.dtype)
            dq_ref[sl, :] = (dq * (HEAD ** -0.5) * (sq * (1.0 + qr * (1.0 - sq)))).astype(dq_ref.dtype)
            dlb = dlb + jnp.sum(dfg * (1.0 - sg), axis=0, keepdims=True)
            dst = ebl * dst + lax.dot_general(d_o, q016, tn, preferred_element_type=F32)
            return dst, dlb

        _, dlb = lax.fori_loop(0, nch // 2, lambda j, cr: step(2 * j + 1, step(2 * j, cr)),
                               (jnp.zeros((HEAD, HEAD), F32), jnp.zeros((1, HEAD), F32)))
        arow = lax.broadcasted_iota(jnp.int32, (SUBLANES, HEAD), 0)
        dlb_ref[...] = jnp.where(arow == 0, dlb, 0.0)

    col = lambda off: pl.BlockSpec((T, HEAD), functools.partial(lambda h, off: (0, off + h), off=off))
    return _pcall(
        body, name="hgrn_bwd", grid=(H,),
        in_specs=[col(0), col(H), col(2 * H), pl.BlockSpec((1, HEAD), lambda h: (0, h)),
                  pl.BlockSpec((None, nch, HEAD, HEAD), lambda h: (h, 0, 0, 0)), col(0), col(0)],
        out_specs=[pl.BlockSpec((4, T, HEAD), lambda h: (0, 0, h)), pl.BlockSpec((SUBLANES, HEAD), lambda h: (0, h))],
        out_shape=[jax.ShapeDtypeStruct((4, T, d_model), BF16), jax.ShapeDtypeStruct((SUBLANES, d_model), F32)],
        compiler_params=_params(("parallel",)),
    )(proj, proj, proj, lb, states, do, dgate)


def _lb_softmax(table):
    n, f = table.shape

    def body(t_ref, p_ref):
        t = t_ref[...]
        e = jnp.exp(t - jnp.max(t, axis=0, keepdims=True))
        p_ref[...] = e / jnp.sum(e, axis=0, keepdims=True)

    padded = jnp.pad(table, ((0, SUBLANES - n), (0, 0)), constant_values=-jnp.inf)
    return _pcall(body, name="lb_softmax", out_shape=jax.ShapeDtypeStruct((SUBLANES, f), F32))(padded)


def _adamw_math(w, g, m, v):
    m = ADAM_B1 * m + (1.0 - ADAM_B1) * g
    v = ADAM_B2 * v + (1.0 - ADAM_B2) * (g * g)
    m_hat = m / (1.0 - ADAM_B1 ** ADAM_STEP)
    v_hat = v / (1.0 - ADAM_B2 ** ADAM_STEP)
    delta = -ADAM_LR * (m_hat / (jnp.sqrt(v_hat) + ADAM_EPS) + ADAM_WD * w)
    return delta, m, v


def _adamw(w, g, m, v, name):
    R, C = w.shape
    tr = _pick(R, (128, 64, 32, 16, 8))

    def body(w_ref, g_ref, m_ref, v_ref, d_ref, nm_ref, nv_ref):
        d, nm, nv = _adamw_math(w_ref[...], g_ref[...], m_ref[...], v_ref[...])
        d_ref[...] = d
        nm_ref[...] = nm
        nv_ref[...] = nv

    spec = pl.BlockSpec((tr, C), lambda i: (i, 0))
    return _pcall(
        body, name=name, grid=(R // tr,), in_specs=[spec] * 4, out_specs=[spec] * 3,
        out_shape=[jax.ShapeDtypeStruct((R, C), F32)] * 3, compiler_params=_params(("parallel",)),
    )(w, g, m, v)


def _adamw_halves(w, m, v, g_mine, g_recv, c, name, layer=0, prev=None):
    C = w.shape[1]
    rh = g_mine.shape[0]
    tr = _pick(rh, (128, 64, 32, 16, 8))
    nb = rh // tr
    r0 = layer * 2 * nb

    def body(c_ref, w_ref, m_ref, v_ref, gm_ref, gr_ref, *rest):
        g_ref, d_ref, nm_ref, nv_ref = rest[-4:]
        g = jnp.where(pl.program_id(0) == c_ref[0], gm_ref[...], gr_ref[...])
        d, nm, nv = _adamw_math(w_ref[...], g, m_ref[...], v_ref[...])
        g_ref[...] = g
        d_ref[...] = d
        nm_ref[...] = nm
        nv_ref[...] = nv

    full = pl.BlockSpec((tr, C), lambda h, i, cr: (r0 + h * nb + i, 0))
    half = pl.BlockSpec((tr, C), lambda h, i, cr: (i, 0))
    in_specs = [full, full, full, half, half]
    args = [c, w, m, v, g_mine, g_recv]
    alias = {}
    if prev is not None:
        in_specs += [pl.BlockSpec(memory_space=pl.ANY)] * 4
        args += list(prev)
        alias = {6 + k: k for k in range(4)}
    return _pcall(
        body, name=name,
        grid_spec=pltpu.PrefetchScalarGridSpec(
            num_scalar_prefetch=1, grid=(2, nb), in_specs=in_specs, out_specs=[full] * 4),
        out_shape=[jax.ShapeDtypeStruct(w.shape, F32)] * 4, input_output_aliases=alias,
        compiler_params=_params(("parallel", "parallel")),
    )(*args)


def _lb_table_grad(p8, dlb, n):
    f = p8.shape[1]

    def body(p_ref, d_ref, o_ref):
        p = p_ref[...]
        d = d_ref[...]
        p0 = p[0:1, :]
        first = lax.broadcasted_iota(jnp.int32, p.shape, 0) == 0
        o_ref[...] = p * (jnp.where(first, d, 0.0) - d * p0)

    return _pcall(body, name="lb_table_grad", out_shape=jax.ShapeDtypeStruct((SUBLANES, f), F32))(p8, dlb)[:n]


def _place():
    x, y, c = lax.axis_index("x"), lax.axis_index("y"), lax.axis_index("c")
    chips = [(1 - x, y), (x, 1 - y), (1 - x, 1 - y)]
    return x, y, c, chips


HBM_SPEC = pl.BlockSpec(memory_space=pltpu.HBM)


def _gather_weights(big, small):
    nb, ns = len(big), len(small)
    n = nb + ns

    def body(*refs):
        ins, outs = refs[:n], refs[n:2 * n]
        send_sems, recv_sems, own_send, own_recv = refs[2 * n:]
        x, y, c, chips = _place()
        me = 2 * x + y
        sib = (x, y, 1 - c)
        own = [pltpu.make_async_remote_copy(
            src_ref=ins[t], dst_ref=outs[t].at[me], send_sem=own_send.at[t], recv_sem=own_recv.at[t],
            device_id=sib, device_id_type=MESH) for t in range(n)]
        for cp in own:
            cp.start()

        def half(t, h):
            rh = big[t].shape[0] // 2
            return pl.ds(pl.multiple_of(h * rh, rh), rh)

        sends = []
        for t in range(n):
            for j, chip in enumerate(chips):
                k = 6 * t + j
                if t < nb:
                    src, dst = ins[t].at[half(t, c)], outs[t].at[me, half(t, c)]
                else:
                    src, dst = ins[t], outs[t].at[me]
                sends.append(pltpu.make_async_remote_copy(
                    src_ref=src, dst_ref=dst, send_sem=send_sems.at[k], recv_sem=recv_sems.at[k],
                    device_id=(*chip, c), device_id_type=MESH))
        for cp in sends:
            cp.start()
        passed = []
        for t in range(n):
            for j, (cx, cy) in enumerate(chips):
                k = 6 * t + j
                s = 2 * cx + cy
                if t < nb:
                    landed = outs[t].at[s, half(t, c)]
                    pltpu.make_async_remote_copy(
                        src_ref=landed, dst_ref=landed, send_sem=send_sems.at[k], recv_sem=recv_sems.at[k],
                        device_id=sib, device_id_type=MESH).wait_recv()
                    fwd = pltpu.make_async_remote_copy(
                        src_ref=landed, dst_ref=landed, send_sem=send_sems.at[k + 3], recv_sem=recv_sems.at[k + 3],
                        device_id=sib, device_id_type=MESH)
                    fwd.start()
                    passed.append(fwd)
                else:
                    landed = outs[t].at[s]
                    pltpu.make_async_remote_copy(
                        src_ref=landed, dst_ref=landed, send_sem=send_sems.at[k], recv_sem=recv_sems.at[k],
                        device_id=sib, device_id_type=MESH).wait_recv()
        for t in range(nb):
            for j, (cx, cy) in enumerate(chips):
                k = 6 * t + j
                other = outs[t].at[2 * cx + cy, half(t, 1 - c)]
                pltpu.make_async_remote_copy(
                    src_ref=other, dst_ref=other, send_sem=send_sems.at[k + 3], recv_sem=recv_sems.at[k + 3],
                    device_id=sib, device_id_type=MESH).wait_recv()
        for cp in sends + passed:
            cp.wait_send()
        for cp in own:
            cp.wait()

    arrs = list(big) + list(small)
    return _pcall(
        body, name="gather_weights", in_specs=[HBM_SPEC] * n, out_specs=[HBM_SPEC] * n,
        out_shape=[jax.ShapeDtypeStruct((N_CHIPS,) + a.shape, a.dtype) for a in arrs],
        scratch_shapes=[pltpu.SemaphoreType.DMA((6 * n,)), pltpu.SemaphoreType.DMA((6 * n,)),
                        pltpu.SemaphoreType.DMA((n,)), pltpu.SemaphoreType.DMA((n,))],
    )(*arrs)


SEM_SPEC = pl.BlockSpec(memory_space=pltpu.SEMAPHORE)
DATAFLOW = pltpu.SideEffectType.DATAFLOW_SIDE_EFFECTING
COPIES_PER_SHARD = 4


def _shard_copies(ins, lands, send_sems, recv_sems):
    x, y, c, chips = _place()
    me = 2 * x + y
    cps = []
    for t in range(len(ins)):
        rh = ins[t].shape[0] // 2
        half = pl.ds(pl.multiple_of(c * rh, rh), rh)
        for j, chip in enumerate(chips):
            k = COPIES_PER_SHARD * t + j
            cps.append(pltpu.make_async_remote_copy(
                src_ref=ins[t].at[half], dst_ref=lands[t].at[me, half], send_sem=send_sems.at[k],
                recv_sem=recv_sems.at[k], device_id=(*chip, c), device_id_type=MESH))
        k = COPIES_PER_SHARD * t + 3
        cps.append(pltpu.make_async_remote_copy(
            src_ref=ins[t], dst_ref=lands[t].at[me], send_sem=send_sems.at[k], recv_sem=recv_sems.at[k],
            device_id=(x, y, 1 - c), device_id_type=MESH))
    return cps


def _gather_start(shards, thru, name):
    n = len(shards)
    nops = 2 * n + len(thru)

    def body(*refs):
        ins, lands = refs[:n], refs[n:2 * n]
        send_sems, recv_sems = refs[nops], refs[nops + 1]
        for cp in _shard_copies(ins, lands, send_sems, recv_sems):
            cp.start()

    lands = [pltpu.with_memory_space_constraint(lax.empty((N_CHIPS,) + s.shape, s.dtype), pltpu.HBM) for s in shards]
    ops = [pltpu.with_memory_space_constraint(s, pltpu.HBM) for s in shards] + lands + list(thru)
    nsem = COPIES_PER_SHARD * n
    res = _pcall(
        body, name=name, in_specs=[HBM_SPEC] * nops,
        out_specs=[SEM_SPEC, SEM_SPEC] + [HBM_SPEC] * nops,
        out_shape=[pltpu.SemaphoreType.DMA((nsem,)), pltpu.SemaphoreType.DMA((nsem,))]
        + [pltpu.HBM(o.shape, o.dtype) for o in ops],
        input_output_aliases={i: 2 + i for i in range(nops)},
        compiler_params=pltpu.CompilerParams(has_side_effects=DATAFLOW),
    )(*ops)
    return res[0], res[1], res[2:2 + n], res[2 + n:2 + 2 * n], list(res[2 + 2 * n:])


def _gather_wait(send_sems, recv_sems, shards, lands, after, name, which=None):
    n = len(shards)

    def body(*refs):
        ins, lnd = refs[:n], refs[n:2 * n]
        ssem, rsem = refs[2 * n], refs[2 * n + 1]
        for k, cp in enumerate(_shard_copies(ins, lnd, ssem, rsem)):
            if which is None or k % COPIES_PER_SHARD in which:
                cp.wait_send()
                cp.wait_recv()

    res = _pcall(
        body, name=name,
        in_specs=[HBM_SPEC] * (2 * n) + [SEM_SPEC, SEM_SPEC, pl.BlockSpec(memory_space=pl.ANY)],
        out_specs=[HBM_SPEC] * (2 * n),
        out_shape=[pltpu.HBM(o.shape, o.dtype) for o in list(shards) + list(lands)],
        input_output_aliases={i: i for i in range(2 * n)},
        compiler_params=pltpu.CompilerParams(has_side_effects=DATAFLOW),
    )(*shards, *lands, send_sems, recv_sems, after)
    return res[:n], res[n:]


def _forward_copies(land, send_sems, recv_sems, which=(0, 1, 2)):
    x, y, c, chips = _place()
    rh = land.shape[1] // 2
    return [pltpu.make_async_remote_copy(
        src_ref=land.at[2 * cx + cy, pl.ds(pl.multiple_of(c * rh, rh), rh)],
        dst_ref=land.at[2 * cx + cy, pl.ds(pl.multiple_of(c * rh, rh), rh)],
        send_sem=send_sems.at[j], recv_sem=recv_sems.at[j], device_id=(x, y, 1 - c), device_id_type=MESH)
        for j, (cx, cy) in enumerate(chips) if j in which]


def _forward_start(land, thru, name, which=(0, 1, 2)):
    def body(land_ref, thru_ref, send_sems, recv_sems, out_ref, thru_out):
        for cp in _forward_copies(land_ref, send_sems, recv_sems, which):
            cp.start()

    return _pcall(
        body, name=name, in_specs=[HBM_SPEC, HBM_SPEC], out_specs=[SEM_SPEC, SEM_SPEC, HBM_SPEC, HBM_SPEC],
        out_shape=[pltpu.SemaphoreType.DMA((3,)), pltpu.SemaphoreType.DMA((3,)), pltpu.HBM(land.shape, land.dtype),
                   pltpu.HBM(thru.shape, thru.dtype)],
        input_output_aliases={0: 2, 1: 3}, compiler_params=pltpu.CompilerParams(has_side_effects=DATAFLOW),
    )(land, thru)


def _forward_wait(send_sems, recv_sems, land, after, name, which=(0, 1, 2)):
    def body(land_ref, ssem, rsem, after_ref, out_ref):
        for cp in _forward_copies(land_ref, ssem, rsem, which):
            cp.wait_send()
            cp.wait_recv()

    return _pcall(
        body, name=name, in_specs=[HBM_SPEC, SEM_SPEC, SEM_SPEC, pl.BlockSpec(memory_space=pl.ANY)],
        out_specs=HBM_SPEC, out_shape=pltpu.HBM(land.shape, land.dtype), input_output_aliases={0: 0},
        compiler_params=pltpu.CompilerParams(has_side_effects=DATAFLOW),
    )(land, send_sems, recv_sems, after)


def _sibling_copies(ins, lands, send_sems, recv_sems, other_half):
    x, y, c, _ = _place()
    return [pltpu.make_async_remote_copy(
        src_ref=ins[t].at[:, 1 - c] if other_half else ins[t], dst_ref=lands[t], send_sem=send_sems.at[t],
        recv_sem=recv_sems.at[t], device_id=(x, y, 1 - c), device_id_type=MESH) for t in range(len(ins))]


def _sibling_start(srcs, other_half, thru, name):
    n = len(srcs)
    nthru = 0 if thru is None else 1

    def body(*refs):
        ins, lands = refs[:n], refs[n:2 * n]
        send_sems, recv_sems = refs[2 * n + nthru], refs[2 * n + nthru + 1]
        for cp in _sibling_copies(ins, lands, send_sems, recv_sems, other_half):
            cp.start()

    shapes = [(s.shape[0],) + s.shape[2:] if other_half else s.shape for s in srcs]
    lands = [pltpu.with_memory_space_constraint(lax.empty(sh, s.dtype), pltpu.HBM) for sh, s in zip(shapes, srcs)]
    ops = [pltpu.with_memory_space_constraint(s, pltpu.HBM) for s in srcs] + lands + ([] if thru is None else [thru])
    res = _pcall(
        body, name=name, in_specs=[HBM_SPEC] * len(ops),
        out_specs=[SEM_SPEC, SEM_SPEC] + [HBM_SPEC] * len(ops),
        out_shape=[pltpu.SemaphoreType.DMA((n,)), pltpu.SemaphoreType.DMA((n,))]
        + [pltpu.HBM(o.shape, o.dtype) for o in ops],
        input_output_aliases={i: 2 + i for i in range(len(ops))},
        compiler_params=pltpu.CompilerParams(has_side_effects=DATAFLOW),
    )(*ops)
    return res[0], res[1], res[2:2 + n], res[2 + n:2 + 2 * n], (None if thru is None else res[2 + 2 * n])


def _sibling_wait(send_sems, recv_sems, srcs, lands, other_half, after, name):
    n = len(srcs)

    def body(*refs):
        ins, lnd = refs[:n], refs[n:2 * n]
        ssem, rsem = refs[2 * n], refs[2 * n + 1]
        for cp in _sibling_copies(ins, lnd, ssem, rsem, other_half):
            cp.wait_send()
            cp.wait_recv()

    res = _pcall(
        body, name=name,
        in_specs=[HBM_SPEC] * (2 * n) + [SEM_SPEC, SEM_SPEC, pl.BlockSpec(memory_space=pl.ANY)],
        out_specs=[HBM_SPEC] * (2 * n),
        out_shape=[pltpu.HBM(o.shape, o.dtype) for o in list(srcs) + list(lands)],
        input_output_aliases={i: i for i in range(2 * n)},
        compiler_params=pltpu.CompilerParams(has_side_effects=DATAFLOW),
    )(*srcs, *lands, send_sems, recv_sems, after)
    return res[:n], res[n:]


def _chip_copies(ins, lands, send_sems, recv_sems):
    x, y, c, chips = _place()
    cps = []
    for t in range(len(ins)):
        for j, (cx, cy) in enumerate(chips):
            cps.append(pltpu.make_async_remote_copy(
                src_ref=ins[t].at[2 * cx + cy], dst_ref=lands[t].at[j],
                send_sem=send_sems.at[3 * t + j], recv_sem=recv_sems.at[3 * t + j],
                device_id=(cx, cy, c), device_id_type=MESH))
    return cps


def _chip_start(parts, thru, name):
    n = len(parts)

    def body(*refs):
        ins, lands = refs[:n], refs[n:2 * n]
        send_sems, recv_sems = refs[2 * n + 1], refs[2 * n + 2]
        for cp in _chip_copies(ins, lands, send_sems, recv_sems):
            cp.start()

    lands = [pltpu.with_memory_space_constraint(lax.empty((3,) + p.shape[1:], p.dtype), pltpu.HBM) for p in parts]
    ops = [pltpu.with_memory_space_constraint(p, pltpu.HBM) for p in parts] + lands + [thru]
    res = _pcall(
        body, name=name, in_specs=[HBM_SPEC] * (2 * n + 1),
        out_specs=[SEM_SPEC, SEM_SPEC] + [HBM_SPEC] * (2 * n + 1),
        out_shape=[pltpu.SemaphoreType.DMA((3 * n,)), pltpu.SemaphoreType.DMA((3 * n,))]
        + [pltpu.HBM(o.shape, o.dtype) for o in ops],
        input_output_aliases={i: 2 + i for i in range(2 * n + 1)},
        compiler_params=pltpu.CompilerParams(has_side_effects=DATAFLOW),
    )(*ops)
    return res[0], res[1], res[2:2 + n], res[2 + n:2 + 2 * n], res[2 + 2 * n]


def _chip_wait(send_sems, recv_sems, parts, lands, after, name):
    n = len(parts)

    def body(*refs):
        ins, lnd = refs[:n], refs[n:2 * n]
        ssem, rsem = refs[2 * n], refs[2 * n + 1]
        for cp in _chip_copies(ins, lnd, ssem, rsem):
            cp.wait_send()
            cp.wait_recv()

    res = _pcall(
        body, name=name,
        in_specs=[HBM_SPEC] * (2 * n) + [SEM_SPEC, SEM_SPEC, pl.BlockSpec(memory_space=pl.ANY)],
        out_specs=[HBM_SPEC] * (2 * n),
        out_shape=[pltpu.HBM(o.shape, o.dtype) for o in list(parts) + list(lands)],
        input_output_aliases={i: i for i in range(2 * n)},
        compiler_params=pltpu.CompilerParams(has_side_effects=DATAFLOW),
    )(*parts, *lands, send_sems, recv_sems, after)
    return res[:n], res[n:]


def _add_pair(grad, recv, c, name):
    s, _, rh, cc = grad.shape
    tr = _pick(rh, (256, 128, 64, 32, 16))

    def body(c_ref, g_ref, r_ref, o_ref):
        o_ref[...] = (g_ref[...].astype(F32) + r_ref[...].astype(F32)).astype(o_ref.dtype)

    return _pcall(
        body, name=name,
        grid_spec=pltpu.PrefetchScalarGridSpec(
            num_scalar_prefetch=1, grid=(s, rh // tr),
            in_specs=[pl.BlockSpec((None, None, tr, cc), lambda a, i, cr: (a, cr[0], i, 0)),
                      pl.BlockSpec((None, tr, cc), lambda a, i, cr: (a, i, 0))],
            out_specs=pl.BlockSpec((None, tr, cc), lambda a, i, cr: (a, i, 0))),
        out_shape=jax.ShapeDtypeStruct((s, rh, cc), BF16),
        compiler_params=_params(("parallel", "parallel")),
    )(c, grad, recv)


def _add_chips(part, recv, me, name):
    _, rh, cc = part.shape
    tr = _pick(rh, (256, 128, 64, 32, 16))

    def body(m_ref, p_ref, r_ref, o_ref):
        o_ref[...] = ((p_ref[...].astype(F32) + r_ref[0].astype(F32)) + r_ref[1].astype(F32)) + r_ref[2].astype(F32)

    return _pcall(
        body, name=name,
        grid_spec=pltpu.PrefetchScalarGridSpec(
            num_scalar_prefetch=1, grid=(rh // tr,),
            in_specs=[pl.BlockSpec((None, tr, cc), lambda i, mr: (mr[0], i, 0)),
                      pl.BlockSpec((3, tr, cc), lambda i, mr: (0, i, 0))],
            out_specs=pl.BlockSpec((tr, cc), lambda i, mr: (i, 0))),
        out_shape=jax.ShapeDtypeStruct((rh, cc), F32),
        compiler_params=_params(("parallel",)),
    )(me, part, recv)


def _all_sum(vec):
    rows = vec.shape[0]

    def body(v_ref, o_ref, buf, send_sems, recv_sems):
        x, y, c, _ = _place()
        me = 4 * x + 2 * y + c
        buf[me] = v_ref[...]
        cps = []
        for r in range(1, 8):
            fx, fy, fc = (r >> 2) & 1, (r >> 1) & 1, r & 1
            peer = (x ^ fx, y ^ fy, c ^ fc)
            cps.append(pltpu.make_async_remote_copy(
                src_ref=v_ref, dst_ref=buf.at[me], send_sem=send_sems.at[r - 1], recv_sem=recv_sems.at[r - 1],
                device_id=peer, device_id_type=MESH))
        for cp in cps:
            cp.start()
        for r in range(1, 8):
            src = me ^ r
            pltpu.make_async_remote_copy(
                src_ref=v_ref, dst_ref=buf.at[src], send_sem=send_sems.at[r - 1], recv_sem=recv_sems.at[r - 1],
                device_id=(x, y, c), device_id_type=MESH).wait_recv()
        for cp in cps:
            cp.wait_send()
        acc = buf[0]
        for d in range(1, 8):
            acc = acc + buf[d]
        o_ref[...] = acc

    return _pcall(
        body, name="all_sum_small",
        in_specs=[pl.BlockSpec(memory_space=pltpu.VMEM)], out_specs=pl.BlockSpec(memory_space=pltpu.VMEM),
        out_shape=jax.ShapeDtypeStruct((rows, LANES), F32),
        scratch_shapes=[pltpu.VMEM((8, rows, LANES), F32), pltpu.SemaphoreType.DMA((7,)), pltpu.SemaphoreType.DMA((7,))],
    )(vec)


def _pack(parts):
    flat = jnp.concatenate([p.reshape(-1) for p in parts])
    tile = SUBLANES * LANES
    pad = (-flat.shape[0]) % tile
    return jnp.pad(flat, (0, pad)).reshape(-1, LANES)


def _unpack(vec, shapes):
    flat = vec.reshape(-1)
    out, p = [], 0
    for s in shapes:
        n = 1
        for d in s:
            n *= d
        out.append(flat[p:p + n].reshape(s))
        p += n
    return out


def _local_step(x, tgt, norm_mix, norm_ffn, lb8, out_norm, final_norm, sc_conv, ffn_conv, first, weights, reduce_start,
                reduce_finish):
    T, D = x.shape
    F2 = ffn_conv.shape[-1]
    FF = F2 // 2
    tm = _pick(T, (1024, 512, 256, 128))
    wide = (1536, 1408, 1024, 768, 512, 384, 256, 128)
    cw_h, cw_s, cw_u = 4 * D // N_CHIPS, 3 * D // N_CHIPS, F2 // N_CHIPS
    kp = FF // N_CHIPS
    tk_ff = kp if kp % LANES == 0 else LANES
    tn_d = _pick(D, (1024, 512, 256, 128))
    tk_w = _pick(D, (512, 256, 128))
    tn_h = _pick(cw_h, (1024, 512, 256, 128))
    tn_s = _pick(D // N_CHIPS, (512, 256, 128))
    tn_u = _pick(cw_u, wide)
    lb = lb8[0:1]
    wm_sq = _wmap_col(D, tn_d, 0)
    wm_sq1 = _wmap_col(D, D, 0)
    seg1 = lambda a: a.reshape((1,) + a.shape)

    def mix_in(h, w):
        return _row_call(_rms_fwd_fn, [(h, 0, D)], [w], [(D, BF16)], 0, "rms_fwd")[0]

    def rms_bwd(h, dxn, dh, w):
        return _row_call(_rms_bwd_fn, [(h, 0, D), (dxn, 0, D), (dh, 0, D)], [w], [(D, F32), (D, BF16)], 1, "rms_bwd")

    def ffn_fwd(h, i, fetch_up, fetch_down, next_stage=None):
        xn = mix_in(h, norm_ffn[i:i + 1])
        tn = _pick(cw_u, wide)
        w_up = fetch_up(xn)
        up = _mm_nn(xn, w_up, _wmap_col(cw_u, tn, 0), D, F2, tm, D, tn, "ffn_up")
        nb = FF // LANES
        a = _col_call(_glu_fwd_fn, [(up, 0), (up, nb)], [(ffn_conv[i], 0), (ffn_conv[i], nb)], [(1, FF, BF16)], 0,
                      "glu_fwd", before=True, after=False)[0][0]
        later = None
        if next_stage is not None:
            later, a = weights(next_stage, a)
        w_down = fetch_down(a)
        h2 = _mm_nn(a, w_down, _wmap_row(kp, tk_ff, 0), FF, D, tm, tk_ff, tn_d, "ffn_down", res=h)
        return h2, (xn, up, a), w_up, w_down, later

    def ffn_bwd(dh, dh16, h, saved, i, w_up, w_down):
        xn, up, a = saved
        g_down = _mm_tn(a, seg1(dh16), (N_CHIPS, 1, kp, D), _wmap_row(kp, tk_ff, 0), FF, D, tk_ff, tn_d,
                        "ffn_down_dw", tm=_pick(T, (2048, 1024, 512, 256, 128)))
        dh16 = reduce_start(("ffn_w_down", i), g_down, dh16)
        da = _mm_nt(seg1(dh16), w_down, _wmap_row(kp, tk_ff, 0), FF, D, tm, tk_ff, D, "ffn_down_dx")
        nb = FF // LANES
        dgv, cg, cv = _col_call(_glu_bwd_fn, [(up, 0), (up, nb), (da, 0)], [(ffn_conv[i], 0), (ffn_conv[i], nb)],
                                [(2, FF, BF16)], 2, "glu_bwd", before=True, after=True)
        g_up = _mm_tn(xn, dgv, (N_CHIPS, 1, D, cw_u), _wmap_col(cw_u, tn_u, 0), D, F2, tk_w, tn_u, "ffn_up_dw")
        dgv = reduce_start(("ffn_w_up", i), g_up, dgv)
        dxn = _mm_nt(dgv, w_up, _wmap_col(cw_u, tn_u, 0), D, F2, tm, D, tn_u, "ffn_up_dx")
        dh2, dh2_16, dnw = rms_bwd(h, dxn, dh, norm_ffn[i:i + 1])
        return dh2, reduce_finish(dh2_16), dnw, jnp.concatenate([cg[:3], cv[:3]], axis=1)

    h0 = x
    xn0 = mix_in(h0, norm_mix[0:1])
    proj, w_hin = first(xn0, lambda w, s, prev: _mm_nn_shard(xn0, w, s, tm, tn_h, "hgrn_in", prev))
    o, states = _hgrn_fwd(proj, lb, D)
    (fetch_hout, fetch_up0, fetch_down0), o = weights(1, o)
    on = _row_call(_onorm_fwd_fn, [(o, 0, D), (proj, 3, D)], [out_norm], [(D, BF16)], 0, "onorm_fwd")[0]
    w_hout1 = fetch_hout(on).reshape(1, D, D)
    h1 = _mm_nn(on, w_hout1, wm_sq, D, D, tm, D, tn_d, "hgrn_out", res=h0)
    h2, ffn0, w_up0, w_down0, (fetch_sin, fetch_sout) = ffn_fwd(h1, 0, fetch_up0, fetch_down0, next_stage=2)
    xn1 = mix_in(h2, norm_mix[1:2])
    w_sin = fetch_sin(xn1)
    tn_si = _pick(cw_s, wide)
    sproj = _mm_nn(xn1, w_sin, _wmap_col(cw_s, tn_si, 0), D, 3 * D, tm, D, tn_si, "sc_in")
    (fetch_up1, fetch_down1), sproj = weights(3, sproj)
    nd = D // LANES
    ysc = _col_call(_sc_fwd_fn, [(sproj, 0), (sproj, nd), (sproj, 2 * nd)], [(sc_conv, 0)], [(1, D, BF16)], 0,
                    "sc_fwd", before=True, after=False)[0][0]
    w_sout1 = fetch_sout(ysc).reshape(1, D, D)
    h3 = _mm_nn(ysc, w_sout1, wm_sq, D, D, tm, D, tn_d, "sc_out", res=h2)
    h4, ffn1, w_up1, w_down1, _ = ffn_fwd(h3, 1, fetch_up1, fetch_down1)

    dh, dh16, esq, dfinal = _row_call(_final_fn, [(h4, 0, D), (tgt, 0, D)], [final_norm], [(D, F32), (D, BF16)], 2,
                                      "final_loss")
    loss = 0.5 / D * jnp.sum(esq)
    dh, dh16, dnf1, dconv1 = ffn_bwd(dh, dh16, h3, ffn1, 1, w_up1, w_down1)
    g_sout = _mm_tn(ysc, seg1(dh16), (1, 1, D, D), wm_sq, D, D, tk_w, tn_d, "sc_out_dw")
    dh16 = reduce_start(("sc_w_out", 0), g_sout, dh16)
    dy = _mm_nt(seg1(dh16), w_sout1, wm_sq1, D, D, tm, D, D, "sc_out_dx")
    dsp, dscc = _col_call(_sc_bwd_fn, [(sproj, 0), (sproj, nd), (sproj, 2 * nd), (dy, 0)], [(sc_conv, 0)],
                          [(3, D, BF16)], 1, "sc_bwd", before=True, after=True)
    g_sin = _mm_tn(xn1, dsp, (N_CHIPS, 1, D, cw_s), _wmap_col(cw_s, tn_s, 0), D, 3 * D, tk_w, tn_s, "sc_in_dw")
    dsp = reduce_start(("sc_w_in", 0), g_sin, dsp)
    dxn = _mm_nt(dsp, w_sin, _wmap_col(cw_s, tn_s, 0), D, 3 * D, tm, D, tn_s, "sc_in_dx", per_step=3)
    dh, dh16, dnm1 = rms_bwd(h2, dxn, dh, norm_mix[1:2])
    dh16 = reduce_finish(dh16)
    dh, dh16, dnf0, dconv0 = ffn_bwd(dh, dh16, h1, ffn0, 0, w_up0, w_down0)
    g_hout = _mm_tn(on, seg1(dh16), (1, 1, D, D), wm_sq, D, D, tk_w, tn_d, "hgrn_out_dw")
    dh16 = reduce_start(("hgrn_w_out", 0), g_hout, dh16)
    don = _mm_nt(seg1(dh16), w_hout1, wm_sq1, D, D, tm, D, D, "hgrn_out_dx")
    do, dgate, dgain = _row_call(_onorm_bwd_fn, [(o, 0, D), (proj, 3, D), (don, 0, D)], [out_norm],
                                 [(D, F32), (D, BF16)], 1, "onorm_bwd")
    dproj, dlb = _hgrn_bwd(proj, lb, states, do, dgate, D)
    g_hin = _mm_tn(xn0, dproj, (N_CHIPS, 1, D, cw_h), _wmap_col(cw_h, tn_h, 0), D, 4 * D, tk_w, tn_h, "hgrn_in_dw")
    dproj = reduce_start(("hgrn_w_in", 0), g_hin, dproj)
    dxn = _mm_nt(dproj, w_hin, _wmap_col(cw_h, tn_h, 0), D, 4 * D, tm, D, tn_h, "hgrn_in_dx", per_step=2)
    grad_x, _, dnm0 = rms_bwd(h0, dxn, dh, norm_mix[0:1])

    small = dict(
        loss=loss,
        norm_mix=jnp.stack([jnp.sum(dnm0, axis=0), jnp.sum(dnm1, axis=0)]),
        norm_ffn=jnp.stack([jnp.sum(dnf0, axis=0), jnp.sum(dnf1, axis=0)]),
        lb=dlb[0:1],
        out_norm=jnp.sum(dgain, axis=0)[None],
        final_norm=jnp.sum(dfinal, axis=0),
        sc_conv=dscc[:3],
        ffn_conv=jnp.stack([dconv0, dconv1]),
    )
    return grad_x, small


def kernel(x, norm_mix, norm_ffn, hgrn_w_in, hgrn_lb_table, hgrn_out_norm, hgrn_w_out, sc_w_in, sc_conv, sc_w_out, ffn_w_up, ffn_conv, ffn_w_down, final_norm, loss_target, m_norm_mix, m_norm_ffn, m_hgrn_w_in, m_hgrn_lb_table, m_hgrn_out_norm, m_hgrn_w_out, m_sc_w_in, m_sc_conv, m_sc_w_out, m_ffn_w_up, m_ffn_conv, m_ffn_w_down, m_final_norm, v_norm_mix, v_norm_ffn, v_hgrn_w_in, v_hgrn_lb_table, v_hgrn_out_norm, v_hgrn_w_out, v_sc_w_in, v_sc_conv, v_sc_w_out, v_ffn_w_up, v_ffn_conv, v_ffn_w_down, v_final_norm):
    D = x.shape[-1]
    xi, yi, ci = lax.axis_index("x"), lax.axis_index("y"), lax.axis_index("c")
    me_chip = (2 * xi + yi).astype(jnp.int32).reshape(1)
    me_core = ci.astype(jnp.int32).reshape(1)

    big_names = ["hgrn_w_in", "hgrn_w_out", "sc_w_in", "sc_w_out", "ffn_w_up", "ffn_w_down"]
    big_w = dict(hgrn_w_in=hgrn_w_in, hgrn_w_out=hgrn_w_out, sc_w_in=sc_w_in, sc_w_out=sc_w_out,
                 ffn_w_up=ffn_w_up, ffn_w_down=ffn_w_down)
    big_m = dict(hgrn_w_in=m_hgrn_w_in, hgrn_w_out=m_hgrn_w_out, sc_w_in=m_sc_w_in, sc_w_out=m_sc_w_out,
                 ffn_w_up=m_ffn_w_up, ffn_w_down=m_ffn_w_down)
    big_v = dict(hgrn_w_in=v_hgrn_w_in, hgrn_w_out=v_hgrn_w_out, sc_w_in=v_sc_w_in, sc_w_out=v_sc_w_out,
                 ffn_w_up=v_ffn_w_up, ffn_w_down=v_ffn_w_down)
    flat2 = lambda a: a.reshape(-1, a.shape[-1])

    sh = lambda a: a.reshape(-1, a.shape[-1]).astype(BF16)
    conv_shards = [flat2(sc_conv), flat2(ffn_conv)]
    stages = [[sh(hgrn_w_in)], [sh(hgrn_w_out), sh(ffn_w_up[0]), sh(ffn_w_down[0])],
              [sh(sc_w_in), sh(sc_w_out)], [sh(ffn_w_up[1]), sh(ffn_w_down[1])]]
    scc4, fcc4 = _gather_weights([], [flat2(sc_conv), flat2(ffn_conv)])
    gathers = []
    thru = [scc4, norm_mix]
    for k, shards in enumerate(stages):
        ss, rs, src, land, thru = _gather_start(shards, thru, "gather_start_%d" % k)
        gathers.append((ss, rs, src, land))
    scc4, norm_mix = thru
    scc = jnp.moveaxis(scc4, 0, 1).reshape(3, D)
    f2 = ffn_conv.shape[-1] * N_CHIPS
    fcc = jnp.moveaxis(fcc4.reshape(N_CHIPS, 2, 3, -1), 0, 2).reshape(2, 3, f2)

    def first(after, matmul):
        ss, rs, src, land = gathers[0]
        src, land = _gather_wait(ss, rs, src, land, after, "gather_wait_0_own", which=(3,))
        proj = matmul(land[0], me_chip, None)
        others = [2 * (1 - xi) + yi, 2 * xi + (1 - yi), 2 * (1 - xi) + (1 - yi)]
        for j, s in enumerate(others):
            src, land = _gather_wait(ss, rs, src, land, proj, "gather_wait_0_%d" % j, which=(j,))
            fs, fr, l0, proj = _forward_start(land[0], proj, "gather_forward_start_0_%d" % j, which=(j,))
            land = [_forward_wait(fs, fr, l0, proj, "gather_forward_wait_0_%d" % j, which=(j,))]
            proj = matmul(land[0], s.astype(jnp.int32).reshape(1), proj)
        return proj, land[0]

    def weights(stage, after):
        _, lands = _gather_wait(*gathers[stage], after, "gather_wait_%d" % stage)
        fetch = []
        for t, land in enumerate(lands):
            ss, rs, land, after = _forward_start(land, after, "gather_forward_start_%d_%d" % (stage, t))
            fetch.append(functools.partial(_forward_wait, ss, rs, land, name="gather_forward_wait_%d_%d" % (stage, t)))
        return fetch, after

    pending = []
    started = []

    def reduce_start(slot, grad, thru):
        t = sum(len(b[0]) for b in pending) + len(started)
        halves = grad.reshape(N_CHIPS, 2, -1, grad.shape[-1])
        ss, rs, src, land, thru = _sibling_start([halves], True, thru, "grad_pair_start_%d" % t)
        started.append((slot, t, ss, rs, src, land))
        return thru

    def reduce_finish(thru):
        k = len(pending)
        pair = []
        for slot, t, ss, rs, src, land in started:
            src, recv = _sibling_wait(ss, rs, src, land, True, thru, "grad_pair_wait_%d" % t)
            pair.append(_add_pair(src[0], recv[0], me_core, "grad_add_pair"))
        ss, rs, pair, land, thru = _chip_start(pair, thru, "grad_chip_start_%d" % k)
        pending.append(([s[0] for s in started], ss, rs, pair, land))
        started.clear()
        return thru

    lb8 = _lb_softmax(hgrn_lb_table)
    grad_x, small = _local_step(
        x[0], loss_target[0], norm_mix, norm_ffn, lb8, hgrn_out_norm, final_norm[None], scc, fcc, first, weights,
        reduce_start, reduce_finish)

    small_names = ["loss", "norm_mix", "norm_ffn", "lb", "out_norm", "final_norm", "sc_conv", "ffn_conv"]
    parts = [small[n].astype(F32) for n in small_names]
    shapes = [p.shape for p in parts]
    tot = dict(zip(small_names, _unpack(reduce_finish(_all_sum(_pack(parts))), shapes)))
    loss = tot["loss"].reshape(())
    g_lb_table = _lb_table_grad(lb8, tot["lb"], hgrn_lb_table.shape[0])
    cw = sc_conv.shape[-1]
    g_sc_conv = lax.dynamic_slice_in_dim(tot["sc_conv"], me_chip[0] * cw, cw, axis=1)[None]
    cf = ffn_conv.shape[-1]
    g_ffn_conv = lax.dynamic_slice_in_dim(tot["ffn_conv"], me_chip[0] * cf, cf, axis=2)
    g_small = dict(norm_mix=tot["norm_mix"], norm_ffn=tot["norm_ffn"], hgrn_lb_table=g_lb_table,
                   hgrn_out_norm=tot["out_norm"], sc_conv=g_sc_conv, ffn_conv=g_ffn_conv, final_norm=tot["final_norm"])
    w_small = dict(norm_mix=norm_mix, norm_ffn=norm_ffn, hgrn_lb_table=hgrn_lb_table, hgrn_out_norm=hgrn_out_norm,
                   sc_conv=sc_conv, ffn_conv=ffn_conv, final_norm=final_norm)
    m_small = dict(norm_mix=m_norm_mix, norm_ffn=m_norm_ffn, hgrn_lb_table=m_hgrn_lb_table, hgrn_out_norm=m_hgrn_out_norm,
                   sc_conv=m_sc_conv, ffn_conv=m_ffn_conv, final_norm=m_final_norm)
    v_small = dict(norm_mix=v_norm_mix, norm_ffn=v_norm_ffn, hgrn_lb_table=v_hgrn_lb_table, hgrn_out_norm=v_hgrn_out_norm,
                   sc_conv=v_sc_conv, ffn_conv=v_ffn_conv, final_norm=v_final_norm)
    sm_names = list(g_small)
    sm_shapes = [w_small[n].shape for n in sm_names]
    d_s, m_s, v_s = _adamw(_pack([w_small[n] for n in sm_names]), _pack([g_small[n] for n in sm_names]),
                           _pack([m_small[n] for n in sm_names]), _pack([v_small[n] for n in sm_names]), "adamw_small")
    out_g, out_d, out_m, out_v = dict(g_small), {}, {}, {}
    for n, d_, m_, v_ in zip(sm_names, _unpack(d_s, sm_shapes), _unpack(m_s, sm_shapes), _unpack(v_s, sm_shapes)):
        out_d[n], out_m[n], out_v[n] = d_, m_, v_

    done = {}
    after = grad_x

    def add_and_share(k, after):
        slots, ss, rs, pair, land = pending[k]
        pair, recv = _chip_wait(ss, rs, pair, land, after, "grad_chip_wait_%d" % k)
        mine = [_add_chips(p, r, me_chip, "grad_add_chips") for p, r in zip(pair, recv)]
        ss, rs, mine, land, _ = _sibling_start(mine, False, None, "grad_share_start_%d" % k)
        return slots, ss, rs, mine, land

    def update(k, share, after):
        slots, ss, rs, mine, land = share
        mine, theirs = _sibling_wait(ss, rs, mine, land, False, after, "grad_share_wait_%d" % k)
        for (n, layer), gm, gr in zip(slots, mine, theirs):
            done[n] = _adamw_halves(flat2(big_w[n]), flat2(big_m[n]), flat2(big_v[n]), gm, gr, me_core, "adamw_" + n,
                                    layer=layer, prev=done.get(n))
        return done[slots[-1][0]][0]

    shares = []
    for k in range(len(pending) - 1):
        shares.append(add_and_share(k, after))
        after = shares[-1][3][0]
    for k, share in enumerate(shares):
        after = update(k, share, after)
    last = len(pending) - 1
    share = add_and_share(last, after)
    update(last, share, share[3][0])
    for n in big_names:
        out_g[n], out_d[n], out_m[n], out_v[n] = (a.reshape(big_w[n].shape) for a in done[n])

    order = ["norm_mix", "norm_ffn", "hgrn_w_in", "hgrn_lb_table", "hgrn_out_norm", "hgrn_w_out", "sc_w_in", "sc_conv",
             "sc_w_out", "ffn_w_up", "ffn_conv", "ffn_w_down", "final_norm"]
    return (loss, grad_x[None], *[out_g[n] for n in order], *[out_d[n] for n in order],
            *[out_m[n] for n in order], *[out_v[n] for n in order])
```

```python
import functools

import jax
import jax.numpy as jnp
from jax import lax
from jax.experimental import pallas as pl
from jax.experimental.pallas import tpu as pltpu

F32 = jnp.float32
BF16 = jnp.bfloat16
MESH = pl.DeviceIdType.MESH

EPS = 1e-6
CHUNK = 64
HEAD = 128
N_CHIPS = 4
ADAM_LR, ADAM_B1, ADAM_B2, ADAM_EPS, ADAM_WD, ADAM_STEP = 0.001, 0.9, 0.999, 1e-08, 0.01, 10
VMEM_LIMIT = 56 * 1024 * 1024
SUBLANES = 8
LANES = 128


def _pcall(body, **kw):
    return pl.pallas_call(body, **kw)


def _params(sem, vmem=VMEM_LIMIT):
    return pltpu.CompilerParams(dimension_semantics=sem, vmem_limit_bytes=vmem)


def _pick(dim, prefs):
    for p in prefs:
        if p <= dim and dim % p == 0:
            return p
    return dim


def _sigmoid(x):
    return 1.0 / (1.0 + jnp.exp(-x))


def _wmap_col(cw, tn, r0):
    bps = cw // tn
    return lambda kb, nb: (nb // bps, r0 + kb, nb % bps)


def _wmap_row(kp, tk, r0):
    bps = kp // tk
    return lambda kb, nb: (kb // bps, r0 + kb % bps, nb)


def _mm_nn(a, w3, wmap, K, N, tm, tk, tn, name, res=None):
    M = a.shape[0]
    nk = K // tk

    def body(*refs):
        if res is None:
            a_ref, w_ref, o_ref = refs[:3]
        else:
            a_ref, w_ref, r_ref, o_ref = refs[:4]
        p = jnp.dot(a_ref[...], w_ref[...], preferred_element_type=F32)
        if nk == 1:
            o_ref[...] = p if res is None else p + r_ref[...]
            return
        acc = refs[-1]
        k = pl.program_id(2)

        @pl.when(k == 0)
        def _():
            acc[...] = p

        @pl.when(k > 0)
        def _():
            acc[...] += p

        @pl.when(k == nk - 1)
        def _():
            o_ref[...] = acc[...] if res is None else acc[...] + r_ref[...]

    if nk == 1:
        grid = (M // tm, N // tn)
        ix = lambda f: (lambda i, j: f(i, j, 0))
        sem = ("parallel", "parallel")
        scratch = []
    else:
        grid = (M // tm, N // tn, nk)
        ix = lambda f: f
        sem = ("parallel", "parallel", "arbitrary")
        scratch = [pltpu.VMEM((tm, tn), F32)]
    in_specs = [pl.BlockSpec((tm, tk), ix(lambda i, j, k: (i, k))),
                pl.BlockSpec((None, tk, tn), ix(lambda i, j, k: wmap(k, j)))]
    args = [a, w3]
    if res is not None:
        in_specs.append(pl.BlockSpec((tm, tn), ix(lambda i, j, k: (i, j))))
        args.append(res)
    return _pcall(
        body, name=name, grid=grid, in_specs=in_specs,
        out_specs=pl.BlockSpec((tm, tn), ix(lambda i, j, k: (i, j))),
        out_shape=jax.ShapeDtypeStruct((M, N), F32), scratch_shapes=scratch, compiler_params=_params(sem),
    )(*args)


def _mm_nn_shard(a, w3, s, tm, tn, name, prev=None):
    M, K = a.shape
    S, _, cw = w3.shape
    bps = cw // tn

    def body(s_ref, a_ref, w_ref, *rest):
        o_ref = rest[-1]
        o_ref[...] = jnp.dot(a_ref[...], w_ref[...], preferred_element_type=F32)

    in_specs = [pl.BlockSpec((tm, K), lambda i, j, sr: (i, 0)),
                pl.BlockSpec((None, K, tn), lambda i, j, sr: (sr[0], 0, j))]
    args = [s, a, w3]
    alias = {}
    if prev is not None:
        in_specs.append(pl.BlockSpec(memory_space=pl.ANY))
        args.append(prev)
        alias = {3: 0}
    return _pcall(
        body, name=name,
        grid_spec=pltpu.PrefetchScalarGridSpec(
            num_scalar_prefetch=1, grid=(M // tm, bps), in_specs=in_specs,
            out_specs=pl.BlockSpec((tm, tn), lambda i, j, sr: (i, sr[0] * bps + j))),
        out_shape=jax.ShapeDtypeStruct((M, S * cw), F32), input_output_aliases=alias,
        compiler_params=_params(("parallel", "parallel")),
    )(*args)


def _mm_nt(dy3, w3, wmap, K, N, tm, tk, tn, name, per_step=1):
    M = dy3.shape[1]
    bps = dy3.shape[2] // tn
    u = per_step
    grid = (M // tm, K // tk, N // (tn * u))
    nn = grid[2]

    def body(*refs):
        o_ref = refs[-1]
        p = None
        for r in range(u):
            d = lax.dot_general(refs[r][...], refs[u + r][...], (((1,), (1,)), ((), ())), preferred_element_type=F32)
            p = d if p is None else p + d
        if nn == 1:
            o_ref[...] = p
            return
        n = pl.program_id(2)

        @pl.when(n == 0)
        def _():
            o_ref[...] = p

        @pl.when(n > 0)
        def _():
            o_ref[...] += p

    def dy_spec(r):
        return pl.BlockSpec((None, tm, tn), lambda i, j, n: ((n * u + r) // bps, i, (n * u + r) % bps))

    def w_spec(r):
        return pl.BlockSpec((None, tk, tn), lambda i, j, n: wmap(j, n * u + r))

    return _pcall(
        body, name=name, grid=grid,
        in_specs=[dy_spec(r) for r in range(u)] + [w_spec(r) for r in range(u)],
        out_specs=pl.BlockSpec((tm, tk), lambda i, j, n: (i, j)),
        out_shape=jax.ShapeDtypeStruct((M, K), F32),
        compiler_params=_params(("parallel", "parallel", "arbitrary")),
    )(*([dy3] * u), *([w3] * u))


def _mm_tn(x, dy3, shape4, wmap, K, N, tk, tn, name, tm=None):
    M = x.shape[0]
    tm = M if tm is None else tm
    nm = M // tm
    bps = dy3.shape[2] // tn

    def body(*refs):
        x_ref, dy_ref = refs[:2]
        p = lax.dot_general(x_ref[...], dy_ref[...], (((0,), (0,)), ((), ())), preferred_element_type=F32)
        if nm == 1:
            o_ref = refs[-1]
            o_ref[...] = p.astype(o_ref.dtype)
            return
        o_ref, acc = refs[-2:]
        m = pl.program_id(2)

        @pl.when(m == 0)
        def _():
            acc[...] = p

        @pl.when(m > 0)
        def _():
            acc[...] += p

        @pl.when(m == nm - 1)
        def _():
            o_ref[...] = acc[...].astype(o_ref.dtype)

    def omap(i, j, m):
        s, rb, cb = wmap(i, j)
        return (s, 0, rb, cb)

    return _pcall(
        body, name=name, grid=(K // tk, N // tn, nm),
        in_specs=[pl.BlockSpec((tm, tk), lambda i, j, m: (m, i)),
                  pl.BlockSpec((None, tm, tn), lambda i, j, m: (j // bps, m, j % bps))],
        out_specs=pl.BlockSpec((None, None, tk, tn), omap),
        out_shape=jax.ShapeDtypeStruct(shape4, BF16),
        scratch_shapes=[] if nm == 1 else [pltpu.VMEM((tk, tn), F32)],
        compiler_params=_params(("parallel", "parallel", "arbitrary")),
    )(x, dy3)


def _row_call(fn, rows, vecs, outs, n_acc, name, t_rows=256, sub=16, per_trip=4):
    T = rows[0][0].shape[0]
    t_rows = min(t_rows, T)
    nsub = t_rows // sub
    n_r, n_v, n_o = len(rows), len(vecs), len(outs)
    width = rows[0][2]

    def body(*refs):
        r_refs = refs[:n_r]
        v_refs = refs[n_r:n_r + n_v]
        o_refs = refs[n_r + n_v:n_r + n_v + n_o]
        a_refs = refs[n_r + n_v + n_o:]

        @pl.when(pl.program_id(0) == 0)
        def _():
            for a in a_refs:
                a[...] = jnp.zeros_like(a)

        vv = [v[...] for v in v_refs]

        def step(i, carry):
            done = []
            for u in range(per_trip):
                sl = pl.ds(pl.multiple_of((i * per_trip + u) * sub, sub), sub)
                done.append((sl,) + tuple(fn([r[sl, :] for r in r_refs], vv)))
            for sl, o_vals, a_vals in done:
                for o, val in zip(o_refs, o_vals):
                    o[sl, :] = val.astype(o.dtype)
            for a_i, a in enumerate(a_refs):
                tot = None
                for _, _, a_vals in done:
                    part = a_vals[a_i].reshape(sub // SUBLANES, SUBLANES, a_vals[a_i].shape[-1]).sum(axis=0)
                    tot = part if tot is None else tot + part
                a[...] += tot
            return carry

        lax.fori_loop(0, nsub // per_trip, step, 0)

    in_specs = [pl.BlockSpec((t_rows, w), functools.partial(lambda i, cb: (i, cb), cb=cb)) for _, cb, w in rows]
    in_specs += [pl.BlockSpec(v.shape, lambda i: (0, 0)) for v in vecs]
    out_specs = [pl.BlockSpec((t_rows, w), lambda i: (i, 0)) for w, _ in outs]
    out_specs += [pl.BlockSpec((SUBLANES, width), lambda i: (0, 0)) for _ in range(n_acc)]
    out_shape = [jax.ShapeDtypeStruct((T, w), dt) for w, dt in outs]
    out_shape += [jax.ShapeDtypeStruct((SUBLANES, width), F32) for _ in range(n_acc)]
    return _pcall(
        body, name=name, grid=(T // t_rows,), in_specs=in_specs, out_specs=out_specs, out_shape=out_shape,
        compiler_params=_params(("arbitrary",)),
    )(*[r[0] for r in rows], *vecs)


def _rms_fwd_fn(rv, vv):
    h, = rv
    w, = vv
    r = lax.rsqrt(jnp.mean(h * h, axis=-1, keepdims=True) + EPS)
    return [h * r * w], []


def _rms_bwd_fn(rv, vv):
    h, dxn, dh_in = rv
    w, = vv
    d = h.shape[-1]
    r = lax.rsqrt(jnp.mean(h * h, axis=-1, keepdims=True) + EPS)
    gy = dxn * w
    dh = r * gy - h * ((r * r * r) * (1.0 / d) * jnp.sum(gy * h, axis=-1, keepdims=True))
    return [dh_in + dh] * 2, [dxn * h * r]


def _final_fn(rv, vv):
    h, tgt = rv
    w, = vv
    d = h.shape[-1]
    r = lax.rsqrt(jnp.mean(h * h, axis=-1, keepdims=True) + EPS)
    hn = h * r
    e = hn * w - tgt
    dy = e * (1.0 / d)
    gy = dy * w
    dh = r * gy - h * ((r * r * r) * (1.0 / d) * jnp.sum(gy * h, axis=-1, keepdims=True))
    return [dh] * 2, [e * e, dy * hn]


def _onorm_fwd_fn(rv, vv):
    o, g = rv
    gain, = vv
    r = lax.rsqrt(jnp.mean(o * o, axis=-1, keepdims=True) + EPS)
    return [o * r * gain * (g * _sigmoid(g))], []


def _onorm_bwd_fn(rv, vv):
    o, g, don = rv
    gain, = vv
    d = o.shape[-1]
    r = lax.rsqrt(jnp.mean(o * o, axis=-1, keepdims=True) + EPS)
    sg = _sigmoid(g)
    sl = g * sg
    n = o * r
    dg = don * n * gain * (sg * (1.0 + g * (1.0 - sg)))
    gy = don * sl * gain
    do = r * gy - o * ((r * r * r) * (1.0 / d) * jnp.sum(gy * o, axis=-1, keepdims=True))
    return [do, dg], [don * sl * n]


HALO = SUBLANES


def _col_call(fn, cols, vecs, outs, n_acc, name, before, after, tc=LANES, chunk=128):
    T = cols[0][0].shape[0]
    chunk = min(chunk, T)
    nch = T // chunk
    ncol = outs[0][1] // tc
    n_c, n_v, n_o = len(cols), len(vecs), len(outs)
    hb = HALO if before else 0
    rw = chunk + hb + (HALO if after else 0)

    def body(*refs):
        c_refs = refs[:n_c]
        v_refs = refs[n_c:n_c + n_v]
        o_refs = refs[n_c + n_v:n_c + n_v + n_o]
        a_refs = refs[n_c + n_v + n_o:]
        vv = [v[...] for v in v_refs]
        wrow = lax.broadcasted_iota(jnp.int32, (rw, tc), 0)
        inside = (wrow >= hb) & (wrow < hb + chunk)

        def step(i, carry):
            r0 = pl.multiple_of(i * chunk, chunk)
            wins = []
            for ref in c_refs:
                parts = []
                if before:
                    pb = ref[pl.ds(pl.multiple_of(jnp.maximum(r0 - HALO, 0), HALO), HALO), :]
                    parts.append(jnp.where(i > 0, pb, 0.0))
                parts.append(ref[pl.ds(r0, chunk), :])
                if after:
                    pa = ref[pl.ds(pl.multiple_of(jnp.minimum(r0 + chunk, T - HALO), HALO), HALO), :]
                    parts.append(jnp.where(i < nch - 1, pa, 0.0))
                wins.append(jnp.concatenate(parts, axis=0) if len(parts) > 1 else parts[0])
            o_vals, a_vals = fn(wins, vv, inside)
            p = 0
            for o, (nseg, _, _) in zip(o_refs, outs):
                for s in range(nseg):
                    o[s, pl.ds(r0, chunk), :] = o_vals[p][hb:hb + chunk].astype(o.dtype)
                    p += 1
            return tuple(c + a for c, a in zip(carry, a_vals))

        taps = [v.shape[0] for v, _ in vecs][:n_acc]
        init = tuple(jnp.zeros((1, tc), F32) for k in taps for _ in range(k))
        sums = lax.fori_loop(0, nch, step, init)
        arow = lax.broadcasted_iota(jnp.int32, (SUBLANES, tc), 0)
        p = 0
        for a, k in zip(a_refs, taps):
            acc = jnp.zeros((SUBLANES, tc), F32)
            for t in range(k):
                acc = jnp.where(arow == t, sums[p], acc)
                p += 1
            a[...] = acc

    in_specs = [pl.BlockSpec((T, tc), functools.partial(lambda j, off: (0, off + j), off=off)) for _, off in cols]
    in_specs += [pl.BlockSpec((v.shape[0], tc), functools.partial(lambda j, off: (0, off + j), off=off))
                 for v, off in vecs]
    out_specs = [pl.BlockSpec((nseg, T, tc), lambda j: (0, 0, j)) for nseg, _, _ in outs]
    out_specs += [pl.BlockSpec((SUBLANES, tc), lambda j: (0, j)) for _ in range(n_acc)]
    out_shape = [jax.ShapeDtypeStruct((nseg, T, w), dt) for nseg, w, dt in outs]
    out_shape += [jax.ShapeDtypeStruct((SUBLANES, ncol * tc), F32) for _ in range(n_acc)]
    return _pcall(
        body, name=name, grid=(ncol,), in_specs=in_specs, out_specs=out_specs, out_shape=out_shape,
        compiler_params=_params(("parallel",)),
    )(*[c[0] for c in cols], *[v[0] for v in vecs])


def _down(x, k):
    return x if k == 0 else pltpu.roll(x, k, 0)


def _up(x, k):
    return x if k == 0 else pltpu.roll(x, x.shape[0] - k, 0)


def _lags(x):
    return _down(x, 2), _down(x, 1), x


def _conv(lags, w):
    return w[0:1] * lags[0] + w[1:2] * lags[1] + w[2:3] * lags[2]


def _conv_t(d, w):
    return w[2:3] * d + w[1:2] * _up(d, 1) + w[0:1] * _up(d, 2)


def _tap_sums(d, lags, inside):
    dm = jnp.where(inside, d, 0.0)
    return [jnp.sum(dm * lag, axis=0, keepdims=True) for lag in lags]


def _glu_fwd_fn(wins, vv, inside):
    xg, xv = wins
    wg, wv = vv
    ug = _conv(_lags(xg), wg)
    uv = _conv(_lags(xv), wv)
    return [ug * _sigmoid(ug) * uv], []


def _glu_bwd_fn(wins, vv, inside):
    xg, xv, da = wins
    wg, wv = vv
    lg, lv = _lags(xg), _lags(xv)
    ug = _conv(lg, wg)
    uv = _conv(lv, wv)
    sg = _sigmoid(ug)
    dug = da * uv * (sg * (1.0 + ug * (1.0 - sg)))
    duv = da * (ug * sg)
    return [_conv_t(dug, wg), _conv_t(duv, wv)], _tap_sums(dug, lg, inside) + _tap_sums(duv, lv, inside)


def _sc_fwd_fn(wins, vv, inside):
    gb, gc, hh = wins
    w, = vv
    return [gb * _conv(_lags(gc * hh), w)], []


def _sc_bwd_fn(wins, vv, inside):
    gb, gc, hh, dy = wins
    w, = vv
    lz = _lags(gc * hh)
    dcv = dy * gb
    dz = _conv_t(dcv, w)
    return [dy * _conv(lz, w), dz * hh, dz * gc], _tap_sums(dcv, lz, inside)


def _gates(qr, fr, lb):
    sg = _sigmoid(fr)
    f = lb + (1.0 - lb) * sg
    sq = _sigmoid(qr)
    q = qr * sq * (HEAD ** -0.5)
    return q, 1.0 - f, jnp.log(f), f, sg, sq


def _boundary_rows(b, g, row):
    c = b.shape[0]
    if 2 * g >= SUBLANES:
        x = b.reshape(c // (2 * g), 2 * g, LANES)
        return jnp.broadcast_to(x[:, g - 1:g, :], x.shape).reshape(c, LANES)
    x = b.reshape(c // SUBLANES, SUBLANES, LANES)
    lo = jnp.broadcast_to(x[:, 1:2, :], x.shape).reshape(c, LANES)
    hi = jnp.broadcast_to(x[:, 5:6, :], x.shape).reshape(c, LANES)
    return jnp.where((row & 4) == 0, lo, hi)


def _chunk_decays(gl, f, row):
    c = gl.shape[0]
    b = gl
    d = 1
    while d < c:
        b = b + jnp.where(row >= d, pltpu.roll(b, d, 0), 0.0)
        d *= 2
    eq, ek = [], []
    g = c // 2
    while g >= 2:
        right = (row & g) != 0
        m = _boundary_rows(b, g, row)
        z = jnp.exp(jnp.where(right, b - m, m - b))
        eq.append(jnp.where(right, z, 0.0))
        ek.append(jnp.where(right, 0.0, z))
        g //= 2
    odd = (row & 1) != 0
    eq.append(jnp.where(odd, f, 0.0))
    ek.append(jnp.where(odd, 0.0, 1.0))
    return b, eq, ek


def _intra(q, k, eq, ek, tt, ss):
    c = q.shape[0]
    qs, ks = [], []
    a = jnp.where(tt == ss, jnp.sum(q * k, axis=1, keepdims=True), 0.0)
    g = c // 2
    for e_q, e_k in zip(eq, ek):
        qg = (q * e_q).astype(BF16)
        kg = (k * e_k).astype(BF16)
        p = lax.dot_general(qg, kg, (((1,), (1,)), ((), ())), preferred_element_type=F32)
        a = a + (p if 2 * g >= c else jnp.where((tt ^ ss) < 2 * g, p, 0.0))
        qs.append(qg)
        ks.append(kg)
        g //= 2
    return a, qs, ks


def _hgrn_fwd(proj, lb, d_model):
    T = proj.shape[0]
    H = d_model // HEAD
    nch = T // CHUNK

    def body(q_ref, f_ref, v_ref, lb_ref, o_ref, s_ref):
        lbv = lb_ref[...]
        row = lax.broadcasted_iota(jnp.int32, (CHUNK, HEAD), 0)
        tt = lax.broadcasted_iota(jnp.int32, (CHUNK, CHUNK), 0)
        ss = lax.broadcasted_iota(jnp.int32, (CHUNK, CHUNK), 1)

        def step(i, st):
            sl = pl.ds(pl.multiple_of(i * CHUNK, CHUNK), CHUNK)
            q, k, gl, f, _, _ = _gates(q_ref[sl, :], f_ref[sl, :], lbv)
            v = v_ref[sl, :].astype(BF16)
            b, eq, ek = _chunk_decays(gl, f, row)
            a, _, _ = _intra(q, k, eq, ek, tt, ss)
            bl = b[CHUNK - 1:CHUNK, :]
            q0 = (q * jnp.exp(b)).astype(BF16)
            kh = (k * jnp.exp(bl - b)).astype(BF16)
            s_ref[i] = st
            o = jnp.dot(a.astype(BF16), v, preferred_element_type=F32)
            o = o + lax.dot_general(q0, st.astype(BF16), (((1,), (1,)), ((), ())), preferred_element_type=F32)
            o_ref[sl, :] = o
            return jnp.exp(bl) * st + lax.dot_general(v, kh, (((0,), (0,)), ((), ())), preferred_element_type=F32)

        per = 4 if nch % 4 == 0 else 2

        def trip(i, st):
            for u in range(per):
                st = step(per * i + u, st)
            return st

        lax.fori_loop(0, nch // per, trip, jnp.zeros((HEAD, HEAD), F32))

    col = lambda off: pl.BlockSpec((T, HEAD), functools.partial(lambda h, off: (0, off + h), off=off))
    return _pcall(
        body, name="hgrn_fwd", grid=(H,),
        in_specs=[col(0), col(H), col(2 * H), pl.BlockSpec((1, HEAD), lambda h: (0, h))],
        out_specs=[pl.BlockSpec((T, HEAD), lambda h: (0, h)),
                   pl.BlockSpec((None, nch, HEAD, HEAD), lambda h: (h, 0, 0, 0))],
        out_shape=[jax.ShapeDtypeStruct((T, d_model), F32), jax.ShapeDtypeStruct((H, nch, HEAD, HEAD), F32)],
        compiler_params=_params(("parallel",)),
    )(proj, proj, proj, lb)


def _hgrn_bwd(proj, lb, states, do, dgate, d_model):
    T = proj.shape[0]
    H = d_model // HEAD
    nch = T // CHUNK

    def body(q_ref, f_ref, v_ref, lb_ref, s_ref, do_ref, dg_ref, dp_ref, dlb_ref):
        dq_ref, df_ref, dv_ref = dp_ref.at[0], dp_ref.at[1], dp_ref.at[2]
        dp_ref[3] = dg_ref[...]
        lbv = lb_ref[...]
        row = lax.broadcasted_iota(jnp.int32, (CHUNK, HEAD), 0)
        tt = lax.broadcasted_iota(jnp.int32, (CHUNK, CHUNK), 0)
        ss = lax.broadcasted_iota(jnp.int32, (CHUNK, CHUNK), 1)
        last = row == CHUNK - 1
        nt = (((1,), (1,)), ((), ()))
        tn = (((0,), (0,)), ((), ()))

        def step(j, carry):
            dst, dlb = carry
            i = nch - 1 - j
            sl = pl.ds(pl.multiple_of(i * CHUNK, CHUNK), CHUNK)
            qr = q_ref[sl, :]
            q, k, gl, f, sg, sq = _gates(qr, f_ref[sl, :], lbv)
            v = v_ref[sl, :].astype(BF16)
            d_o = do_ref[sl, :].astype(BF16)
            st = s_ref[i]
            st16 = st.astype(BF16)
            dst16 = dst.astype(BF16)
            b, eq, ek = _chunk_decays(gl, f, row)
            a, qs, ks = _intra(q, k, eq, ek, tt, ss)
            bl = b[CHUNK - 1:CHUNK, :]
            e0 = jnp.exp(b)
            eh = jnp.exp(bl - b)
            ebl = jnp.exp(bl)
            q0 = q * e0
            kh = k * eh
            q016 = q0.astype(BF16)
            kh16 = kh.astype(BF16)
            dv = lax.dot_general(a.astype(BF16), d_o, tn, preferred_element_type=F32)
            dv = dv + lax.dot_general(kh16, dst16, nt, preferred_element_type=F32)
            dv_ref[sl, :] = dv.astype(dv_ref.dtype)
            da = lax.dot_general(d_o, v, nt, preferred_element_type=F32)
            da = jnp.where(tt >= ss, da, 0.0)
            dd = jnp.sum(jnp.where(tt == ss, da, 0.0), axis=1, keepdims=True)
            dq0 = jnp.dot(d_o, st16, preferred_element_type=F32)
            dkh = jnp.dot(v, dst16, preferred_element_type=F32)
            dq = dq0 * e0 + dd * k
            dk = dkh * eh + dd * q
            db = dq0 * q016.astype(F32) - dkh * kh16.astype(F32)
            g = CHUNK // 2
            for e_q, e_k, qg, kg in zip(eq, ek, qs, ks):
                dag = (da if 2 * g >= CHUNK else jnp.where((tt ^ ss) < 2 * g, da, 0.0)).astype(BF16)
                dqg = jnp.dot(dag, kg, preferred_element_type=F32)
                dkg = lax.dot_general(dag, qg, tn, preferred_element_type=F32)
                dq = dq + dqg * e_q
                dk = dk + dkg * e_k
                db = db + (dqg * qg.astype(F32) - dkg * kg.astype(F32))
                g //= 2
            dbl = jnp.sum(dkh * kh16.astype(F32), axis=0, keepdims=True) + ebl * jnp.sum(dst * st, axis=0, keepdims=True)
            db = db + jnp.where(last, dbl, 0.0)
            d = 1
            while d < CHUNK:
                db = db + jnp.where(row < CHUNK - d, pltpu.roll(db, CHUNK - d, 0), 0.0)
                d *= 2
            dfg = db / f - dk
            df_ref[sl, :] = (dfg * (1.0 - lbv) * sg * (1.0 - sg)).astype(df_ref.dtype)
            dq_ref[sl, :] = (dq * (HEAD ** -0.5) * (sq * (1.0 + qr * (1.0 - sq)))).astype(dq_ref.dtype)
            dlb = dlb + jnp.sum(dfg * (1.0 - sg), axis=0, keepdims=True)
            dst = ebl * dst + lax.dot_general(d_o, q016, tn, preferred_element_type=F32)
            return dst, dlb

        _, dlb = lax.fori_loop(0, nch // 2, lambda j, cr: step(2 * j + 1, step(2 * j, cr)),
                               (jnp.zeros((HEAD, HEAD), F32), jnp.zeros((1, HEAD), F32)))
        arow = lax.broadcasted_iota(jnp.int32, (SUBLANES, HEAD), 0)
        dlb_ref[...] = jnp.where(arow == 0, dlb, 0.0)

    col = lambda off: pl.BlockSpec((T, HEAD), functools.partial(lambda h, off: (0, off + h), off=off))
    return _pcall(
        body, name="hgrn_bwd", grid=(H,),
        in_specs=[col(0), col(H), col(2 * H), pl.BlockSpec((1, HEAD), lambda h: (0, h)),
                  pl.BlockSpec((None, nch, HEAD, HEAD), lambda h: (h, 0, 0, 0)), col(0), col(0)],
        out_specs=[pl.BlockSpec((4, T, HEAD), lambda h: (0, 0, h)), pl.BlockSpec((SUBLANES, HEAD), lambda h: (0, h))],
        out_shape=[jax.ShapeDtypeStruct((4, T, d_model), BF16), jax.ShapeDtypeStruct((SUBLANES, d_model), F32)],
        compiler_params=_params(("parallel",)),
    )(proj, proj, proj, lb, states, do, dgate)


def _lb_softmax(table):
    n, f = table.shape

    def body(t_ref, p_ref):
        t = t_ref[...]
        e = jnp.exp(t - jnp.max(t, axis=0, keepdims=True))
        p_ref[...] = e / jnp.sum(e, axis=0, keepdims=True)

    padded = jnp.pad(table, ((0, SUBLANES - n), (0, 0)), constant_values=-jnp.inf)
    return _pcall(body, name="lb_softmax", out_shape=jax.ShapeDtypeStruct((SUBLANES, f), F32))(padded)


def _adamw_math(w, g, m, v):
    m = ADAM_B1 * m + (1.0 - ADAM_B1) * g
    v = ADAM_B2 * v + (1.0 - ADAM_B2) * (g * g)
    m_hat = m / (1.0 - ADAM_B1 ** ADAM_STEP)
    v_hat = v / (1.0 - ADAM_B2 ** ADAM_STEP)
    delta = -ADAM_LR * (m_hat / (jnp.sqrt(v_hat) + ADAM_EPS) + ADAM_WD * w)
    return delta, m, v


def _adamw(w, g, m, v, name):
    R, C = w.shape
    tr = _pick(R, (128, 64, 32, 16, 8))

    def body(w_ref, g_ref, m_ref, v_ref, d_ref, nm_ref, nv_ref):
        d, nm, nv = _adamw_math(w_ref[...], g_ref[...], m_ref[...], v_ref[...])
        d_ref[...] = d
        nm_ref[...] = nm
        nv_ref[...] = nv

    spec = pl.BlockSpec((tr, C), lambda i: (i, 0))
    return _pcall(
        body, name=name, grid=(R // tr,), in_specs=[spec] * 4, out_specs=[spec] * 3,
        out_shape=[jax.ShapeDtypeStruct((R, C), F32)] * 3, compiler_params=_params(("parallel",)),
    )(w, g, m, v)


def _adamw_halves(w, m, v, g_mine, g_recv, c, name, layer=0, prev=None):
    C = w.shape[1]
    rh = g_mine.shape[0]
    tr = _pick(rh, (128, 64, 32, 16, 8))
    nb = rh // tr
    r0 = layer * 2 * nb

    def body(c_ref, w_ref, m_ref, v_ref, gm_ref, gr_ref, *rest):
        g_ref, d_ref, nm_ref, nv_ref = rest[-4:]
        g = jnp.where(pl.program_id(0) == c_ref[0], gm_ref[...], gr_ref[...])
        d, nm, nv = _adamw_math(w_ref[...], g, m_ref[...], v_ref[...])
        g_ref[...] = g
        d_ref[...] = d
        nm_ref[...] = nm
        nv_ref[...] = nv

    full = pl.BlockSpec((tr, C), lambda h, i, cr: (r0 + h * nb + i, 0))
    half = pl.BlockSpec((tr, C), lambda h, i, cr: (i, 0))
    in_specs = [full, full, full, half, half]
    args = [c, w, m, v, g_mine, g_recv]
    alias = {}
    if prev is not None:
        in_specs += [pl.BlockSpec(memory_space=pl.ANY)] * 4
        args += list(prev)
        alias = {6 + k: k for k in range(4)}
    return _pcall(
        body, name=name,
        grid_spec=pltpu.PrefetchScalarGridSpec(
            num_scalar_prefetch=1, grid=(2, nb), in_specs=in_specs, out_specs=[full] * 4),
        out_shape=[jax.ShapeDtypeStruct(w.shape, F32)] * 4, input_output_aliases=alias,
        compiler_params=_params(("parallel", "parallel")),
    )(*args)


def _lb_table_grad(p8, dlb, n):
    f = p8.shape[1]

    def body(p_ref, d_ref, o_ref):
        p = p_ref[...]
        d = d_ref[...]
        p0 = p[0:1, :]
        first = lax.broadcasted_iota(jnp.int32, p.shape, 0) == 0
        o_ref[...] = p * (jnp.where(first, d, 0.0) - d * p0)

    return _pcall(body, name="lb_table_grad", out_shape=jax.ShapeDtypeStruct((SUBLANES, f), F32))(p8, dlb)[:n]


def _place():
    x, y, c = lax.axis_index("x"), lax.axis_index("y"), lax.axis_index("c")
    chips = [(1 - x, y), (x, 1 - y), (1 - x, 1 - y)]
    return x, y, c, chips


HBM_SPEC = pl.BlockSpec(memory_space=pltpu.HBM)


def _gather_weights(big, small):
    nb, ns = len(big), len(small)
    n = nb + ns

    def body(*refs):
        ins, outs = refs[:n], refs[n:2 * n]
        send_sems, recv_sems, own_send, own_recv = refs[2 * n:]
        x, y, c, chips = _place()
        me = 2 * x + y
        sib = (x, y, 1 - c)
        own = [pltpu.make_async_remote_copy(
            src_ref=ins[t], dst_ref=outs[t].at[me], send_sem=own_send.at[t], recv_sem=own_recv.at[t],
            device_id=sib, device_id_type=MESH) for t in range(n)]
        for cp in own:
            cp.start()

        def half(t, h):
            rh = big[t].shape[0] // 2
            return pl.ds(pl.multiple_of(h * rh, rh), rh)

        sends = []
        for t in range(n):
            for j, chip in enumerate(chips):
                k = 6 * t + j
                if t < nb:
                    src, dst = ins[t].at[half(t, c)], outs[t].at[me, half(t, c)]
                else:
                    src, dst = ins[t], outs[t].at[me]
                sends.append(pltpu.make_async_remote_copy(
                    src_ref=src, dst_ref=dst, send_sem=send_sems.at[k], recv_sem=recv_sems.at[k],
                    device_id=(*chip, c), device_id_type=MESH))
        for cp in sends:
            cp.start()
        passed = []
        for t in range(n):
            for j, (cx, cy) in enumerate(chips):
                k = 6 * t + j
                s = 2 * cx + cy
                if t < nb:
                    landed = outs[t].at[s, half(t, c)]
                    pltpu.make_async_remote_copy(
                        src_ref=landed, dst_ref=landed, send_sem=send_sems.at[k], recv_sem=recv_sems.at[k],
                        device_id=sib, device_id_type=MESH).wait_recv()
                    fwd = pltpu.make_async_remote_copy(
                        src_ref=landed, dst_ref=landed, send_sem=send_sems.at[k + 3], recv_sem=recv_sems.at[k + 3],
                        device_id=sib, device_id_type=MESH)
                    fwd.start()
                    passed.append(fwd)
                else:
                    landed = outs[t].at[s]
                    pltpu.make_async_remote_copy(
                        src_ref=landed, dst_ref=landed, send_sem=send_sems.at[k], recv_sem=recv_sems.at[k],
                        device_id=sib, device_id_type=MESH).wait_recv()
        for t in range(nb):
            for j, (cx, cy) in enumerate(chips):
                k = 6 * t + j
                other = outs[t].at[2 * cx + cy, half(t, 1 - c)]
                pltpu.make_async_remote_copy(
                    src_ref=other, dst_ref=other, send_sem=send_sems.at[k + 3], recv_sem=recv_sems.at[k + 3],
                    device_id=sib, device_id_type=MESH).wait_recv()
        for cp in sends + passed:
            cp.wait_send()
        for cp in own:
            cp.wait()

    arrs = list(big) + list(small)
    return _pcall(
        body, name="gather_weights", in_specs=[HBM_SPEC] * n, out_specs=[HBM_SPEC] * n,
        out_shape=[jax.ShapeDtypeStruct((N_CHIPS,) + a.shape, a.dtype) for a in arrs],
        scratch_shapes=[pltpu.SemaphoreType.DMA((6 * n,)), pltpu.SemaphoreType.DMA((6 * n,)),
                        pltpu.SemaphoreType.DMA((n,)), pltpu.SemaphoreType.DMA((n,))],
    )(*arrs)


SEM_SPEC = pl.BlockSpec(memory_space=pltpu.SEMAPHORE)
DATAFLOW = pltpu.SideEffectType.DATAFLOW_SIDE_EFFECTING
COPIES_PER_SHARD = 4


def _shard_copies(ins, lands, send_sems, recv_sems):
    x, y, c, chips = _place()
    me = 2 * x + y
    cps = []
    for t in range(len(ins)):
        rh = ins[t].shape[0] // 2
        half = pl.ds(pl.multiple_of(c * rh, rh), rh)
        for j, chip in enumerate(chips):
            k = COPIES_PER_SHARD * t + j
            cps.append(pltpu.make_async_remote_copy(
                src_ref=ins[t].at[half], dst_ref=lands[t].at[me, half], send_sem=send_sems.at[k],
                recv_sem=recv_sems.at[k], device_id=(*chip, c), device_id_type=MESH))
        k = COPIES_PER_SHARD * t + 3
        cps.append(pltpu.make_async_remote_copy(
            src_ref=ins[t], dst_ref=lands[t].at[me], send_sem=send_sems.at[k], recv_sem=recv_sems.at[k],
            device_id=(x, y, 1 - c), device_id_type=MESH))
    return cps


def _gather_start(shards, thru, name):
    n = len(shards)
    nops = 2 * n + len(thru)

    def body(*refs):
        ins, lands = refs[:n], refs[n:2 * n]
        send_sems, recv_sems = refs[nops], refs[nops + 1]
        for cp in _shard_copies(ins, lands, send_sems, recv_sems):
            cp.start()

    lands = [pltpu.with_memory_space_constraint(lax.empty((N_CHIPS,) + s.shape, s.dtype), pltpu.HBM) for s in shards]
    ops = [pltpu.with_memory_space_constraint(s, pltpu.HBM) for s in shards] + lands + list(thru)
    nsem = COPIES_PER_SHARD * n
    res = _pcall(
        body, name=name, in_specs=[HBM_SPEC] * nops,
        out_specs=[SEM_SPEC, SEM_SPEC] + [HBM_SPEC] * nops,
        out_shape=[pltpu.SemaphoreType.DMA((nsem,)), pltpu.SemaphoreType.DMA((nsem,))]
        + [pltpu.HBM(o.shape, o.dtype) for o in ops],
        input_output_aliases={i: 2 + i for i in range(nops)},
        compiler_params=pltpu.CompilerParams(has_side_effects=DATAFLOW),
    )(*ops)
    return res[0], res[1], res[2:2 + n], res[2 + n:2 + 2 * n], list(res[2 + 2 * n:])


def _gather_wait(send_sems, recv_sems, shards, lands, after, name, which=None):
    n = len(shards)

    def body(*refs):
        ins, lnd = refs[:n], refs[n:2 * n]
        ssem, rsem = refs[2 * n], refs[2 * n + 1]
        for k, cp in enumerate(_shard_copies(ins, lnd, ssem, rsem)):
            if which is None or k % COPIES_PER_SHARD in which:
                cp.wait_send()
                cp.wait_recv()

    res = _pcall(
        body, name=name,
        in_specs=[HBM_SPEC] * (2 * n) + [SEM_SPEC, SEM_SPEC, pl.BlockSpec(memory_space=pl.ANY)],
        out_specs=[HBM_SPEC] * (2 * n),
        out_shape=[pltpu.HBM(o.shape, o.dtype) for o in list(shards) + list(lands)],
        input_output_aliases={i: i for i in range(2 * n)},
        compiler_params=pltpu.CompilerParams(has_side_effects=DATAFLOW),
    )(*shards, *lands, send_sems, recv_sems, after)
    return res[:n], res[n:]


def _forward_copies(land, send_sems, recv_sems, which=(0, 1, 2)):
    x, y, c, chips = _place()
    rh = land.shape[1] // 2
    return [pltpu.make_async_remote_copy(
        src_ref=land.at[2 * cx + cy, pl.ds(pl.multiple_of(c * rh, rh), rh)],
        dst_ref=land.at[2 * cx + cy, pl.ds(pl.multiple_of(c * rh, rh), rh)],
        send_sem=send_sems.at[j], recv_sem=recv_sems.at[j], device_id=(x, y, 1 - c), device_id_type=MESH)
        for j, (cx, cy) in enumerate(chips) if j in which]


def _forward_start(land, thru, name, which=(0, 1, 2)):
    def body(land_ref, thru_ref, send_sems, recv_sems, out_ref, thru_out):
        for cp in _forward_copies(land_ref, send_sems, recv_sems, which):
            cp.start()

    return _pcall(
        body, name=name, in_specs=[HBM_SPEC, HBM_SPEC], out_specs=[SEM_SPEC, SEM_SPEC, HBM_SPEC, HBM_SPEC],
        out_shape=[pltpu.SemaphoreType.DMA((3,)), pltpu.SemaphoreType.DMA((3,)), pltpu.HBM(land.shape, land.dtype),
                   pltpu.HBM(thru.shape, thru.dtype)],
        input_output_aliases={0: 2, 1: 3}, compiler_params=pltpu.CompilerParams(has_side_effects=DATAFLOW),
    )(land, thru)


def _forward_wait(send_sems, recv_sems, land, after, name, which=(0, 1, 2)):
    def body(land_ref, ssem, rsem, after_ref, out_ref):
        for cp in _forward_copies(land_ref, ssem, rsem, which):
            cp.wait_send()
            cp.wait_recv()

    return _pcall(
        body, name=name, in_specs=[HBM_SPEC, SEM_SPEC, SEM_SPEC, pl.BlockSpec(memory_space=pl.ANY)],
        out_specs=HBM_SPEC, out_shape=pltpu.HBM(land.shape, land.dtype), input_output_aliases={0: 0},
        compiler_params=pltpu.CompilerParams(has_side_effects=DATAFLOW),
    )(land, send_sems, recv_sems, after)


def _sibling_copies(ins, lands, send_sems, recv_sems, other_half):
    x, y, c, _ = _place()
    return [pltpu.make_async_remote_copy(
        src_ref=ins[t].at[:, 1 - c] if other_half else ins[t], dst_ref=lands[t], send_sem=send_sems.at[t],
        recv_sem=recv_sems.at[t], device_id=(x, y, 1 - c), device_id_type=MESH) for t in range(len(ins))]


def _sibling_start(srcs, other_half, thru, name):
    n = len(srcs)
    nthru = 0 if thru is None else 1

    def body(*refs):
        ins, lands = refs[:n], refs[n:2 * n]
        send_sems, recv_sems = refs[2 * n + nthru], refs[2 * n + nthru + 1]
        for cp in _sibling_copies(ins, lands, send_sems, recv_sems, other_half):
            cp.start()

    shapes = [(s.shape[0],) + s.shape[2:] if other_half else s.shape for s in srcs]
    lands = [pltpu.with_memory_space_constraint(lax.empty(sh, s.dtype), pltpu.HBM) for sh, s in zip(shapes, srcs)]
    ops = [pltpu.with_memory_space_constraint(s, pltpu.HBM) for s in srcs] + lands + ([] if thru is None else [thru])
    res = _pcall(
        body, name=name, in_specs=[HBM_SPEC] * len(ops),
        out_specs=[SEM_SPEC, SEM_SPEC] + [HBM_SPEC] * len(ops),
        out_shape=[pltpu.SemaphoreType.DMA((n,)), pltpu.SemaphoreType.DMA((n,))]
        + [pltpu.HBM(o.shape, o.dtype) for o in ops],
        input_output_aliases={i: 2 + i for i in range(len(ops))},
        compiler_params=pltpu.CompilerParams(has_side_effects=DATAFLOW),
    )(*ops)
    return res[0], res[1], res[2:2 + n], res[2 + n:2 + 2 * n], (None if thru is None else res[2 + 2 * n])


def _sibling_wait(send_sems, recv_sems, srcs, lands, other_half, after, name):
    n = len(srcs)

    def body(*refs):
        ins, lnd = refs[:n], refs[n:2 * n]
        ssem, rsem = refs[2 * n], refs[2 * n + 1]
        for cp in _sibling_copies(ins, lnd, ssem, rsem, other_half):
            cp.wait_send()
            cp.wait_recv()

    res = _pcall(
        body, name=name,
        in_specs=[HBM_SPEC] * (2 * n) + [SEM_SPEC, SEM_SPEC, pl.BlockSpec(memory_space=pl.ANY)],
        out_specs=[HBM_SPEC] * (2 * n),
        out_shape=[pltpu.HBM(o.shape, o.dtype) for o in list(srcs) + list(lands)],
        input_output_aliases={i: i for i in range(2 * n)},
        compiler_params=pltpu.CompilerParams(has_side_effects=DATAFLOW),
    )(*srcs, *lands, send_sems, recv_sems, after)
    return res[:n], res[n:]


def _chip_copies(ins, lands, send_sems, recv_sems):
    x, y, c, chips = _place()
    cps = []
    for t in range(len(ins)):
        for j, (cx, cy) in enumerate(chips):
            cps.append(pltpu.make_async_remote_copy(
                src_ref=ins[t].at[2 * cx + cy], dst_ref=lands[t].at[j],
                send_sem=send_sems.at[3 * t + j], recv_sem=recv_sems.at[3 * t + j],
                device_id=(cx, cy, c), device_id_type=MESH))
    return cps


def _chip_start(parts, thru, name):
    n = len(parts)

    def body(*refs):
        ins, lands = refs[:n], refs[n:2 * n]
        send_sems, recv_sems = refs[2 * n + 1], refs[2 * n + 2]
        for cp in _chip_copies(ins, lands, send_sems, recv_sems):
            cp.start()

    lands = [pltpu.with_memory_space_constraint(lax.empty((3,) + p.shape[1:], p.dtype), pltpu.HBM) for p in parts]
    ops = [pltpu.with_memory_space_constraint(p, pltpu.HBM) for p in parts] + lands + [thru]
    res = _pcall(
        body, name=name, in_specs=[HBM_SPEC] * (2 * n + 1),
        out_specs=[SEM_SPEC, SEM_SPEC] + [HBM_SPEC] * (2 * n + 1),
        out_shape=[pltpu.SemaphoreType.DMA((3 * n,)), pltpu.SemaphoreType.DMA((3 * n,))]
        + [pltpu.HBM(o.shape, o.dtype) for o in ops],
        input_output_aliases={i: 2 + i for i in range(2 * n + 1)},
        compiler_params=pltpu.CompilerParams(has_side_effects=DATAFLOW),
    )(*ops)
    return res[0], res[1], res[2:2 + n], res[2 + n:2 + 2 * n], res[2 + 2 * n]


def _chip_wait(send_sems, recv_sems, parts, lands, after, name):
    n = len(parts)

    def body(*refs):
        ins, lnd = refs[:n], refs[n:2 * n]
        ssem, rsem = refs[2 * n], refs[2 * n + 1]
        for cp in _chip_copies(ins, lnd, ssem, rsem):
            cp.wait_send()
            cp.wait_recv()

    res = _pcall(
        body, name=name,
        in_specs=[HBM_SPEC] * (2 * n) + [SEM_SPEC, SEM_SPEC, pl.BlockSpec(memory_space=pl.ANY)],
        out_specs=[HBM_SPEC] * (2 * n),
        out_shape=[pltpu.HBM(o.shape, o.dtype) for o in list(parts) + list(lands)],
        input_output_aliases={i: i for i in range(2 * n)},
        compiler_params=pltpu.CompilerParams(has_side_effects=DATAFLOW),
    )(*parts, *lands, send_sems, recv_sems, after)
    return res[:n], res[n:]


def _add_pair(grad, recv, c, name):
    s, _, rh, cc = grad.shape
    tr = _pick(rh, (256, 128, 64, 32, 16))

    def body(c_ref, g_ref, r_ref, o_ref):
        o_ref[...] = (g_ref[...].astype(F32) + r_ref[...].astype(F32)).astype(o_ref.dtype)

    return _pcall(
        body, name=name,
        grid_spec=pltpu.PrefetchScalarGridSpec(
            num_scalar_prefetch=1, grid=(s, rh // tr),
            in_specs=[pl.BlockSpec((None, None, tr, cc), lambda a, i, cr: (a, cr[0], i, 0)),
                      pl.BlockSpec((None, tr, cc), lambda a, i, cr: (a, i, 0))],
            out_specs=pl.BlockSpec((None, tr, cc), lambda a, i, cr: (a, i, 0))),
        out_shape=jax.ShapeDtypeStruct((s, rh, cc), BF16),
        compiler_params=_params(("parallel", "parallel")),
    )(c, grad, recv)


def _add_chips(part, recv, me, name):
    _, rh, cc = part.shape
    tr = _pick(rh, (256, 128, 64, 32, 16))

    def body(m_ref, p_ref, r_ref, o_ref):
        o_ref[...] = ((p_ref[...].astype(F32) + r_ref[0].astype(F32)) + r_ref[1].astype(F32)) + r_ref[2].astype(F32)

    return _pcall(
        body, name=name,
        grid_spec=pltpu.PrefetchScalarGridSpec(
            num_scalar_prefetch=1, grid=(rh // tr,),
            in_specs=[pl.BlockSpec((None, tr, cc), lambda i, mr: (mr[0], i, 0)),
                      pl.BlockSpec((3, tr, cc), lambda i, mr: (0, i, 0))],
            out_specs=pl.BlockSpec((tr, cc), lambda i, mr: (i, 0))),
        out_shape=jax.ShapeDtypeStruct((rh, cc), F32),
        compiler_params=_params(("parallel",)),
    )(me, part, recv)


def _all_sum(vec):
    rows = vec.shape[0]

    def body(v_ref, o_ref, buf, send_sems, recv_sems):
        x, y, c, _ = _place()
        me = 4 * x + 2 * y + c
        buf[me] = v_ref[...]
        cps = []
        for r in range(1, 8):
            fx, fy, fc = (r >> 2) & 1, (r >> 1) & 1, r & 1
            peer = (x ^ fx, y ^ fy, c ^ fc)
            cps.append(pltpu.make_async_remote_copy(
                src_ref=v_ref, dst_ref=buf.at[me], send_sem=send_sems.at[r - 1], recv_sem=recv_sems.at[r - 1],
                device_id=peer, device_id_type=MESH))
        for cp in cps:
            cp.start()
        for r in range(1, 8):
            src = me ^ r
            pltpu.make_async_remote_copy(
                src_ref=v_ref, dst_ref=buf.at[src], send_sem=send_sems.at[r - 1], recv_sem=recv_sems.at[r - 1],
                device_id=(x, y, c), device_id_type=MESH).wait_recv()
        for cp in cps:
            cp.wait_send()
        acc = buf[0]
        for d in range(1, 8):
            acc = acc + buf[d]
        o_ref[...] = acc

    return _pcall(
        body, name="all_sum_small",
        in_specs=[pl.BlockSpec(memory_space=pltpu.VMEM)], out_specs=pl.BlockSpec(memory_space=pltpu.VMEM),
        out_shape=jax.ShapeDtypeStruct((rows, LANES), F32),
        scratch_shapes=[pltpu.VMEM((8, rows, LANES), F32), pltpu.SemaphoreType.DMA((7,)), pltpu.SemaphoreType.DMA((7,))],
    )(vec)


def _pack(parts):
    flat = jnp.concatenate([p.reshape(-1) for p in parts])
    tile = SUBLANES * LANES
    pad = (-flat.shape[0]) % tile
    return jnp.pad(flat, (0, pad)).reshape(-1, LANES)


def _unpack(vec, shapes):
    flat = vec.reshape(-1)
    out, p = [], 0
    for s in shapes:
        n = 1
        for d in s:
            n *= d
        out.append(flat[p:p + n].reshape(s))
        p += n
    return out


def _local_step(x, tgt, norm_mix, norm_ffn, lb8, out_norm, final_norm, sc_conv, ffn_conv, first, arrive, reduce_start,
                reduce_finish):
    T, D = x.shape
    F2 = ffn_conv.shape[-1]
    FF = F2 // 2
    tm = _pick(T, (1024, 512, 256, 128))
    wide = (1536, 1408, 1024, 768, 512, 384, 256, 128)
    cw_h, cw_s, cw_u = 4 * D // N_CHIPS, 3 * D // N_CHIPS, F2 // N_CHIPS
    kp = FF // N_CHIPS
    tk_ff = kp if kp % LANES == 0 else LANES
    tn_d = _pick(D, (1024, 512, 256, 128))
    tk_w = _pick(D, (512, 256, 128))
    tn_h = _pick(cw_h, (1024, 512, 256, 128))
    tn_s = _pick(D // N_CHIPS, (512, 256, 128))
    tn_u = _pick(cw_u, wide)
    lb = lb8[0:1]
    wm_sq = _wmap_col(D, tn_d, 0)
    wm_sq1 = _wmap_col(D, D, 0)
    seg1 = lambda a: a.reshape((1,) + a.shape)

    def mix_in(h, w):
        return _row_call(_rms_fwd_fn, [(h, 0, D)], [w], [(D, BF16)], 0, "rms_fwd")[0]

    def rms_bwd(h, dxn, dh, w):
        return _row_call(_rms_bwd_fn, [(h, 0, D), (dxn, 0, D), (dh, 0, D)], [w], [(D, F32), (D, BF16)], 1, "rms_bwd")

    def ffn_fwd(h, i, fetch_up, behind_down=()):
        xn = mix_in(h, norm_ffn[i:i + 1])
        tn = _pick(cw_u, wide)
        fetch_down, xn = arrive("ffn_w_down%d" % i, xn)
        w_up = fetch_up(xn)
        up = _mm_nn(xn, w_up, _wmap_col(cw_u, tn, 0), D, F2, tm, D, tn, "ffn_up")
        nb = FF // LANES
        a = _col_call(_glu_fwd_fn, [(up, 0), (up, nb)], [(ffn_conv[i], 0), (ffn_conv[i], nb)], [(1, FF, BF16)], 0,
                      "glu_fwd", before=True, after=False)[0][0]
        later = []
        for name in behind_down:
            fetch, a = arrive(name, a)
            later.append(fetch)
        w_down = fetch_down(a)
        h2 = _mm_nn(a, w_down, _wmap_row(kp, tk_ff, 0), FF, D, tm, tk_ff, tn_d, "ffn_down", res=h)
        return h2, (xn, up, a), w_up, w_down, later

    def ffn_bwd(dh, dh16, h, saved, i, w_up, w_down):
        xn, up, a = saved
        g_down = _mm_tn(a, seg1(dh16), (N_CHIPS, 1, kp, D), _wmap_row(kp, tk_ff, 0), FF, D, tk_ff, tn_d,
                        "ffn_down_dw", tm=_pick(T, (2048, 1024, 512, 256, 128)))
        dh16 = reduce_start(("ffn_w_down", i), g_down, dh16)
        da = _mm_nt(seg1(dh16), w_down, _wmap_row(kp, tk_ff, 0), FF, D, tm, tk_ff, D, "ffn_down_dx")
        nb = FF // LANES
        dgv, cg, cv = _col_call(_glu_bwd_fn, [(up, 0), (up, nb), (da, 0)], [(ffn_conv[i], 0), (ffn_conv[i], nb)],
                                [(2, FF, BF16)], 2, "glu_bwd", before=True, after=True)
        g_up = _mm_tn(xn, dgv, (N_CHIPS, 1, D, cw_u), _wmap_col(cw_u, tn_u, 0), D, F2, tk_w, tn_u, "ffn_up_dw")
        dgv = reduce_start(("ffn_w_up", i), g_up, dgv)
        dxn = _mm_nt(dgv, w_up, _wmap_col(cw_u, tn_u, 0), D, F2, tm, D, tn_u, "ffn_up_dx")
        dh2, dh2_16, dnw = rms_bwd(h, dxn, dh, norm_ffn[i:i + 1])
        return dh2, reduce_finish(dh2_16), dnw, jnp.concatenate([cg[:3], cv[:3]], axis=1)

    h0 = x
    xn0 = mix_in(h0, norm_mix[0:1])
    proj, w_hin = first(xn0, lambda w, s, prev: _mm_nn_shard(xn0, w, s, tm, tn_h, "hgrn_in", prev))
    o, states = _hgrn_fwd(proj, lb, D)
    fetch_hout, o = arrive("hgrn_w_out", o)
    on = _row_call(_onorm_fwd_fn, [(o, 0, D), (proj, 3, D)], [out_norm], [(D, BF16)], 0, "onorm_fwd")[0]
    fetch_up0, on = arrive("ffn_w_up0", on)
    w_hout1 = fetch_hout(on).reshape(1, D, D)
    h1 = _mm_nn(on, w_hout1, wm_sq, D, D, tm, D, tn_d, "hgrn_out", res=h0)
    h2, ffn0, w_up0, w_down0, (fetch_sin, fetch_sout) = ffn_fwd(h1, 0, fetch_up0, ("sc_w_in", "sc_w_out"))
    xn1 = mix_in(h2, norm_mix[1:2])
    w_sin = fetch_sin(xn1)
    tn_si = _pick(cw_s, wide)
    sproj = _mm_nn(xn1, w_sin, _wmap_col(cw_s, tn_si, 0), D, 3 * D, tm, D, tn_si, "sc_in")
    fetch_up1, sproj = arrive("ffn_w_up1", sproj)
    nd = D // LANES
    ysc = _col_call(_sc_fwd_fn, [(sproj, 0), (sproj, nd), (sproj, 2 * nd)], [(sc_conv, 0)], [(1, D, BF16)], 0,
                    "sc_fwd", before=True, after=False)[0][0]
    w_sout1 = fetch_sout(ysc).reshape(1, D, D)
    h3 = _mm_nn(ysc, w_sout1, wm_sq, D, D, tm, D, tn_d, "sc_out", res=h2)
    h4, ffn1, w_up1, w_down1, _ = ffn_fwd(h3, 1, fetch_up1)

    dh, dh16, esq, dfinal = _row_call(_final_fn, [(h4, 0, D), (tgt, 0, D)], [final_norm], [(D, F32), (D, BF16)], 2,
                                      "final_loss")
    loss = 0.5 / D * jnp.sum(esq)
    dh, dh16, dnf1, dconv1 = ffn_bwd(dh, dh16, h3, ffn1, 1, w_up1, w_down1)
    g_sout = _mm_tn(ysc, seg1(dh16), (1, 1, D, D), wm_sq, D, D, tk_w, tn_d, "sc_out_dw")
    dh16 = reduce_start(("sc_w_out", 0), g_sout, dh16)
    dy = _mm_nt(seg1(dh16), w_sout1, wm_sq1, D, D, tm, D, D, "sc_out_dx")
    dsp, dscc = _col_call(_sc_bwd_fn, [(sproj, 0), (sproj, nd), (sproj, 2 * nd), (dy, 0)], [(sc_conv, 0)],
                          [(3, D, BF16)], 1, "sc_bwd", before=True, after=True)
    g_sin = _mm_tn(xn1, dsp, (N_CHIPS, 1, D, cw_s), _wmap_col(cw_s, tn_s, 0), D, 3 * D, tk_w, tn_s, "sc_in_dw")
    dsp = reduce_start(("sc_w_in", 0), g_sin, dsp)
    dxn = _mm_nt(dsp, w_sin, _wmap_col(cw_s, tn_s, 0), D, 3 * D, tm, D, tn_s, "sc_in_dx", per_step=3)
    dh, dh16, dnm1 = rms_bwd(h2, dxn, dh, norm_mix[1:2])
    dh16 = reduce_finish(dh16)
    dh, dh16, dnf0, dconv0 = ffn_bwd(dh, dh16, h1, ffn0, 0, w_up0, w_down0)
    g_hout = _mm_tn(on, seg1(dh16), (1, 1, D, D), wm_sq, D, D, tk_w, tn_d, "hgrn_out_dw")
    dh16 = reduce_start(("hgrn_w_out", 0), g_hout, dh16)
    don = _mm_nt(seg1(dh16), w_hout1, wm_sq1, D, D, tm, D, D, "hgrn_out_dx")
    do, dgate, dgain = _row_call(_onorm_bwd_fn, [(o, 0, D), (proj, 3, D), (don, 0, D)], [out_norm],
                                 [(D, F32), (D, BF16)], 1, "onorm_bwd")
    dproj, dlb = _hgrn_bwd(proj, lb, states, do, dgate, D)
    g_hin = _mm_tn(xn0, dproj, (N_CHIPS, 1, D, cw_h), _wmap_col(cw_h, tn_h, 0), D, 4 * D, tk_w, tn_h, "hgrn_in_dw")
    dproj = reduce_start(("hgrn_w_in", 0), g_hin, dproj)
    dxn = _mm_nt(dproj, w_hin, _wmap_col(cw_h, tn_h, 0), D, 4 * D, tm, D, tn_h, "hgrn_in_dx", per_step=2)
    grad_x, _, dnm0 = rms_bwd(h0, dxn, dh, norm_mix[0:1])

    small = dict(
        loss=loss,
        norm_mix=jnp.stack([jnp.sum(dnm0, axis=0), jnp.sum(dnm1, axis=0)]),
        norm_ffn=jnp.stack([jnp.sum(dnf0, axis=0), jnp.sum(dnf1, axis=0)]),
        lb=dlb[0:1],
        out_norm=jnp.sum(dgain, axis=0)[None],
        final_norm=jnp.sum(dfinal, axis=0),
        sc_conv=dscc[:3],
        ffn_conv=jnp.stack([dconv0, dconv1]),
    )
    return grad_x, small


def kernel(x, norm_mix, norm_ffn, hgrn_w_in, hgrn_lb_table, hgrn_out_norm, hgrn_w_out, sc_w_in, sc_conv, sc_w_out, ffn_w_up, ffn_conv, ffn_w_down, final_norm, loss_target, m_norm_mix, m_norm_ffn, m_hgrn_w_in, m_hgrn_lb_table, m_hgrn_out_norm, m_hgrn_w_out, m_sc_w_in, m_sc_conv, m_sc_w_out, m_ffn_w_up, m_ffn_conv, m_ffn_w_down, m_final_norm, v_norm_mix, v_norm_ffn, v_hgrn_w_in, v_hgrn_lb_table, v_hgrn_out_norm, v_hgrn_w_out, v_sc_w_in, v_sc_conv, v_sc_w_out, v_ffn_w_up, v_ffn_conv, v_ffn_w_down, v_final_norm):
    D = x.shape[-1]
    xi, yi, ci = lax.axis_index("x"), lax.axis_index("y"), lax.axis_index("c")
    me_chip = (2 * xi + yi).astype(jnp.int32).reshape(1)
    me_core = ci.astype(jnp.int32).reshape(1)

    big_names = ["hgrn_w_in", "hgrn_w_out", "sc_w_in", "sc_w_out", "ffn_w_up", "ffn_w_down"]
    big_w = dict(hgrn_w_in=hgrn_w_in, hgrn_w_out=hgrn_w_out, sc_w_in=sc_w_in, sc_w_out=sc_w_out,
                 ffn_w_up=ffn_w_up, ffn_w_down=ffn_w_down)
    big_m = dict(hgrn_w_in=m_hgrn_w_in, hgrn_w_out=m_hgrn_w_out, sc_w_in=m_sc_w_in, sc_w_out=m_sc_w_out,
                 ffn_w_up=m_ffn_w_up, ffn_w_down=m_ffn_w_down)
    big_v = dict(hgrn_w_in=v_hgrn_w_in, hgrn_w_out=v_hgrn_w_out, sc_w_in=v_sc_w_in, sc_w_out=v_sc_w_out,
                 ffn_w_up=v_ffn_w_up, ffn_w_down=v_ffn_w_down)
    flat2 = lambda a: a.reshape(-1, a.shape[-1])

    sh = lambda a: a.reshape(-1, a.shape[-1]).astype(BF16)
    conv_shards = [flat2(sc_conv), flat2(ffn_conv)]
    in_order_of_use = [("hgrn_w_in", sh(hgrn_w_in)), ("hgrn_w_out", sh(hgrn_w_out)), ("ffn_w_up0", sh(ffn_w_up[0])),
                       ("ffn_w_down0", sh(ffn_w_down[0])), ("sc_w_in", sh(sc_w_in)), ("sc_w_out", sh(sc_w_out)),
                       ("ffn_w_up1", sh(ffn_w_up[1])), ("ffn_w_down1", sh(ffn_w_down[1]))]
    scc4, fcc4 = _gather_weights([], [flat2(sc_conv), flat2(ffn_conv)])
    gathers = {}
    thru = [scc4, norm_mix]
    for name, shard in in_order_of_use:
        ss, rs, src, land, thru = _gather_start([shard], thru, "gather_start_" + name)
        gathers[name] = (ss, rs, src, land)
    scc4, norm_mix = thru
    scc = jnp.moveaxis(scc4, 0, 1).reshape(3, D)
    f2 = ffn_conv.shape[-1] * N_CHIPS
    fcc = jnp.moveaxis(fcc4.reshape(N_CHIPS, 2, 3, -1), 0, 2).reshape(2, 3, f2)

    def first(after, matmul):
        ss, rs, src, land = gathers["hgrn_w_in"]
        src, land = _gather_wait(ss, rs, src, land, after, "gather_wait_0_own", which=(3,))
        proj = matmul(land[0], me_chip, None)
        others = [2 * (1 - xi) + yi, 2 * xi + (1 - yi), 2 * (1 - xi) + (1 - yi)]
        for j, s in enumerate(others):
            src, land = _gather_wait(ss, rs, src, land, proj, "gather_wait_0_%d" % j, which=(j,))
            fs, fr, l0, proj = _forward_start(land[0], proj, "gather_forward_start_0_%d" % j, which=(j,))
            land = [_forward_wait(fs, fr, l0, proj, "gather_forward_wait_0_%d" % j, which=(j,))]
            proj = matmul(land[0], s.astype(jnp.int32).reshape(1), proj)
        return proj, land[0]

    def arrive(name, after):
        _, (land,) = _gather_wait(*gathers[name], after, "gather_wait_" + name)
        ss, rs, land, after = _forward_start(land, after, "gather_forward_start_" + name)
        return functools.partial(_forward_wait, ss, rs, land, name="gather_forward_wait_" + name), after

    pending = []
    started = []

    def reduce_start(slot, grad, thru):
        t = sum(len(b[0]) for b in pending) + len(started)
        halves = grad.reshape(N_CHIPS, 2, -1, grad.shape[-1])
        ss, rs, src, land, thru = _sibling_start([halves], True, thru, "grad_pair_start_%d" % t)
        started.append((slot, t, ss, rs, src, land))
        return thru

    def reduce_finish(thru):
        k = len(pending)
        pair = []
        for slot, t, ss, rs, src, land in started:
            src, recv = _sibling_wait(ss, rs, src, land, True, thru, "grad_pair_wait_%d" % t)
            pair.append(_add_pair(src[0], recv[0], me_core, "grad_add_pair"))
        ss, rs, pair, land, thru = _chip_start(pair, thru, "grad_chip_start_%d" % k)
        pending.append(([s[0] for s in started], ss, rs, pair, land))
        started.clear()
        return thru

    lb8 = _lb_softmax(hgrn_lb_table)
    grad_x, small = _local_step(
        x[0], loss_target[0], norm_mix, norm_ffn, lb8, hgrn_out_norm, final_norm[None], scc, fcc, first, arrive,
        reduce_start, reduce_finish)

    small_names = ["loss", "norm_mix", "norm_ffn", "lb", "out_norm", "final_norm", "sc_conv", "ffn_conv"]
    parts = [small[n].astype(F32) for n in small_names]
    shapes = [p.shape for p in parts]
    tot = dict(zip(small_names, _unpack(reduce_finish(_all_sum(_pack(parts))), shapes)))
    loss = tot["loss"].reshape(())
    g_lb_table = _lb_table_grad(lb8, tot["lb"], hgrn_lb_table.shape[0])
    cw = sc_conv.shape[-1]
    g_sc_conv = lax.dynamic_slice_in_dim(tot["sc_conv"], me_chip[0] * cw, cw, axis=1)[None]
    cf = ffn_conv.shape[-1]
    g_ffn_conv = lax.dynamic_slice_in_dim(tot["ffn_conv"], me_chip[0] * cf, cf, axis=2)
    g_small = dict(norm_mix=tot["norm_mix"], norm_ffn=tot["norm_ffn"], hgrn_lb_table=g_lb_table,
                   hgrn_out_norm=tot["out_norm"], sc_conv=g_sc_conv, ffn_conv=g_ffn_conv, final_norm=tot["final_norm"])
    w_small = dict(norm_mix=norm_mix, norm_ffn=norm_ffn, hgrn_lb_table=hgrn_lb_table, hgrn_out_norm=hgrn_out_norm,
                   sc_conv=sc_conv, ffn_conv=ffn_conv, final_norm=final_norm)
    m_small = dict(norm_mix=m_norm_mix, norm_ffn=m_norm_ffn, hgrn_lb_table=m_hgrn_lb_table, hgrn_out_norm=m_hgrn_out_norm,
                   sc_conv=m_sc_conv, ffn_conv=m_ffn_conv, final_norm=m_final_norm)
    v_small = dict(norm_mix=v_norm_mix, norm_ffn=v_norm_ffn, hgrn_lb_table=v_hgrn_lb_table, hgrn_out_norm=v_hgrn_out_norm,
                   sc_conv=v_sc_conv, ffn_conv=v_ffn_conv, final_norm=v_final_norm)
    sm_names = list(g_small)
    sm_shapes = [w_small[n].shape for n in sm_names]
    d_s, m_s, v_s = _adamw(_pack([w_small[n] for n in sm_names]), _pack([g_small[n] for n in sm_names]),
                           _pack([m_small[n] for n in sm_names]), _pack([v_small[n] for n in sm_names]), "adamw_small")
    out_g, out_d, out_m, out_v = dict(g_small), {}, {}, {}
    for n, d_, m_, v_ in zip(sm_names, _unpack(d_s, sm_shapes), _unpack(m_s, sm_shapes), _unpack(v_s, sm_shapes)):
        out_d[n], out_m[n], out_v[n] = d_, m_, v_

    done = {}
    after = grad_x

    def add_and_share(k, after):
        slots, ss, rs, pair, land = pending[k]
        pair, recv = _chip_wait(ss, rs, pair, land, after, "grad_chip_wait_%d" % k)
        mine = [_add_chips(p, r, me_chip, "grad_add_chips") for p, r in zip(pair, recv)]
        ss, rs, mine, land, _ = _sibling_start(mine, False, None, "grad_share_start_%d" % k)
        return slots, ss, rs, mine, land

    def update(k, share, after):
        slots, ss, rs, mine, land = share
        mine, theirs = _sibling_wait(ss, rs, mine, land, False, after, "grad_share_wait_%d" % k)
        for (n, layer), gm, gr in zip(slots, mine, theirs):
            done[n] = _adamw_halves(flat2(big_w[n]), flat2(big_m[n]), flat2(big_v[n]), gm, gr, me_core, "adamw_" + n,
                                    layer=layer, prev=done.get(n))
        return done[slots[-1][0]][0]

    shares = []
    for k in range(len(pending) - 1):
        shares.append(add_and_share(k, after))
        after = shares[-1][3][0]
    for k, share in enumerate(shares):
        after = update(k, share, after)
    last = len(pending) - 1
    share = add_and_share(last, after)
    update(last, share, share[3][0])
    for n in big_names:
        out_g[n], out_d[n], out_m[n], out_v[n] = (a.reshape(big_w[n].shape) for a in done[n])

    order = ["norm_mix", "norm_ffn", "hgrn_w_in", "hgrn_lb_table", "hgrn_out_norm", "hgrn_w_out", "sc_w_in", "sc_conv",
             "sc_w_out", "ffn_w_up", "ffn_conv", "ffn_w_down", "final_norm"]
    return (loss, grad_x[None], *[out_g[n] for n in order], *[out_d[n] for n in order],
            *[out_m[n] for n in order], *[out_v[n] for n in order])
```

```python
import functools

import jax
import jax.numpy as jnp
from jax import lax
from jax.experimental import pallas as pl
from jax.experimental.pallas import tpu as pltpu

F32 = jnp.float32
BF16 = jnp.bfloat16
MESH = pl.DeviceIdType.MESH

EPS = 1e-6
CHUNK = 64
HEAD = 128
N_CHIPS = 4
ADAM_LR, ADAM_B1, ADAM_B2, ADAM_EPS, ADAM_WD, ADAM_STEP = 0.001, 0.9, 0.999, 1e-08, 0.01, 10
VMEM_LIMIT = 56 * 1024 * 1024
SUBLANES = 8
LANES = 128


def _pcall(body, **kw):
    return pl.pallas_call(body, **kw)


def _params(sem, vmem=VMEM_LIMIT):
    return pltpu.CompilerParams(dimension_semantics=sem, vmem_limit_bytes=vmem)


def _pick(dim, prefs):
    for p in prefs:
        if p <= dim and dim % p == 0:
            return p
    return dim


def _sigmoid(x):
    return 1.0 / (1.0 + jnp.exp(-x))


def _wmap_col(cw, tn, r0):
    bps = cw // tn
    return lambda kb, nb: (nb // bps, r0 + kb, nb % bps)


def _wmap_row(kp, tk, r0):
    bps = kp // tk
    return lambda kb, nb: (kb // bps, r0 + kb % bps, nb)


def _mm_nn(a, w3, wmap, K, N, tm, tk, tn, name, res=None, per_step=1):
    M = a.shape[0]
    u = per_step
    nk = K // (tk * u)

    def body(*refs):
        r_ref = None if res is None else refs[2 * u]
        o_ref = refs[2 * u + (0 if res is None else 1)]
        p = None
        for r in range(u):
            d = jnp.dot(refs[r][...], refs[u + r][...], preferred_element_type=F32)
            p = d if p is None else p + d
        if nk == 1:
            o_ref[...] = p if res is None else p + r_ref[...]
            return
        acc = refs[-1]
        k = pl.program_id(2)

        @pl.when(k == 0)
        def _():
            acc[...] = p

        @pl.when(k > 0)
        def _():
            acc[...] += p

        @pl.when(k == nk - 1)
        def _():
            o_ref[...] = acc[...] if res is None else acc[...] + r_ref[...]

    if nk == 1:
        grid = (M // tm, N // tn)
        ix = lambda f: (lambda i, j: f(i, j, 0))
        sem = ("parallel", "parallel")
        scratch = []
    else:
        grid = (M // tm, N // tn, nk)
        ix = lambda f: f
        sem = ("parallel", "parallel", "arbitrary")
        scratch = [pltpu.VMEM((tm, tn), F32)]
    def a_spec(r):
        return pl.BlockSpec((tm, tk), ix(lambda i, j, k: (i, k * u + r)))

    def w_spec(r):
        return pl.BlockSpec((None, tk, tn), ix(lambda i, j, k: wmap(k * u + r, j)))

    in_specs = [a_spec(r) for r in range(u)] + [w_spec(r) for r in range(u)]
    args = [a] * u + [w3] * u
    if res is not None:
        in_specs.append(pl.BlockSpec((tm, tn), ix(lambda i, j, k: (i, j))))
        args.append(res)
    return _pcall(
        body, name=name, grid=grid, in_specs=in_specs,
        out_specs=pl.BlockSpec((tm, tn), ix(lambda i, j, k: (i, j))),
        out_shape=jax.ShapeDtypeStruct((M, N), F32), scratch_shapes=scratch, compiler_params=_params(sem),
    )(*args)


def _mm_nn_shard(a, w3, s, tm, tn, name, prev=None):
    M, K = a.shape
    S, _, cw = w3.shape
    bps = cw // tn

    def body(s_ref, a_ref, w_ref, *rest):
        o_ref = rest[-1]
        o_ref[...] = jnp.dot(a_ref[...], w_ref[...], preferred_element_type=F32)

    in_specs = [pl.BlockSpec((tm, K), lambda i, j, sr: (i, 0)),
                pl.BlockSpec((None, K, tn), lambda i, j, sr: (sr[0], 0, j))]
    args = [s, a, w3]
    alias = {}
    if prev is not None:
        in_specs.append(pl.BlockSpec(memory_space=pl.ANY))
        args.append(prev)
        alias = {3: 0}
    return _pcall(
        body, name=name,
        grid_spec=pltpu.PrefetchScalarGridSpec(
            num_scalar_prefetch=1, grid=(M // tm, bps), in_specs=in_specs,
            out_specs=pl.BlockSpec((tm, tn), lambda i, j, sr: (i, sr[0] * bps + j))),
        out_shape=jax.ShapeDtypeStruct((M, S * cw), F32), input_output_aliases=alias,
        compiler_params=_params(("parallel", "parallel")),
    )(*args)


def _mm_nt(dy3, w3, wmap, K, N, tm, tk, tn, name, per_step=1):
    M = dy3.shape[1]
    bps = dy3.shape[2] // tn
    u = per_step
    grid = (M // tm, K // tk, N // (tn * u))
    nn = grid[2]

    def body(*refs):
        o_ref = refs[-1]
        p = None
        for r in range(u):
            d = lax.dot_general(refs[r][...], refs[u + r][...], (((1,), (1,)), ((), ())), preferred_element_type=F32)
            p = d if p is None else p + d
        if nn == 1:
            o_ref[...] = p
            return
        n = pl.program_id(2)

        @pl.when(n == 0)
        def _():
            o_ref[...] = p

        @pl.when(n > 0)
        def _():
            o_ref[...] += p

    def dy_spec(r):
        return pl.BlockSpec((None, tm, tn), lambda i, j, n: ((n * u + r) // bps, i, (n * u + r) % bps))

    def w_spec(r):
        return pl.BlockSpec((None, tk, tn), lambda i, j, n: wmap(j, n * u + r))

    return _pcall(
        body, name=name, grid=grid,
        in_specs=[dy_spec(r) for r in range(u)] + [w_spec(r) for r in range(u)],
        out_specs=pl.BlockSpec((tm, tk), lambda i, j, n: (i, j)),
        out_shape=jax.ShapeDtypeStruct((M, K), F32),
        compiler_params=_params(("parallel", "parallel", "arbitrary")),
    )(*([dy3] * u), *([w3] * u))


def _mm_tn(x, dy3, shape4, wmap, K, N, tk, tn, name, tm=None):
    M = x.shape[0]
    tm = M if tm is None else tm
    nm = M // tm
    bps = dy3.shape[2] // tn

    def body(*refs):
        x_ref, dy_ref = refs[:2]
        p = lax.dot_general(x_ref[...], dy_ref[...], (((0,), (0,)), ((), ())), preferred_element_type=F32)
        if nm == 1:
            o_ref = refs[-1]
            o_ref[...] = p.astype(o_ref.dtype)
            return
        o_ref, acc = refs[-2:]
        m = pl.program_id(2)

        @pl.when(m == 0)
        def _():
            acc[...] = p

        @pl.when(m > 0)
        def _():
            acc[...] += p

        @pl.when(m == nm - 1)
        def _():
            o_ref[...] = acc[...].astype(o_ref.dtype)

    def omap(i, j, m):
        s, rb, cb = wmap(i, j)
        return (s, 0, rb, cb)

    return _pcall(
        body, name=name, grid=(K // tk, N // tn, nm),
        in_specs=[pl.BlockSpec((tm, tk), lambda i, j, m: (m, i)),
                  pl.BlockSpec((None, tm, tn), lambda i, j, m: (j // bps, m, j % bps))],
        out_specs=pl.BlockSpec((None, None, tk, tn), omap),
        out_shape=jax.ShapeDtypeStruct(shape4, BF16),
        scratch_shapes=[] if nm == 1 else [pltpu.VMEM((tk, tn), F32)],
        compiler_params=_params(("parallel", "parallel", "arbitrary")),
    )(x, dy3)


def _row_call(fn, rows, vecs, outs, n_acc, name, t_rows=256, sub=16, per_trip=4):
    T = rows[0][0].shape[0]
    t_rows = min(t_rows, T)
    nsub = t_rows // sub
    n_r, n_v, n_o = len(rows), len(vecs), len(outs)
    width = rows[0][2]

    def body(*refs):
        r_refs = refs[:n_r]
        v_refs = refs[n_r:n_r + n_v]
        o_refs = refs[n_r + n_v:n_r + n_v + n_o]
        a_refs = refs[n_r + n_v + n_o:]

        @pl.when(pl.program_id(0) == 0)
        def _():
            for a in a_refs:
                a[...] = jnp.zeros_like(a)

        vv = [v[...] for v in v_refs]

        def step(i, carry):
            done = []
            for u in range(per_trip):
                sl = pl.ds(pl.multiple_of((i * per_trip + u) * sub, sub), sub)
                done.append((sl,) + tuple(fn([r[sl, :] for r in r_refs], vv)))
            for sl, o_vals, a_vals in done:
                for o, val in zip(o_refs, o_vals):
                    o[sl, :] = val.astype(o.dtype)
            for a_i, a in enumerate(a_refs):
                tot = None
                for _, _, a_vals in done:
                    part = a_vals[a_i].reshape(sub // SUBLANES, SUBLANES, a_vals[a_i].shape[-1]).sum(axis=0)
                    tot = part if tot is None else tot + part
                a[...] += tot
            return carry

        lax.fori_loop(0, nsub // per_trip, step, 0)

    in_specs = [pl.BlockSpec((t_rows, w), functools.partial(lambda i, cb: (i, cb), cb=cb)) for _, cb, w in rows]
    in_specs += [pl.BlockSpec(v.shape, lambda i: (0, 0)) for v in vecs]
    out_specs = [pl.BlockSpec((t_rows, w), lambda i: (i, 0)) for w, _ in outs]
    out_specs += [pl.BlockSpec((SUBLANES, width), lambda i: (0, 0)) for _ in range(n_acc)]
    out_shape = [jax.ShapeDtypeStruct((T, w), dt) for w, dt in outs]
    out_shape += [jax.ShapeDtypeStruct((SUBLANES, width), F32) for _ in range(n_acc)]
    return _pcall(
        body, name=name, grid=(T // t_rows,), in_specs=in_specs, out_specs=out_specs, out_shape=out_shape,
        compiler_params=_params(("arbitrary",)),
    )(*[r[0] for r in rows], *vecs)


def _rms_fwd_fn(rv, vv):
    h, = rv
    w, = vv
    r = lax.rsqrt(jnp.mean(h * h, axis=-1, keepdims=True) + EPS)
    return [h * r * w], []


def _rms_bwd_fn(rv, vv):
    h, dxn, dh_in = rv
    w, = vv
    d = h.shape[-1]
    r = lax.rsqrt(jnp.mean(h * h, axis=-1, keepdims=True) + EPS)
    gy = dxn * w
    dh = r * gy - h * ((r * r * r) * (1.0 / d) * jnp.sum(gy * h, axis=-1, keepdims=True))
    return [dh_in + dh] * 2, [dxn * h * r]


def _final_fn(rv, vv):
    h, tgt = rv
    w, = vv
    d = h.shape[-1]
    r = lax.rsqrt(jnp.mean(h * h, axis=-1, keepdims=True) + EPS)
    hn = h * r
    e = hn * w - tgt
    dy = e * (1.0 / d)
    gy = dy * w
    dh = r * gy - h * ((r * r * r) * (1.0 / d) * jnp.sum(gy * h, axis=-1, keepdims=True))
    return [dh] * 2, [e * e, dy * hn]


def _onorm_fwd_fn(rv, vv):
    o, g = rv
    gain, = vv
    r = lax.rsqrt(jnp.mean(o * o, axis=-1, keepdims=True) + EPS)
    return [o * r * gain * (g * _sigmoid(g))], []


def _onorm_bwd_fn(rv, vv):
    o, g, don = rv
    gain, = vv
    d = o.shape[-1]
    r = lax.rsqrt(jnp.mean(o * o, axis=-1, keepdims=True) + EPS)
    sg = _sigmoid(g)
    sl = g * sg
    n = o * r
    dg = don * n * gain * (sg * (1.0 + g * (1.0 - sg)))
    gy = don * sl * gain
    do = r * gy - o * ((r * r * r) * (1.0 / d) * jnp.sum(gy * o, axis=-1, keepdims=True))
    return [do, dg], [don * sl * n]


HALO = SUBLANES


def _col_call(fn, cols, vecs, outs, n_acc, name, before, after, tc=LANES, chunk=128):
    T = cols[0][0].shape[0]
    chunk = min(chunk, T)
    nch = T // chunk
    ncol = outs[0][1] // tc
    n_c, n_v, n_o = len(cols), len(vecs), len(outs)
    hb = HALO if before else 0
    rw = chunk + hb + (HALO if after else 0)

    def body(*refs):
        c_refs = refs[:n_c]
        v_refs = refs[n_c:n_c + n_v]
        o_refs = refs[n_c + n_v:n_c + n_v + n_o]
        a_refs = refs[n_c + n_v + n_o:]
        vv = [v[...] for v in v_refs]
        wrow = lax.broadcasted_iota(jnp.int32, (rw, tc), 0)
        inside = (wrow >= hb) & (wrow < hb + chunk)

        def step(i, carry):
            r0 = pl.multiple_of(i * chunk, chunk)
            wins = []
            for ref in c_refs:
                parts = []
                if before:
                    pb = ref[pl.ds(pl.multiple_of(jnp.maximum(r0 - HALO, 0), HALO), HALO), :]
                    parts.append(jnp.where(i > 0, pb, 0.0))
                parts.append(ref[pl.ds(r0, chunk), :])
                if after:
                    pa = ref[pl.ds(pl.multiple_of(jnp.minimum(r0 + chunk, T - HALO), HALO), HALO), :]
                    parts.append(jnp.where(i < nch - 1, pa, 0.0))
                wins.append(jnp.concatenate(parts, axis=0) if len(parts) > 1 else parts[0])
            o_vals, a_vals = fn(wins, vv, inside)
            p = 0
            for o, (nseg, _, _) in zip(o_refs, outs):
                for s in range(nseg):
                    o[s, pl.ds(r0, chunk), :] = o_vals[p][hb:hb + chunk].astype(o.dtype)
                    p += 1
            return tuple(c + a for c, a in zip(carry, a_vals))

        taps = [v.shape[0] for v, _ in vecs][:n_acc]
        init = tuple(jnp.zeros((1, tc), F32) for k in taps for _ in range(k))
        sums = lax.fori_loop(0, nch, step, init)
        arow = lax.broadcasted_iota(jnp.int32, (SUBLANES, tc), 0)
        p = 0
        for a, k in zip(a_refs, taps):
            acc = jnp.zeros((SUBLANES, tc), F32)
            for t in range(k):
                acc = jnp.where(arow == t, sums[p], acc)
                p += 1
            a[...] = acc

    in_specs = [pl.BlockSpec((T, tc), functools.partial(lambda j, off: (0, off + j), off=off)) for _, off in cols]
    in_specs += [pl.BlockSpec((v.shape[0], tc), functools.partial(lambda j, off: (0, off + j), off=off))
                 for v, off in vecs]
    out_specs = [pl.BlockSpec((nseg, T, tc), lambda j: (0, 0, j)) for nseg, _, _ in outs]
    out_specs += [pl.BlockSpec((SUBLANES, tc), lambda j: (0, j)) for _ in range(n_acc)]
    out_shape = [jax.ShapeDtypeStruct((nseg, T, w), dt) for nseg, w, dt in outs]
    out_shape += [jax.ShapeDtypeStruct((SUBLANES, ncol * tc), F32) for _ in range(n_acc)]
    return _pcall(
        body, name=name, grid=(ncol,), in_specs=in_specs, out_specs=out_specs, out_shape=out_shape,
        compiler_params=_params(("parallel",)),
    )(*[c[0] for c in cols], *[v[0] for v in vecs])


def _down(x, k):
    return x if k == 0 else pltpu.roll(x, k, 0)


def _up(x, k):
    return x if k == 0 else pltpu.roll(x, x.shape[0] - k, 0)


def _lags(x):
    return _down(x, 2), _down(x, 1), x


def _conv(lags, w):
    return w[0:1] * lags[0] + w[1:2] * lags[1] + w[2:3] * lags[2]


def _conv_t(d, w):
    return w[2:3] * d + w[1:2] * _up(d, 1) + w[0:1] * _up(d, 2)


def _tap_sums(d, lags, inside):
    dm = jnp.where(inside, d, 0.0)
    return [jnp.sum(dm * lag, axis=0, keepdims=True) for lag in lags]


def _glu_fwd_fn(wins, vv, inside):
    xg, xv = wins
    wg, wv = vv
    ug = _conv(_lags(xg), wg)
    uv = _conv(_lags(xv), wv)
    return [ug * _sigmoid(ug) * uv], []


def _glu_bwd_fn(wins, vv, inside):
    xg, xv, da = wins
    wg, wv = vv
    lg, lv = _lags(xg), _lags(xv)
    ug = _conv(lg, wg)
    uv = _conv(lv, wv)
    sg = _sigmoid(ug)
    dug = da * uv * (sg * (1.0 + ug * (1.0 - sg)))
    duv = da * (ug * sg)
    return [_conv_t(dug, wg), _conv_t(duv, wv)], _tap_sums(dug, lg, inside) + _tap_sums(duv, lv, inside)


def _sc_fwd_fn(wins, vv, inside):
    gb, gc, hh = wins
    w, = vv
    return [gb * _conv(_lags(gc * hh), w)], []


def _sc_bwd_fn(wins, vv, inside):
    gb, gc, hh, dy = wins
    w, = vv
    lz = _lags(gc * hh)
    dcv = dy * gb
    dz = _conv_t(dcv, w)
    return [dy * _conv(lz, w), dz * hh, dz * gc], _tap_sums(dcv, lz, inside)


def _gates(qr, fr, lb):
    sg = _sigmoid(fr)
    f = lb + (1.0 - lb) * sg
    sq = _sigmoid(qr)
    q = qr * sq * (HEAD ** -0.5)
    return q, 1.0 - f, jnp.log(f), f, sg, sq


def _boundary_rows(b, g, row):
    c = b.shape[0]
    if 2 * g >= SUBLANES:
        x = b.reshape(c // (2 * g), 2 * g, LANES)
        return jnp.broadcast_to(x[:, g - 1:g, :], x.shape).reshape(c, LANES)
    x = b.reshape(c // SUBLANES, SUBLANES, LANES)
    lo = jnp.broadcast_to(x[:, 1:2, :], x.shape).reshape(c, LANES)
    hi = jnp.broadcast_to(x[:, 5:6, :], x.shape).reshape(c, LANES)
    return jnp.where((row & 4) == 0, lo, hi)


def _chunk_decays(gl, f, row):
    c = gl.shape[0]
    b = gl
    d = 1
    while d < c:
        b = b + jnp.where(row >= d, pltpu.roll(b, d, 0), 0.0)
        d *= 2
    eq, ek = [], []
    g = c // 2
    while g >= 2:
        right = (row & g) != 0
        m = _boundary_rows(b, g, row)
        z = jnp.exp(jnp.where(right, b - m, m - b))
        eq.append(jnp.where(right, z, 0.0))
        ek.append(jnp.where(right, 0.0, z))
        g //= 2
    odd = (row & 1) != 0
    eq.append(jnp.where(odd, f, 0.0))
    ek.append(jnp.where(odd, 0.0, 1.0))
    return b, eq, ek


def _intra(q, k, eq, ek, tt, ss):
    c = q.shape[0]
    qs, ks = [], []
    a = jnp.where(tt == ss, jnp.sum(q * k, axis=1, keepdims=True), 0.0)
    g = c // 2
    for e_q, e_k in zip(eq, ek):
        qg = (q * e_q).astype(BF16)
        kg = (k * e_k).astype(BF16)
        p = lax.dot_general(qg, kg, (((1,), (1,)), ((), ())), preferred_element_type=F32)
        a = a + (p if 2 * g >= c else jnp.where((tt ^ ss) < 2 * g, p, 0.0))
        qs.append(qg)
        ks.append(kg)
        g //= 2
    return a, qs, ks


def _hgrn_fwd(proj, lb, d_model):
    T = proj.shape[0]
    H = d_model // HEAD
    nch = T // CHUNK

    def body(q_ref, f_ref, v_ref, lb_ref, o_ref, s_ref):
        lbv = lb_ref[...]
        row = lax.broadcasted_iota(jnp.int32, (CHUNK, HEAD), 0)
        tt = lax.broadcasted_iota(jnp.int32, (CHUNK, CHUNK), 0)
        ss = lax.broadcasted_iota(jnp.int32, (CHUNK, CHUNK), 1)

        def step(i, st):
            sl = pl.ds(pl.multiple_of(i * CHUNK, CHUNK), CHUNK)
            q, k, gl, f, _, _ = _gates(q_ref[sl, :], f_ref[sl, :], lbv)
            v = v_ref[sl, :].astype(BF16)
            b, eq, ek = _chunk_decays(gl, f, row)
            a, _, _ = _intra(q, k, eq, ek, tt, ss)
            bl = b[CHUNK - 1:CHUNK, :]
            q0 = (q * jnp.exp(b)).astype(BF16)
            kh = (k * jnp.exp(bl - b)).astype(BF16)
            s_ref[i] = st
            o = jnp.dot(a.astype(BF16), v, preferred_element_type=F32)
            o = o + lax.dot_general(q0, st.astype(BF16), (((1,), (1,)), ((), ())), preferred_element_type=F32)
            o_ref[sl, :] = o
            return jnp.exp(bl) * st + lax.dot_general(v, kh, (((0,), (0,)), ((), ())), preferred_element_type=F32)

        per = 4 if nch % 4 == 0 else 2

        def trip(i, st):
            for u in range(per):
                st = step(per * i + u, st)
            return st

        lax.fori_loop(0, nch // per, trip, jnp.zeros((HEAD, HEAD), F32))

    col = lambda off: pl.BlockSpec((T, HEAD), functools.partial(lambda h, off: (0, off + h), off=off))
    return _pcall(
        body, name="hgrn_fwd", grid=(H,),
        in_specs=[col(0), col(H), col(2 * H), pl.BlockSpec((1, HEAD), lambda h: (0, h))],
        out_specs=[pl.BlockSpec((T, HEAD), lambda h: (0, h)),
                   pl.BlockSpec((None, nch, HEAD, HEAD), lambda h: (h, 0, 0, 0))],
        out_shape=[jax.ShapeDtypeStruct((T, d_model), F32), jax.ShapeDtypeStruct((H, nch, HEAD, HEAD), F32)],
        compiler_params=_params(("parallel",)),
    )(proj, proj, proj, lb)


def _hgrn_bwd(proj, lb, states, do, dgate, d_model):
    T = proj.shape[0]
    H = d_model // HEAD
    nch = T // CHUNK

    def body(q_ref, f_ref, v_ref, lb_ref, s_ref, do_ref, dg_ref, dp_ref, dlb_ref):
        dq_ref, df_ref, dv_ref = dp_ref.at[0], dp_ref.at[1], dp_ref.at[2]
        dp_ref[3] = dg_ref[...]
        lbv = lb_ref[...]
        row = lax.broadcasted_iota(jnp.int32, (CHUNK, HEAD), 0)
        tt = lax.broadcasted_iota(jnp.int32, (CHUNK, CHUNK), 0)
        ss = lax.broadcasted_iota(jnp.int32, (CHUNK, CHUNK), 1)
        last = row == CHUNK - 1
        nt = (((1,), (1,)), ((), ()))
        tn = (((0,), (0,)), ((), ()))

        def step(j, carry):
            dst, dlb = carry
            i = nch - 1 - j
            sl = pl.ds(pl.multiple_of(i * CHUNK, CHUNK), CHUNK)
            qr = q_ref[sl, :]
            q, k, gl, f, sg, sq = _gates(qr, f_ref[sl, :], lbv)
            v = v_ref[sl, :].astype(BF16)
            d_o = do_ref[sl, :].astype(BF16)
            st = s_ref[i]
            st16 = st.astype(BF16)
            dst16 = dst.astype(BF16)
            b, eq, ek = _chunk_decays(gl, f, row)
            a, qs, ks = _intra(q, k, eq, ek, tt, ss)
            bl = b[CHUNK - 1:CHUNK, :]
            e0 = jnp.exp(b)
            eh = jnp.exp(bl - b)
            ebl = jnp.exp(bl)
            q0 = q * e0
            kh = k * eh
            q016 = q0.astype(BF16)
            kh16 = kh.astype(BF16)
            dv = lax.dot_general(a.astype(BF16), d_o, tn, preferred_element_type=F32)
            dv = dv + lax.dot_general(kh16, dst16, nt, preferred_element_type=F32)
            dv_ref[sl, :] = dv.astype(dv_ref.dtype)
            da = lax.dot_general(d_o, v, nt, preferred_element_type=F32)
            da = jnp.where(tt >= ss, da, 0.0)
            dd = jnp.sum(jnp.where(tt == ss, da, 0.0), axis=1, keepdims=True)
            dq0 = jnp.dot(d_o, st16, preferred_element_type=F32)
            dkh = jnp.dot(v, dst16, preferred_element_type=F32)
            dq = dq0 * e0 + dd * k
            dk = dkh * eh + dd * q
            db = dq0 * q016.astype(F32) - dkh * kh16.astype(F32)
            g = CHUNK // 2
            for e_q, e_k, qg, kg in zip(eq, ek, qs, ks):
                dag = (da if 2 * g >= CHUNK else jnp.where((tt ^ ss) < 2 * g, da, 0.0)).astype(BF16)
                dqg = jnp.dot(dag, kg, preferred_element_type=F32)
                dkg = lax.dot_general(dag, qg, tn, preferred_element_type=F32)
                dq = dq + dqg * e_q
                dk = dk + dkg * e_k
                db = db + (dqg * qg.astype(F32) - dkg * kg.astype(F32))
                g //= 2
            dbl = jnp.sum(dkh * kh16.astype(F32), axis=0, keepdims=True) + ebl * jnp.sum(dst * st, axis=0, keepdims=True)
            db = db + jnp.where(last, dbl, 0.0)
            d = 1
            while d < CHUNK:
                db = db + jnp.where(row < CHUNK - d, pltpu.roll(db, CHUNK - d, 0), 0.0)
                d *= 2
            dfg = db / f - dk
            df_ref[sl, :] = (dfg * (1.0 - lbv) * sg * (1.0 - sg)).astype(df_ref.dtype)
            dq_ref[sl, :] = (dq * (HEAD ** -0.5) * (sq * (1.0 + qr * (1.0 - sq)))).astype(dq_ref.dtype)
            dlb = dlb + jnp.sum(dfg * (1.0 - sg), axis=0, keepdims=True)
            dst = ebl * dst + lax.dot_general(d_o, q016, tn, preferred_element_type=F32)
            return dst, dlb

        _, dlb = lax.fori_loop(0, nch // 2, lambda j, cr: step(2 * j + 1, step(2 * j, cr)),
                               (jnp.zeros((HEAD, HEAD), F32), jnp.zeros((1, HEAD), F32)))
        arow = lax.broadcasted_iota(jnp.int32, (SUBLANES, HEAD), 0)
        dlb_ref[...] = jnp.where(arow == 0, dlb, 0.0)

    col = lambda off: pl.BlockSpec((T, HEAD), functools.partial(lambda h, off: (0, off + h), off=off))
    return _pcall(
        body, name="hgrn_bwd", grid=(H,),
        in_specs=[col(0), col(H), col(2 * H), pl.BlockSpec((1, HEAD), lambda h: (0, h)),
                  pl.BlockSpec((None, nch, HEAD, HEAD), lambda h: (h, 0, 0, 0)), col(0), col(0)],
        out_specs=[pl.BlockSpec((4, T, HEAD), lambda h: (0, 0, h)), pl.BlockSpec((SUBLANES, HEAD), lambda h: (0, h))],
        out_shape=[jax.ShapeDtypeStruct((4, T, d_model), BF16), jax.ShapeDtypeStruct((SUBLANES, d_model), F32)],
        compiler_params=_params(("parallel",)),
    )(proj, proj, proj, lb, states, do, dgate)


def _lb_softmax(table):
    n, f = table.shape

    def body(t_ref, p_ref):
        t = t_ref[...]
        e = jnp.exp(t - jnp.max(t, axis=0, keepdims=True))
        p_ref[...] = e / jnp.sum(e, axis=0, keepdims=True)

    padded = jnp.pad(table, ((0, SUBLANES - n), (0, 0)), constant_values=-jnp.inf)
    return _pcall(body, name="lb_softmax", out_shape=jax.ShapeDtypeStruct((SUBLANES, f), F32))(padded)


def _adamw_math(w, g, m, v):
    m = ADAM_B1 * m + (1.0 - ADAM_B1) * g
    v = ADAM_B2 * v + (1.0 - ADAM_B2) * (g * g)
    m_hat = m / (1.0 - ADAM_B1 ** ADAM_STEP)
    v_hat = v / (1.0 - ADAM_B2 ** ADAM_STEP)
    delta = -ADAM_LR * (m_hat / (jnp.sqrt(v_hat) + ADAM_EPS) + ADAM_WD * w)
    return delta, m, v


def _adamw(w, g, m, v, name):
    R, C = w.shape
    tr = _pick(R, (128, 64, 32, 16, 8))

    def body(w_ref, g_ref, m_ref, v_ref, d_ref, nm_ref, nv_ref):
        d, nm, nv = _adamw_math(w_ref[...], g_ref[...], m_ref[...], v_ref[...])
        d_ref[...] = d
        nm_ref[...] = nm
        nv_ref[...] = nv

    spec = pl.BlockSpec((tr, C), lambda i: (i, 0))
    return _pcall(
        body, name=name, grid=(R // tr,), in_specs=[spec] * 4, out_specs=[spec] * 3,
        out_shape=[jax.ShapeDtypeStruct((R, C), F32)] * 3, compiler_params=_params(("parallel",)),
    )(w, g, m, v)


def _adamw_halves(w, m, v, g_mine, g_recv, c, name, layer=0, prev=None):
    C = w.shape[1]
    rh = g_mine.shape[0]
    tr = _pick(rh, (128, 64, 32, 16, 8))
    nb = rh // tr
    r0 = layer * 2 * nb

    def body(c_ref, w_ref, m_ref, v_ref, gm_ref, gr_ref, *rest):
        g_ref, d_ref, nm_ref, nv_ref = rest[-4:]
        g = jnp.where(pl.program_id(0) == c_ref[0], gm_ref[...], gr_ref[...])
        d, nm, nv = _adamw_math(w_ref[...], g, m_ref[...], v_ref[...])
        g_ref[...] = g
        d_ref[...] = d
        nm_ref[...] = nm
        nv_ref[...] = nv

    full = pl.BlockSpec((tr, C), lambda h, i, cr: (r0 + h * nb + i, 0))
    half = pl.BlockSpec((tr, C), lambda h, i, cr: (i, 0))
    in_specs = [full, full, full, half, half]
    args = [c, w, m, v, g_mine, g_recv]
    alias = {}
    if prev is not None:
        in_specs += [pl.BlockSpec(memory_space=pl.ANY)] * 4
        args += list(prev)
        alias = {6 + k: k for k in range(4)}
    return _pcall(
        body, name=name,
        grid_spec=pltpu.PrefetchScalarGridSpec(
            num_scalar_prefetch=1, grid=(2, nb), in_specs=in_specs, out_specs=[full] * 4),
        out_shape=[jax.ShapeDtypeStruct(w.shape, F32)] * 4, input_output_aliases=alias,
        compiler_params=_params(("parallel", "parallel")),
    )(*args)


def _lb_table_grad(p8, dlb, n):
    f = p8.shape[1]

    def body(p_ref, d_ref, o_ref):
        p = p_ref[...]
        d = d_ref[...]
        p0 = p[0:1, :]
        first = lax.broadcasted_iota(jnp.int32, p.shape, 0) == 0
        o_ref[...] = p * (jnp.where(first, d, 0.0) - d * p0)

    return _pcall(body, name="lb_table_grad", out_shape=jax.ShapeDtypeStruct((SUBLANES, f), F32))(p8, dlb)[:n]


def _place():
    x, y, c = lax.axis_index("x"), lax.axis_index("y"), lax.axis_index("c")
    chips = [(1 - x, y), (x, 1 - y), (1 - x, 1 - y)]
    return x, y, c, chips


HBM_SPEC = pl.BlockSpec(memory_space=pltpu.HBM)


def _gather_weights(big, small):
    nb, ns = len(big), len(small)
    n = nb + ns

    def body(*refs):
        ins, outs = refs[:n], refs[n:2 * n]
        send_sems, recv_sems, own_send, own_recv = refs[2 * n:]
        x, y, c, chips = _place()
        me = 2 * x + y
        sib = (x, y, 1 - c)
        own = [pltpu.make_async_remote_copy(
            src_ref=ins[t], dst_ref=outs[t].at[me], send_sem=own_send.at[t], recv_sem=own_recv.at[t],
            device_id=sib, device_id_type=MESH) for t in range(n)]
        for cp in own:
            cp.start()

        def half(t, h):
            rh = big[t].shape[0] // 2
            return pl.ds(pl.multiple_of(h * rh, rh), rh)

        sends = []
        for t in range(n):
            for j, chip in enumerate(chips):
                k = 6 * t + j
                if t < nb:
                    src, dst = ins[t].at[half(t, c)], outs[t].at[me, half(t, c)]
                else:
                    src, dst = ins[t], outs[t].at[me]
                sends.append(pltpu.make_async_remote_copy(
                    src_ref=src, dst_ref=dst, send_sem=send_sems.at[k], recv_sem=recv_sems.at[k],
                    device_id=(*chip, c), device_id_type=MESH))
        for cp in sends:
            cp.start()
        passed = []
        for t in range(n):
            for j, (cx, cy) in enumerate(chips):
                k = 6 * t + j
                s = 2 * cx + cy
                if t < nb:
                    landed = outs[t].at[s, half(t, c)]
                    pltpu.make_async_remote_copy(
                        src_ref=landed, dst_ref=landed, send_sem=send_sems.at[k], recv_sem=recv_sems.at[k],
                        device_id=sib, device_id_type=MESH).wait_recv()
                    fwd = pltpu.make_async_remote_copy(
                        src_ref=landed, dst_ref=landed, send_sem=send_sems.at[k + 3], recv_sem=recv_sems.at[k + 3],
                        device_id=sib, device_id_type=MESH)
                    fwd.start()
                    passed.append(fwd)
                else:
                    landed = outs[t].at[s]
                    pltpu.make_async_remote_copy(
                        src_ref=landed, dst_ref=landed, send_sem=send_sems.at[k], recv_sem=recv_sems.at[k],
                        device_id=sib, device_id_type=MESH).wait_recv()
        for t in range(nb):
            for j, (cx, cy) in enumerate(chips):
                k = 6 * t + j
                other = outs[t].at[2 * cx + cy, half(t, 1 - c)]
                pltpu.make_async_remote_copy(
                    src_ref=other, dst_ref=other, send_sem=send_sems.at[k + 3], recv_sem=recv_sems.at[k + 3],
                    device_id=sib, device_id_type=MESH).wait_recv()
        for cp in sends + passed:
            cp.wait_send()
        for cp in own:
            cp.wait()

    arrs = list(big) + list(small)
    return _pcall(
        body, name="gather_weights", in_specs=[HBM_SPEC] * n, out_specs=[HBM_SPEC] * n,
        out_shape=[jax.ShapeDtypeStruct((N_CHIPS,) + a.shape, a.dtype) for a in arrs],
        scratch_shapes=[pltpu.SemaphoreType.DMA((6 * n,)), pltpu.SemaphoreType.DMA((6 * n,)),
                        pltpu.SemaphoreType.DMA((n,)), pltpu.SemaphoreType.DMA((n,))],
    )(*arrs)


SEM_SPEC = pl.BlockSpec(memory_space=pltpu.SEMAPHORE)
DATAFLOW = pltpu.SideEffectType.DATAFLOW_SIDE_EFFECTING
COPIES_PER_SHARD = 4


def _shard_copies(ins, lands, send_sems, recv_sems, base=0):
    x, y, c, chips = _place()
    me = 2 * x + y
    cps = []
    for t in range(len(ins)):
        rh = ins[t].shape[0] // 2
        half = pl.ds(pl.multiple_of(c * rh, rh), rh)
        for j, chip in enumerate(chips):
            k = COPIES_PER_SHARD * (base + t) + j
            cps.append(pltpu.make_async_remote_copy(
                src_ref=ins[t].at[half], dst_ref=lands[t].at[me, half], send_sem=send_sems.at[k],
                recv_sem=recv_sems.at[k], device_id=(*chip, c), device_id_type=MESH))
        k = COPIES_PER_SHARD * (base + t) + 3
        cps.append(pltpu.make_async_remote_copy(
            src_ref=ins[t], dst_ref=lands[t].at[me], send_sem=send_sems.at[k], recv_sem=recv_sems.at[k],
            device_id=(x, y, 1 - c), device_id_type=MESH))
    return cps


def _gather_start(shards, thru, name):
    n = len(shards)
    nops = 2 * n + len(thru)

    def body(*refs):
        ins, lands = refs[:n], refs[n:2 * n]
        send_sems, recv_sems = refs[nops], refs[nops + 1]
        for cp in _shard_copies(ins, lands, send_sems, recv_sems):
            cp.start()

    lands = [pltpu.with_memory_space_constraint(lax.empty((N_CHIPS,) + s.shape, s.dtype), pltpu.HBM) for s in shards]
    ops = [pltpu.with_memory_space_constraint(s, pltpu.HBM) for s in shards] + lands + list(thru)
    nsem = COPIES_PER_SHARD * n
    res = _pcall(
        body, name=name, in_specs=[HBM_SPEC] * nops,
        out_specs=[SEM_SPEC, SEM_SPEC] + [HBM_SPEC] * nops,
        out_shape=[pltpu.SemaphoreType.DMA((nsem,)), pltpu.SemaphoreType.DMA((nsem,))]
        + [pltpu.HBM(o.shape, o.dtype) for o in ops],
        input_output_aliases={i: 2 + i for i in range(nops)},
        compiler_params=pltpu.CompilerParams(has_side_effects=DATAFLOW),
    )(*ops)
    return res[0], res[1], res[2:2 + n], res[2 + n:2 + 2 * n], list(res[2 + 2 * n:])


def _gather_wait(send_sems, recv_sems, shards, lands, after, name, base=0, which=None):
    n = len(shards)

    def body(*refs):
        ins, lnd = refs[:n], refs[n:2 * n]
        ssem, rsem = refs[2 * n], refs[2 * n + 1]
        for k, cp in enumerate(_shard_copies(ins, lnd, ssem, rsem, base)):
            if which is None or k % COPIES_PER_SHARD in which:
                cp.wait_send()
                cp.wait_recv()

    res = _pcall(
        body, name=name,
        in_specs=[HBM_SPEC] * (2 * n) + [SEM_SPEC, SEM_SPEC, pl.BlockSpec(memory_space=pl.ANY)],
        out_specs=[HBM_SPEC] * (2 * n),
        out_shape=[pltpu.HBM(o.shape, o.dtype) for o in list(shards) + list(lands)],
        input_output_aliases={i: i for i in range(2 * n)},
        compiler_params=pltpu.CompilerParams(has_side_effects=DATAFLOW),
    )(*shards, *lands, send_sems, recv_sems, after)
    return res[:n], res[n:]


def _forward_copies(land, send_sems, recv_sems, which=(0, 1, 2)):
    x, y, c, chips = _place()
    rh = land.shape[1] // 2
    return [pltpu.make_async_remote_copy(
        src_ref=land.at[2 * cx + cy, pl.ds(pl.multiple_of(c * rh, rh), rh)],
        dst_ref=land.at[2 * cx + cy, pl.ds(pl.multiple_of(c * rh, rh), rh)],
        send_sem=send_sems.at[j], recv_sem=recv_sems.at[j], device_id=(x, y, 1 - c), device_id_type=MESH)
        for j, (cx, cy) in enumerate(chips) if j in which]


def _forward_start(land, thru, name, which=(0, 1, 2)):
    def body(land_ref, thru_ref, send_sems, recv_sems, out_ref, thru_out):
        for cp in _forward_copies(land_ref, send_sems, recv_sems, which):
            cp.start()

    return _pcall(
        body, name=name, in_specs=[HBM_SPEC, HBM_SPEC], out_specs=[SEM_SPEC, SEM_SPEC, HBM_SPEC, HBM_SPEC],
        out_shape=[pltpu.SemaphoreType.DMA((3,)), pltpu.SemaphoreType.DMA((3,)), pltpu.HBM(land.shape, land.dtype),
                   pltpu.HBM(thru.shape, thru.dtype)],
        input_output_aliases={0: 2, 1: 3}, compiler_params=pltpu.CompilerParams(has_side_effects=DATAFLOW),
    )(land, thru)


def _forward_wait(send_sems, recv_sems, land, after, name, which=(0, 1, 2)):
    def body(land_ref, ssem, rsem, after_ref, out_ref):
        for cp in _forward_copies(land_ref, ssem, rsem, which):
            cp.wait_send()
            cp.wait_recv()

    return _pcall(
        body, name=name, in_specs=[HBM_SPEC, SEM_SPEC, SEM_SPEC, pl.BlockSpec(memory_space=pl.ANY)],
        out_specs=HBM_SPEC, out_shape=pltpu.HBM(land.shape, land.dtype), input_output_aliases={0: 0},
        compiler_params=pltpu.CompilerParams(has_side_effects=DATAFLOW),
    )(land, send_sems, recv_sems, after)


def _sibling_copies(ins, lands, send_sems, recv_sems, other_half):
    x, y, c, _ = _place()
    return [pltpu.make_async_remote_copy(
        src_ref=ins[t].at[:, 1 - c] if other_half else ins[t], dst_ref=lands[t], send_sem=send_sems.at[t],
        recv_sem=recv_sems.at[t], device_id=(x, y, 1 - c), device_id_type=MESH) for t in range(len(ins))]


def _sibling_start(srcs, other_half, thru, name):
    n = len(srcs)
    nthru = 0 if thru is None else 1

    def body(*refs):
        ins, lands = refs[:n], refs[n:2 * n]
        send_sems, recv_sems = refs[2 * n + nthru], refs[2 * n + nthru + 1]
        for cp in _sibling_copies(ins, lands, send_sems, recv_sems, other_half):
            cp.start()

    shapes = [(s.shape[0],) + s.shape[2:] if other_half else s.shape for s in srcs]
    lands = [pltpu.with_memory_space_constraint(lax.empty(sh, s.dtype), pltpu.HBM) for sh, s in zip(shapes, srcs)]
    ops = [pltpu.with_memory_space_constraint(s, pltpu.HBM) for s in srcs] + lands + ([] if thru is None else [thru])
    res = _pcall(
        body, name=name, in_specs=[HBM_SPEC] * len(ops),
        out_specs=[SEM_SPEC, SEM_SPEC] + [HBM_SPEC] * len(ops),
        out_shape=[pltpu.SemaphoreType.DMA((n,)), pltpu.SemaphoreType.DMA((n,))]
        + [pltpu.HBM(o.shape, o.dtype) for o in ops],
        input_output_aliases={i: 2 + i for i in range(len(ops))},
        compiler_params=pltpu.CompilerParams(has_side_effects=DATAFLOW),
    )(*ops)
    return res[0], res[1], res[2:2 + n], res[2 + n:2 + 2 * n], (None if thru is None else res[2 + 2 * n])


def _sibling_wait(send_sems, recv_sems, srcs, lands, other_half, after, name):
    n = len(srcs)

    def body(*refs):
        ins, lnd = refs[:n], refs[n:2 * n]
        ssem, rsem = refs[2 * n], refs[2 * n + 1]
        for cp in _sibling_copies(ins, lnd, ssem, rsem, other_half):
            cp.wait_send()
            cp.wait_recv()

    res = _pcall(
        body, name=name,
        in_specs=[HBM_SPEC] * (2 * n) + [SEM_SPEC, SEM_SPEC, pl.BlockSpec(memory_space=pl.ANY)],
        out_specs=[HBM_SPEC] * (2 * n),
        out_shape=[pltpu.HBM(o.shape, o.dtype) for o in list(srcs) + list(lands)],
        input_output_aliases={i: i for i in range(2 * n)},
        compiler_params=pltpu.CompilerParams(has_side_effects=DATAFLOW),
    )(*srcs, *lands, send_sems, recv_sems, after)
    return res[:n], res[n:]


def _chip_copies(ins, lands, send_sems, recv_sems):
    x, y, c, chips = _place()
    cps = []
    for t in range(len(ins)):
        for j, (cx, cy) in enumerate(chips):
            cps.append(pltpu.make_async_remote_copy(
                src_ref=ins[t].at[2 * cx + cy], dst_ref=lands[t].at[j],
                send_sem=send_sems.at[3 * t + j], recv_sem=recv_sems.at[3 * t + j],
                device_id=(cx, cy, c), device_id_type=MESH))
    return cps


def _chip_start(parts, thru, name):
    n = len(parts)

    def body(*refs):
        ins, lands = refs[:n], refs[n:2 * n]
        send_sems, recv_sems = refs[2 * n + 1], refs[2 * n + 2]
        for cp in _chip_copies(ins, lands, send_sems, recv_sems):
            cp.start()

    lands = [pltpu.with_memory_space_constraint(lax.empty((3,) + p.shape[1:], p.dtype), pltpu.HBM) for p in parts]
    ops = [pltpu.with_memory_space_constraint(p, pltpu.HBM) for p in parts] + lands + [thru]
    res = _pcall(
        body, name=name, in_specs=[HBM_SPEC] * (2 * n + 1),
        out_specs=[SEM_SPEC, SEM_SPEC] + [HBM_SPEC] * (2 * n + 1),
        out_shape=[pltpu.SemaphoreType.DMA((3 * n,)), pltpu.SemaphoreType.DMA((3 * n,))]
        + [pltpu.HBM(o.shape, o.dtype) for o in ops],
        input_output_aliases={i: 2 + i for i in range(2 * n + 1)},
        compiler_params=pltpu.CompilerParams(has_side_effects=DATAFLOW),
    )(*ops)
    return res[0], res[1], res[2:2 + n], res[2 + n:2 + 2 * n], res[2 + 2 * n]


def _chip_wait(send_sems, recv_sems, parts, lands, after, name):
    n = len(parts)

    def body(*refs):
        ins, lnd = refs[:n], refs[n:2 * n]
        ssem, rsem = refs[2 * n], refs[2 * n + 1]
        for cp in _chip_copies(ins, lnd, ssem, rsem):
            cp.wait_send()
            cp.wait_recv()

    res = _pcall(
        body, name=name,
        in_specs=[HBM_SPEC] * (2 * n) + [SEM_SPEC, SEM_SPEC, pl.BlockSpec(memory_space=pl.ANY)],
        out_specs=[HBM_SPEC] * (2 * n),
        out_shape=[pltpu.HBM(o.shape, o.dtype) for o in list(parts) + list(lands)],
        input_output_aliases={i: i for i in range(2 * n)},
        compiler_params=pltpu.CompilerParams(has_side_effects=DATAFLOW),
    )(*parts, *lands, send_sems, recv_sems, after)
    return res[:n], res[n:]


def _add_pair(grad, recv, c, name):
    s, _, rh, cc = grad.shape
    tr = _pick(rh, (256, 128, 64, 32, 16))

    def body(c_ref, g_ref, r_ref, o_ref):
        o_ref[...] = (g_ref[...].astype(F32) + r_ref[...].astype(F32)).astype(o_ref.dtype)

    return _pcall(
        body, name=name,
        grid_spec=pltpu.PrefetchScalarGridSpec(
            num_scalar_prefetch=1, grid=(s, rh // tr),
            in_specs=[pl.BlockSpec((None, None, tr, cc), lambda a, i, cr: (a, cr[0], i, 0)),
                      pl.BlockSpec((None, tr, cc), lambda a, i, cr: (a, i, 0))],
            out_specs=pl.BlockSpec((None, tr, cc), lambda a, i, cr: (a, i, 0))),
        out_shape=jax.ShapeDtypeStruct((s, rh, cc), BF16),
        compiler_params=_params(("parallel", "parallel")),
    )(c, grad, recv)


def _add_chips(part, recv, me, name):
    _, rh, cc = part.shape
    tr = _pick(rh, (256, 128, 64, 32, 16))

    def body(m_ref, p_ref, r_ref, o_ref):
        o_ref[...] = ((p_ref[...].astype(F32) + r_ref[0].astype(F32)) + r_ref[1].astype(F32)) + r_ref[2].astype(F32)

    return _pcall(
        body, name=name,
        grid_spec=pltpu.PrefetchScalarGridSpec(
            num_scalar_prefetch=1, grid=(rh // tr,),
            in_specs=[pl.BlockSpec((None, tr, cc), lambda i, mr: (mr[0], i, 0)),
                      pl.BlockSpec((3, tr, cc), lambda i, mr: (0, i, 0))],
            out_specs=pl.BlockSpec((tr, cc), lambda i, mr: (i, 0))),
        out_shape=jax.ShapeDtypeStruct((rh, cc), F32),
        compiler_params=_params(("parallel",)),
    )(me, part, recv)


def _all_sum(vec):
    rows = vec.shape[0]

    def body(v_ref, o_ref, buf, send_sems, recv_sems):
        x, y, c, _ = _place()
        me = 4 * x + 2 * y + c
        buf[me] = v_ref[...]
        cps = []
        for r in range(1, 8):
            fx, fy, fc = (r >> 2) & 1, (r >> 1) & 1, r & 1
            peer = (x ^ fx, y ^ fy, c ^ fc)
            cps.append(pltpu.make_async_remote_copy(
                src_ref=v_ref, dst_ref=buf.at[me], send_sem=send_sems.at[r - 1], recv_sem=recv_sems.at[r - 1],
                device_id=peer, device_id_type=MESH))
        for cp in cps:
            cp.start()
        for r in range(1, 8):
            src = me ^ r
            pltpu.make_async_remote_copy(
                src_ref=v_ref, dst_ref=buf.at[src], send_sem=send_sems.at[r - 1], recv_sem=recv_sems.at[r - 1],
                device_id=(x, y, c), device_id_type=MESH).wait_recv()
        for cp in cps:
            cp.wait_send()
        acc = buf[0]
        for d in range(1, 8):
            acc = acc + buf[d]
        o_ref[...] = acc

    return _pcall(
        body, name="all_sum_small",
        in_specs=[pl.BlockSpec(memory_space=pltpu.VMEM)], out_specs=pl.BlockSpec(memory_space=pltpu.VMEM),
        out_shape=jax.ShapeDtypeStruct((rows, LANES), F32),
        scratch_shapes=[pltpu.VMEM((8, rows, LANES), F32), pltpu.SemaphoreType.DMA((7,)), pltpu.SemaphoreType.DMA((7,))],
    )(vec)


def _pack(parts):
    flat = jnp.concatenate([p.reshape(-1) for p in parts])
    tile = SUBLANES * LANES
    pad = (-flat.shape[0]) % tile
    return jnp.pad(flat, (0, pad)).reshape(-1, LANES)


def _unpack(vec, shapes):
    flat = vec.reshape(-1)
    out, p = [], 0
    for s in shapes:
        n = 1
        for d in s:
            n *= d
        out.append(flat[p:p + n].reshape(s))
        p += n
    return out


def _local_step(x, tgt, norm_mix, norm_ffn, lb8, out_norm, final_norm, sc_conv, ffn_conv, first, arrive, reduce_start,
                reduce_finish):
    T, D = x.shape
    F2 = ffn_conv.shape[-1]
    FF = F2 // 2
    tm = _pick(T, (1024, 512, 256, 128))
    wide = (1536, 1408, 1024, 768, 512, 384, 256, 128)
    cw_h, cw_s, cw_u = 4 * D // N_CHIPS, 3 * D // N_CHIPS, F2 // N_CHIPS
    kp = FF // N_CHIPS
    tk_ff = kp if kp % LANES == 0 else LANES
    tn_d = _pick(D, (1024, 512, 256, 128))
    tk_w = _pick(D, (512, 256, 128))
    tn_h = _pick(cw_h, (1024, 512, 256, 128))
    tn_s = _pick(D // N_CHIPS, (512, 256, 128))
    tn_u = _pick(cw_u, wide)
    lb = lb8[0:1]
    wm_sq = _wmap_col(D, tn_d, 0)
    wm_sq1 = _wmap_col(D, D, 0)
    seg1 = lambda a: a.reshape((1,) + a.shape)

    def mix_in(h, w):
        return _row_call(_rms_fwd_fn, [(h, 0, D)], [w], [(D, BF16)], 0, "rms_fwd")[0]

    def rms_bwd(h, dxn, dh, w):
        return _row_call(_rms_bwd_fn, [(h, 0, D), (dxn, 0, D), (dh, 0, D)], [w], [(D, F32), (D, BF16)], 1, "rms_bwd")

    def ffn_fwd(h, i, fetch_up, behind_down=()):
        xn = mix_in(h, norm_ffn[i:i + 1])
        tn = _pick(cw_u, wide)
        fetch_down, xn = arrive("ffn_w_down%d" % i, xn)
        w_up = fetch_up(xn)
        up = _mm_nn(xn, w_up, _wmap_col(cw_u, tn, 0), D, F2, tm, D, tn, "ffn_up")
        nb = FF // LANES
        a = _col_call(_glu_fwd_fn, [(up, 0), (up, nb)], [(ffn_conv[i], 0), (ffn_conv[i], nb)], [(1, FF, BF16)], 0,
                      "glu_fwd", before=True, after=False)[0][0]
        later = []
        for name in behind_down:
            fetch, a = arrive(name, a)
            later.append(fetch)
        w_down = fetch_down(a)
        h2 = _mm_nn(a, w_down, _wmap_row(kp, tk_ff, 0), FF, D, tm, tk_ff, tn_d, "ffn_down", res=h, per_step=2)
        return h2, (xn, up, a), w_up, w_down, later

    def ffn_bwd(dh, dh16, h, saved, i, w_up, w_down):
        xn, up, a = saved
        g_down = _mm_tn(a, seg1(dh16), (N_CHIPS, 1, kp, D), _wmap_row(kp, tk_ff, 0), FF, D, tk_ff, tn_d,
                        "ffn_down_dw", tm=_pick(T, (2048, 1024, 512, 256, 128)))
        dh16 = reduce_start(("ffn_w_down", i), g_down, dh16)
        da = _mm_nt(seg1(dh16), w_down, _wmap_row(kp, tk_ff, 0), FF, D, tm, tk_ff, D, "ffn_down_dx")
        nb = FF // LANES
        dgv, cg, cv = _col_call(_glu_bwd_fn, [(up, 0), (up, nb), (da, 0)], [(ffn_conv[i], 0), (ffn_conv[i], nb)],
                                [(2, FF, BF16)], 2, "glu_bwd", before=True, after=True)
        g_up = _mm_tn(xn, dgv, (N_CHIPS, 1, D, cw_u), _wmap_col(cw_u, tn_u, 0), D, F2, tk_w, tn_u, "ffn_up_dw")
        dgv = reduce_start(("ffn_w_up", i), g_up, dgv)
        dxn = _mm_nt(dgv, w_up, _wmap_col(cw_u, tn_u, 0), D, F2, _pick(T, (512, 256, 128)), D, tn_u, "ffn_up_dx",
                     per_step=2)
        dh2, dh2_16, dnw = rms_bwd(h, dxn, dh, norm_ffn[i:i + 1])
        return dh2, reduce_finish(dh2_16), dnw, jnp.concatenate([cg[:3], cv[:3]], axis=1)

    h0 = x
    xn0 = mix_in(h0, norm_mix[0:1])
    proj, w_hin = first(xn0, lambda w, s, prev: _mm_nn_shard(xn0, w, s, tm, tn_h, "hgrn_in", prev))
    o, states = _hgrn_fwd(proj, lb, D)
    fetch_hout, o = arrive("hgrn_w_out", o)
    on = _row_call(_onorm_fwd_fn, [(o, 0, D), (proj, 3, D)], [out_norm], [(D, BF16)], 0, "onorm_fwd")[0]
    fetch_up0, on = arrive("ffn_w_up0", on)
    w_hout1 = fetch_hout(on).reshape(1, D, D)
    h1 = _mm_nn(on, w_hout1, wm_sq, D, D, tm, D, tn_d, "hgrn_out", res=h0)
    h2, ffn0, w_up0, w_down0, (fetch_sin, fetch_sout) = ffn_fwd(h1, 0, fetch_up0, ("sc_w_in", "sc_w_out"))
    xn1 = mix_in(h2, norm_mix[1:2])
    w_sin = fetch_sin(xn1)
    tn_si = _pick(cw_s, wide)
    sproj = _mm_nn(xn1, w_sin, _wmap_col(cw_s, tn_si, 0), D, 3 * D, tm, D, tn_si, "sc_in")
    fetch_up1, sproj = arrive("ffn_w_up1", sproj)
    nd = D // LANES
    ysc = _col_call(_sc_fwd_fn, [(sproj, 0), (sproj, nd), (sproj, 2 * nd)], [(sc_conv, 0)], [(1, D, BF16)], 0,
                    "sc_fwd", before=True, after=False)[0][0]
    w_sout1 = fetch_sout(ysc).reshape(1, D, D)
    h3 = _mm_nn(ysc, w_sout1, wm_sq, D, D, tm, D, tn_d, "sc_out", res=h2)
    h4, ffn1, w_up1, w_down1, _ = ffn_fwd(h3, 1, fetch_up1)

    dh, dh16, esq, dfinal = _row_call(_final_fn, [(h4, 0, D), (tgt, 0, D)], [final_norm], [(D, F32), (D, BF16)], 2,
                                      "final_loss")
    loss = 0.5 / D * jnp.sum(esq)
    dh, dh16, dnf1, dconv1 = ffn_bwd(dh, dh16, h3, ffn1, 1, w_up1, w_down1)
    g_sout = _mm_tn(ysc, seg1(dh16), (1, 1, D, D), wm_sq, D, D, tk_w, tn_d, "sc_out_dw")
    dh16 = reduce_start(("sc_w_out", 0), g_sout, dh16)
    dy = _mm_nt(seg1(dh16), w_sout1, wm_sq1, D, D, tm, D, D, "sc_out_dx")
    dsp, dscc = _col_call(_sc_bwd_fn, [(sproj, 0), (sproj, nd), (sproj, 2 * nd), (dy, 0)], [(sc_conv, 0)],
                          [(3, D, BF16)], 1, "sc_bwd", before=True, after=True)
    g_sin = _mm_tn(xn1, dsp, (N_CHIPS, 1, D, cw_s), _wmap_col(cw_s, tn_s, 0), D, 3 * D, tk_w, tn_s, "sc_in_dw")
    dsp = reduce_start(("sc_w_in", 0), g_sin, dsp)
    dxn = _mm_nt(dsp, w_sin, _wmap_col(cw_s, tn_s, 0), D, 3 * D, tm, D, tn_s, "sc_in_dx", per_step=3)
    dh, dh16, dnm1 = rms_bwd(h2, dxn, dh, norm_mix[1:2])
    dh16 = reduce_finish(dh16)
    dh, dh16, dnf0, dconv0 = ffn_bwd(dh, dh16, h1, ffn0, 0, w_up0, w_down0)
    g_hout = _mm_tn(on, seg1(dh16), (1, 1, D, D), wm_sq, D, D, tk_w, tn_d, "hgrn_out_dw")
    dh16 = reduce_start(("hgrn_w_out", 0), g_hout, dh16)
    don = _mm_nt(seg1(dh16), w_hout1, wm_sq1, D, D, tm, D, D, "hgrn_out_dx")
    do, dgate, dgain = _row_call(_onorm_bwd_fn, [(o, 0, D), (proj, 3, D), (don, 0, D)], [out_norm],
                                 [(D, F32), (D, BF16)], 1, "onorm_bwd")
    dproj, dlb = _hgrn_bwd(proj, lb, states, do, dgate, D)
    g_hin = _mm_tn(xn0, dproj, (N_CHIPS, 1, D, cw_h), _wmap_col(cw_h, tn_h, 0), D, 4 * D, tk_w, tn_h, "hgrn_in_dw")
    dproj = reduce_start(("hgrn_w_in", 0), g_hin, dproj)
    dxn = _mm_nt(dproj, w_hin, _wmap_col(cw_h, tn_h, 0), D, 4 * D, tm, D, tn_h, "hgrn_in_dx", per_step=2)
    grad_x, _, dnm0 = rms_bwd(h0, dxn, dh, norm_mix[0:1])

    small = dict(
        loss=loss,
        norm_mix=jnp.stack([jnp.sum(dnm0, axis=0), jnp.sum(dnm1, axis=0)]),
        norm_ffn=jnp.stack([jnp.sum(dnf0, axis=0), jnp.sum(dnf1, axis=0)]),
        lb=dlb[0:1],
        out_norm=jnp.sum(dgain, axis=0)[None],
        final_norm=jnp.sum(dfinal, axis=0),
        sc_conv=dscc[:3],
        ffn_conv=jnp.stack([dconv0, dconv1]),
    )
    return grad_x, small


def kernel(x, norm_mix, norm_ffn, hgrn_w_in, hgrn_lb_table, hgrn_out_norm, hgrn_w_out, sc_w_in, sc_conv, sc_w_out, ffn_w_up, ffn_conv, ffn_w_down, final_norm, loss_target, m_norm_mix, m_norm_ffn, m_hgrn_w_in, m_hgrn_lb_table, m_hgrn_out_norm, m_hgrn_w_out, m_sc_w_in, m_sc_conv, m_sc_w_out, m_ffn_w_up, m_ffn_conv, m_ffn_w_down, m_final_norm, v_norm_mix, v_norm_ffn, v_hgrn_w_in, v_hgrn_lb_table, v_hgrn_out_norm, v_hgrn_w_out, v_sc_w_in, v_sc_conv, v_sc_w_out, v_ffn_w_up, v_ffn_conv, v_ffn_w_down, v_final_norm):
    D = x.shape[-1]
    xi, yi, ci = lax.axis_index("x"), lax.axis_index("y"), lax.axis_index("c")
    me_chip = (2 * xi + yi).astype(jnp.int32).reshape(1)
    me_core = ci.astype(jnp.int32).reshape(1)

    big_names = ["hgrn_w_in", "hgrn_w_out", "sc_w_in", "sc_w_out", "ffn_w_up", "ffn_w_down"]
    big_w = dict(hgrn_w_in=hgrn_w_in, hgrn_w_out=hgrn_w_out, sc_w_in=sc_w_in, sc_w_out=sc_w_out,
                 ffn_w_up=ffn_w_up, ffn_w_down=ffn_w_down)
    big_m = dict(hgrn_w_in=m_hgrn_w_in, hgrn_w_out=m_hgrn_w_out, sc_w_in=m_sc_w_in, sc_w_out=m_sc_w_out,
                 ffn_w_up=m_ffn_w_up, ffn_w_down=m_ffn_w_down)
    big_v = dict(hgrn_w_in=v_hgrn_w_in, hgrn_w_out=v_hgrn_w_out, sc_w_in=v_sc_w_in, sc_w_out=v_sc_w_out,
                 ffn_w_up=v_ffn_w_up, ffn_w_down=v_ffn_w_down)
    flat2 = lambda a: a.reshape(-1, a.shape[-1])

    sh = lambda a: a.reshape(-1, a.shape[-1]).astype(BF16)
    conv_shards = [flat2(sc_conv), flat2(ffn_conv)]
    in_order_of_use = [("hgrn_w_in", sh(hgrn_w_in)), ("hgrn_w_out", sh(hgrn_w_out)), ("ffn_w_up0", sh(ffn_w_up[0])),
                       ("ffn_w_down0", sh(ffn_w_down[0])), ("sc_w_in", sh(sc_w_in)), ("sc_w_out", sh(sc_w_out)),
                       ("ffn_w_up1", sh(ffn_w_up[1])), ("ffn_w_down1", sh(ffn_w_down[1]))]
    scc4, fcc4 = _gather_weights([], [flat2(sc_conv), flat2(ffn_conv)])
    names = [n for n, _ in in_order_of_use]
    ss, rs, src, land, (scc4, norm_mix) = _gather_start([s for _, s in in_order_of_use], [scc4, norm_mix], "gather_start")
    travelling = {n: (s, l) for n, s, l in zip(names, src, land)}

    def landed(name, after, call, which=None):
        (s,), (l,) = _gather_wait(ss, rs, [travelling[name][0]], [travelling[name][1]], after, call,
                                  base=names.index(name), which=which)
        travelling[name] = (s, l)
        return l

    scc = jnp.moveaxis(scc4, 0, 1).reshape(3, D)
    f2 = ffn_conv.shape[-1] * N_CHIPS
    fcc = jnp.moveaxis(fcc4.reshape(N_CHIPS, 2, 3, -1), 0, 2).reshape(2, 3, f2)

    def first(after, matmul):
        w = landed("hgrn_w_in", after, "gather_wait_0_own", which=(3,))
        proj = matmul(w, me_chip, None)
        others = [2 * (1 - xi) + yi, 2 * xi + (1 - yi), 2 * (1 - xi) + (1 - yi)]
        for j, s in enumerate(others):
            w = landed("hgrn_w_in", proj, "gather_wait_0_%d" % j, which=(j,))
            fs, fr, w, proj = _forward_start(w, proj, "gather_forward_start_0_%d" % j, which=(j,))
            w = _forward_wait(fs, fr, w, proj, "gather_forward_wait_0_%d" % j, which=(j,))
            travelling["hgrn_w_in"] = (travelling["hgrn_w_in"][0], w)
            proj = matmul(w, s.astype(jnp.int32).reshape(1), proj)
        return proj, w

    def arrive(name, after):
        fs, fr, w, after = _forward_start(landed(name, after, "gather_wait_" + name), after, "gather_forward_start_" + name)
        return functools.partial(_forward_wait, fs, fr, w, name="gather_forward_wait_" + name), after

    pending = []
    started = []

    def reduce_start(slot, grad, thru):
        t = sum(len(b[0]) for b in pending) + len(started)
        halves = grad.reshape(N_CHIPS, 2, -1, grad.shape[-1])
        ss, rs, src, land, thru = _sibling_start([halves], True, thru, "grad_pair_start_%d" % t)
        started.append((slot, t, ss, rs, src, land))
        return thru

    def reduce_finish(thru):
        k = len(pending)
        pair = []
        for slot, t, ss, rs, src, land in started:
            src, recv = _sibling_wait(ss, rs, src, land, True, thru, "grad_pair_wait_%d" % t)
            pair.append(_add_pair(src[0], recv[0], me_core, "grad_add_pair"))
        ss, rs, pair, land, thru = _chip_start(pair, thru, "grad_chip_start_%d" % k)
        pending.append(([s[0] for s in started], ss, rs, pair, land))
        started.clear()
        return thru

    lb8 = _lb_softmax(hgrn_lb_table)
    grad_x, small = _local_step(
        x[0], loss_target[0], norm_mix, norm_ffn, lb8, hgrn_out_norm, final_norm[None], scc, fcc, first, arrive,
        reduce_start, reduce_finish)

    small_names = ["loss", "norm_mix", "norm_ffn", "lb", "out_norm", "final_norm", "sc_conv", "ffn_conv"]
    parts = [small[n].astype(F32) for n in small_names]
    shapes = [p.shape for p in parts]
    tot = dict(zip(small_names, _unpack(reduce_finish(_all_sum(_pack(parts))), shapes)))
    loss = tot["loss"].reshape(())
    g_lb_table = _lb_table_grad(lb8, tot["lb"], hgrn_lb_table.shape[0])
    cw = sc_conv.shape[-1]
    g_sc_conv = lax.dynamic_slice_in_dim(tot["sc_conv"], me_chip[0] * cw, cw, axis=1)[None]
    cf = ffn_conv.shape[-1]
    g_ffn_conv = lax.dynamic_slice_in_dim(tot["ffn_conv"], me_chip[0] * cf, cf, axis=2)
    g_small = dict(norm_mix=tot["norm_mix"], norm_ffn=tot["norm_ffn"], hgrn_lb_table=g_lb_table,
                   hgrn_out_norm=tot["out_norm"], sc_conv=g_sc_conv, ffn_conv=g_ffn_conv, final_norm=tot["final_norm"])
    w_small = dict(norm_mix=norm_mix, norm_ffn=norm_ffn, hgrn_lb_table=hgrn_lb_table, hgrn_out_norm=hgrn_out_norm,
                   sc_conv=sc_conv, ffn_conv=ffn_conv, final_norm=final_norm)
    m_small = dict(norm_mix=m_norm_mix, norm_ffn=m_norm_ffn, hgrn_lb_table=m_hgrn_lb_table, hgrn_out_norm=m_hgrn_out_norm,
                   sc_conv=m_sc_conv, ffn_conv=m_ffn_conv, final_norm=m_final_norm)
    v_small = dict(norm_mix=v_norm_mix, norm_ffn=v_norm_ffn, hgrn_lb_table=v_hgrn_lb_table, hgrn_out_norm=v_hgrn_out_norm,
                   sc_conv=v_sc_conv, ffn_conv=v_ffn_conv, final_norm=v_final_norm)
    sm_names = list(g_small)
    sm_shapes = [w_small[n].shape for n in sm_names]
    d_s, m_s, v_s = _adamw(_pack([w_small[n] for n in sm_names]), _pack([g_small[n] for n in sm_names]),
                           _pack([m_small[n] for n in sm_names]), _pack([v_small[n] for n in sm_names]), "adamw_small")
    out_g, out_d, out_m, out_v = dict(g_small), {}, {}, {}
    for n, d_, m_, v_ in zip(sm_names, _unpack(d_s, sm_shapes), _unpack(m_s, sm_shapes), _unpack(v_s, sm_shapes)):
        out_d[n], out_m[n], out_v[n] = d_, m_, v_

    done = {}
    after = grad_x

    def add_and_share(k, after):
        slots, ss, rs, pair, land = pending[k]
        pair, recv = _chip_wait(ss, rs, pair, land, after, "grad_chip_wait_%d" % k)
        mine = [_add_chips(p, r, me_chip, "grad_add_chips") for p, r in zip(pair, recv)]
        ss, rs, mine, land, _ = _sibling_start(mine, False, None, "grad_share_start_%d" % k)
        return slots, ss, rs, mine, land

    def update(k, share, after):
        slots, ss, rs, mine, land = share
        mine, theirs = _sibling_wait(ss, rs, mine, land, False, after, "grad_share_wait_%d" % k)
        for (n, layer), gm, gr in zip(slots, mine, theirs):
            done[n] = _adamw_halves(flat2(big_w[n]), flat2(big_m[n]), flat2(big_v[n]), gm, gr, me_core, "adamw_" + n,
                                    layer=layer, prev=done.get(n))
        return done[slots[-1][0]][0]

    shares = []
    for k in range(len(pending) - 1):
        shares.append(add_and_share(k, after))
        after = shares[-1][3][0]
    for k, share in enumerate(shares):
        after = update(k, share, after)
    last = len(pending) - 1
    share = add_and_share(last, after)
    update(last, share, share[3][0])
    for n in big_names:
        out_g[n], out_d[n], out_m[n], out_v[n] = (a.reshape(big_w[n].shape) for a in done[n])

    order = ["norm_mix", "norm_ffn", "hgrn_w_in", "hgrn_lb_table", "hgrn_out_norm", "hgrn_w_out", "sc_w_in", "sc_conv",
             "sc_w_out", "ffn_w_up", "ffn_conv", "ffn_w_down", "final_norm"]
    return (loss, grad_x[None], *[out_g[n] for n in order], *[out_d[n] for n in order],
            *[out_m[n] for n in order], *[out_v[n] for n in order])
```

```python
import functools

import jax
import jax.numpy as jnp
from jax import lax
from jax.experimental import pallas as pl
from jax.experimental.pallas import tpu as pltpu

F32 = jnp.float32
BF16 = jnp.bfloat16
MESH = pl.DeviceIdType.MESH

EPS = 1e-6
CHUNK = 64
HEAD = 128
N_CHIPS = 4
ADAM_LR, ADAM_B1, ADAM_B2, ADAM_EPS, ADAM_WD, ADAM_STEP = 0.001, 0.9, 0.999, 1e-08, 0.01, 10
VMEM_LIMIT = 56 * 1024 * 1024
SUBLANES = 8
LANES = 128


def _pcall(body, **kw):
    return pl.pallas_call(body, **kw)


def _params(sem, vmem=VMEM_LIMIT):
    return pltpu.CompilerParams(dimension_semantics=sem, vmem_limit_bytes=vmem)


def _pick(dim, prefs):
    for p in prefs:
        if p <= dim and dim % p == 0:
            return p
    return dim


def _sigmoid(x):
    return 1.0 / (1.0 + jnp.exp(-x))


def _wmap_col(cw, tn, r0):
    bps = cw // tn
    return lambda kb, nb: (nb // bps, r0 + kb, nb % bps)


def _wmap_row(kp, tk, r0):
    bps = kp // tk
    return lambda kb, nb: (kb // bps, r0 + kb % bps, nb)


def _mm_nn(a, w3, wmap, K, N, tm, tk, tn, name, res=None, per_step=1):
    M = a.shape[0]
    u = per_step
    nk = K // (tk * u)

    def body(*refs):
        r_ref = None if res is None else refs[2 * u]
        o_ref = refs[2 * u + (0 if res is None else 1)]
        p = None
        for r in range(u):
            d = jnp.dot(refs[r][...], refs[u + r][...], preferred_element_type=F32)
            p = d if p is None else p + d
        if nk == 1:
            o_ref[...] = p if res is None else p + r_ref[...]
            return
        acc = refs[-1]
        k = pl.program_id(2)

        @pl.when(k == 0)
        def _():
            acc[...] = p

        @pl.when(k > 0)
        def _():
            acc[...] += p

        @pl.when(k == nk - 1)
        def _():
            o_ref[...] = acc[...] if res is None else acc[...] + r_ref[...]

    if nk == 1:
        grid = (M // tm, N // tn)
        ix = lambda f: (lambda i, j: f(i, j, 0))
        sem = ("parallel", "parallel")
        scratch = []
    else:
        grid = (M // tm, N // tn, nk)
        ix = lambda f: f
        sem = ("parallel", "parallel", "arbitrary")
        scratch = [pltpu.VMEM((tm, tn), F32)]
    def a_spec(r):
        return pl.BlockSpec((tm, tk), ix(lambda i, j, k: (i, k * u + r)))

    def w_spec(r):
        return pl.BlockSpec((None, tk, tn), ix(lambda i, j, k: wmap(k * u + r, j)))

    in_specs = [a_spec(r) for r in range(u)] + [w_spec(r) for r in range(u)]
    args = [a] * u + [w3] * u
    if res is not None:
        in_specs.append(pl.BlockSpec((tm, tn), ix(lambda i, j, k: (i, j))))
        args.append(res)
    return _pcall(
        body, name=name, grid=grid, in_specs=in_specs,
        out_specs=pl.BlockSpec((tm, tn), ix(lambda i, j, k: (i, j))),
        out_shape=jax.ShapeDtypeStruct((M, N), F32), scratch_shapes=scratch, compiler_params=_params(sem),
    )(*args)


def _mm_nn_shard(a, w3, s, tm, tn, name, prev=None):
    M, K = a.shape
    S, _, cw = w3.shape
    bps = cw // tn

    def body(s_ref, a_ref, w_ref, *rest):
        o_ref = rest[-1]
        o_ref[...] = jnp.dot(a_ref[...], w_ref[...], preferred_element_type=F32)

    in_specs = [pl.BlockSpec((tm, K), lambda i, j, sr: (i, 0)),
                pl.BlockSpec((None, K, tn), lambda i, j, sr: (sr[0], 0, j))]
    args = [s, a, w3]
    alias = {}
    if prev is not None:
        in_specs.append(pl.BlockSpec(memory_space=pl.ANY))
        args.append(prev)
        alias = {3: 0}
    return _pcall(
        body, name=name,
        grid_spec=pltpu.PrefetchScalarGridSpec(
            num_scalar_prefetch=1, grid=(M // tm, bps), in_specs=in_specs,
            out_specs=pl.BlockSpec((tm, tn), lambda i, j, sr: (i, sr[0] * bps + j))),
        out_shape=jax.ShapeDtypeStruct((M, S * cw), F32), input_output_aliases=alias,
        compiler_params=_params(("parallel", "parallel")),
    )(*args)


def _mm_nt(dy3, w3, wmap, K, N, tm, tk, tn, name, per_step=1):
    M = dy3.shape[1]
    bps = dy3.shape[2] // tn
    u = per_step
    grid = (M // tm, K // tk, N // (tn * u))
    nn = grid[2]

    def body(*refs):
        o_ref = refs[-1]
        p = None
        for r in range(u):
            d = lax.dot_general(refs[r][...], refs[u + r][...], (((1,), (1,)), ((), ())), preferred_element_type=F32)
            p = d if p is None else p + d
        if nn == 1:
            o_ref[...] = p
            return
        n = pl.program_id(2)

        @pl.when(n == 0)
        def _():
            o_ref[...] = p

        @pl.when(n > 0)
        def _():
            o_ref[...] += p

    def dy_spec(r):
        return pl.BlockSpec((None, tm, tn), lambda i, j, n: ((n * u + r) // bps, i, (n * u + r) % bps))

    def w_spec(r):
        return pl.BlockSpec((None, tk, tn), lambda i, j, n: wmap(j, n * u + r))

    return _pcall(
        body, name=name, grid=grid,
        in_specs=[dy_spec(r) for r in range(u)] + [w_spec(r) for r in range(u)],
        out_specs=pl.BlockSpec((tm, tk), lambda i, j, n: (i, j)),
        out_shape=jax.ShapeDtypeStruct((M, K), F32),
        compiler_params=_params(("parallel", "parallel", "arbitrary")),
    )(*([dy3] * u), *([w3] * u))


def _mm_tn(x, dy3, shape4, wmap, K, N, tk, tn, name, tm=None):
    M = x.shape[0]
    tm = M if tm is None else tm
    nm = M // tm
    bps = dy3.shape[2] // tn

    def body(*refs):
        x_ref, dy_ref = refs[:2]
        p = lax.dot_general(x_ref[...], dy_ref[...], (((0,), (0,)), ((), ())), preferred_element_type=F32)
        if nm == 1:
            o_ref = refs[-1]
            o_ref[...] = p.astype(o_ref.dtype)
            return
        o_ref, acc = refs[-2:]
        m = pl.program_id(2)

        @pl.when(m == 0)
        def _():
            acc[...] = p

        @pl.when(m > 0)
        def _():
            acc[...] += p

        @pl.when(m == nm - 1)
        def _():
            o_ref[...] = acc[...].astype(o_ref.dtype)

    def omap(i, j, m):
        s, rb, cb = wmap(i, j)
        return (s, 0, rb, cb)

    return _pcall(
        body, name=name, grid=(K // tk, N // tn, nm),
        in_specs=[pl.BlockSpec((tm, tk), lambda i, j, m: (m, i)),
                  pl.BlockSpec((None, tm, tn), lambda i, j, m: (j // bps, m, j % bps))],
        out_specs=pl.BlockSpec((None, None, tk, tn), omap),
        out_shape=jax.ShapeDtypeStruct(shape4, BF16),
        scratch_shapes=[] if nm == 1 else [pltpu.VMEM((tk, tn), F32)],
        compiler_params=_params(("parallel", "parallel", "arbitrary")),
    )(x, dy3)


def _row_call(fn, rows, vecs, outs, n_acc, name, t_rows=256, sub=16, per_trip=4):
    T = rows[0][0].shape[0]
    t_rows = min(t_rows, T)
    nsub = t_rows // sub
    n_r, n_v, n_o = len(rows), len(vecs), len(outs)
    width = rows[0][2]

    def body(*refs):
        r_refs = refs[:n_r]
        v_refs = refs[n_r:n_r + n_v]
        o_refs = refs[n_r + n_v:n_r + n_v + n_o]
        a_refs = refs[n_r + n_v + n_o:]

        @pl.when(pl.program_id(0) == 0)
        def _():
            for a in a_refs:
                a[...] = jnp.zeros_like(a)

        vv = [v[...] for v in v_refs]

        def step(i, carry):
            done = []
            for u in range(per_trip):
                sl = pl.ds(pl.multiple_of((i * per_trip + u) * sub, sub), sub)
                done.append((sl,) + tuple(fn([r[sl, :] for r in r_refs], vv)))
            for sl, o_vals, a_vals in done:
                for o, val in zip(o_refs, o_vals):
                    o[sl, :] = val.astype(o.dtype)
            for a_i, a in enumerate(a_refs):
                tot = None
                for _, _, a_vals in done:
                    part = a_vals[a_i].reshape(sub // SUBLANES, SUBLANES, a_vals[a_i].shape[-1]).sum(axis=0)
                    tot = part if tot is None else tot + part
                a[...] += tot
            return carry

        lax.fori_loop(0, nsub // per_trip, step, 0)

    in_specs = [pl.BlockSpec((t_rows, w), functools.partial(lambda i, cb: (i, cb), cb=cb)) for _, cb, w in rows]
    in_specs += [pl.BlockSpec(v.shape, lambda i: (0, 0)) for v in vecs]
    out_specs = [pl.BlockSpec((t_rows, w), lambda i: (i, 0)) for w, _ in outs]
    out_specs += [pl.BlockSpec((SUBLANES, width), lambda i: (0, 0)) for _ in range(n_acc)]
    out_shape = [jax.ShapeDtypeStruct((T, w), dt) for w, dt in outs]
    out_shape += [jax.ShapeDtypeStruct((SUBLANES, width), F32) for _ in range(n_acc)]
    return _pcall(
        body, name=name, grid=(T // t_rows,), in_specs=in_specs, out_specs=out_specs, out_shape=out_shape,
        compiler_params=_params(("arbitrary",)),
    )(*[r[0] for r in rows], *vecs)


def _rms_fwd_fn(rv, vv):
    h, = rv
    w, = vv
    r = lax.rsqrt(jnp.mean(h * h, axis=-1, keepdims=True) + EPS)
    return [h * r * w], []


def _rms_bwd_fn(rv, vv):
    h, dxn, dh_in = rv
    w, = vv
    d = h.shape[-1]
    r = lax.rsqrt(jnp.mean(h * h, axis=-1, keepdims=True) + EPS)
    gy = dxn * w
    dh = r * gy - h * ((r * r * r) * (1.0 / d) * jnp.sum(gy * h, axis=-1, keepdims=True))
    return [dh_in + dh] * 2, [dxn * h * r]


def _final_fn(rv, vv):
    h, tgt = rv
    w, = vv
    d = h.shape[-1]
    r = lax.rsqrt(jnp.mean(h * h, axis=-1, keepdims=True) + EPS)
    hn = h * r
    e = hn * w - tgt
    dy = e * (1.0 / d)
    gy = dy * w
    dh = r * gy - h * ((r * r * r) * (1.0 / d) * jnp.sum(gy * h, axis=-1, keepdims=True))
    return [dh] * 2, [e * e, dy * hn]


def _onorm_fwd_fn(rv, vv):
    o, g = rv
    gain, = vv
    r = lax.rsqrt(jnp.mean(o * o, axis=-1, keepdims=True) + EPS)
    return [o * r * gain * (g * _sigmoid(g))], []


def _onorm_bwd_fn(rv, vv):
    o, g, don = rv
    gain, = vv
    d = o.shape[-1]
    r = lax.rsqrt(jnp.mean(o * o, axis=-1, keepdims=True) + EPS)
    sg = _sigmoid(g)
    sl = g * sg
    n = o * r
    dg = don * n * gain * (sg * (1.0 + g * (1.0 - sg)))
    gy = don * sl * gain
    do = r * gy - o * ((r * r * r) * (1.0 / d) * jnp.sum(gy * o, axis=-1, keepdims=True))
    return [do, dg], [don * sl * n]


HALO = SUBLANES


def _col_call(fn, cols, vecs, outs, n_acc, name, before, after, tc=LANES, chunk=128):
    T = cols[0][0].shape[0]
    chunk = min(chunk, T)
    nch = T // chunk
    ncol = outs[0][1] // tc
    n_c, n_v, n_o = len(cols), len(vecs), len(outs)
    hb = HALO if before else 0
    rw = chunk + hb + (HALO if after else 0)

    def body(*refs):
        c_refs = refs[:n_c]
        v_refs = refs[n_c:n_c + n_v]
        o_refs = refs[n_c + n_v:n_c + n_v + n_o]
        a_refs = refs[n_c + n_v + n_o:]
        vv = [v[...] for v in v_refs]
        wrow = lax.broadcasted_iota(jnp.int32, (rw, tc), 0)
        inside = (wrow >= hb) & (wrow < hb + chunk)

        def step(i, carry):
            r0 = pl.multiple_of(i * chunk, chunk)
            wins = []
            for ref in c_refs:
                parts = []
                if before:
                    pb = ref[pl.ds(pl.multiple_of(jnp.maximum(r0 - HALO, 0), HALO), HALO), :]
                    parts.append(jnp.where(i > 0, pb, 0.0))
                parts.append(ref[pl.ds(r0, chunk), :])
                if after:
                    pa = ref[pl.ds(pl.multiple_of(jnp.minimum(r0 + chunk, T - HALO), HALO), HALO), :]
                    parts.append(jnp.where(i < nch - 1, pa, 0.0))
                wins.append(jnp.concatenate(parts, axis=0) if len(parts) > 1 else parts[0])
            o_vals, a_vals = fn(wins, vv, inside)
            p = 0
            for o, (nseg, _, _) in zip(o_refs, outs):
                for s in range(nseg):
                    o[s, pl.ds(r0, chunk), :] = o_vals[p][hb:hb + chunk].astype(o.dtype)
                    p += 1
            return tuple(c + a for c, a in zip(carry, a_vals))

        taps = [v.shape[0] for v, _ in vecs][:n_acc]
        init = tuple(jnp.zeros((1, tc), F32) for k in taps for _ in range(k))
        sums = lax.fori_loop(0, nch, step, init)
        arow = lax.broadcasted_iota(jnp.int32, (SUBLANES, tc), 0)
        p = 0
        for a, k in zip(a_refs, taps):
            acc = jnp.zeros((SUBLANES, tc), F32)
            for t in range(k):
                acc = jnp.where(arow == t, sums[p], acc)
                p += 1
            a[...] = acc

    in_specs = [pl.BlockSpec((T, tc), functools.partial(lambda j, off: (0, off + j), off=off)) for _, off in cols]
    in_specs += [pl.BlockSpec((v.shape[0], tc), functools.partial(lambda j, off: (0, off + j), off=off))
                 for v, off in vecs]
    out_specs = [pl.BlockSpec((nseg, T, tc), lambda j: (0, 0, j)) for nseg, _, _ in outs]
    out_specs += [pl.BlockSpec((SUBLANES, tc), lambda j: (0, j)) for _ in range(n_acc)]
    out_shape = [jax.ShapeDtypeStruct((nseg, T, w), dt) for nseg, w, dt in outs]
    out_shape += [jax.ShapeDtypeStruct((SUBLANES, ncol * tc), F32) for _ in range(n_acc)]
    return _pcall(
        body, name=name, grid=(ncol,), in_specs=in_specs, out_specs=out_specs, out_shape=out_shape,
        compiler_params=_params(("parallel",)),
    )(*[c[0] for c in cols], *[v[0] for v in vecs])


def _down(x, k):
    return x if k == 0 else pltpu.roll(x, k, 0)


def _up(x, k):
    return x if k == 0 else pltpu.roll(x, x.shape[0] - k, 0)


def _lags(x):
    return _down(x, 2), _down(x, 1), x


def _conv(lags, w):
    return w[0:1] * lags[0] + w[1:2] * lags[1] + w[2:3] * lags[2]


def _conv_t(d, w):
    return w[2:3] * d + w[1:2] * _up(d, 1) + w[0:1] * _up(d, 2)


def _tap_sums(d, lags, inside):
    dm = jnp.where(inside, d, 0.0)
    return [jnp.sum(dm * lag, axis=0, keepdims=True) for lag in lags]


def _glu_fwd_fn(wins, vv, inside):
    xg, xv = wins
    wg, wv = vv
    ug = _conv(_lags(xg), wg)
    uv = _conv(_lags(xv), wv)
    return [ug * _sigmoid(ug) * uv], []


def _glu_bwd_fn(wins, vv, inside):
    xg, xv, da = wins
    wg, wv = vv
    lg, lv = _lags(xg), _lags(xv)
    ug = _conv(lg, wg)
    uv = _conv(lv, wv)
    sg = _sigmoid(ug)
    dug = da * uv * (sg * (1.0 + ug * (1.0 - sg)))
    duv = da * (ug * sg)
    return [_conv_t(dug, wg), _conv_t(duv, wv)], _tap_sums(dug, lg, inside) + _tap_sums(duv, lv, inside)


def _sc_fwd_fn(wins, vv, inside):
    gb, gc, hh = wins
    w, = vv
    return [gb * _conv(_lags(gc * hh), w)], []


def _sc_bwd_fn(wins, vv, inside):
    gb, gc, hh, dy = wins
    w, = vv
    lz = _lags(gc * hh)
    dcv = dy * gb
    dz = _conv_t(dcv, w)
    return [dy * _conv(lz, w), dz * hh, dz * gc], _tap_sums(dcv, lz, inside)


def _gates(qr, fr, lb):
    sg = _sigmoid(fr)
    f = lb + (1.0 - lb) * sg
    sq = _sigmoid(qr)
    q = qr * sq * (HEAD ** -0.5)
    return q, 1.0 - f, jnp.log(f), f, sg, sq


def _boundary_rows(b, g, row):
    c = b.shape[0]
    if 2 * g >= SUBLANES:
        x = b.reshape(c // (2 * g), 2 * g, LANES)
        return jnp.broadcast_to(x[:, g - 1:g, :], x.shape).reshape(c, LANES)
    x = b.reshape(c // SUBLANES, SUBLANES, LANES)
    lo = jnp.broadcast_to(x[:, 1:2, :], x.shape).reshape(c, LANES)
    hi = jnp.broadcast_to(x[:, 5:6, :], x.shape).reshape(c, LANES)
    return jnp.where((row & 4) == 0, lo, hi)


def _chunk_decays(gl, f, row):
    c = gl.shape[0]
    b = gl
    d = 1
    while d < c:
        b = b + jnp.where(row >= d, pltpu.roll(b, d, 0), 0.0)
        d *= 2
    eq, ek = [], []
    g = c // 2
    while g >= 2:
        right = (row & g) != 0
        m = _boundary_rows(b, g, row)
        z = jnp.exp(jnp.where(right, b - m, m - b))
        eq.append(jnp.where(right, z, 0.0))
        ek.append(jnp.where(right, 0.0, z))
        g //= 2
    odd = (row & 1) != 0
    eq.append(jnp.where(odd, f, 0.0))
    ek.append(jnp.where(odd, 0.0, 1.0))
    return b, eq, ek


def _intra(q, k, eq, ek, tt, ss):
    c = q.shape[0]
    qs, ks = [], []
    a = jnp.where(tt == ss, jnp.sum(q * k, axis=1, keepdims=True), 0.0)
    g = c // 2
    for e_q, e_k in zip(eq, ek):
        qg = (q * e_q).astype(BF16)
        kg = (k * e_k).astype(BF16)
        p = lax.dot_general(qg, kg, (((1,), (1,)), ((), ())), preferred_element_type=F32)
        a = a + (p if 2 * g >= c else jnp.where((tt ^ ss) < 2 * g, p, 0.0))
        qs.append(qg)
        ks.append(kg)
        g //= 2
    return a, qs, ks


def _hgrn_fwd(proj, lb, d_model):
    T = proj.shape[0]
    H = d_model // HEAD
    nch = T // CHUNK

    def body(q_ref, f_ref, v_ref, lb_ref, o_ref, s_ref):
        lbv = lb_ref[...]
        row = lax.broadcasted_iota(jnp.int32, (CHUNK, HEAD), 0)
        tt = lax.broadcasted_iota(jnp.int32, (CHUNK, CHUNK), 0)
        ss = lax.broadcasted_iota(jnp.int32, (CHUNK, CHUNK), 1)

        def step(i, st):
            sl = pl.ds(pl.multiple_of(i * CHUNK, CHUNK), CHUNK)
            q, k, gl, f, _, _ = _gates(q_ref[sl, :], f_ref[sl, :], lbv)
            v = v_ref[sl, :].astype(BF16)
            b, eq, ek = _chunk_decays(gl, f, row)
            a, _, _ = _intra(q, k, eq, ek, tt, ss)
            bl = b[CHUNK - 1:CHUNK, :]
            q0 = (q * jnp.exp(b)).astype(BF16)
            kh = (k * jnp.exp(bl - b)).astype(BF16)
            s_ref[i] = st
            o = jnp.dot(a.astype(BF16), v, preferred_element_type=F32)
            o = o + lax.dot_general(q0, st.astype(BF16), (((1,), (1,)), ((), ())), preferred_element_type=F32)
            o_ref[sl, :] = o
            return jnp.exp(bl) * st + lax.dot_general(v, kh, (((0,), (0,)), ((), ())), preferred_element_type=F32)

        per = 4 if nch % 4 == 0 else 2

        def trip(i, st):
            for u in range(per):
                st = step(per * i + u, st)
            return st

        lax.fori_loop(0, nch // per, trip, jnp.zeros((HEAD, HEAD), F32))

    col = lambda off: pl.BlockSpec((T, HEAD), functools.partial(lambda h, off: (0, off + h), off=off))
    return _pcall(
        body, name="hgrn_fwd", grid=(H,),
        in_specs=[col(0), col(H), col(2 * H), pl.BlockSpec((1, HEAD), lambda h: (0, h))],
        out_specs=[pl.BlockSpec((T, HEAD), lambda h: (0, h)),
                   pl.BlockSpec((None, nch, HEAD, HEAD), lambda h: (h, 0, 0, 0))],
        out_shape=[jax.ShapeDtypeStruct((T, d_model), F32), jax.ShapeDtypeStruct((H, nch, HEAD, HEAD), F32)],
        compiler_params=_params(("parallel",)),
    )(proj, proj, proj, lb)


def _hgrn_bwd(proj, lb, states, do, dgate, d_model):
    T = proj.shape[0]
    H = d_model // HEAD
    nch = T // CHUNK

    def body(q_ref, f_ref, v_ref, lb_ref, s_ref, do_ref, dg_ref, dp_ref, dlb_ref):
        dq_ref, df_ref, dv_ref = dp_ref.at[0], dp_ref.at[1], dp_ref.at[2]
        dp_ref[3] = dg_ref[...]
        lbv = lb_ref[...]
        row = lax.broadcasted_iota(jnp.int32, (CHUNK, HEAD), 0)
        tt = lax.broadcasted_iota(jnp.int32, (CHUNK, CHUNK), 0)
        ss = lax.broadcasted_iota(jnp.int32, (CHUNK, CHUNK), 1)
        last = row == CHUNK - 1
        nt = (((1,), (1,)), ((), ()))
        tn = (((0,), (0,)), ((), ()))

        def step(j, carry):
            dst, dlb = carry
            i = nch - 1 - j
            sl = pl.ds(pl.multiple_of(i * CHUNK, CHUNK), CHUNK)
            qr = q_ref[sl, :]
            q, k, gl, f, sg, sq = _gates(qr, f_ref[sl, :], lbv)
            v = v_ref[sl, :].astype(BF16)
            d_o = do_ref[sl, :].astype(BF16)
            st = s_ref[i]
            st16 = st.astype(BF16)
            dst16 = dst.astype(BF16)
            b, eq, ek = _chunk_decays(gl, f, row)
            a, qs, ks = _intra(q, k, eq, ek, tt, ss)
            bl = b[CHUNK - 1:CHUNK, :]
            e0 = jnp.exp(b)
            eh = jnp.exp(bl - b)
            ebl = jnp.exp(bl)
            q0 = q * e0
            kh = k * eh
            q016 = q0.astype(BF16)
            kh16 = kh.astype(BF16)
            dv = lax.dot_general(a.astype(BF16), d_o, tn, preferred_element_type=F32)
            dv = dv + lax.dot_general(kh16, dst16, nt, preferred_element_type=F32)
            dv_ref[sl, :] = dv.astype(dv_ref.dtype)
            da = lax.dot_general(d_o, v, nt, preferred_element_type=F32)
            da = jnp.where(tt >= ss, da, 0.0)
            dd = jnp.sum(jnp.where(tt == ss, da, 0.0), axis=1, keepdims=True)
            dq0 = jnp.dot(d_o, st16, preferred_element_type=F32)
            dkh = jnp.dot(v, dst16, preferred_element_type=F32)
            dq = dq0 * e0 + dd * k
            dk = dkh * eh + dd * q
            db = dq0 * q016.astype(F32) - dkh * kh16.astype(F32)
            g = CHUNK // 2
            for e_q, e_k, qg, kg in zip(eq, ek, qs, ks):
                dag = (da if 2 * g >= CHUNK else jnp.where((tt ^ ss) < 2 * g, da, 0.0)).astype(BF16)
                dqg = jnp.dot(dag, kg, preferred_element_type=F32)
                dkg = lax.dot_general(dag, qg, tn, preferred_element_type=F32)
                dq = dq + dqg * e_q
                dk = dk + dkg * e_k
                db = db + (dqg * qg.astype(F32) - dkg * kg.astype(F32))
                g //= 2
            dbl = jnp.sum(dkh * kh16.astype(F32), axis=0, keepdims=True) + ebl * jnp.sum(dst * st, axis=0, keepdims=True)
            db = db + jnp.where(last, dbl, 0.0)
            d = 1
            while d < CHUNK:
                db = db + jnp.where(row < CHUNK - d, pltpu.roll(db, CHUNK - d, 0), 0.0)
                d *= 2
            dfg = db / f - dk
            df_ref[sl, :] = (dfg * (1.0 - lbv) * sg * (1.0 - sg)).astype(df_ref.dtype)
            dq_ref[sl, :] = (dq * (HEAD ** -0.5) * (sq * (1.0 + qr * (1.0 - sq)))).astype(dq_ref.dtype)
            dlb = dlb + jnp.sum(dfg * (1.0 - sg), axis=0, keepdims=True)
            dst = ebl * dst + lax.dot_general(d_o, q016, tn, preferred_element_type=F32)
            return dst, dlb

        _, dlb = lax.fori_loop(0, nch // 2, lambda j, cr: step(2 * j + 1, step(2 * j, cr)),
                               (jnp.zeros((HEAD, HEAD), F32), jnp.zeros((1, HEAD), F32)))
        arow = lax.broadcasted_iota(jnp.int32, (SUBLANES, HEAD), 0)
        dlb_ref[...] = jnp.where(arow == 0, dlb, 0.0)

    col = lambda off: pl.BlockSpec((T, HEAD), functools.partial(lambda h, off: (0, off + h), off=off))
    return _pcall(
        body, name="hgrn_bwd", grid=(H,),
        in_specs=[col(0), col(H), col(2 * H), pl.BlockSpec((1, HEAD), lambda h: (0, h)),
                  pl.BlockSpec((None, nch, HEAD, HEAD), lambda h: (h, 0, 0, 0)), col(0), col(0)],
        out_specs=[pl.BlockSpec((4, T, HEAD), lambda h: (0, 0, h)), pl.BlockSpec((SUBLANES, HEAD), lambda h: (0, h))],
        out_shape=[jax.ShapeDtypeStruct((4, T, d_model), BF16), jax.ShapeDtypeStruct((SUBLANES, d_model), F32)],
        compiler_params=_params(("parallel",)),
    )(proj, proj, proj, lb, states, do, dgate)


def _lb_softmax(table):
    n, f = table.shape

    def body(t_ref, p_ref):
        t = t_ref[...]
        e = jnp.exp(t - jnp.max(t, axis=0, keepdims=True))
        p_ref[...] = e / jnp.sum(e, axis=0, keepdims=True)

    padded = jnp.pad(table, ((0, SUBLANES - n), (0, 0)), constant_values=-jnp.inf)
    return _pcall(body, name="lb_softmax", out_shape=jax.ShapeDtypeStruct((SUBLANES, f), F32))(padded)


def _adamw_math(w, g, m, v):
    m = ADAM_B1 * m + (1.0 - ADAM_B1) * g
    v = ADAM_B2 * v + (1.0 - ADAM_B2) * (g * g)
    m_hat = m / (1.0 - ADAM_B1 ** ADAM_STEP)
    v_hat = v / (1.0 - ADAM_B2 ** ADAM_STEP)
    delta = -ADAM_LR * (m_hat / (jnp.sqrt(v_hat) + ADAM_EPS) + ADAM_WD * w)
    return delta, m, v


def _adamw(w, g, m, v, name):
    R, C = w.shape
    tr = _pick(R, (128, 64, 32, 16, 8))

    def body(w_ref, g_ref, m_ref, v_ref, d_ref, nm_ref, nv_ref):
        d, nm, nv = _adamw_math(w_ref[...], g_ref[...], m_ref[...], v_ref[...])
        d_ref[...] = d
        nm_ref[...] = nm
        nv_ref[...] = nv

    spec = pl.BlockSpec((tr, C), lambda i: (i, 0))
    return _pcall(
        body, name=name, grid=(R // tr,), in_specs=[spec] * 4, out_specs=[spec] * 3,
        out_shape=[jax.ShapeDtypeStruct((R, C), F32)] * 3, compiler_params=_params(("parallel",)),
    )(w, g, m, v)


def _adamw_halves(w, m, v, g_mine, g_recv, c, name, layer=0, prev=None):
    C = w.shape[1]
    rh = g_mine.shape[0]
    tr = _pick(rh, (128, 64, 32, 16, 8))
    nb = rh // tr
    r0 = layer * 2 * nb

    def body(c_ref, w_ref, m_ref, v_ref, gm_ref, gr_ref, *rest):
        g_ref, d_ref, nm_ref, nv_ref = rest[-4:]
        g = jnp.where(pl.program_id(0) == c_ref[0], gm_ref[...], gr_ref[...])
        d, nm, nv = _adamw_math(w_ref[...], g, m_ref[...], v_ref[...])
        g_ref[...] = g
        d_ref[...] = d
        nm_ref[...] = nm
        nv_ref[...] = nv

    full = pl.BlockSpec((tr, C), lambda h, i, cr: (r0 + h * nb + i, 0))
    mine = pl.BlockSpec((tr, C), lambda h, i, cr: (jnp.where(h == cr[0], i, 0), 0))
    recv = pl.BlockSpec((tr, C), lambda h, i, cr: (jnp.where(h == cr[0], 0, i), 0))
    in_specs = [full, full, full, mine, recv]
    args = [c, w, m, v, g_mine, g_recv]
    alias = {}
    if prev is not None:
        in_specs += [pl.BlockSpec(memory_space=pl.ANY)] * 4
        args += list(prev)
        alias = {6 + k: k for k in range(4)}
    return _pcall(
        body, name=name,
        grid_spec=pltpu.PrefetchScalarGridSpec(
            num_scalar_prefetch=1, grid=(2, nb), in_specs=in_specs, out_specs=[full] * 4),
        out_shape=[jax.ShapeDtypeStruct(w.shape, F32)] * 4, input_output_aliases=alias,
        compiler_params=_params(("parallel", "parallel")),
    )(*args)


def _lb_table_grad(p8, dlb, n):
    f = p8.shape[1]

    def body(p_ref, d_ref, o_ref):
        p = p_ref[...]
        d = d_ref[...]
        p0 = p[0:1, :]
        first = lax.broadcasted_iota(jnp.int32, p.shape, 0) == 0
        o_ref[...] = p * (jnp.where(first, d, 0.0) - d * p0)

    return _pcall(body, name="lb_table_grad", out_shape=jax.ShapeDtypeStruct((SUBLANES, f), F32))(p8, dlb)[:n]


def _place():
    x, y, c = lax.axis_index("x"), lax.axis_index("y"), lax.axis_index("c")
    chips = [(1 - x, y), (x, 1 - y), (1 - x, 1 - y)]
    return x, y, c, chips


HBM_SPEC = pl.BlockSpec(memory_space=pltpu.HBM)


def _gather_weights(big, small):
    nb, ns = len(big), len(small)
    n = nb + ns

    def body(*refs):
        ins, outs = refs[:n], refs[n:2 * n]
        send_sems, recv_sems, own_send, own_recv = refs[2 * n:]
        x, y, c, chips = _place()
        me = 2 * x + y
        sib = (x, y, 1 - c)
        own = [pltpu.make_async_remote_copy(
            src_ref=ins[t], dst_ref=outs[t].at[me], send_sem=own_send.at[t], recv_sem=own_recv.at[t],
            device_id=sib, device_id_type=MESH) for t in range(n)]
        for cp in own:
            cp.start()

        def half(t, h):
            rh = big[t].shape[0] // 2
            return pl.ds(pl.multiple_of(h * rh, rh), rh)

        sends = []
        for t in range(n):
            for j, chip in enumerate(chips):
                k = 6 * t + j
                if t < nb:
                    src, dst = ins[t].at[half(t, c)], outs[t].at[me, half(t, c)]
                else:
                    src, dst = ins[t], outs[t].at[me]
                sends.append(pltpu.make_async_remote_copy(
                    src_ref=src, dst_ref=dst, send_sem=send_sems.at[k], recv_sem=recv_sems.at[k],
                    device_id=(*chip, c), device_id_type=MESH))
        for cp in sends:
            cp.start()
        passed = []
        for t in range(n):
            for j, (cx, cy) in enumerate(chips):
                k = 6 * t + j
                s = 2 * cx + cy
                if t < nb:
                    landed = outs[t].at[s, half(t, c)]
                    pltpu.make_async_remote_copy(
                        src_ref=landed, dst_ref=landed, send_sem=send_sems.at[k], recv_sem=recv_sems.at[k],
                        device_id=sib, device_id_type=MESH).wait_recv()
                    fwd = pltpu.make_async_remote_copy(
                        src_ref=landed, dst_ref=landed, send_sem=send_sems.at[k + 3], recv_sem=recv_sems.at[k + 3],
                        device_id=sib, device_id_type=MESH)
                    fwd.start()
                    passed.append(fwd)
                else:
                    landed = outs[t].at[s]
                    pltpu.make_async_remote_copy(
                        src_ref=landed, dst_ref=landed, send_sem=send_sems.at[k], recv_sem=recv_sems.at[k],
                        device_id=sib, device_id_type=MESH).wait_recv()
        for t in range(nb):
            for j, (cx, cy) in enumerate(chips):
                k = 6 * t + j
                other = outs[t].at[2 * cx + cy, half(t, 1 - c)]
                pltpu.make_async_remote_copy(
                    src_ref=other, dst_ref=other, send_sem=send_sems.at[k + 3], recv_sem=recv_sems.at[k + 3],
                    device_id=sib, device_id_type=MESH).wait_recv()
        for cp in sends + passed:
            cp.wait_send()
        for cp in own:
            cp.wait()

    arrs = list(big) + list(small)
    return _pcall(
        body, name="gather_weights", in_specs=[HBM_SPEC] * n, out_specs=[HBM_SPEC] * n,
        out_shape=[jax.ShapeDtypeStruct((N_CHIPS,) + a.shape, a.dtype) for a in arrs],
        scratch_shapes=[pltpu.SemaphoreType.DMA((6 * n,)), pltpu.SemaphoreType.DMA((6 * n,)),
                        pltpu.SemaphoreType.DMA((n,)), pltpu.SemaphoreType.DMA((n,))],
    )(*arrs)


SEM_SPEC = pl.BlockSpec(memory_space=pltpu.SEMAPHORE)
DATAFLOW = pltpu.SideEffectType.DATAFLOW_SIDE_EFFECTING
COPIES_PER_SHARD = 4


def _shard_copies(ins, lands, send_sems, recv_sems, base=0):
    x, y, c, chips = _place()
    me = 2 * x + y
    cps = []
    for t in range(len(ins)):
        rh = ins[t].shape[0] // 2
        half = pl.ds(pl.multiple_of(c * rh, rh), rh)
        for j, chip in enumerate(chips):
            k = COPIES_PER_SHARD * (base + t) + j
            cps.append(pltpu.make_async_remote_copy(
                src_ref=ins[t].at[half], dst_ref=lands[t].at[me, half], send_sem=send_sems.at[k],
                recv_sem=recv_sems.at[k], device_id=(*chip, c), device_id_type=MESH))
        k = COPIES_PER_SHARD * (base + t) + 3
        cps.append(pltpu.make_async_remote_copy(
            src_ref=ins[t], dst_ref=lands[t].at[me], send_sem=send_sems.at[k], recv_sem=recv_sems.at[k],
            device_id=(x, y, 1 - c), device_id_type=MESH))
    return cps


def _gather_start(shards, thru, name):
    n = len(shards)
    nops = 2 * n + len(thru)

    def body(*refs):
        ins, lands = refs[:n], refs[n:2 * n]
        send_sems, recv_sems = refs[nops], refs[nops + 1]
        for cp in _shard_copies(ins, lands, send_sems, recv_sems):
            cp.start()

    lands = [pltpu.with_memory_space_constraint(lax.empty((N_CHIPS,) + s.shape, s.dtype), pltpu.HBM) for s in shards]
    ops = [pltpu.with_memory_space_constraint(s, pltpu.HBM) for s in shards] + lands + list(thru)
    nsem = COPIES_PER_SHARD * n
    res = _pcall(
        body, name=name, in_specs=[HBM_SPEC] * nops,
        out_specs=[SEM_SPEC, SEM_SPEC] + [HBM_SPEC] * nops,
        out_shape=[pltpu.SemaphoreType.DMA((nsem,)), pltpu.SemaphoreType.DMA((nsem,))]
        + [pltpu.HBM(o.shape, o.dtype) for o in ops],
        input_output_aliases={i: 2 + i for i in range(nops)},
        compiler_params=pltpu.CompilerParams(has_side_effects=DATAFLOW),
    )(*ops)
    return res[0], res[1], res[2:2 + n], res[2 + n:2 + 2 * n], list(res[2 + 2 * n:])


def _gather_wait(send_sems, recv_sems, shards, lands, after, name, base=0, which=None):
    n = len(shards)

    def body(*refs):
        ins, lnd = refs[:n], refs[n:2 * n]
        ssem, rsem = refs[2 * n], refs[2 * n + 1]
        for k, cp in enumerate(_shard_copies(ins, lnd, ssem, rsem, base)):
            if which is None or k % COPIES_PER_SHARD in which:
                cp.wait_send()
                cp.wait_recv()

    res = _pcall(
        body, name=name,
        in_specs=[HBM_SPEC] * (2 * n) + [SEM_SPEC, SEM_SPEC, pl.BlockSpec(memory_space=pl.ANY)],
        out_specs=[HBM_SPEC] * (2 * n),
        out_shape=[pltpu.HBM(o.shape, o.dtype) for o in list(shards) + list(lands)],
        input_output_aliases={i: i for i in range(2 * n)},
        compiler_params=pltpu.CompilerParams(has_side_effects=DATAFLOW),
    )(*shards, *lands, send_sems, recv_sems, after)
    return res[:n], res[n:]


SIBLING_PAIR = 1


def _sibling_handshake():
    x, y, c, _ = _place()
    barrier = pltpu.get_barrier_semaphore()
    pl.semaphore_signal(barrier, inc=1, device_id=(x, y, 1 - c), device_id_type=MESH)
    pl.semaphore_wait(barrier, 1)


def _forward_copies(land, send_sems, recv_sems, which=(0, 1, 2)):
    x, y, c, chips = _place()
    rh = land.shape[1] // 2
    return [pltpu.make_async_remote_copy(
        src_ref=land.at[2 * cx + cy, pl.ds(pl.multiple_of(c * rh, rh), rh)],
        dst_ref=land.at[2 * cx + cy, pl.ds(pl.multiple_of(c * rh, rh), rh)],
        send_sem=send_sems.at[j], recv_sem=recv_sems.at[j], device_id=(x, y, 1 - c), device_id_type=MESH)
        for j, (cx, cy) in enumerate(chips) if j in which]


def _forward_start(land, thru, name, which=(0, 1, 2)):
    def body(land_ref, thru_ref, send_sems, recv_sems, out_ref, thru_out):
        _sibling_handshake()
        for cp in _forward_copies(land_ref, send_sems, recv_sems, which):
            cp.start()

    return _pcall(
        body, name=name, in_specs=[HBM_SPEC, HBM_SPEC], out_specs=[SEM_SPEC, SEM_SPEC, HBM_SPEC, HBM_SPEC],
        out_shape=[pltpu.SemaphoreType.DMA((3,)), pltpu.SemaphoreType.DMA((3,)), pltpu.HBM(land.shape, land.dtype),
                   pltpu.HBM(thru.shape, thru.dtype)],
        input_output_aliases={0: 2, 1: 3},
        compiler_params=pltpu.CompilerParams(has_side_effects=DATAFLOW, collective_id=SIBLING_PAIR),
    )(land, thru)


def _forward_wait(send_sems, recv_sems, land, after, name, which=(0, 1, 2)):
    def body(land_ref, ssem, rsem, after_ref, out_ref):
        for cp in _forward_copies(land_ref, ssem, rsem, which):
            cp.wait_send()
            cp.wait_recv()

    return _pcall(
        body, name=name, in_specs=[HBM_SPEC, SEM_SPEC, SEM_SPEC, pl.BlockSpec(memory_space=pl.ANY)],
        out_specs=HBM_SPEC, out_shape=pltpu.HBM(land.shape, land.dtype), input_output_aliases={0: 0},
        compiler_params=pltpu.CompilerParams(has_side_effects=DATAFLOW),
    )(land, send_sems, recv_sems, after)


def _sibling_copies(ins, lands, send_sems, recv_sems, other_half):
    x, y, c, _ = _place()
    return [pltpu.make_async_remote_copy(
        src_ref=ins[t].at[:, 1 - c] if other_half else ins[t], dst_ref=lands[t], send_sem=send_sems.at[t],
        recv_sem=recv_sems.at[t], device_id=(x, y, 1 - c), device_id_type=MESH) for t in range(len(ins))]


def _sibling_start(srcs, other_half, thru, name):
    n = len(srcs)
    nthru = 0 if thru is None else 1

    def body(*refs):
        ins, lands = refs[:n], refs[n:2 * n]
        send_sems, recv_sems = refs[2 * n + nthru], refs[2 * n + nthru + 1]
        _sibling_handshake()
        for cp in _sibling_copies(ins, lands, send_sems, recv_sems, other_half):
            cp.start()

    shapes = [(s.shape[0],) + s.shape[2:] if other_half else s.shape for s in srcs]
    lands = [pltpu.with_memory_space_constraint(lax.empty(sh, s.dtype), pltpu.HBM) for sh, s in zip(shapes, srcs)]
    ops = [pltpu.with_memory_space_constraint(s, pltpu.HBM) for s in srcs] + lands + ([] if thru is None else [thru])
    res = _pcall(
        body, name=name, in_specs=[HBM_SPEC] * len(ops),
        out_specs=[SEM_SPEC, SEM_SPEC] + [HBM_SPEC] * len(ops),
        out_shape=[pltpu.SemaphoreType.DMA((n,)), pltpu.SemaphoreType.DMA((n,))]
        + [pltpu.HBM(o.shape, o.dtype) for o in ops],
        input_output_aliases={i: 2 + i for i in range(len(ops))},
        compiler_params=pltpu.CompilerParams(has_side_effects=DATAFLOW, collective_id=SIBLING_PAIR),
    )(*ops)
    return res[0], res[1], res[2:2 + n], res[2 + n:2 + 2 * n], (None if thru is None else res[2 + 2 * n])


def _sibling_wait(send_sems, recv_sems, srcs, lands, other_half, after, name):
    n = len(srcs)

    def body(*refs):
        ins, lnd = refs[:n], refs[n:2 * n]
        ssem, rsem = refs[2 * n], refs[2 * n + 1]
        for cp in _sibling_copies(ins, lnd, ssem, rsem, other_half):
            cp.wait_send()
            cp.wait_recv()

    res = _pcall(
        body, name=name,
        in_specs=[HBM_SPEC] * (2 * n) + [SEM_SPEC, SEM_SPEC, pl.BlockSpec(memory_space=pl.ANY)],
        out_specs=[HBM_SPEC] * (2 * n),
        out_shape=[pltpu.HBM(o.shape, o.dtype) for o in list(srcs) + list(lands)],
        input_output_aliases={i: i for i in range(2 * n)},
        compiler_params=pltpu.CompilerParams(has_side_effects=DATAFLOW),
    )(*srcs, *lands, send_sems, recv_sems, after)
    return res[:n], res[n:]


def _chip_copies(ins, lands, send_sems, recv_sems):
    x, y, c, chips = _place()
    cps = []
    for t in range(len(ins)):
        for j, (cx, cy) in enumerate(chips):
            cps.append(pltpu.make_async_remote_copy(
                src_ref=ins[t].at[2 * cx + cy], dst_ref=lands[t].at[j],
                send_sem=send_sems.at[3 * t + j], recv_sem=recv_sems.at[3 * t + j],
                device_id=(cx, cy, c), device_id_type=MESH))
    return cps


def _chip_start(parts, thru, name):
    n = len(parts)

    def body(*refs):
        ins, lands = refs[:n], refs[n:2 * n]
        send_sems, recv_sems = refs[2 * n + 1], refs[2 * n + 2]
        for cp in _chip_copies(ins, lands, send_sems, recv_sems):
            cp.start()

    lands = [pltpu.with_memory_space_constraint(lax.empty((3,) + p.shape[1:], p.dtype), pltpu.HBM) for p in parts]
    ops = [pltpu.with_memory_space_constraint(p, pltpu.HBM) for p in parts] + lands + [thru]
    res = _pcall(
        body, name=name, in_specs=[HBM_SPEC] * (2 * n + 1),
        out_specs=[SEM_SPEC, SEM_SPEC] + [HBM_SPEC] * (2 * n + 1),
        out_shape=[pltpu.SemaphoreType.DMA((3 * n,)), pltpu.SemaphoreType.DMA((3 * n,))]
        + [pltpu.HBM(o.shape, o.dtype) for o in ops],
        input_output_aliases={i: 2 + i for i in range(2 * n + 1)},
        compiler_params=pltpu.CompilerParams(has_side_effects=DATAFLOW),
    )(*ops)
    return res[0], res[1], res[2:2 + n], res[2 + n:2 + 2 * n], res[2 + 2 * n]


def _chip_wait(send_sems, recv_sems, parts, lands, after, name):
    n = len(parts)

    def body(*refs):
        ins, lnd = refs[:n], refs[n:2 * n]
        ssem, rsem = refs[2 * n], refs[2 * n + 1]
        for cp in _chip_copies(ins, lnd, ssem, rsem):
            cp.wait_send()
            cp.wait_recv()

    res = _pcall(
        body, name=name,
        in_specs=[HBM_SPEC] * (2 * n) + [SEM_SPEC, SEM_SPEC, pl.BlockSpec(memory_space=pl.ANY)],
        out_specs=[HBM_SPEC] * (2 * n),
        out_shape=[pltpu.HBM(o.shape, o.dtype) for o in list(parts) + list(lands)],
        input_output_aliases={i: i for i in range(2 * n)},
        compiler_params=pltpu.CompilerParams(has_side_effects=DATAFLOW),
    )(*parts, *lands, send_sems, recv_sems, after)
    return res[:n], res[n:]


def _add_pair(grad, recv, c, name):
    s, _, rh, cc = grad.shape
    tr = _pick(rh, (256, 128, 64, 32, 16))

    def body(c_ref, g_ref, r_ref, o_ref):
        o_ref[...] = (g_ref[...].astype(F32) + r_ref[...].astype(F32)).astype(o_ref.dtype)

    return _pcall(
        body, name=name,
        grid_spec=pltpu.PrefetchScalarGridSpec(
            num_scalar_prefetch=1, grid=(s, rh // tr),
            in_specs=[pl.BlockSpec((None, None, tr, cc), lambda a, i, cr: (a, cr[0], i, 0)),
                      pl.BlockSpec((None, tr, cc), lambda a, i, cr: (a, i, 0))],
            out_specs=pl.BlockSpec((None, tr, cc), lambda a, i, cr: (a, i, 0))),
        out_shape=jax.ShapeDtypeStruct((s, rh, cc), BF16),
        compiler_params=_params(("parallel", "parallel")),
    )(c, grad, recv)


def _add_chips(part, recv, me, name):
    _, rh, cc = part.shape
    tr = _pick(rh, (256, 128, 64, 32, 16))

    def body(m_ref, p_ref, r_ref, o_ref):
        o_ref[...] = ((p_ref[...].astype(F32) + r_ref[0].astype(F32)) + r_ref[1].astype(F32)) + r_ref[2].astype(F32)

    return _pcall(
        body, name=name,
        grid_spec=pltpu.PrefetchScalarGridSpec(
            num_scalar_prefetch=1, grid=(rh // tr,),
            in_specs=[pl.BlockSpec((None, tr, cc), lambda i, mr: (mr[0], i, 0)),
                      pl.BlockSpec((3, tr, cc), lambda i, mr: (0, i, 0))],
            out_specs=pl.BlockSpec((tr, cc), lambda i, mr: (i, 0))),
        out_shape=jax.ShapeDtypeStruct((rh, cc), F32),
        compiler_params=_params(("parallel",)),
    )(me, part, recv)


def _all_sum(vec):
    rows = vec.shape[0]

    def body(v_ref, o_ref, buf, send_sems, recv_sems):
        x, y, c, _ = _place()
        me = 4 * x + 2 * y + c
        buf[me] = v_ref[...]
        cps = []
        for r in range(1, 8):
            fx, fy, fc = (r >> 2) & 1, (r >> 1) & 1, r & 1
            peer = (x ^ fx, y ^ fy, c ^ fc)
            cps.append(pltpu.make_async_remote_copy(
                src_ref=v_ref, dst_ref=buf.at[me], send_sem=send_sems.at[r - 1], recv_sem=recv_sems.at[r - 1],
                device_id=peer, device_id_type=MESH))
        for cp in cps:
            cp.start()
        for r in range(1, 8):
            src = me ^ r
            pltpu.make_async_remote_copy(
                src_ref=v_ref, dst_ref=buf.at[src], send_sem=send_sems.at[r - 1], recv_sem=recv_sems.at[r - 1],
                device_id=(x, y, c), device_id_type=MESH).wait_recv()
        for cp in cps:
            cp.wait_send()
        acc = buf[0]
        for d in range(1, 8):
            acc = acc + buf[d]
        o_ref[...] = acc

    return _pcall(
        body, name="all_sum_small",
        in_specs=[pl.BlockSpec(memory_space=pltpu.VMEM)], out_specs=pl.BlockSpec(memory_space=pltpu.VMEM),
        out_shape=jax.ShapeDtypeStruct((rows, LANES), F32),
        scratch_shapes=[pltpu.VMEM((8, rows, LANES), F32), pltpu.SemaphoreType.DMA((7,)), pltpu.SemaphoreType.DMA((7,))],
    )(vec)


def _pack(parts):
    flat = jnp.concatenate([p.reshape(-1) for p in parts])
    tile = SUBLANES * LANES
    pad = (-flat.shape[0]) % tile
    return jnp.pad(flat, (0, pad)).reshape(-1, LANES)


def _unpack(vec, shapes):
    flat = vec.reshape(-1)
    out, p = [], 0
    for s in shapes:
        n = 1
        for d in s:
            n *= d
        out.append(flat[p:p + n].reshape(s))
        p += n
    return out


def _local_step(x, tgt, norm_mix, norm_ffn, lb8, out_norm, final_norm, sc_conv, ffn_conv, first, arrive, reduce_start,
                reduce_finish):
    T, D = x.shape
    F2 = ffn_conv.shape[-1]
    FF = F2 // 2
    tm = _pick(T, (1024, 512, 256, 128))
    wide = (1536, 1408, 1024, 768, 512, 384, 256, 128)
    cw_h, cw_s, cw_u = 4 * D // N_CHIPS, 3 * D // N_CHIPS, F2 // N_CHIPS
    kp = FF // N_CHIPS
    tk_ff = kp if kp % LANES == 0 else LANES
    tn_d = _pick(D, (1024, 512, 256, 128))
    tk_w = _pick(D, (512, 256, 128))
    tn_h = _pick(cw_h, (1024, 512, 256, 128))
    tn_s = _pick(D // N_CHIPS, (512, 256, 128))
    tn_u = _pick(cw_u, wide)
    lb = lb8[0:1]
    wm_sq = _wmap_col(D, tn_d, 0)
    wm_sq1 = _wmap_col(D, D, 0)
    seg1 = lambda a: a.reshape((1,) + a.shape)

    def mix_in(h, w):
        return _row_call(_rms_fwd_fn, [(h, 0, D)], [w], [(D, BF16)], 0, "rms_fwd")[0]

    def rms_bwd(h, dxn, dh, w):
        return _row_call(_rms_bwd_fn, [(h, 0, D), (dxn, 0, D), (dh, 0, D)], [w], [(D, F32), (D, BF16)], 1, "rms_bwd")

    def ffn_fwd(h, i, fetch_up, behind_down=()):
        xn = mix_in(h, norm_ffn[i:i + 1])
        tn = _pick(cw_u, wide)
        fetch_down, xn = arrive("ffn_w_down%d" % i, xn)
        w_up = fetch_up(xn)
        up = _mm_nn(xn, w_up, _wmap_col(cw_u, tn, 0), D, F2, tm, D, tn, "ffn_up")
        nb = FF // LANES
        a = _col_call(_glu_fwd_fn, [(up, 0), (up, nb)], [(ffn_conv[i], 0), (ffn_conv[i], nb)], [(1, FF, BF16)], 0,
                      "glu_fwd", before=True, after=False)[0][0]
        later = []
        for name in behind_down:
            fetch, a = arrive(name, a)
            later.append(fetch)
        w_down = fetch_down(a)
        h2 = _mm_nn(a, w_down, _wmap_row(kp, tk_ff, 0), FF, D, tm, tk_ff, tn_d, "ffn_down", res=h, per_step=2)
        return h2, (xn, up, a), w_up, w_down, later

    def ffn_bwd(dh, dh16, h, saved, i, w_up, w_down):
        xn, up, a = saved
        g_down = _mm_tn(a, seg1(dh16), (N_CHIPS, 1, kp, D), _wmap_row(kp, tk_ff, 0), FF, D, tk_ff, tn_d,
                        "ffn_down_dw", tm=_pick(T, (2048, 1024, 512, 256, 128)))
        dh16 = reduce_start(("ffn_w_down", i), g_down, dh16)
        da = _mm_nt(seg1(dh16), w_down, _wmap_row(kp, tk_ff, 0), FF, D, tm, tk_ff, D, "ffn_down_dx")
        nb = FF // LANES
        dgv, cg, cv = _col_call(_glu_bwd_fn, [(up, 0), (up, nb), (da, 0)], [(ffn_conv[i], 0), (ffn_conv[i], nb)],
                                [(2, FF, BF16)], 2, "glu_bwd", before=True, after=True)
        g_up = _mm_tn(xn, dgv, (N_CHIPS, 1, D, cw_u), _wmap_col(cw_u, tn_u, 0), D, F2, tk_w, tn_u, "ffn_up_dw")
        dgv = reduce_start(("ffn_w_up", i), g_up, dgv)
        dxn = _mm_nt(dgv, w_up, _wmap_col(cw_u, tn_u, 0), D, F2, _pick(T, (512, 256, 128)), D, tn_u, "ffn_up_dx",
                     per_step=2)
        dh2, dh2_16, dnw = rms_bwd(h, dxn, dh, norm_ffn[i:i + 1])
        return dh2, reduce_finish(dh2_16), dnw, jnp.concatenate([cg[:3], cv[:3]], axis=1)

    h0 = x
    xn0 = mix_in(h0, norm_mix[0:1])
    proj, w_hin = first(xn0, lambda w, s, prev: _mm_nn_shard(xn0, w, s, tm, tn_h, "hgrn_in", prev))
    o, states = _hgrn_fwd(proj, lb, D)
    fetch_hout, o = arrive("hgrn_w_out", o)
    on = _row_call(_onorm_fwd_fn, [(o, 0, D), (proj, 3, D)], [out_norm], [(D, BF16)], 0, "onorm_fwd")[0]
    fetch_up0, on = arrive("ffn_w_up0", on)
    w_hout1 = fetch_hout(on).reshape(1, D, D)
    h1 = _mm_nn(on, w_hout1, wm_sq, D, D, tm, D, tn_d, "hgrn_out", res=h0)
    h2, ffn0, w_up0, w_down0, (fetch_sin, fetch_sout) = ffn_fwd(h1, 0, fetch_up0, ("sc_w_in", "sc_w_out"))
    xn1 = mix_in(h2, norm_mix[1:2])
    w_sin = fetch_sin(xn1)
    tn_si = _pick(cw_s, wide)
    sproj = _mm_nn(xn1, w_sin, _wmap_col(cw_s, tn_si, 0), D, 3 * D, tm, D, tn_si, "sc_in")
    fetch_up1, sproj = arrive("ffn_w_up1", sproj)
    nd = D // LANES
    ysc = _col_call(_sc_fwd_fn, [(sproj, 0), (sproj, nd), (sproj, 2 * nd)], [(sc_conv, 0)], [(1, D, BF16)], 0,
                    "sc_fwd", before=True, after=False)[0][0]
    w_sout1 = fetch_sout(ysc).reshape(1, D, D)
    h3 = _mm_nn(ysc, w_sout1, wm_sq, D, D, tm, D, tn_d, "sc_out", res=h2)
    h4, ffn1, w_up1, w_down1, _ = ffn_fwd(h3, 1, fetch_up1)

    dh, dh16, esq, dfinal = _row_call(_final_fn, [(h4, 0, D), (tgt, 0, D)], [final_norm], [(D, F32), (D, BF16)], 2,
                                      "final_loss")
    loss = 0.5 / D * jnp.sum(esq)
    dh, dh16, dnf1, dconv1 = ffn_bwd(dh, dh16, h3, ffn1, 1, w_up1, w_down1)
    g_sout = _mm_tn(ysc, seg1(dh16), (1, 1, D, D), wm_sq, D, D, tk_w, tn_d, "sc_out_dw")
    dh16 = reduce_start(("sc_w_out", 0), g_sout, dh16)
    dy = _mm_nt(seg1(dh16), w_sout1, wm_sq1, D, D, tm, D, D, "sc_out_dx")
    dsp, dscc = _col_call(_sc_bwd_fn, [(sproj, 0), (sproj, nd), (sproj, 2 * nd), (dy, 0)], [(sc_conv, 0)],
                          [(3, D, BF16)], 1, "sc_bwd", before=True, after=True)
    g_sin = _mm_tn(xn1, dsp, (N_CHIPS, 1, D, cw_s), _wmap_col(cw_s, tn_s, 0), D, 3 * D, tk_w, tn_s, "sc_in_dw")
    dsp = reduce_start(("sc_w_in", 0), g_sin, dsp)
    dxn = _mm_nt(dsp, w_sin, _wmap_col(cw_s, tn_s, 0), D, 3 * D, tm, D, tn_s, "sc_in_dx", per_step=3)
    dh, dh16, dnm1 = rms_bwd(h2, dxn, dh, norm_mix[1:2])
    dh16 = reduce_finish(dh16)
    dh, dh16, dnf0, dconv0 = ffn_bwd(dh, dh16, h1, ffn0, 0, w_up0, w_down0)
    g_hout = _mm_tn(on, seg1(dh16), (1, 1, D, D), wm_sq, D, D, tk_w, tn_d, "hgrn_out_dw")
    dh16 = reduce_start(("hgrn_w_out", 0), g_hout, dh16)
    don = _mm_nt(seg1(dh16), w_hout1, wm_sq1, D, D, tm, D, D, "hgrn_out_dx")
    do, dgate, dgain = _row_call(_onorm_bwd_fn, [(o, 0, D), (proj, 3, D), (don, 0, D)], [out_norm],
                                 [(D, F32), (D, BF16)], 1, "onorm_bwd")
    dproj, dlb = _hgrn_bwd(proj, lb, states, do, dgate, D)
    g_hin = _mm_tn(xn0, dproj, (N_CHIPS, 1, D, cw_h), _wmap_col(cw_h, tn_h, 0), D, 4 * D, tk_w, tn_h, "hgrn_in_dw")
    dproj = reduce_start(("hgrn_w_in", 0), g_hin, dproj)
    dxn = _mm_nt(dproj, w_hin, _wmap_col(cw_h, tn_h, 0), D, 4 * D, tm, D, tn_h, "hgrn_in_dx", per_step=2)
    grad_x, _, dnm0 = rms_bwd(h0, dxn, dh, norm_mix[0:1])

    small = dict(
        loss=loss,
        norm_mix=jnp.stack([jnp.sum(dnm0, axis=0), jnp.sum(dnm1, axis=0)]),
        norm_ffn=jnp.stack([jnp.sum(dnf0, axis=0), jnp.sum(dnf1, axis=0)]),
        lb=dlb[0:1],
        out_norm=jnp.sum(dgain, axis=0)[None],
        final_norm=jnp.sum(dfinal, axis=0),
        sc_conv=dscc[:3],
        ffn_conv=jnp.stack([dconv0, dconv1]),
    )
    return grad_x, small


def kernel(x, norm_mix, norm_ffn, hgrn_w_in, hgrn_lb_table, hgrn_out_norm, hgrn_w_out, sc_w_in, sc_conv, sc_w_out, ffn_w_up, ffn_conv, ffn_w_down, final_norm, loss_target, m_norm_mix, m_norm_ffn, m_hgrn_w_in, m_hgrn_lb_table, m_hgrn_out_norm, m_hgrn_w_out, m_sc_w_in, m_sc_conv, m_sc_w_out, m_ffn_w_up, m_ffn_conv, m_ffn_w_down, m_final_norm, v_norm_mix, v_norm_ffn, v_hgrn_w_in, v_hgrn_lb_table, v_hgrn_out_norm, v_hgrn_w_out, v_sc_w_in, v_sc_conv, v_sc_w_out, v_ffn_w_up, v_ffn_conv, v_ffn_w_down, v_final_norm):
    D = x.shape[-1]
    xi, yi, ci = lax.axis_index("x"), lax.axis_index("y"), lax.axis_index("c")
    me_chip = (2 * xi + yi).astype(jnp.int32).reshape(1)
    me_core = ci.astype(jnp.int32).reshape(1)

    big_names = ["hgrn_w_in", "hgrn_w_out", "sc_w_in", "sc_w_out", "ffn_w_up", "ffn_w_down"]
    big_w = dict(hgrn_w_in=hgrn_w_in, hgrn_w_out=hgrn_w_out, sc_w_in=sc_w_in, sc_w_out=sc_w_out,
                 ffn_w_up=ffn_w_up, ffn_w_down=ffn_w_down)
    big_m = dict(hgrn_w_in=m_hgrn_w_in, hgrn_w_out=m_hgrn_w_out, sc_w_in=m_sc_w_in, sc_w_out=m_sc_w_out,
                 ffn_w_up=m_ffn_w_up, ffn_w_down=m_ffn_w_down)
    big_v = dict(hgrn_w_in=v_hgrn_w_in, hgrn_w_out=v_hgrn_w_out, sc_w_in=v_sc_w_in, sc_w_out=v_sc_w_out,
                 ffn_w_up=v_ffn_w_up, ffn_w_down=v_ffn_w_down)
    flat2 = lambda a: a.reshape(-1, a.shape[-1])

    sh = lambda a: a.reshape(-1, a.shape[-1]).astype(BF16)
    conv_shards = [flat2(sc_conv), flat2(ffn_conv)]
    in_order_of_use = [("hgrn_w_in", sh(hgrn_w_in)), ("hgrn_w_out", sh(hgrn_w_out)), ("ffn_w_up0", sh(ffn_w_up[0])),
                       ("ffn_w_down0", sh(ffn_w_down[0])), ("sc_w_in", sh(sc_w_in)), ("sc_w_out", sh(sc_w_out)),
                       ("ffn_w_up1", sh(ffn_w_up[1])), ("ffn_w_down1", sh(ffn_w_down[1]))]
    scc4, fcc4 = _gather_weights([], [flat2(sc_conv), flat2(ffn_conv)])
    names = [n for n, _ in in_order_of_use]
    ss, rs, src, land, (scc4, norm_mix) = _gather_start([s for _, s in in_order_of_use], [scc4, norm_mix], "gather_start")
    travelling = {n: (s, l) for n, s, l in zip(names, src, land)}

    def landed(name, after, call, which=None):
        (s,), (l,) = _gather_wait(ss, rs, [travelling[name][0]], [travelling[name][1]], after, call,
                                  base=names.index(name), which=which)
        travelling[name] = (s, l)
        return l

    scc = jnp.moveaxis(scc4, 0, 1).reshape(3, D)
    f2 = ffn_conv.shape[-1] * N_CHIPS
    fcc = jnp.moveaxis(fcc4.reshape(N_CHIPS, 2, 3, -1), 0, 2).reshape(2, 3, f2)

    def first(after, matmul):
        w = landed("hgrn_w_in", after, "gather_wait_0_own", which=(3,))
        proj = matmul(w, me_chip, None)
        others = [2 * (1 - xi) + yi, 2 * xi + (1 - yi), 2 * (1 - xi) + (1 - yi)]
        for j, s in enumerate(others):
            w = landed("hgrn_w_in", proj, "gather_wait_0_%d" % j, which=(j,))
            fs, fr, w, proj = _forward_start(w, proj, "gather_forward_start_0_%d" % j, which=(j,))
            w = _forward_wait(fs, fr, w, proj, "gather_forward_wait_0_%d" % j, which=(j,))
            travelling["hgrn_w_in"] = (travelling["hgrn_w_in"][0], w)
            proj = matmul(w, s.astype(jnp.int32).reshape(1), proj)
        return proj, w

    def arrive(name, after):
        fs, fr, w, after = _forward_start(landed(name, after, "gather_wait_" + name), after, "gather_forward_start_" + name)
        return functools.partial(_forward_wait, fs, fr, w, name="gather_forward_wait_" + name), after

    pending = []
    started = []

    def reduce_start(slot, grad, thru):
        t = sum(len(b[0]) for b in pending) + len(started)
        halves = grad.reshape(N_CHIPS, 2, -1, grad.shape[-1])
        ss, rs, src, land, thru = _sibling_start([halves], True, thru, "grad_pair_start_%d" % t)
        started.append((slot, t, ss, rs, src, land))
        return thru

    def reduce_finish(thru):
        k = len(pending)
        pair = []
        for slot, t, ss, rs, src, land in started:
            src, recv = _sibling_wait(ss, rs, src, land, True, thru, "grad_pair_wait_%d" % t)
            pair.append(_add_pair(src[0], recv[0], me_core, "grad_add_pair"))
        ss, rs, pair, land, thru = _chip_start(pair, thru, "grad_chip_start_%d" % k)
        pending.append(([s[0] for s in started], ss, rs, pair, land))
        started.clear()
        return thru

    lb8 = _lb_softmax(hgrn_lb_table)
    grad_x, small = _local_step(
        x[0], loss_target[0], norm_mix, norm_ffn, lb8, hgrn_out_norm, final_norm[None], scc, fcc, first, arrive,
        reduce_start, reduce_finish)

    small_names = ["loss", "norm_mix", "norm_ffn", "lb", "out_norm", "final_norm", "sc_conv", "ffn_conv"]
    parts = [small[n].astype(F32) for n in small_names]
    shapes = [p.shape for p in parts]
    tot = dict(zip(small_names, _unpack(reduce_finish(_all_sum(_pack(parts))), shapes)))
    loss = tot["loss"].reshape(())
    g_lb_table = _lb_table_grad(lb8, tot["lb"], hgrn_lb_table.shape[0])
    cw = sc_conv.shape[-1]
    g_sc_conv = lax.dynamic_slice_in_dim(tot["sc_conv"], me_chip[0] * cw, cw, axis=1)[None]
    cf = ffn_conv.shape[-1]
    g_ffn_conv = lax.dynamic_slice_in_dim(tot["ffn_conv"], me_chip[0] * cf, cf, axis=2)
    g_small = dict(norm_mix=tot["norm_mix"], norm_ffn=tot["norm_ffn"], hgrn_lb_table=g_lb_table,
                   hgrn_out_norm=tot["out_norm"], sc_conv=g_sc_conv, ffn_conv=g_ffn_conv, final_norm=tot["final_norm"])
    w_small = dict(norm_mix=norm_mix, norm_ffn=norm_ffn, hgrn_lb_table=hgrn_lb_table, hgrn_out_norm=hgrn_out_norm,
                   sc_conv=sc_conv, ffn_conv=ffn_conv, final_norm=final_norm)
    m_small = dict(norm_mix=m_norm_mix, norm_ffn=m_norm_ffn, hgrn_lb_table=m_hgrn_lb_table, hgrn_out_norm=m_hgrn_out_norm,
                   sc_conv=m_sc_conv, ffn_conv=m_ffn_conv, final_norm=m_final_norm)
    v_small = dict(norm_mix=v_norm_mix, norm_ffn=v_norm_ffn, hgrn_lb_table=v_hgrn_lb_table, hgrn_out_norm=v_hgrn_out_norm,
                   sc_conv=v_sc_conv, ffn_conv=v_ffn_conv, final_norm=v_final_norm)
    sm_names = list(g_small)
    sm_shapes = [w_small[n].shape for n in sm_names]
    d_s, m_s, v_s = _adamw(_pack([w_small[n] for n in sm_names]), _pack([g_small[n] for n in sm_names]),
                           _pack([m_small[n] for n in sm_names]), _pack([v_small[n] for n in sm_names]), "adamw_small")
    out_g, out_d, out_m, out_v = dict(g_small), {}, {}, {}
    for n, d_, m_, v_ in zip(sm_names, _unpack(d_s, sm_shapes), _unpack(m_s, sm_shapes), _unpack(v_s, sm_shapes)):
        out_d[n], out_m[n], out_v[n] = d_, m_, v_

    done = {}
    after = grad_x

    def add_and_share(k, after):
        slots, ss, rs, pair, land = pending[k]
        pair, recv = _chip_wait(ss, rs, pair, land, after, "grad_chip_wait_%d" % k)
        mine = [_add_chips(p, r, me_chip, "grad_add_chips") for p, r in zip(pair, recv)]
        ss, rs, mine, land, _ = _sibling_start(mine, False, None, "grad_share_start_%d" % k)
        return slots, ss, rs, mine, land

    def update(k, share, after):
        slots, ss, rs, mine, land = share
        mine, theirs = _sibling_wait(ss, rs, mine, land, False, after, "grad_share_wait_%d" % k)
        for (n, layer), gm, gr in zip(slots, mine, theirs):
            done[n] = _adamw_halves(flat2(big_w[n]), flat2(big_m[n]), flat2(big_v[n]), gm, gr, me_core, "adamw_" + n,
                                    layer=layer, prev=done.get(n))
        return done[slots[-1][0]][0]

    shares = []
    for k in range(len(pending) - 1):
        shares.append(add_and_share(k, after))
        after = shares[-1][3][0]
    for k, share in enumerate(shares):
        after = update(k, share, after)
    last = len(pending) - 1
    share = add_and_share(last, after)
    update(last, share, share[3][0])
    for n in big_names:
        out_g[n], out_d[n], out_m[n], out_v[n] = (a.reshape(big_w[n].shape) for a in done[n])

    order = ["norm_mix", "norm_ffn", "hgrn_w_in", "hgrn_lb_table", "hgrn_out_norm", "hgrn_w_out", "sc_w_in", "sc_conv",
             "sc_w_out", "ffn_w_up", "ffn_conv", "ffn_w_down", "final_norm"]
    return (loss, grad_x[None], *[out_g[n] for n in order], *[out_d[n] for n in order],
            *[out_m[n] for n in order], *[out_v[n] for n in order])
```

```python
import functools

import jax
import jax.numpy as jnp
from jax import lax
from jax.experimental import pallas as pl
from jax.experimental.pallas import tpu as pltpu

F32 = jnp.float32
BF16 = jnp.bfloat16
MESH = pl.DeviceIdType.MESH

EPS = 1e-6
CHUNK = 64
HEAD = 128
N_CHIPS = 4
ADAM_LR, ADAM_B1, ADAM_B2, ADAM_EPS, ADAM_WD, ADAM_STEP = 0.001, 0.9, 0.999, 1e-08, 0.01, 10
VMEM_LIMIT = 56 * 1024 * 1024
SUBLANES = 8
LANES = 128


def _pcall(body, **kw):
    return pl.pallas_call(body, **kw)


def _params(sem, vmem=VMEM_LIMIT):
    return pltpu.CompilerParams(dimension_semantics=sem, vmem_limit_bytes=vmem)


def _pick(dim, prefs):
    for p in prefs:
        if p <= dim and dim % p == 0:
            return p
    return dim


def _sigmoid(x):
    return 1.0 / (1.0 + jnp.exp(-x))


def _wmap_col(cw, tn, r0):
    bps = cw // tn
    return lambda kb, nb: (nb // bps, r0 + kb, nb % bps)


def _wmap_row(kp, tk, r0):
    bps = kp // tk
    return lambda kb, nb: (kb // bps, r0 + kb % bps, nb)


def _mm_nn(a, w3, wmap, K, N, tm, tk, tn, name, res=None, per_step=1):
    M = a.shape[0]
    u = per_step
    nk = K // (tk * u)

    def body(*refs):
        r_ref = None if res is None else refs[2 * u]
        o_ref = refs[2 * u + (0 if res is None else 1)]
        p = None
        for r in range(u):
            d = jnp.dot(refs[r][...], refs[u + r][...], preferred_element_type=F32)
            p = d if p is None else p + d
        if nk == 1:
            o_ref[...] = p if res is None else p + r_ref[...]
            return
        acc = refs[-1]
        k = pl.program_id(2)

        @pl.when(k == 0)
        def _():
            acc[...] = p

        @pl.when(k > 0)
        def _():
            acc[...] += p

        @pl.when(k == nk - 1)
        def _():
            o_ref[...] = acc[...] if res is None else acc[...] + r_ref[...]

    if nk == 1:
        grid = (M // tm, N // tn)
        ix = lambda f: (lambda i, j: f(i, j, 0))
        sem = ("parallel", "parallel")
        scratch = []
    else:
        grid = (M // tm, N // tn, nk)
        ix = lambda f: f
        sem = ("parallel", "parallel", "arbitrary")
        scratch = [pltpu.VMEM((tm, tn), F32)]
    def a_spec(r):
        return pl.BlockSpec((tm, tk), ix(lambda i, j, k: (i, k * u + r)))

    def w_spec(r):
        return pl.BlockSpec((None, tk, tn), ix(lambda i, j, k: wmap(k * u + r, j)))

    in_specs = [a_spec(r) for r in range(u)] + [w_spec(r) for r in range(u)]
    args = [a] * u + [w3] * u
    if res is not None:
        in_specs.append(pl.BlockSpec((tm, tn), ix(lambda i, j, k: (i, j))))
        args.append(res)
    return _pcall(
        body, name=name, grid=grid, in_specs=in_specs,
        out_specs=pl.BlockSpec((tm, tn), ix(lambda i, j, k: (i, j))),
        out_shape=jax.ShapeDtypeStruct((M, N), F32), scratch_shapes=scratch, compiler_params=_params(sem),
    )(*args)


def _mm_nn_shard(a, w3, s, tm, tn, name, prev=None):
    M, K = a.shape
    S, _, cw = w3.shape
    bps = cw // tn

    def body(s_ref, a_ref, w_ref, *rest):
        o_ref = rest[-1]
        o_ref[...] = jnp.dot(a_ref[...], w_ref[...], preferred_element_type=F32)

    in_specs = [pl.BlockSpec((tm, K), lambda i, j, sr: (i, 0)),
                pl.BlockSpec((None, K, tn), lambda i, j, sr: (sr[0], 0, j))]
    args = [s, a, w3]
    alias = {}
    if prev is not None:
        in_specs.append(pl.BlockSpec(memory_space=pl.ANY))
        args.append(prev)
        alias = {3: 0}
    return _pcall(
        body, name=name,
        grid_spec=pltpu.PrefetchScalarGridSpec(
            num_scalar_prefetch=1, grid=(M // tm, bps), in_specs=in_specs,
            out_specs=pl.BlockSpec((tm, tn), lambda i, j, sr: (i, sr[0] * bps + j))),
        out_shape=jax.ShapeDtypeStruct((M, S * cw), F32), input_output_aliases=alias,
        compiler_params=_params(("parallel", "parallel")),
    )(*args)


def _mm_nt(dy3, w3, wmap, K, N, tm, tk, tn, name, per_step=1):
    M = dy3.shape[1]
    bps = dy3.shape[2] // tn
    u = per_step
    grid = (M // tm, K // tk, N // (tn * u))
    nn = grid[2]

    def body(*refs):
        o_ref = refs[-1]
        p = None
        for r in range(u):
            d = lax.dot_general(refs[r][...], refs[u + r][...], (((1,), (1,)), ((), ())), preferred_element_type=F32)
            p = d if p is None else p + d
        if nn == 1:
            o_ref[...] = p
            return
        n = pl.program_id(2)

        @pl.when(n == 0)
        def _():
            o_ref[...] = p

        @pl.when(n > 0)
        def _():
            o_ref[...] += p

    def dy_spec(r):
        return pl.BlockSpec((None, tm, tn), lambda i, j, n: ((n * u + r) // bps, i, (n * u + r) % bps))

    def w_spec(r):
        return pl.BlockSpec((None, tk, tn), lambda i, j, n: wmap(j, n * u + r))

    return _pcall(
        body, name=name, grid=grid,
        in_specs=[dy_spec(r) for r in range(u)] + [w_spec(r) for r in range(u)],
        out_specs=pl.BlockSpec((tm, tk), lambda i, j, n: (i, j)),
        out_shape=jax.ShapeDtypeStruct((M, K), F32),
        compiler_params=_params(("parallel", "parallel", "arbitrary")),
    )(*([dy3] * u), *([w3] * u))


def _mm_tn(x, dy3, shape4, wmap, K, N, tk, tn, name, tm=None):
    M = x.shape[0]
    tm = M if tm is None else tm
    nm = M // tm
    bps = dy3.shape[2] // tn

    def body(*refs):
        x_ref, dy_ref = refs[:2]
        p = lax.dot_general(x_ref[...], dy_ref[...], (((0,), (0,)), ((), ())), preferred_element_type=F32)
        if nm == 1:
            o_ref = refs[-1]
            o_ref[...] = p.astype(o_ref.dtype)
            return
        o_ref, acc = refs[-2:]
        m = pl.program_id(2)

        @pl.when(m == 0)
        def _():
            acc[...] = p

        @pl.when(m > 0)
        def _():
            acc[...] += p

        @pl.when(m == nm - 1)
        def _():
            o_ref[...] = acc[...].astype(o_ref.dtype)

    def omap(i, j, m):
        s, rb, cb = wmap(i, j)
        return (s, 0, rb, cb)

    return _pcall(
        body, name=name, grid=(K // tk, N // tn, nm),
        in_specs=[pl.BlockSpec((tm, tk), lambda i, j, m: (m, i)),
                  pl.BlockSpec((None, tm, tn), lambda i, j, m: (j // bps, m, j % bps))],
        out_specs=pl.BlockSpec((None, None, tk, tn), omap),
        out_shape=jax.ShapeDtypeStruct(shape4, BF16),
        scratch_shapes=[] if nm == 1 else [pltpu.VMEM((tk, tn), F32)],
        compiler_params=_params(("parallel", "parallel", "arbitrary")),
    )(x, dy3)


def _row_call(fn, rows, vecs, outs, n_acc, name, t_rows=256, sub=16, per_trip=4):
    T = rows[0][0].shape[0]
    t_rows = min(t_rows, T)
    nsub = t_rows // sub
    n_r, n_v, n_o = len(rows), len(vecs), len(outs)
    width = rows[0][2]

    def body(*refs):
        r_refs = refs[:n_r]
        v_refs = refs[n_r:n_r + n_v]
        o_refs = refs[n_r + n_v:n_r + n_v + n_o]
        a_refs = refs[n_r + n_v + n_o:]

        @pl.when(pl.program_id(0) == 0)
        def _():
            for a in a_refs:
                a[...] = jnp.zeros_like(a)

        vv = [v[...] for v in v_refs]

        def step(i, carry):
            done = []
            for u in range(per_trip):
                sl = pl.ds(pl.multiple_of((i * per_trip + u) * sub, sub), sub)
                done.append((sl,) + tuple(fn([r[sl, :] for r in r_refs], vv)))
            for sl, o_vals, a_vals in done:
                for o, val in zip(o_refs, o_vals):
                    o[sl, :] = val.astype(o.dtype)
            for a_i, a in enumerate(a_refs):
                tot = None
                for _, _, a_vals in done:
                    part = a_vals[a_i].reshape(sub // SUBLANES, SUBLANES, a_vals[a_i].shape[-1]).sum(axis=0)
                    tot = part if tot is None else tot + part
                a[...] += tot
            return carry

        lax.fori_loop(0, nsub // per_trip, step, 0)

    in_specs = [pl.BlockSpec((t_rows, w), functools.partial(lambda i, cb: (i, cb), cb=cb)) for _, cb, w in rows]
    in_specs += [pl.BlockSpec(v.shape, lambda i: (0, 0)) for v in vecs]
    out_specs = [pl.BlockSpec((t_rows, w), lambda i: (i, 0)) for w, _ in outs]
    out_specs += [pl.BlockSpec((SUBLANES, width), lambda i: (0, 0)) for _ in range(n_acc)]
    out_shape = [jax.ShapeDtypeStruct((T, w), dt) for w, dt in outs]
    out_shape += [jax.ShapeDtypeStruct((SUBLANES, width), F32) for _ in range(n_acc)]
    return _pcall(
        body, name=name, grid=(T // t_rows,), in_specs=in_specs, out_specs=out_specs, out_shape=out_shape,
        compiler_params=_params(("arbitrary",)),
    )(*[r[0] for r in rows], *vecs)


def _rms_fwd_fn(rv, vv):
    h, = rv
    w, = vv
    r = lax.rsqrt(jnp.mean(h * h, axis=-1, keepdims=True) + EPS)
    return [h * r * w], []


def _rms_bwd_fn(rv, vv):
    h, dxn, dh_in = rv
    w, = vv
    d = h.shape[-1]
    r = lax.rsqrt(jnp.mean(h * h, axis=-1, keepdims=True) + EPS)
    gy = dxn * w
    dh = r * gy - h * ((r * r * r) * (1.0 / d) * jnp.sum(gy * h, axis=-1, keepdims=True))
    return [dh_in + dh] * 2, [dxn * h * r]


def _final_fn(rv, vv):
    h, tgt = rv
    w, = vv
    d = h.shape[-1]
    r = lax.rsqrt(jnp.mean(h * h, axis=-1, keepdims=True) + EPS)
    hn = h * r
    e = hn * w - tgt
    dy = e * (1.0 / d)
    gy = dy * w
    dh = r * gy - h * ((r * r * r) * (1.0 / d) * jnp.sum(gy * h, axis=-1, keepdims=True))
    return [dh] * 2, [e * e, dy * hn]


def _onorm_fwd_fn(rv, vv):
    o, g = rv
    gain, = vv
    r = lax.rsqrt(jnp.mean(o * o, axis=-1, keepdims=True) + EPS)
    return [o * r * gain * (g * _sigmoid(g))], []


def _onorm_bwd_fn(rv, vv):
    o, g, don = rv
    gain, = vv
    d = o.shape[-1]
    r = lax.rsqrt(jnp.mean(o * o, axis=-1, keepdims=True) + EPS)
    sg = _sigmoid(g)
    sl = g * sg
    n = o * r
    dg = don * n * gain * (sg * (1.0 + g * (1.0 - sg)))
    gy = don * sl * gain
    do = r * gy - o * ((r * r * r) * (1.0 / d) * jnp.sum(gy * o, axis=-1, keepdims=True))
    return [do, dg], [don * sl * n]


HALO = SUBLANES


def _col_call(fn, cols, vecs, outs, n_acc, name, before, after, tc=LANES, chunk=256):
    T = cols[0][0].shape[0]
    chunk = min(chunk, T)
    nch = T // chunk
    ncol = outs[0][1] // tc
    n_c, n_v, n_o = len(cols), len(vecs), len(outs)
    hb = HALO if before else 0
    rw = chunk + hb + (HALO if after else 0)

    def body(*refs):
        c_refs = refs[:n_c]
        v_refs = refs[n_c:n_c + n_v]
        o_refs = refs[n_c + n_v:n_c + n_v + n_o]
        a_refs = refs[n_c + n_v + n_o:]
        vv = [v[...] for v in v_refs]
        wrow = lax.broadcasted_iota(jnp.int32, (rw, tc), 0)
        inside = (wrow >= hb) & (wrow < hb + chunk)

        def step(i, carry):
            r0 = pl.multiple_of(i * chunk, chunk)
            wins = []
            for ref in c_refs:
                parts = []
                if before:
                    pb = ref[pl.ds(pl.multiple_of(jnp.maximum(r0 - HALO, 0), HALO), HALO), :]
                    parts.append(jnp.where(i > 0, pb, 0.0))
                parts.append(ref[pl.ds(r0, chunk), :])
                if after:
                    pa = ref[pl.ds(pl.multiple_of(jnp.minimum(r0 + chunk, T - HALO), HALO), HALO), :]
                    parts.append(jnp.where(i < nch - 1, pa, 0.0))
                wins.append(jnp.concatenate(parts, axis=0) if len(parts) > 1 else parts[0])
            o_vals, a_vals = fn(wins, vv, inside)
            p = 0
            for o, (nseg, _, _) in zip(o_refs, outs):
                for s in range(nseg):
                    o[s, pl.ds(r0, chunk), :] = o_vals[p][hb:hb + chunk].astype(o.dtype)
                    p += 1
            return tuple(c + a for c, a in zip(carry, a_vals))

        taps = [v.shape[0] for v, _ in vecs][:n_acc]
        init = tuple(jnp.zeros((1, tc), F32) for k in taps for _ in range(k))
        sums = lax.fori_loop(0, nch, step, init)
        arow = lax.broadcasted_iota(jnp.int32, (SUBLANES, tc), 0)
        p = 0
        for a, k in zip(a_refs, taps):
            acc = jnp.zeros((SUBLANES, tc), F32)
            for t in range(k):
                acc = jnp.where(arow == t, sums[p], acc)
                p += 1
            a[...] = acc

    in_specs = [pl.BlockSpec((T, tc), functools.partial(lambda j, off: (0, off + j), off=off)) for _, off in cols]
    in_specs += [pl.BlockSpec((v.shape[0], tc), functools.partial(lambda j, off: (0, off + j), off=off))
                 for v, off in vecs]
    out_specs = [pl.BlockSpec((nseg, T, tc), lambda j: (0, 0, j)) for nseg, _, _ in outs]
    out_specs += [pl.BlockSpec((SUBLANES, tc), lambda j: (0, j)) for _ in range(n_acc)]
    out_shape = [jax.ShapeDtypeStruct((nseg, T, w), dt) for nseg, w, dt in outs]
    out_shape += [jax.ShapeDtypeStruct((SUBLANES, ncol * tc), F32) for _ in range(n_acc)]
    return _pcall(
        body, name=name, grid=(ncol,), in_specs=in_specs, out_specs=out_specs, out_shape=out_shape,
        compiler_params=_params(("parallel",)),
    )(*[c[0] for c in cols], *[v[0] for v in vecs])


def _down(x, k):
    return x if k == 0 else pltpu.roll(x, k, 0)


def _up(x, k):
    return x if k == 0 else pltpu.roll(x, x.shape[0] - k, 0)


def _lags(x):
    return _down(x, 2), _down(x, 1), x


def _conv(lags, w):
    return w[0:1] * lags[0] + w[1:2] * lags[1] + w[2:3] * lags[2]


def _conv_t(d, w):
    return w[2:3] * d + w[1:2] * _up(d, 1) + w[0:1] * _up(d, 2)


def _tap_sums(d, lags, inside):
    dm = jnp.where(inside, d, 0.0)
    return [jnp.sum(dm * lag, axis=0, keepdims=True) for lag in lags]


def _glu_fwd_fn(wins, vv, inside):
    xg, xv = wins
    wg, wv = vv
    ug = _conv(_lags(xg), wg)
    uv = _conv(_lags(xv), wv)
    return [ug * _sigmoid(ug) * uv], []


def _glu_bwd_fn(wins, vv, inside):
    xg, xv, da = wins
    wg, wv = vv
    lg, lv = _lags(xg), _lags(xv)
    ug = _conv(lg, wg)
    uv = _conv(lv, wv)
    sg = _sigmoid(ug)
    dug = da * uv * (sg * (1.0 + ug * (1.0 - sg)))
    duv = da * (ug * sg)
    return [_conv_t(dug, wg), _conv_t(duv, wv)], _tap_sums(dug, lg, inside) + _tap_sums(duv, lv, inside)


def _sc_fwd_fn(wins, vv, inside):
    gb, gc, hh = wins
    w, = vv
    return [gb * _conv(_lags(gc * hh), w)], []


def _sc_bwd_fn(wins, vv, inside):
    gb, gc, hh, dy = wins
    w, = vv
    lz = _lags(gc * hh)
    dcv = dy * gb
    dz = _conv_t(dcv, w)
    return [dy * _conv(lz, w), dz * hh, dz * gc], _tap_sums(dcv, lz, inside)


def _gates(qr, fr, lb):
    sg = _sigmoid(fr)
    f = lb + (1.0 - lb) * sg
    sq = _sigmoid(qr)
    q = qr * sq * (HEAD ** -0.5)
    return q, 1.0 - f, jnp.log(f), f, sg, sq


def _boundary_rows(b, g, row):
    c = b.shape[0]
    if 2 * g >= SUBLANES:
        x = b.reshape(c // (2 * g), 2 * g, LANES)
        return jnp.broadcast_to(x[:, g - 1:g, :], x.shape).reshape(c, LANES)
    x = b.reshape(c // SUBLANES, SUBLANES, LANES)
    lo = jnp.broadcast_to(x[:, 1:2, :], x.shape).reshape(c, LANES)
    hi = jnp.broadcast_to(x[:, 5:6, :], x.shape).reshape(c, LANES)
    return jnp.where((row & 4) == 0, lo, hi)


def _chunk_decays(gl, f, row):
    c = gl.shape[0]
    b = gl
    d = 1
    while d < c:
        b = b + jnp.where(row >= d, pltpu.roll(b, d, 0), 0.0)
        d *= 2
    eq, ek = [], []
    g = c // 2
    while g >= 2:
        right = (row & g) != 0
        m = _boundary_rows(b, g, row)
        z = jnp.exp(jnp.where(right, b - m, m - b))
        eq.append(jnp.where(right, z, 0.0))
        ek.append(jnp.where(right, 0.0, z))
        g //= 2
    odd = (row & 1) != 0
    eq.append(jnp.where(odd, f, 0.0))
    ek.append(jnp.where(odd, 0.0, 1.0))
    return b, eq, ek


def _intra(q, k, eq, ek, tt, ss):
    c = q.shape[0]
    qs, ks = [], []
    a = jnp.where(tt == ss, jnp.sum(q * k, axis=1, keepdims=True), 0.0)
    g = c // 2
    for e_q, e_k in zip(eq, ek):
        qg = (q * e_q).astype(BF16)
        kg = (k * e_k).astype(BF16)
        p = lax.dot_general(qg, kg, (((1,), (1,)), ((), ())), preferred_element_type=F32)
        a = a + (p if 2 * g >= c else jnp.where((tt ^ ss) < 2 * g, p, 0.0))
        qs.append(qg)
        ks.append(kg)
        g //= 2
    return a, qs, ks


def _hgrn_fwd(proj, lb, d_model):
    T = proj.shape[0]
    H = d_model // HEAD
    nch = T // CHUNK

    def body(q_ref, f_ref, v_ref, lb_ref, o_ref, s_ref):
        lbv = lb_ref[...]
        row = lax.broadcasted_iota(jnp.int32, (CHUNK, HEAD), 0)
        tt = lax.broadcasted_iota(jnp.int32, (CHUNK, CHUNK), 0)
        ss = lax.broadcasted_iota(jnp.int32, (CHUNK, CHUNK), 1)

        def step(i, st):
            sl = pl.ds(pl.multiple_of(i * CHUNK, CHUNK), CHUNK)
            q, k, gl, f, _, _ = _gates(q_ref[sl, :], f_ref[sl, :], lbv)
            v = v_ref[sl, :].astype(BF16)
            b, eq, ek = _chunk_decays(gl, f, row)
            a, _, _ = _intra(q, k, eq, ek, tt, ss)
            bl = b[CHUNK - 1:CHUNK, :]
            q0 = (q * jnp.exp(b)).astype(BF16)
            kh = (k * jnp.exp(bl - b)).astype(BF16)
            s_ref[i] = st
            o = jnp.dot(a.astype(BF16), v, preferred_element_type=F32)
            o = o + lax.dot_general(q0, st.astype(BF16), (((1,), (1,)), ((), ())), preferred_element_type=F32)
            o_ref[sl, :] = o
            return jnp.exp(bl) * st + lax.dot_general(v, kh, (((0,), (0,)), ((), ())), preferred_element_type=F32)

        per = 4 if nch % 4 == 0 else 2

        def trip(i, st):
            for u in range(per):
                st = step(per * i + u, st)
            return st

        lax.fori_loop(0, nch // per, trip, jnp.zeros((HEAD, HEAD), F32))

    col = lambda off: pl.BlockSpec((T, HEAD), functools.partial(lambda h, off: (0, off + h), off=off))
    return _pcall(
        body, name="hgrn_fwd", grid=(H,),
        in_specs=[col(0), col(H), col(2 * H), pl.BlockSpec((1, HEAD), lambda h: (0, h))],
        out_specs=[pl.BlockSpec((T, HEAD), lambda h: (0, h)),
                   pl.BlockSpec((None, nch, HEAD, HEAD), lambda h: (h, 0, 0, 0))],
        out_shape=[jax.ShapeDtypeStruct((T, d_model), F32), jax.ShapeDtypeStruct((H, nch, HEAD, HEAD), F32)],
        compiler_params=_params(("parallel",)),
    )(proj, proj, proj, lb)


def _hgrn_bwd(proj, lb, states, do, dgate, d_model):
    T = proj.shape[0]
    H = d_model // HEAD
    nch = T // CHUNK

    def body(q_ref, f_ref, v_ref, lb_ref, s_ref, do_ref, dg_ref, dp_ref, dlb_ref):
        dq_ref, df_ref, dv_ref = dp_ref.at[0], dp_ref.at[1], dp_ref.at[2]
        dp_ref[3] = dg_ref[...]
        lbv = lb_ref[...]
        row = lax.broadcasted_iota(jnp.int32, (CHUNK, HEAD), 0)
        tt = lax.broadcasted_iota(jnp.int32, (CHUNK, CHUNK), 0)
        ss = lax.broadcasted_iota(jnp.int32, (CHUNK, CHUNK), 1)
        last = row == CHUNK - 1
        nt = (((1,), (1,)), ((), ()))
        tn = (((0,), (0,)), ((), ()))

        def step(j, carry):
            dst, dlb = carry
            i = nch - 1 - j
            sl = pl.ds(pl.multiple_of(i * CHUNK, CHUNK), CHUNK)
            qr = q_ref[sl, :]
            q, k, gl, f, sg, sq = _gates(qr, f_ref[sl, :], lbv)
            v = v_ref[sl, :].astype(BF16)
            d_o = do_ref[sl, :].astype(BF16)
            st = s_ref[i]
            st16 = st.astype(BF16)
            dst16 = dst.astype(BF16)
            b, eq, ek = _chunk_decays(gl, f, row)
            a, qs, ks = _intra(q, k, eq, ek, tt, ss)
            bl = b[CHUNK - 1:CHUNK, :]
            e0 = jnp.exp(b)
            eh = jnp.exp(bl - b)
            ebl = jnp.exp(bl)
            q0 = q * e0
            kh = k * eh
            q016 = q0.astype(BF16)
            kh16 = kh.astype(BF16)
            dv = lax.dot_general(a.astype(BF16), d_o, tn, preferred_element_type=F32)
            dv = dv + lax.dot_general(kh16, dst16, nt, preferred_element_type=F32)
            dv_ref[sl, :] = dv.astype(dv_ref.dtype)
            da = lax.dot_general(d_o, v, nt, preferred_element_type=F32)
            da = jnp.where(tt >= ss, da, 0.0)
            dd = jnp.sum(jnp.where(tt == ss, da, 0.0), axis=1, keepdims=True)
            dq0 = jnp.dot(d_o, st16, preferred_element_type=F32)
            dkh = jnp.dot(v, dst16, preferred_element_type=F32)
            dq = dq0 * e0 + dd * k
            dk = dkh * eh + dd * q
            db = dq0 * q016.astype(F32) - dkh * kh16.astype(F32)
            g = CHUNK // 2
            for e_q, e_k, qg, kg in zip(eq, ek, qs, ks):
                dag = (da if 2 * g >= CHUNK else jnp.where((tt ^ ss) < 2 * g, da, 0.0)).astype(BF16)
                dqg = jnp.dot(dag, kg, preferred_element_type=F32)
                dkg = lax.dot_general(dag, qg, tn, preferred_element_type=F32)
                dq = dq + dqg * e_q
                dk = dk + dkg * e_k
                db = db + (dqg * qg.astype(F32) - dkg * kg.astype(F32))
                g //= 2
            dbl = jnp.sum(dkh * kh16.astype(F32), axis=0, keepdims=True) + ebl * jnp.sum(dst * st, axis=0, keepdims=True)
            db = db + jnp.where(last, dbl, 0.0)
            d = 1
            while d < CHUNK:
                db = db + jnp.where(row < CHUNK - d, pltpu.roll(db, CHUNK - d, 0), 0.0)
                d *= 2
            dfg = db / f - dk
            df_ref[sl, :] = (dfg * (1.0 - lbv) * sg * (1.0 - sg)).astype(df_ref.dtype)
            dq_ref[sl, :] = (dq * (HEAD ** -0.5) * (sq * (1.0 + qr * (1.0 - sq)))).astype(dq_ref.dtype)
            dlb = dlb + jnp.sum(dfg * (1.0 - sg), axis=0, keepdims=True)
            dst = ebl * dst + lax.dot_general(d_o, q016, tn, preferred_element_type=F32)
            return dst, dlb

        _, dlb = lax.fori_loop(0, nch // 2, lambda j, cr: step(2 * j + 1, step(2 * j, cr)),
                               (jnp.zeros((HEAD, HEAD), F32), jnp.zeros((1, HEAD), F32)))
        arow = lax.broadcasted_iota(jnp.int32, (SUBLANES, HEAD), 0)
        dlb_ref[...] = jnp.where(arow == 0, dlb, 0.0)

    col = lambda off: pl.BlockSpec((T, HEAD), functools.partial(lambda h, off: (0, off + h), off=off))
    return _pcall(
        body, name="hgrn_bwd", grid=(H,),
        in_specs=[col(0), col(H), col(2 * H), pl.BlockSpec((1, HEAD), lambda h: (0, h)),
                  pl.BlockSpec((None, nch, HEAD, HEAD), lambda h: (h, 0, 0, 0)), col(0), col(0)],
        out_specs=[pl.BlockSpec((4, T, HEAD), lambda h: (0, 0, h)), pl.BlockSpec((SUBLANES, HEAD), lambda h: (0, h))],
        out_shape=[jax.ShapeDtypeStruct((4, T, d_model), BF16), jax.ShapeDtypeStruct((SUBLANES, d_model), F32)],
        compiler_params=_params(("parallel",)),
    )(proj, proj, proj, lb, states, do, dgate)


def _lb_softmax(table):
    n, f = table.shape

    def body(t_ref, p_ref):
        t = t_ref[...]
        e = jnp.exp(t - jnp.max(t, axis=0, keepdims=True))
        p_ref[...] = e / jnp.sum(e, axis=0, keepdims=True)

    padded = jnp.pad(table, ((0, SUBLANES - n), (0, 0)), constant_values=-jnp.inf)
    return _pcall(body, name="lb_softmax", out_shape=jax.ShapeDtypeStruct((SUBLANES, f), F32))(padded)


def _adamw_math(w, g, m, v):
    m = ADAM_B1 * m + (1.0 - ADAM_B1) * g
    v = ADAM_B2 * v + (1.0 - ADAM_B2) * (g * g)
    m_hat = m / (1.0 - ADAM_B1 ** ADAM_STEP)
    v_hat = v / (1.0 - ADAM_B2 ** ADAM_STEP)
    delta = -ADAM_LR * (m_hat / (jnp.sqrt(v_hat) + ADAM_EPS) + ADAM_WD * w)
    return delta, m, v


def _adamw(w, g, m, v, name):
    R, C = w.shape
    tr = _pick(R, (128, 64, 32, 16, 8))

    def body(w_ref, g_ref, m_ref, v_ref, d_ref, nm_ref, nv_ref):
        d, nm, nv = _adamw_math(w_ref[...], g_ref[...], m_ref[...], v_ref[...])
        d_ref[...] = d
        nm_ref[...] = nm
        nv_ref[...] = nv

    spec = pl.BlockSpec((tr, C), lambda i: (i, 0))
    return _pcall(
        body, name=name, grid=(R // tr,), in_specs=[spec] * 4, out_specs=[spec] * 3,
        out_shape=[jax.ShapeDtypeStruct((R, C), F32)] * 3, compiler_params=_params(("parallel",)),
    )(w, g, m, v)


def _adamw_halves(w, m, v, g_mine, g_recv, c, name, layer=0, prev=None):
    C = w.shape[1]
    rh = g_mine.shape[0]
    tr = _pick(rh, (128, 64, 32, 16, 8))
    nb = rh // tr
    r0 = layer * 2 * nb

    def body(c_ref, w_ref, m_ref, v_ref, gm_ref, gr_ref, *rest):
        g_ref, d_ref, nm_ref, nv_ref = rest[-4:]
        g = jnp.where(pl.program_id(0) == c_ref[0], gm_ref[...], gr_ref[...])
        d, nm, nv = _adamw_math(w_ref[...], g, m_ref[...], v_ref[...])
        g_ref[...] = g
        d_ref[...] = d
        nm_ref[...] = nm
        nv_ref[...] = nv

    full = pl.BlockSpec((tr, C), lambda h, i, cr: (r0 + h * nb + i, 0))
    mine = pl.BlockSpec((tr, C), lambda h, i, cr: (jnp.where(h == cr[0], i, 0), 0))
    recv = pl.BlockSpec((tr, C), lambda h, i, cr: (jnp.where(h == cr[0], 0, i), 0))
    in_specs = [full, full, full, mine, recv]
    args = [c, w, m, v, g_mine, g_recv]
    alias = {}
    if prev is not None:
        in_specs += [pl.BlockSpec(memory_space=pl.ANY)] * 4
        args += list(prev)
        alias = {6 + k: k for k in range(4)}
    return _pcall(
        body, name=name,
        grid_spec=pltpu.PrefetchScalarGridSpec(
            num_scalar_prefetch=1, grid=(2, nb), in_specs=in_specs, out_specs=[full] * 4),
        out_shape=[jax.ShapeDtypeStruct(w.shape, F32)] * 4, input_output_aliases=alias,
        compiler_params=_params(("parallel", "parallel")),
    )(*args)


def _lb_table_grad(p8, dlb, n):
    f = p8.shape[1]

    def body(p_ref, d_ref, o_ref):
        p = p_ref[...]
        d = d_ref[...]
        p0 = p[0:1, :]
        first = lax.broadcasted_iota(jnp.int32, p.shape, 0) == 0
        o_ref[...] = p * (jnp.where(first, d, 0.0) - d * p0)

    return _pcall(body, name="lb_table_grad", out_shape=jax.ShapeDtypeStruct((SUBLANES, f), F32))(p8, dlb)[:n]


def _place():
    x, y, c = lax.axis_index("x"), lax.axis_index("y"), lax.axis_index("c")
    chips = [(1 - x, y), (x, 1 - y), (1 - x, 1 - y)]
    return x, y, c, chips


HBM_SPEC = pl.BlockSpec(memory_space=pltpu.HBM)


def _gather_weights(big, small):
    nb, ns = len(big), len(small)
    n = nb + ns

    def body(*refs):
        ins, outs = refs[:n], refs[n:2 * n]
        send_sems, recv_sems, own_send, own_recv = refs[2 * n:]
        x, y, c, chips = _place()
        me = 2 * x + y
        sib = (x, y, 1 - c)
        own = [pltpu.make_async_remote_copy(
            src_ref=ins[t], dst_ref=outs[t].at[me], send_sem=own_send.at[t], recv_sem=own_recv.at[t],
            device_id=sib, device_id_type=MESH) for t in range(n)]
        for cp in own:
            cp.start()

        def half(t, h):
            rh = big[t].shape[0] // 2
            return pl.ds(pl.multiple_of(h * rh, rh), rh)

        sends = []
        for t in range(n):
            for j, chip in enumerate(chips):
                k = 6 * t + j
                if t < nb:
                    src, dst = ins[t].at[half(t, c)], outs[t].at[me, half(t, c)]
                else:
                    src, dst = ins[t], outs[t].at[me]
                sends.append(pltpu.make_async_remote_copy(
                    src_ref=src, dst_ref=dst, send_sem=send_sems.at[k], recv_sem=recv_sems.at[k],
                    device_id=(*chip, c), device_id_type=MESH))
        for cp in sends:
            cp.start()
        passed = []
        for t in range(n):
            for j, (cx, cy) in enumerate(chips):
                k = 6 * t + j
                s = 2 * cx + cy
                if t < nb:
                    landed = outs[t].at[s, half(t, c)]
                    pltpu.make_async_remote_copy(
                        src_ref=landed, dst_ref=landed, send_sem=send_sems.at[k], recv_sem=recv_sems.at[k],
                        device_id=sib, device_id_type=MESH).wait_recv()
                    fwd = pltpu.make_async_remote_copy(
                        src_ref=landed, dst_ref=landed, send_sem=send_sems.at[k + 3], recv_sem=recv_sems.at[k + 3],
                        device_id=sib, device_id_type=MESH)
                    fwd.start()
                    passed.append(fwd)
                else:
                    landed = outs[t].at[s]
                    pltpu.make_async_remote_copy(
                        src_ref=landed, dst_ref=landed, send_sem=send_sems.at[k], recv_sem=recv_sems.at[k],
                        device_id=sib, device_id_type=MESH).wait_recv()
        for t in range(nb):
            for j, (cx, cy) in enumerate(chips):
                k = 6 * t + j
                other = outs[t].at[2 * cx + cy, half(t, 1 - c)]
                pltpu.make_async_remote_copy(
                    src_ref=other, dst_ref=other, send_sem=send_sems.at[k + 3], recv_sem=recv_sems.at[k + 3],
                    device_id=sib, device_id_type=MESH).wait_recv()
        for cp in sends + passed:
            cp.wait_send()
        for cp in own:
            cp.wait()

    arrs = list(big) + list(small)
    return _pcall(
        body, name="gather_weights", in_specs=[HBM_SPEC] * n, out_specs=[HBM_SPEC] * n,
        out_shape=[jax.ShapeDtypeStruct((N_CHIPS,) + a.shape, a.dtype) for a in arrs],
        scratch_shapes=[pltpu.SemaphoreType.DMA((6 * n,)), pltpu.SemaphoreType.DMA((6 * n,)),
                        pltpu.SemaphoreType.DMA((n,)), pltpu.SemaphoreType.DMA((n,))],
    )(*arrs)


SEM_SPEC = pl.BlockSpec(memory_space=pltpu.SEMAPHORE)
DATAFLOW = pltpu.SideEffectType.DATAFLOW_SIDE_EFFECTING
COPIES_PER_SHARD = 4


def _shard_copies(ins, lands, send_sems, recv_sems, base=0):
    x, y, c, chips = _place()
    me = 2 * x + y
    cps = []
    for t in range(len(ins)):
        rh = ins[t].shape[0] // 2
        half = pl.ds(pl.multiple_of(c * rh, rh), rh)
        for j, chip in enumerate(chips):
            k = COPIES_PER_SHARD * (base + t) + j
            cps.append(pltpu.make_async_remote_copy(
                src_ref=ins[t].at[half], dst_ref=lands[t].at[me, half], send_sem=send_sems.at[k],
                recv_sem=recv_sems.at[k], device_id=(*chip, c), device_id_type=MESH))
        k = COPIES_PER_SHARD * (base + t) + 3
        cps.append(pltpu.make_async_remote_copy(
            src_ref=ins[t], dst_ref=lands[t].at[me], send_sem=send_sems.at[k], recv_sem=recv_sems.at[k],
            device_id=(x, y, 1 - c), device_id_type=MESH))
    return cps


def _gather_start(shards, thru, name):
    n = len(shards)
    nops = 2 * n + len(thru)

    def body(*refs):
        ins, lands = refs[:n], refs[n:2 * n]
        send_sems, recv_sems = refs[nops], refs[nops + 1]
        for cp in _shard_copies(ins, lands, send_sems, recv_sems):
            cp.start()

    lands = [pltpu.with_memory_space_constraint(lax.empty((N_CHIPS,) + s.shape, s.dtype), pltpu.HBM) for s in shards]
    ops = [pltpu.with_memory_space_constraint(s, pltpu.HBM) for s in shards] + lands + list(thru)
    nsem = COPIES_PER_SHARD * n
    res = _pcall(
        body, name=name, in_specs=[HBM_SPEC] * nops,
        out_specs=[SEM_SPEC, SEM_SPEC] + [HBM_SPEC] * nops,
        out_shape=[pltpu.SemaphoreType.DMA((nsem,)), pltpu.SemaphoreType.DMA((nsem,))]
        + [pltpu.HBM(o.shape, o.dtype) for o in ops],
        input_output_aliases={i: 2 + i for i in range(nops)},
        compiler_params=pltpu.CompilerParams(has_side_effects=DATAFLOW),
    )(*ops)
    return res[0], res[1], res[2:2 + n], res[2 + n:2 + 2 * n], list(res[2 + 2 * n:])


def _gather_wait(send_sems, recv_sems, shards, lands, after, name, base=0, which=None):
    n = len(shards)

    def body(*refs):
        ins, lnd = refs[:n], refs[n:2 * n]
        ssem, rsem = refs[2 * n], refs[2 * n + 1]
        for k, cp in enumerate(_shard_copies(ins, lnd, ssem, rsem, base)):
            if which is None or k % COPIES_PER_SHARD in which:
                cp.wait_send()
                cp.wait_recv()

    res = _pcall(
        body, name=name,
        in_specs=[HBM_SPEC] * (2 * n) + [SEM_SPEC, SEM_SPEC, pl.BlockSpec(memory_space=pl.ANY)],
        out_specs=[HBM_SPEC] * (2 * n),
        out_shape=[pltpu.HBM(o.shape, o.dtype) for o in list(shards) + list(lands)],
        input_output_aliases={i: i for i in range(2 * n)},
        compiler_params=pltpu.CompilerParams(has_side_effects=DATAFLOW),
    )(*shards, *lands, send_sems, recv_sems, after)
    return res[:n], res[n:]


SIBLING_PAIR = 1


def _sibling_handshake():
    x, y, c, _ = _place()
    barrier = pltpu.get_barrier_semaphore()
    pl.semaphore_signal(barrier, inc=1, device_id=(x, y, 1 - c), device_id_type=MESH)
    pl.semaphore_wait(barrier, 1)


def _forward_copies(land, send_sems, recv_sems, which=(0, 1, 2)):
    x, y, c, chips = _place()
    rh = land.shape[1] // 2
    return [pltpu.make_async_remote_copy(
        src_ref=land.at[2 * cx + cy, pl.ds(pl.multiple_of(c * rh, rh), rh)],
        dst_ref=land.at[2 * cx + cy, pl.ds(pl.multiple_of(c * rh, rh), rh)],
        send_sem=send_sems.at[j], recv_sem=recv_sems.at[j], device_id=(x, y, 1 - c), device_id_type=MESH)
        for j, (cx, cy) in enumerate(chips) if j in which]


def _forward_start(land, thru, name, which=(0, 1, 2)):
    def body(land_ref, thru_ref, send_sems, recv_sems, out_ref, thru_out):
        _sibling_handshake()
        for cp in _forward_copies(land_ref, send_sems, recv_sems, which):
            cp.start()

    return _pcall(
        body, name=name, in_specs=[HBM_SPEC, HBM_SPEC], out_specs=[SEM_SPEC, SEM_SPEC, HBM_SPEC, HBM_SPEC],
        out_shape=[pltpu.SemaphoreType.DMA((3,)), pltpu.SemaphoreType.DMA((3,)), pltpu.HBM(land.shape, land.dtype),
                   pltpu.HBM(thru.shape, thru.dtype)],
        input_output_aliases={0: 2, 1: 3},
        compiler_params=pltpu.CompilerParams(has_side_effects=DATAFLOW, collective_id=SIBLING_PAIR),
    )(land, thru)


def _forward_wait(send_sems, recv_sems, land, after, name, which=(0, 1, 2)):
    def body(land_ref, ssem, rsem, after_ref, out_ref):
        for cp in _forward_copies(land_ref, ssem, rsem, which):
            cp.wait_send()
            cp.wait_recv()

    return _pcall(
        body, name=name, in_specs=[HBM_SPEC, SEM_SPEC, SEM_SPEC, pl.BlockSpec(memory_space=pl.ANY)],
        out_specs=HBM_SPEC, out_shape=pltpu.HBM(land.shape, land.dtype), input_output_aliases={0: 0},
        compiler_params=pltpu.CompilerParams(has_side_effects=DATAFLOW),
    )(land, send_sems, recv_sems, after)


def _sibling_copies(ins, lands, send_sems, recv_sems, other_half):
    x, y, c, _ = _place()
    return [pltpu.make_async_remote_copy(
        src_ref=ins[t].at[:, 1 - c] if other_half else ins[t], dst_ref=lands[t], send_sem=send_sems.at[t],
        recv_sem=recv_sems.at[t], device_id=(x, y, 1 - c), device_id_type=MESH) for t in range(len(ins))]


def _sibling_start(srcs, other_half, thru, name):
    n = len(srcs)
    nthru = 0 if thru is None else 1

    def body(*refs):
        ins, lands = refs[:n], refs[n:2 * n]
        send_sems, recv_sems = refs[2 * n + nthru], refs[2 * n + nthru + 1]
        _sibling_handshake()
        for cp in _sibling_copies(ins, lands, send_sems, recv_sems, other_half):
            cp.start()

    shapes = [(s.shape[0],) + s.shape[2:] if other_half else s.shape for s in srcs]
    lands = [pltpu.with_memory_space_constraint(lax.empty(sh, s.dtype), pltpu.HBM) for sh, s in zip(shapes, srcs)]
    ops = [pltpu.with_memory_space_constraint(s, pltpu.HBM) for s in srcs] + lands + ([] if thru is None else [thru])
    res = _pcall(
        body, name=name, in_specs=[HBM_SPEC] * len(ops),
        out_specs=[SEM_SPEC, SEM_SPEC] + [HBM_SPEC] * len(ops),
        out_shape=[pltpu.SemaphoreType.DMA((n,)), pltpu.SemaphoreType.DMA((n,))]
        + [pltpu.HBM(o.shape, o.dtype) for o in ops],
        input_output_aliases={i: 2 + i for i in range(len(ops))},
        compiler_params=pltpu.CompilerParams(has_side_effects=DATAFLOW, collective_id=SIBLING_PAIR),
    )(*ops)
    return res[0], res[1], res[2:2 + n], res[2 + n:2 + 2 * n], (None if thru is None else res[2 + 2 * n])


def _sibling_wait(send_sems, recv_sems, srcs, lands, other_half, after, name):
    n = len(srcs)

    def body(*refs):
        ins, lnd = refs[:n], refs[n:2 * n]
        ssem, rsem = refs[2 * n], refs[2 * n + 1]
        for cp in _sibling_copies(ins, lnd, ssem, rsem, other_half):
            cp.wait_send()
            cp.wait_recv()

    res = _pcall(
        body, name=name,
        in_specs=[HBM_SPEC] * (2 * n) + [SEM_SPEC, SEM_SPEC, pl.BlockSpec(memory_space=pl.ANY)],
        out_specs=[HBM_SPEC] * (2 * n),
        out_shape=[pltpu.HBM(o.shape, o.dtype) for o in list(srcs) + list(lands)],
        input_output_aliases={i: i for i in range(2 * n)},
        compiler_params=pltpu.CompilerParams(has_side_effects=DATAFLOW),
    )(*srcs, *lands, send_sems, recv_sems, after)
    return res[:n], res[n:]


def _chip_copies(ins, lands, send_sems, recv_sems):
    x, y, c, chips = _place()
    cps = []
    for t in range(len(ins)):
        for j, (cx, cy) in enumerate(chips):
            cps.append(pltpu.make_async_remote_copy(
                src_ref=ins[t].at[2 * cx + cy], dst_ref=lands[t].at[j],
                send_sem=send_sems.at[3 * t + j], recv_sem=recv_sems.at[3 * t + j],
                device_id=(cx, cy, c), device_id_type=MESH))
    return cps


def _chip_start(parts, thru, name):
    n = len(parts)

    def body(*refs):
        ins, lands = refs[:n], refs[n:2 * n]
        send_sems, recv_sems = refs[2 * n + 1], refs[2 * n + 2]
        for cp in _chip_copies(ins, lands, send_sems, recv_sems):
            cp.start()

    lands = [pltpu.with_memory_space_constraint(lax.empty((3,) + p.shape[1:], p.dtype), pltpu.HBM) for p in parts]
    ops = [pltpu.with_memory_space_constraint(p, pltpu.HBM) for p in parts] + lands + [thru]
    res = _pcall(
        body, name=name, in_specs=[HBM_SPEC] * (2 * n + 1),
        out_specs=[SEM_SPEC, SEM_SPEC] + [HBM_SPEC] * (2 * n + 1),
        out_shape=[pltpu.SemaphoreType.DMA((3 * n,)), pltpu.SemaphoreType.DMA((3 * n,))]
        + [pltpu.HBM(o.shape, o.dtype) for o in ops],
        input_output_aliases={i: 2 + i for i in range(2 * n + 1)},
        compiler_params=pltpu.CompilerParams(has_side_effects=DATAFLOW),
    )(*ops)
    return res[0], res[1], res[2:2 + n], res[2 + n:2 + 2 * n], res[2 + 2 * n]


def _chip_wait(send_sems, recv_sems, parts, lands, after, name):
    n = len(parts)

    def body(*refs):
        ins, lnd = refs[:n], refs[n:2 * n]
        ssem, rsem = refs[2 * n], refs[2 * n + 1]
        for cp in _chip_copies(ins, lnd, ssem, rsem):
            cp.wait_send()
            cp.wait_recv()

    res = _pcall(
        body, name=name,
        in_specs=[HBM_SPEC] * (2 * n) + [SEM_SPEC, SEM_SPEC, pl.BlockSpec(memory_space=pl.ANY)],
        out_specs=[HBM_SPEC] * (2 * n),
        out_shape=[pltpu.HBM(o.shape, o.dtype) for o in list(parts) + list(lands)],
        input_output_aliases={i: i for i in range(2 * n)},
        compiler_params=pltpu.CompilerParams(has_side_effects=DATAFLOW),
    )(*parts, *lands, send_sems, recv_sems, after)
    return res[:n], res[n:]


def _add_pair(grad, recv, c, name):
    s, _, rh, cc = grad.shape
    tr = _pick(rh, (256, 128, 64, 32, 16))

    def body(c_ref, g_ref, r_ref, o_ref):
        o_ref[...] = (g_ref[...].astype(F32) + r_ref[...].astype(F32)).astype(o_ref.dtype)

    return _pcall(
        body, name=name,
        grid_spec=pltpu.PrefetchScalarGridSpec(
            num_scalar_prefetch=1, grid=(s, rh // tr),
            in_specs=[pl.BlockSpec((None, None, tr, cc), lambda a, i, cr: (a, cr[0], i, 0)),
                      pl.BlockSpec((None, tr, cc), lambda a, i, cr: (a, i, 0))],
            out_specs=pl.BlockSpec((None, tr, cc), lambda a, i, cr: (a, i, 0))),
        out_shape=jax.ShapeDtypeStruct((s, rh, cc), BF16),
        compiler_params=_params(("parallel", "parallel")),
    )(c, grad, recv)


def _add_chips(part, recv, me, name):
    _, rh, cc = part.shape
    tr = _pick(rh, (256, 128, 64, 32, 16))

    def body(m_ref, p_ref, r_ref, o_ref):
        o_ref[...] = ((p_ref[...].astype(F32) + r_ref[0].astype(F32)) + r_ref[1].astype(F32)) + r_ref[2].astype(F32)

    return _pcall(
        body, name=name,
        grid_spec=pltpu.PrefetchScalarGridSpec(
            num_scalar_prefetch=1, grid=(rh // tr,),
            in_specs=[pl.BlockSpec((None, tr, cc), lambda i, mr: (mr[0], i, 0)),
                      pl.BlockSpec((3, tr, cc), lambda i, mr: (0, i, 0))],
            out_specs=pl.BlockSpec((tr, cc), lambda i, mr: (i, 0))),
        out_shape=jax.ShapeDtypeStruct((rh, cc), F32),
        compiler_params=_params(("parallel",)),
    )(me, part, recv)


def _all_sum(vec):
    rows = vec.shape[0]

    def body(v_ref, o_ref, buf, send_sems, recv_sems):
        x, y, c, _ = _place()
        me = 4 * x + 2 * y + c
        buf[me] = v_ref[...]
        cps = []
        for r in range(1, 8):
            fx, fy, fc = (r >> 2) & 1, (r >> 1) & 1, r & 1
            peer = (x ^ fx, y ^ fy, c ^ fc)
            cps.append(pltpu.make_async_remote_copy(
                src_ref=v_ref, dst_ref=buf.at[me], send_sem=send_sems.at[r - 1], recv_sem=recv_sems.at[r - 1],
                device_id=peer, device_id_type=MESH))
        for cp in cps:
            cp.start()
        for r in range(1, 8):
            src = me ^ r
            pltpu.make_async_remote_copy(
                src_ref=v_ref, dst_ref=buf.at[src], send_sem=send_sems.at[r - 1], recv_sem=recv_sems.at[r - 1],
                device_id=(x, y, c), device_id_type=MESH).wait_recv()
        for cp in cps:
            cp.wait_send()
        acc = buf[0]
        for d in range(1, 8):
            acc = acc + buf[d]
        o_ref[...] = acc

    return _pcall(
        body, name="all_sum_small",
        in_specs=[pl.BlockSpec(memory_space=pltpu.VMEM)], out_specs=pl.BlockSpec(memory_space=pltpu.VMEM),
        out_shape=jax.ShapeDtypeStruct((rows, LANES), F32),
        scratch_shapes=[pltpu.VMEM((8, rows, LANES), F32), pltpu.SemaphoreType.DMA((7,)), pltpu.SemaphoreType.DMA((7,))],
    )(vec)


def _pack(parts):
    flat = jnp.concatenate([p.reshape(-1) for p in parts])
    tile = SUBLANES * LANES
    pad = (-flat.shape[0]) % tile
    return jnp.pad(flat, (0, pad)).reshape(-1, LANES)


def _unpack(vec, shapes):
    flat = vec.reshape(-1)
    out, p = [], 0
    for s in shapes:
        n = 1
        for d in s:
            n *= d
        out.append(flat[p:p + n].reshape(s))
        p += n
    return out


def _local_step(x, tgt, norm_mix, norm_ffn, lb8, out_norm, final_norm, sc_conv, ffn_conv, first, arrive, reduce_start,
                reduce_finish):
    T, D = x.shape
    F2 = ffn_conv.shape[-1]
    FF = F2 // 2
    tm = _pick(T, (1024, 512, 256, 128))
    wide = (1536, 1408, 1024, 768, 512, 384, 256, 128)
    cw_h, cw_s, cw_u = 4 * D // N_CHIPS, 3 * D // N_CHIPS, F2 // N_CHIPS
    kp = FF // N_CHIPS
    tk_ff = kp if kp % LANES == 0 else LANES
    tn_d = _pick(D, (1024, 512, 256, 128))
    tk_w = _pick(D, (512, 256, 128))
    tn_h = _pick(cw_h, (1024, 512, 256, 128))
    tn_s = _pick(D // N_CHIPS, (512, 256, 128))
    tn_u = _pick(cw_u, wide)
    lb = lb8[0:1]
    wm_sq = _wmap_col(D, tn_d, 0)
    wm_sq1 = _wmap_col(D, D, 0)
    seg1 = lambda a: a.reshape((1,) + a.shape)

    def mix_in(h, w):
        return _row_call(_rms_fwd_fn, [(h, 0, D)], [w], [(D, BF16)], 0, "rms_fwd")[0]

    def rms_bwd(h, dxn, dh, w):
        return _row_call(_rms_bwd_fn, [(h, 0, D), (dxn, 0, D), (dh, 0, D)], [w], [(D, F32), (D, BF16)], 1, "rms_bwd")

    def ffn_fwd(h, i, fetch_up, behind_down=()):
        xn = mix_in(h, norm_ffn[i:i + 1])
        tn = _pick(cw_u, wide)
        fetch_down, xn = arrive("ffn_w_down%d" % i, xn)
        w_up = fetch_up(xn)
        up = _mm_nn(xn, w_up, _wmap_col(cw_u, tn, 0), D, F2, tm, D, tn, "ffn_up")
        nb = FF // LANES
        a = _col_call(_glu_fwd_fn, [(up, 0), (up, nb)], [(ffn_conv[i], 0), (ffn_conv[i], nb)], [(1, FF, BF16)], 0,
                      "glu_fwd", before=True, after=False)[0][0]
        later = []
        for name in behind_down:
            fetch, a = arrive(name, a)
            later.append(fetch)
        w_down = fetch_down(a)
        h2 = _mm_nn(a, w_down, _wmap_row(kp, tk_ff, 0), FF, D, tm, tk_ff, tn_d, "ffn_down", res=h, per_step=2)
        return h2, (xn, up, a), w_up, w_down, later

    def ffn_bwd(dh, dh16, h, saved, i, w_up, w_down):
        xn, up, a = saved
        g_down = _mm_tn(a, seg1(dh16), (N_CHIPS, 1, kp, D), _wmap_row(kp, tk_ff, 0), FF, D, tk_ff, tn_d,
                        "ffn_down_dw", tm=_pick(T, (2048, 1024, 512, 256, 128)))
        dh16 = reduce_start(("ffn_w_down", i), g_down, dh16)
        da = _mm_nt(seg1(dh16), w_down, _wmap_row(kp, tk_ff, 0), FF, D, tm, tk_ff, D, "ffn_down_dx")
        nb = FF // LANES
        dgv, cg, cv = _col_call(_glu_bwd_fn, [(up, 0), (up, nb), (da, 0)], [(ffn_conv[i], 0), (ffn_conv[i], nb)],
                                [(2, FF, BF16)], 2, "glu_bwd", before=True, after=True)
        g_up = _mm_tn(xn, dgv, (N_CHIPS, 1, D, cw_u), _wmap_col(cw_u, tn_u, 0), D, F2, tk_w, tn_u, "ffn_up_dw")
        dgv = reduce_start(("ffn_w_up", i), g_up, dgv)
        dxn = _mm_nt(dgv, w_up, _wmap_col(cw_u, tn_u, 0), D, F2, _pick(T, (512, 256, 128)), D, tn_u, "ffn_up_dx",
                     per_step=2)
        dh2, dh2_16, dnw = rms_bwd(h, dxn, dh, norm_ffn[i:i + 1])
        return dh2, reduce_finish(dh2_16), dnw, jnp.concatenate([cg[:3], cv[:3]], axis=1)

    h0 = x
    xn0 = mix_in(h0, norm_mix[0:1])
    proj, w_hin = first(xn0, lambda w, s, prev: _mm_nn_shard(xn0, w, s, tm, tn_h, "hgrn_in", prev))
    o, states = _hgrn_fwd(proj, lb, D)
    fetch_hout, o = arrive("hgrn_w_out", o)
    on = _row_call(_onorm_fwd_fn, [(o, 0, D), (proj, 3, D)], [out_norm], [(D, BF16)], 0, "onorm_fwd")[0]
    fetch_up0, on = arrive("ffn_w_up0", on)
    w_hout1 = fetch_hout(on).reshape(1, D, D)
    h1 = _mm_nn(on, w_hout1, wm_sq, D, D, tm, D, tn_d, "hgrn_out", res=h0)
    h2, ffn0, w_up0, w_down0, (fetch_sin, fetch_sout) = ffn_fwd(h1, 0, fetch_up0, ("sc_w_in", "sc_w_out"))
    xn1 = mix_in(h2, norm_mix[1:2])
    w_sin = fetch_sin(xn1)
    tn_si = _pick(cw_s, wide)
    sproj = _mm_nn(xn1, w_sin, _wmap_col(cw_s, tn_si, 0), D, 3 * D, tm, D, tn_si, "sc_in")
    fetch_up1, sproj = arrive("ffn_w_up1", sproj)
    nd = D // LANES
    ysc = _col_call(_sc_fwd_fn, [(sproj, 0), (sproj, nd), (sproj, 2 * nd)], [(sc_conv, 0)], [(1, D, BF16)], 0,
                    "sc_fwd", before=True, after=False)[0][0]
    w_sout1 = fetch_sout(ysc).reshape(1, D, D)
    h3 = _mm_nn(ysc, w_sout1, wm_sq, D, D, tm, D, tn_d, "sc_out", res=h2)
    h4, ffn1, w_up1, w_down1, _ = ffn_fwd(h3, 1, fetch_up1)

    dh, dh16, esq, dfinal = _row_call(_final_fn, [(h4, 0, D), (tgt, 0, D)], [final_norm], [(D, F32), (D, BF16)], 2,
                                      "final_loss")
    loss = 0.5 / D * jnp.sum(esq)
    dh, dh16, dnf1, dconv1 = ffn_bwd(dh, dh16, h3, ffn1, 1, w_up1, w_down1)
    g_sout = _mm_tn(ysc, seg1(dh16), (1, 1, D, D), wm_sq, D, D, tk_w, tn_d, "sc_out_dw")
    dh16 = reduce_start(("sc_w_out", 0), g_sout, dh16)
    dy = _mm_nt(seg1(dh16), w_sout1, wm_sq1, D, D, tm, D, D, "sc_out_dx")
    dsp, dscc = _col_call(_sc_bwd_fn, [(sproj, 0), (sproj, nd), (sproj, 2 * nd), (dy, 0)], [(sc_conv, 0)],
                          [(3, D, BF16)], 1, "sc_bwd", before=True, after=True)
    g_sin = _mm_tn(xn1, dsp, (N_CHIPS, 1, D, cw_s), _wmap_col(cw_s, tn_s, 0), D, 3 * D, tk_w, tn_s, "sc_in_dw")
    dsp = reduce_start(("sc_w_in", 0), g_sin, dsp)
    dxn = _mm_nt(dsp, w_sin, _wmap_col(cw_s, tn_s, 0), D, 3 * D, tm, D, tn_s, "sc_in_dx", per_step=3)
    dh, dh16, dnm1 = rms_bwd(h2, dxn, dh, norm_mix[1:2])
    dh16 = reduce_finish(dh16)
    dh, dh16, dnf0, dconv0 = ffn_bwd(dh, dh16, h1, ffn0, 0, w_up0, w_down0)
    g_hout = _mm_tn(on, seg1(dh16), (1, 1, D, D), wm_sq, D, D, tk_w, tn_d, "hgrn_out_dw")
    dh16 = reduce_start(("hgrn_w_out", 0), g_hout, dh16)
    don = _mm_nt(seg1(dh16), w_hout1, wm_sq1, D, D, tm, D, D, "hgrn_out_dx")
    do, dgate, dgain = _row_call(_onorm_bwd_fn, [(o, 0, D), (proj, 3, D), (don, 0, D)], [out_norm],
                                 [(D, F32), (D, BF16)], 1, "onorm_bwd")
    dproj, dlb = _hgrn_bwd(proj, lb, states, do, dgate, D)
    g_hin = _mm_tn(xn0, dproj, (N_CHIPS, 1, D, cw_h), _wmap_col(cw_h, tn_h, 0), D, 4 * D, tk_w, tn_h, "hgrn_in_dw")
    dproj = reduce_start(("hgrn_w_in", 0), g_hin, dproj)
    dxn = _mm_nt(dproj, w_hin, _wmap_col(cw_h, tn_h, 0), D, 4 * D, tm, D, tn_h, "hgrn_in_dx", per_step=2)
    grad_x, _, dnm0 = rms_bwd(h0, dxn, dh, norm_mix[0:1])

    small = dict(
        loss=loss,
        norm_mix=jnp.stack([jnp.sum(dnm0, axis=0), jnp.sum(dnm1, axis=0)]),
        norm_ffn=jnp.stack([jnp.sum(dnf0, axis=0), jnp.sum(dnf1, axis=0)]),
        lb=dlb[0:1],
        out_norm=jnp.sum(dgain, axis=0)[None],
        final_norm=jnp.sum(dfinal, axis=0),
        sc_conv=dscc[:3],
        ffn_conv=jnp.stack([dconv0, dconv1]),
    )
    return grad_x, small


def kernel(x, norm_mix, norm_ffn, hgrn_w_in, hgrn_lb_table, hgrn_out_norm, hgrn_w_out, sc_w_in, sc_conv, sc_w_out, ffn_w_up, ffn_conv, ffn_w_down, final_norm, loss_target, m_norm_mix, m_norm_ffn, m_hgrn_w_in, m_hgrn_lb_table, m_hgrn_out_norm, m_hgrn_w_out, m_sc_w_in, m_sc_conv, m_sc_w_out, m_ffn_w_up, m_ffn_conv, m_ffn_w_down, m_final_norm, v_norm_mix, v_norm_ffn, v_hgrn_w_in, v_hgrn_lb_table, v_hgrn_out_norm, v_hgrn_w_out, v_sc_w_in, v_sc_conv, v_sc_w_out, v_ffn_w_up, v_ffn_conv, v_ffn_w_down, v_final_norm):
    D = x.shape[-1]
    xi, yi, ci = lax.axis_index("x"), lax.axis_index("y"), lax.axis_index("c")
    me_chip = (2 * xi + yi).astype(jnp.int32).reshape(1)
    me_core = ci.astype(jnp.int32).reshape(1)

    big_names = ["hgrn_w_in", "hgrn_w_out", "sc_w_in", "sc_w_out", "ffn_w_up", "ffn_w_down"]
    big_w = dict(hgrn_w_in=hgrn_w_in, hgrn_w_out=hgrn_w_out, sc_w_in=sc_w_in, sc_w_out=sc_w_out,
                 ffn_w_up=ffn_w_up, ffn_w_down=ffn_w_down)
    big_m = dict(hgrn_w_in=m_hgrn_w_in, hgrn_w_out=m_hgrn_w_out, sc_w_in=m_sc_w_in, sc_w_out=m_sc_w_out,
                 ffn_w_up=m_ffn_w_up, ffn_w_down=m_ffn_w_down)
    big_v = dict(hgrn_w_in=v_hgrn_w_in, hgrn_w_out=v_hgrn_w_out, sc_w_in=v_sc_w_in, sc_w_out=v_sc_w_out,
                 ffn_w_up=v_ffn_w_up, ffn_w_down=v_ffn_w_down)
    flat2 = lambda a: a.reshape(-1, a.shape[-1])

    sh = lambda a: a.reshape(-1, a.shape[-1]).astype(BF16)
    conv_shards = [flat2(sc_conv), flat2(ffn_conv)]
    in_order_of_use = [("hgrn_w_in", sh(hgrn_w_in)), ("hgrn_w_out", sh(hgrn_w_out)), ("ffn_w_up0", sh(ffn_w_up[0])),
                       ("ffn_w_down0", sh(ffn_w_down[0])), ("sc_w_in", sh(sc_w_in)), ("sc_w_out", sh(sc_w_out)),
                       ("ffn_w_up1", sh(ffn_w_up[1])), ("ffn_w_down1", sh(ffn_w_down[1]))]
    scc4, fcc4 = _gather_weights([], [flat2(sc_conv), flat2(ffn_conv)])
    names = [n for n, _ in in_order_of_use]
    ss, rs, src, land, (scc4, norm_mix) = _gather_start([s for _, s in in_order_of_use], [scc4, norm_mix], "gather_start")
    travelling = {n: (s, l) for n, s, l in zip(names, src, land)}

    def landed(name, after, call, which=None):
        (s,), (l,) = _gather_wait(ss, rs, [travelling[name][0]], [travelling[name][1]], after, call,
                                  base=names.index(name), which=which)
        travelling[name] = (s, l)
        return l

    scc = jnp.moveaxis(scc4, 0, 1).reshape(3, D)
    f2 = ffn_conv.shape[-1] * N_CHIPS
    fcc = jnp.moveaxis(fcc4.reshape(N_CHIPS, 2, 3, -1), 0, 2).reshape(2, 3, f2)

    def first(after, matmul):
        w = landed("hgrn_w_in", after, "gather_wait_0_own", which=(3,))
        proj = matmul(w, me_chip, None)
        others = [2 * (1 - xi) + yi, 2 * xi + (1 - yi), 2 * (1 - xi) + (1 - yi)]
        for j, s in enumerate(others):
            w = landed("hgrn_w_in", proj, "gather_wait_0_%d" % j, which=(j,))
            fs, fr, w, proj = _forward_start(w, proj, "gather_forward_start_0_%d" % j, which=(j,))
            w = _forward_wait(fs, fr, w, proj, "gather_forward_wait_0_%d" % j, which=(j,))
            travelling["hgrn_w_in"] = (travelling["hgrn_w_in"][0], w)
            proj = matmul(w, s.astype(jnp.int32).reshape(1), proj)
        return proj, w

    def arrive(name, after):
        fs, fr, w, after = _forward_start(landed(name, after, "gather_wait_" + name), after, "gather_forward_start_" + name)
        return functools.partial(_forward_wait, fs, fr, w, name="gather_forward_wait_" + name), after

    pending = []
    started = []

    def reduce_start(slot, grad, thru):
        t = sum(len(b[0]) for b in pending) + len(started)
        halves = grad.reshape(N_CHIPS, 2, -1, grad.shape[-1])
        ss, rs, src, land, thru = _sibling_start([halves], True, thru, "grad_pair_start_%d" % t)
        started.append((slot, t, ss, rs, src, land))
        return thru

    def reduce_finish(thru):
        k = len(pending)
        pair = []
        for slot, t, ss, rs, src, land in started:
            src, recv = _sibling_wait(ss, rs, src, land, True, thru, "grad_pair_wait_%d" % t)
            pair.append(_add_pair(src[0], recv[0], me_core, "grad_add_pair"))
        ss, rs, pair, land, thru = _chip_start(pair, thru, "grad_chip_start_%d" % k)
        pending.append(([s[0] for s in started], ss, rs, pair, land))
        started.clear()
        return thru

    lb8 = _lb_softmax(hgrn_lb_table)
    grad_x, small = _local_step(
        x[0], loss_target[0], norm_mix, norm_ffn, lb8, hgrn_out_norm, final_norm[None], scc, fcc, first, arrive,
        reduce_start, reduce_finish)

    small_names = ["loss", "norm_mix", "norm_ffn", "lb", "out_norm", "final_norm", "sc_conv", "ffn_conv"]
    parts = [small[n].astype(F32) for n in small_names]
    shapes = [p.shape for p in parts]
    tot = dict(zip(small_names, _unpack(reduce_finish(_all_sum(_pack(parts))), shapes)))
    loss = tot["loss"].reshape(())
    g_lb_table = _lb_table_grad(lb8, tot["lb"], hgrn_lb_table.shape[0])
    cw = sc_conv.shape[-1]
    g_sc_conv = lax.dynamic_slice_in_dim(tot["sc_conv"], me_chip[0] * cw, cw, axis=1)[None]
    cf = ffn_conv.shape[-1]
    g_ffn_conv = lax.dynamic_slice_in_dim(tot["ffn_conv"], me_chip[0] * cf, cf, axis=2)
    g_small = dict(norm_mix=tot["norm_mix"], norm_ffn=tot["norm_ffn"], hgrn_lb_table=g_lb_table,
                   hgrn_out_norm=tot["out_norm"], sc_conv=g_sc_conv, ffn_conv=g_ffn_conv, final_norm=tot["final_norm"])
    w_small = dict(norm_mix=norm_mix, norm_ffn=norm_ffn, hgrn_lb_table=hgrn_lb_table, hgrn_out_norm=hgrn_out_norm,
                   sc_conv=sc_conv, ffn_conv=ffn_conv, final_norm=final_norm)
    m_small = dict(norm_mix=m_norm_mix, norm_ffn=m_norm_ffn, hgrn_lb_table=m_hgrn_lb_table, hgrn_out_norm=m_hgrn_out_norm,
                   sc_conv=m_sc_conv, ffn_conv=m_ffn_conv, final_norm=m_final_norm)
    v_small = dict(norm_mix=v_norm_mix, norm_ffn=v_norm_ffn, hgrn_lb_table=v_hgrn_lb_table, hgrn_out_norm=v_hgrn_out_norm,
                   sc_conv=v_sc_conv, ffn_conv=v_ffn_conv, final_norm=v_final_norm)
    sm_names = list(g_small)
    sm_shapes = [w_small[n].shape for n in sm_names]
    d_s, m_s, v_s = _adamw(_pack([w_small[n] for n in sm_names]), _pack([g_small[n] for n in sm_names]),
                           _pack([m_small[n] for n in sm_names]), _pack([v_small[n] for n in sm_names]), "adamw_small")
    out_g, out_d, out_m, out_v = dict(g_small), {}, {}, {}
    for n, d_, m_, v_ in zip(sm_names, _unpack(d_s, sm_shapes), _unpack(m_s, sm_shapes), _unpack(v_s, sm_shapes)):
        out_d[n], out_m[n], out_v[n] = d_, m_, v_

    done = {}
    after = grad_x

    def add_and_share(k, after):
        slots, ss, rs, pair, land = pending[k]
        pair, recv = _chip_wait(ss, rs, pair, land, after, "grad_chip_wait_%d" % k)
        mine = [_add_chips(p, r, me_chip, "grad_add_chips") for p, r in zip(pair, recv)]
        ss, rs, mine, land, _ = _sibling_start(mine, False, None, "grad_share_start_%d" % k)
        return slots, ss, rs, mine, land

    def update(k, share, after):
        slots, ss, rs, mine, land = share
        mine, theirs = _sibling_wait(ss, rs, mine, land, False, after, "grad_share_wait_%d" % k)
        for (n, layer), gm, gr in zip(slots, mine, theirs):
            done[n] = _adamw_halves(flat2(big_w[n]), flat2(big_m[n]), flat2(big_v[n]), gm, gr, me_core, "adamw_" + n,
                                    layer=layer, prev=done.get(n))
        return done[slots[-1][0]][0]

    shares = []
    for k in range(len(pending) - 1):
        shares.append(add_and_share(k, after))
        after = shares[-1][3][0]
    for k, share in enumerate(shares):
        after = update(k, share, after)
    last = len(pending) - 1
    share = add_and_share(last, after)
    update(last, share, share[3][0])
    for n in big_names:
        out_g[n], out_d[n], out_m[n], out_v[n] = (a.reshape(big_w[n].shape) for a in done[n])

    order = ["norm_mix", "norm_ffn", "hgrn_w_in", "hgrn_lb_table", "hgrn_out_norm", "hgrn_w_out", "sc_w_in", "sc_conv",
             "sc_w_out", "ffn_w_up", "ffn_conv", "ffn_w_down", "final_norm"]
    return (loss, grad_x[None], *[out_g[n] for n in order], *[out_d[n] for n in order],
            *[out_m[n] for n in order], *[out_v[n] for n in order])
```

```python
import functools

import jax
import jax.numpy as jnp
from jax import lax
from jax.experimental import pallas as pl
from jax.experimental.pallas import tpu as pltpu

F32 = jnp.float32
BF16 = jnp.bfloat16
MESH = pl.DeviceIdType.MESH

EPS = 1e-6
CHUNK = 64
HEAD = 128
N_CHIPS = 4
ADAM_LR, ADAM_B1, ADAM_B2, ADAM_EPS, ADAM_WD, ADAM_STEP = 0.001, 0.9, 0.999, 1e-08, 0.01, 10
VMEM_LIMIT = 56 * 1024 * 1024
SUBLANES = 8
LANES = 128


def _pcall(body, **kw):
    return pl.pallas_call(body, **kw)


def _params(sem, vmem=VMEM_LIMIT):
    return pltpu.CompilerParams(dimension_semantics=sem, vmem_limit_bytes=vmem)


def _pick(dim, prefs):
    for p in prefs:
        if p <= dim and dim % p == 0:
            return p
    return dim


def _sigmoid(x):
    return 1.0 / (1.0 + jnp.exp(-x))


def _wmap_col(cw, tn, r0):
    bps = cw // tn
    return lambda kb, nb: (nb // bps, r0 + kb, nb % bps)


def _wmap_row(kp, tk, r0):
    bps = kp // tk
    return lambda kb, nb: (kb // bps, r0 + kb % bps, nb)


def _mm_nn(a, w3, wmap, K, N, tm, tk, tn, name, res=None, per_step=1):
    M = a.shape[0]
    u = per_step
    nk = K // (tk * u)

    def body(*refs):
        r_ref = None if res is None else refs[2 * u]
        o_ref = refs[2 * u + (0 if res is None else 1)]
        p = None
        for r in range(u):
            d = jnp.dot(refs[r][...], refs[u + r][...], preferred_element_type=F32)
            p = d if p is None else p + d
        if nk == 1:
            o_ref[...] = p if res is None else p + r_ref[...]
            return
        acc = refs[-1]
        k = pl.program_id(2)

        @pl.when(k == 0)
        def _():
            acc[...] = p

        @pl.when(k > 0)
        def _():
            acc[...] += p

        @pl.when(k == nk - 1)
        def _():
            o_ref[...] = acc[...] if res is None else acc[...] + r_ref[...]

    if nk == 1:
        grid = (M // tm, N // tn)
        ix = lambda f: (lambda i, j: f(i, j, 0))
        sem = ("parallel", "parallel")
        scratch = []
    else:
        grid = (M // tm, N // tn, nk)
        ix = lambda f: f
        sem = ("parallel", "parallel", "arbitrary")
        scratch = [pltpu.VMEM((tm, tn), F32)]
    def a_spec(r):
        return pl.BlockSpec((tm, tk), ix(lambda i, j, k: (i, k * u + r)))

    def w_spec(r):
        return pl.BlockSpec((None, tk, tn), ix(lambda i, j, k: wmap(k * u + r, j)))

    in_specs = [a_spec(r) for r in range(u)] + [w_spec(r) for r in range(u)]
    args = [a] * u + [w3] * u
    if res is not None:
        in_specs.append(pl.BlockSpec((tm, tn), ix(lambda i, j, k: (i, j))))
        args.append(res)
    return _pcall(
        body, name=name, grid=grid, in_specs=in_specs,
        out_specs=pl.BlockSpec((tm, tn), ix(lambda i, j, k: (i, j))),
        out_shape=jax.ShapeDtypeStruct((M, N), F32), scratch_shapes=scratch, compiler_params=_params(sem),
    )(*args)


def _mm_nn_shard(a, w3, s, tm, tn, name, prev=None):
    M, K = a.shape
    S, _, cw = w3.shape
    bps = cw // tn

    def body(s_ref, a_ref, w_ref, *rest):
        o_ref = rest[-1]
        o_ref[...] = jnp.dot(a_ref[...], w_ref[...], preferred_element_type=F32)

    in_specs = [pl.BlockSpec((tm, K), lambda i, j, sr: (i, 0)),
                pl.BlockSpec((None, K, tn), lambda i, j, sr: (sr[0], 0, j))]
    args = [s, a, w3]
    alias = {}
    if prev is not None:
        in_specs.append(pl.BlockSpec(memory_space=pl.ANY))
        args.append(prev)
        alias = {3: 0}
    return _pcall(
        body, name=name,
        grid_spec=pltpu.PrefetchScalarGridSpec(
            num_scalar_prefetch=1, grid=(M // tm, bps), in_specs=in_specs,
            out_specs=pl.BlockSpec((tm, tn), lambda i, j, sr: (i, sr[0] * bps + j))),
        out_shape=jax.ShapeDtypeStruct((M, S * cw), F32), input_output_aliases=alias,
        compiler_params=_params(("parallel", "parallel")),
    )(*args)


def _mm_nt(dy3, w3, wmap, K, N, tm, tk, tn, name, per_step=1):
    M = dy3.shape[1]
    bps = dy3.shape[2] // tn
    u = per_step
    grid = (M // tm, K // tk, N // (tn * u))
    nn = grid[2]

    def body(*refs):
        o_ref = refs[-1]
        p = None
        for r in range(u):
            d = lax.dot_general(refs[r][...], refs[u + r][...], (((1,), (1,)), ((), ())), preferred_element_type=F32)
            p = d if p is None else p + d
        if nn == 1:
            o_ref[...] = p
            return
        n = pl.program_id(2)

        @pl.when(n == 0)
        def _():
            o_ref[...] = p

        @pl.when(n > 0)
        def _():
            o_ref[...] += p

    def dy_spec(r):
        return pl.BlockSpec((None, tm, tn), lambda i, j, n: ((n * u + r) // bps, i, (n * u + r) % bps))

    def w_spec(r):
        return pl.BlockSpec((None, tk, tn), lambda i, j, n: wmap(j, n * u + r))

    return _pcall(
        body, name=name, grid=grid,
        in_specs=[dy_spec(r) for r in range(u)] + [w_spec(r) for r in range(u)],
        out_specs=pl.BlockSpec((tm, tk), lambda i, j, n: (i, j)),
        out_shape=jax.ShapeDtypeStruct((M, K), F32),
        compiler_params=_params(("parallel", "parallel", "arbitrary")),
    )(*([dy3] * u), *([w3] * u))


def _mm_tn(x, dy3, shape4, wmap, K, N, tk, tn, name, tm=None):
    M = x.shape[0]
    tm = M if tm is None else tm
    nm = M // tm
    bps = dy3.shape[2] // tn

    def body(*refs):
        x_ref, dy_ref = refs[:2]
        p = lax.dot_general(x_ref[...], dy_ref[...], (((0,), (0,)), ((), ())), preferred_element_type=F32)
        if nm == 1:
            o_ref = refs[-1]
            o_ref[...] = p.astype(o_ref.dtype)
            return
        o_ref, acc = refs[-2:]
        m = pl.program_id(2)

        @pl.when(m == 0)
        def _():
            acc[...] = p

        @pl.when(m > 0)
        def _():
            acc[...] += p

        @pl.when(m == nm - 1)
        def _():
            o_ref[...] = acc[...].astype(o_ref.dtype)

    def omap(i, j, m):
        s, rb, cb = wmap(i, j)
        return (s, 0, rb, cb)

    return _pcall(
        body, name=name, grid=(K // tk, N // tn, nm),
        in_specs=[pl.BlockSpec((tm, tk), lambda i, j, m: (m, i)),
                  pl.BlockSpec((None, tm, tn), lambda i, j, m: (j // bps, m, j % bps))],
        out_specs=pl.BlockSpec((None, None, tk, tn), omap),
        out_shape=jax.ShapeDtypeStruct(shape4, BF16),
        scratch_shapes=[] if nm == 1 else [pltpu.VMEM((tk, tn), F32)],
        compiler_params=_params(("parallel", "parallel", "arbitrary")),
    )(x, dy3)


def _row_call(fn, rows, vecs, outs, n_acc, name, t_rows=256, sub=16, per_trip=4):
    T = rows[0][0].shape[0]
    t_rows = min(t_rows, T)
    nsub = t_rows // sub
    n_r, n_v, n_o = len(rows), len(vecs), len(outs)
    width = rows[0][2]

    def body(*refs):
        r_refs = refs[:n_r]
        v_refs = refs[n_r:n_r + n_v]
        o_refs = refs[n_r + n_v:n_r + n_v + n_o]
        a_refs = refs[n_r + n_v + n_o:]

        @pl.when(pl.program_id(0) == 0)
        def _():
            for a in a_refs:
                a[...] = jnp.zeros_like(a)

        vv = [v[...] for v in v_refs]

        def step(i, carry):
            done = []
            for u in range(per_trip):
                sl = pl.ds(pl.multiple_of((i * per_trip + u) * sub, sub), sub)
                done.append((sl,) + tuple(fn([r[sl, :] for r in r_refs], vv)))
            for sl, o_vals, a_vals in done:
                for o, val in zip(o_refs, o_vals):
                    o[sl, :] = val.astype(o.dtype)
            for a_i, a in enumerate(a_refs):
                tot = None
                for _, _, a_vals in done:
                    part = a_vals[a_i].reshape(sub // SUBLANES, SUBLANES, a_vals[a_i].shape[-1]).sum(axis=0)
                    tot = part if tot is None else tot + part
                a[...] += tot
            return carry

        lax.fori_loop(0, nsub // per_trip, step, 0)

    in_specs = [pl.BlockSpec((t_rows, w), functools.partial(lambda i, cb: (i, cb), cb=cb)) for _, cb, w in rows]
    in_specs += [pl.BlockSpec(v.shape, lambda i: (0, 0)) for v in vecs]
    out_specs = [pl.BlockSpec((t_rows, w), lambda i: (i, 0)) for w, _ in outs]
    out_specs += [pl.BlockSpec((SUBLANES, width), lambda i: (0, 0)) for _ in range(n_acc)]
    out_shape = [jax.ShapeDtypeStruct((T, w), dt) for w, dt in outs]
    out_shape += [jax.ShapeDtypeStruct((SUBLANES, width), F32) for _ in range(n_acc)]
    return _pcall(
        body, name=name, grid=(T // t_rows,), in_specs=in_specs, out_specs=out_specs, out_shape=out_shape,
        compiler_params=_params(("arbitrary",)),
    )(*[r[0] for r in rows], *vecs)


def _rms_fwd_fn(rv, vv):
    h, = rv
    w, = vv
    r = lax.rsqrt(jnp.mean(h * h, axis=-1, keepdims=True) + EPS)
    return [h * r * w], []


def _rms_bwd_fn(rv, vv):
    h, dxn, dh_in = rv
    w, = vv
    d = h.shape[-1]
    r = lax.rsqrt(jnp.mean(h * h, axis=-1, keepdims=True) + EPS)
    gy = dxn * w
    dh = r * gy - h * ((r * r * r) * (1.0 / d) * jnp.sum(gy * h, axis=-1, keepdims=True))
    return [dh_in + dh] * 2, [dxn * h * r]


def _final_fn(rv, vv):
    h, tgt = rv
    w, = vv
    d = h.shape[-1]
    r = lax.rsqrt(jnp.mean(h * h, axis=-1, keepdims=True) + EPS)
    hn = h * r
    e = hn * w - tgt
    dy = e * (1.0 / d)
    gy = dy * w
    dh = r * gy - h * ((r * r * r) * (1.0 / d) * jnp.sum(gy * h, axis=-1, keepdims=True))
    return [dh] * 2, [e * e, dy * hn]


def _onorm_fwd_fn(rv, vv):
    o, g = rv
    gain, = vv
    r = lax.rsqrt(jnp.mean(o * o, axis=-1, keepdims=True) + EPS)
    return [o * r * gain * (g * _sigmoid(g))], []


def _onorm_bwd_fn(rv, vv):
    o, g, don = rv
    gain, = vv
    d = o.shape[-1]
    r = lax.rsqrt(jnp.mean(o * o, axis=-1, keepdims=True) + EPS)
    sg = _sigmoid(g)
    sl = g * sg
    n = o * r
    dg = don * n * gain * (sg * (1.0 + g * (1.0 - sg)))
    gy = don * sl * gain
    do = r * gy - o * ((r * r * r) * (1.0 / d) * jnp.sum(gy * o, axis=-1, keepdims=True))
    return [do, dg], [don * sl * n]


HALO = SUBLANES


def _col_call(fn, cols, vecs, outs, n_acc, name, before, after, tc=LANES, chunk=256):
    T = cols[0][0].shape[0]
    chunk = min(chunk, T)
    nch = T // chunk
    ncol = outs[0][1] // tc
    n_c, n_v, n_o = len(cols), len(vecs), len(outs)
    hb = HALO if before else 0
    rw = chunk + hb + (HALO if after else 0)

    def body(*refs):
        c_refs = refs[:n_c]
        v_refs = refs[n_c:n_c + n_v]
        o_refs = refs[n_c + n_v:n_c + n_v + n_o]
        a_refs = refs[n_c + n_v + n_o:]
        vv = [v[...] for v in v_refs]
        wrow = lax.broadcasted_iota(jnp.int32, (rw, tc), 0)
        inside = (wrow >= hb) & (wrow < hb + chunk)

        def step(i, carry):
            r0 = pl.multiple_of(i * chunk, chunk)
            wins = []
            for ref in c_refs:
                parts = []
                if before:
                    pb = ref[pl.ds(pl.multiple_of(jnp.maximum(r0 - HALO, 0), HALO), HALO), :]
                    parts.append(jnp.where(i > 0, pb, 0.0))
                parts.append(ref[pl.ds(r0, chunk), :])
                if after:
                    pa = ref[pl.ds(pl.multiple_of(jnp.minimum(r0 + chunk, T - HALO), HALO), HALO), :]
                    parts.append(jnp.where(i < nch - 1, pa, 0.0))
                wins.append(jnp.concatenate(parts, axis=0) if len(parts) > 1 else parts[0])
            o_vals, a_vals = fn(wins, vv, inside)
            p = 0
            for o, (nseg, _, _) in zip(o_refs, outs):
                for s in range(nseg):
                    o[s, pl.ds(r0, chunk), :] = o_vals[p][hb:hb + chunk].astype(o.dtype)
                    p += 1
            return tuple(c + a for c, a in zip(carry, a_vals))

        taps = [v.shape[0] for v, _ in vecs][:n_acc]
        init = tuple(jnp.zeros((1, tc), F32) for k in taps for _ in range(k))
        sums = lax.fori_loop(0, nch, step, init)
        arow = lax.broadcasted_iota(jnp.int32, (SUBLANES, tc), 0)
        p = 0
        for a, k in zip(a_refs, taps):
            acc = jnp.zeros((SUBLANES, tc), F32)
            for t in range(k):
                acc = jnp.where(arow == t, sums[p], acc)
                p += 1
            a[...] = acc

    in_specs = [pl.BlockSpec((T, tc), functools.partial(lambda j, off: (0, off + j), off=off)) for _, off in cols]
    in_specs += [pl.BlockSpec((v.shape[0], tc), functools.partial(lambda j, off: (0, off + j), off=off))
                 for v, off in vecs]
    out_specs = [pl.BlockSpec((nseg, T, tc), lambda j: (0, 0, j)) for nseg, _, _ in outs]
    out_specs += [pl.BlockSpec((SUBLANES, tc), lambda j: (0, j)) for _ in range(n_acc)]
    out_shape = [jax.ShapeDtypeStruct((nseg, T, w), dt) for nseg, w, dt in outs]
    out_shape += [jax.ShapeDtypeStruct((SUBLANES, ncol * tc), F32) for _ in range(n_acc)]
    return _pcall(
        body, name=name, grid=(ncol,), in_specs=in_specs, out_specs=out_specs, out_shape=out_shape,
        compiler_params=_params(("parallel",)),
    )(*[c[0] for c in cols], *[v[0] for v in vecs])


def _down(x, k):
    return x if k == 0 else pltpu.roll(x, k, 0)


def _up(x, k):
    return x if k == 0 else pltpu.roll(x, x.shape[0] - k, 0)


def _lags(x):
    return _down(x, 2), _down(x, 1), x


def _conv(lags, w):
    return w[0:1] * lags[0] + w[1:2] * lags[1] + w[2:3] * lags[2]


def _conv_t(d, w):
    return w[2:3] * d + w[1:2] * _up(d, 1) + w[0:1] * _up(d, 2)


def _tap_sums(d, lags, inside):
    dm = jnp.where(inside, d, 0.0)
    return [jnp.sum(dm * lag, axis=0, keepdims=True) for lag in lags]


def _glu_fwd_fn(wins, vv, inside):
    xg, xv = wins
    wg, wv = vv
    ug = _conv(_lags(xg), wg)
    uv = _conv(_lags(xv), wv)
    return [ug * _sigmoid(ug) * uv], []


def _glu_bwd_fn(wins, vv, inside):
    xg, xv, da = wins
    wg, wv = vv
    lg, lv = _lags(xg), _lags(xv)
    ug = _conv(lg, wg)
    uv = _conv(lv, wv)
    sg = _sigmoid(ug)
    dug = da * uv * (sg * (1.0 + ug * (1.0 - sg)))
    duv = da * (ug * sg)
    return [_conv_t(dug, wg), _conv_t(duv, wv)], _tap_sums(dug, lg, inside) + _tap_sums(duv, lv, inside)


def _sc_fwd_fn(wins, vv, inside):
    gb, gc, hh = wins
    w, = vv
    return [gb * _conv(_lags(gc * hh), w)], []


def _sc_bwd_fn(wins, vv, inside):
    gb, gc, hh, dy = wins
    w, = vv
    lz = _lags(gc * hh)
    dcv = dy * gb
    dz = _conv_t(dcv, w)
    return [dy * _conv(lz, w), dz * hh, dz * gc], _tap_sums(dcv, lz, inside)


def _gates(qr, fr, lb):
    sg = _sigmoid(fr)
    f = lb + (1.0 - lb) * sg
    sq = _sigmoid(qr)
    q = qr * sq * (HEAD ** -0.5)
    return q, 1.0 - f, jnp.log(f), f, sg, sq


def _boundary_rows(b, g, row):
    c = b.shape[0]
    if 2 * g >= SUBLANES:
        x = b.reshape(c // (2 * g), 2 * g, LANES)
        return jnp.broadcast_to(x[:, g - 1:g, :], x.shape).reshape(c, LANES)
    x = b.reshape(c // SUBLANES, SUBLANES, LANES)
    lo = jnp.broadcast_to(x[:, 1:2, :], x.shape).reshape(c, LANES)
    hi = jnp.broadcast_to(x[:, 5:6, :], x.shape).reshape(c, LANES)
    return jnp.where((row & 4) == 0, lo, hi)


def _chunk_decays(gl, f, row):
    c = gl.shape[0]
    b = gl
    d = 1
    while d < c:
        b = b + jnp.where(row >= d, pltpu.roll(b, d, 0), 0.0)
        d *= 2
    eq, ek = [], []
    g = c // 2
    while g >= 2:
        right = (row & g) != 0
        m = _boundary_rows(b, g, row)
        z = jnp.exp(jnp.where(right, b - m, m - b))
        eq.append(jnp.where(right, z, 0.0))
        ek.append(jnp.where(right, 0.0, z))
        g //= 2
    odd = (row & 1) != 0
    eq.append(jnp.where(odd, f, 0.0))
    ek.append(jnp.where(odd, 0.0, 1.0))
    return b, eq, ek


def _intra(q, k, eq, ek, tt, ss):
    c = q.shape[0]
    qs, ks = [], []
    a = jnp.where(tt == ss, jnp.sum(q * k, axis=1, keepdims=True), 0.0)
    g = c // 2
    for e_q, e_k in zip(eq, ek):
        qg = (q * e_q).astype(BF16)
        kg = (k * e_k).astype(BF16)
        p = lax.dot_general(qg, kg, (((1,), (1,)), ((), ())), preferred_element_type=F32)
        a = a + (p if 2 * g >= c else jnp.where((tt ^ ss) < 2 * g, p, 0.0))
        qs.append(qg)
        ks.append(kg)
        g //= 2
    return a, qs, ks


def _hgrn_fwd(proj, lb, d_model):
    T = proj.shape[0]
    H = d_model // HEAD
    nch = T // CHUNK

    def body(q_ref, f_ref, v_ref, lb_ref, o_ref, s_ref):
        lbv = lb_ref[...]
        row = lax.broadcasted_iota(jnp.int32, (CHUNK, HEAD), 0)
        tt = lax.broadcasted_iota(jnp.int32, (CHUNK, CHUNK), 0)
        ss = lax.broadcasted_iota(jnp.int32, (CHUNK, CHUNK), 1)

        def step(i, st):
            sl = pl.ds(pl.multiple_of(i * CHUNK, CHUNK), CHUNK)
            q, k, gl, f, _, _ = _gates(q_ref[sl, :], f_ref[sl, :], lbv)
            v = v_ref[sl, :].astype(BF16)
            b, eq, ek = _chunk_decays(gl, f, row)
            a, _, _ = _intra(q, k, eq, ek, tt, ss)
            bl = b[CHUNK - 1:CHUNK, :]
            q0 = (q * jnp.exp(b)).astype(BF16)
            kh = (k * jnp.exp(bl - b)).astype(BF16)
            s_ref[i] = st
            o = jnp.dot(a.astype(BF16), v, preferred_element_type=F32)
            o = o + lax.dot_general(q0, st.astype(BF16), (((1,), (1,)), ((), ())), preferred_element_type=F32)
            o_ref[sl, :] = o
            return jnp.exp(bl) * st + lax.dot_general(v, kh, (((0,), (0,)), ((), ())), preferred_element_type=F32)

        per = 4 if nch % 4 == 0 else 2

        def trip(i, st):
            for u in range(per):
                st = step(per * i + u, st)
            return st

        lax.fori_loop(0, nch // per, trip, jnp.zeros((HEAD, HEAD), F32))

    col = lambda off: pl.BlockSpec((T, HEAD), functools.partial(lambda h, off: (0, off + h), off=off))
    return _pcall(
        body, name="hgrn_fwd", grid=(H,),
        in_specs=[col(0), col(H), col(2 * H), pl.BlockSpec((1, HEAD), lambda h: (0, h))],
        out_specs=[pl.BlockSpec((T, HEAD), lambda h: (0, h)),
                   pl.BlockSpec((None, nch, HEAD, HEAD), lambda h: (h, 0, 0, 0))],
        out_shape=[jax.ShapeDtypeStruct((T, d_model), F32), jax.ShapeDtypeStruct((H, nch, HEAD, HEAD), F32)],
        compiler_params=_params(("parallel",)),
    )(proj, proj, proj, lb)


def _hgrn_bwd(proj, lb, states, do, dgate, d_model):
    T = proj.shape[0]
    H = d_model // HEAD
    nch = T // CHUNK

    def body(q_ref, f_ref, v_ref, lb_ref, s_ref, do_ref, dg_ref, dp_ref, dlb_ref):
        dq_ref, df_ref, dv_ref = dp_ref.at[0], dp_ref.at[1], dp_ref.at[2]
        dp_ref[3] = dg_ref[...]
        lbv = lb_ref[...]
        row = lax.broadcasted_iota(jnp.int32, (CHUNK, HEAD), 0)
        tt = lax.broadcasted_iota(jnp.int32, (CHUNK, CHUNK), 0)
        ss = lax.broadcasted_iota(jnp.int32, (CHUNK, CHUNK), 1)
        last = row == CHUNK - 1
        nt = (((1,), (1,)), ((), ()))
        tn = (((0,), (0,)), ((), ()))

        def step(j, carry):
            dst, dlb = carry
            i = nch - 1 - j
            sl = pl.ds(pl.multiple_of(i * CHUNK, CHUNK), CHUNK)
            qr = q_ref[sl, :]
            q, k, gl, f, sg, sq = _gates(qr, f_ref[sl, :], lbv)
            v = v_ref[sl, :].astype(BF16)
            d_o = do_ref[sl, :].astype(BF16)
            st = s_ref[i]
            st16 = st.astype(BF16)
            dst16 = dst.astype(BF16)
            b, eq, ek = _chunk_decays(gl, f, row)
            a, qs, ks = _intra(q, k, eq, ek, tt, ss)
            bl = b[CHUNK - 1:CHUNK, :]
            e0 = jnp.exp(b)
            eh = jnp.exp(bl - b)
            ebl = jnp.exp(bl)
            q0 = q * e0
            kh = k * eh
            q016 = q0.astype(BF16)
            kh16 = kh.astype(BF16)
            dv = lax.dot_general(a.astype(BF16), d_o, tn, preferred_element_type=F32)
            dv = dv + lax.dot_general(kh16, dst16, nt, preferred_element_type=F32)
            dv_ref[sl, :] = dv.astype(dv_ref.dtype)
            da = lax.dot_general(d_o, v, nt, preferred_element_type=F32)
            da = jnp.where(tt >= ss, da, 0.0)
            dd = jnp.sum(jnp.where(tt == ss, da, 0.0), axis=1, keepdims=True)
            dq0 = jnp.dot(d_o, st16, preferred_element_type=F32)
            dkh = jnp.dot(v, dst16, preferred_element_type=F32)
            dq = dq0 * e0 + dd * k
            dk = dkh * eh + dd * q
            db = dq0 * q016.astype(F32) - dkh * kh16.astype(F32)
            g = CHUNK // 2
            for e_q, e_k, qg, kg in zip(eq, ek, qs, ks):
                dag = (da if 2 * g >= CHUNK else jnp.where((tt ^ ss) < 2 * g, da, 0.0)).astype(BF16)
                dqg = jnp.dot(dag, kg, preferred_element_type=F32)
                dkg = lax.dot_general(dag, qg, tn, preferred_element_type=F32)
                dq = dq + dqg * e_q
                dk = dk + dkg * e_k
                db = db + (dqg * qg.astype(F32) - dkg * kg.astype(F32))
                g //= 2
            dbl = jnp.sum(dkh * kh16.astype(F32), axis=0, keepdims=True) + ebl * jnp.sum(dst * st, axis=0, keepdims=True)
            db = db + jnp.where(last, dbl, 0.0)
            d = 1
            while d < CHUNK:
                db = db + jnp.where(row < CHUNK - d, pltpu.roll(db, CHUNK - d, 0), 0.0)
                d *= 2
            dfg = db / f - dk
            df_ref[sl, :] = (dfg * (1.0 - lbv) * sg * (1.0 - sg)).astype(df_ref.dtype)
            dq_ref[sl, :] = (dq * (HEAD ** -0.5) * (sq * (1.0 + qr * (1.0 - sq)))).astype(dq_ref.dtype)
            dlb = dlb + jnp.sum(dfg * (1.0 - sg), axis=0, keepdims=True)
            dst = ebl * dst + lax.dot_general(d_o, q016, tn, preferred_element_type=F32)
            return dst, dlb

        _, dlb = lax.fori_loop(0, nch // 2, lambda j, cr: step(2 * j + 1, step(2 * j, cr)),
                               (jnp.zeros((HEAD, HEAD), F32), jnp.zeros((1, HEAD), F32)))
        arow = lax.broadcasted_iota(jnp.int32, (SUBLANES, HEAD), 0)
        dlb_ref[...] = jnp.where(arow == 0, dlb, 0.0)

    col = lambda off: pl.BlockSpec((T, HEAD), functools.partial(lambda h, off: (0, off + h), off=off))
    return _pcall(
        body, name="hgrn_bwd", grid=(H,),
        in_specs=[col(0), col(H), col(2 * H), pl.BlockSpec((1, HEAD), lambda h: (0, h)),
                  pl.BlockSpec((None, nch, HEAD, HEAD), lambda h: (h, 0, 0, 0)), col(0), col(0)],
        out_specs=[pl.BlockSpec((4, T, HEAD), lambda h: (0, 0, h)), pl.BlockSpec((SUBLANES, HEAD), lambda h: (0, h))],
        out_shape=[jax.ShapeDtypeStruct((4, T, d_model), BF16), jax.ShapeDtypeStruct((SUBLANES, d_model), F32)],
        compiler_params=_params(("parallel",)),
    )(proj, proj, proj, lb, states, do, dgate)


def _lb_softmax(table):
    n, f = table.shape

    def body(t_ref, p_ref):
        t = t_ref[...]
        e = jnp.exp(t - jnp.max(t, axis=0, keepdims=True))
        p_ref[...] = e / jnp.sum(e, axis=0, keepdims=True)

    padded = jnp.pad(table, ((0, SUBLANES - n), (0, 0)), constant_values=-jnp.inf)
    return _pcall(body, name="lb_softmax", out_shape=jax.ShapeDtypeStruct((SUBLANES, f), F32))(padded)


def _to_bf16(w, layer, nlayers):
    R = w.shape[0] // nlayers
    C = w.shape[1]
    tr = _pick(R, (256, 128, 64, 32, 16))
    nb = R // tr

    def body(w_ref, o_ref):
        o_ref[...] = w_ref[...].astype(o_ref.dtype)

    return _pcall(
        body, name="to_bf16", grid=(nb,), in_specs=[pl.BlockSpec((tr, C), lambda i: (layer * nb + i, 0))],
        out_specs=pl.BlockSpec((tr, C), lambda i: (i, 0)), out_shape=jax.ShapeDtypeStruct((R, C), BF16),
        compiler_params=_params(("parallel",)),
    )(w)


def _adamw_math(w, g, m, v):
    m = ADAM_B1 * m + (1.0 - ADAM_B1) * g
    v = ADAM_B2 * v + (1.0 - ADAM_B2) * (g * g)
    m_hat = m / (1.0 - ADAM_B1 ** ADAM_STEP)
    v_hat = v / (1.0 - ADAM_B2 ** ADAM_STEP)
    delta = -ADAM_LR * (m_hat / (jnp.sqrt(v_hat) + ADAM_EPS) + ADAM_WD * w)
    return delta, m, v


def _adamw(w, g, m, v, name):
    R, C = w.shape
    tr = _pick(R, (128, 64, 32, 16, 8))

    def body(w_ref, g_ref, m_ref, v_ref, d_ref, nm_ref, nv_ref):
        d, nm, nv = _adamw_math(w_ref[...], g_ref[...], m_ref[...], v_ref[...])
        d_ref[...] = d
        nm_ref[...] = nm
        nv_ref[...] = nv

    spec = pl.BlockSpec((tr, C), lambda i: (i, 0))
    return _pcall(
        body, name=name, grid=(R // tr,), in_specs=[spec] * 4, out_specs=[spec] * 3,
        out_shape=[jax.ShapeDtypeStruct((R, C), F32)] * 3, compiler_params=_params(("parallel",)),
    )(w, g, m, v)


def _adamw_halves(w, m, v, g_mine, g_recv, c, name, layer=0, prev=None):
    C = w.shape[1]
    rh = g_mine.shape[0]
    tr = _pick(rh, (128, 64, 32, 16, 8))
    nb = rh // tr
    r0 = layer * 2 * nb

    def body(c_ref, w_ref, m_ref, v_ref, gm_ref, gr_ref, *rest):
        g_ref, d_ref, nm_ref, nv_ref = rest[-4:]
        g = jnp.where(pl.program_id(0) == c_ref[0], gm_ref[...], gr_ref[...])
        d, nm, nv = _adamw_math(w_ref[...], g, m_ref[...], v_ref[...])
        g_ref[...] = g
        d_ref[...] = d
        nm_ref[...] = nm
        nv_ref[...] = nv

    full = pl.BlockSpec((tr, C), lambda h, i, cr: (r0 + h * nb + i, 0))
    mine = pl.BlockSpec((tr, C), lambda h, i, cr: (jnp.where(h == cr[0], i, 0), 0))
    recv = pl.BlockSpec((tr, C), lambda h, i, cr: (jnp.where(h == cr[0], 0, i), 0))
    in_specs = [full, full, full, mine, recv]
    args = [c, w, m, v, g_mine, g_recv]
    alias = {}
    if prev is not None:
        in_specs += [pl.BlockSpec(memory_space=pl.ANY)] * 4
        args += list(prev)
        alias = {6 + k: k for k in range(4)}
    return _pcall(
        body, name=name,
        grid_spec=pltpu.PrefetchScalarGridSpec(
            num_scalar_prefetch=1, grid=(2, nb), in_specs=in_specs, out_specs=[full] * 4),
        out_shape=[jax.ShapeDtypeStruct(w.shape, F32)] * 4, input_output_aliases=alias,
        compiler_params=_params(("parallel", "parallel")),
    )(*args)


def _lb_table_grad(p8, dlb, n):
    f = p8.shape[1]

    def body(p_ref, d_ref, o_ref):
        p = p_ref[...]
        d = d_ref[...]
        p0 = p[0:1, :]
        first = lax.broadcasted_iota(jnp.int32, p.shape, 0) == 0
        o_ref[...] = p * (jnp.where(first, d, 0.0) - d * p0)

    return _pcall(body, name="lb_table_grad", out_shape=jax.ShapeDtypeStruct((SUBLANES, f), F32))(p8, dlb)[:n]


def _place():
    x, y, c = lax.axis_index("x"), lax.axis_index("y"), lax.axis_index("c")
    chips = [(1 - x, y), (x, 1 - y), (1 - x, 1 - y)]
    return x, y, c, chips


HBM_SPEC = pl.BlockSpec(memory_space=pltpu.HBM)


def _gather_weights(big, small):
    nb, ns = len(big), len(small)
    n = nb + ns

    def body(*refs):
        ins, outs = refs[:n], refs[n:2 * n]
        send_sems, recv_sems, own_send, own_recv = refs[2 * n:]
        x, y, c, chips = _place()
        me = 2 * x + y
        sib = (x, y, 1 - c)
        own = [pltpu.make_async_remote_copy(
            src_ref=ins[t], dst_ref=outs[t].at[me], send_sem=own_send.at[t], recv_sem=own_recv.at[t],
            device_id=sib, device_id_type=MESH) for t in range(n)]
        for cp in own:
            cp.start()

        def half(t, h):
            rh = big[t].shape[0] // 2
            return pl.ds(pl.multiple_of(h * rh, rh), rh)

        sends = []
        for t in range(n):
            for j, chip in enumerate(chips):
                k = 6 * t + j
                if t < nb:
                    src, dst = ins[t].at[half(t, c)], outs[t].at[me, half(t, c)]
                else:
                    src, dst = ins[t], outs[t].at[me]
                sends.append(pltpu.make_async_remote_copy(
                    src_ref=src, dst_ref=dst, send_sem=send_sems.at[k], recv_sem=recv_sems.at[k],
                    device_id=(*chip, c), device_id_type=MESH))
        for cp in sends:
            cp.start()
        passed = []
        for t in range(n):
            for j, (cx, cy) in enumerate(chips):
                k = 6 * t + j
                s = 2 * cx + cy
                if t < nb:
                    landed = outs[t].at[s, half(t, c)]
                    pltpu.make_async_remote_copy(
                        src_ref=landed, dst_ref=landed, send_sem=send_sems.at[k], recv_sem=recv_sems.at[k],
                        device_id=sib, device_id_type=MESH).wait_recv()
                    fwd = pltpu.make_async_remote_copy(
                        src_ref=landed, dst_ref=landed, send_sem=send_sems.at[k + 3], recv_sem=recv_sems.at[k + 3],
                        device_id=sib, device_id_type=MESH)
                    fwd.start()
                    passed.append(fwd)
                else:
                    landed = outs[t].at[s]
                    pltpu.make_async_remote_copy(
                        src_ref=landed, dst_ref=landed, send_sem=send_sems.at[k], recv_sem=recv_sems.at[k],
                        device_id=sib, device_id_type=MESH).wait_recv()
        for t in range(nb):
            for j, (cx, cy) in enumerate(chips):
                k = 6 * t + j
                other = outs[t].at[2 * cx + cy, half(t, 1 - c)]
                pltpu.make_async_remote_copy(
                    src_ref=other, dst_ref=other, send_sem=send_sems.at[k + 3], recv_sem=recv_sems.at[k + 3],
                    device_id=sib, device_id_type=MESH).wait_recv()
        for cp in sends + passed:
            cp.wait_send()
        for cp in own:
            cp.wait()

    arrs = list(big) + list(small)
    return _pcall(
        body, name="gather_weights", in_specs=[HBM_SPEC] * n, out_specs=[HBM_SPEC] * n,
        out_shape=[jax.ShapeDtypeStruct((N_CHIPS,) + a.shape, a.dtype) for a in arrs],
        scratch_shapes=[pltpu.SemaphoreType.DMA((6 * n,)), pltpu.SemaphoreType.DMA((6 * n,)),
                        pltpu.SemaphoreType.DMA((n,)), pltpu.SemaphoreType.DMA((n,))],
    )(*arrs)


SEM_SPEC = pl.BlockSpec(memory_space=pltpu.SEMAPHORE)
DATAFLOW = pltpu.SideEffectType.DATAFLOW_SIDE_EFFECTING
COPIES_PER_SHARD = 4


def _shard_copies(ins, lands, send_sems, recv_sems, base=0):
    x, y, c, chips = _place()
    me = 2 * x + y
    cps = []
    for t in range(len(ins)):
        rh = ins[t].shape[0] // 2
        half = pl.ds(pl.multiple_of(c * rh, rh), rh)
        for j, chip in enumerate(chips):
            k = COPIES_PER_SHARD * (base + t) + j
            cps.append(pltpu.make_async_remote_copy(
                src_ref=ins[t].at[half], dst_ref=lands[t].at[me, half], send_sem=send_sems.at[k],
                recv_sem=recv_sems.at[k], device_id=(*chip, c), device_id_type=MESH))
        k = COPIES_PER_SHARD * (base + t) + 3
        cps.append(pltpu.make_async_remote_copy(
            src_ref=ins[t], dst_ref=lands[t].at[me], send_sem=send_sems.at[k], recv_sem=recv_sems.at[k],
            device_id=(x, y, 1 - c), device_id_type=MESH))
    return cps


def _gather_start(shards, thru, name):
    n = len(shards)
    nops = 2 * n + len(thru)

    def body(*refs):
        ins, lands = refs[:n], refs[n:2 * n]
        send_sems, recv_sems = refs[nops], refs[nops + 1]
        for cp in _shard_copies(ins, lands, send_sems, recv_sems):
            cp.start()

    lands = [pltpu.with_memory_space_constraint(lax.empty((N_CHIPS,) + s.shape, s.dtype), pltpu.HBM) for s in shards]
    ops = [pltpu.with_memory_space_constraint(s, pltpu.HBM) for s in shards] + lands + list(thru)
    nsem = COPIES_PER_SHARD * n
    res = _pcall(
        body, name=name, in_specs=[HBM_SPEC] * nops,
        out_specs=[SEM_SPEC, SEM_SPEC] + [HBM_SPEC] * nops,
        out_shape=[pltpu.SemaphoreType.DMA((nsem,)), pltpu.SemaphoreType.DMA((nsem,))]
        + [pltpu.HBM(o.shape, o.dtype) for o in ops],
        input_output_aliases={i: 2 + i for i in range(nops)},
        compiler_params=pltpu.CompilerParams(has_side_effects=DATAFLOW),
    )(*ops)
    return res[0], res[1], res[2:2 + n], res[2 + n:2 + 2 * n], list(res[2 + 2 * n:])


def _gather_wait(send_sems, recv_sems, shards, lands, after, name, base=0, which=None):
    n = len(shards)

    def body(*refs):
        ins, lnd = refs[:n], refs[n:2 * n]
        ssem, rsem = refs[2 * n], refs[2 * n + 1]
        for k, cp in enumerate(_shard_copies(ins, lnd, ssem, rsem, base)):
            if which is None or k % COPIES_PER_SHARD in which:
                cp.wait_send()
                cp.wait_recv()

    res = _pcall(
        body, name=name,
        in_specs=[HBM_SPEC] * (2 * n) + [SEM_SPEC, SEM_SPEC, pl.BlockSpec(memory_space=pl.ANY)],
        out_specs=[HBM_SPEC] * (2 * n),
        out_shape=[pltpu.HBM(o.shape, o.dtype) for o in list(shards) + list(lands)],
        input_output_aliases={i: i for i in range(2 * n)},
        compiler_params=pltpu.CompilerParams(has_side_effects=DATAFLOW),
    )(*shards, *lands, send_sems, recv_sems, after)
    return res[:n], res[n:]


SIBLING_PAIR = 1


def _sibling_handshake():
    x, y, c, _ = _place()
    barrier = pltpu.get_barrier_semaphore()
    pl.semaphore_signal(barrier, inc=1, device_id=(x, y, 1 - c), device_id_type=MESH)
    pl.semaphore_wait(barrier, 1)


def _forward_copies(land, send_sems, recv_sems, which=(0, 1, 2)):
    x, y, c, chips = _place()
    rh = land.shape[1] // 2
    return [pltpu.make_async_remote_copy(
        src_ref=land.at[2 * cx + cy, pl.ds(pl.multiple_of(c * rh, rh), rh)],
        dst_ref=land.at[2 * cx + cy, pl.ds(pl.multiple_of(c * rh, rh), rh)],
        send_sem=send_sems.at[j], recv_sem=recv_sems.at[j], device_id=(x, y, 1 - c), device_id_type=MESH)
        for j, (cx, cy) in enumerate(chips) if j in which]


def _forward_start(land, thru, name, which=(0, 1, 2)):
    def body(land_ref, thru_ref, send_sems, recv_sems, out_ref, thru_out):
        _sibling_handshake()
        for cp in _forward_copies(land_ref, send_sems, recv_sems, which):
            cp.start()

    return _pcall(
        body, name=name, in_specs=[HBM_SPEC, HBM_SPEC], out_specs=[SEM_SPEC, SEM_SPEC, HBM_SPEC, HBM_SPEC],
        out_shape=[pltpu.SemaphoreType.DMA((3,)), pltpu.SemaphoreType.DMA((3,)), pltpu.HBM(land.shape, land.dtype),
                   pltpu.HBM(thru.shape, thru.dtype)],
        input_output_aliases={0: 2, 1: 3},
        compiler_params=pltpu.CompilerParams(has_side_effects=DATAFLOW, collective_id=SIBLING_PAIR),
    )(land, thru)


def _forward_wait(send_sems, recv_sems, land, after, name, which=(0, 1, 2)):
    def body(land_ref, ssem, rsem, after_ref, out_ref):
        for cp in _forward_copies(land_ref, ssem, rsem, which):
            cp.wait_send()
            cp.wait_recv()

    return _pcall(
        body, name=name, in_specs=[HBM_SPEC, SEM_SPEC, SEM_SPEC, pl.BlockSpec(memory_space=pl.ANY)],
        out_specs=HBM_SPEC, out_shape=pltpu.HBM(land.shape, land.dtype), input_output_aliases={0: 0},
        compiler_params=pltpu.CompilerParams(has_side_effects=DATAFLOW),
    )(land, send_sems, recv_sems, after)


def _sibling_copies(ins, lands, send_sems, recv_sems, other_half):
    x, y, c, _ = _place()
    return [pltpu.make_async_remote_copy(
        src_ref=ins[t].at[:, 1 - c] if other_half else ins[t], dst_ref=lands[t], send_sem=send_sems.at[t],
        recv_sem=recv_sems.at[t], device_id=(x, y, 1 - c), device_id_type=MESH) for t in range(len(ins))]


def _sibling_start(srcs, other_half, thru, name):
    n = len(srcs)
    nthru = 0 if thru is None else 1

    def body(*refs):
        ins, lands = refs[:n], refs[n:2 * n]
        send_sems, recv_sems = refs[2 * n + nthru], refs[2 * n + nthru + 1]
        _sibling_handshake()
        for cp in _sibling_copies(ins, lands, send_sems, recv_sems, other_half):
            cp.start()

    shapes = [(s.shape[0],) + s.shape[2:] if other_half else s.shape for s in srcs]
    lands = [pltpu.with_memory_space_constraint(lax.empty(sh, s.dtype), pltpu.HBM) for sh, s in zip(shapes, srcs)]
    ops = [pltpu.with_memory_space_constraint(s, pltpu.HBM) for s in srcs] + lands + ([] if thru is None else [thru])
    res = _pcall(
        body, name=name, in_specs=[HBM_SPEC] * len(ops),
        out_specs=[SEM_SPEC, SEM_SPEC] + [HBM_SPEC] * len(ops),
        out_shape=[pltpu.SemaphoreType.DMA((n,)), pltpu.SemaphoreType.DMA((n,))]
        + [pltpu.HBM(o.shape, o.dtype) for o in ops],
        input_output_aliases={i: 2 + i for i in range(len(ops))},
        compiler_params=pltpu.CompilerParams(has_side_effects=DATAFLOW, collective_id=SIBLING_PAIR),
    )(*ops)
    return res[0], res[1], res[2:2 + n], res[2 + n:2 + 2 * n], (None if thru is None else res[2 + 2 * n])


def _sibling_wait(send_sems, recv_sems, srcs, lands, other_half, after, name):
    n = len(srcs)

    def body(*refs):
        ins, lnd = refs[:n], refs[n:2 * n]
        ssem, rsem = refs[2 * n], refs[2 * n + 1]
        for cp in _sibling_copies(ins, lnd, ssem, rsem, other_half):
            cp.wait_send()
            cp.wait_recv()

    res = _pcall(
        body, name=name,
        in_specs=[HBM_SPEC] * (2 * n) + [SEM_SPEC, SEM_SPEC, pl.BlockSpec(memory_space=pl.ANY)],
        out_specs=[HBM_SPEC] * (2 * n),
        out_shape=[pltpu.HBM(o.shape, o.dtype) for o in list(srcs) + list(lands)],
        input_output_aliases={i: i for i in range(2 * n)},
        compiler_params=pltpu.CompilerParams(has_side_effects=DATAFLOW),
    )(*srcs, *lands, send_sems, recv_sems, after)
    return res[:n], res[n:]


def _chip_copies(ins, lands, send_sems, recv_sems):
    x, y, c, chips = _place()
    cps = []
    for t in range(len(ins)):
        for j, (cx, cy) in enumerate(chips):
            cps.append(pltpu.make_async_remote_copy(
                src_ref=ins[t].at[2 * cx + cy], dst_ref=lands[t].at[j],
                send_sem=send_sems.at[3 * t + j], recv_sem=recv_sems.at[3 * t + j],
                device_id=(cx, cy, c), device_id_type=MESH))
    return cps


def _chip_start(parts, thru, name):
    n = len(parts)

    def body(*refs):
        ins, lands = refs[:n], refs[n:2 * n]
        send_sems, recv_sems = refs[2 * n + 1], refs[2 * n + 2]
        for cp in _chip_copies(ins, lands, send_sems, recv_sems):
            cp.start()

    lands = [pltpu.with_memory_space_constraint(lax.empty((3,) + p.shape[1:], p.dtype), pltpu.HBM) for p in parts]
    ops = [pltpu.with_memory_space_constraint(p, pltpu.HBM) for p in parts] + lands + [thru]
    res = _pcall(
        body, name=name, in_specs=[HBM_SPEC] * (2 * n + 1),
        out_specs=[SEM_SPEC, SEM_SPEC] + [HBM_SPEC] * (2 * n + 1),
        out_shape=[pltpu.SemaphoreType.DMA((3 * n,)), pltpu.SemaphoreType.DMA((3 * n,))]
        + [pltpu.HBM(o.shape, o.dtype) for o in ops],
        input_output_aliases={i: 2 + i for i in range(2 * n + 1)},
        compiler_params=pltpu.CompilerParams(has_side_effects=DATAFLOW),
    )(*ops)
    return res[0], res[1], res[2:2 + n], res[2 + n:2 + 2 * n], res[2 + 2 * n]


def _chip_wait(send_sems, recv_sems, parts, lands, after, name):
    n = len(parts)

    def body(*refs):
        ins, lnd = refs[:n], refs[n:2 * n]
        ssem, rsem = refs[2 * n], refs[2 * n + 1]
        for cp in _chip_copies(ins, lnd, ssem, rsem):
            cp.wait_send()
            cp.wait_recv()

    res = _pcall(
        body, name=name,
        in_specs=[HBM_SPEC] * (2 * n) + [SEM_SPEC, SEM_SPEC, pl.BlockSpec(memory_space=pl.ANY)],
        out_specs=[HBM_SPEC] * (2 * n),
        out_shape=[pltpu.HBM(o.shape, o.dtype) for o in list(parts) + list(lands)],
        input_output_aliases={i: i for i in range(2 * n)},
        compiler_params=pltpu.CompilerParams(has_side_effects=DATAFLOW),
    )(*parts, *lands, send_sems, recv_sems, after)
    return res[:n], res[n:]


def _add_pair(grad, recv, c, name):
    s, _, rh, cc = grad.shape
    tr = _pick(rh, (256, 128, 64, 32, 16))

    def body(c_ref, g_ref, r_ref, o_ref):
        o_ref[...] = (g_ref[...].astype(F32) + r_ref[...].astype(F32)).astype(o_ref.dtype)

    return _pcall(
        body, name=name,
        grid_spec=pltpu.PrefetchScalarGridSpec(
            num_scalar_prefetch=1, grid=(s, rh // tr),
            in_specs=[pl.BlockSpec((None, None, tr, cc), lambda a, i, cr: (a, cr[0], i, 0)),
                      pl.BlockSpec((None, tr, cc), lambda a, i, cr: (a, i, 0))],
            out_specs=pl.BlockSpec((None, tr, cc), lambda a, i, cr: (a, i, 0))),
        out_shape=jax.ShapeDtypeStruct((s, rh, cc), BF16),
        compiler_params=_params(("parallel", "parallel")),
    )(c, grad, recv)


def _add_chips(part, recv, me, name):
    _, rh, cc = part.shape
    tr = _pick(rh, (256, 128, 64, 32, 16))

    def body(m_ref, p_ref, r_ref, o_ref):
        o_ref[...] = ((p_ref[...].astype(F32) + r_ref[0].astype(F32)) + r_ref[1].astype(F32)) + r_ref[2].astype(F32)

    return _pcall(
        body, name=name,
        grid_spec=pltpu.PrefetchScalarGridSpec(
            num_scalar_prefetch=1, grid=(rh // tr,),
            in_specs=[pl.BlockSpec((None, tr, cc), lambda i, mr: (mr[0], i, 0)),
                      pl.BlockSpec((3, tr, cc), lambda i, mr: (0, i, 0))],
            out_specs=pl.BlockSpec((tr, cc), lambda i, mr: (i, 0))),
        out_shape=jax.ShapeDtypeStruct((rh, cc), F32),
        compiler_params=_params(("parallel",)),
    )(me, part, recv)


def _all_sum(vec):
    rows = vec.shape[0]

    def body(v_ref, o_ref, buf, send_sems, recv_sems):
        x, y, c, _ = _place()
        me = 4 * x + 2 * y + c
        buf[me] = v_ref[...]
        cps = []
        for r in range(1, 8):
            fx, fy, fc = (r >> 2) & 1, (r >> 1) & 1, r & 1
            peer = (x ^ fx, y ^ fy, c ^ fc)
            cps.append(pltpu.make_async_remote_copy(
                src_ref=v_ref, dst_ref=buf.at[me], send_sem=send_sems.at[r - 1], recv_sem=recv_sems.at[r - 1],
                device_id=peer, device_id_type=MESH))
        for cp in cps:
            cp.start()
        for r in range(1, 8):
            src = me ^ r
            pltpu.make_async_remote_copy(
                src_ref=v_ref, dst_ref=buf.at[src], send_sem=send_sems.at[r - 1], recv_sem=recv_sems.at[r - 1],
                device_id=(x, y, c), device_id_type=MESH).wait_recv()
        for cp in cps:
            cp.wait_send()
        acc = buf[0]
        for d in range(1, 8):
            acc = acc + buf[d]
        o_ref[...] = acc

    return _pcall(
        body, name="all_sum_small",
        in_specs=[pl.BlockSpec(memory_space=pltpu.VMEM)], out_specs=pl.BlockSpec(memory_space=pltpu.VMEM),
        out_shape=jax.ShapeDtypeStruct((rows, LANES), F32),
        scratch_shapes=[pltpu.VMEM((8, rows, LANES), F32), pltpu.SemaphoreType.DMA((7,)), pltpu.SemaphoreType.DMA((7,))],
    )(vec)


def _pack(parts):
    flat = jnp.concatenate([p.reshape(-1) for p in parts])
    tile = SUBLANES * LANES
    pad = (-flat.shape[0]) % tile
    return jnp.pad(flat, (0, pad)).reshape(-1, LANES)


def _unpack(vec, shapes):
    flat = vec.reshape(-1)
    out, p = [], 0
    for s in shapes:
        n = 1
        for d in s:
            n *= d
        out.append(flat[p:p + n].reshape(s))
        p += n
    return out


def _local_step(x, tgt, norm_mix, norm_ffn, lb8, out_norm, final_norm, sc_conv, ffn_conv, first, arrive, reduce_start,
                reduce_finish):
    T, D = x.shape
    F2 = ffn_conv.shape[-1]
    FF = F2 // 2
    tm = _pick(T, (1024, 512, 256, 128))
    wide = (1536, 1408, 1024, 768, 512, 384, 256, 128)
    cw_h, cw_s, cw_u = 4 * D // N_CHIPS, 3 * D // N_CHIPS, F2 // N_CHIPS
    kp = FF // N_CHIPS
    tk_ff = kp if kp % LANES == 0 else LANES
    tn_d = _pick(D, (1024, 512, 256, 128))
    tk_w = _pick(D, (512, 256, 128))
    tn_h = _pick(cw_h, (1024, 512, 256, 128))
    tn_s = _pick(D // N_CHIPS, (512, 256, 128))
    tn_u = _pick(cw_u, wide)
    lb = lb8[0:1]
    wm_sq = _wmap_col(D, tn_d, 0)
    wm_sq1 = _wmap_col(D, D, 0)
    seg1 = lambda a: a.reshape((1,) + a.shape)

    def mix_in(h, w):
        return _row_call(_rms_fwd_fn, [(h, 0, D)], [w], [(D, BF16)], 0, "rms_fwd")[0]

    def rms_bwd(h, dxn, dh, w):
        return _row_call(_rms_bwd_fn, [(h, 0, D), (dxn, 0, D), (dh, 0, D)], [w], [(D, F32), (D, BF16)], 1, "rms_bwd")

    def ffn_fwd(h, i, fetch_up, behind_down=()):
        xn = mix_in(h, norm_ffn[i:i + 1])
        tn = _pick(cw_u, wide)
        fetch_down, xn = arrive("ffn_w_down%d" % i, xn)
        w_up = fetch_up(xn)
        up = _mm_nn(xn, w_up, _wmap_col(cw_u, tn, 0), D, F2, tm, D, tn, "ffn_up")
        nb = FF // LANES
        a = _col_call(_glu_fwd_fn, [(up, 0), (up, nb)], [(ffn_conv[i], 0), (ffn_conv[i], nb)], [(1, FF, BF16)], 0,
                      "glu_fwd", before=True, after=False)[0][0]
        later = []
        for name in behind_down:
            fetch, a = arrive(name, a)
            later.append(fetch)
        w_down = fetch_down(a)
        h2 = _mm_nn(a, w_down, _wmap_row(kp, tk_ff, 0), FF, D, tm, tk_ff, tn_d, "ffn_down", res=h, per_step=2)
        return h2, (xn, up, a), w_up, w_down, later

    def ffn_bwd(dh, dh16, h, saved, i, w_up, w_down):
        xn, up, a = saved
        g_down = _mm_tn(a, seg1(dh16), (N_CHIPS, 1, kp, D), _wmap_row(kp, tk_ff, 0), FF, D, tk_ff, tn_d,
                        "ffn_down_dw", tm=_pick(T, (2048, 1024, 512, 256, 128)))
        dh16 = reduce_start(("ffn_w_down", i), g_down, dh16)
        da = _mm_nt(seg1(dh16), w_down, _wmap_row(kp, tk_ff, 0), FF, D, tm, tk_ff, D, "ffn_down_dx")
        nb = FF // LANES
        dgv, cg, cv = _col_call(_glu_bwd_fn, [(up, 0), (up, nb), (da, 0)], [(ffn_conv[i], 0), (ffn_conv[i], nb)],
                                [(2, FF, BF16)], 2, "glu_bwd", before=True, after=True)
        g_up = _mm_tn(xn, dgv, (N_CHIPS, 1, D, cw_u), _wmap_col(cw_u, tn_u, 0), D, F2, tk_w, tn_u, "ffn_up_dw")
        dgv = reduce_start(("ffn_w_up", i), g_up, dgv)
        dxn = _mm_nt(dgv, w_up, _wmap_col(cw_u, tn_u, 0), D, F2, _pick(T, (512, 256, 128)), D, tn_u, "ffn_up_dx",
                     per_step=2)
        dh2, dh2_16, dnw = rms_bwd(h, dxn, dh, norm_ffn[i:i + 1])
        return dh2, reduce_finish(dh2_16), dnw, jnp.concatenate([cg[:3], cv[:3]], axis=1)

    h0 = x
    xn0 = mix_in(h0, norm_mix[0:1])
    proj, w_hin = first(xn0, lambda w, s, prev: _mm_nn_shard(xn0, w, s, tm, tn_h, "hgrn_in", prev))
    o, states = _hgrn_fwd(proj, lb, D)
    fetch_hout, o = arrive("hgrn_w_out", o)
    on = _row_call(_onorm_fwd_fn, [(o, 0, D), (proj, 3, D)], [out_norm], [(D, BF16)], 0, "onorm_fwd")[0]
    fetch_up0, on = arrive("ffn_w_up0", on)
    w_hout1 = fetch_hout(on).reshape(1, D, D)
    h1 = _mm_nn(on, w_hout1, wm_sq, D, D, tm, D, tn_d, "hgrn_out", res=h0)
    h2, ffn0, w_up0, w_down0, (fetch_sin, fetch_sout) = ffn_fwd(h1, 0, fetch_up0, ("sc_w_in", "sc_w_out"))
    xn1 = mix_in(h2, norm_mix[1:2])
    w_sin = fetch_sin(xn1)
    tn_si = _pick(cw_s, wide)
    sproj = _mm_nn(xn1, w_sin, _wmap_col(cw_s, tn_si, 0), D, 3 * D, tm, D, tn_si, "sc_in")
    fetch_up1, sproj = arrive("ffn_w_up1", sproj)
    nd = D // LANES
    ysc = _col_call(_sc_fwd_fn, [(sproj, 0), (sproj, nd), (sproj, 2 * nd)], [(sc_conv, 0)], [(1, D, BF16)], 0,
                    "sc_fwd", before=True, after=False)[0][0]
    w_sout1 = fetch_sout(ysc).reshape(1, D, D)
    h3 = _mm_nn(ysc, w_sout1, wm_sq, D, D, tm, D, tn_d, "sc_out", res=h2)
    h4, ffn1, w_up1, w_down1, _ = ffn_fwd(h3, 1, fetch_up1)

    dh, dh16, esq, dfinal = _row_call(_final_fn, [(h4, 0, D), (tgt, 0, D)], [final_norm], [(D, F32), (D, BF16)], 2,
                                      "final_loss")
    loss = 0.5 / D * jnp.sum(esq)
    dh, dh16, dnf1, dconv1 = ffn_bwd(dh, dh16, h3, ffn1, 1, w_up1, w_down1)
    g_sout = _mm_tn(ysc, seg1(dh16), (1, 1, D, D), wm_sq, D, D, tk_w, tn_d, "sc_out_dw")
    dh16 = reduce_start(("sc_w_out", 0), g_sout, dh16)
    dy = _mm_nt(seg1(dh16), w_sout1, wm_sq1, D, D, tm, D, D, "sc_out_dx")
    dsp, dscc = _col_call(_sc_bwd_fn, [(sproj, 0), (sproj, nd), (sproj, 2 * nd), (dy, 0)], [(sc_conv, 0)],
                          [(3, D, BF16)], 1, "sc_bwd", before=True, after=True)
    g_sin = _mm_tn(xn1, dsp, (N_CHIPS, 1, D, cw_s), _wmap_col(cw_s, tn_s, 0), D, 3 * D, tk_w, tn_s, "sc_in_dw")
    dsp = reduce_start(("sc_w_in", 0), g_sin, dsp)
    dxn = _mm_nt(dsp, w_sin, _wmap_col(cw_s, tn_s, 0), D, 3 * D, tm, D, tn_s, "sc_in_dx", per_step=3)
    dh, dh16, dnm1 = rms_bwd(h2, dxn, dh, norm_mix[1:2])
    dh16 = reduce_finish(dh16)
    dh, dh16, dnf0, dconv0 = ffn_bwd(dh, dh16, h1, ffn0, 0, w_up0, w_down0)
    g_hout = _mm_tn(on, seg1(dh16), (1, 1, D, D), wm_sq, D, D, tk_w, tn_d, "hgrn_out_dw")
    dh16 = reduce_start(("hgrn_w_out", 0), g_hout, dh16)
    don = _mm_nt(seg1(dh16), w_hout1, wm_sq1, D, D, tm, D, D, "hgrn_out_dx")
    do, dgate, dgain = _row_call(_onorm_bwd_fn, [(o, 0, D), (proj, 3, D), (don, 0, D)], [out_norm],
                                 [(D, F32), (D, BF16)], 1, "onorm_bwd")
    dproj, dlb = _hgrn_bwd(proj, lb, states, do, dgate, D)
    g_hin = _mm_tn(xn0, dproj, (N_CHIPS, 1, D, cw_h), _wmap_col(cw_h, tn_h, 0), D, 4 * D, tk_w, tn_h, "hgrn_in_dw")
    dproj = reduce_start(("hgrn_w_in", 0), g_hin, dproj)
    dxn = _mm_nt(dproj, w_hin, _wmap_col(cw_h, tn_h, 0), D, 4 * D, tm, D, tn_h, "hgrn_in_dx", per_step=2)
    grad_x, _, dnm0 = rms_bwd(h0, dxn, dh, norm_mix[0:1])

    small = dict(
        loss=loss,
        norm_mix=jnp.stack([jnp.sum(dnm0, axis=0), jnp.sum(dnm1, axis=0)]),
        norm_ffn=jnp.stack([jnp.sum(dnf0, axis=0), jnp.sum(dnf1, axis=0)]),
        lb=dlb[0:1],
        out_norm=jnp.sum(dgain, axis=0)[None],
        final_norm=jnp.sum(dfinal, axis=0),
        sc_conv=dscc[:3],
        ffn_conv=jnp.stack([dconv0, dconv1]),
    )
    return grad_x, small


def kernel(x, norm_mix, norm_ffn, hgrn_w_in, hgrn_lb_table, hgrn_out_norm, hgrn_w_out, sc_w_in, sc_conv, sc_w_out, ffn_w_up, ffn_conv, ffn_w_down, final_norm, loss_target, m_norm_mix, m_norm_ffn, m_hgrn_w_in, m_hgrn_lb_table, m_hgrn_out_norm, m_hgrn_w_out, m_sc_w_in, m_sc_conv, m_sc_w_out, m_ffn_w_up, m_ffn_conv, m_ffn_w_down, m_final_norm, v_norm_mix, v_norm_ffn, v_hgrn_w_in, v_hgrn_lb_table, v_hgrn_out_norm, v_hgrn_w_out, v_sc_w_in, v_sc_conv, v_sc_w_out, v_ffn_w_up, v_ffn_conv, v_ffn_w_down, v_final_norm):
    D = x.shape[-1]
    xi, yi, ci = lax.axis_index("x"), lax.axis_index("y"), lax.axis_index("c")
    me_chip = (2 * xi + yi).astype(jnp.int32).reshape(1)
    me_core = ci.astype(jnp.int32).reshape(1)

    big_names = ["hgrn_w_in", "hgrn_w_out", "sc_w_in", "sc_w_out", "ffn_w_up", "ffn_w_down"]
    big_w = dict(hgrn_w_in=hgrn_w_in, hgrn_w_out=hgrn_w_out, sc_w_in=sc_w_in, sc_w_out=sc_w_out,
                 ffn_w_up=ffn_w_up, ffn_w_down=ffn_w_down)
    big_m = dict(hgrn_w_in=m_hgrn_w_in, hgrn_w_out=m_hgrn_w_out, sc_w_in=m_sc_w_in, sc_w_out=m_sc_w_out,
                 ffn_w_up=m_ffn_w_up, ffn_w_down=m_ffn_w_down)
    big_v = dict(hgrn_w_in=v_hgrn_w_in, hgrn_w_out=v_hgrn_w_out, sc_w_in=v_sc_w_in, sc_w_out=v_sc_w_out,
                 ffn_w_up=v_ffn_w_up, ffn_w_down=v_ffn_w_down)
    flat2 = lambda a: a.reshape(-1, a.shape[-1])

    sh = lambda a, layer=0: _to_bf16(flat2(a), layer, a.shape[0])
    in_order_of_use = [("hgrn_w_in", sh(hgrn_w_in)), ("hgrn_w_out", sh(hgrn_w_out)), ("ffn_w_up0", sh(ffn_w_up, 0)),
                       ("ffn_w_down0", sh(ffn_w_down, 0)), ("sc_w_in", sh(sc_w_in)), ("sc_w_out", sh(sc_w_out)),
                       ("ffn_w_up1", sh(ffn_w_up, 1)), ("ffn_w_down1", sh(ffn_w_down, 1))]
    scc4, fcc4 = _gather_weights([], [flat2(sc_conv), flat2(ffn_conv)])
    names = [n for n, _ in in_order_of_use]
    ss, rs, src, land, (scc4, norm_mix) = _gather_start([s for _, s in in_order_of_use], [scc4, norm_mix], "gather_start")
    travelling = {n: (s, l) for n, s, l in zip(names, src, land)}

    def landed(name, after, call, which=None):
        (s,), (l,) = _gather_wait(ss, rs, [travelling[name][0]], [travelling[name][1]], after, call,
                                  base=names.index(name), which=which)
        travelling[name] = (s, l)
        return l

    scc = jnp.moveaxis(scc4, 0, 1).reshape(3, D)
    f2 = ffn_conv.shape[-1] * N_CHIPS
    fcc = jnp.moveaxis(fcc4.reshape(N_CHIPS, 2, 3, -1), 0, 2).reshape(2, 3, f2)

    def first(after, matmul):
        w = landed("hgrn_w_in", after, "gather_wait_0_own", which=(3,))
        proj = matmul(w, me_chip, None)
        others = [2 * (1 - xi) + yi, 2 * xi + (1 - yi), 2 * (1 - xi) + (1 - yi)]
        for j, s in enumerate(others):
            w = landed("hgrn_w_in", proj, "gather_wait_0_%d" % j, which=(j,))
            fs, fr, w, proj = _forward_start(w, proj, "gather_forward_start_0_%d" % j, which=(j,))
            w = _forward_wait(fs, fr, w, proj, "gather_forward_wait_0_%d" % j, which=(j,))
            travelling["hgrn_w_in"] = (travelling["hgrn_w_in"][0], w)
            proj = matmul(w, s.astype(jnp.int32).reshape(1), proj)
        return proj, w

    def arrive(name, after):
        fs, fr, w, after = _forward_start(landed(name, after, "gather_wait_" + name), after, "gather_forward_start_" + name)
        return functools.partial(_forward_wait, fs, fr, w, name="gather_forward_wait_" + name), after

    pending = []
    started = []

    def reduce_start(slot, grad, thru):
        t = sum(len(b[0]) for b in pending) + len(started)
        halves = grad.reshape(N_CHIPS, 2, -1, grad.shape[-1])
        ss, rs, src, land, thru = _sibling_start([halves], True, thru, "grad_pair_start_%d" % t)
        started.append((slot, t, ss, rs, src, land))
        return thru

    def reduce_finish(thru):
        k = len(pending)
        pair = []
        for slot, t, ss, rs, src, land in started:
            src, recv = _sibling_wait(ss, rs, src, land, True, thru, "grad_pair_wait_%d" % t)
            pair.append(_add_pair(src[0], recv[0], me_core, "grad_add_pair"))
        ss, rs, pair, land, thru = _chip_start(pair, thru, "grad_chip_start_%d" % k)
        pending.append(([s[0] for s in started], ss, rs, pair, land))
        started.clear()
        return thru

    lb8 = _lb_softmax(hgrn_lb_table)
    grad_x, small = _local_step(
        x[0], loss_target[0], norm_mix, norm_ffn, lb8, hgrn_out_norm, final_norm[None], scc, fcc, first, arrive,
        reduce_start, reduce_finish)

    small_names = ["loss", "norm_mix", "norm_ffn", "lb", "out_norm", "final_norm", "sc_conv", "ffn_conv"]
    parts = [small[n].astype(F32) for n in small_names]
    shapes = [p.shape for p in parts]
    tot = dict(zip(small_names, _unpack(reduce_finish(_all_sum(_pack(parts))), shapes)))
    loss = tot["loss"].reshape(())
    g_lb_table = _lb_table_grad(lb8, tot["lb"], hgrn_lb_table.shape[0])
    cw = sc_conv.shape[-1]
    g_sc_conv = lax.dynamic_slice_in_dim(tot["sc_conv"], me_chip[0] * cw, cw, axis=1)[None]
    cf = ffn_conv.shape[-1]
    g_ffn_conv = lax.dynamic_slice_in_dim(tot["ffn_conv"], me_chip[0] * cf, cf, axis=2)
    g_small = dict(norm_mix=tot["norm_mix"], norm_ffn=tot["norm_ffn"], hgrn_lb_table=g_lb_table,
                   hgrn_out_norm=tot["out_norm"], sc_conv=g_sc_conv, ffn_conv=g_ffn_conv, final_norm=tot["final_norm"])
    w_small = dict(norm_mix=norm_mix, norm_ffn=norm_ffn, hgrn_lb_table=hgrn_lb_table, hgrn_out_norm=hgrn_out_norm,
                   sc_conv=sc_conv, ffn_conv=ffn_conv, final_norm=final_norm)
    m_small = dict(norm_mix=m_norm_mix, norm_ffn=m_norm_ffn, hgrn_lb_table=m_hgrn_lb_table, hgrn_out_norm=m_hgrn_out_norm,
                   sc_conv=m_sc_conv, ffn_conv=m_ffn_conv, final_norm=m_final_norm)
    v_small = dict(norm_mix=v_norm_mix, norm_ffn=v_norm_ffn, hgrn_lb_table=v_hgrn_lb_table, hgrn_out_norm=v_hgrn_out_norm,
                   sc_conv=v_sc_conv, ffn_conv=v_ffn_conv, final_norm=v_final_norm)
    sm_names = list(g_small)
    sm_shapes = [w_small[n].shape for n in sm_names]
    d_s, m_s, v_s = _adamw(_pack([w_small[n] for n in sm_names]), _pack([g_small[n] for n in sm_names]),
                           _pack([m_small[n] for n in sm_names]), _pack([v_small[n] for n in sm_names]), "adamw_small")
    out_g, out_d, out_m, out_v = dict(g_small), {}, {}, {}
    for n, d_, m_, v_ in zip(sm_names, _unpack(d_s, sm_shapes), _unpack(m_s, sm_shapes), _unpack(v_s, sm_shapes)):
        out_d[n], out_m[n], out_v[n] = d_, m_, v_

    done = {}
    after = grad_x

    def add_and_share(k, after):
        slots, ss, rs, pair, land = pending[k]
        pair, recv = _chip_wait(ss, rs, pair, land, after, "grad_chip_wait_%d" % k)
        mine = [_add_chips(p, r, me_chip, "grad_add_chips") for p, r in zip(pair, recv)]
        ss, rs, mine, land, _ = _sibling_start(mine, False, None, "grad_share_start_%d" % k)
        return slots, ss, rs, mine, land

    def update(k, share, after):
        slots, ss, rs, mine, land = share
        mine, theirs = _sibling_wait(ss, rs, mine, land, False, after, "grad_share_wait_%d" % k)
        for (n, layer), gm, gr in zip(slots, mine, theirs):
            done[n] = _adamw_halves(flat2(big_w[n]), flat2(big_m[n]), flat2(big_v[n]), gm, gr, me_core, "adamw_" + n,
                                    layer=layer, prev=done.get(n))
        return done[slots[-1][0]][0]

    shares = []
    for k in range(len(pending) - 1):
        shares.append(add_and_share(k, after))
        after = shares[-1][3][0]
    for k, share in enumerate(shares):
        after = update(k, share, after)
    last = len(pending) - 1
    share = add_and_share(last, after)
    update(last, share, share[3][0])
    for n in big_names:
        out_g[n], out_d[n], out_m[n], out_v[n] = (a.reshape(big_w[n].shape) for a in done[n])

    order = ["norm_mix", "norm_ffn", "hgrn_w_in", "hgrn_lb_table", "hgrn_out_norm", "hgrn_w_out", "sc_w_in", "sc_conv",
             "sc_w_out", "ffn_w_up", "ffn_conv", "ffn_w_down", "final_norm"]
    return (loss, grad_x[None], *[out_g[n] for n in order], *[out_d[n] for n in order],
            *[out_m[n] for n in order], *[out_v[n] for n in order])
```

```python
import functools

import jax
import jax.numpy as jnp
from jax import lax
from jax.experimental import pallas as pl
from jax.experimental.pallas import tpu as pltpu

F32 = jnp.float32
BF16 = jnp.bfloat16
MESH = pl.DeviceIdType.MESH

EPS = 1e-6
CHUNK = 64
HEAD = 128
N_CHIPS = 4
ADAM_LR, ADAM_B1, ADAM_B2, ADAM_EPS, ADAM_WD, ADAM_STEP = 0.001, 0.9, 0.999, 1e-08, 0.01, 10
VMEM_LIMIT = 56 * 1024 * 1024
SUBLANES = 8
LANES = 128


def _pcall(body, **kw):
    return pl.pallas_call(body, **kw)


def _params(sem, vmem=VMEM_LIMIT):
    return pltpu.CompilerParams(dimension_semantics=sem, vmem_limit_bytes=vmem)


def _pick(dim, prefs):
    for p in prefs:
        if p <= dim and dim % p == 0:
            return p
    return dim


def _sigmoid(x):
    return 1.0 / (1.0 + jnp.exp(-x))


def _wmap_col(cw, tn, r0):
    bps = cw // tn
    return lambda kb, nb: (nb // bps, r0 + kb, nb % bps)


def _wmap_row(kp, tk, r0):
    bps = kp // tk
    return lambda kb, nb: (kb // bps, r0 + kb % bps, nb)


def _mm_nn(a, w3, wmap, K, N, tm, tk, tn, name, res=None, per_step=1):
    M = a.shape[0]
    u = per_step
    nk = K // (tk * u)

    def body(*refs):
        r_ref = None if res is None else refs[2 * u]
        o_ref = refs[2 * u + (0 if res is None else 1)]
        p = None
        for r in range(u):
            d = jnp.dot(refs[r][...], refs[u + r][...], preferred_element_type=F32)
            p = d if p is None else p + d
        if nk == 1:
            o_ref[...] = p if res is None else p + r_ref[...]
            return
        acc = refs[-1]
        k = pl.program_id(2)

        @pl.when(k == 0)
        def _():
            acc[...] = p

        @pl.when(k > 0)
        def _():
            acc[...] += p

        @pl.when(k == nk - 1)
        def _():
            o_ref[...] = acc[...] if res is None else acc[...] + r_ref[...]

    if nk == 1:
        grid = (M // tm, N // tn)
        ix = lambda f: (lambda i, j: f(i, j, 0))
        sem = ("parallel", "parallel")
        scratch = []
    else:
        grid = (M // tm, N // tn, nk)
        ix = lambda f: f
        sem = ("parallel", "parallel", "arbitrary")
        scratch = [pltpu.VMEM((tm, tn), F32)]
    def a_spec(r):
        return pl.BlockSpec((tm, tk), ix(lambda i, j, k: (i, k * u + r)))

    def w_spec(r):
        return pl.BlockSpec((None, tk, tn), ix(lambda i, j, k: wmap(k * u + r, j)))

    in_specs = [a_spec(r) for r in range(u)] + [w_spec(r) for r in range(u)]
    args = [a] * u + [w3] * u
    if res is not None:
        in_specs.append(pl.BlockSpec((tm, tn), ix(lambda i, j, k: (i, j))))
        args.append(res)
    return _pcall(
        body, name=name, grid=grid, in_specs=in_specs,
        out_specs=pl.BlockSpec((tm, tn), ix(lambda i, j, k: (i, j))),
        out_shape=jax.ShapeDtypeStruct((M, N), F32), scratch_shapes=scratch, compiler_params=_params(sem),
    )(*args)


def _mm_nn_shard(a, w3, s, tm, tn, name, prev=None):
    M, K = a.shape
    S, _, cw = w3.shape
    bps = cw // tn

    def body(s_ref, a_ref, w_ref, *rest):
        o_ref = rest[-1]
        o_ref[...] = jnp.dot(a_ref[...], w_ref[...], preferred_element_type=F32)

    in_specs = [pl.BlockSpec((tm, K), lambda i, j, sr: (i, 0)),
                pl.BlockSpec((None, K, tn), lambda i, j, sr: (sr[0], 0, j))]
    args = [s, a, w3]
    alias = {}
    if prev is not None:
        in_specs.append(pl.BlockSpec(memory_space=pl.ANY))
        args.append(prev)
        alias = {3: 0}
    return _pcall(
        body, name=name,
        grid_spec=pltpu.PrefetchScalarGridSpec(
            num_scalar_prefetch=1, grid=(M // tm, bps), in_specs=in_specs,
            out_specs=pl.BlockSpec((tm, tn), lambda i, j, sr: (i, sr[0] * bps + j))),
        out_shape=jax.ShapeDtypeStruct((M, S * cw), F32), input_output_aliases=alias,
        compiler_params=_params(("parallel", "parallel")),
    )(*args)


def _mm_nt(dy3, w3, wmap, K, N, tm, tk, tn, name, per_step=1):
    M = dy3.shape[1]
    bps = dy3.shape[2] // tn
    u = per_step
    grid = (M // tm, K // tk, N // (tn * u))
    nn = grid[2]

    def body(*refs):
        o_ref = refs[-1]
        p = None
        for r in range(u):
            d = lax.dot_general(refs[r][...], refs[u + r][...], (((1,), (1,)), ((), ())), preferred_element_type=F32)
            p = d if p is None else p + d
        if nn == 1:
            o_ref[...] = p
            return
        n = pl.program_id(2)

        @pl.when(n == 0)
        def _():
            o_ref[...] = p

        @pl.when(n > 0)
        def _():
            o_ref[...] += p

    def dy_spec(r):
        return pl.BlockSpec((None, tm, tn), lambda i, j, n: ((n * u + r) // bps, i, (n * u + r) % bps))

    def w_spec(r):
        return pl.BlockSpec((None, tk, tn), lambda i, j, n: wmap(j, n * u + r))

    return _pcall(
        body, name=name, grid=grid,
        in_specs=[dy_spec(r) for r in range(u)] + [w_spec(r) for r in range(u)],
        out_specs=pl.BlockSpec((tm, tk), lambda i, j, n: (i, j)),
        out_shape=jax.ShapeDtypeStruct((M, K), F32),
        compiler_params=_params(("parallel", "parallel", "arbitrary")),
    )(*([dy3] * u), *([w3] * u))


def _mm_tn(x, dy3, shape4, wmap, K, N, tk, tn, name, tm=None):
    M = x.shape[0]
    tm = M if tm is None else tm
    nm = M // tm
    bps = dy3.shape[2] // tn

    def body(*refs):
        x_ref, dy_ref = refs[:2]
        p = lax.dot_general(x_ref[...], dy_ref[...], (((0,), (0,)), ((), ())), preferred_element_type=F32)
        if nm == 1:
            o_ref = refs[-1]
            o_ref[...] = p.astype(o_ref.dtype)
            return
        o_ref, acc = refs[-2:]
        m = pl.program_id(2)

        @pl.when(m == 0)
        def _():
            acc[...] = p

        @pl.when(m > 0)
        def _():
            acc[...] += p

        @pl.when(m == nm - 1)
        def _():
            o_ref[...] = acc[...].astype(o_ref.dtype)

    def omap(i, j, m):
        s, rb, cb = wmap(i, j)
        return (s, 0, rb, cb)

    return _pcall(
        body, name=name, grid=(K // tk, N // tn, nm),
        in_specs=[pl.BlockSpec((tm, tk), lambda i, j, m: (m, i)),
                  pl.BlockSpec((None, tm, tn), lambda i, j, m: (j // bps, m, j % bps))],
        out_specs=pl.BlockSpec((None, None, tk, tn), omap),
        out_shape=jax.ShapeDtypeStruct(shape4, BF16),
        scratch_shapes=[] if nm == 1 else [pltpu.VMEM((tk, tn), F32)],
        compiler_params=_params(("parallel", "parallel", "arbitrary")),
    )(x, dy3)


def _row_call(fn, rows, vecs, outs, n_acc, name, t_rows=256, sub=16, per_trip=4):
    T = rows[0][0].shape[0]
    t_rows = min(t_rows, T)
    nsub = t_rows // sub
    n_r, n_v, n_o = len(rows), len(vecs), len(outs)
    width = rows[0][2]

    def body(*refs):
        r_refs = refs[:n_r]
        v_refs = refs[n_r:n_r + n_v]
        o_refs = refs[n_r + n_v:n_r + n_v + n_o]
        a_refs = refs[n_r + n_v + n_o:]

        @pl.when(pl.program_id(0) == 0)
        def _():
            for a in a_refs:
                a[...] = jnp.zeros_like(a)

        vv = [v[...] for v in v_refs]

        def step(i, carry):
            done = []
            for u in range(per_trip):
                sl = pl.ds(pl.multiple_of((i * per_trip + u) * sub, sub), sub)
                done.append((sl,) + tuple(fn([r[sl, :] for r in r_refs], vv)))
            for sl, o_vals, a_vals in done:
                for o, val in zip(o_refs, o_vals):
                    o[sl, :] = val.astype(o.dtype)
            for a_i, a in enumerate(a_refs):
                tot = None
                for _, _, a_vals in done:
                    part = a_vals[a_i].reshape(sub // SUBLANES, SUBLANES, a_vals[a_i].shape[-1]).sum(axis=0)
                    tot = part if tot is None else tot + part
                a[...] += tot
            return carry

        lax.fori_loop(0, nsub // per_trip, step, 0)

    in_specs = [pl.BlockSpec((t_rows, w), functools.partial(lambda i, cb: (i, cb), cb=cb)) for _, cb, w in rows]
    in_specs += [pl.BlockSpec(v.shape, lambda i: (0, 0)) for v in vecs]
    out_specs = [pl.BlockSpec((t_rows, w), lambda i: (i, 0)) for w, _ in outs]
    out_specs += [pl.BlockSpec((SUBLANES, width), lambda i: (0, 0)) for _ in range(n_acc)]
    out_shape = [jax.ShapeDtypeStruct((T, w), dt) for w, dt in outs]
    out_shape += [jax.ShapeDtypeStruct((SUBLANES, width), F32) for _ in range(n_acc)]
    return _pcall(
        body, name=name, grid=(T // t_rows,), in_specs=in_specs, out_specs=out_specs, out_shape=out_shape,
        compiler_params=_params(("arbitrary",)),
    )(*[r[0] for r in rows], *vecs)


def _rms_fwd_fn(rv, vv):
    h, = rv
    w, = vv
    r = lax.rsqrt(jnp.mean(h * h, axis=-1, keepdims=True) + EPS)
    return [h * r * w], []


def _rms_bwd_fn(rv, vv):
    h, dxn, dh_in = rv
    w, = vv
    d = h.shape[-1]
    r = lax.rsqrt(jnp.mean(h * h, axis=-1, keepdims=True) + EPS)
    gy = dxn * w
    dh = r * gy - h * ((r * r * r) * (1.0 / d) * jnp.sum(gy * h, axis=-1, keepdims=True))
    return [dh_in + dh] * 2, [dxn * h * r]


def _final_fn(rv, vv):
    h, tgt = rv
    w, = vv
    d = h.shape[-1]
    r = lax.rsqrt(jnp.mean(h * h, axis=-1, keepdims=True) + EPS)
    hn = h * r
    e = hn * w - tgt
    dy = e * (1.0 / d)
    gy = dy * w
    dh = r * gy - h * ((r * r * r) * (1.0 / d) * jnp.sum(gy * h, axis=-1, keepdims=True))
    return [dh] * 2, [e * e, dy * hn]


def _onorm_fwd_fn(rv, vv):
    o, g = rv
    gain, = vv
    r = lax.rsqrt(jnp.mean(o * o, axis=-1, keepdims=True) + EPS)
    return [o * r * gain * (g * _sigmoid(g))], []


def _onorm_bwd_fn(rv, vv):
    o, g, don = rv
    gain, = vv
    d = o.shape[-1]
    r = lax.rsqrt(jnp.mean(o * o, axis=-1, keepdims=True) + EPS)
    sg = _sigmoid(g)
    sl = g * sg
    n = o * r
    dg = don * n * gain * (sg * (1.0 + g * (1.0 - sg)))
    gy = don * sl * gain
    do = r * gy - o * ((r * r * r) * (1.0 / d) * jnp.sum(gy * o, axis=-1, keepdims=True))
    return [do, dg], [don * sl * n]


HALO = SUBLANES


def _col_call(fn, cols, vecs, outs, n_acc, name, before, after, tc=LANES, chunk=256):
    T = cols[0][0].shape[0]
    chunk = min(chunk, T)
    nch = T // chunk
    ncol = outs[0][1] // tc
    n_c, n_v, n_o = len(cols), len(vecs), len(outs)
    hb = HALO if before else 0
    rw = chunk + hb + (HALO if after else 0)

    def body(*refs):
        c_refs = refs[:n_c]
        v_refs = refs[n_c:n_c + n_v]
        o_refs = refs[n_c + n_v:n_c + n_v + n_o]
        a_refs = refs[n_c + n_v + n_o:]
        vv = [v[...] for v in v_refs]
        wrow = lax.broadcasted_iota(jnp.int32, (rw, tc), 0)
        inside = (wrow >= hb) & (wrow < hb + chunk)

        def step(i, carry):
            r0 = pl.multiple_of(i * chunk, chunk)
            wins = []
            for ref in c_refs:
                parts = []
                if before:
                    pb = ref[pl.ds(pl.multiple_of(jnp.maximum(r0 - HALO, 0), HALO), HALO), :]
                    parts.append(jnp.where(i > 0, pb, 0.0))
                parts.append(ref[pl.ds(r0, chunk), :])
                if after:
                    pa = ref[pl.ds(pl.multiple_of(jnp.minimum(r0 + chunk, T - HALO), HALO), HALO), :]
                    parts.append(jnp.where(i < nch - 1, pa, 0.0))
                wins.append(jnp.concatenate(parts, axis=0) if len(parts) > 1 else parts[0])
            o_vals, a_vals = fn(wins, vv, inside)
            p = 0
            for o, (nseg, _, _) in zip(o_refs, outs):
                for s in range(nseg):
                    o[s, pl.ds(r0, chunk), :] = o_vals[p][hb:hb + chunk].astype(o.dtype)
                    p += 1
            return tuple(c + a for c, a in zip(carry, a_vals))

        taps = [v.shape[0] for v, _ in vecs][:n_acc]
        init = tuple(jnp.zeros((1, tc), F32) for k in taps for _ in range(k))
        sums = lax.fori_loop(0, nch, step, init)
        arow = lax.broadcasted_iota(jnp.int32, (SUBLANES, tc), 0)
        p = 0
        for a, k in zip(a_refs, taps):
            acc = jnp.zeros((SUBLANES, tc), F32)
            for t in range(k):
                acc = jnp.where(arow == t, sums[p], acc)
                p += 1
            a[...] = acc

    in_specs = [pl.BlockSpec((T, tc), functools.partial(lambda j, off: (0, off + j), off=off)) for _, off in cols]
    in_specs += [pl.BlockSpec((v.shape[0], tc), functools.partial(lambda j, off: (0, off + j), off=off))
                 for v, off in vecs]
    out_specs = [pl.BlockSpec((nseg, T, tc), lambda j: (0, 0, j)) for nseg, _, _ in outs]
    out_specs += [pl.BlockSpec((SUBLANES, tc), lambda j: (0, j)) for _ in range(n_acc)]
    out_shape = [jax.ShapeDtypeStruct((nseg, T, w), dt) for nseg, w, dt in outs]
    out_shape += [jax.ShapeDtypeStruct((SUBLANES, ncol * tc), F32) for _ in range(n_acc)]
    return _pcall(
        body, name=name, grid=(ncol,), in_specs=in_specs, out_specs=out_specs, out_shape=out_shape,
        compiler_params=_params(("parallel",)),
    )(*[c[0] for c in cols], *[v[0] for v in vecs])


def _down(x, k):
    return x if k == 0 else pltpu.roll(x, k, 0)


def _up(x, k):
    return x if k == 0 else pltpu.roll(x, x.shape[0] - k, 0)


def _lags(x):
    return _down(x, 2), _down(x, 1), x


def _conv(lags, w):
    return w[0:1] * lags[0] + w[1:2] * lags[1] + w[2:3] * lags[2]


def _conv_t(d, w):
    return w[2:3] * d + w[1:2] * _up(d, 1) + w[0:1] * _up(d, 2)


def _tap_sums(d, lags, inside):
    dm = jnp.where(inside, d, 0.0)
    return [jnp.sum(dm * lag, axis=0, keepdims=True) for lag in lags]


def _glu_fwd_fn(wins, vv, inside):
    xg, xv = wins
    wg, wv = vv
    ug = _conv(_lags(xg), wg)
    uv = _conv(_lags(xv), wv)
    return [ug * _sigmoid(ug) * uv], []


def _glu_bwd_fn(wins, vv, inside):
    xg, xv, da = wins
    wg, wv = vv
    lg, lv = _lags(xg), _lags(xv)
    ug = _conv(lg, wg)
    uv = _conv(lv, wv)
    sg = _sigmoid(ug)
    dug = da * uv * (sg * (1.0 + ug * (1.0 - sg)))
    duv = da * (ug * sg)
    return [_conv_t(dug, wg), _conv_t(duv, wv)], _tap_sums(dug, lg, inside) + _tap_sums(duv, lv, inside)


def _sc_fwd_fn(wins, vv, inside):
    gb, gc, hh = wins
    w, = vv
    return [gb * _conv(_lags(gc * hh), w)], []


def _sc_bwd_fn(wins, vv, inside):
    gb, gc, hh, dy = wins
    w, = vv
    lz = _lags(gc * hh)
    dcv = dy * gb
    dz = _conv_t(dcv, w)
    return [dy * _conv(lz, w), dz * hh, dz * gc], _tap_sums(dcv, lz, inside)


def _gates(qr, fr, lb):
    sg = _sigmoid(fr)
    f = lb + (1.0 - lb) * sg
    sq = _sigmoid(qr)
    q = qr * sq * (HEAD ** -0.5)
    return q, 1.0 - f, jnp.log(f), f, sg, sq


def _boundary_rows(b, g, row):
    c = b.shape[0]
    if 2 * g >= SUBLANES:
        x = b.reshape(c // (2 * g), 2 * g, LANES)
        return jnp.broadcast_to(x[:, g - 1:g, :], x.shape).reshape(c, LANES)
    x = b.reshape(c // SUBLANES, SUBLANES, LANES)
    lo = jnp.broadcast_to(x[:, 1:2, :], x.shape).reshape(c, LANES)
    hi = jnp.broadcast_to(x[:, 5:6, :], x.shape).reshape(c, LANES)
    return jnp.where((row & 4) == 0, lo, hi)


def _chunk_decays(gl, f, row):
    c = gl.shape[0]
    b = gl
    d = 1
    while d < c:
        b = b + jnp.where(row >= d, pltpu.roll(b, d, 0), 0.0)
        d *= 2
    eq, ek = [], []
    g = c // 2
    while g >= 2:
        right = (row & g) != 0
        m = _boundary_rows(b, g, row)
        z = jnp.exp(jnp.where(right, b - m, m - b))
        eq.append(jnp.where(right, z, 0.0))
        ek.append(jnp.where(right, 0.0, z))
        g //= 2
    odd = (row & 1) != 0
    eq.append(jnp.where(odd, f, 0.0))
    ek.append(jnp.where(odd, 0.0, 1.0))
    return b, eq, ek


def _intra(q, k, eq, ek, tt, ss):
    c = q.shape[0]
    qs, ks = [], []
    a = jnp.where(tt == ss, jnp.sum(q * k, axis=1, keepdims=True), 0.0)
    g = c // 2
    for e_q, e_k in zip(eq, ek):
        qg = (q * e_q).astype(BF16)
        kg = (k * e_k).astype(BF16)
        p = lax.dot_general(qg, kg, (((1,), (1,)), ((), ())), preferred_element_type=F32)
        a = a + (p if 2 * g >= c else jnp.where((tt ^ ss) < 2 * g, p, 0.0))
        qs.append(qg)
        ks.append(kg)
        g //= 2
    return a, qs, ks


def _hgrn_fwd(proj, lb, d_model):
    T = proj.shape[0]
    H = d_model // HEAD
    nch = T // CHUNK

    def body(q_ref, f_ref, v_ref, lb_ref, o_ref, s_ref):
        lbv = lb_ref[...]
        row = lax.broadcasted_iota(jnp.int32, (CHUNK, HEAD), 0)
        tt = lax.broadcasted_iota(jnp.int32, (CHUNK, CHUNK), 0)
        ss = lax.broadcasted_iota(jnp.int32, (CHUNK, CHUNK), 1)

        def step(i, st):
            sl = pl.ds(pl.multiple_of(i * CHUNK, CHUNK), CHUNK)
            q, k, gl, f, _, _ = _gates(q_ref[sl, :], f_ref[sl, :], lbv)
            v = v_ref[sl, :].astype(BF16)
            b, eq, ek = _chunk_decays(gl, f, row)
            a, _, _ = _intra(q, k, eq, ek, tt, ss)
            bl = b[CHUNK - 1:CHUNK, :]
            q0 = (q * jnp.exp(b)).astype(BF16)
            kh = (k * jnp.exp(bl - b)).astype(BF16)
            s_ref[i] = st
            o = jnp.dot(a.astype(BF16), v, preferred_element_type=F32)
            o = o + lax.dot_general(q0, st.astype(BF16), (((1,), (1,)), ((), ())), preferred_element_type=F32)
            o_ref[sl, :] = o
            return jnp.exp(bl) * st + lax.dot_general(v, kh, (((0,), (0,)), ((), ())), preferred_element_type=F32)

        per = 4 if nch % 4 == 0 else 2

        def trip(i, st):
            for u in range(per):
                st = step(per * i + u, st)
            return st

        lax.fori_loop(0, nch // per, trip, jnp.zeros((HEAD, HEAD), F32))

    col = lambda off: pl.BlockSpec((T, HEAD), functools.partial(lambda h, off: (0, off + h), off=off))
    return _pcall(
        body, name="hgrn_fwd", grid=(H,),
        in_specs=[col(0), col(H), col(2 * H), pl.BlockSpec((1, HEAD), lambda h: (0, h))],
        out_specs=[pl.BlockSpec((T, HEAD), lambda h: (0, h)),
                   pl.BlockSpec((None, nch, HEAD, HEAD), lambda h: (h, 0, 0, 0))],
        out_shape=[jax.ShapeDtypeStruct((T, d_model), F32), jax.ShapeDtypeStruct((H, nch, HEAD, HEAD), F32)],
        compiler_params=_params(("parallel",)),
    )(proj, proj, proj, lb)


def _hgrn_bwd(proj, lb, states, do, dgate, d_model):
    T = proj.shape[0]
    H = d_model // HEAD
    nch = T // CHUNK

    def body(q_ref, f_ref, v_ref, lb_ref, s_ref, do_ref, dg_ref, dp_ref, dlb_ref):
        dq_ref, df_ref, dv_ref = dp_ref.at[0], dp_ref.at[1], dp_ref.at[2]
        dp_ref[3] = dg_ref[...]
        lbv = lb_ref[...]
        row = lax.broadcasted_iota(jnp.int32, (CHUNK, HEAD), 0)
        tt = lax.broadcasted_iota(jnp.int32, (CHUNK, CHUNK), 0)
        ss = lax.broadcasted_iota(jnp.int32, (CHUNK, CHUNK), 1)
        last = row == CHUNK - 1
        nt = (((1,), (1,)), ((), ()))
        tn = (((0,), (0,)), ((), ()))

        def step(j, carry):
            dst, dlb = carry
            i = nch - 1 - j
            sl = pl.ds(pl.multiple_of(i * CHUNK, CHUNK), CHUNK)
            qr = q_ref[sl, :]
            q, k, gl, f, sg, sq = _gates(qr, f_ref[sl, :], lbv)
            v = v_ref[sl, :].astype(BF16)
            d_o = do_ref[sl, :].astype(BF16)
            st = s_ref[i]
            st16 = st.astype(BF16)
            dst16 = dst.astype(BF16)
            b, eq, ek = _chunk_decays(gl, f, row)
            a, qs, ks = _intra(q, k, eq, ek, tt, ss)
            bl = b[CHUNK - 1:CHUNK, :]
            e0 = jnp.exp(b)
            eh = jnp.exp(bl - b)
            ebl = jnp.exp(bl)
            q0 = q * e0
            kh = k * eh
            q016 = q0.astype(BF16)
            kh16 = kh.astype(BF16)
            dv = lax.dot_general(a.astype(BF16), d_o, tn, preferred_element_type=F32)
            dv = dv + lax.dot_general(kh16, dst16, nt, preferred_element_type=F32)
            dv_ref[sl, :] = dv.astype(dv_ref.dtype)
            da = lax.dot_general(d_o, v, nt, preferred_element_type=F32)
            da = jnp.where(tt >= ss, da, 0.0)
            dd = jnp.sum(jnp.where(tt == ss, da, 0.0), axis=1, keepdims=True)
            dq0 = jnp.dot(d_o, st16, preferred_element_type=F32)
            dkh = jnp.dot(v, dst16, preferred_element_type=F32)
            dq = dq0 * e0 + dd * k
            dk = dkh * eh + dd * q
            db = dq0 * q016.astype(F32) - dkh * kh16.astype(F32)
            g = CHUNK // 2
            for e_q, e_k, qg, kg in zip(eq, ek, qs, ks):
                dag = (da if 2 * g >= CHUNK else jnp.where((tt ^ ss) < 2 * g, da, 0.0)).astype(BF16)
                dqg = jnp.dot(dag, kg, preferred_element_type=F32)
                dkg = lax.dot_general(dag, qg, tn, preferred_element_type=F32)
                dq = dq + dqg * e_q
                dk = dk + dkg * e_k
                db = db + (dqg * qg.astype(F32) - dkg * kg.astype(F32))
                g //= 2
            dbl = jnp.sum(dkh * kh16.astype(F32), axis=0, keepdims=True) + ebl * jnp.sum(dst * st, axis=0, keepdims=True)
            db = db + jnp.where(last, dbl, 0.0)
            d = 1
            while d < CHUNK:
                db = db + jnp.where(row < CHUNK - d, pltpu.roll(db, CHUNK - d, 0), 0.0)
                d *= 2
            dfg = db / f - dk
            df_ref[sl, :] = (dfg * (1.0 - lbv) * sg * (1.0 - sg)).astype(df_ref.dtype)
            dq_ref[sl, :] = (dq * (HEAD ** -0.5) * (sq * (1.0 + qr * (1.0 - sq)))).astype(dq_ref.dtype)
            dlb = dlb + jnp.sum(dfg * (1.0 - sg), axis=0, keepdims=True)
            dst = ebl * dst + lax.dot_general(d_o, q016, tn, preferred_element_type=F32)
            return dst, dlb

        _, dlb = lax.fori_loop(0, nch // 2, lambda j, cr: step(2 * j + 1, step(2 * j, cr)),
                               (jnp.zeros((HEAD, HEAD), F32), jnp.zeros((1, HEAD), F32)))
        arow = lax.broadcasted_iota(jnp.int32, (SUBLANES, HEAD), 0)
        dlb_ref[...] = jnp.where(arow == 0, dlb, 0.0)

    col = lambda off: pl.BlockSpec((T, HEAD), functools.partial(lambda h, off: (0, off + h), off=off))
    return _pcall(
        body, name="hgrn_bwd", grid=(H,),
        in_specs=[col(0), col(H), col(2 * H), pl.BlockSpec((1, HEAD), lambda h: (0, h)),
                  pl.BlockSpec((None, nch, HEAD, HEAD), lambda h: (h, 0, 0, 0)), col(0), col(0)],
        out_specs=[pl.BlockSpec((4, T, HEAD), lambda h: (0, 0, h)), pl.BlockSpec((SUBLANES, HEAD), lambda h: (0, h))],
        out_shape=[jax.ShapeDtypeStruct((4, T, d_model), BF16), jax.ShapeDtypeStruct((SUBLANES, d_model), F32)],
        compiler_params=_params(("parallel",)),
    )(proj, proj, proj, lb, states, do, dgate)


def _lb_softmax(table):
    n, f = table.shape

    def body(t_ref, p_ref):
        t = t_ref[...]
        e = jnp.exp(t - jnp.max(t, axis=0, keepdims=True))
        p_ref[...] = e / jnp.sum(e, axis=0, keepdims=True)

    padded = jnp.pad(table, ((0, SUBLANES - n), (0, 0)), constant_values=-jnp.inf)
    return _pcall(body, name="lb_softmax", out_shape=jax.ShapeDtypeStruct((SUBLANES, f), F32))(padded)


def _to_bf16(w, layer, nlayers):
    R = w.shape[0] // nlayers
    C = w.shape[1]
    tr = _pick(R, (256, 128, 64, 32, 16))
    nb = R // tr

    def body(w_ref, o_ref):
        o_ref[...] = w_ref[...].astype(o_ref.dtype)

    return _pcall(
        body, name="to_bf16", grid=(nb,), in_specs=[pl.BlockSpec((tr, C), lambda i: (layer * nb + i, 0))],
        out_specs=pl.BlockSpec((tr, C), lambda i: (i, 0)), out_shape=jax.ShapeDtypeStruct((R, C), BF16),
        compiler_params=_params(("parallel",)),
    )(w)


def _adamw_math(w, g, m, v):
    m = ADAM_B1 * m + (1.0 - ADAM_B1) * g
    v = ADAM_B2 * v + (1.0 - ADAM_B2) * (g * g)
    m_hat = m / (1.0 - ADAM_B1 ** ADAM_STEP)
    v_hat = v / (1.0 - ADAM_B2 ** ADAM_STEP)
    delta = -ADAM_LR * (m_hat / (jnp.sqrt(v_hat) + ADAM_EPS) + ADAM_WD * w)
    return delta, m, v


def _adamw(w, g, m, v, name):
    R, C = w.shape
    tr = _pick(R, (128, 64, 32, 16, 8))

    def body(w_ref, g_ref, m_ref, v_ref, d_ref, nm_ref, nv_ref):
        d, nm, nv = _adamw_math(w_ref[...], g_ref[...], m_ref[...], v_ref[...])
        d_ref[...] = d
        nm_ref[...] = nm
        nv_ref[...] = nv

    spec = pl.BlockSpec((tr, C), lambda i: (i, 0))
    return _pcall(
        body, name=name, grid=(R // tr,), in_specs=[spec] * 4, out_specs=[spec] * 3,
        out_shape=[jax.ShapeDtypeStruct((R, C), F32)] * 3, compiler_params=_params(("parallel",)),
    )(w, g, m, v)


def _adamw_halves(w, m, v, g_mine, g_recv, c, name, layer=0, prev=None):
    C = w.shape[1]
    rh = g_mine.shape[0]
    tr = _pick(rh, (128, 64, 32, 16, 8))
    nb = rh // tr
    r0 = layer * 2 * nb

    def body(c_ref, w_ref, m_ref, v_ref, gm_ref, gr_ref, *rest):
        g_ref, d_ref, nm_ref, nv_ref = rest[-4:]
        g = jnp.where(pl.program_id(0) == c_ref[0], gm_ref[...], gr_ref[...])
        d, nm, nv = _adamw_math(w_ref[...], g, m_ref[...], v_ref[...])
        g_ref[...] = g
        d_ref[...] = d
        nm_ref[...] = nm
        nv_ref[...] = nv

    full = pl.BlockSpec((tr, C), lambda h, i, cr: (r0 + h * nb + i, 0))
    mine = pl.BlockSpec((tr, C), lambda h, i, cr: (jnp.where(h == cr[0], i, 0), 0))
    recv = pl.BlockSpec((tr, C), lambda h, i, cr: (jnp.where(h == cr[0], 0, i), 0))
    in_specs = [full, full, full, mine, recv]
    args = [c, w, m, v, g_mine, g_recv]
    alias = {}
    if prev is not None:
        in_specs += [pl.BlockSpec(memory_space=pl.ANY)] * 4
        args += list(prev)
        alias = {6 + k: k for k in range(4)}
    return _pcall(
        body, name=name,
        grid_spec=pltpu.PrefetchScalarGridSpec(
            num_scalar_prefetch=1, grid=(2, nb), in_specs=in_specs, out_specs=[full] * 4),
        out_shape=[jax.ShapeDtypeStruct(w.shape, F32)] * 4, input_output_aliases=alias,
        compiler_params=_params(("parallel", "parallel")),
    )(*args)


def _lb_table_grad(p8, dlb, n):
    f = p8.shape[1]

    def body(p_ref, d_ref, o_ref):
        p = p_ref[...]
        d = d_ref[...]
        p0 = p[0:1, :]
        first = lax.broadcasted_iota(jnp.int32, p.shape, 0) == 0
        o_ref[...] = p * (jnp.where(first, d, 0.0) - d * p0)

    return _pcall(body, name="lb_table_grad", out_shape=jax.ShapeDtypeStruct((SUBLANES, f), F32))(p8, dlb)[:n]


def _place():
    x, y, c = lax.axis_index("x"), lax.axis_index("y"), lax.axis_index("c")
    chips = [(1 - x, y), (x, 1 - y), (1 - x, 1 - y)]
    return x, y, c, chips


HBM_SPEC = pl.BlockSpec(memory_space=pltpu.HBM)


def _gather_weights(big, small):
    nb, ns = len(big), len(small)
    n = nb + ns

    def body(*refs):
        ins, outs = refs[:n], refs[n:2 * n]
        send_sems, recv_sems, own_send, own_recv = refs[2 * n:]
        x, y, c, chips = _place()
        me = 2 * x + y
        sib = (x, y, 1 - c)
        own = [pltpu.make_async_remote_copy(
            src_ref=ins[t], dst_ref=outs[t].at[me], send_sem=own_send.at[t], recv_sem=own_recv.at[t],
            device_id=sib, device_id_type=MESH) for t in range(n)]
        for cp in own:
            cp.start()

        def half(t, h):
            rh = big[t].shape[0] // 2
            return pl.ds(pl.multiple_of(h * rh, rh), rh)

        sends = []
        for t in range(n):
            for j, chip in enumerate(chips):
                k = 6 * t + j
                if t < nb:
                    src, dst = ins[t].at[half(t, c)], outs[t].at[me, half(t, c)]
                else:
                    src, dst = ins[t], outs[t].at[me]
                sends.append(pltpu.make_async_remote_copy(
                    src_ref=src, dst_ref=dst, send_sem=send_sems.at[k], recv_sem=recv_sems.at[k],
                    device_id=(*chip, c), device_id_type=MESH))
        for cp in sends:
            cp.start()
        passed = []
        for t in range(n):
            for j, (cx, cy) in enumerate(chips):
                k = 6 * t + j
                s = 2 * cx + cy
                if t < nb:
                    landed = outs[t].at[s, half(t, c)]
                    pltpu.make_async_remote_copy(
                        src_ref=landed, dst_ref=landed, send_sem=send_sems.at[k], recv_sem=recv_sems.at[k],
                        device_id=sib, device_id_type=MESH).wait_recv()
                    fwd = pltpu.make_async_remote_copy(
                        src_ref=landed, dst_ref=landed, send_sem=send_sems.at[k + 3], recv_sem=recv_sems.at[k + 3],
                        device_id=sib, device_id_type=MESH)
                    fwd.start()
                    passed.append(fwd)
                else:
                    landed = outs[t].at[s]
                    pltpu.make_async_remote_copy(
                        src_ref=landed, dst_ref=landed, send_sem=send_sems.at[k], recv_sem=recv_sems.at[k],
                        device_id=sib, device_id_type=MESH).wait_recv()
        for t in range(nb):
            for j, (cx, cy) in enumerate(chips):
                k = 6 * t + j
                other = outs[t].at[2 * cx + cy, half(t, 1 - c)]
                pltpu.make_async_remote_copy(
                    src_ref=other, dst_ref=other, send_sem=send_sems.at[k + 3], recv_sem=recv_sems.at[k + 3],
                    device_id=sib, device_id_type=MESH).wait_recv()
        for cp in sends + passed:
            cp.wait_send()
        for cp in own:
            cp.wait()

    arrs = list(big) + list(small)
    return _pcall(
        body, name="gather_weights", in_specs=[HBM_SPEC] * n, out_specs=[HBM_SPEC] * n,
        out_shape=[jax.ShapeDtypeStruct((N_CHIPS,) + a.shape, a.dtype) for a in arrs],
        scratch_shapes=[pltpu.SemaphoreType.DMA((6 * n,)), pltpu.SemaphoreType.DMA((6 * n,)),
                        pltpu.SemaphoreType.DMA((n,)), pltpu.SemaphoreType.DMA((n,))],
    )(*arrs)


SEM_SPEC = pl.BlockSpec(memory_space=pltpu.SEMAPHORE)
DATAFLOW = pltpu.SideEffectType.DATAFLOW_SIDE_EFFECTING
COPIES_PER_SHARD = 4


def _shard_copies(ins, lands, send_sems, recv_sems, base=0):
    x, y, c, chips = _place()
    me = 2 * x + y
    cps = []
    for t in range(len(ins)):
        rh = ins[t].shape[0] // 2
        half = pl.ds(pl.multiple_of(c * rh, rh), rh)
        for j, chip in enumerate(chips):
            k = COPIES_PER_SHARD * (base + t) + j
            cps.append(pltpu.make_async_remote_copy(
                src_ref=ins[t].at[half], dst_ref=lands[t].at[me, half], send_sem=send_sems.at[k],
                recv_sem=recv_sems.at[k], device_id=(*chip, c), device_id_type=MESH))
        k = COPIES_PER_SHARD * (base + t) + 3
        cps.append(pltpu.make_async_remote_copy(
            src_ref=ins[t], dst_ref=lands[t].at[me], send_sem=send_sems.at[k], recv_sem=recv_sems.at[k],
            device_id=(x, y, 1 - c), device_id_type=MESH))
    return cps


def _gather_start(shards, thru, name):
    n = len(shards)
    nops = 2 * n + len(thru)

    def body(*refs):
        ins, lands = refs[:n], refs[n:2 * n]
        send_sems, recv_sems = refs[nops], refs[nops + 1]
        for cp in _shard_copies(ins, lands, send_sems, recv_sems):
            cp.start()

    lands = [pltpu.with_memory_space_constraint(lax.empty((N_CHIPS,) + s.shape, s.dtype), pltpu.HBM) for s in shards]
    ops = [pltpu.with_memory_space_constraint(s, pltpu.HBM) for s in shards] + lands + list(thru)
    nsem = COPIES_PER_SHARD * n
    res = _pcall(
        body, name=name, in_specs=[HBM_SPEC] * nops,
        out_specs=[SEM_SPEC, SEM_SPEC] + [HBM_SPEC] * nops,
        out_shape=[pltpu.SemaphoreType.DMA((nsem,)), pltpu.SemaphoreType.DMA((nsem,))]
        + [pltpu.HBM(o.shape, o.dtype) for o in ops],
        input_output_aliases={i: 2 + i for i in range(nops)},
        compiler_params=pltpu.CompilerParams(has_side_effects=DATAFLOW),
    )(*ops)
    return res[0], res[1], res[2:2 + n], res[2 + n:2 + 2 * n], list(res[2 + 2 * n:])


def _gather_wait(send_sems, recv_sems, shards, lands, after, name, base=0, which=None):
    n = len(shards)

    def body(*refs):
        ins, lnd = refs[:n], refs[n:2 * n]
        ssem, rsem = refs[2 * n], refs[2 * n + 1]
        for k, cp in enumerate(_shard_copies(ins, lnd, ssem, rsem, base)):
            if which is None or k % COPIES_PER_SHARD in which:
                cp.wait_send()
                cp.wait_recv()

    res = _pcall(
        body, name=name,
        in_specs=[HBM_SPEC] * (2 * n) + [SEM_SPEC, SEM_SPEC, pl.BlockSpec(memory_space=pl.ANY)],
        out_specs=[HBM_SPEC] * (2 * n),
        out_shape=[pltpu.HBM(o.shape, o.dtype) for o in list(shards) + list(lands)],
        input_output_aliases={i: i for i in range(2 * n)},
        compiler_params=pltpu.CompilerParams(has_side_effects=DATAFLOW),
    )(*shards, *lands, send_sems, recv_sems, after)
    return res[:n], res[n:]


SIBLING_PAIR = 1


def _sibling_handshake():
    x, y, c, _ = _place()
    barrier = pltpu.get_barrier_semaphore()
    pl.semaphore_signal(barrier, inc=1, device_id=(x, y, 1 - c), device_id_type=MESH)
    pl.semaphore_wait(barrier, 1)


def _forward_copies(land, send_sems, recv_sems, which=(0, 1, 2)):
    x, y, c, chips = _place()
    rh = land.shape[1] // 2
    return [pltpu.make_async_remote_copy(
        src_ref=land.at[2 * cx + cy, pl.ds(pl.multiple_of(c * rh, rh), rh)],
        dst_ref=land.at[2 * cx + cy, pl.ds(pl.multiple_of(c * rh, rh), rh)],
        send_sem=send_sems.at[j], recv_sem=recv_sems.at[j], device_id=(x, y, 1 - c), device_id_type=MESH)
        for j, (cx, cy) in enumerate(chips) if j in which]


def _forward_start(land, thru, name, which=(0, 1, 2)):
    def body(land_ref, thru_ref, send_sems, recv_sems, out_ref, thru_out):
        _sibling_handshake()
        for cp in _forward_copies(land_ref, send_sems, recv_sems, which):
            cp.start()

    return _pcall(
        body, name=name, in_specs=[HBM_SPEC, HBM_SPEC], out_specs=[SEM_SPEC, SEM_SPEC, HBM_SPEC, HBM_SPEC],
        out_shape=[pltpu.SemaphoreType.DMA((3,)), pltpu.SemaphoreType.DMA((3,)), pltpu.HBM(land.shape, land.dtype),
                   pltpu.HBM(thru.shape, thru.dtype)],
        input_output_aliases={0: 2, 1: 3},
        compiler_params=pltpu.CompilerParams(has_side_effects=DATAFLOW, collective_id=SIBLING_PAIR),
    )(land, thru)


def _forward_wait(send_sems, recv_sems, land, after, name, which=(0, 1, 2)):
    def body(land_ref, ssem, rsem, after_ref, out_ref):
        for cp in _forward_copies(land_ref, ssem, rsem, which):
            cp.wait_send()
            cp.wait_recv()

    return _pcall(
        body, name=name, in_specs=[HBM_SPEC, SEM_SPEC, SEM_SPEC, pl.BlockSpec(memory_space=pl.ANY)],
        out_specs=HBM_SPEC, out_shape=pltpu.HBM(land.shape, land.dtype), input_output_aliases={0: 0},
        compiler_params=pltpu.CompilerParams(has_side_effects=DATAFLOW),
    )(land, send_sems, recv_sems, after)


def _sibling_copies(ins, lands, send_sems, recv_sems, other_half):
    x, y, c, _ = _place()
    return [pltpu.make_async_remote_copy(
        src_ref=ins[t].at[:, 1 - c] if other_half else ins[t], dst_ref=lands[t], send_sem=send_sems.at[t],
        recv_sem=recv_sems.at[t], device_id=(x, y, 1 - c), device_id_type=MESH) for t in range(len(ins))]


def _sibling_start(srcs, other_half, thru, name):
    n = len(srcs)
    nthru = 0 if thru is None else 1

    def body(*refs):
        ins, lands = refs[:n], refs[n:2 * n]
        send_sems, recv_sems = refs[2 * n + nthru], refs[2 * n + nthru + 1]
        _sibling_handshake()
        for cp in _sibling_copies(ins, lands, send_sems, recv_sems, other_half):
            cp.start()

    shapes = [(s.shape[0],) + s.shape[2:] if other_half else s.shape for s in srcs]
    lands = [pltpu.with_memory_space_constraint(lax.empty(sh, s.dtype), pltpu.HBM) for sh, s in zip(shapes, srcs)]
    ops = [pltpu.with_memory_space_constraint(s, pltpu.HBM) for s in srcs] + lands + ([] if thru is None else [thru])
    res = _pcall(
        body, name=name, in_specs=[HBM_SPEC] * len(ops),
        out_specs=[SEM_SPEC, SEM_SPEC] + [HBM_SPEC] * len(ops),
        out_shape=[pltpu.SemaphoreType.DMA((n,)), pltpu.SemaphoreType.DMA((n,))]
        + [pltpu.HBM(o.shape, o.dtype) for o in ops],
        input_output_aliases={i: 2 + i for i in range(len(ops))},
        compiler_params=pltpu.CompilerParams(has_side_effects=DATAFLOW, collective_id=SIBLING_PAIR),
    )(*ops)
    return res[0], res[1], res[2:2 + n], res[2 + n:2 + 2 * n], (None if thru is None else res[2 + 2 * n])


def _sibling_wait(send_sems, recv_sems, srcs, lands, other_half, after, name):
    n = len(srcs)

    def body(*refs):
        ins, lnd = refs[:n], refs[n:2 * n]
        ssem, rsem = refs[2 * n], refs[2 * n + 1]
        for cp in _sibling_copies(ins, lnd, ssem, rsem, other_half):
            cp.wait_send()
            cp.wait_recv()

    res = _pcall(
        body, name=name,
        in_specs=[HBM_SPEC] * (2 * n) + [SEM_SPEC, SEM_SPEC, pl.BlockSpec(memory_space=pl.ANY)],
        out_specs=[HBM_SPEC] * (2 * n),
        out_shape=[pltpu.HBM(o.shape, o.dtype) for o in list(srcs) + list(lands)],
        input_output_aliases={i: i for i in range(2 * n)},
        compiler_params=pltpu.CompilerParams(has_side_effects=DATAFLOW),
    )(*srcs, *lands, send_sems, recv_sems, after)
    return res[:n], res[n:]


def _chip_copies(ins, lands, send_sems, recv_sems):
    x, y, c, chips = _place()
    cps = []
    for t in range(len(ins)):
        for j, (cx, cy) in enumerate(chips):
            cps.append(pltpu.make_async_remote_copy(
                src_ref=ins[t].at[2 * cx + cy], dst_ref=lands[t].at[j],
                send_sem=send_sems.at[3 * t + j], recv_sem=recv_sems.at[3 * t + j],
                device_id=(cx, cy, c), device_id_type=MESH))
    return cps


def _chip_start(parts, thru, name):
    n = len(parts)

    def body(*refs):
        ins, lands = refs[:n], refs[n:2 * n]
        send_sems, recv_sems = refs[2 * n + 1], refs[2 * n + 2]
        for cp in _chip_copies(ins, lands, send_sems, recv_sems):
            cp.start()

    lands = [pltpu.with_memory_space_constraint(lax.empty((3,) + p.shape[1:], p.dtype), pltpu.HBM) for p in parts]
    ops = [pltpu.with_memory_space_constraint(p, pltpu.HBM) for p in parts] + lands + [thru]
    res = _pcall(
        body, name=name, in_specs=[HBM_SPEC] * (2 * n + 1),
        out_specs=[SEM_SPEC, SEM_SPEC] + [HBM_SPEC] * (2 * n + 1),
        out_shape=[pltpu.SemaphoreType.DMA((3 * n,)), pltpu.SemaphoreType.DMA((3 * n,))]
        + [pltpu.HBM(o.shape, o.dtype) for o in ops],
        input_output_aliases={i: 2 + i for i in range(2 * n + 1)},
        compiler_params=pltpu.CompilerParams(has_side_effects=DATAFLOW),
    )(*ops)
    return res[0], res[1], res[2:2 + n], res[2 + n:2 + 2 * n], res[2 + 2 * n]


def _chip_wait(send_sems, recv_sems, parts, lands, after, name):
    n = len(parts)

    def body(*refs):
        ins, lnd = refs[:n], refs[n:2 * n]
        ssem, rsem = refs[2 * n], refs[2 * n + 1]
        for cp in _chip_copies(ins, lnd, ssem, rsem):
            cp.wait_send()
            cp.wait_recv()

    res = _pcall(
        body, name=name,
        in_specs=[HBM_SPEC] * (2 * n) + [SEM_SPEC, SEM_SPEC, pl.BlockSpec(memory_space=pl.ANY)],
        out_specs=[HBM_SPEC] * (2 * n),
        out_shape=[pltpu.HBM(o.shape, o.dtype) for o in list(parts) + list(lands)],
        input_output_aliases={i: i for i in range(2 * n)},
        compiler_params=pltpu.CompilerParams(has_side_effects=DATAFLOW),
    )(*parts, *lands, send_sems, recv_sems, after)
    return res[:n], res[n:]


def _add_pair(grad, recv, c, name):
    s, _, rh, cc = grad.shape
    tr = _pick(rh, (256, 128, 64, 32, 16))

    def body(c_ref, g_ref, r_ref, o_ref):
        o_ref[...] = (g_ref[...].astype(F32) + r_ref[...].astype(F32)).astype(o_ref.dtype)

    return _pcall(
        body, name=name,
        grid_spec=pltpu.PrefetchScalarGridSpec(
            num_scalar_prefetch=1, grid=(s, rh // tr),
            in_specs=[pl.BlockSpec((None, None, tr, cc), lambda a, i, cr: (a, cr[0], i, 0)),
                      pl.BlockSpec((None, tr, cc), lambda a, i, cr: (a, i, 0))],
            out_specs=pl.BlockSpec((None, tr, cc), lambda a, i, cr: (a, i, 0))),
        out_shape=jax.ShapeDtypeStruct((s, rh, cc), BF16),
        compiler_params=_params(("parallel", "parallel")),
    )(c, grad, recv)


def _add_chips(part, recv, me, name):
    _, rh, cc = part.shape
    tr = _pick(rh, (256, 128, 64, 32, 16))

    def body(m_ref, p_ref, r_ref, o_ref):
        o_ref[...] = ((p_ref[...].astype(F32) + r_ref[0].astype(F32)) + r_ref[1].astype(F32)) + r_ref[2].astype(F32)

    return _pcall(
        body, name=name,
        grid_spec=pltpu.PrefetchScalarGridSpec(
            num_scalar_prefetch=1, grid=(rh // tr,),
            in_specs=[pl.BlockSpec((None, tr, cc), lambda i, mr: (mr[0], i, 0)),
                      pl.BlockSpec((3, tr, cc), lambda i, mr: (0, i, 0))],
            out_specs=pl.BlockSpec((tr, cc), lambda i, mr: (i, 0))),
        out_shape=jax.ShapeDtypeStruct((rh, cc), F32),
        compiler_params=_params(("parallel",)),
    )(me, part, recv)


def _all_sum(vec):
    rows = vec.shape[0]

    def body(v_ref, o_ref, buf, send_sems, recv_sems):
        x, y, c, _ = _place()
        me = 4 * x + 2 * y + c
        buf[me] = v_ref[...]
        cps = []
        for r in range(1, 8):
            fx, fy, fc = (r >> 2) & 1, (r >> 1) & 1, r & 1
            peer = (x ^ fx, y ^ fy, c ^ fc)
            cps.append(pltpu.make_async_remote_copy(
                src_ref=v_ref, dst_ref=buf.at[me], send_sem=send_sems.at[r - 1], recv_sem=recv_sems.at[r - 1],
                device_id=peer, device_id_type=MESH))
        for cp in cps:
            cp.start()
        for r in range(1, 8):
            src = me ^ r
            pltpu.make_async_remote_copy(
                src_ref=v_ref, dst_ref=buf.at[src], send_sem=send_sems.at[r - 1], recv_sem=recv_sems.at[r - 1],
                device_id=(x, y, c), device_id_type=MESH).wait_recv()
        for cp in cps:
            cp.wait_send()
        acc = buf[0]
        for d in range(1, 8):
            acc = acc + buf[d]
        o_ref[...] = acc

    return _pcall(
        body, name="all_sum_small",
        in_specs=[pl.BlockSpec(memory_space=pltpu.VMEM)], out_specs=pl.BlockSpec(memory_space=pltpu.VMEM),
        out_shape=jax.ShapeDtypeStruct((rows, LANES), F32),
        scratch_shapes=[pltpu.VMEM((8, rows, LANES), F32), pltpu.SemaphoreType.DMA((7,)), pltpu.SemaphoreType.DMA((7,))],
    )(vec)


def _pack(parts):
    flat = jnp.concatenate([p.reshape(-1) for p in parts])
    tile = SUBLANES * LANES
    pad = (-flat.shape[0]) % tile
    return jnp.pad(flat, (0, pad)).reshape(-1, LANES)


def _unpack(vec, shapes):
    flat = vec.reshape(-1)
    out, p = [], 0
    for s in shapes:
        n = 1
        for d in s:
            n *= d
        out.append(flat[p:p + n].reshape(s))
        p += n
    return out


def _local_step(x, tgt, norm_mix, norm_ffn, lb8, out_norm, final_norm, sc_conv, ffn_conv, first, arrive, reduce_start,
                reduce_finish):
    T, D = x.shape
    F2 = ffn_conv.shape[-1]
    FF = F2 // 2
    tm = _pick(T, (1024, 512, 256, 128))
    wide = (1536, 1408, 1024, 768, 512, 384, 256, 128)
    cw_h, cw_s, cw_u = 4 * D // N_CHIPS, 3 * D // N_CHIPS, F2 // N_CHIPS
    kp = FF // N_CHIPS
    tk_ff = kp if kp % LANES == 0 else LANES
    tn_d = _pick(D, (1024, 512, 256, 128))
    tk_w = _pick(D, (512, 256, 128))
    tn_h = _pick(cw_h, (1024, 512, 256, 128))
    tn_s = _pick(D // N_CHIPS, (512, 256, 128))
    tn_u = _pick(cw_u, wide)
    lb = lb8[0:1]
    wm_sq = _wmap_col(D, tn_d, 0)
    wm_sq1 = _wmap_col(D, D, 0)
    seg1 = lambda a: a.reshape((1,) + a.shape)

    def mix_in(h, w):
        return _row_call(_rms_fwd_fn, [(h, 0, D)], [w], [(D, BF16)], 0, "rms_fwd")[0]

    def rms_bwd(h, dxn, dh, w):
        return _row_call(_rms_bwd_fn, [(h, 0, D), (dxn, 0, D), (dh, 0, D)], [w], [(D, F32), (D, BF16)], 1, "rms_bwd")

    def ffn_fwd(h, i, fetch_up, behind_down=()):
        xn = mix_in(h, norm_ffn[i:i + 1])
        tn = _pick(cw_u, wide)
        fetch_down, xn = arrive("ffn_w_down%d" % i, xn)
        w_up = fetch_up(xn)
        up = _mm_nn(xn, w_up, _wmap_col(cw_u, tn, 0), D, F2, tm, D, tn, "ffn_up")
        nb = FF // LANES
        a = _col_call(_glu_fwd_fn, [(up, 0), (up, nb)], [(ffn_conv[i], 0), (ffn_conv[i], nb)], [(1, FF, BF16)], 0,
                      "glu_fwd", before=True, after=False)[0][0]
        later = []
        for name in behind_down:
            fetch, a = arrive(name, a)
            later.append(fetch)
        w_down = fetch_down(a)
        h2 = _mm_nn(a, w_down, _wmap_row(kp, tk_ff, 0), FF, D, tm, tk_ff, tn_d, "ffn_down", res=h, per_step=2)
        return h2, (xn, up, a), w_up, w_down, later

    def ffn_bwd(dh, dh16, h, saved, i, w_up, w_down):
        xn, up, a = saved
        g_down = _mm_tn(a, seg1(dh16), (N_CHIPS, 1, kp, D), _wmap_row(kp, tk_ff, 0), FF, D, tk_ff, tn_d,
                        "ffn_down_dw", tm=_pick(T, (2048, 1024, 512, 256, 128)))
        dh16 = reduce_start(("ffn_w_down", i), g_down, dh16)
        da = _mm_nt(seg1(dh16), w_down, _wmap_row(kp, tk_ff, 0), FF, D, tm, tk_ff, D, "ffn_down_dx")
        nb = FF // LANES
        dgv, cg, cv = _col_call(_glu_bwd_fn, [(up, 0), (up, nb), (da, 0)], [(ffn_conv[i], 0), (ffn_conv[i], nb)],
                                [(2, FF, BF16)], 2, "glu_bwd", before=True, after=True)
        g_up = _mm_tn(xn, dgv, (N_CHIPS, 1, D, cw_u), _wmap_col(cw_u, tn_u, 0), D, F2, tk_w, tn_u, "ffn_up_dw")
        dgv = reduce_start(("ffn_w_up", i), g_up, dgv)
        dxn = _mm_nt(dgv, w_up, _wmap_col(cw_u, tn_u, 0), D, F2, _pick(T, (512, 256, 128)), D, tn_u, "ffn_up_dx",
                     per_step=2)
        dh2, dh2_16, dnw = rms_bwd(h, dxn, dh, norm_ffn[i:i + 1])
        return dh2, reduce_finish(dh2_16), dnw, jnp.concatenate([cg[:3], cv[:3]], axis=1)

    h0 = x
    xn0 = mix_in(h0, norm_mix[0:1])
    proj, w_hin, xn0 = first(xn0, lambda a, w, s, prev: _mm_nn_shard(a, w, s, tm, tn_h, "hgrn_in", prev))
    o, states = _hgrn_fwd(proj, lb, D)
    fetch_hout, o = arrive("hgrn_w_out", o)
    on = _row_call(_onorm_fwd_fn, [(o, 0, D), (proj, 3, D)], [out_norm], [(D, BF16)], 0, "onorm_fwd")[0]
    fetch_up0, on = arrive("ffn_w_up0", on)
    w_hout1 = fetch_hout(on).reshape(1, D, D)
    h1 = _mm_nn(on, w_hout1, wm_sq, D, D, tm, D, tn_d, "hgrn_out", res=h0)
    h2, ffn0, w_up0, w_down0, (fetch_sin, fetch_sout) = ffn_fwd(h1, 0, fetch_up0, ("sc_w_in", "sc_w_out"))
    xn1 = mix_in(h2, norm_mix[1:2])
    w_sin = fetch_sin(xn1)
    tn_si = _pick(cw_s, wide)
    sproj = _mm_nn(xn1, w_sin, _wmap_col(cw_s, tn_si, 0), D, 3 * D, tm, D, tn_si, "sc_in")
    fetch_up1, sproj = arrive("ffn_w_up1", sproj)
    nd = D // LANES
    ysc = _col_call(_sc_fwd_fn, [(sproj, 0), (sproj, nd), (sproj, 2 * nd)], [(sc_conv, 0)], [(1, D, BF16)], 0,
                    "sc_fwd", before=True, after=False)[0][0]
    w_sout1 = fetch_sout(ysc).reshape(1, D, D)
    h3 = _mm_nn(ysc, w_sout1, wm_sq, D, D, tm, D, tn_d, "sc_out", res=h2)
    h4, ffn1, w_up1, w_down1, _ = ffn_fwd(h3, 1, fetch_up1)

    dh, dh16, esq, dfinal = _row_call(_final_fn, [(h4, 0, D), (tgt, 0, D)], [final_norm], [(D, F32), (D, BF16)], 2,
                                      "final_loss")
    loss = 0.5 / D * jnp.sum(esq)
    dh, dh16, dnf1, dconv1 = ffn_bwd(dh, dh16, h3, ffn1, 1, w_up1, w_down1)
    g_sout = _mm_tn(ysc, seg1(dh16), (1, 1, D, D), wm_sq, D, D, tk_w, tn_d, "sc_out_dw")
    dh16 = reduce_start(("sc_w_out", 0), g_sout, dh16)
    dy = _mm_nt(seg1(dh16), w_sout1, wm_sq1, D, D, tm, D, D, "sc_out_dx")
    dsp, dscc = _col_call(_sc_bwd_fn, [(sproj, 0), (sproj, nd), (sproj, 2 * nd), (dy, 0)], [(sc_conv, 0)],
                          [(3, D, BF16)], 1, "sc_bwd", before=True, after=True)
    g_sin = _mm_tn(xn1, dsp, (N_CHIPS, 1, D, cw_s), _wmap_col(cw_s, tn_s, 0), D, 3 * D, tk_w, tn_s, "sc_in_dw")
    dsp = reduce_start(("sc_w_in", 0), g_sin, dsp)
    dxn = _mm_nt(dsp, w_sin, _wmap_col(cw_s, tn_s, 0), D, 3 * D, tm, D, tn_s, "sc_in_dx", per_step=3)
    dh, dh16, dnm1 = rms_bwd(h2, dxn, dh, norm_mix[1:2])
    dh16 = reduce_finish(dh16)
    dh, dh16, dnf0, dconv0 = ffn_bwd(dh, dh16, h1, ffn0, 0, w_up0, w_down0)
    g_hout = _mm_tn(on, seg1(dh16), (1, 1, D, D), wm_sq, D, D, tk_w, tn_d, "hgrn_out_dw")
    dh16 = reduce_start(("hgrn_w_out", 0), g_hout, dh16)
    don = _mm_nt(seg1(dh16), w_hout1, wm_sq1, D, D, tm, D, D, "hgrn_out_dx")
    do, dgate, dgain = _row_call(_onorm_bwd_fn, [(o, 0, D), (proj, 3, D), (don, 0, D)], [out_norm],
                                 [(D, F32), (D, BF16)], 1, "onorm_bwd")
    dproj, dlb = _hgrn_bwd(proj, lb, states, do, dgate, D)
    g_hin = _mm_tn(xn0, dproj, (N_CHIPS, 1, D, cw_h), _wmap_col(cw_h, tn_h, 0), D, 4 * D, tk_w, tn_h, "hgrn_in_dw")
    dproj = reduce_start(("hgrn_w_in", 0), g_hin, dproj)
    dxn = _mm_nt(dproj, w_hin, _wmap_col(cw_h, tn_h, 0), D, 4 * D, tm, D, tn_h, "hgrn_in_dx", per_step=2)
    grad_x, _, dnm0 = rms_bwd(h0, dxn, dh, norm_mix[0:1])

    small = dict(
        loss=loss,
        norm_mix=jnp.stack([jnp.sum(dnm0, axis=0), jnp.sum(dnm1, axis=0)]),
        norm_ffn=jnp.stack([jnp.sum(dnf0, axis=0), jnp.sum(dnf1, axis=0)]),
        lb=dlb[0:1],
        out_norm=jnp.sum(dgain, axis=0)[None],
        final_norm=jnp.sum(dfinal, axis=0),
        sc_conv=dscc[:3],
        ffn_conv=jnp.stack([dconv0, dconv1]),
    )
    return grad_x, small


def kernel(x, norm_mix, norm_ffn, hgrn_w_in, hgrn_lb_table, hgrn_out_norm, hgrn_w_out, sc_w_in, sc_conv, sc_w_out, ffn_w_up, ffn_conv, ffn_w_down, final_norm, loss_target, m_norm_mix, m_norm_ffn, m_hgrn_w_in, m_hgrn_lb_table, m_hgrn_out_norm, m_hgrn_w_out, m_sc_w_in, m_sc_conv, m_sc_w_out, m_ffn_w_up, m_ffn_conv, m_ffn_w_down, m_final_norm, v_norm_mix, v_norm_ffn, v_hgrn_w_in, v_hgrn_lb_table, v_hgrn_out_norm, v_hgrn_w_out, v_sc_w_in, v_sc_conv, v_sc_w_out, v_ffn_w_up, v_ffn_conv, v_ffn_w_down, v_final_norm):
    D = x.shape[-1]
    xi, yi, ci = lax.axis_index("x"), lax.axis_index("y"), lax.axis_index("c")
    me_chip = (2 * xi + yi).astype(jnp.int32).reshape(1)
    me_core = ci.astype(jnp.int32).reshape(1)

    big_names = ["hgrn_w_in", "hgrn_w_out", "sc_w_in", "sc_w_out", "ffn_w_up", "ffn_w_down"]
    big_w = dict(hgrn_w_in=hgrn_w_in, hgrn_w_out=hgrn_w_out, sc_w_in=sc_w_in, sc_w_out=sc_w_out,
                 ffn_w_up=ffn_w_up, ffn_w_down=ffn_w_down)
    big_m = dict(hgrn_w_in=m_hgrn_w_in, hgrn_w_out=m_hgrn_w_out, sc_w_in=m_sc_w_in, sc_w_out=m_sc_w_out,
                 ffn_w_up=m_ffn_w_up, ffn_w_down=m_ffn_w_down)
    big_v = dict(hgrn_w_in=v_hgrn_w_in, hgrn_w_out=v_hgrn_w_out, sc_w_in=v_sc_w_in, sc_w_out=v_sc_w_out,
                 ffn_w_up=v_ffn_w_up, ffn_w_down=v_ffn_w_down)
    flat2 = lambda a: a.reshape(-1, a.shape[-1])

    sh = lambda a, layer=0: _to_bf16(flat2(a), layer, a.shape[0])
    in_order_of_use = [("hgrn_w_in", sh(hgrn_w_in)), ("hgrn_w_out", sh(hgrn_w_out)), ("ffn_w_up0", sh(ffn_w_up, 0)),
                       ("ffn_w_down0", sh(ffn_w_down, 0)), ("sc_w_in", sh(sc_w_in)), ("sc_w_out", sh(sc_w_out)),
                       ("ffn_w_up1", sh(ffn_w_up, 1)), ("ffn_w_down1", sh(ffn_w_down, 1))]
    scc4, fcc4 = _gather_weights([], [flat2(sc_conv), flat2(ffn_conv)])
    names = [n for n, _ in in_order_of_use]
    ss, rs, src, land, (scc4, norm_mix) = _gather_start([s for _, s in in_order_of_use], [scc4, norm_mix], "gather_start")
    travelling = {n: (s, l) for n, s, l in zip(names, src, land)}

    def landed(name, after, call, which=None):
        (s,), (l,) = _gather_wait(ss, rs, [travelling[name][0]], [travelling[name][1]], after, call,
                                  base=names.index(name), which=which)
        travelling[name] = (s, l)
        return l

    scc = jnp.moveaxis(scc4, 0, 1).reshape(3, D)
    f2 = ffn_conv.shape[-1] * N_CHIPS
    fcc = jnp.moveaxis(fcc4.reshape(N_CHIPS, 2, 3, -1), 0, 2).reshape(2, 3, f2)

    def first(after, matmul):
        def pass_on(j, thru):
            w = landed("hgrn_w_in", thru, "gather_wait_0_%d" % j, which=(j,))
            fs, fr, w, thru = _forward_start(w, thru, "gather_forward_start_0_%d" % j, which=(j,))
            travelling["hgrn_w_in"] = (travelling["hgrn_w_in"][0], w)
            return (fs, fr), w, thru

        landed("hgrn_w_in", after, "gather_wait_0_own", which=(3,))
        others = [2 * (1 - xi) + yi, 2 * xi + (1 - yi), 2 * (1 - xi) + (1 - yi)]
        sems, w, after = pass_on(0, after)
        proj = matmul(after, w, me_chip, None)
        for j, s in enumerate(others):
            w = _forward_wait(*sems, w, proj, "gather_forward_wait_0_%d" % j, which=(j,))
            travelling["hgrn_w_in"] = (travelling["hgrn_w_in"][0], w)
            if j + 1 < len(others):
                sems, w, proj = pass_on(j + 1, proj)
            proj = matmul(after, w, s.astype(jnp.int32).reshape(1), proj)
        return proj, w, after

    def arrive(name, after):
        fs, fr, w, after = _forward_start(landed(name, after, "gather_wait_" + name), after, "gather_forward_start_" + name)
        return functools.partial(_forward_wait, fs, fr, w, name="gather_forward_wait_" + name), after

    pending = []
    started = []

    def reduce_start(slot, grad, thru):
        t = sum(len(b[0]) for b in pending) + len(started)
        halves = grad.reshape(N_CHIPS, 2, -1, grad.shape[-1])
        ss, rs, src, land, thru = _sibling_start([halves], True, thru, "grad_pair_start_%d" % t)
        started.append((slot, t, ss, rs, src, land))
        return thru

    def reduce_finish(thru):
        k = len(pending)
        pair = []
        for slot, t, ss, rs, src, land in started:
            src, recv = _sibling_wait(ss, rs, src, land, True, thru, "grad_pair_wait_%d" % t)
            pair.append(_add_pair(src[0], recv[0], me_core, "grad_add_pair"))
        ss, rs, pair, land, thru = _chip_start(pair, thru, "grad_chip_start_%d" % k)
        pending.append(([s[0] for s in started], ss, rs, pair, land))
        started.clear()
        return thru

    lb8 = _lb_softmax(hgrn_lb_table)
    grad_x, small = _local_step(
        x[0], loss_target[0], norm_mix, norm_ffn, lb8, hgrn_out_norm, final_norm[None], scc, fcc, first, arrive,
        reduce_start, reduce_finish)

    small_names = ["loss", "norm_mix", "norm_ffn", "lb", "out_norm", "final_norm", "sc_conv", "ffn_conv"]
    parts = [small[n].astype(F32) for n in small_names]
    shapes = [p.shape for p in parts]
    tot = dict(zip(small_names, _unpack(reduce_finish(_all_sum(_pack(parts))), shapes)))
    loss = tot["loss"].reshape(())
    g_lb_table = _lb_table_grad(lb8, tot["lb"], hgrn_lb_table.shape[0])
    cw = sc_conv.shape[-1]
    g_sc_conv = lax.dynamic_slice_in_dim(tot["sc_conv"], me_chip[0] * cw, cw, axis=1)[None]
    cf = ffn_conv.shape[-1]
    g_ffn_conv = lax.dynamic_slice_in_dim(tot["ffn_conv"], me_chip[0] * cf, cf, axis=2)
    g_small = dict(norm_mix=tot["norm_mix"], norm_ffn=tot["norm_ffn"], hgrn_lb_table=g_lb_table,
                   hgrn_out_norm=tot["out_norm"], sc_conv=g_sc_conv, ffn_conv=g_ffn_conv, final_norm=tot["final_norm"])
    w_small = dict(norm_mix=norm_mix, norm_ffn=norm_ffn, hgrn_lb_table=hgrn_lb_table, hgrn_out_norm=hgrn_out_norm,
                   sc_conv=sc_conv, ffn_conv=ffn_conv, final_norm=final_norm)
    m_small = dict(norm_mix=m_norm_mix, norm_ffn=m_norm_ffn, hgrn_lb_table=m_hgrn_lb_table, hgrn_out_norm=m_hgrn_out_norm,
                   sc_conv=m_sc_conv, ffn_conv=m_ffn_conv, final_norm=m_final_norm)
    v_small = dict(norm_mix=v_norm_mix, norm_ffn=v_norm_ffn, hgrn_lb_table=v_hgrn_lb_table, hgrn_out_norm=v_hgrn_out_norm,
                   sc_conv=v_sc_conv, ffn_conv=v_ffn_conv, final_norm=v_final_norm)
    sm_names = list(g_small)
    sm_shapes = [w_small[n].shape for n in sm_names]
    d_s, m_s, v_s = _adamw(_pack([w_small[n] for n in sm_names]), _pack([g_small[n] for n in sm_names]),
                           _pack([m_small[n] for n in sm_names]), _pack([v_small[n] for n in sm_names]), "adamw_small")
    out_g, out_d, out_m, out_v = dict(g_small), {}, {}, {}
    for n, d_, m_, v_ in zip(sm_names, _unpack(d_s, sm_shapes), _unpack(m_s, sm_shapes), _unpack(v_s, sm_shapes)):
        out_d[n], out_m[n], out_v[n] = d_, m_, v_

    done = {}
    after = grad_x

    def add_and_share(k, after):
        slots, ss, rs, pair, land = pending[k]
        pair, recv = _chip_wait(ss, rs, pair, land, after, "grad_chip_wait_%d" % k)
        mine = [_add_chips(p, r, me_chip, "grad_add_chips") for p, r in zip(pair, recv)]
        ss, rs, mine, land, _ = _sibling_start(mine, False, None, "grad_share_start_%d" % k)
        return slots, ss, rs, mine, land

    def update(k, share, after):
        slots, ss, rs, mine, land = share
        mine, theirs = _sibling_wait(ss, rs, mine, land, False, after, "grad_share_wait_%d" % k)
        for (n, layer), gm, gr in zip(slots, mine, theirs):
            done[n] = _adamw_halves(flat2(big_w[n]), flat2(big_m[n]), flat2(big_v[n]), gm, gr, me_core, "adamw_" + n,
                                    layer=layer, prev=done.get(n))
        return done[slots[-1][0]][0]

    shares = []
    for k in range(len(pending) - 1):
        shares.append(add_and_share(k, after))
        after = shares[-1][3][0]
    for k, share in enumerate(shares):
        after = update(k, share, after)
    last = len(pending) - 1
    share = add_and_share(last, after)
    update(last, share, share[3][0])
    for n in big_names:
        out_g[n], out_d[n], out_m[n], out_v[n] = (a.reshape(big_w[n].shape) for a in done[n])

    order = ["norm_mix", "norm_ffn", "hgrn_w_in", "hgrn_lb_table", "hgrn_out_norm", "hgrn_w_out", "sc_w_in", "sc_conv",
             "sc_w_out", "ffn_w_up", "ffn_conv", "ffn_w_down", "final_norm"]
    return (loss, grad_x[None], *[out_g[n] for n in order], *[out_d[n] for n in order],
            *[out_m[n] for n in order], *[out_v[n] for n in order])
```

```python
import functools

import jax
import jax.numpy as jnp
from jax import lax
from jax.experimental import pallas as pl
from jax.experimental.pallas import tpu as pltpu

F32 = jnp.float32
BF16 = jnp.bfloat16
MESH = pl.DeviceIdType.MESH

EPS = 1e-6
CHUNK = 64
HEAD = 128
N_CHIPS = 4
ADAM_LR, ADAM_B1, ADAM_B2, ADAM_EPS, ADAM_WD, ADAM_STEP = 0.001, 0.9, 0.999, 1e-08, 0.01, 10
VMEM_LIMIT = 56 * 1024 * 1024
SUBLANES = 8
LANES = 128


def _pcall(body, **kw):
    return pl.pallas_call(body, **kw)


def _params(sem, vmem=VMEM_LIMIT):
    return pltpu.CompilerParams(dimension_semantics=sem, vmem_limit_bytes=vmem)


def _pick(dim, prefs):
    for p in prefs:
        if p <= dim and dim % p == 0:
            return p
    return dim


def _sigmoid(x):
    return 1.0 / (1.0 + jnp.exp(-x))


def _wmap_col(cw, tn, r0):
    bps = cw // tn
    return lambda kb, nb: (nb // bps, r0 + kb, nb % bps)


def _wmap_row(kp, tk, r0):
    bps = kp // tk
    return lambda kb, nb: (kb // bps, r0 + kb % bps, nb)


def _mm_nn(a, w3, wmap, K, N, tm, tk, tn, name, res=None, per_step=1):
    M = a.shape[0]
    u = per_step
    nk = K // (tk * u)

    def body(*refs):
        r_ref = None if res is None else refs[2 * u]
        o_ref = refs[2 * u + (0 if res is None else 1)]
        p = None
        for r in range(u):
            d = jnp.dot(refs[r][...], refs[u + r][...], preferred_element_type=F32)
            p = d if p is None else p + d
        if nk == 1:
            o_ref[...] = p if res is None else p + r_ref[...]
            return
        acc = refs[-1]
        k = pl.program_id(2)

        @pl.when(k == 0)
        def _():
            acc[...] = p

        @pl.when(k > 0)
        def _():
            acc[...] += p

        @pl.when(k == nk - 1)
        def _():
            o_ref[...] = acc[...] if res is None else acc[...] + r_ref[...]

    if nk == 1:
        grid = (M // tm, N // tn)
        ix = lambda f: (lambda i, j: f(i, j, 0))
        sem = ("parallel", "parallel")
        scratch = []
    else:
        grid = (M // tm, N // tn, nk)
        ix = lambda f: f
        sem = ("parallel", "parallel", "arbitrary")
        scratch = [pltpu.VMEM((tm, tn), F32)]
    def a_spec(r):
        return pl.BlockSpec((tm, tk), ix(lambda i, j, k: (i, k * u + r)))

    def w_spec(r):
        return pl.BlockSpec((None, tk, tn), ix(lambda i, j, k: wmap(k * u + r, j)))

    in_specs = [a_spec(r) for r in range(u)] + [w_spec(r) for r in range(u)]
    args = [a] * u + [w3] * u
    if res is not None:
        in_specs.append(pl.BlockSpec((tm, tn), ix(lambda i, j, k: (i, j))))
        args.append(res)
    return _pcall(
        body, name=name, grid=grid, in_specs=in_specs,
        out_specs=pl.BlockSpec((tm, tn), ix(lambda i, j, k: (i, j))),
        out_shape=jax.ShapeDtypeStruct((M, N), F32), scratch_shapes=scratch, compiler_params=_params(sem),
    )(*args)


def _mm_nn_shard(a, w3, s, tm, tn, name, prev=None):
    M, K = a.shape
    S, _, cw = w3.shape
    bps = cw // tn

    def body(s_ref, a_ref, w_ref, *rest):
        o_ref = rest[-1]
        o_ref[...] = jnp.dot(a_ref[...], w_ref[...], preferred_element_type=F32)

    in_specs = [pl.BlockSpec((tm, K), lambda i, j, sr: (i, 0)),
                pl.BlockSpec((None, K, tn), lambda i, j, sr: (sr[0], 0, j))]
    args = [s, a, w3]
    alias = {}
    if prev is not None:
        in_specs.append(pl.BlockSpec(memory_space=pl.ANY))
        args.append(prev)
        alias = {3: 0}
    return _pcall(
        body, name=name,
        grid_spec=pltpu.PrefetchScalarGridSpec(
            num_scalar_prefetch=1, grid=(M // tm, bps), in_specs=in_specs,
            out_specs=pl.BlockSpec((tm, tn), lambda i, j, sr: (i, sr[0] * bps + j))),
        out_shape=jax.ShapeDtypeStruct((M, S * cw), F32), input_output_aliases=alias,
        compiler_params=_params(("parallel", "parallel")),
    )(*args)


def _mm_nt(dy3, w3, wmap, K, N, tm, tk, tn, name, per_step=1):
    M = dy3.shape[1]
    bps = dy3.shape[2] // tn
    u = per_step
    grid = (M // tm, K // tk, N // (tn * u))
    nn = grid[2]

    def body(*refs):
        o_ref = refs[-1]
        p = None
        for r in range(u):
            d = lax.dot_general(refs[r][...], refs[u + r][...], (((1,), (1,)), ((), ())), preferred_element_type=F32)
            p = d if p is None else p + d
        if nn == 1:
            o_ref[...] = p
            return
        n = pl.program_id(2)

        @pl.when(n == 0)
        def _():
            o_ref[...] = p

        @pl.when(n > 0)
        def _():
            o_ref[...] += p

    def dy_spec(r):
        return pl.BlockSpec((None, tm, tn), lambda i, j, n: ((n * u + r) // bps, i, (n * u + r) % bps))

    def w_spec(r):
        return pl.BlockSpec((None, tk, tn), lambda i, j, n: wmap(j, n * u + r))

    return _pcall(
        body, name=name, grid=grid,
        in_specs=[dy_spec(r) for r in range(u)] + [w_spec(r) for r in range(u)],
        out_specs=pl.BlockSpec((tm, tk), lambda i, j, n: (i, j)),
        out_shape=jax.ShapeDtypeStruct((M, K), F32),
        compiler_params=_params(("parallel", "parallel", "arbitrary")),
    )(*([dy3] * u), *([w3] * u))


def _mm_tn(x, dy3, shape4, wmap, K, N, tk, tn, name, tm=None):
    M = x.shape[0]
    tm = M if tm is None else tm
    nm = M // tm
    bps = dy3.shape[2] // tn

    def body(*refs):
        x_ref, dy_ref = refs[:2]
        p = lax.dot_general(x_ref[...], dy_ref[...], (((0,), (0,)), ((), ())), preferred_element_type=F32)
        if nm == 1:
            o_ref = refs[-1]
            o_ref[...] = p.astype(o_ref.dtype)
            return
        o_ref, acc = refs[-2:]
        m = pl.program_id(2)

        @pl.when(m == 0)
        def _():
            acc[...] = p

        @pl.when(m > 0)
        def _():
            acc[...] += p

        @pl.when(m == nm - 1)
        def _():
            o_ref[...] = acc[...].astype(o_ref.dtype)

    def omap(i, j, m):
        s, rb, cb = wmap(i, j)
        return (s, 0, rb, cb)

    return _pcall(
        body, name=name, grid=(K // tk, N // tn, nm),
        in_specs=[pl.BlockSpec((tm, tk), lambda i, j, m: (m, i)),
                  pl.BlockSpec((None, tm, tn), lambda i, j, m: (j // bps, m, j % bps))],
        out_specs=pl.BlockSpec((None, None, tk, tn), omap),
        out_shape=jax.ShapeDtypeStruct(shape4, BF16),
        scratch_shapes=[] if nm == 1 else [pltpu.VMEM((tk, tn), F32)],
        compiler_params=_params(("parallel", "parallel", "arbitrary")),
    )(x, dy3)


def _row_call(fn, rows, vecs, outs, n_acc, name, t_rows=256, sub=16, per_trip=4):
    T = rows[0][0].shape[0]
    t_rows = min(t_rows, T)
    nsub = t_rows // sub
    n_r, n_v, n_o = len(rows), len(vecs), len(outs)
    width = rows[0][2]

    def body(*refs):
        r_refs = refs[:n_r]
        v_refs = refs[n_r:n_r + n_v]
        o_refs = refs[n_r + n_v:n_r + n_v + n_o]
        a_refs = refs[n_r + n_v + n_o:]

        @pl.when(pl.program_id(0) == 0)
        def _():
            for a in a_refs:
                a[...] = jnp.zeros_like(a)

        vv = [v[...] for v in v_refs]

        def step(i, carry):
            done = []
            for u in range(per_trip):
                sl = pl.ds(pl.multiple_of((i * per_trip + u) * sub, sub), sub)
                done.append((sl,) + tuple(fn([r[sl, :] for r in r_refs], vv)))
            for sl, o_vals, a_vals in done:
                for o, val in zip(o_refs, o_vals):
                    o[sl, :] = val.astype(o.dtype)
            for a_i, a in enumerate(a_refs):
                tot = None
                for _, _, a_vals in done:
                    part = a_vals[a_i].reshape(sub // SUBLANES, SUBLANES, a_vals[a_i].shape[-1]).sum(axis=0)
                    tot = part if tot is None else tot + part
                a[...] += tot
            return carry

        lax.fori_loop(0, nsub // per_trip, step, 0)

    in_specs = [pl.BlockSpec((t_rows, w), functools.partial(lambda i, cb: (i, cb), cb=cb)) for _, cb, w in rows]
    in_specs += [pl.BlockSpec(v.shape, lambda i: (0, 0)) for v in vecs]
    out_specs = [pl.BlockSpec((t_rows, w), lambda i: (i, 0)) for w, _ in outs]
    out_specs += [pl.BlockSpec((SUBLANES, width), lambda i: (0, 0)) for _ in range(n_acc)]
    out_shape = [jax.ShapeDtypeStruct((T, w), dt) for w, dt in outs]
    out_shape += [jax.ShapeDtypeStruct((SUBLANES, width), F32) for _ in range(n_acc)]
    return _pcall(
        body, name=name, grid=(T // t_rows,), in_specs=in_specs, out_specs=out_specs, out_shape=out_shape,
        compiler_params=_params(("arbitrary",)),
    )(*[r[0] for r in rows], *vecs)


def _rms_fwd_fn(rv, vv):
    h, = rv
    w, = vv
    r = lax.rsqrt(jnp.mean(h * h, axis=-1, keepdims=True) + EPS)
    return [h * r * w], []


def _rms_bwd_fn(rv, vv):
    h, dxn, dh_in = rv
    w, = vv
    d = h.shape[-1]
    r = lax.rsqrt(jnp.mean(h * h, axis=-1, keepdims=True) + EPS)
    gy = dxn * w
    dh = r * gy - h * ((r * r * r) * (1.0 / d) * jnp.sum(gy * h, axis=-1, keepdims=True))
    return [dh_in + dh] * 2, [dxn * h * r]


def _final_fn(rv, vv):
    h, tgt = rv
    w, = vv
    d = h.shape[-1]
    r = lax.rsqrt(jnp.mean(h * h, axis=-1, keepdims=True) + EPS)
    hn = h * r
    e = hn * w - tgt
    dy = e * (1.0 / d)
    gy = dy * w
    dh = r * gy - h * ((r * r * r) * (1.0 / d) * jnp.sum(gy * h, axis=-1, keepdims=True))
    return [dh] * 2, [e * e, dy * hn]


def _onorm_fwd_fn(rv, vv):
    o, g = rv
    gain, = vv
    r = lax.rsqrt(jnp.mean(o * o, axis=-1, keepdims=True) + EPS)
    return [o * r * gain * (g * _sigmoid(g))], []


def _onorm_bwd_fn(rv, vv):
    o, g, don = rv
    gain, = vv
    d = o.shape[-1]
    r = lax.rsqrt(jnp.mean(o * o, axis=-1, keepdims=True) + EPS)
    sg = _sigmoid(g)
    sl = g * sg
    n = o * r
    dg = don * n * gain * (sg * (1.0 + g * (1.0 - sg)))
    gy = don * sl * gain
    do = r * gy - o * ((r * r * r) * (1.0 / d) * jnp.sum(gy * o, axis=-1, keepdims=True))
    return [do, dg], [don * sl * n]


HALO = SUBLANES


def _col_call(fn, cols, vecs, outs, n_acc, name, before, after, tc=LANES, chunk=256):
    T = cols[0][0].shape[0]
    chunk = min(chunk, T)
    nch = T // chunk
    ncol = outs[0][1] // tc
    n_c, n_v, n_o = len(cols), len(vecs), len(outs)
    hb = HALO if before else 0
    rw = chunk + hb + (HALO if after else 0)

    def body(*refs):
        c_refs = refs[:n_c]
        v_refs = refs[n_c:n_c + n_v]
        o_refs = refs[n_c + n_v:n_c + n_v + n_o]
        a_refs = refs[n_c + n_v + n_o:]
        vv = [v[...] for v in v_refs]
        wrow = lax.broadcasted_iota(jnp.int32, (rw, tc), 0)
        inside = (wrow >= hb) & (wrow < hb + chunk)

        def step(i, carry):
            r0 = pl.multiple_of(i * chunk, chunk)
            wins = []
            for ref in c_refs:
                parts = []
                if before:
                    pb = ref[pl.ds(pl.multiple_of(jnp.maximum(r0 - HALO, 0), HALO), HALO), :]
                    parts.append(jnp.where(i > 0, pb, 0.0))
                parts.append(ref[pl.ds(r0, chunk), :])
                if after:
                    pa = ref[pl.ds(pl.multiple_of(jnp.minimum(r0 + chunk, T - HALO), HALO), HALO), :]
                    parts.append(jnp.where(i < nch - 1, pa, 0.0))
                wins.append(jnp.concatenate(parts, axis=0) if len(parts) > 1 else parts[0])
            o_vals, a_vals = fn(wins, vv, inside)
            p = 0
            for o, (nseg, _, _) in zip(o_refs, outs):
                for s in range(nseg):
                    o[s, pl.ds(r0, chunk), :] = o_vals[p][hb:hb + chunk].astype(o.dtype)
                    p += 1
            return tuple(c + a for c, a in zip(carry, a_vals))

        taps = [v.shape[0] for v, _ in vecs][:n_acc]
        init = tuple(jnp.zeros((1, tc), F32) for k in taps for _ in range(k))
        sums = lax.fori_loop(0, nch, step, init)
        arow = lax.broadcasted_iota(jnp.int32, (SUBLANES, tc), 0)
        p = 0
        for a, k in zip(a_refs, taps):
            acc = jnp.zeros((SUBLANES, tc), F32)
            for t in range(k):
                acc = jnp.where(arow == t, sums[p], acc)
                p += 1
            a[...] = acc

    in_specs = [pl.BlockSpec((T, tc), functools.partial(lambda j, off: (0, off + j), off=off)) for _, off in cols]
    in_specs += [pl.BlockSpec((v.shape[0], tc), functools.partial(lambda j, off: (0, off + j), off=off))
                 for v, off in vecs]
    out_specs = [pl.BlockSpec((nseg, T, tc), lambda j: (0, 0, j)) for nseg, _, _ in outs]
    out_specs += [pl.BlockSpec((SUBLANES, tc), lambda j: (0, j)) for _ in range(n_acc)]
    out_shape = [jax.ShapeDtypeStruct((nseg, T, w), dt) for nseg, w, dt in outs]
    out_shape += [jax.ShapeDtypeStruct((SUBLANES, ncol * tc), F32) for _ in range(n_acc)]
    return _pcall(
        body, name=name, grid=(ncol,), in_specs=in_specs, out_specs=out_specs, out_shape=out_shape,
        compiler_params=_params(("parallel",)),
    )(*[c[0] for c in cols], *[v[0] for v in vecs])


def _down(x, k):
    return x if k == 0 else pltpu.roll(x, k, 0)


def _up(x, k):
    return x if k == 0 else pltpu.roll(x, x.shape[0] - k, 0)


def _lags(x):
    return _down(x, 2), _down(x, 1), x


def _conv(lags, w):
    return w[0:1] * lags[0] + w[1:2] * lags[1] + w[2:3] * lags[2]


def _conv_t(d, w):
    return w[2:3] * d + w[1:2] * _up(d, 1) + w[0:1] * _up(d, 2)


def _tap_sums(d, lags, inside):
    dm = jnp.where(inside, d, 0.0)
    return [jnp.sum(dm * lag, axis=0, keepdims=True) for lag in lags]


def _glu_fwd_fn(wins, vv, inside):
    xg, xv = wins
    wg, wv = vv
    ug = _conv(_lags(xg), wg)
    uv = _conv(_lags(xv), wv)
    return [ug * _sigmoid(ug) * uv], []


def _glu_bwd_fn(wins, vv, inside):
    xg, xv, da = wins
    wg, wv = vv
    lg, lv = _lags(xg), _lags(xv)
    ug = _conv(lg, wg)
    uv = _conv(lv, wv)
    sg = _sigmoid(ug)
    dug = da * uv * (sg * (1.0 + ug * (1.0 - sg)))
    duv = da * (ug * sg)
    return [_conv_t(dug, wg), _conv_t(duv, wv)], _tap_sums(dug, lg, inside) + _tap_sums(duv, lv, inside)


def _sc_fwd_fn(wins, vv, inside):
    gb, gc, hh = wins
    w, = vv
    return [gb * _conv(_lags(gc * hh), w)], []


def _sc_bwd_fn(wins, vv, inside):
    gb, gc, hh, dy = wins
    w, = vv
    lz = _lags(gc * hh)
    dcv = dy * gb
    dz = _conv_t(dcv, w)
    return [dy * _conv(lz, w), dz * hh, dz * gc], _tap_sums(dcv, lz, inside)


def _gates(qr, fr, lb):
    sg = _sigmoid(fr)
    f = lb + (1.0 - lb) * sg
    sq = _sigmoid(qr)
    q = qr * sq * (HEAD ** -0.5)
    return q, 1.0 - f, jnp.log(f), f, sg, sq


def _boundary_rows(b, g, row):
    c = b.shape[0]
    if 2 * g >= SUBLANES:
        x = b.reshape(c // (2 * g), 2 * g, LANES)
        return jnp.broadcast_to(x[:, g - 1:g, :], x.shape).reshape(c, LANES)
    x = b.reshape(c // SUBLANES, SUBLANES, LANES)
    lo = jnp.broadcast_to(x[:, 1:2, :], x.shape).reshape(c, LANES)
    hi = jnp.broadcast_to(x[:, 5:6, :], x.shape).reshape(c, LANES)
    return jnp.where((row & 4) == 0, lo, hi)


def _chunk_decays(gl, f, row):
    c = gl.shape[0]
    b = gl
    d = 1
    while d < c:
        b = b + jnp.where(row >= d, pltpu.roll(b, d, 0), 0.0)
        d *= 2
    eq, ek = [], []
    g = c // 2
    while g >= 2:
        right = (row & g) != 0
        m = _boundary_rows(b, g, row)
        z = jnp.exp(jnp.where(right, b - m, m - b))
        eq.append(jnp.where(right, z, 0.0))
        ek.append(jnp.where(right, 0.0, z))
        g //= 2
    odd = (row & 1) != 0
    eq.append(jnp.where(odd, f, 0.0))
    ek.append(jnp.where(odd, 0.0, 1.0))
    return b, eq, ek


def _intra(q, k, eq, ek, tt, ss):
    c = q.shape[0]
    qs, ks = [], []
    a = jnp.where(tt == ss, jnp.sum(q * k, axis=1, keepdims=True), 0.0)
    g = c // 2
    for e_q, e_k in zip(eq, ek):
        qg = (q * e_q).astype(BF16)
        kg = (k * e_k).astype(BF16)
        p = lax.dot_general(qg, kg, (((1,), (1,)), ((), ())), preferred_element_type=F32)
        a = a + (p if 2 * g >= c else jnp.where((tt ^ ss) < 2 * g, p, 0.0))
        qs.append(qg)
        ks.append(kg)
        g //= 2
    return a, qs, ks


def _hgrn_fwd(proj, lb, d_model):
    T = proj.shape[0]
    H = d_model // HEAD
    nch = T // CHUNK

    def body(q_ref, f_ref, v_ref, lb_ref, o_ref, s_ref):
        lbv = lb_ref[...]
        row = lax.broadcasted_iota(jnp.int32, (CHUNK, HEAD), 0)
        tt = lax.broadcasted_iota(jnp.int32, (CHUNK, CHUNK), 0)
        ss = lax.broadcasted_iota(jnp.int32, (CHUNK, CHUNK), 1)

        def step(i, st):
            sl = pl.ds(pl.multiple_of(i * CHUNK, CHUNK), CHUNK)
            q, k, gl, f, _, _ = _gates(q_ref[sl, :], f_ref[sl, :], lbv)
            v = v_ref[sl, :].astype(BF16)
            b, eq, ek = _chunk_decays(gl, f, row)
            a, _, _ = _intra(q, k, eq, ek, tt, ss)
            bl = b[CHUNK - 1:CHUNK, :]
            q0 = (q * jnp.exp(b)).astype(BF16)
            kh = (k * jnp.exp(bl - b)).astype(BF16)
            s_ref[i] = st
            o = jnp.dot(a.astype(BF16), v, preferred_element_type=F32)
            o = o + lax.dot_general(q0, st.astype(BF16), (((1,), (1,)), ((), ())), preferred_element_type=F32)
            o_ref[sl, :] = o
            return jnp.exp(bl) * st + lax.dot_general(v, kh, (((0,), (0,)), ((), ())), preferred_element_type=F32)

        per = 4 if nch % 4 == 0 else 2

        def trip(i, st):
            for u in range(per):
                st = step(per * i + u, st)
            return st

        lax.fori_loop(0, nch // per, trip, jnp.zeros((HEAD, HEAD), F32))

    col = lambda off: pl.BlockSpec((T, HEAD), functools.partial(lambda h, off: (0, off + h), off=off))
    return _pcall(
        body, name="hgrn_fwd", grid=(H,),
        in_specs=[col(0), col(H), col(2 * H), pl.BlockSpec((1, HEAD), lambda h: (0, h))],
        out_specs=[pl.BlockSpec((T, HEAD), lambda h: (0, h)),
                   pl.BlockSpec((None, nch, HEAD, HEAD), lambda h: (h, 0, 0, 0))],
        out_shape=[jax.ShapeDtypeStruct((T, d_model), F32), jax.ShapeDtypeStruct((H, nch, HEAD, HEAD), F32)],
        compiler_params=_params(("parallel",)),
    )(proj, proj, proj, lb)


def _hgrn_bwd(proj, lb, states, do, dgate, d_model):
    T = proj.shape[0]
    H = d_model // HEAD
    nch = T // CHUNK

    def body(q_ref, f_ref, v_ref, lb_ref, s_ref, do_ref, dg_ref, dp_ref, dlb_ref):
        dq_ref, df_ref, dv_ref = dp_ref.at[0], dp_ref.at[1], dp_ref.at[2]
        dp_ref[3] = dg_ref[...]
        lbv = lb_ref[...]
        row = lax.broadcasted_iota(jnp.int32, (CHUNK, HEAD), 0)
        tt = lax.broadcasted_iota(jnp.int32, (CHUNK, CHUNK), 0)
        ss = lax.broadcasted_iota(jnp.int32, (CHUNK, CHUNK), 1)
        last = row == CHUNK - 1
        nt = (((1,), (1,)), ((), ()))
        tn = (((0,), (0,)), ((), ()))

        def step(j, carry):
            dst, dlb = carry
            i = nch - 1 - j
            sl = pl.ds(pl.multiple_of(i * CHUNK, CHUNK), CHUNK)
            qr = q_ref[sl, :]
            q, k, gl, f, sg, sq = _gates(qr, f_ref[sl, :], lbv)
            v = v_ref[sl, :].astype(BF16)
            d_o = do_ref[sl, :].astype(BF16)
            st = s_ref[i]
            st16 = st.astype(BF16)
            dst16 = dst.astype(BF16)
            b, eq, ek = _chunk_decays(gl, f, row)
            a, qs, ks = _intra(q, k, eq, ek, tt, ss)
            bl = b[CHUNK - 1:CHUNK, :]
            e0 = jnp.exp(b)
            eh = jnp.exp(bl - b)
            ebl = jnp.exp(bl)
            q0 = q * e0
            kh = k * eh
            q016 = q0.astype(BF16)
            kh16 = kh.astype(BF16)
            dv = lax.dot_general(a.astype(BF16), d_o, tn, preferred_element_type=F32)
            dv = dv + lax.dot_general(kh16, dst16, nt, preferred_element_type=F32)
            dv_ref[sl, :] = dv.astype(dv_ref.dtype)
            da = lax.dot_general(d_o, v, nt, preferred_element_type=F32)
            da = jnp.where(tt >= ss, da, 0.0)
            dd = jnp.sum(jnp.where(tt == ss, da, 0.0), axis=1, keepdims=True)
            dq0 = jnp.dot(d_o, st16, preferred_element_type=F32)
            dkh = jnp.dot(v, dst16, preferred_element_type=F32)
            dq = dq0 * e0 + dd * k
            dk = dkh * eh + dd * q
            db = dq0 * q016.astype(F32) - dkh * kh16.astype(F32)
            g = CHUNK // 2
            for e_q, e_k, qg, kg in zip(eq, ek, qs, ks):
                dag = (da if 2 * g >= CHUNK else jnp.where((tt ^ ss) < 2 * g, da, 0.0)).astype(BF16)
                dqg = jnp.dot(dag, kg, preferred_element_type=F32)
                dkg = lax.dot_general(dag, qg, tn, preferred_element_type=F32)
                dq = dq + dqg * e_q
                dk = dk + dkg * e_k
                db = db + (dqg * qg.astype(F32) - dkg * kg.astype(F32))
                g //= 2
            dbl = jnp.sum(dkh * kh16.astype(F32), axis=0, keepdims=True) + ebl * jnp.sum(dst * st, axis=0, keepdims=True)
            db = db + jnp.where(last, dbl, 0.0)
            d = 1
            while d < CHUNK:
                db = db + jnp.where(row < CHUNK - d, pltpu.roll(db, CHUNK - d, 0), 0.0)
                d *= 2
            dfg = db / f - dk
            df_ref[sl, :] = (dfg * (1.0 - lbv) * sg * (1.0 - sg)).astype(df_ref.dtype)
            dq_ref[sl, :] = (dq * (HEAD ** -0.5) * (sq * (1.0 + qr * (1.0 - sq)))).astype(dq_ref.dtype)
            dlb = dlb + jnp.sum(dfg * (1.0 - sg), axis=0, keepdims=True)
            dst = ebl * dst + lax.dot_general(d_o, q016, tn, preferred_element_type=F32)
            return dst, dlb

        _, dlb = lax.fori_loop(0, nch // 2, lambda j, cr: step(2 * j + 1, step(2 * j, cr)),
                               (jnp.zeros((HEAD, HEAD), F32), jnp.zeros((1, HEAD), F32)))
        arow = lax.broadcasted_iota(jnp.int32, (SUBLANES, HEAD), 0)
        dlb_ref[...] = jnp.where(arow == 0, dlb, 0.0)

    col = lambda off: pl.BlockSpec((T, HEAD), functools.partial(lambda h, off: (0, off + h), off=off))
    return _pcall(
        body, name="hgrn_bwd", grid=(H,),
        in_specs=[col(0), col(H), col(2 * H), pl.BlockSpec((1, HEAD), lambda h: (0, h)),
                  pl.BlockSpec((None, nch, HEAD, HEAD), lambda h: (h, 0, 0, 0)), col(0), col(0)],
        out_specs=[pl.BlockSpec((4, T, HEAD), lambda h: (0, 0, h)), pl.BlockSpec((SUBLANES, HEAD), lambda h: (0, h))],
        out_shape=[jax.ShapeDtypeStruct((4, T, d_model), BF16), jax.ShapeDtypeStruct((SUBLANES, d_model), F32)],
        compiler_params=_params(("parallel",)),
    )(proj, proj, proj, lb, states, do, dgate)


def _lb_softmax(table):
    n, f = table.shape

    def body(t_ref, p_ref):
        t = t_ref[...]
        e = jnp.exp(t - jnp.max(t, axis=0, keepdims=True))
        p_ref[...] = e / jnp.sum(e, axis=0, keepdims=True)

    padded = jnp.pad(table, ((0, SUBLANES - n), (0, 0)), constant_values=-jnp.inf)
    return _pcall(body, name="lb_softmax", out_shape=jax.ShapeDtypeStruct((SUBLANES, f), F32))(padded)


def _to_bf16(w, layer, nlayers):
    R = w.shape[0] // nlayers
    C = w.shape[1]
    tr = _pick(R, (256, 128, 64, 32, 16))
    nb = R // tr

    def body(w_ref, o_ref):
        o_ref[...] = w_ref[...].astype(o_ref.dtype)

    return _pcall(
        body, name="to_bf16", grid=(nb,), in_specs=[pl.BlockSpec((tr, C), lambda i: (layer * nb + i, 0))],
        out_specs=pl.BlockSpec((tr, C), lambda i: (i, 0)), out_shape=jax.ShapeDtypeStruct((R, C), BF16),
        compiler_params=_params(("parallel",)),
    )(w)


def _adamw_math(w, g, m, v):
    m = ADAM_B1 * m + (1.0 - ADAM_B1) * g
    v = ADAM_B2 * v + (1.0 - ADAM_B2) * (g * g)
    m_hat = m / (1.0 - ADAM_B1 ** ADAM_STEP)
    v_hat = v / (1.0 - ADAM_B2 ** ADAM_STEP)
    delta = -ADAM_LR * (m_hat / (jnp.sqrt(v_hat) + ADAM_EPS) + ADAM_WD * w)
    return delta, m, v


def _adamw(w, g, m, v, name):
    R, C = w.shape
    tr = _pick(R, (128, 64, 32, 16, 8))

    def body(w_ref, g_ref, m_ref, v_ref, d_ref, nm_ref, nv_ref):
        d, nm, nv = _adamw_math(w_ref[...], g_ref[...], m_ref[...], v_ref[...])
        d_ref[...] = d
        nm_ref[...] = nm
        nv_ref[...] = nv

    spec = pl.BlockSpec((tr, C), lambda i: (i, 0))
    return _pcall(
        body, name=name, grid=(R // tr,), in_specs=[spec] * 4, out_specs=[spec] * 3,
        out_shape=[jax.ShapeDtypeStruct((R, C), F32)] * 3, compiler_params=_params(("parallel",)),
    )(w, g, m, v)


def _adamw_halves(w, m, v, g_mine, g_recv, c, name, layer=0, prev=None):
    C = w.shape[1]
    rh = g_mine.shape[0]
    tr = _pick(rh, (128, 64, 32, 16, 8))
    nb = rh // tr
    r0 = layer * 2 * nb

    def body(c_ref, w_ref, m_ref, v_ref, gm_ref, gr_ref, *rest):
        g_ref, d_ref, nm_ref, nv_ref = rest[-4:]
        g = jnp.where(pl.program_id(0) == c_ref[0], gm_ref[...], gr_ref[...])
        d, nm, nv = _adamw_math(w_ref[...], g, m_ref[...], v_ref[...])
        g_ref[...] = g
        d_ref[...] = d
        nm_ref[...] = nm
        nv_ref[...] = nv

    full = pl.BlockSpec((tr, C), lambda h, i, cr: (r0 + h * nb + i, 0))
    mine = pl.BlockSpec((tr, C), lambda h, i, cr: (jnp.where(h == cr[0], i, 0), 0))
    recv = pl.BlockSpec((tr, C), lambda h, i, cr: (jnp.where(h == cr[0], 0, i), 0))
    in_specs = [full, full, full, mine, recv]
    args = [c, w, m, v, g_mine, g_recv]
    alias = {}
    if prev is not None:
        in_specs += [pl.BlockSpec(memory_space=pl.ANY)] * 4
        args += list(prev)
        alias = {6 + k: k for k in range(4)}
    return _pcall(
        body, name=name,
        grid_spec=pltpu.PrefetchScalarGridSpec(
            num_scalar_prefetch=1, grid=(2, nb), in_specs=in_specs, out_specs=[full] * 4),
        out_shape=[jax.ShapeDtypeStruct(w.shape, F32)] * 4, input_output_aliases=alias,
        compiler_params=_params(("parallel", "parallel")),
    )(*args)


def _lb_table_grad(p8, dlb, n):
    f = p8.shape[1]

    def body(p_ref, d_ref, o_ref):
        p = p_ref[...]
        d = d_ref[...]
        p0 = p[0:1, :]
        first = lax.broadcasted_iota(jnp.int32, p.shape, 0) == 0
        o_ref[...] = p * (jnp.where(first, d, 0.0) - d * p0)

    return _pcall(body, name="lb_table_grad", out_shape=jax.ShapeDtypeStruct((SUBLANES, f), F32))(p8, dlb)[:n]


def _place():
    x, y, c = lax.axis_index("x"), lax.axis_index("y"), lax.axis_index("c")
    chips = [(1 - x, y), (x, 1 - y), (1 - x, 1 - y)]
    return x, y, c, chips


HBM_SPEC = pl.BlockSpec(memory_space=pltpu.HBM)


def _gather_taps(shards):
    n = len(shards)

    def body(*refs):
        ins, outs = refs[:n], refs[n:2 * n]
        send_sems, recv_sems = refs[2 * n:]
        x, y, c, chips = _place()
        me = 2 * x + y
        peers = [(*chip, c) for chip in chips] + [(x, y, 1 - c)]
        sends = [pltpu.make_async_remote_copy(
            src_ref=ins[t], dst_ref=outs[t].at[me], send_sem=send_sems.at[4 * t + j], recv_sem=recv_sems.at[4 * t + j],
            device_id=peer, device_id_type=MESH) for t in range(n) for j, peer in enumerate(peers)]
        for cp in sends:
            cp.start()
        for t in range(n):
            for j, (px, py, _) in enumerate(peers):
                landed = outs[t].at[2 * px + py]
                pltpu.make_async_remote_copy(
                    src_ref=landed, dst_ref=landed, send_sem=send_sems.at[4 * t + j], recv_sem=recv_sems.at[4 * t + j],
                    device_id=(x, y, c), device_id_type=MESH).wait_recv()
        for cp in sends:
            cp.wait_send()

    return _pcall(
        body, name="gather_taps", in_specs=[HBM_SPEC] * n, out_specs=[HBM_SPEC] * n,
        out_shape=[jax.ShapeDtypeStruct((N_CHIPS,) + a.shape, a.dtype) for a in shards],
        scratch_shapes=[pltpu.SemaphoreType.DMA((4 * n,)), pltpu.SemaphoreType.DMA((4 * n,))],
    )(*shards)


SEM_SPEC = pl.BlockSpec(memory_space=pltpu.SEMAPHORE)
DATAFLOW = pltpu.SideEffectType.DATAFLOW_SIDE_EFFECTING
COPIES_PER_SHARD = 4


def _shard_copies(ins, lands, send_sems, recv_sems, base=0):
    x, y, c, chips = _place()
    me = 2 * x + y
    cps = []
    for t in range(len(ins)):
        rh = ins[t].shape[0] // 2
        half = pl.ds(pl.multiple_of(c * rh, rh), rh)
        for j, chip in enumerate(chips):
            k = COPIES_PER_SHARD * (base + t) + j
            cps.append(pltpu.make_async_remote_copy(
                src_ref=ins[t].at[half], dst_ref=lands[t].at[me, half], send_sem=send_sems.at[k],
                recv_sem=recv_sems.at[k], device_id=(*chip, c), device_id_type=MESH))
        k = COPIES_PER_SHARD * (base + t) + 3
        cps.append(pltpu.make_async_remote_copy(
            src_ref=ins[t], dst_ref=lands[t].at[me], send_sem=send_sems.at[k], recv_sem=recv_sems.at[k],
            device_id=(x, y, 1 - c), device_id_type=MESH))
    return cps


def _gather_start(shards, thru, name):
    n = len(shards)
    nops = 2 * n + len(thru)

    def body(*refs):
        ins, lands = refs[:n], refs[n:2 * n]
        send_sems, recv_sems = refs[nops], refs[nops + 1]
        for cp in _shard_copies(ins, lands, send_sems, recv_sems):
            cp.start()

    lands = [pltpu.with_memory_space_constraint(lax.empty((N_CHIPS,) + s.shape, s.dtype), pltpu.HBM) for s in shards]
    ops = [pltpu.with_memory_space_constraint(s, pltpu.HBM) for s in shards] + lands + list(thru)
    nsem = COPIES_PER_SHARD * n
    res = _pcall(
        body, name=name, in_specs=[HBM_SPEC] * nops,
        out_specs=[SEM_SPEC, SEM_SPEC] + [HBM_SPEC] * nops,
        out_shape=[pltpu.SemaphoreType.DMA((nsem,)), pltpu.SemaphoreType.DMA((nsem,))]
        + [pltpu.HBM(o.shape, o.dtype) for o in ops],
        input_output_aliases={i: 2 + i for i in range(nops)},
        compiler_params=pltpu.CompilerParams(has_side_effects=DATAFLOW),
    )(*ops)
    return res[0], res[1], res[2:2 + n], res[2 + n:2 + 2 * n], list(res[2 + 2 * n:])


def _gather_wait(send_sems, recv_sems, shards, lands, after, name, base=0, which=None):
    n = len(shards)

    def body(*refs):
        ins, lnd = refs[:n], refs[n:2 * n]
        ssem, rsem = refs[2 * n], refs[2 * n + 1]
        for k, cp in enumerate(_shard_copies(ins, lnd, ssem, rsem, base)):
            if which is None or k % COPIES_PER_SHARD in which:
                cp.wait_send()
                cp.wait_recv()

    res = _pcall(
        body, name=name,
        in_specs=[HBM_SPEC] * (2 * n) + [SEM_SPEC, SEM_SPEC, pl.BlockSpec(memory_space=pl.ANY)],
        out_specs=[HBM_SPEC] * (2 * n),
        out_shape=[pltpu.HBM(o.shape, o.dtype) for o in list(shards) + list(lands)],
        input_output_aliases={i: i for i in range(2 * n)},
        compiler_params=pltpu.CompilerParams(has_side_effects=DATAFLOW),
    )(*shards, *lands, send_sems, recv_sems, after)
    return res[:n], res[n:]


SIBLING_PAIR = 1


def _sibling_handshake():
    x, y, c, _ = _place()
    barrier = pltpu.get_barrier_semaphore()
    pl.semaphore_signal(barrier, inc=1, device_id=(x, y, 1 - c), device_id_type=MESH)
    pl.semaphore_wait(barrier, 1)


def _forward_copies(land, send_sems, recv_sems, which=(0, 1, 2)):
    x, y, c, chips = _place()
    rh = land.shape[1] // 2
    return [pltpu.make_async_remote_copy(
        src_ref=land.at[2 * cx + cy, pl.ds(pl.multiple_of(c * rh, rh), rh)],
        dst_ref=land.at[2 * cx + cy, pl.ds(pl.multiple_of(c * rh, rh), rh)],
        send_sem=send_sems.at[j], recv_sem=recv_sems.at[j], device_id=(x, y, 1 - c), device_id_type=MESH)
        for j, (cx, cy) in enumerate(chips) if j in which]


def _forward_start(land, thru, name, which=(0, 1, 2)):
    def body(land_ref, thru_ref, send_sems, recv_sems, out_ref, thru_out):
        _sibling_handshake()
        for cp in _forward_copies(land_ref, send_sems, recv_sems, which):
            cp.start()

    return _pcall(
        body, name=name, in_specs=[HBM_SPEC, HBM_SPEC], out_specs=[SEM_SPEC, SEM_SPEC, HBM_SPEC, HBM_SPEC],
        out_shape=[pltpu.SemaphoreType.DMA((3,)), pltpu.SemaphoreType.DMA((3,)), pltpu.HBM(land.shape, land.dtype),
                   pltpu.HBM(thru.shape, thru.dtype)],
        input_output_aliases={0: 2, 1: 3},
        compiler_params=pltpu.CompilerParams(has_side_effects=DATAFLOW, collective_id=SIBLING_PAIR),
    )(land, thru)


def _forward_wait(send_sems, recv_sems, land, after, name, which=(0, 1, 2)):
    def body(land_ref, ssem, rsem, after_ref, out_ref):
        for cp in _forward_copies(land_ref, ssem, rsem, which):
            cp.wait_send()
            cp.wait_recv()

    return _pcall(
        body, name=name, in_specs=[HBM_SPEC, SEM_SPEC, SEM_SPEC, pl.BlockSpec(memory_space=pl.ANY)],
        out_specs=HBM_SPEC, out_shape=pltpu.HBM(land.shape, land.dtype), input_output_aliases={0: 0},
        compiler_params=pltpu.CompilerParams(has_side_effects=DATAFLOW),
    )(land, send_sems, recv_sems, after)


def _sibling_copies(ins, lands, send_sems, recv_sems, other_half):
    x, y, c, _ = _place()
    return [pltpu.make_async_remote_copy(
        src_ref=ins[t].at[:, 1 - c] if other_half else ins[t], dst_ref=lands[t], send_sem=send_sems.at[t],
        recv_sem=recv_sems.at[t], device_id=(x, y, 1 - c), device_id_type=MESH) for t in range(len(ins))]


def _sibling_start(srcs, other_half, thru, name):
    n = len(srcs)
    nthru = 0 if thru is None else 1

    def body(*refs):
        ins, lands = refs[:n], refs[n:2 * n]
        send_sems, recv_sems = refs[2 * n + nthru], refs[2 * n + nthru + 1]
        _sibling_handshake()
        for cp in _sibling_copies(ins, lands, send_sems, recv_sems, other_half):
            cp.start()

    shapes = [(s.shape[0],) + s.shape[2:] if other_half else s.shape for s in srcs]
    lands = [pltpu.with_memory_space_constraint(lax.empty(sh, s.dtype), pltpu.HBM) for sh, s in zip(shapes, srcs)]
    ops = [pltpu.with_memory_space_constraint(s, pltpu.HBM) for s in srcs] + lands + ([] if thru is None else [thru])
    res = _pcall(
        body, name=name, in_specs=[HBM_SPEC] * len(ops),
        out_specs=[SEM_SPEC, SEM_SPEC] + [HBM_SPEC] * len(ops),
        out_shape=[pltpu.SemaphoreType.DMA((n,)), pltpu.SemaphoreType.DMA((n,))]
        + [pltpu.HBM(o.shape, o.dtype) for o in ops],
        input_output_aliases={i: 2 + i for i in range(len(ops))},
        compiler_params=pltpu.CompilerParams(has_side_effects=DATAFLOW, collective_id=SIBLING_PAIR),
    )(*ops)
    return res[0], res[1], res[2:2 + n], res[2 + n:2 + 2 * n], (None if thru is None else res[2 + 2 * n])


def _sibling_wait(send_sems, recv_sems, srcs, lands, other_half, after, name):
    n = len(srcs)

    def body(*refs):
        ins, lnd = refs[:n], refs[n:2 * n]
        ssem, rsem = refs[2 * n], refs[2 * n + 1]
        for cp in _sibling_copies(ins, lnd, ssem, rsem, other_half):
            cp.wait_send()
            cp.wait_recv()

    res = _pcall(
        body, name=name,
        in_specs=[HBM_SPEC] * (2 * n) + [SEM_SPEC, SEM_SPEC, pl.BlockSpec(memory_space=pl.ANY)],
        out_specs=[HBM_SPEC] * (2 * n),
        out_shape=[pltpu.HBM(o.shape, o.dtype) for o in list(srcs) + list(lands)],
        input_output_aliases={i: i for i in range(2 * n)},
        compiler_params=pltpu.CompilerParams(has_side_effects=DATAFLOW),
    )(*srcs, *lands, send_sems, recv_sems, after)
    return res[:n], res[n:]


def _chip_copies(ins, lands, send_sems, recv_sems):
    x, y, c, chips = _place()
    cps = []
    for t in range(len(ins)):
        for j, (cx, cy) in enumerate(chips):
            cps.append(pltpu.make_async_remote_copy(
                src_ref=ins[t].at[2 * cx + cy], dst_ref=lands[t].at[j],
                send_sem=send_sems.at[3 * t + j], recv_sem=recv_sems.at[3 * t + j],
                device_id=(cx, cy, c), device_id_type=MESH))
    return cps


def _chip_start(parts, thru, name):
    n = len(parts)

    def body(*refs):
        ins, lands = refs[:n], refs[n:2 * n]
        send_sems, recv_sems = refs[2 * n + 1], refs[2 * n + 2]
        for cp in _chip_copies(ins, lands, send_sems, recv_sems):
            cp.start()

    lands = [pltpu.with_memory_space_constraint(lax.empty((3,) + p.shape[1:], p.dtype), pltpu.HBM) for p in parts]
    ops = [pltpu.with_memory_space_constraint(p, pltpu.HBM) for p in parts] + lands + [thru]
    res = _pcall(
        body, name=name, in_specs=[HBM_SPEC] * (2 * n + 1),
        out_specs=[SEM_SPEC, SEM_SPEC] + [HBM_SPEC] * (2 * n + 1),
        out_shape=[pltpu.SemaphoreType.DMA((3 * n,)), pltpu.SemaphoreType.DMA((3 * n,))]
        + [pltpu.HBM(o.shape, o.dtype) for o in ops],
        input_output_aliases={i: 2 + i for i in range(2 * n + 1)},
        compiler_params=pltpu.CompilerParams(has_side_effects=DATAFLOW),
    )(*ops)
    return res[0], res[1], res[2:2 + n], res[2 + n:2 + 2 * n], res[2 + 2 * n]


def _chip_wait(send_sems, recv_sems, parts, lands, after, name):
    n = len(parts)

    def body(*refs):
        ins, lnd = refs[:n], refs[n:2 * n]
        ssem, rsem = refs[2 * n], refs[2 * n + 1]
        for cp in _chip_copies(ins, lnd, ssem, rsem):
            cp.wait_send()
            cp.wait_recv()

    res = _pcall(
        body, name=name,
        in_specs=[HBM_SPEC] * (2 * n) + [SEM_SPEC, SEM_SPEC, pl.BlockSpec(memory_space=pl.ANY)],
        out_specs=[HBM_SPEC] * (2 * n),
        out_shape=[pltpu.HBM(o.shape, o.dtype) for o in list(parts) + list(lands)],
        input_output_aliases={i: i for i in range(2 * n)},
        compiler_params=pltpu.CompilerParams(has_side_effects=DATAFLOW),
    )(*parts, *lands, send_sems, recv_sems, after)
    return res[:n], res[n:]


def _add_pair(grad, recv, c, name):
    s, _, rh, cc = grad.shape
    tr = _pick(rh, (256, 128, 64, 32, 16))

    def body(c_ref, g_ref, r_ref, o_ref):
        o_ref[...] = (g_ref[...].astype(F32) + r_ref[...].astype(F32)).astype(o_ref.dtype)

    return _pcall(
        body, name=name,
        grid_spec=pltpu.PrefetchScalarGridSpec(
            num_scalar_prefetch=1, grid=(s, rh // tr),
            in_specs=[pl.BlockSpec((None, None, tr, cc), lambda a, i, cr: (a, cr[0], i, 0)),
                      pl.BlockSpec((None, tr, cc), lambda a, i, cr: (a, i, 0))],
            out_specs=pl.BlockSpec((None, tr, cc), lambda a, i, cr: (a, i, 0))),
        out_shape=jax.ShapeDtypeStruct((s, rh, cc), BF16),
        compiler_params=_params(("parallel", "parallel")),
    )(c, grad, recv)


def _add_chips(part, recv, me, name):
    _, rh, cc = part.shape
    tr = _pick(rh, (256, 128, 64, 32, 16))

    def body(m_ref, p_ref, r_ref, o_ref):
        o_ref[...] = ((p_ref[...].astype(F32) + r_ref[0].astype(F32)) + r_ref[1].astype(F32)) + r_ref[2].astype(F32)

    return _pcall(
        body, name=name,
        grid_spec=pltpu.PrefetchScalarGridSpec(
            num_scalar_prefetch=1, grid=(rh // tr,),
            in_specs=[pl.BlockSpec((None, tr, cc), lambda i, mr: (mr[0], i, 0)),
                      pl.BlockSpec((3, tr, cc), lambda i, mr: (0, i, 0))],
            out_specs=pl.BlockSpec((tr, cc), lambda i, mr: (i, 0))),
        out_shape=jax.ShapeDtypeStruct((rh, cc), F32),
        compiler_params=_params(("parallel",)),
    )(me, part, recv)


def _all_sum(vec):
    rows = vec.shape[0]

    def body(v_ref, o_ref, buf, send_sems, recv_sems):
        x, y, c, _ = _place()
        me = 4 * x + 2 * y + c
        buf[me] = v_ref[...]
        cps = []
        for r in range(1, 8):
            fx, fy, fc = (r >> 2) & 1, (r >> 1) & 1, r & 1
            peer = (x ^ fx, y ^ fy, c ^ fc)
            cps.append(pltpu.make_async_remote_copy(
                src_ref=v_ref, dst_ref=buf.at[me], send_sem=send_sems.at[r - 1], recv_sem=recv_sems.at[r - 1],
                device_id=peer, device_id_type=MESH))
        for cp in cps:
            cp.start()
        for r in range(1, 8):
            src = me ^ r
            pltpu.make_async_remote_copy(
                src_ref=v_ref, dst_ref=buf.at[src], send_sem=send_sems.at[r - 1], recv_sem=recv_sems.at[r - 1],
                device_id=(x, y, c), device_id_type=MESH).wait_recv()
        for cp in cps:
            cp.wait_send()
        acc = buf[0]
        for d in range(1, 8):
            acc = acc + buf[d]
        o_ref[...] = acc

    return _pcall(
        body, name="all_sum_small",
        in_specs=[pl.BlockSpec(memory_space=pltpu.VMEM)], out_specs=pl.BlockSpec(memory_space=pltpu.VMEM),
        out_shape=jax.ShapeDtypeStruct((rows, LANES), F32),
        scratch_shapes=[pltpu.VMEM((8, rows, LANES), F32), pltpu.SemaphoreType.DMA((7,)), pltpu.SemaphoreType.DMA((7,))],
    )(vec)


def _pack(parts):
    flat = jnp.concatenate([p.reshape(-1) for p in parts])
    tile = SUBLANES * LANES
    pad = (-flat.shape[0]) % tile
    return jnp.pad(flat, (0, pad)).reshape(-1, LANES)


def _unpack(vec, shapes):
    flat = vec.reshape(-1)
    out, p = [], 0
    for s in shapes:
        n = 1
        for d in s:
            n *= d
        out.append(flat[p:p + n].reshape(s))
        p += n
    return out


def _local_step(x, tgt, norm_mix, norm_ffn, lb8, out_norm, final_norm, sc_conv, ffn_conv, first, arrive, reduce_start,
                reduce_finish):
    T, D = x.shape
    F2 = ffn_conv.shape[-1]
    FF = F2 // 2
    tm = _pick(T, (1024, 512, 256, 128))
    wide = (1536, 1408, 1024, 768, 512, 384, 256, 128)
    cw_h, cw_s, cw_u = 4 * D // N_CHIPS, 3 * D // N_CHIPS, F2 // N_CHIPS
    kp = FF // N_CHIPS
    tk_ff = kp if kp % LANES == 0 else LANES
    tn_d = _pick(D, (1024, 512, 256, 128))
    tk_w = _pick(D, (512, 256, 128))
    tn_h = _pick(cw_h, (1024, 512, 256, 128))
    tn_s = _pick(D // N_CHIPS, (512, 256, 128))
    tn_u = _pick(cw_u, wide)
    lb = lb8[0:1]
    wm_sq = _wmap_col(D, tn_d, 0)
    wm_sq1 = _wmap_col(D, D, 0)
    seg1 = lambda a: a.reshape((1,) + a.shape)

    def mix_in(h, w):
        return _row_call(_rms_fwd_fn, [(h, 0, D)], [w], [(D, BF16)], 0, "rms_fwd")[0]

    def rms_bwd(h, dxn, dh, w):
        return _row_call(_rms_bwd_fn, [(h, 0, D), (dxn, 0, D), (dh, 0, D)], [w], [(D, F32), (D, BF16)], 1, "rms_bwd")

    def ffn_fwd(h, i, fetch_up, behind_down=()):
        xn = mix_in(h, norm_ffn[i:i + 1])
        tn = _pick(cw_u, wide)
        fetch_down, xn = arrive("ffn_w_down%d" % i, xn)
        w_up = fetch_up(xn)
        up = _mm_nn(xn, w_up, _wmap_col(cw_u, tn, 0), D, F2, tm, D, tn, "ffn_up")
        nb = FF // LANES
        a = _col_call(_glu_fwd_fn, [(up, 0), (up, nb)], [(ffn_conv[i], 0), (ffn_conv[i], nb)], [(1, FF, BF16)], 0,
                      "glu_fwd", before=True, after=False)[0][0]
        later = []
        for name in behind_down:
            fetch, a = arrive(name, a)
            later.append(fetch)
        w_down = fetch_down(a)
        h2 = _mm_nn(a, w_down, _wmap_row(kp, tk_ff, 0), FF, D, tm, tk_ff, tn_d, "ffn_down", res=h, per_step=2)
        return h2, (xn, up, a), w_up, w_down, later

    def ffn_bwd(dh, dh16, h, saved, i, w_up, w_down):
        xn, up, a = saved
        g_down = _mm_tn(a, seg1(dh16), (N_CHIPS, 1, kp, D), _wmap_row(kp, tk_ff, 0), FF, D, tk_ff, tn_d,
                        "ffn_down_dw", tm=_pick(T, (2048, 1024, 512, 256, 128)))
        dh16 = reduce_start(("ffn_w_down", i), g_down, dh16)
        da = _mm_nt(seg1(dh16), w_down, _wmap_row(kp, tk_ff, 0), FF, D, tm, tk_ff, D, "ffn_down_dx")
        nb = FF // LANES
        dgv, cg, cv = _col_call(_glu_bwd_fn, [(up, 0), (up, nb), (da, 0)], [(ffn_conv[i], 0), (ffn_conv[i], nb)],
                                [(2, FF, BF16)], 2, "glu_bwd", before=True, after=True)
        g_up = _mm_tn(xn, dgv, (N_CHIPS, 1, D, cw_u), _wmap_col(cw_u, tn_u, 0), D, F2, tk_w, tn_u, "ffn_up_dw")
        dgv = reduce_start(("ffn_w_up", i), g_up, dgv)
        dxn = _mm_nt(dgv, w_up, _wmap_col(cw_u, tn_u, 0), D, F2, _pick(T, (512, 256, 128)), D, tn_u, "ffn_up_dx",
                     per_step=2)
        dh2, dh2_16, dnw = rms_bwd(h, dxn, dh, norm_ffn[i:i + 1])
        return dh2, reduce_finish(dh2_16), dnw, jnp.concatenate([cg[:3], cv[:3]], axis=1)

    h0 = x
    xn0 = mix_in(h0, norm_mix[0:1])
    proj, w_hin = first(xn0, lambda w, s, prev: _mm_nn_shard(xn0, w, s, tm, tn_h, "hgrn_in", prev))
    o, states = _hgrn_fwd(proj, lb, D)
    fetch_hout, o = arrive("hgrn_w_out", o)
    on = _row_call(_onorm_fwd_fn, [(o, 0, D), (proj, 3, D)], [out_norm], [(D, BF16)], 0, "onorm_fwd")[0]
    fetch_up0, on = arrive("ffn_w_up0", on)
    w_hout1 = fetch_hout(on).reshape(1, D, D)
    h1 = _mm_nn(on, w_hout1, wm_sq, D, D, tm, D, tn_d, "hgrn_out", res=h0)
    h2, ffn0, w_up0, w_down0, (fetch_sin, fetch_sout) = ffn_fwd(h1, 0, fetch_up0, ("sc_w_in", "sc_w_out"))
    xn1 = mix_in(h2, norm_mix[1:2])
    w_sin = fetch_sin(xn1)
    tn_si = _pick(cw_s, wide)
    sproj = _mm_nn(xn1, w_sin, _wmap_col(cw_s, tn_si, 0), D, 3 * D, tm, D, tn_si, "sc_in")
    fetch_up1, sproj = arrive("ffn_w_up1", sproj)
    nd = D // LANES
    ysc = _col_call(_sc_fwd_fn, [(sproj, 0), (sproj, nd), (sproj, 2 * nd)], [(sc_conv, 0)], [(1, D, BF16)], 0,
                    "sc_fwd", before=True, after=False)[0][0]
    w_sout1 = fetch_sout(ysc).reshape(1, D, D)
    h3 = _mm_nn(ysc, w_sout1, wm_sq, D, D, tm, D, tn_d, "sc_out", res=h2)
    h4, ffn1, w_up1, w_down1, _ = ffn_fwd(h3, 1, fetch_up1)

    dh, dh16, esq, dfinal = _row_call(_final_fn, [(h4, 0, D), (tgt, 0, D)], [final_norm], [(D, F32), (D, BF16)], 2,
                                      "final_loss")
    loss = 0.5 / D * jnp.sum(esq)
    dh, dh16, dnf1, dconv1 = ffn_bwd(dh, dh16, h3, ffn1, 1, w_up1, w_down1)
    g_sout = _mm_tn(ysc, seg1(dh16), (1, 1, D, D), wm_sq, D, D, tk_w, tn_d, "sc_out_dw")
    dh16 = reduce_start(("sc_w_out", 0), g_sout, dh16)
    dy = _mm_nt(seg1(dh16), w_sout1, wm_sq1, D, D, tm, D, D, "sc_out_dx")
    dsp, dscc = _col_call(_sc_bwd_fn, [(sproj, 0), (sproj, nd), (sproj, 2 * nd), (dy, 0)], [(sc_conv, 0)],
                          [(3, D, BF16)], 1, "sc_bwd", before=True, after=True)
    g_sin = _mm_tn(xn1, dsp, (N_CHIPS, 1, D, cw_s), _wmap_col(cw_s, tn_s, 0), D, 3 * D, tk_w, tn_s, "sc_in_dw")
    dsp = reduce_start(("sc_w_in", 0), g_sin, dsp)
    dxn = _mm_nt(dsp, w_sin, _wmap_col(cw_s, tn_s, 0), D, 3 * D, tm, D, tn_s, "sc_in_dx", per_step=3)
    dh, dh16, dnm1 = rms_bwd(h2, dxn, dh, norm_mix[1:2])
    dh16 = reduce_finish(dh16)
    dh, dh16, dnf0, dconv0 = ffn_bwd(dh, dh16, h1, ffn0, 0, w_up0, w_down0)
    g_hout = _mm_tn(on, seg1(dh16), (1, 1, D, D), wm_sq, D, D, tk_w, tn_d, "hgrn_out_dw")
    dh16 = reduce_start(("hgrn_w_out", 0), g_hout, dh16)
    don = _mm_nt(seg1(dh16), w_hout1, wm_sq1, D, D, tm, D, D, "hgrn_out_dx")
    do, dgate, dgain = _row_call(_onorm_bwd_fn, [(o, 0, D), (proj, 3, D), (don, 0, D)], [out_norm],
                                 [(D, F32), (D, BF16)], 1, "onorm_bwd")
    dproj, dlb = _hgrn_bwd(proj, lb, states, do, dgate, D)
    g_hin = _mm_tn(xn0, dproj, (N_CHIPS, 1, D, cw_h), _wmap_col(cw_h, tn_h, 0), D, 4 * D, tk_w, tn_h, "hgrn_in_dw")
    dproj = reduce_start(("hgrn_w_in", 0), g_hin, dproj)
    dxn = _mm_nt(dproj, w_hin, _wmap_col(cw_h, tn_h, 0), D, 4 * D, tm, D, tn_h, "hgrn_in_dx", per_step=2)
    grad_x, _, dnm0 = rms_bwd(h0, dxn, dh, norm_mix[0:1])

    small = dict(
        loss=loss,
        norm_mix=jnp.stack([jnp.sum(dnm0, axis=0), jnp.sum(dnm1, axis=0)]),
        norm_ffn=jnp.stack([jnp.sum(dnf0, axis=0), jnp.sum(dnf1, axis=0)]),
        lb=dlb[0:1],
        out_norm=jnp.sum(dgain, axis=0)[None],
        final_norm=jnp.sum(dfinal, axis=0),
        sc_conv=dscc[:3],
        ffn_conv=jnp.stack([dconv0, dconv1]),
    )
    return grad_x, small


def kernel(x, norm_mix, norm_ffn, hgrn_w_in, hgrn_lb_table, hgrn_out_norm, hgrn_w_out, sc_w_in, sc_conv, sc_w_out, ffn_w_up, ffn_conv, ffn_w_down, final_norm, loss_target, m_norm_mix, m_norm_ffn, m_hgrn_w_in, m_hgrn_lb_table, m_hgrn_out_norm, m_hgrn_w_out, m_sc_w_in, m_sc_conv, m_sc_w_out, m_ffn_w_up, m_ffn_conv, m_ffn_w_down, m_final_norm, v_norm_mix, v_norm_ffn, v_hgrn_w_in, v_hgrn_lb_table, v_hgrn_out_norm, v_hgrn_w_out, v_sc_w_in, v_sc_conv, v_sc_w_out, v_ffn_w_up, v_ffn_conv, v_ffn_w_down, v_final_norm):
    D = x.shape[-1]
    xi, yi, ci = lax.axis_index("x"), lax.axis_index("y"), lax.axis_index("c")
    me_chip = (2 * xi + yi).astype(jnp.int32).reshape(1)
    me_core = ci.astype(jnp.int32).reshape(1)

    big_names = ["hgrn_w_in", "hgrn_w_out", "sc_w_in", "sc_w_out", "ffn_w_up", "ffn_w_down"]
    big_w = dict(hgrn_w_in=hgrn_w_in, hgrn_w_out=hgrn_w_out, sc_w_in=sc_w_in, sc_w_out=sc_w_out,
                 ffn_w_up=ffn_w_up, ffn_w_down=ffn_w_down)
    big_m = dict(hgrn_w_in=m_hgrn_w_in, hgrn_w_out=m_hgrn_w_out, sc_w_in=m_sc_w_in, sc_w_out=m_sc_w_out,
                 ffn_w_up=m_ffn_w_up, ffn_w_down=m_ffn_w_down)
    big_v = dict(hgrn_w_in=v_hgrn_w_in, hgrn_w_out=v_hgrn_w_out, sc_w_in=v_sc_w_in, sc_w_out=v_sc_w_out,
                 ffn_w_up=v_ffn_w_up, ffn_w_down=v_ffn_w_down)
    flat2 = lambda a: a.reshape(-1, a.shape[-1])

    sh = lambda a, layer=0: _to_bf16(flat2(a), layer, a.shape[0])
    in_order_of_use = [("hgrn_w_in", sh(hgrn_w_in)), ("hgrn_w_out", sh(hgrn_w_out)), ("ffn_w_up0", sh(ffn_w_up, 0)),
                       ("ffn_w_down0", sh(ffn_w_down, 0)), ("sc_w_in", sh(sc_w_in)), ("sc_w_out", sh(sc_w_out)),
                       ("ffn_w_up1", sh(ffn_w_up, 1)), ("ffn_w_down1", sh(ffn_w_down, 1))]
    scc4, fcc4 = _gather_taps([flat2(sc_conv), flat2(ffn_conv)])
    names = [n for n, _ in in_order_of_use]
    ss, rs, src, land, (scc4, norm_mix) = _gather_start([s for _, s in in_order_of_use], [scc4, norm_mix], "gather_start")
    travelling = {n: (s, l) for n, s, l in zip(names, src, land)}

    def landed(name, after, call, which=None):
        (s,), (l,) = _gather_wait(ss, rs, [travelling[name][0]], [travelling[name][1]], after, call,
                                  base=names.index(name), which=which)
        travelling[name] = (s, l)
        return l

    scc = jnp.moveaxis(scc4, 0, 1).reshape(3, D)
    f2 = ffn_conv.shape[-1] * N_CHIPS
    fcc = jnp.moveaxis(fcc4.reshape(N_CHIPS, 2, 3, -1), 0, 2).reshape(2, 3, f2)

    def first(after, matmul):
        w = landed("hgrn_w_in", after, "gather_wait_0_own", which=(3,))
        proj = matmul(w, me_chip, None)
        others = [2 * (1 - xi) + yi, 2 * xi + (1 - yi), 2 * (1 - xi) + (1 - yi)]
        for j, s in enumerate(others):
            w = landed("hgrn_w_in", proj, "gather_wait_0_%d" % j, which=(j,))
            fs, fr, w, proj = _forward_start(w, proj, "gather_forward_start_0_%d" % j, which=(j,))
            w = _forward_wait(fs, fr, w, proj, "gather_forward_wait_0_%d" % j, which=(j,))
            travelling["hgrn_w_in"] = (travelling["hgrn_w_in"][0], w)
            proj = matmul(w, s.astype(jnp.int32).reshape(1), proj)
        return proj, w

    def arrive(name, after):
        fs, fr, w, after = _forward_start(landed(name, after, "gather_wait_" + name), after, "gather_forward_start_" + name)
        return functools.partial(_forward_wait, fs, fr, w, name="gather_forward_wait_" + name), after

    pending = []
    started = []

    def reduce_start(slot, grad, thru):
        t = sum(len(b[0]) for b in pending) + len(started)
        halves = grad.reshape(N_CHIPS, 2, -1, grad.shape[-1])
        ss, rs, src, land, thru = _sibling_start([halves], True, thru, "grad_pair_start_%d" % t)
        started.append((slot, t, ss, rs, src, land))
        return thru

    def reduce_finish(thru):
        k = len(pending)
        pair = []
        for slot, t, ss, rs, src, land in started:
            src, recv = _sibling_wait(ss, rs, src, land, True, thru, "grad_pair_wait_%d" % t)
            pair.append(_add_pair(src[0], recv[0], me_core, "grad_add_pair"))
        ss, rs, pair, land, thru = _chip_start(pair, thru, "grad_chip_start_%d" % k)
        pending.append(([s[0] for s in started], ss, rs, pair, land))
        started.clear()
        return thru

    lb8 = _lb_softmax(hgrn_lb_table)
    grad_x, small = _local_step(
        x[0], loss_target[0], norm_mix, norm_ffn, lb8, hgrn_out_norm, final_norm[None], scc, fcc, first, arrive,
        reduce_start, reduce_finish)

    small_names = ["loss", "norm_mix", "norm_ffn", "lb", "out_norm", "final_norm", "sc_conv", "ffn_conv"]
    parts = [small[n].astype(F32) for n in small_names]
    shapes = [p.shape for p in parts]
    tot = dict(zip(small_names, _unpack(reduce_finish(_all_sum(_pack(parts))), shapes)))
    loss = tot["loss"].reshape(())
    g_lb_table = _lb_table_grad(lb8, tot["lb"], hgrn_lb_table.shape[0])
    cw = sc_conv.shape[-1]
    g_sc_conv = lax.dynamic_slice_in_dim(tot["sc_conv"], me_chip[0] * cw, cw, axis=1)[None]
    cf = ffn_conv.shape[-1]
    g_ffn_conv = lax.dynamic_slice_in_dim(tot["ffn_conv"], me_chip[0] * cf, cf, axis=2)
    g_small = dict(norm_mix=tot["norm_mix"], norm_ffn=tot["norm_ffn"], hgrn_lb_table=g_lb_table,
                   hgrn_out_norm=tot["out_norm"], sc_conv=g_sc_conv, ffn_conv=g_ffn_conv, final_norm=tot["final_norm"])
    w_small = dict(norm_mix=norm_mix, norm_ffn=norm_ffn, hgrn_lb_table=hgrn_lb_table, hgrn_out_norm=hgrn_out_norm,
                   sc_conv=sc_conv, ffn_conv=ffn_conv, final_norm=final_norm)
    m_small = dict(norm_mix=m_norm_mix, norm_ffn=m_norm_ffn, hgrn_lb_table=m_hgrn_lb_table, hgrn_out_norm=m_hgrn_out_norm,
                   sc_conv=m_sc_conv, ffn_conv=m_ffn_conv, final_norm=m_final_norm)
    v_small = dict(norm_mix=v_norm_mix, norm_ffn=v_norm_ffn, hgrn_lb_table=v_hgrn_lb_table, hgrn_out_norm=v_hgrn_out_norm,
                   sc_conv=v_sc_conv, ffn_conv=v_ffn_conv, final_norm=v_final_norm)
    sm_names = list(g_small)
    sm_shapes = [w_small[n].shape for n in sm_names]
    d_s, m_s, v_s = _adamw(_pack([w_small[n] for n in sm_names]), _pack([g_small[n] for n in sm_names]),
                           _pack([m_small[n] for n in sm_names]), _pack([v_small[n] for n in sm_names]), "adamw_small")
    out_g, out_d, out_m, out_v = dict(g_small), {}, {}, {}
    for n, d_, m_, v_ in zip(sm_names, _unpack(d_s, sm_shapes), _unpack(m_s, sm_shapes), _unpack(v_s, sm_shapes)):
        out_d[n], out_m[n], out_v[n] = d_, m_, v_

    done = {}
    after = grad_x

    def add_and_share(k, after):
        slots, ss, rs, pair, land = pending[k]
        pair, recv = _chip_wait(ss, rs, pair, land, after, "grad_chip_wait_%d" % k)
        mine = [_add_chips(p, r, me_chip, "grad_add_chips") for p, r in zip(pair, recv)]
        ss, rs, mine, land, _ = _sibling_start(mine, False, None, "grad_share_start_%d" % k)
        return slots, ss, rs, mine, land

    def update(k, share, after):
        slots, ss, rs, mine, land = share
        mine, theirs = _sibling_wait(ss, rs, mine, land, False, after, "grad_share_wait_%d" % k)
        for (n, layer), gm, gr in zip(slots, mine, theirs):
            done[n] = _adamw_halves(flat2(big_w[n]), flat2(big_m[n]), flat2(big_v[n]), gm, gr, me_core, "adamw_" + n,
                                    layer=layer, prev=done.get(n))
        return done[slots[-1][0]][0]

    shares = []
    for k in range(len(pending) - 1):
        shares.append(add_and_share(k, after))
        after = shares[-1][3][0]
    for k, share in enumerate(shares):
        after = update(k, share, after)
    last = len(pending) - 1
    share = add_and_share(last, after)
    update(last, share, share[3][0])
    for n in big_names:
        out_g[n], out_d[n], out_m[n], out_v[n] = (a.reshape(big_w[n].shape) for a in done[n])

    order = ["norm_mix", "norm_ffn", "hgrn_w_in", "hgrn_lb_table", "hgrn_out_norm", "hgrn_w_out", "sc_w_in", "sc_conv",
             "sc_w_out", "ffn_w_up", "ffn_conv", "ffn_w_down", "final_norm"]
    return (loss, grad_x[None], *[out_g[n] for n in order], *[out_d[n] for n in order],
            *[out_m[n] for n in order], *[out_v[n] for n in order])
```

```python
import functools

import jax
import jax.numpy as jnp
from jax import lax
from jax.experimental import pallas as pl
from jax.experimental.pallas import tpu as pltpu

F32 = jnp.float32
BF16 = jnp.bfloat16
MESH = pl.DeviceIdType.MESH

EPS = 1e-6
CHUNK = 64
HEAD = 128
N_CHIPS = 4
ADAM_LR, ADAM_B1, ADAM_B2, ADAM_EPS, ADAM_WD, ADAM_STEP = 0.001, 0.9, 0.999, 1e-08, 0.01, 10
VMEM_LIMIT = 56 * 1024 * 1024
SUBLANES = 8
LANES = 128


def _pcall(body, **kw):
    return pl.pallas_call(body, **kw)


def _params(sem, vmem=VMEM_LIMIT):
    return pltpu.CompilerParams(dimension_semantics=sem, vmem_limit_bytes=vmem)


def _pick(dim, prefs):
    for p in prefs:
        if p <= dim and dim % p == 0:
            return p
    return dim


def _sigmoid(x):
    return 1.0 / (1.0 + jnp.exp(-x))


def _wmap_col(cw, tn, r0):
    bps = cw // tn
    return lambda kb, nb: (nb // bps, r0 + kb, nb % bps)


def _wmap_row(kp, tk, r0):
    bps = kp // tk
    return lambda kb, nb: (kb // bps, r0 + kb % bps, nb)


def _mm_nn(a, w3, wmap, K, N, tm, tk, tn, name, res=None, per_step=1):
    M = a.shape[0]
    u = per_step
    nk = K // (tk * u)

    def body(*refs):
        r_ref = None if res is None else refs[2 * u]
        o_ref = refs[2 * u + (0 if res is None else 1)]
        p = None
        for r in range(u):
            d = jnp.dot(refs[r][...], refs[u + r][...], preferred_element_type=F32)
            p = d if p is None else p + d
        if nk == 1:
            o_ref[...] = p if res is None else p + r_ref[...]
            return
        acc = refs[-1]
        k = pl.program_id(2)

        @pl.when(k == 0)
        def _():
            acc[...] = p

        @pl.when(k > 0)
        def _():
            acc[...] += p

        @pl.when(k == nk - 1)
        def _():
            o_ref[...] = acc[...] if res is None else acc[...] + r_ref[...]

    if nk == 1:
        grid = (M // tm, N // tn)
        ix = lambda f: (lambda i, j: f(i, j, 0))
        sem = ("parallel", "parallel")
        scratch = []
    else:
        grid = (M // tm, N // tn, nk)
        ix = lambda f: f
        sem = ("parallel", "parallel", "arbitrary")
        scratch = [pltpu.VMEM((tm, tn), F32)]
    def a_spec(r):
        return pl.BlockSpec((tm, tk), ix(lambda i, j, k: (i, k * u + r)))

    def w_spec(r):
        return pl.BlockSpec((None, tk, tn), ix(lambda i, j, k: wmap(k * u + r, j)))

    in_specs = [a_spec(r) for r in range(u)] + [w_spec(r) for r in range(u)]
    args = [a] * u + [w3] * u
    if res is not None:
        in_specs.append(pl.BlockSpec((tm, tn), ix(lambda i, j, k: (i, j))))
        args.append(res)
    return _pcall(
        body, name=name, grid=grid, in_specs=in_specs,
        out_specs=pl.BlockSpec((tm, tn), ix(lambda i, j, k: (i, j))),
        out_shape=jax.ShapeDtypeStruct((M, N), F32), scratch_shapes=scratch, compiler_params=_params(sem),
    )(*args)


def _mm_nn_shard(a, w3, s, tm, tn, name, prev=None):
    M, K = a.shape
    S, _, cw = w3.shape
    bps = cw // tn

    def body(s_ref, a_ref, w_ref, *rest):
        o_ref = rest[-1]
        o_ref[...] = jnp.dot(a_ref[...], w_ref[...], preferred_element_type=F32)

    in_specs = [pl.BlockSpec((tm, K), lambda i, j, sr: (i, 0)),
                pl.BlockSpec((None, K, tn), lambda i, j, sr: (sr[0], 0, j))]
    args = [s, a, w3]
    alias = {}
    if prev is not None:
        in_specs.append(pl.BlockSpec(memory_space=pl.ANY))
        args.append(prev)
        alias = {3: 0}
    return _pcall(
        body, name=name,
        grid_spec=pltpu.PrefetchScalarGridSpec(
            num_scalar_prefetch=1, grid=(M // tm, bps), in_specs=in_specs,
            out_specs=pl.BlockSpec((tm, tn), lambda i, j, sr: (i, sr[0] * bps + j))),
        out_shape=jax.ShapeDtypeStruct((M, S * cw), F32), input_output_aliases=alias,
        compiler_params=_params(("parallel", "parallel")),
    )(*args)


def _mm_nt(dy3, w3, wmap, K, N, tm, tk, tn, name, per_step=1):
    M = dy3.shape[1]
    bps = dy3.shape[2] // tn
    u = per_step
    grid = (M // tm, K // tk, N // (tn * u))
    nn = grid[2]

    def body(*refs):
        o_ref = refs[-1]
        p = None
        for r in range(u):
            d = lax.dot_general(refs[r][...], refs[u + r][...], (((1,), (1,)), ((), ())), preferred_element_type=F32)
            p = d if p is None else p + d
        if nn == 1:
            o_ref[...] = p
            return
        n = pl.program_id(2)

        @pl.when(n == 0)
        def _():
            o_ref[...] = p

        @pl.when(n > 0)
        def _():
            o_ref[...] += p

    def dy_spec(r):
        return pl.BlockSpec((None, tm, tn), lambda i, j, n: ((n * u + r) // bps, i, (n * u + r) % bps))

    def w_spec(r):
        return pl.BlockSpec((None, tk, tn), lambda i, j, n: wmap(j, n * u + r))

    return _pcall(
        body, name=name, grid=grid,
        in_specs=[dy_spec(r) for r in range(u)] + [w_spec(r) for r in range(u)],
        out_specs=pl.BlockSpec((tm, tk), lambda i, j, n: (i, j)),
        out_shape=jax.ShapeDtypeStruct((M, K), F32),
        compiler_params=_params(("parallel", "parallel", "arbitrary")),
    )(*([dy3] * u), *([w3] * u))


def _mm_tn(x, dy3, shape4, wmap, K, N, tk, tn, name, tm=None):
    M = x.shape[0]
    tm = M if tm is None else tm
    nm = M // tm
    bps = dy3.shape[2] // tn

    def body(*refs):
        x_ref, dy_ref = refs[:2]
        p = lax.dot_general(x_ref[...], dy_ref[...], (((0,), (0,)), ((), ())), preferred_element_type=F32)
        if nm == 1:
            o_ref = refs[-1]
            o_ref[...] = p.astype(o_ref.dtype)
            return
        o_ref, acc = refs[-2:]
        m = pl.program_id(2)

        @pl.when(m == 0)
        def _():
            acc[...] = p

        @pl.when(m > 0)
        def _():
            acc[...] += p

        @pl.when(m == nm - 1)
        def _():
            o_ref[...] = acc[...].astype(o_ref.dtype)

    def omap(i, j, m):
        s, rb, cb = wmap(i, j)
        return (s, 0, rb, cb)

    return _pcall(
        body, name=name, grid=(K // tk, N // tn, nm),
        in_specs=[pl.BlockSpec((tm, tk), lambda i, j, m: (m, i)),
                  pl.BlockSpec((None, tm, tn), lambda i, j, m: (j // bps, m, j % bps))],
        out_specs=pl.BlockSpec((None, None, tk, tn), omap),
        out_shape=jax.ShapeDtypeStruct(shape4, BF16),
        scratch_shapes=[] if nm == 1 else [pltpu.VMEM((tk, tn), F32)],
        compiler_params=_params(("parallel", "parallel", "arbitrary")),
    )(x, dy3)


def _row_call(fn, rows, vecs, outs, n_acc, name, t_rows=256, sub=16, per_trip=4):
    T = rows[0][0].shape[0]
    t_rows = min(t_rows, T)
    nsub = t_rows // sub
    n_r, n_v, n_o = len(rows), len(vecs), len(outs)
    width = rows[0][2]

    def body(*refs):
        r_refs = refs[:n_r]
        v_refs = refs[n_r:n_r + n_v]
        o_refs = refs[n_r + n_v:n_r + n_v + n_o]
        a_refs = refs[n_r + n_v + n_o:]

        @pl.when(pl.program_id(0) == 0)
        def _():
            for a in a_refs:
                a[...] = jnp.zeros_like(a)

        vv = [v[...] for v in v_refs]

        def step(i, carry):
            done = []
            for u in range(per_trip):
                sl = pl.ds(pl.multiple_of((i * per_trip + u) * sub, sub), sub)
                done.append((sl,) + tuple(fn([r[sl, :] for r in r_refs], vv)))
            for sl, o_vals, a_vals in done:
                for o, val in zip(o_refs, o_vals):
                    o[sl, :] = val.astype(o.dtype)
            for a_i, a in enumerate(a_refs):
                tot = None
                for _, _, a_vals in done:
                    part = a_vals[a_i].reshape(sub // SUBLANES, SUBLANES, a_vals[a_i].shape[-1]).sum(axis=0)
                    tot = part if tot is None else tot + part
                a[...] += tot
            return carry

        lax.fori_loop(0, nsub // per_trip, step, 0)

    in_specs = [pl.BlockSpec((t_rows, w), functools.partial(lambda i, cb: (i, cb), cb=cb)) for _, cb, w in rows]
    in_specs += [pl.BlockSpec(v.shape, lambda i: (0, 0)) for v in vecs]
    out_specs = [pl.BlockSpec((t_rows, w), lambda i: (i, 0)) for w, _ in outs]
    out_specs += [pl.BlockSpec((SUBLANES, width), lambda i: (0, 0)) for _ in range(n_acc)]
    out_shape = [jax.ShapeDtypeStruct((T, w), dt) for w, dt in outs]
    out_shape += [jax.ShapeDtypeStruct((SUBLANES, width), F32) for _ in range(n_acc)]
    return _pcall(
        body, name=name, grid=(T // t_rows,), in_specs=in_specs, out_specs=out_specs, out_shape=out_shape,
        compiler_params=_params(("arbitrary",)),
    )(*[r[0] for r in rows], *vecs)


def _rms_fwd_fn(rv, vv):
    h, = rv
    w, = vv
    r = lax.rsqrt(jnp.mean(h * h, axis=-1, keepdims=True) + EPS)
    return [h * r * w], []


def _rms_bwd_fn(rv, vv):
    h, dxn, dh_in = rv
    w, = vv
    d = h.shape[-1]
    r = lax.rsqrt(jnp.mean(h * h, axis=-1, keepdims=True) + EPS)
    gy = dxn * w
    dh = r * gy - h * ((r * r * r) * (1.0 / d) * jnp.sum(gy * h, axis=-1, keepdims=True))
    return [dh_in + dh] * 2, [dxn * h * r]


def _final_fn(rv, vv):
    h, tgt = rv
    w, = vv
    d = h.shape[-1]
    r = lax.rsqrt(jnp.mean(h * h, axis=-1, keepdims=True) + EPS)
    hn = h * r
    e = hn * w - tgt
    dy = e * (1.0 / d)
    gy = dy * w
    dh = r * gy - h * ((r * r * r) * (1.0 / d) * jnp.sum(gy * h, axis=-1, keepdims=True))
    return [dh] * 2, [e * e, dy * hn]


def _onorm_fwd_fn(rv, vv):
    o, g = rv
    gain, = vv
    r = lax.rsqrt(jnp.mean(o * o, axis=-1, keepdims=True) + EPS)
    return [o * r * gain * (g * _sigmoid(g))], []


def _onorm_bwd_fn(rv, vv):
    o, g, don = rv
    gain, = vv
    d = o.shape[-1]
    r = lax.rsqrt(jnp.mean(o * o, axis=-1, keepdims=True) + EPS)
    sg = _sigmoid(g)
    sl = g * sg
    n = o * r
    dg = don * n * gain * (sg * (1.0 + g * (1.0 - sg)))
    gy = don * sl * gain
    do = r * gy - o * ((r * r * r) * (1.0 / d) * jnp.sum(gy * o, axis=-1, keepdims=True))
    return [do, dg], [don * sl * n]


HALO = SUBLANES


def _col_call(fn, cols, vecs, outs, n_acc, name, before, after, tc=LANES, chunk=256):
    T = cols[0][0].shape[0]
    chunk = min(chunk, T)
    nch = T // chunk
    ncol = outs[0][1] // tc
    n_c, n_v, n_o = len(cols), len(vecs), len(outs)
    hb = HALO if before else 0
    rw = chunk + hb + (HALO if after else 0)

    def body(*refs):
        c_refs = refs[:n_c]
        v_refs = refs[n_c:n_c + n_v]
        o_refs = refs[n_c + n_v:n_c + n_v + n_o]
        a_refs = refs[n_c + n_v + n_o:]
        vv = [v[...] for v in v_refs]
        wrow = lax.broadcasted_iota(jnp.int32, (rw, tc), 0)
        inside = (wrow >= hb) & (wrow < hb + chunk)

        def step(i, carry):
            r0 = pl.multiple_of(i * chunk, chunk)
            wins = []
            for ref in c_refs:
                parts = []
                if before:
                    pb = ref[pl.ds(pl.multiple_of(jnp.maximum(r0 - HALO, 0), HALO), HALO), :]
                    parts.append(jnp.where(i > 0, pb, 0.0))
                parts.append(ref[pl.ds(r0, chunk), :])
                if after:
                    pa = ref[pl.ds(pl.multiple_of(jnp.minimum(r0 + chunk, T - HALO), HALO), HALO), :]
                    parts.append(jnp.where(i < nch - 1, pa, 0.0))
                wins.append(jnp.concatenate(parts, axis=0) if len(parts) > 1 else parts[0])
            o_vals, a_vals = fn(wins, vv, inside)
            p = 0
            for o, (nseg, _, _) in zip(o_refs, outs):
                for s in range(nseg):
                    o[s, pl.ds(r0, chunk), :] = o_vals[p][hb:hb + chunk].astype(o.dtype)
                    p += 1
            return tuple(c + a for c, a in zip(carry, a_vals))

        taps = [v.shape[0] for v, _ in vecs][:n_acc]
        init = tuple(jnp.zeros((1, tc), F32) for k in taps for _ in range(k))
        sums = lax.fori_loop(0, nch, step, init)
        arow = lax.broadcasted_iota(jnp.int32, (SUBLANES, tc), 0)
        p = 0
        for a, k in zip(a_refs, taps):
            acc = jnp.zeros((SUBLANES, tc), F32)
            for t in range(k):
                acc = jnp.where(arow == t, sums[p], acc)
                p += 1
            a[...] = acc

    in_specs = [pl.BlockSpec((T, tc), functools.partial(lambda j, off: (0, off + j), off=off)) for _, off in cols]
    in_specs += [pl.BlockSpec((v.shape[0], tc), functools.partial(lambda j, off: (0, off + j), off=off))
                 for v, off in vecs]
    out_specs = [pl.BlockSpec((nseg, T, tc), lambda j: (0, 0, j)) for nseg, _, _ in outs]
    out_specs += [pl.BlockSpec((SUBLANES, tc), lambda j: (0, j)) for _ in range(n_acc)]
    out_shape = [jax.ShapeDtypeStruct((nseg, T, w), dt) for nseg, w, dt in outs]
    out_shape += [jax.ShapeDtypeStruct((SUBLANES, ncol * tc), F32) for _ in range(n_acc)]
    return _pcall(
        body, name=name, grid=(ncol,), in_specs=in_specs, out_specs=out_specs, out_shape=out_shape,
        compiler_params=_params(("parallel",)),
    )(*[c[0] for c in cols], *[v[0] for v in vecs])


def _down(x, k):
    return x if k == 0 else pltpu.roll(x, k, 0)


def _up(x, k):
    return x if k == 0 else pltpu.roll(x, x.shape[0] - k, 0)


def _lags(x):
    return _down(x, 2), _down(x, 1), x


def _conv(lags, w):
    return w[0:1] * lags[0] + w[1:2] * lags[1] + w[2:3] * lags[2]


def _conv_t(d, w):
    return w[2:3] * d + w[1:2] * _up(d, 1) + w[0:1] * _up(d, 2)


def _tap_sums(d, lags, inside):
    dm = jnp.where(inside, d, 0.0)
    return [jnp.sum(dm * lag, axis=0, keepdims=True) for lag in lags]


def _glu_fwd_fn(wins, vv, inside):
    xg, xv = wins
    wg, wv = vv
    ug = _conv(_lags(xg), wg)
    uv = _conv(_lags(xv), wv)
    return [ug * _sigmoid(ug) * uv], []


def _glu_bwd_fn(wins, vv, inside):
    xg, xv, da = wins
    wg, wv = vv
    lg, lv = _lags(xg), _lags(xv)
    ug = _conv(lg, wg)
    uv = _conv(lv, wv)
    sg = _sigmoid(ug)
    dug = da * uv * (sg * (1.0 + ug * (1.0 - sg)))
    duv = da * (ug * sg)
    return [_conv_t(dug, wg), _conv_t(duv, wv)], _tap_sums(dug, lg, inside) + _tap_sums(duv, lv, inside)


def _sc_fwd_fn(wins, vv, inside):
    gb, gc, hh = wins
    w, = vv
    return [gb * _conv(_lags(gc * hh), w)], []


def _sc_bwd_fn(wins, vv, inside):
    gb, gc, hh, dy = wins
    w, = vv
    lz = _lags(gc * hh)
    dcv = dy * gb
    dz = _conv_t(dcv, w)
    return [dy * _conv(lz, w), dz * hh, dz * gc], _tap_sums(dcv, lz, inside)


def _gates(qr, fr, lb):
    sg = _sigmoid(fr)
    f = lb + (1.0 - lb) * sg
    sq = _sigmoid(qr)
    q = qr * sq * (HEAD ** -0.5)
    return q, 1.0 - f, jnp.log(f), f, sg, sq


def _boundary_rows(b, g, row):
    c = b.shape[0]
    if 2 * g >= SUBLANES:
        x = b.reshape(c // (2 * g), 2 * g, LANES)
        return jnp.broadcast_to(x[:, g - 1:g, :], x.shape).reshape(c, LANES)
    x = b.reshape(c // SUBLANES, SUBLANES, LANES)
    lo = jnp.broadcast_to(x[:, 1:2, :], x.shape).reshape(c, LANES)
    hi = jnp.broadcast_to(x[:, 5:6, :], x.shape).reshape(c, LANES)
    return jnp.where((row & 4) == 0, lo, hi)


def _chunk_decays(gl, f, row):
    c = gl.shape[0]
    b = gl
    d = 1
    while d < c:
        b = b + jnp.where(row >= d, pltpu.roll(b, d, 0), 0.0)
        d *= 2
    eq, ek = [], []
    g = c // 2
    while g >= 2:
        right = (row & g) != 0
        m = _boundary_rows(b, g, row)
        z = jnp.exp(jnp.where(right, b - m, m - b))
        eq.append(jnp.where(right, z, 0.0))
        ek.append(jnp.where(right, 0.0, z))
        g //= 2
    odd = (row & 1) != 0
    eq.append(jnp.where(odd, f, 0.0))
    ek.append(jnp.where(odd, 0.0, 1.0))
    return b, eq, ek


def _intra(q, k, eq, ek, tt, ss):
    c = q.shape[0]
    qs, ks = [], []
    a = jnp.where(tt == ss, jnp.sum(q * k, axis=1, keepdims=True), 0.0)
    g = c // 2
    for e_q, e_k in zip(eq, ek):
        qg = (q * e_q).astype(BF16)
        kg = (k * e_k).astype(BF16)
        p = lax.dot_general(qg, kg, (((1,), (1,)), ((), ())), preferred_element_type=F32)
        a = a + (p if 2 * g >= c else jnp.where((tt ^ ss) < 2 * g, p, 0.0))
        qs.append(qg)
        ks.append(kg)
        g //= 2
    return a, qs, ks


def _hgrn_fwd(proj, lb, d_model):
    T = proj.shape[0]
    H = d_model // HEAD
    nch = T // CHUNK

    def body(q_ref, f_ref, v_ref, lb_ref, o_ref, s_ref):
        lbv = lb_ref[...]
        row = lax.broadcasted_iota(jnp.int32, (CHUNK, HEAD), 0)
        tt = lax.broadcasted_iota(jnp.int32, (CHUNK, CHUNK), 0)
        ss = lax.broadcasted_iota(jnp.int32, (CHUNK, CHUNK), 1)

        def step(i, st):
            sl = pl.ds(pl.multiple_of(i * CHUNK, CHUNK), CHUNK)
            q, k, gl, f, _, _ = _gates(q_ref[sl, :], f_ref[sl, :], lbv)
            v = v_ref[sl, :].astype(BF16)
            b, eq, ek = _chunk_decays(gl, f, row)
            a, _, _ = _intra(q, k, eq, ek, tt, ss)
            bl = b[CHUNK - 1:CHUNK, :]
            q0 = (q * jnp.exp(b)).astype(BF16)
            kh = (k * jnp.exp(bl - b)).astype(BF16)
            s_ref[i] = st
            o = jnp.dot(a.astype(BF16), v, preferred_element_type=F32)
            o = o + lax.dot_general(q0, st.astype(BF16), (((1,), (1,)), ((), ())), preferred_element_type=F32)
            o_ref[sl, :] = o
            return jnp.exp(bl) * st + lax.dot_general(v, kh, (((0,), (0,)), ((), ())), preferred_element_type=F32)

        per = 4 if nch % 4 == 0 else 2

        def trip(i, st):
            for u in range(per):
                st = step(per * i + u, st)
            return st

        lax.fori_loop(0, nch // per, trip, jnp.zeros((HEAD, HEAD), F32))

    col = lambda off: pl.BlockSpec((T, HEAD), functools.partial(lambda h, off: (0, off + h), off=off))
    return _pcall(
        body, name="hgrn_fwd", grid=(H,),
        in_specs=[col(0), col(H), col(2 * H), pl.BlockSpec((1, HEAD), lambda h: (0, h))],
        out_specs=[pl.BlockSpec((T, HEAD), lambda h: (0, h)),
                   pl.BlockSpec((None, nch, HEAD, HEAD), lambda h: (h, 0, 0, 0))],
        out_shape=[jax.ShapeDtypeStruct((T, d_model), F32), jax.ShapeDtypeStruct((H, nch, HEAD, HEAD), F32)],
        compiler_params=_params(("parallel",)),
    )(proj, proj, proj, lb)


def _hgrn_bwd(proj, lb, states, do, dgate, d_model):
    T = proj.shape[0]
    H = d_model // HEAD
    nch = T // CHUNK

    def body(q_ref, f_ref, v_ref, lb_ref, s_ref, do_ref, dg_ref, dp_ref, dlb_ref):
        dq_ref, df_ref, dv_ref = dp_ref.at[0], dp_ref.at[1], dp_ref.at[2]
        dp_ref[3] = dg_ref[...]
        lbv = lb_ref[...]
        row = lax.broadcasted_iota(jnp.int32, (CHUNK, HEAD), 0)
        tt = lax.broadcasted_iota(jnp.int32, (CHUNK, CHUNK), 0)
        ss = lax.broadcasted_iota(jnp.int32, (CHUNK, CHUNK), 1)
        last = row == CHUNK - 1
        nt = (((1,), (1,)), ((), ()))
        tn = (((0,), (0,)), ((), ()))

        def step(j, carry):
            dst, dlb = carry
            i = nch - 1 - j
            sl = pl.ds(pl.multiple_of(i * CHUNK, CHUNK), CHUNK)
            qr = q_ref[sl, :]
            q, k, gl, f, sg, sq = _gates(qr, f_ref[sl, :], lbv)
            v = v_ref[sl, :].astype(BF16)
            d_o = do_ref[sl, :].astype(BF16)
            st = s_ref[i]
            st16 = st.astype(BF16)
            dst16 = dst.astype(BF16)
            b, eq, ek = _chunk_decays(gl, f, row)
            a, qs, ks = _intra(q, k, eq, ek, tt, ss)
            bl = b[CHUNK - 1:CHUNK, :]
            e0 = jnp.exp(b)
            eh = jnp.exp(bl - b)
            ebl = jnp.exp(bl)
            q0 = q * e0
            kh = k * eh
            q016 = q0.astype(BF16)
            kh16 = kh.astype(BF16)
            dv = lax.dot_general(a.astype(BF16), d_o, tn, preferred_element_type=F32)
            dv = dv + lax.dot_general(kh16, dst16, nt, preferred_element_type=F32)
            dv_ref[sl, :] = dv.astype(dv_ref.dtype)
            da = lax.dot_general(d_o, v, nt, preferred_element_type=F32)
            da = jnp.where(tt >= ss, da, 0.0)
            dd = jnp.sum(jnp.where(tt == ss, da, 0.0), axis=1, keepdims=True)
            dq0 = jnp.dot(d_o, st16, preferred_element_type=F32)
            dkh = jnp.dot(v, dst16, preferred_element_type=F32)
            dq = dq0 * e0 + dd * k
            dk = dkh * eh + dd * q
            db = dq0 * q016.astype(F32) - dkh * kh16.astype(F32)
            g = CHUNK // 2
            for e_q, e_k, qg, kg in zip(eq, ek, qs, ks):
                dag = (da if 2 * g >= CHUNK else jnp.where((tt ^ ss) < 2 * g, da, 0.0)).astype(BF16)
                dqg = jnp.dot(dag, kg, preferred_element_type=F32)
                dkg = lax.dot_general(dag, qg, tn, preferred_element_type=F32)
                dq = dq + dqg * e_q
                dk = dk + dkg * e_k
                db = db + (dqg * qg.astype(F32) - dkg * kg.astype(F32))
                g //= 2
            dbl = jnp.sum(dkh * kh16.astype(F32), axis=0, keepdims=True) + ebl * jnp.sum(dst * st, axis=0, keepdims=True)
            db = db + jnp.where(last, dbl, 0.0)
            d = 1
            while d < CHUNK:
                db = db + jnp.where(row < CHUNK - d, pltpu.roll(db, CHUNK - d, 0), 0.0)
                d *= 2
            dfg = db / f - dk
            df_ref[sl, :] = (dfg * (1.0 - lbv) * sg * (1.0 - sg)).astype(df_ref.dtype)
            dq_ref[sl, :] = (dq * (HEAD ** -0.5) * (sq * (1.0 + qr * (1.0 - sq)))).astype(dq_ref.dtype)
            dlb = dlb + jnp.sum(dfg * (1.0 - sg), axis=0, keepdims=True)
            dst = ebl * dst + lax.dot_general(d_o, q016, tn, preferred_element_type=F32)
            return dst, dlb

        _, dlb = lax.fori_loop(0, nch // 2, lambda j, cr: step(2 * j + 1, step(2 * j, cr)),
                               (jnp.zeros((HEAD, HEAD), F32), jnp.zeros((1, HEAD), F32)))
        arow = lax.broadcasted_iota(jnp.int32, (SUBLANES, HEAD), 0)
        dlb_ref[...] = jnp.where(arow == 0, dlb, 0.0)

    col = lambda off: pl.BlockSpec((T, HEAD), functools.partial(lambda h, off: (0, off + h), off=off))
    return _pcall(
        body, name="hgrn_bwd", grid=(H,),
        in_specs=[col(0), col(H), col(2 * H), pl.BlockSpec((1, HEAD), lambda h: (0, h)),
                  pl.BlockSpec((None, nch, HEAD, HEAD), lambda h: (h, 0, 0, 0)), col(0), col(0)],
        out_specs=[pl.BlockSpec((4, T, HEAD), lambda h: (0, 0, h)), pl.BlockSpec((SUBLANES, HEAD), lambda h: (0, h))],
        out_shape=[jax.ShapeDtypeStruct((4, T, d_model), BF16), jax.ShapeDtypeStruct((SUBLANES, d_model), F32)],
        compiler_params=_params(("parallel",)),
    )(proj, proj, proj, lb, states, do, dgate)


def _lb_softmax(table):
    n, f = table.shape

    def body(t_ref, p_ref):
        t = t_ref[...]
        e = jnp.exp(t - jnp.max(t, axis=0, keepdims=True))
        p_ref[...] = e / jnp.sum(e, axis=0, keepdims=True)

    padded = jnp.pad(table, ((0, SUBLANES - n), (0, 0)), constant_values=-jnp.inf)
    return _pcall(body, name="lb_softmax", out_shape=jax.ShapeDtypeStruct((SUBLANES, f), F32))(padded)


def _to_bf16(w, layer, nlayers):
    R = w.shape[0] // nlayers
    C = w.shape[1]
    tr = _pick(R, (256, 128, 64, 32, 16))
    nb = R // tr

    def body(w_ref, o_ref):
        o_ref[...] = w_ref[...].astype(o_ref.dtype)

    return _pcall(
        body, name="to_bf16", grid=(nb,), in_specs=[pl.BlockSpec((tr, C), lambda i: (layer * nb + i, 0))],
        out_specs=pl.BlockSpec((tr, C), lambda i: (i, 0)), out_shape=jax.ShapeDtypeStruct((R, C), BF16),
        compiler_params=_params(("parallel",)),
    )(w)


def _adamw_math(w, g, m, v):
    m = ADAM_B1 * m + (1.0 - ADAM_B1) * g
    v = ADAM_B2 * v + (1.0 - ADAM_B2) * (g * g)
    m_hat = m / (1.0 - ADAM_B1 ** ADAM_STEP)
    v_hat = v / (1.0 - ADAM_B2 ** ADAM_STEP)
    delta = -ADAM_LR * (m_hat / (jnp.sqrt(v_hat) + ADAM_EPS) + ADAM_WD * w)
    return delta, m, v


def _adamw(w, g, m, v, name):
    R, C = w.shape
    tr = _pick(R, (128, 64, 32, 16, 8))

    def body(w_ref, g_ref, m_ref, v_ref, d_ref, nm_ref, nv_ref):
        d, nm, nv = _adamw_math(w_ref[...], g_ref[...], m_ref[...], v_ref[...])
        d_ref[...] = d
        nm_ref[...] = nm
        nv_ref[...] = nv

    spec = pl.BlockSpec((tr, C), lambda i: (i, 0))
    return _pcall(
        body, name=name, grid=(R // tr,), in_specs=[spec] * 4, out_specs=[spec] * 3,
        out_shape=[jax.ShapeDtypeStruct((R, C), F32)] * 3, compiler_params=_params(("parallel",)),
    )(w, g, m, v)


def _adamw_halves(w, m, v, g_mine, g_recv, c, name, layer=0, prev=None):
    C = w.shape[1]
    rh = g_mine.shape[0]
    tr = _pick(rh, (128, 64, 32, 16, 8))
    nb = rh // tr
    r0 = layer * 2 * nb

    def body(c_ref, w_ref, m_ref, v_ref, gm_ref, gr_ref, *rest):
        g_ref, d_ref, nm_ref, nv_ref = rest[-4:]
        g = jnp.where(pl.program_id(0) == c_ref[0], gm_ref[...], gr_ref[...])
        d, nm, nv = _adamw_math(w_ref[...], g, m_ref[...], v_ref[...])
        g_ref[...] = g
        d_ref[...] = d
        nm_ref[...] = nm
        nv_ref[...] = nv

    full = pl.BlockSpec((tr, C), lambda h, i, cr: (r0 + h * nb + i, 0))
    mine = pl.BlockSpec((tr, C), lambda h, i, cr: (jnp.where(h == cr[0], i, 0), 0))
    recv = pl.BlockSpec((tr, C), lambda h, i, cr: (jnp.where(h == cr[0], 0, i), 0))
    in_specs = [full, full, full, mine, recv]
    args = [c, w, m, v, g_mine, g_recv]
    alias = {}
    if prev is not None:
        in_specs += [pl.BlockSpec(memory_space=pl.ANY)] * 4
        args += list(prev)
        alias = {6 + k: k for k in range(4)}
    return _pcall(
        body, name=name,
        grid_spec=pltpu.PrefetchScalarGridSpec(
            num_scalar_prefetch=1, grid=(2, nb), in_specs=in_specs, out_specs=[full] * 4),
        out_shape=[jax.ShapeDtypeStruct(w.shape, F32)] * 4, input_output_aliases=alias,
        compiler_params=_params(("parallel", "parallel")),
    )(*args)


def _lb_table_grad(p8, dlb, n):
    f = p8.shape[1]

    def body(p_ref, d_ref, o_ref):
        p = p_ref[...]
        d = d_ref[...]
        p0 = p[0:1, :]
        first = lax.broadcasted_iota(jnp.int32, p.shape, 0) == 0
        o_ref[...] = p * (jnp.where(first, d, 0.0) - d * p0)

    return _pcall(body, name="lb_table_grad", out_shape=jax.ShapeDtypeStruct((SUBLANES, f), F32))(p8, dlb)[:n]


def _place():
    x, y, c = lax.axis_index("x"), lax.axis_index("y"), lax.axis_index("c")
    chips = [(1 - x, y), (x, 1 - y), (1 - x, 1 - y)]
    return x, y, c, chips


HBM_SPEC = pl.BlockSpec(memory_space=pltpu.HBM)


def _gather_taps(shards):
    n = len(shards)

    def body(*refs):
        ins, outs = refs[:n], refs[n:2 * n]
        send_sems, recv_sems = refs[2 * n:]
        x, y, c, chips = _place()
        me = 2 * x + y
        peers = [(*chip, c) for chip in chips] + [(x, y, 1 - c)]
        sends = [pltpu.make_async_remote_copy(
            src_ref=ins[t], dst_ref=outs[t].at[me], send_sem=send_sems.at[4 * t + j], recv_sem=recv_sems.at[4 * t + j],
            device_id=peer, device_id_type=MESH) for t in range(n) for j, peer in enumerate(peers)]
        for cp in sends:
            cp.start()
        for t in range(n):
            for j, (px, py, _) in enumerate(peers):
                landed = outs[t].at[2 * px + py]
                pltpu.make_async_remote_copy(
                    src_ref=landed, dst_ref=landed, send_sem=send_sems.at[4 * t + j], recv_sem=recv_sems.at[4 * t + j],
                    device_id=(x, y, c), device_id_type=MESH).wait_recv()
        for cp in sends:
            cp.wait_send()

    return _pcall(
        body, name="gather_taps", in_specs=[HBM_SPEC] * n, out_specs=[HBM_SPEC] * n,
        out_shape=[jax.ShapeDtypeStruct((N_CHIPS,) + a.shape, a.dtype) for a in shards],
        scratch_shapes=[pltpu.SemaphoreType.DMA((4 * n,)), pltpu.SemaphoreType.DMA((4 * n,))],
    )(*shards)


SEM_SPEC = pl.BlockSpec(memory_space=pltpu.SEMAPHORE)
DATAFLOW = pltpu.SideEffectType.DATAFLOW_SIDE_EFFECTING
COPIES_PER_SHARD = 4


def _shard_copies(ins, lands, send_sems, recv_sems, base=0):
    x, y, c, chips = _place()
    me = 2 * x + y
    cps = []
    for t in range(len(ins)):
        rh = ins[t].shape[0] // 2
        half = pl.ds(pl.multiple_of(c * rh, rh), rh)
        for j, chip in enumerate(chips):
            k = COPIES_PER_SHARD * (base + t) + j
            cps.append(pltpu.make_async_remote_copy(
                src_ref=ins[t].at[half], dst_ref=lands[t].at[me, half], send_sem=send_sems.at[k],
                recv_sem=recv_sems.at[k], device_id=(*chip, c), device_id_type=MESH))
        k = COPIES_PER_SHARD * (base + t) + 3
        cps.append(pltpu.make_async_remote_copy(
            src_ref=ins[t], dst_ref=lands[t].at[me], send_sem=send_sems.at[k], recv_sem=recv_sems.at[k],
            device_id=(x, y, 1 - c), device_id_type=MESH))
    return cps


def _gather_start(shards, thru, name):
    n = len(shards)
    nops = 2 * n + len(thru)

    def body(*refs):
        ins, lands = refs[:n], refs[n:2 * n]
        send_sems, recv_sems = refs[nops], refs[nops + 1]
        for cp in _shard_copies(ins, lands, send_sems, recv_sems):
            cp.start()

    lands = [pltpu.with_memory_space_constraint(lax.empty((N_CHIPS,) + s.shape, s.dtype), pltpu.HBM) for s in shards]
    ops = [pltpu.with_memory_space_constraint(s, pltpu.HBM) for s in shards] + lands + list(thru)
    nsem = COPIES_PER_SHARD * n
    res = _pcall(
        body, name=name, in_specs=[HBM_SPEC] * nops,
        out_specs=[SEM_SPEC, SEM_SPEC] + [HBM_SPEC] * nops,
        out_shape=[pltpu.SemaphoreType.DMA((nsem,)), pltpu.SemaphoreType.DMA((nsem,))]
        + [pltpu.HBM(o.shape, o.dtype) for o in ops],
        input_output_aliases={i: 2 + i for i in range(nops)},
        compiler_params=pltpu.CompilerParams(has_side_effects=DATAFLOW),
    )(*ops)
    return res[0], res[1], res[2:2 + n], res[2 + n:2 + 2 * n], list(res[2 + 2 * n:])


def _gather_wait(send_sems, recv_sems, shards, lands, after, name, base=0, which=None):
    n = len(shards)

    def body(*refs):
        ins, lnd = refs[:n], refs[n:2 * n]
        ssem, rsem = refs[2 * n], refs[2 * n + 1]
        for k, cp in enumerate(_shard_copies(ins, lnd, ssem, rsem, base)):
            if which is None or k % COPIES_PER_SHARD in which:
                cp.wait_send()
                cp.wait_recv()

    res = _pcall(
        body, name=name,
        in_specs=[HBM_SPEC] * (2 * n) + [SEM_SPEC, SEM_SPEC, pl.BlockSpec(memory_space=pl.ANY)],
        out_specs=[HBM_SPEC] * (2 * n),
        out_shape=[pltpu.HBM(o.shape, o.dtype) for o in list(shards) + list(lands)],
        input_output_aliases={i: i for i in range(2 * n)},
        compiler_params=pltpu.CompilerParams(has_side_effects=DATAFLOW),
    )(*shards, *lands, send_sems, recv_sems, after)
    return res[:n], res[n:]


SIBLING_PAIR = 1


def _sibling_handshake():
    x, y, c, _ = _place()
    barrier = pltpu.get_barrier_semaphore()
    pl.semaphore_signal(barrier, inc=1, device_id=(x, y, 1 - c), device_id_type=MESH)
    pl.semaphore_wait(barrier, 1)


def _forward_copies(land, send_sems, recv_sems, which=(0, 1, 2)):
    x, y, c, chips = _place()
    rh = land.shape[1] // 2
    return [pltpu.make_async_remote_copy(
        src_ref=land.at[2 * cx + cy, pl.ds(pl.multiple_of(c * rh, rh), rh)],
        dst_ref=land.at[2 * cx + cy, pl.ds(pl.multiple_of(c * rh, rh), rh)],
        send_sem=send_sems.at[j], recv_sem=recv_sems.at[j], device_id=(x, y, 1 - c), device_id_type=MESH)
        for j, (cx, cy) in enumerate(chips) if j in which]


def _forward_start(land, thru, name, which=(0, 1, 2)):
    def body(land_ref, thru_ref, send_sems, recv_sems, out_ref, thru_out):
        _sibling_handshake()
        for cp in _forward_copies(land_ref, send_sems, recv_sems, which):
            cp.start()

    return _pcall(
        body, name=name, in_specs=[HBM_SPEC, HBM_SPEC], out_specs=[SEM_SPEC, SEM_SPEC, HBM_SPEC, HBM_SPEC],
        out_shape=[pltpu.SemaphoreType.DMA((3,)), pltpu.SemaphoreType.DMA((3,)), pltpu.HBM(land.shape, land.dtype),
                   pltpu.HBM(thru.shape, thru.dtype)],
        input_output_aliases={0: 2, 1: 3},
        compiler_params=pltpu.CompilerParams(has_side_effects=DATAFLOW, collective_id=SIBLING_PAIR),
    )(land, thru)


def _forward_wait(send_sems, recv_sems, land, after, name, which=(0, 1, 2)):
    def body(land_ref, ssem, rsem, after_ref, out_ref):
        for cp in _forward_copies(land_ref, ssem, rsem, which):
            cp.wait_send()
            cp.wait_recv()

    return _pcall(
        body, name=name, in_specs=[HBM_SPEC, SEM_SPEC, SEM_SPEC, pl.BlockSpec(memory_space=pl.ANY)],
        out_specs=HBM_SPEC, out_shape=pltpu.HBM(land.shape, land.dtype), input_output_aliases={0: 0},
        compiler_params=pltpu.CompilerParams(has_side_effects=DATAFLOW),
    )(land, send_sems, recv_sems, after)


def _sibling_copies(ins, lands, send_sems, recv_sems, other_half):
    x, y, c, _ = _place()
    return [pltpu.make_async_remote_copy(
        src_ref=ins[t].at[:, 1 - c] if other_half else ins[t], dst_ref=lands[t], send_sem=send_sems.at[t],
        recv_sem=recv_sems.at[t], device_id=(x, y, 1 - c), device_id_type=MESH) for t in range(len(ins))]


def _sibling_start(srcs, other_half, thru, name):
    n = len(srcs)
    nthru = 0 if thru is None else 1

    def body(*refs):
        ins, lands = refs[:n], refs[n:2 * n]
        send_sems, recv_sems = refs[2 * n + nthru], refs[2 * n + nthru + 1]
        _sibling_handshake()
        for cp in _sibling_copies(ins, lands, send_sems, recv_sems, other_half):
            cp.start()

    shapes = [(s.shape[0],) + s.shape[2:] if other_half else s.shape for s in srcs]
    lands = [pltpu.with_memory_space_constraint(lax.empty(sh, s.dtype), pltpu.HBM) for sh, s in zip(shapes, srcs)]
    ops = [pltpu.with_memory_space_constraint(s, pltpu.HBM) for s in srcs] + lands + ([] if thru is None else [thru])
    res = _pcall(
        body, name=name, in_specs=[HBM_SPEC] * len(ops),
        out_specs=[SEM_SPEC, SEM_SPEC] + [HBM_SPEC] * len(ops),
        out_shape=[pltpu.SemaphoreType.DMA((n,)), pltpu.SemaphoreType.DMA((n,))]
        + [pltpu.HBM(o.shape, o.dtype) for o in ops],
        input_output_aliases={i: 2 + i for i in range(len(ops))},
        compiler_params=pltpu.CompilerParams(has_side_effects=DATAFLOW, collective_id=SIBLING_PAIR),
    )(*ops)
    return res[0], res[1], res[2:2 + n], res[2 + n:2 + 2 * n], (None if thru is None else res[2 + 2 * n])


def _sibling_wait(send_sems, recv_sems, srcs, lands, other_half, after, name):
    n = len(srcs)

    def body(*refs):
        ins, lnd = refs[:n], refs[n:2 * n]
        ssem, rsem = refs[2 * n], refs[2 * n + 1]
        for cp in _sibling_copies(ins, lnd, ssem, rsem, other_half):
            cp.wait_send()
            cp.wait_recv()

    res = _pcall(
        body, name=name,
        in_specs=[HBM_SPEC] * (2 * n) + [SEM_SPEC, SEM_SPEC, pl.BlockSpec(memory_space=pl.ANY)],
        out_specs=[HBM_SPEC] * (2 * n),
        out_shape=[pltpu.HBM(o.shape, o.dtype) for o in list(srcs) + list(lands)],
        input_output_aliases={i: i for i in range(2 * n)},
        compiler_params=pltpu.CompilerParams(has_side_effects=DATAFLOW),
    )(*srcs, *lands, send_sems, recv_sems, after)
    return res[:n], res[n:]


def _chip_copies(ins, lands, send_sems, recv_sems):
    x, y, c, chips = _place()
    cps = []
    for t in range(len(ins)):
        for j, (cx, cy) in enumerate(chips):
            cps.append(pltpu.make_async_remote_copy(
                src_ref=ins[t].at[2 * cx + cy], dst_ref=lands[t].at[j],
                send_sem=send_sems.at[3 * t + j], recv_sem=recv_sems.at[3 * t + j],
                device_id=(cx, cy, c), device_id_type=MESH))
    return cps


def _chip_start(parts, thru, name):
    n = len(parts)

    def body(*refs):
        ins, lands = refs[:n], refs[n:2 * n]
        send_sems, recv_sems = refs[2 * n + 1], refs[2 * n + 2]
        for cp in _chip_copies(ins, lands, send_sems, recv_sems):
            cp.start()

    lands = [pltpu.with_memory_space_constraint(lax.empty((3,) + p.shape[1:], p.dtype), pltpu.HBM) for p in parts]
    ops = [pltpu.with_memory_space_constraint(p, pltpu.HBM) for p in parts] + lands + [thru]
    res = _pcall(
        body, name=name, in_specs=[HBM_SPEC] * (2 * n + 1),
        out_specs=[SEM_SPEC, SEM_SPEC] + [HBM_SPEC] * (2 * n + 1),
        out_shape=[pltpu.SemaphoreType.DMA((3 * n,)), pltpu.SemaphoreType.DMA((3 * n,))]
        + [pltpu.HBM(o.shape, o.dtype) for o in ops],
        input_output_aliases={i: 2 + i for i in range(2 * n + 1)},
        compiler_params=pltpu.CompilerParams(has_side_effects=DATAFLOW),
    )(*ops)
    return res[0], res[1], res[2:2 + n], res[2 + n:2 + 2 * n], res[2 + 2 * n]


def _chip_wait(send_sems, recv_sems, parts, lands, after, name):
    n = len(parts)

    def body(*refs):
        ins, lnd = refs[:n], refs[n:2 * n]
        ssem, rsem = refs[2 * n], refs[2 * n + 1]
        for cp in _chip_copies(ins, lnd, ssem, rsem):
            cp.wait_send()
            cp.wait_recv()

    res = _pcall(
        body, name=name,
        in_specs=[HBM_SPEC] * (2 * n) + [SEM_SPEC, SEM_SPEC, pl.BlockSpec(memory_space=pl.ANY)],
        out_specs=[HBM_SPEC] * (2 * n),
        out_shape=[pltpu.HBM(o.shape, o.dtype) for o in list(parts) + list(lands)],
        input_output_aliases={i: i for i in range(2 * n)},
        compiler_params=pltpu.CompilerParams(has_side_effects=DATAFLOW),
    )(*parts, *lands, send_sems, recv_sems, after)
    return res[:n], res[n:]


def _add_pair(grad, recv, c, name):
    s, _, rh, cc = grad.shape
    tr = _pick(rh, (256, 128, 64, 32, 16))

    def body(c_ref, g_ref, r_ref, o_ref):
        o_ref[...] = (g_ref[...].astype(F32) + r_ref[...].astype(F32)).astype(o_ref.dtype)

    return _pcall(
        body, name=name,
        grid_spec=pltpu.PrefetchScalarGridSpec(
            num_scalar_prefetch=1, grid=(s, rh // tr),
            in_specs=[pl.BlockSpec((None, None, tr, cc), lambda a, i, cr: (a, cr[0], i, 0)),
                      pl.BlockSpec((None, tr, cc), lambda a, i, cr: (a, i, 0))],
            out_specs=pl.BlockSpec((None, tr, cc), lambda a, i, cr: (a, i, 0))),
        out_shape=jax.ShapeDtypeStruct((s, rh, cc), BF16),
        compiler_params=_params(("parallel", "parallel")),
    )(c, grad, recv)


def _add_chips(part, recv, me, name):
    _, rh, cc = part.shape
    tr = _pick(rh, (256, 128, 64, 32, 16))

    def body(m_ref, p_ref, r_ref, o_ref):
        o_ref[...] = ((p_ref[...].astype(F32) + r_ref[0].astype(F32)) + r_ref[1].astype(F32)) + r_ref[2].astype(F32)

    return _pcall(
        body, name=name,
        grid_spec=pltpu.PrefetchScalarGridSpec(
            num_scalar_prefetch=1, grid=(rh // tr,),
            in_specs=[pl.BlockSpec((None, tr, cc), lambda i, mr: (mr[0], i, 0)),
                      pl.BlockSpec((3, tr, cc), lambda i, mr: (0, i, 0))],
            out_specs=pl.BlockSpec((tr, cc), lambda i, mr: (i, 0))),
        out_shape=jax.ShapeDtypeStruct((rh, cc), F32),
        compiler_params=_params(("parallel",)),
    )(me, part, recv)


def _all_sum(vec):
    rows = vec.shape[0]

    def body(v_ref, o_ref, buf, send_sems, recv_sems):
        x, y, c, _ = _place()
        me = 4 * x + 2 * y + c
        buf[me] = v_ref[...]
        cps = []
        for r in range(1, 8):
            fx, fy, fc = (r >> 2) & 1, (r >> 1) & 1, r & 1
            peer = (x ^ fx, y ^ fy, c ^ fc)
            cps.append(pltpu.make_async_remote_copy(
                src_ref=v_ref, dst_ref=buf.at[me], send_sem=send_sems.at[r - 1], recv_sem=recv_sems.at[r - 1],
                device_id=peer, device_id_type=MESH))
        for cp in cps:
            cp.start()
        for r in range(1, 8):
            src = me ^ r
            pltpu.make_async_remote_copy(
                src_ref=v_ref, dst_ref=buf.at[src], send_sem=send_sems.at[r - 1], recv_sem=recv_sems.at[r - 1],
                device_id=(x, y, c), device_id_type=MESH).wait_recv()
        for cp in cps:
            cp.wait_send()
        acc = buf[0]
        for d in range(1, 8):
            acc = acc + buf[d]
        o_ref[...] = acc

    return _pcall(
        body, name="all_sum_small",
        in_specs=[pl.BlockSpec(memory_space=pltpu.VMEM)], out_specs=pl.BlockSpec(memory_space=pltpu.VMEM),
        out_shape=jax.ShapeDtypeStruct((rows, LANES), F32),
        scratch_shapes=[pltpu.VMEM((8, rows, LANES), F32), pltpu.SemaphoreType.DMA((7,)), pltpu.SemaphoreType.DMA((7,))],
    )(vec)


def _pack(parts):
    flat = jnp.concatenate([p.reshape(-1) for p in parts])
    tile = SUBLANES * LANES
    pad = (-flat.shape[0]) % tile
    return jnp.pad(flat, (0, pad)).reshape(-1, LANES)


def _unpack(vec, shapes):
    flat = vec.reshape(-1)
    out, p = [], 0
    for s in shapes:
        n = 1
        for d in s:
            n *= d
        out.append(flat[p:p + n].reshape(s))
        p += n
    return out


def _local_step(x, tgt, norm_mix, norm_ffn, lb8, out_norm, final_norm, sc_conv, ffn_conv, first, arrive, reduce_start,
                reduce_finish):
    T, D = x.shape
    F2 = ffn_conv.shape[-1]
    FF = F2 // 2
    tm = _pick(T, (1024, 512, 256, 128))
    wide = (1536, 1408, 1024, 768, 512, 384, 256, 128)
    cw_h, cw_s, cw_u = 4 * D // N_CHIPS, 3 * D // N_CHIPS, F2 // N_CHIPS
    kp = FF // N_CHIPS
    tk_ff = kp if kp % LANES == 0 else LANES
    tn_d = _pick(D, (1024, 512, 256, 128))
    tk_w = _pick(D, (512, 256, 128))
    tn_h = _pick(cw_h, (1024, 512, 256, 128))
    tn_s = _pick(D // N_CHIPS, (512, 256, 128))
    tn_u = _pick(cw_u, wide)
    lb = lb8[0:1]
    wm_sq = _wmap_col(D, tn_d, 0)
    wm_sq1 = _wmap_col(D, D, 0)
    seg1 = lambda a: a.reshape((1,) + a.shape)

    def mix_in(h, w):
        return _row_call(_rms_fwd_fn, [(h, 0, D)], [w], [(D, BF16)], 0, "rms_fwd")[0]

    def rms_bwd(h, dxn, dh, w):
        return _row_call(_rms_bwd_fn, [(h, 0, D), (dxn, 0, D), (dh, 0, D)], [w], [(D, F32), (D, BF16)], 1, "rms_bwd")

    def ffn_fwd(h, i, fetch_up, behind_down=()):
        xn = mix_in(h, norm_ffn[i:i + 1])
        tn = _pick(cw_u, wide)
        fetch_down, xn = arrive("ffn_w_down%d" % i, xn)
        w_up = fetch_up(xn)
        up = _mm_nn(xn, w_up, _wmap_col(cw_u, tn, 0), D, F2, tm, D, tn, "ffn_up")
        nb = FF // LANES
        a = _col_call(_glu_fwd_fn, [(up, 0), (up, nb)], [(ffn_conv[i], 0), (ffn_conv[i], nb)], [(1, FF, BF16)], 0,
                      "glu_fwd", before=True, after=False)[0][0]
        later = []
        for name in behind_down:
            fetch, a = arrive(name, a)
            later.append(fetch)
        w_down = fetch_down(a)
        h2 = _mm_nn(a, w_down, _wmap_row(kp, tk_ff, 0), FF, D, tm, tk_ff, tn_d, "ffn_down", res=h, per_step=2)
        return h2, (xn, up, a), w_up, w_down, later

    def ffn_bwd(dh, dh16, h, saved, i, w_up, w_down):
        xn, up, a = saved
        g_down = _mm_tn(a, seg1(dh16), (N_CHIPS, 1, kp, D), _wmap_row(kp, tk_ff, 0), FF, D, tk_ff, tn_d,
                        "ffn_down_dw", tm=_pick(T, (2048, 1024, 512, 256, 128)))
        dh16 = reduce_start(("ffn_w_down", i), g_down, dh16)
        da = _mm_nt(seg1(dh16), w_down, _wmap_row(kp, tk_ff, 0), FF, D, tm, tk_ff, D, "ffn_down_dx")
        nb = FF // LANES
        dgv, cg, cv = _col_call(_glu_bwd_fn, [(up, 0), (up, nb), (da, 0)], [(ffn_conv[i], 0), (ffn_conv[i], nb)],
                                [(2, FF, BF16)], 2, "glu_bwd", before=True, after=True)
        g_up = _mm_tn(xn, dgv, (N_CHIPS, 1, D, cw_u), _wmap_col(cw_u, tn_u, 0), D, F2, tk_w, tn_u, "ffn_up_dw")
        dgv = reduce_start(("ffn_w_up", i), g_up, dgv)
        dxn = _mm_nt(dgv, w_up, _wmap_col(cw_u, tn_u, 0), D, F2, _pick(T, (512, 256, 128)), D, tn_u, "ffn_up_dx",
                     per_step=2)
        dh2, dh2_16, dnw = rms_bwd(h, dxn, dh, norm_ffn[i:i + 1])
        return dh2, reduce_finish(dh2_16), dnw, jnp.concatenate([cg[:3], cv[:3]], axis=1)

    h0 = x
    xn0 = mix_in(h0, norm_mix[0:1])
    proj, w_hin = first(xn0, lambda w, s, prev: _mm_nn_shard(xn0, w, s, tm, cw_h, "hgrn_in", prev))
    o, states = _hgrn_fwd(proj, lb, D)
    fetch_hout, o = arrive("hgrn_w_out", o)
    on = _row_call(_onorm_fwd_fn, [(o, 0, D), (proj, 3, D)], [out_norm], [(D, BF16)], 0, "onorm_fwd")[0]
    fetch_up0, on = arrive("ffn_w_up0", on)
    w_hout1 = fetch_hout(on).reshape(1, D, D)
    h1 = _mm_nn(on, w_hout1, wm_sq, D, D, _pick(T, (512, 256, 128)), D, tn_d, "hgrn_out", res=h0)
    h2, ffn0, w_up0, w_down0, (fetch_sin, fetch_sout) = ffn_fwd(h1, 0, fetch_up0, ("sc_w_in", "sc_w_out"))
    xn1 = mix_in(h2, norm_mix[1:2])
    w_sin = fetch_sin(xn1)
    tn_si = _pick(cw_s, wide)
    sproj = _mm_nn(xn1, w_sin, _wmap_col(cw_s, tn_si, 0), D, 3 * D, tm, D, tn_si, "sc_in")
    fetch_up1, sproj = arrive("ffn_w_up1", sproj)
    nd = D // LANES
    ysc = _col_call(_sc_fwd_fn, [(sproj, 0), (sproj, nd), (sproj, 2 * nd)], [(sc_conv, 0)], [(1, D, BF16)], 0,
                    "sc_fwd", before=True, after=False)[0][0]
    w_sout1 = fetch_sout(ysc).reshape(1, D, D)
    h3 = _mm_nn(ysc, w_sout1, wm_sq, D, D, _pick(T, (512, 256, 128)), D, tn_d, "sc_out", res=h2)
    h4, ffn1, w_up1, w_down1, _ = ffn_fwd(h3, 1, fetch_up1)

    dh, dh16, esq, dfinal = _row_call(_final_fn, [(h4, 0, D), (tgt, 0, D)], [final_norm], [(D, F32), (D, BF16)], 2,
                                      "final_loss")
    loss = 0.5 / D * jnp.sum(esq)
    dh, dh16, dnf1, dconv1 = ffn_bwd(dh, dh16, h3, ffn1, 1, w_up1, w_down1)
    g_sout = _mm_tn(ysc, seg1(dh16), (1, 1, D, D), wm_sq, D, D, tk_w, tn_d, "sc_out_dw")
    dh16 = reduce_start(("sc_w_out", 0), g_sout, dh16)
    dy = _mm_nt(seg1(dh16), w_sout1, wm_sq1, D, D, tm, D, D, "sc_out_dx")
    dsp, dscc = _col_call(_sc_bwd_fn, [(sproj, 0), (sproj, nd), (sproj, 2 * nd), (dy, 0)], [(sc_conv, 0)],
                          [(3, D, BF16)], 1, "sc_bwd", before=True, after=True)
    g_sin = _mm_tn(xn1, dsp, (N_CHIPS, 1, D, cw_s), _wmap_col(cw_s, tn_s, 0), D, 3 * D, tk_w, tn_s, "sc_in_dw")
    dsp = reduce_start(("sc_w_in", 0), g_sin, dsp)
    dxn = _mm_nt(dsp, w_sin, _wmap_col(cw_s, tn_s, 0), D, 3 * D, tm, D, tn_s, "sc_in_dx", per_step=3)
    dh, dh16, dnm1 = rms_bwd(h2, dxn, dh, norm_mix[1:2])
    dh16 = reduce_finish(dh16)
    dh, dh16, dnf0, dconv0 = ffn_bwd(dh, dh16, h1, ffn0, 0, w_up0, w_down0)
    g_hout = _mm_tn(on, seg1(dh16), (1, 1, D, D), wm_sq, D, D, tk_w, tn_d, "hgrn_out_dw")
    dh16 = reduce_start(("hgrn_w_out", 0), g_hout, dh16)
    don = _mm_nt(seg1(dh16), w_hout1, wm_sq1, D, D, tm, D, D, "hgrn_out_dx")
    do, dgate, dgain = _row_call(_onorm_bwd_fn, [(o, 0, D), (proj, 3, D), (don, 0, D)], [out_norm],
                                 [(D, F32), (D, BF16)], 1, "onorm_bwd")
    dproj, dlb = _hgrn_bwd(proj, lb, states, do, dgate, D)
    g_hin = _mm_tn(xn0, dproj, (N_CHIPS, 1, D, cw_h), _wmap_col(cw_h, tn_h, 0), D, 4 * D, tk_w, tn_h, "hgrn_in_dw")
    dproj = reduce_start(("hgrn_w_in", 0), g_hin, dproj)
    dxn = _mm_nt(dproj, w_hin, _wmap_col(cw_h, tn_h, 0), D, 4 * D, tm, D, tn_h, "hgrn_in_dx", per_step=2)
    grad_x, _, dnm0 = rms_bwd(h0, dxn, dh, norm_mix[0:1])

    small = dict(
        loss=loss,
        norm_mix=jnp.stack([jnp.sum(dnm0, axis=0), jnp.sum(dnm1, axis=0)]),
        norm_ffn=jnp.stack([jnp.sum(dnf0, axis=0), jnp.sum(dnf1, axis=0)]),
        lb=dlb[0:1],
        out_norm=jnp.sum(dgain, axis=0)[None],
        final_norm=jnp.sum(dfinal, axis=0),
        sc_conv=dscc[:3],
        ffn_conv=jnp.stack([dconv0, dconv1]),
    )
    return grad_x, small


def kernel(x, norm_mix, norm_ffn, hgrn_w_in, hgrn_lb_table, hgrn_out_norm, hgrn_w_out, sc_w_in, sc_conv, sc_w_out, ffn_w_up, ffn_conv, ffn_w_down, final_norm, loss_target, m_norm_mix, m_norm_ffn, m_hgrn_w_in, m_hgrn_lb_table, m_hgrn_out_norm, m_hgrn_w_out, m_sc_w_in, m_sc_conv, m_sc_w_out, m_ffn_w_up, m_ffn_conv, m_ffn_w_down, m_final_norm, v_norm_mix, v_norm_ffn, v_hgrn_w_in, v_hgrn_lb_table, v_hgrn_out_norm, v_hgrn_w_out, v_sc_w_in, v_sc_conv, v_sc_w_out, v_ffn_w_up, v_ffn_conv, v_ffn_w_down, v_final_norm):
    D = x.shape[-1]
    xi, yi, ci = lax.axis_index("x"), lax.axis_index("y"), lax.axis_index("c")
    me_chip = (2 * xi + yi).astype(jnp.int32).reshape(1)
    me_core = ci.astype(jnp.int32).reshape(1)

    big_names = ["hgrn_w_in", "hgrn_w_out", "sc_w_in", "sc_w_out", "ffn_w_up", "ffn_w_down"]
    big_w = dict(hgrn_w_in=hgrn_w_in, hgrn_w_out=hgrn_w_out, sc_w_in=sc_w_in, sc_w_out=sc_w_out,
                 ffn_w_up=ffn_w_up, ffn_w_down=ffn_w_down)
    big_m = dict(hgrn_w_in=m_hgrn_w_in, hgrn_w_out=m_hgrn_w_out, sc_w_in=m_sc_w_in, sc_w_out=m_sc_w_out,
                 ffn_w_up=m_ffn_w_up, ffn_w_down=m_ffn_w_down)
    big_v = dict(hgrn_w_in=v_hgrn_w_in, hgrn_w_out=v_hgrn_w_out, sc_w_in=v_sc_w_in, sc_w_out=v_sc_w_out,
                 ffn_w_up=v_ffn_w_up, ffn_w_down=v_ffn_w_down)
    flat2 = lambda a: a.reshape(-1, a.shape[-1])

    sh = lambda a, layer=0: _to_bf16(flat2(a), layer, a.shape[0])
    in_order_of_use = [("hgrn_w_in", sh(hgrn_w_in)), ("hgrn_w_out", sh(hgrn_w_out)), ("ffn_w_up0", sh(ffn_w_up, 0)),
                       ("ffn_w_down0", sh(ffn_w_down, 0)), ("sc_w_in", sh(sc_w_in)), ("sc_w_out", sh(sc_w_out)),
                       ("ffn_w_up1", sh(ffn_w_up, 1)), ("ffn_w_down1", sh(ffn_w_down, 1))]
    scc4, fcc4 = _gather_taps([flat2(sc_conv), flat2(ffn_conv)])
    names = [n for n, _ in in_order_of_use]
    ss, rs, src, land, (scc4, norm_mix) = _gather_start([s for _, s in in_order_of_use], [scc4, norm_mix], "gather_start")
    travelling = {n: (s, l) for n, s, l in zip(names, src, land)}

    def landed(name, after, call, which=None):
        (s,), (l,) = _gather_wait(ss, rs, [travelling[name][0]], [travelling[name][1]], after, call,
                                  base=names.index(name), which=which)
        travelling[name] = (s, l)
        return l

    scc = jnp.moveaxis(scc4, 0, 1).reshape(3, D)
    f2 = ffn_conv.shape[-1] * N_CHIPS
    fcc = jnp.moveaxis(fcc4.reshape(N_CHIPS, 2, 3, -1), 0, 2).reshape(2, 3, f2)

    def first(after, matmul):
        w = landed("hgrn_w_in", after, "gather_wait_0_own", which=(3,))
        proj = matmul(w, me_chip, None)
        others = [2 * (1 - xi) + yi, 2 * xi + (1 - yi), 2 * (1 - xi) + (1 - yi)]
        for j, s in enumerate(others):
            w = landed("hgrn_w_in", proj, "gather_wait_0_%d" % j, which=(j,))
            fs, fr, w, proj = _forward_start(w, proj, "gather_forward_start_0_%d" % j, which=(j,))
            w = _forward_wait(fs, fr, w, proj, "gather_forward_wait_0_%d" % j, which=(j,))
            travelling["hgrn_w_in"] = (travelling["hgrn_w_in"][0], w)
            proj = matmul(w, s.astype(jnp.int32).reshape(1), proj)
        return proj, w

    def arrive(name, after):
        fs, fr, w, after = _forward_start(landed(name, after, "gather_wait_" + name), after, "gather_forward_start_" + name)
        return functools.partial(_forward_wait, fs, fr, w, name="gather_forward_wait_" + name), after

    pending = []
    started = []

    def reduce_start(slot, grad, thru):
        t = sum(len(b[0]) for b in pending) + len(started)
        halves = grad.reshape(N_CHIPS, 2, -1, grad.shape[-1])
        ss, rs, src, land, thru = _sibling_start([halves], True, thru, "grad_pair_start_%d" % t)
        started.append((slot, t, ss, rs, src, land))
        return thru

    def reduce_finish(thru):
        k = len(pending)
        pair = []
        for slot, t, ss, rs, src, land in started:
            src, recv = _sibling_wait(ss, rs, src, land, True, thru, "grad_pair_wait_%d" % t)
            pair.append(_add_pair(src[0], recv[0], me_core, "grad_add_pair"))
        ss, rs, pair, land, thru = _chip_start(pair, thru, "grad_chip_start_%d" % k)
        pending.append(([s[0] for s in started], ss, rs, pair, land))
        started.clear()
        return thru

    lb8 = _lb_softmax(hgrn_lb_table)
    grad_x, small = _local_step(
        x[0], loss_target[0], norm_mix, norm_ffn, lb8, hgrn_out_norm, final_norm[None], scc, fcc, first, arrive,
        reduce_start, reduce_finish)

    small_names = ["loss", "norm_mix", "norm_ffn", "lb", "out_norm", "final_norm", "sc_conv", "ffn_conv"]
    parts = [small[n].astype(F32) for n in small_names]
    shapes = [p.shape for p in parts]
    tot = dict(zip(small_names, _unpack(reduce_finish(_all_sum(_pack(parts))), shapes)))
    loss = tot["loss"].reshape(())
    g_lb_table = _lb_table_grad(lb8, tot["lb"], hgrn_lb_table.shape[0])
    cw = sc_conv.shape[-1]
    g_sc_conv = lax.dynamic_slice_in_dim(tot["sc_conv"], me_chip[0] * cw, cw, axis=1)[None]
    cf = ffn_conv.shape[-1]
    g_ffn_conv = lax.dynamic_slice_in_dim(tot["ffn_conv"], me_chip[0] * cf, cf, axis=2)
    g_small = dict(norm_mix=tot["norm_mix"], norm_ffn=tot["norm_ffn"], hgrn_lb_table=g_lb_table,
                   hgrn_out_norm=tot["out_norm"], sc_conv=g_sc_conv, ffn_conv=g_ffn_conv, final_norm=tot["final_norm"])
    w_small = dict(norm_mix=norm_mix, norm_ffn=norm_ffn, hgrn_lb_table=hgrn_lb_table, hgrn_out_norm=hgrn_out_norm,
                   sc_conv=sc_conv, ffn_conv=ffn_conv, final_norm=final_norm)
    m_small = dict(norm_mix=m_norm_mix, norm_ffn=m_norm_ffn, hgrn_lb_table=m_hgrn_lb_table, hgrn_out_norm=m_hgrn_out_norm,
                   sc_conv=m_sc_conv, ffn_conv=m_ffn_conv, final_norm=m_final_norm)
    v_small = dict(norm_mix=v_norm_mix, norm_ffn=v_norm_ffn, hgrn_lb_table=v_hgrn_lb_table, hgrn_out_norm=v_hgrn_out_norm,
                   sc_conv=v_sc_conv, ffn_conv=v_ffn_conv, final_norm=v_final_norm)
    sm_names = list(g_small)
    sm_shapes = [w_small[n].shape for n in sm_names]
    d_s, m_s, v_s = _adamw(_pack([w_small[n] for n in sm_names]), _pack([g_small[n] for n in sm_names]),
                           _pack([m_small[n] for n in sm_names]), _pack([v_small[n] for n in sm_names]), "adamw_small")
    out_g, out_d, out_m, out_v = dict(g_small), {}, {}, {}
    for n, d_, m_, v_ in zip(sm_names, _unpack(d_s, sm_shapes), _unpack(m_s, sm_shapes), _unpack(v_s, sm_shapes)):
        out_d[n], out_m[n], out_v[n] = d_, m_, v_

    done = {}
    after = grad_x

    def add_and_share(k, after):
        slots, ss, rs, pair, land = pending[k]
        pair, recv = _chip_wait(ss, rs, pair, land, after, "grad_chip_wait_%d" % k)
        mine = [_add_chips(p, r, me_chip, "grad_add_chips") for p, r in zip(pair, recv)]
        ss, rs, mine, land, _ = _sibling_start(mine, False, None, "grad_share_start_%d" % k)
        return slots, ss, rs, mine, land

    def update(k, share, after):
        slots, ss, rs, mine, land = share
        mine, theirs = _sibling_wait(ss, rs, mine, land, False, after, "grad_share_wait_%d" % k)
        for (n, layer), gm, gr in zip(slots, mine, theirs):
            done[n] = _adamw_halves(flat2(big_w[n]), flat2(big_m[n]), flat2(big_v[n]), gm, gr, me_core, "adamw_" + n,
                                    layer=layer, prev=done.get(n))
        return done[slots[-1][0]][0]

    shares = []
    for k in range(len(pending) - 1):
        shares.append(add_and_share(k, after))
        after = shares[-1][3][0]
    for k, share in enumerate(shares):
        after = update(k, share, after)
    last = len(pending) - 1
    share = add_and_share(last, after)
    update(last, share, share[3][0])
    for n in big_names:
        out_g[n], out_d[n], out_m[n], out_v[n] = (a.reshape(big_w[n].shape) for a in done[n])

    order = ["norm_mix", "norm_ffn", "hgrn_w_in", "hgrn_lb_table", "hgrn_out_norm", "hgrn_w_out", "sc_w_in", "sc_conv",
             "sc_w_out", "ffn_w_up", "ffn_conv", "ffn_w_down", "final_norm"]
    return (loss, grad_x[None], *[out_g[n] for n in order], *[out_d[n] for n in order],
            *[out_m[n] for n in order], *[out_v[n] for n in order])
```

```python
import functools

import jax
import jax.numpy as jnp
from jax import lax
from jax.experimental import pallas as pl
from jax.experimental.pallas import tpu as pltpu

F32 = jnp.float32
BF16 = jnp.bfloat16
MESH = pl.DeviceIdType.MESH

EPS = 1e-6
CHUNK = 128
HEAD = 128
N_CHIPS = 4
ADAM_LR, ADAM_B1, ADAM_B2, ADAM_EPS, ADAM_WD, ADAM_STEP = 0.001, 0.9, 0.999, 1e-08, 0.01, 10
VMEM_LIMIT = 56 * 1024 * 1024
SUBLANES = 8
LANES = 128


def _pcall(body, **kw):
    return pl.pallas_call(body, **kw)


def _params(sem, vmem=VMEM_LIMIT):
    return pltpu.CompilerParams(dimension_semantics=sem, vmem_limit_bytes=vmem)


def _pick(dim, prefs):
    for p in prefs:
        if p <= dim and dim % p == 0:
            return p
    return dim


def _sigmoid(x):
    return 1.0 / (1.0 + jnp.exp(-x))


def _wmap_col(cw, tn, r0):
    bps = cw // tn
    return lambda kb, nb: (nb // bps, r0 + kb, nb % bps)


def _wmap_row(kp, tk, r0):
    bps = kp // tk
    return lambda kb, nb: (kb // bps, r0 + kb % bps, nb)


def _mm_nn(a, w3, wmap, K, N, tm, tk, tn, name, res=None, per_step=1):
    M = a.shape[0]
    u = per_step
    nk = K // (tk * u)

    def body(*refs):
        r_ref = None if res is None else refs[2 * u]
        o_ref = refs[2 * u + (0 if res is None else 1)]
        p = None
        for r in range(u):
            d = jnp.dot(refs[r][...], refs[u + r][...], preferred_element_type=F32)
            p = d if p is None else p + d
        if nk == 1:
            o_ref[...] = p if res is None else p + r_ref[...]
            return
        acc = refs[-1]
        k = pl.program_id(2)

        @pl.when(k == 0)
        def _():
            acc[...] = p

        @pl.when(k > 0)
        def _():
            acc[...] += p

        @pl.when(k == nk - 1)
        def _():
            o_ref[...] = acc[...] if res is None else acc[...] + r_ref[...]

    if nk == 1:
        grid = (M // tm, N // tn)
        ix = lambda f: (lambda i, j: f(i, j, 0))
        sem = ("parallel", "parallel")
        scratch = []
    else:
        grid = (M // tm, N // tn, nk)
        ix = lambda f: f
        sem = ("parallel", "parallel", "arbitrary")
        scratch = [pltpu.VMEM((tm, tn), F32)]
    def a_spec(r):
        return pl.BlockSpec((tm, tk), ix(lambda i, j, k: (i, k * u + r)))

    def w_spec(r):
        return pl.BlockSpec((None, tk, tn), ix(lambda i, j, k: wmap(k * u + r, j)))

    in_specs = [a_spec(r) for r in range(u)] + [w_spec(r) for r in range(u)]
    args = [a] * u + [w3] * u
    if res is not None:
        in_specs.append(pl.BlockSpec((tm, tn), ix(lambda i, j, k: (i, j))))
        args.append(res)
    return _pcall(
        body, name=name, grid=grid, in_specs=in_specs,
        out_specs=pl.BlockSpec((tm, tn), ix(lambda i, j, k: (i, j))),
        out_shape=jax.ShapeDtypeStruct((M, N), F32), scratch_shapes=scratch, compiler_params=_params(sem),
    )(*args)


def _mm_nn_shard(a, w3, s, tm, tn, name, prev=None):
    M, K = a.shape
    S, _, cw = w3.shape
    bps = cw // tn

    def body(s_ref, a_ref, w_ref, *rest):
        o_ref = rest[-1]
        o_ref[...] = jnp.dot(a_ref[...], w_ref[...], preferred_element_type=F32)

    in_specs = [pl.BlockSpec((tm, K), lambda i, j, sr: (i, 0)),
                pl.BlockSpec((None, K, tn), lambda i, j, sr: (sr[0], 0, j))]
    args = [s, a, w3]
    alias = {}
    if prev is not None:
        in_specs.append(pl.BlockSpec(memory_space=pl.ANY))
        args.append(prev)
        alias = {3: 0}
    return _pcall(
        body, name=name,
        grid_spec=pltpu.PrefetchScalarGridSpec(
            num_scalar_prefetch=1, grid=(M // tm, bps), in_specs=in_specs,
            out_specs=pl.BlockSpec((tm, tn), lambda i, j, sr: (i, sr[0] * bps + j))),
        out_shape=jax.ShapeDtypeStruct((M, S * cw), F32), input_output_aliases=alias,
        compiler_params=_params(("parallel", "parallel")),
    )(*args)


def _mm_nt(dy3, w3, wmap, K, N, tm, tk, tn, name, per_step=1):
    M = dy3.shape[1]
    bps = dy3.shape[2] // tn
    u = per_step
    grid = (M // tm, K // tk, N // (tn * u))
    nn = grid[2]

    def body(*refs):
        o_ref = refs[-1]
        p = None
        for r in range(u):
            d = lax.dot_general(refs[r][...], refs[u + r][...], (((1,), (1,)), ((), ())), preferred_element_type=F32)
            p = d if p is None else p + d
        if nn == 1:
            o_ref[...] = p
            return
        n = pl.program_id(2)

        @pl.when(n == 0)
        def _():
            o_ref[...] = p

        @pl.when(n > 0)
        def _():
            o_ref[...] += p

    def dy_spec(r):
        return pl.BlockSpec((None, tm, tn), lambda i, j, n: ((n * u + r) // bps, i, (n * u + r) % bps))

    def w_spec(r):
        return pl.BlockSpec((None, tk, tn), lambda i, j, n: wmap(j, n * u + r))

    return _pcall(
        body, name=name, grid=grid,
        in_specs=[dy_spec(r) for r in range(u)] + [w_spec(r) for r in range(u)],
        out_specs=pl.BlockSpec((tm, tk), lambda i, j, n: (i, j)),
        out_shape=jax.ShapeDtypeStruct((M, K), F32),
        compiler_params=_params(("parallel", "parallel", "arbitrary")),
    )(*([dy3] * u), *([w3] * u))


def _mm_tn(x, dy3, shape4, wmap, K, N, tk, tn, name, tm=None):
    M = x.shape[0]
    tm = M if tm is None else tm
    nm = M // tm
    bps = dy3.shape[2] // tn

    def body(*refs):
        x_ref, dy_ref = refs[:2]
        p = lax.dot_general(x_ref[...], dy_ref[...], (((0,), (0,)), ((), ())), preferred_element_type=F32)
        if nm == 1:
            o_ref = refs[-1]
            o_ref[...] = p.astype(o_ref.dtype)
            return
        o_ref, acc = refs[-2:]
        m = pl.program_id(2)

        @pl.when(m == 0)
        def _():
            acc[...] = p

        @pl.when(m > 0)
        def _():
            acc[...] += p

        @pl.when(m == nm - 1)
        def _():
            o_ref[...] = acc[...].astype(o_ref.dtype)

    def omap(i, j, m):
        s, rb, cb = wmap(i, j)
        return (s, 0, rb, cb)

    return _pcall(
        body, name=name, grid=(K // tk, N // tn, nm),
        in_specs=[pl.BlockSpec((tm, tk), lambda i, j, m: (m, i)),
                  pl.BlockSpec((None, tm, tn), lambda i, j, m: (j // bps, m, j % bps))],
        out_specs=pl.BlockSpec((None, None, tk, tn), omap),
        out_shape=jax.ShapeDtypeStruct(shape4, BF16),
        scratch_shapes=[] if nm == 1 else [pltpu.VMEM((tk, tn), F32)],
        compiler_params=_params(("parallel", "parallel", "arbitrary")),
    )(x, dy3)


def _row_call(fn, rows, vecs, outs, n_acc, name, t_rows=256, sub=16, per_trip=4):
    T = rows[0][0].shape[0]
    t_rows = min(t_rows, T)
    nsub = t_rows // sub
    n_r, n_v, n_o = len(rows), len(vecs), len(outs)
    width = rows[0][2]

    def body(*refs):
        r_refs = refs[:n_r]
        v_refs = refs[n_r:n_r + n_v]
        o_refs = refs[n_r + n_v:n_r + n_v + n_o]
        a_refs = refs[n_r + n_v + n_o:]

        @pl.when(pl.program_id(0) == 0)
        def _():
            for a in a_refs:
                a[...] = jnp.zeros_like(a)

        vv = [v[...] for v in v_refs]

        def step(i, carry):
            done = []
            for u in range(per_trip):
                sl = pl.ds(pl.multiple_of((i * per_trip + u) * sub, sub), sub)
                done.append((sl,) + tuple(fn([r[sl, :] for r in r_refs], vv)))
            for sl, o_vals, a_vals in done:
                for o, val in zip(o_refs, o_vals):
                    o[sl, :] = val.astype(o.dtype)
            for a_i, a in enumerate(a_refs):
                tot = None
                for _, _, a_vals in done:
                    part = a_vals[a_i].reshape(sub // SUBLANES, SUBLANES, a_vals[a_i].shape[-1]).sum(axis=0)
                    tot = part if tot is None else tot + part
                a[...] += tot
            return carry

        lax.fori_loop(0, nsub // per_trip, step, 0)

    in_specs = [pl.BlockSpec((t_rows, w), functools.partial(lambda i, cb: (i, cb), cb=cb)) for _, cb, w in rows]
    in_specs += [pl.BlockSpec(v.shape, lambda i: (0, 0)) for v in vecs]
    out_specs = [pl.BlockSpec((t_rows, w), lambda i: (i, 0)) for w, _ in outs]
    out_specs += [pl.BlockSpec((SUBLANES, width), lambda i: (0, 0)) for _ in range(n_acc)]
    out_shape = [jax.ShapeDtypeStruct((T, w), dt) for w, dt in outs]
    out_shape += [jax.ShapeDtypeStruct((SUBLANES, width), F32) for _ in range(n_acc)]
    return _pcall(
        body, name=name, grid=(T // t_rows,), in_specs=in_specs, out_specs=out_specs, out_shape=out_shape,
        compiler_params=_params(("arbitrary",)),
    )(*[r[0] for r in rows], *vecs)


def _rms_fwd_fn(rv, vv):
    h, = rv
    w, = vv
    r = lax.rsqrt(jnp.mean(h * h, axis=-1, keepdims=True) + EPS)
    return [h * r * w], []


def _rms_bwd_fn(rv, vv):
    h, dxn, dh_in = rv
    w, = vv
    d = h.shape[-1]
    r = lax.rsqrt(jnp.mean(h * h, axis=-1, keepdims=True) + EPS)
    gy = dxn * w
    dh = r * gy - h * ((r * r * r) * (1.0 / d) * jnp.sum(gy * h, axis=-1, keepdims=True))
    return [dh_in + dh] * 2, [dxn * h * r]


def _final_fn(rv, vv):
    h, tgt = rv
    w, = vv
    d = h.shape[-1]
    r = lax.rsqrt(jnp.mean(h * h, axis=-1, keepdims=True) + EPS)
    hn = h * r
    e = hn * w - tgt
    dy = e * (1.0 / d)
    gy = dy * w
    dh = r * gy - h * ((r * r * r) * (1.0 / d) * jnp.sum(gy * h, axis=-1, keepdims=True))
    return [dh] * 2, [e * e, dy * hn]


def _onorm_fwd_fn(rv, vv):
    o, g = rv
    gain, = vv
    r = lax.rsqrt(jnp.mean(o * o, axis=-1, keepdims=True) + EPS)
    return [o * r * gain * (g * _sigmoid(g))], []


def _onorm_bwd_fn(rv, vv):
    o, g, don = rv
    gain, = vv
    d = o.shape[-1]
    r = lax.rsqrt(jnp.mean(o * o, axis=-1, keepdims=True) + EPS)
    sg = _sigmoid(g)
    sl = g * sg
    n = o * r
    dg = don * n * gain * (sg * (1.0 + g * (1.0 - sg)))
    gy = don * sl * gain
    do = r * gy - o * ((r * r * r) * (1.0 / d) * jnp.sum(gy * o, axis=-1, keepdims=True))
    return [do, dg], [don * sl * n]


HALO = SUBLANES


def _col_call(fn, cols, vecs, outs, n_acc, name, before, after, tc=LANES, chunk=256):
    T = cols[0][0].shape[0]
    chunk = min(chunk, T)
    nch = T // chunk
    ncol = outs[0][1] // tc
    n_c, n_v, n_o = len(cols), len(vecs), len(outs)
    hb = HALO if before else 0
    rw = chunk + hb + (HALO if after else 0)

    def body(*refs):
        c_refs = refs[:n_c]
        v_refs = refs[n_c:n_c + n_v]
        o_refs = refs[n_c + n_v:n_c + n_v + n_o]
        a_refs = refs[n_c + n_v + n_o:]
        vv = [v[...] for v in v_refs]
        wrow = lax.broadcasted_iota(jnp.int32, (rw, tc), 0)
        inside = (wrow >= hb) & (wrow < hb + chunk)

        def step(i, carry):
            r0 = pl.multiple_of(i * chunk, chunk)
            wins = []
            for ref in c_refs:
                parts = []
                if before:
                    pb = ref[pl.ds(pl.multiple_of(jnp.maximum(r0 - HALO, 0), HALO), HALO), :]
                    parts.append(jnp.where(i > 0, pb, 0.0))
                parts.append(ref[pl.ds(r0, chunk), :])
                if after:
                    pa = ref[pl.ds(pl.multiple_of(jnp.minimum(r0 + chunk, T - HALO), HALO), HALO), :]
                    parts.append(jnp.where(i < nch - 1, pa, 0.0))
                wins.append(jnp.concatenate(parts, axis=0) if len(parts) > 1 else parts[0])
            o_vals, a_vals = fn(wins, vv, inside)
            p = 0
            for o, (nseg, _, _) in zip(o_refs, outs):
                for s in range(nseg):
                    o[s, pl.ds(r0, chunk), :] = o_vals[p][hb:hb + chunk].astype(o.dtype)
                    p += 1
            return tuple(c + a for c, a in zip(carry, a_vals))

        taps = [v.shape[0] for v, _ in vecs][:n_acc]
        init = tuple(jnp.zeros((1, tc), F32) for k in taps for _ in range(k))
        sums = lax.fori_loop(0, nch, step, init)
        arow = lax.broadcasted_iota(jnp.int32, (SUBLANES, tc), 0)
        p = 0
        for a, k in zip(a_refs, taps):
            acc = jnp.zeros((SUBLANES, tc), F32)
            for t in range(k):
                acc = jnp.where(arow == t, sums[p], acc)
                p += 1
            a[...] = acc

    in_specs = [pl.BlockSpec((T, tc), functools.partial(lambda j, off: (0, off + j), off=off)) for _, off in cols]
    in_specs += [pl.BlockSpec((v.shape[0], tc), functools.partial(lambda j, off: (0, off + j), off=off))
                 for v, off in vecs]
    out_specs = [pl.BlockSpec((nseg, T, tc), lambda j: (0, 0, j)) for nseg, _, _ in outs]
    out_specs += [pl.BlockSpec((SUBLANES, tc), lambda j: (0, j)) for _ in range(n_acc)]
    out_shape = [jax.ShapeDtypeStruct((nseg, T, w), dt) for nseg, w, dt in outs]
    out_shape += [jax.ShapeDtypeStruct((SUBLANES, ncol * tc), F32) for _ in range(n_acc)]
    return _pcall(
        body, name=name, grid=(ncol,), in_specs=in_specs, out_specs=out_specs, out_shape=out_shape,
        compiler_params=_params(("parallel",)),
    )(*[c[0] for c in cols], *[v[0] for v in vecs])


def _down(x, k):
    return x if k == 0 else pltpu.roll(x, k, 0)


def _up(x, k):
    return x if k == 0 else pltpu.roll(x, x.shape[0] - k, 0)


def _lags(x):
    return _down(x, 2), _down(x, 1), x


def _conv(lags, w):
    return w[0:1] * lags[0] + w[1:2] * lags[1] + w[2:3] * lags[2]


def _conv_t(d, w):
    return w[2:3] * d + w[1:2] * _up(d, 1) + w[0:1] * _up(d, 2)


def _tap_sums(d, lags, inside):
    dm = jnp.where(inside, d, 0.0)
    return [jnp.sum(dm * lag, axis=0, keepdims=True) for lag in lags]


def _glu_fwd_fn(wins, vv, inside):
    xg, xv = wins
    wg, wv = vv
    ug = _conv(_lags(xg), wg)
    uv = _conv(_lags(xv), wv)
    return [ug * _sigmoid(ug) * uv], []


def _glu_bwd_fn(wins, vv, inside):
    xg, xv, da = wins
    wg, wv = vv
    lg, lv = _lags(xg), _lags(xv)
    ug = _conv(lg, wg)
    uv = _conv(lv, wv)
    sg = _sigmoid(ug)
    dug = da * uv * (sg * (1.0 + ug * (1.0 - sg)))
    duv = da * (ug * sg)
    return [_conv_t(dug, wg), _conv_t(duv, wv)], _tap_sums(dug, lg, inside) + _tap_sums(duv, lv, inside)


def _sc_fwd_fn(wins, vv, inside):
    gb, gc, hh = wins
    w, = vv
    return [gb * _conv(_lags(gc * hh), w)], []


def _sc_bwd_fn(wins, vv, inside):
    gb, gc, hh, dy = wins
    w, = vv
    lz = _lags(gc * hh)
    dcv = dy * gb
    dz = _conv_t(dcv, w)
    return [dy * _conv(lz, w), dz * hh, dz * gc], _tap_sums(dcv, lz, inside)


def _gates(qr, fr, lb):
    sg = _sigmoid(fr)
    f = lb + (1.0 - lb) * sg
    sq = _sigmoid(qr)
    q = qr * sq * (HEAD ** -0.5)
    return q, 1.0 - f, jnp.log(f), f, sg, sq


def _boundary_rows(b, g, row):
    c = b.shape[0]
    if 2 * g >= SUBLANES:
        x = b.reshape(c // (2 * g), 2 * g, LANES)
        return jnp.broadcast_to(x[:, g - 1:g, :], x.shape).reshape(c, LANES)
    x = b.reshape(c // SUBLANES, SUBLANES, LANES)
    lo = jnp.broadcast_to(x[:, 1:2, :], x.shape).reshape(c, LANES)
    hi = jnp.broadcast_to(x[:, 5:6, :], x.shape).reshape(c, LANES)
    return jnp.where((row & 4) == 0, lo, hi)


def _chunk_decays(gl, f, row):
    c = gl.shape[0]
    b = gl
    d = 1
    while d < c:
        b = b + jnp.where(row >= d, pltpu.roll(b, d, 0), 0.0)
        d *= 2
    eq, ek = [], []
    g = c // 2
    while g >= 2:
        right = (row & g) != 0
        m = _boundary_rows(b, g, row)
        z = jnp.exp(jnp.where(right, b - m, m - b))
        eq.append(jnp.where(right, z, 0.0))
        ek.append(jnp.where(right, 0.0, z))
        g //= 2
    odd = (row & 1) != 0
    eq.append(jnp.where(odd, f, 0.0))
    ek.append(jnp.where(odd, 0.0, 1.0))
    return b, eq, ek


def _intra(q, k, eq, ek, tt, ss):
    c = q.shape[0]
    qs, ks = [], []
    a = jnp.where(tt == ss, jnp.sum(q * k, axis=1, keepdims=True), 0.0)
    g = c // 2
    for e_q, e_k in zip(eq, ek):
        qg = (q * e_q).astype(BF16)
        kg = (k * e_k).astype(BF16)
        p = lax.dot_general(qg, kg, (((1,), (1,)), ((), ())), preferred_element_type=F32)
        a = a + (p if 2 * g >= c else jnp.where((tt ^ ss) < 2 * g, p, 0.0))
        qs.append(qg)
        ks.append(kg)
        g //= 2
    return a, qs, ks


def _hgrn_fwd(proj, lb, d_model):
    T = proj.shape[0]
    H = d_model // HEAD
    nch = T // CHUNK

    def body(q_ref, f_ref, v_ref, lb_ref, o_ref, s_ref):
        lbv = lb_ref[...]
        row = lax.broadcasted_iota(jnp.int32, (CHUNK, HEAD), 0)
        tt = lax.broadcasted_iota(jnp.int32, (CHUNK, CHUNK), 0)
        ss = lax.broadcasted_iota(jnp.int32, (CHUNK, CHUNK), 1)

        def step(i, st):
            sl = pl.ds(pl.multiple_of(i * CHUNK, CHUNK), CHUNK)
            q, k, gl, f, _, _ = _gates(q_ref[sl, :], f_ref[sl, :], lbv)
            v = v_ref[sl, :].astype(BF16)
            b, eq, ek = _chunk_decays(gl, f, row)
            a, _, _ = _intra(q, k, eq, ek, tt, ss)
            bl = b[CHUNK - 1:CHUNK, :]
            q0 = (q * jnp.exp(b)).astype(BF16)
            kh = (k * jnp.exp(bl - b)).astype(BF16)
            s_ref[i] = st
            o = jnp.dot(a.astype(BF16), v, preferred_element_type=F32)
            o = o + lax.dot_general(q0, st.astype(BF16), (((1,), (1,)), ((), ())), preferred_element_type=F32)
            o_ref[sl, :] = o
            return jnp.exp(bl) * st + lax.dot_general(v, kh, (((0,), (0,)), ((), ())), preferred_element_type=F32)

        per = 4 if nch % 4 == 0 else 2

        def trip(i, st):
            for u in range(per):
                st = step(per * i + u, st)
            return st

        lax.fori_loop(0, nch // per, trip, jnp.zeros((HEAD, HEAD), F32))

    col = lambda off: pl.BlockSpec((T, HEAD), functools.partial(lambda h, off: (0, off + h), off=off))
    return _pcall(
        body, name="hgrn_fwd", grid=(H,),
        in_specs=[col(0), col(H), col(2 * H), pl.BlockSpec((1, HEAD), lambda h: (0, h))],
        out_specs=[pl.BlockSpec((T, HEAD), lambda h: (0, h)),
                   pl.BlockSpec((None, nch, HEAD, HEAD), lambda h: (h, 0, 0, 0))],
        out_shape=[jax.ShapeDtypeStruct((T, d_model), F32), jax.ShapeDtypeStruct((H, nch, HEAD, HEAD), F32)],
        compiler_params=_params(("parallel",)),
    )(proj, proj, proj, lb)


def _hgrn_bwd(proj, lb, states, do, dgate, d_model):
    T = proj.shape[0]
    H = d_model // HEAD
    nch = T // CHUNK

    def body(q_ref, f_ref, v_ref, lb_ref, s_ref, do_ref, dg_ref, dp_ref, dlb_ref):
        dq_ref, df_ref, dv_ref = dp_ref.at[0], dp_ref.at[1], dp_ref.at[2]
        dp_ref[3] = dg_ref[...]
        lbv = lb_ref[...]
        row = lax.broadcasted_iota(jnp.int32, (CHUNK, HEAD), 0)
        tt = lax.broadcasted_iota(jnp.int32, (CHUNK, CHUNK), 0)
        ss = lax.broadcasted_iota(jnp.int32, (CHUNK, CHUNK), 1)
        last = row == CHUNK - 1
        nt = (((1,), (1,)), ((), ()))
        tn = (((0,), (0,)), ((), ()))

        def step(j, carry):
            dst, dlb = carry
            i = nch - 1 - j
            sl = pl.ds(pl.multiple_of(i * CHUNK, CHUNK), CHUNK)
            qr = q_ref[sl, :]
            q, k, gl, f, sg, sq = _gates(qr, f_ref[sl, :], lbv)
            v = v_ref[sl, :].astype(BF16)
            d_o = do_ref[sl, :].astype(BF16)
            st = s_ref[i]
            st16 = st.astype(BF16)
            dst16 = dst.astype(BF16)
            b, eq, ek = _chunk_decays(gl, f, row)
            a, qs, ks = _intra(q, k, eq, ek, tt, ss)
            bl = b[CHUNK - 1:CHUNK, :]
            e0 = jnp.exp(b)
            eh = jnp.exp(bl - b)
            ebl = jnp.exp(bl)
            q0 = q * e0
            kh = k * eh
            q016 = q0.astype(BF16)
            kh16 = kh.astype(BF16)
            dv = lax.dot_general(a.astype(BF16), d_o, tn, preferred_element_type=F32)
            dv = dv + lax.dot_general(kh16, dst16, nt, preferred_element_type=F32)
            dv_ref[sl, :] = dv.astype(dv_ref.dtype)
            da = lax.dot_general(d_o, v, nt, preferred_element_type=F32)
            da = jnp.where(tt >= ss, da, 0.0)
            dd = jnp.sum(jnp.where(tt == ss, da, 0.0), axis=1, keepdims=True)
            dq0 = jnp.dot(d_o, st16, preferred_element_type=F32)
            dkh = jnp.dot(v, dst16, preferred_element_type=F32)
            dq = dq0 * e0 + dd * k
            dk = dkh * eh + dd * q
            db = dq0 * q016.astype(F32) - dkh * kh16.astype(F32)
            g = CHUNK // 2
            for e_q, e_k, qg, kg in zip(eq, ek, qs, ks):
                dag = (da if 2 * g >= CHUNK else jnp.where((tt ^ ss) < 2 * g, da, 0.0)).astype(BF16)
                dqg = jnp.dot(dag, kg, preferred_element_type=F32)
                dkg = lax.dot_general(dag, qg, tn, preferred_element_type=F32)
                dq = dq + dqg * e_q
                dk = dk + dkg * e_k
                db = db + (dqg * qg.astype(F32) - dkg * kg.astype(F32))
                g //= 2
            dbl = jnp.sum(dkh * kh16.astype(F32), axis=0, keepdims=True) + ebl * jnp.sum(dst * st, axis=0, keepdims=True)
            db = db + jnp.where(last, dbl, 0.0)
            d = 1
            while d < CHUNK:
                db = db + jnp.where(row < CHUNK - d, pltpu.roll(db, CHUNK - d, 0), 0.0)
                d *= 2
            dfg = db / f - dk
            df_ref[sl, :] = (dfg * (1.0 - lbv) * sg * (1.0 - sg)).astype(df_ref.dtype)
            dq_ref[sl, :] = (dq * (HEAD ** -0.5) * (sq * (1.0 + qr * (1.0 - sq)))).astype(dq_ref.dtype)
            dlb = dlb + jnp.sum(dfg * (1.0 - sg), axis=0, keepdims=True)
            dst = ebl * dst + lax.dot_general(d_o, q016, tn, preferred_element_type=F32)
            return dst, dlb

        _, dlb = lax.fori_loop(0, nch // 2, lambda j, cr: step(2 * j + 1, step(2 * j, cr)),
                               (jnp.zeros((HEAD, HEAD), F32), jnp.zeros((1, HEAD), F32)))
        arow = lax.broadcasted_iota(jnp.int32, (SUBLANES, HEAD), 0)
        dlb_ref[...] = jnp.where(arow == 0, dlb, 0.0)

    col = lambda off: pl.BlockSpec((T, HEAD), functools.partial(lambda h, off: (0, off + h), off=off))
    return _pcall(
        body, name="hgrn_bwd", grid=(H,),
        in_specs=[col(0), col(H), col(2 * H), pl.BlockSpec((1, HEAD), lambda h: (0, h)),
                  pl.BlockSpec((None, nch, HEAD, HEAD), lambda h: (h, 0, 0, 0)), col(0), col(0)],
        out_specs=[pl.BlockSpec((4, T, HEAD), lambda h: (0, 0, h)), pl.BlockSpec((SUBLANES, HEAD), lambda h: (0, h))],
        out_shape=[jax.ShapeDtypeStruct((4, T, d_model), BF16), jax.ShapeDtypeStruct((SUBLANES, d_model), F32)],
        compiler_params=_params(("parallel",)),
    )(proj, proj, proj, lb, states, do, dgate)


def _lb_softmax(table):
    n, f = table.shape

    def body(t_ref, p_ref):
        t = t_ref[...]
        e = jnp.exp(t - jnp.max(t, axis=0, keepdims=True))
        p_ref[...] = e / jnp.sum(e, axis=0, keepdims=True)

    padded = jnp.pad(table, ((0, SUBLANES - n), (0, 0)), constant_values=-jnp.inf)
    return _pcall(body, name="lb_softmax", out_shape=jax.ShapeDtypeStruct((SUBLANES, f), F32))(padded)


def _to_bf16(w, layer, nlayers):
    R = w.shape[0] // nlayers
    C = w.shape[1]
    tr = _pick(R, (256, 128, 64, 32, 16))
    nb = R // tr

    def body(w_ref, o_ref):
        o_ref[...] = w_ref[...].astype(o_ref.dtype)

    return _pcall(
        body, name="to_bf16", grid=(nb,), in_specs=[pl.BlockSpec((tr, C), lambda i: (layer * nb + i, 0))],
        out_specs=pl.BlockSpec((tr, C), lambda i: (i, 0)), out_shape=jax.ShapeDtypeStruct((R, C), BF16),
        compiler_params=_params(("parallel",)),
    )(w)


def _adamw_math(w, g, m, v):
    m = ADAM_B1 * m + (1.0 - ADAM_B1) * g
    v = ADAM_B2 * v + (1.0 - ADAM_B2) * (g * g)
    m_hat = m / (1.0 - ADAM_B1 ** ADAM_STEP)
    v_hat = v / (1.0 - ADAM_B2 ** ADAM_STEP)
    delta = -ADAM_LR * (m_hat / (jnp.sqrt(v_hat) + ADAM_EPS) + ADAM_WD * w)
    return delta, m, v


def _adamw(w, g, m, v, name):
    R, C = w.shape
    tr = _pick(R, (128, 64, 32, 16, 8))

    def body(w_ref, g_ref, m_ref, v_ref, d_ref, nm_ref, nv_ref):
        d, nm, nv = _adamw_math(w_ref[...], g_ref[...], m_ref[...], v_ref[...])
        d_ref[...] = d
        nm_ref[...] = nm
        nv_ref[...] = nv

    spec = pl.BlockSpec((tr, C), lambda i: (i, 0))
    return _pcall(
        body, name=name, grid=(R // tr,), in_specs=[spec] * 4, out_specs=[spec] * 3,
        out_shape=[jax.ShapeDtypeStruct((R, C), F32)] * 3, compiler_params=_params(("parallel",)),
    )(w, g, m, v)


def _adamw_halves(w, m, v, g_mine, g_recv, c, name, layer=0, prev=None):
    C = w.shape[1]
    rh = g_mine.shape[0]
    tr = _pick(rh, (128, 64, 32, 16, 8))
    nb = rh // tr
    r0 = layer * 2 * nb

    def body(c_ref, w_ref, m_ref, v_ref, gm_ref, gr_ref, *rest):
        g_ref, d_ref, nm_ref, nv_ref = rest[-4:]
        g = jnp.where(pl.program_id(0) == c_ref[0], gm_ref[...], gr_ref[...])
        d, nm, nv = _adamw_math(w_ref[...], g, m_ref[...], v_ref[...])
        g_ref[...] = g
        d_ref[...] = d
        nm_ref[...] = nm
        nv_ref[...] = nv

    full = pl.BlockSpec((tr, C), lambda h, i, cr: (r0 + h * nb + i, 0))
    mine = pl.BlockSpec((tr, C), lambda h, i, cr: (jnp.where(h == cr[0], i, 0), 0))
    recv = pl.BlockSpec((tr, C), lambda h, i, cr: (jnp.where(h == cr[0], 0, i), 0))
    in_specs = [full, full, full, mine, recv]
    args = [c, w, m, v, g_mine, g_recv]
    alias = {}
    if prev is not None:
        in_specs += [pl.BlockSpec(memory_space=pl.ANY)] * 4
        args += list(prev)
        alias = {6 + k: k for k in range(4)}
    return _pcall(
        body, name=name,
        grid_spec=pltpu.PrefetchScalarGridSpec(
            num_scalar_prefetch=1, grid=(2, nb), in_specs=in_specs, out_specs=[full] * 4),
        out_shape=[jax.ShapeDtypeStruct(w.shape, F32)] * 4, input_output_aliases=alias,
        compiler_params=_params(("parallel", "parallel")),
    )(*args)


def _lb_table_grad(p8, dlb, n):
    f = p8.shape[1]

    def body(p_ref, d_ref, o_ref):
        p = p_ref[...]
        d = d_ref[...]
        p0 = p[0:1, :]
        first = lax.broadcasted_iota(jnp.int32, p.shape, 0) == 0
        o_ref[...] = p * (jnp.where(first, d, 0.0) - d * p0)

    return _pcall(body, name="lb_table_grad", out_shape=jax.ShapeDtypeStruct((SUBLANES, f), F32))(p8, dlb)[:n]


def _place():
    x, y, c = lax.axis_index("x"), lax.axis_index("y"), lax.axis_index("c")
    chips = [(1 - x, y), (x, 1 - y), (1 - x, 1 - y)]
    return x, y, c, chips


HBM_SPEC = pl.BlockSpec(memory_space=pltpu.HBM)


def _gather_taps(shards):
    n = len(shards)

    def body(*refs):
        ins, outs = refs[:n], refs[n:2 * n]
        send_sems, recv_sems = refs[2 * n:]
        x, y, c, chips = _place()
        me = 2 * x + y
        peers = [(*chip, c) for chip in chips] + [(x, y, 1 - c)]
        sends = [pltpu.make_async_remote_copy(
            src_ref=ins[t], dst_ref=outs[t].at[me], send_sem=send_sems.at[4 * t + j], recv_sem=recv_sems.at[4 * t + j],
            device_id=peer, device_id_type=MESH) for t in range(n) for j, peer in enumerate(peers)]
        for cp in sends:
            cp.start()
        for t in range(n):
            for j, (px, py, _) in enumerate(peers):
                landed = outs[t].at[2 * px + py]
                pltpu.make_async_remote_copy(
                    src_ref=landed, dst_ref=landed, send_sem=send_sems.at[4 * t + j], recv_sem=recv_sems.at[4 * t + j],
                    device_id=(x, y, c), device_id_type=MESH).wait_recv()
        for cp in sends:
            cp.wait_send()

    return _pcall(
        body, name="gather_taps", in_specs=[HBM_SPEC] * n, out_specs=[HBM_SPEC] * n,
        out_shape=[jax.ShapeDtypeStruct((N_CHIPS,) + a.shape, a.dtype) for a in shards],
        scratch_shapes=[pltpu.SemaphoreType.DMA((4 * n,)), pltpu.SemaphoreType.DMA((4 * n,))],
    )(*shards)


SEM_SPEC = pl.BlockSpec(memory_space=pltpu.SEMAPHORE)
DATAFLOW = pltpu.SideEffectType.DATAFLOW_SIDE_EFFECTING
COPIES_PER_SHARD = 4


def _shard_copies(ins, lands, send_sems, recv_sems, base=0):
    x, y, c, chips = _place()
    me = 2 * x + y
    cps = []
    for t in range(len(ins)):
        rh = ins[t].shape[0] // 2
        half = pl.ds(pl.multiple_of(c * rh, rh), rh)
        for j, chip in enumerate(chips):
            k = COPIES_PER_SHARD * (base + t) + j
            cps.append(pltpu.make_async_remote_copy(
                src_ref=ins[t].at[half], dst_ref=lands[t].at[me, half], send_sem=send_sems.at[k],
                recv_sem=recv_sems.at[k], device_id=(*chip, c), device_id_type=MESH))
        k = COPIES_PER_SHARD * (base + t) + 3
        cps.append(pltpu.make_async_remote_copy(
            src_ref=ins[t], dst_ref=lands[t].at[me], send_sem=send_sems.at[k], recv_sem=recv_sems.at[k],
            device_id=(x, y, 1 - c), device_id_type=MESH))
    return cps


def _gather_start(shards, thru, name):
    n = len(shards)
    nops = 2 * n + len(thru)

    def body(*refs):
        ins, lands = refs[:n], refs[n:2 * n]
        send_sems, recv_sems = refs[nops], refs[nops + 1]
        for cp in _shard_copies(ins, lands, send_sems, recv_sems):
            cp.start()

    lands = [pltpu.with_memory_space_constraint(lax.empty((N_CHIPS,) + s.shape, s.dtype), pltpu.HBM) for s in shards]
    ops = [pltpu.with_memory_space_constraint(s, pltpu.HBM) for s in shards] + lands + list(thru)
    nsem = COPIES_PER_SHARD * n
    res = _pcall(
        body, name=name, in_specs=[HBM_SPEC] * nops,
        out_specs=[SEM_SPEC, SEM_SPEC] + [HBM_SPEC] * nops,
        out_shape=[pltpu.SemaphoreType.DMA((nsem,)), pltpu.SemaphoreType.DMA((nsem,))]
        + [pltpu.HBM(o.shape, o.dtype) for o in ops],
        input_output_aliases={i: 2 + i for i in range(nops)},
        compiler_params=pltpu.CompilerParams(has_side_effects=DATAFLOW),
    )(*ops)
    return res[0], res[1], res[2:2 + n], res[2 + n:2 + 2 * n], list(res[2 + 2 * n:])


def _gather_wait(send_sems, recv_sems, shards, lands, after, name, base=0, which=None):
    n = len(shards)

    def body(*refs):
        ins, lnd = refs[:n], refs[n:2 * n]
        ssem, rsem = refs[2 * n], refs[2 * n + 1]
        for k, cp in enumerate(_shard_copies(ins, lnd, ssem, rsem, base)):
            if which is None or k % COPIES_PER_SHARD in which:
                cp.wait_send()
                cp.wait_recv()

    res = _pcall(
        body, name=name,
        in_specs=[HBM_SPEC] * (2 * n) + [SEM_SPEC, SEM_SPEC, pl.BlockSpec(memory_space=pl.ANY)],
        out_specs=[HBM_SPEC] * (2 * n),
        out_shape=[pltpu.HBM(o.shape, o.dtype) for o in list(shards) + list(lands)],
        input_output_aliases={i: i for i in range(2 * n)},
        compiler_params=pltpu.CompilerParams(has_side_effects=DATAFLOW),
    )(*shards, *lands, send_sems, recv_sems, after)
    return res[:n], res[n:]


SIBLING_PAIR = 1


def _sibling_handshake():
    x, y, c, _ = _place()
    barrier = pltpu.get_barrier_semaphore()
    pl.semaphore_signal(barrier, inc=1, device_id=(x, y, 1 - c), device_id_type=MESH)
    pl.semaphore_wait(barrier, 1)


def _forward_copies(land, send_sems, recv_sems, which=(0, 1, 2)):
    x, y, c, chips = _place()
    rh = land.shape[1] // 2
    return [pltpu.make_async_remote_copy(
        src_ref=land.at[2 * cx + cy, pl.ds(pl.multiple_of(c * rh, rh), rh)],
        dst_ref=land.at[2 * cx + cy, pl.ds(pl.multiple_of(c * rh, rh), rh)],
        send_sem=send_sems.at[j], recv_sem=recv_sems.at[j], device_id=(x, y, 1 - c), device_id_type=MESH)
        for j, (cx, cy) in enumerate(chips) if j in which]


def _forward_start(land, thru, name, which=(0, 1, 2)):
    def body(land_ref, thru_ref, send_sems, recv_sems, out_ref, thru_out):
        _sibling_handshake()
        for cp in _forward_copies(land_ref, send_sems, recv_sems, which):
            cp.start()

    return _pcall(
        body, name=name, in_specs=[HBM_SPEC, HBM_SPEC], out_specs=[SEM_SPEC, SEM_SPEC, HBM_SPEC, HBM_SPEC],
        out_shape=[pltpu.SemaphoreType.DMA((3,)), pltpu.SemaphoreType.DMA((3,)), pltpu.HBM(land.shape, land.dtype),
                   pltpu.HBM(thru.shape, thru.dtype)],
        input_output_aliases={0: 2, 1: 3},
        compiler_params=pltpu.CompilerParams(has_side_effects=DATAFLOW, collective_id=SIBLING_PAIR),
    )(land, thru)


def _forward_wait(send_sems, recv_sems, land, after, name, which=(0, 1, 2)):
    def body(land_ref, ssem, rsem, after_ref, out_ref):
        for cp in _forward_copies(land_ref, ssem, rsem, which):
            cp.wait_send()
            cp.wait_recv()

    return _pcall(
        body, name=name, in_specs=[HBM_SPEC, SEM_SPEC, SEM_SPEC, pl.BlockSpec(memory_space=pl.ANY)],
        out_specs=HBM_SPEC, out_shape=pltpu.HBM(land.shape, land.dtype), input_output_aliases={0: 0},
        compiler_params=pltpu.CompilerParams(has_side_effects=DATAFLOW),
    )(land, send_sems, recv_sems, after)


def _sibling_copies(ins, lands, send_sems, recv_sems, other_half):
    x, y, c, _ = _place()
    return [pltpu.make_async_remote_copy(
        src_ref=ins[t].at[:, 1 - c] if other_half else ins[t], dst_ref=lands[t], send_sem=send_sems.at[t],
        recv_sem=recv_sems.at[t], device_id=(x, y, 1 - c), device_id_type=MESH) for t in range(len(ins))]


def _sibling_start(srcs, other_half, thru, name):
    n = len(srcs)
    nthru = 0 if thru is None else 1

    def body(*refs):
        ins, lands = refs[:n], refs[n:2 * n]
        send_sems, recv_sems = refs[2 * n + nthru], refs[2 * n + nthru + 1]
        _sibling_handshake()
        for cp in _sibling_copies(ins, lands, send_sems, recv_sems, other_half):
            cp.start()

    shapes = [(s.shape[0],) + s.shape[2:] if other_half else s.shape for s in srcs]
    lands = [pltpu.with_memory_space_constraint(lax.empty(sh, s.dtype), pltpu.HBM) for sh, s in zip(shapes, srcs)]
    ops = [pltpu.with_memory_space_constraint(s, pltpu.HBM) for s in srcs] + lands + ([] if thru is None else [thru])
    res = _pcall(
        body, name=name, in_specs=[HBM_SPEC] * len(ops),
        out_specs=[SEM_SPEC, SEM_SPEC] + [HBM_SPEC] * len(ops),
        out_shape=[pltpu.SemaphoreType.DMA((n,)), pltpu.SemaphoreType.DMA((n,))]
        + [pltpu.HBM(o.shape, o.dtype) for o in ops],
        input_output_aliases={i: 2 + i for i in range(len(ops))},
        compiler_params=pltpu.CompilerParams(has_side_effects=DATAFLOW, collective_id=SIBLING_PAIR),
    )(*ops)
    return res[0], res[1], res[2:2 + n], res[2 + n:2 + 2 * n], (None if thru is None else res[2 + 2 * n])


def _sibling_wait(send_sems, recv_sems, srcs, lands, other_half, after, name):
    n = len(srcs)

    def body(*refs):
        ins, lnd = refs[:n], refs[n:2 * n]
        ssem, rsem = refs[2 * n], refs[2 * n + 1]
        for cp in _sibling_copies(ins, lnd, ssem, rsem, other_half):
            cp.wait_send()
            cp.wait_recv()

    res = _pcall(
        body, name=name,
        in_specs=[HBM_SPEC] * (2 * n) + [SEM_SPEC, SEM_SPEC, pl.BlockSpec(memory_space=pl.ANY)],
        out_specs=[HBM_SPEC] * (2 * n),
        out_shape=[pltpu.HBM(o.shape, o.dtype) for o in list(srcs) + list(lands)],
        input_output_aliases={i: i for i in range(2 * n)},
        compiler_params=pltpu.CompilerParams(has_side_effects=DATAFLOW),
    )(*srcs, *lands, send_sems, recv_sems, after)
    return res[:n], res[n:]


def _chip_copies(ins, lands, send_sems, recv_sems):
    x, y, c, chips = _place()
    cps = []
    for t in range(len(ins)):
        for j, (cx, cy) in enumerate(chips):
            cps.append(pltpu.make_async_remote_copy(
                src_ref=ins[t].at[2 * cx + cy], dst_ref=lands[t].at[j],
                send_sem=send_sems.at[3 * t + j], recv_sem=recv_sems.at[3 * t + j],
                device_id=(cx, cy, c), device_id_type=MESH))
    return cps


def _chip_start(parts, thru, name):
    n = len(parts)

    def body(*refs):
        ins, lands = refs[:n], refs[n:2 * n]
        send_sems, recv_sems = refs[2 * n + 1], refs[2 * n + 2]
        for cp in _chip_copies(ins, lands, send_sems, recv_sems):
            cp.start()

    lands = [pltpu.with_memory_space_constraint(lax.empty((3,) + p.shape[1:], p.dtype), pltpu.HBM) for p in parts]
    ops = [pltpu.with_memory_space_constraint(p, pltpu.HBM) for p in parts] + lands + [thru]
    res = _pcall(
        body, name=name, in_specs=[HBM_SPEC] * (2 * n + 1),
        out_specs=[SEM_SPEC, SEM_SPEC] + [HBM_SPEC] * (2 * n + 1),
        out_shape=[pltpu.SemaphoreType.DMA((3 * n,)), pltpu.SemaphoreType.DMA((3 * n,))]
        + [pltpu.HBM(o.shape, o.dtype) for o in ops],
        input_output_aliases={i: 2 + i for i in range(2 * n + 1)},
        compiler_params=pltpu.CompilerParams(has_side_effects=DATAFLOW),
    )(*ops)
    return res[0], res[1], res[2:2 + n], res[2 + n:2 + 2 * n], res[2 + 2 * n]


def _chip_wait(send_sems, recv_sems, parts, lands, after, name):
    n = len(parts)

    def body(*refs):
        ins, lnd = refs[:n], refs[n:2 * n]
        ssem, rsem = refs[2 * n], refs[2 * n + 1]
        for cp in _chip_copies(ins, lnd, ssem, rsem):
            cp.wait_send()
            cp.wait_recv()

    res = _pcall(
        body, name=name,
        in_specs=[HBM_SPEC] * (2 * n) + [SEM_SPEC, SEM_SPEC, pl.BlockSpec(memory_space=pl.ANY)],
        out_specs=[HBM_SPEC] * (2 * n),
        out_shape=[pltpu.HBM(o.shape, o.dtype) for o in list(parts) + list(lands)],
        input_output_aliases={i: i for i in range(2 * n)},
        compiler_params=pltpu.CompilerParams(has_side_effects=DATAFLOW),
    )(*parts, *lands, send_sems, recv_sems, after)
    return res[:n], res[n:]


def _add_pair(grad, recv, c, name):
    s, _, rh, cc = grad.shape
    tr = _pick(rh, (256, 128, 64, 32, 16))

    def body(c_ref, g_ref, r_ref, o_ref):
        o_ref[...] = (g_ref[...].astype(F32) + r_ref[...].astype(F32)).astype(o_ref.dtype)

    return _pcall(
        body, name=name,
        grid_spec=pltpu.PrefetchScalarGridSpec(
            num_scalar_prefetch=1, grid=(s, rh // tr),
            in_specs=[pl.BlockSpec((None, None, tr, cc), lambda a, i, cr: (a, cr[0], i, 0)),
                      pl.BlockSpec((None, tr, cc), lambda a, i, cr: (a, i, 0))],
            out_specs=pl.BlockSpec((None, tr, cc), lambda a, i, cr: (a, i, 0))),
        out_shape=jax.ShapeDtypeStruct((s, rh, cc), BF16),
        compiler_params=_params(("parallel", "parallel")),
    )(c, grad, recv)


def _add_chips(part, recv, me, name):
    _, rh, cc = part.shape
    tr = _pick(rh, (256, 128, 64, 32, 16))

    def body(m_ref, p_ref, r_ref, o_ref):
        o_ref[...] = ((p_ref[...].astype(F32) + r_ref[0].astype(F32)) + r_ref[1].astype(F32)) + r_ref[2].astype(F32)

    return _pcall(
        body, name=name,
        grid_spec=pltpu.PrefetchScalarGridSpec(
            num_scalar_prefetch=1, grid=(rh // tr,),
            in_specs=[pl.BlockSpec((None, tr, cc), lambda i, mr: (mr[0], i, 0)),
                      pl.BlockSpec((3, tr, cc), lambda i, mr: (0, i, 0))],
            out_specs=pl.BlockSpec((tr, cc), lambda i, mr: (i, 0))),
        out_shape=jax.ShapeDtypeStruct((rh, cc), F32),
        compiler_params=_params(("parallel",)),
    )(me, part, recv)


def _all_sum(vec):
    rows = vec.shape[0]

    def body(v_ref, o_ref, buf, send_sems, recv_sems):
        x, y, c, _ = _place()
        me = 4 * x + 2 * y + c
        buf[me] = v_ref[...]
        cps = []
        for r in range(1, 8):
            fx, fy, fc = (r >> 2) & 1, (r >> 1) & 1, r & 1
            peer = (x ^ fx, y ^ fy, c ^ fc)
            cps.append(pltpu.make_async_remote_copy(
                src_ref=v_ref, dst_ref=buf.at[me], send_sem=send_sems.at[r - 1], recv_sem=recv_sems.at[r - 1],
                device_id=peer, device_id_type=MESH))
        for cp in cps:
            cp.start()
        for r in range(1, 8):
            src = me ^ r
            pltpu.make_async_remote_copy(
                src_ref=v_ref, dst_ref=buf.at[src], send_sem=send_sems.at[r - 1], recv_sem=recv_sems.at[r - 1],
                device_id=(x, y, c), device_id_type=MESH).wait_recv()
        for cp in cps:
            cp.wait_send()
        acc = buf[0]
        for d in range(1, 8):
            acc = acc + buf[d]
        o_ref[...] = acc

    return _pcall(
        body, name="all_sum_small",
        in_specs=[pl.BlockSpec(memory_space=pltpu.VMEM)], out_specs=pl.BlockSpec(memory_space=pltpu.VMEM),
        out_shape=jax.ShapeDtypeStruct((rows, LANES), F32),
        scratch_shapes=[pltpu.VMEM((8, rows, LANES), F32), pltpu.SemaphoreType.DMA((7,)), pltpu.SemaphoreType.DMA((7,))],
    )(vec)


def _pack(parts):
    flat = jnp.concatenate([p.reshape(-1) for p in parts])
    tile = SUBLANES * LANES
    pad = (-flat.shape[0]) % tile
    return jnp.pad(flat, (0, pad)).reshape(-1, LANES)


def _unpack(vec, shapes):
    flat = vec.reshape(-1)
    out, p = [], 0
    for s in shapes:
        n = 1
        for d in s:
            n *= d
        out.append(flat[p:p + n].reshape(s))
        p += n
    return out


def _local_step(x, tgt, norm_mix, norm_ffn, lb8, out_norm, final_norm, sc_conv, ffn_conv, first, arrive, reduce_start,
                reduce_finish):
    T, D = x.shape
    F2 = ffn_conv.shape[-1]
    FF = F2 // 2
    tm = _pick(T, (1024, 512, 256, 128))
    wide = (1536, 1408, 1024, 768, 512, 384, 256, 128)
    cw_h, cw_s, cw_u = 4 * D // N_CHIPS, 3 * D // N_CHIPS, F2 // N_CHIPS
    kp = FF // N_CHIPS
    tk_ff = kp if kp % LANES == 0 else LANES
    tn_d = _pick(D, (1024, 512, 256, 128))
    tk_w = _pick(D, (512, 256, 128))
    tn_h = _pick(cw_h, (1024, 512, 256, 128))
    tn_s = _pick(D // N_CHIPS, (512, 256, 128))
    tn_u = _pick(cw_u, wide)
    lb = lb8[0:1]
    wm_sq = _wmap_col(D, tn_d, 0)
    wm_sq1 = _wmap_col(D, D, 0)
    seg1 = lambda a: a.reshape((1,) + a.shape)

    def mix_in(h, w):
        return _row_call(_rms_fwd_fn, [(h, 0, D)], [w], [(D, BF16)], 0, "rms_fwd")[0]

    def rms_bwd(h, dxn, dh, w):
        return _row_call(_rms_bwd_fn, [(h, 0, D), (dxn, 0, D), (dh, 0, D)], [w], [(D, F32), (D, BF16)], 1, "rms_bwd")

    def ffn_fwd(h, i, fetch_up, behind_down=()):
        xn = mix_in(h, norm_ffn[i:i + 1])
        tn = _pick(cw_u, wide)
        fetch_down, xn = arrive("ffn_w_down%d" % i, xn)
        w_up = fetch_up(xn)
        up = _mm_nn(xn, w_up, _wmap_col(cw_u, tn, 0), D, F2, tm, D, tn, "ffn_up")
        nb = FF // LANES
        a = _col_call(_glu_fwd_fn, [(up, 0), (up, nb)], [(ffn_conv[i], 0), (ffn_conv[i], nb)], [(1, FF, BF16)], 0,
                      "glu_fwd", before=True, after=False)[0][0]
        later = []
        for name in behind_down:
            fetch, a = arrive(name, a)
            later.append(fetch)
        w_down = fetch_down(a)
        h2 = _mm_nn(a, w_down, _wmap_row(kp, tk_ff, 0), FF, D, tm, tk_ff, tn_d, "ffn_down", res=h, per_step=2)
        return h2, (xn, up, a), w_up, w_down, later

    def ffn_bwd(dh, dh16, h, saved, i, w_up, w_down):
        xn, up, a = saved
        g_down = _mm_tn(a, seg1(dh16), (N_CHIPS, 1, kp, D), _wmap_row(kp, tk_ff, 0), FF, D, tk_ff, tn_d,
                        "ffn_down_dw", tm=_pick(T, (2048, 1024, 512, 256, 128)))
        dh16 = reduce_start(("ffn_w_down", i), g_down, dh16)
        da = _mm_nt(seg1(dh16), w_down, _wmap_row(kp, tk_ff, 0), FF, D, tm, tk_ff, D, "ffn_down_dx")
        nb = FF // LANES
        dgv, cg, cv = _col_call(_glu_bwd_fn, [(up, 0), (up, nb), (da, 0)], [(ffn_conv[i], 0), (ffn_conv[i], nb)],
                                [(2, FF, BF16)], 2, "glu_bwd", before=True, after=True)
        g_up = _mm_tn(xn, dgv, (N_CHIPS, 1, D, cw_u), _wmap_col(cw_u, tn_u, 0), D, F2, tk_w, tn_u, "ffn_up_dw")
        dgv = reduce_start(("ffn_w_up", i), g_up, dgv)
        dxn = _mm_nt(dgv, w_up, _wmap_col(cw_u, tn_u, 0), D, F2, _pick(T, (512, 256, 128)), D, tn_u, "ffn_up_dx",
                     per_step=2)
        dh2, dh2_16, dnw = rms_bwd(h, dxn, dh, norm_ffn[i:i + 1])
        return dh2, reduce_finish(dh2_16), dnw, jnp.concatenate([cg[:3], cv[:3]], axis=1)

    h0 = x
    xn0 = mix_in(h0, norm_mix[0:1])
    proj, w_hin = first(xn0, lambda w, s, prev: _mm_nn_shard(xn0, w, s, tm, tn_h, "hgrn_in", prev))
    o, states = _hgrn_fwd(proj, lb, D)
    fetch_hout, o = arrive("hgrn_w_out", o)
    on = _row_call(_onorm_fwd_fn, [(o, 0, D), (proj, 3, D)], [out_norm], [(D, BF16)], 0, "onorm_fwd")[0]
    fetch_up0, on = arrive("ffn_w_up0", on)
    w_hout1 = fetch_hout(on).reshape(1, D, D)
    h1 = _mm_nn(on, w_hout1, wm_sq, D, D, tm, D, tn_d, "hgrn_out", res=h0)
    h2, ffn0, w_up0, w_down0, (fetch_sin, fetch_sout) = ffn_fwd(h1, 0, fetch_up0, ("sc_w_in", "sc_w_out"))
    xn1 = mix_in(h2, norm_mix[1:2])
    w_sin = fetch_sin(xn1)
    tn_si = _pick(cw_s, wide)
    sproj = _mm_nn(xn1, w_sin, _wmap_col(cw_s, tn_si, 0), D, 3 * D, tm, D, tn_si, "sc_in")
    fetch_up1, sproj = arrive("ffn_w_up1", sproj)
    nd = D // LANES
    ysc = _col_call(_sc_fwd_fn, [(sproj, 0), (sproj, nd), (sproj, 2 * nd)], [(sc_conv, 0)], [(1, D, BF16)], 0,
                    "sc_fwd", before=True, after=False)[0][0]
    w_sout1 = fetch_sout(ysc).reshape(1, D, D)
    h3 = _mm_nn(ysc, w_sout1, wm_sq, D, D, tm, D, tn_d, "sc_out", res=h2)
    h4, ffn1, w_up1, w_down1, _ = ffn_fwd(h3, 1, fetch_up1)

    dh, dh16, esq, dfinal = _row_call(_final_fn, [(h4, 0, D), (tgt, 0, D)], [final_norm], [(D, F32), (D, BF16)], 2,
                                      "final_loss")
    loss = 0.5 / D * jnp.sum(esq)
    dh, dh16, dnf1, dconv1 = ffn_bwd(dh, dh16, h3, ffn1, 1, w_up1, w_down1)
    g_sout = _mm_tn(ysc, seg1(dh16), (1, 1, D, D), wm_sq, D, D, tk_w, tn_d, "sc_out_dw")
    dh16 = reduce_start(("sc_w_out", 0), g_sout, dh16)
    dy = _mm_nt(seg1(dh16), w_sout1, wm_sq1, D, D, tm, D, D, "sc_out_dx")
    dsp, dscc = _col_call(_sc_bwd_fn, [(sproj, 0), (sproj, nd), (sproj, 2 * nd), (dy, 0)], [(sc_conv, 0)],
                          [(3, D, BF16)], 1, "sc_bwd", before=True, after=True)
    g_sin = _mm_tn(xn1, dsp, (N_CHIPS, 1, D, cw_s), _wmap_col(cw_s, tn_s, 0), D, 3 * D, tk_w, tn_s, "sc_in_dw")
    dsp = reduce_start(("sc_w_in", 0), g_sin, dsp)
    dxn = _mm_nt(dsp, w_sin, _wmap_col(cw_s, tn_s, 0), D, 3 * D, tm, D, tn_s, "sc_in_dx", per_step=3)
    dh, dh16, dnm1 = rms_bwd(h2, dxn, dh, norm_mix[1:2])
    dh16 = reduce_finish(dh16)
    dh, dh16, dnf0, dconv0 = ffn_bwd(dh, dh16, h1, ffn0, 0, w_up0, w_down0)
    g_hout = _mm_tn(on, seg1(dh16), (1, 1, D, D), wm_sq, D, D, tk_w, tn_d, "hgrn_out_dw")
    dh16 = reduce_start(("hgrn_w_out", 0), g_hout, dh16)
    don = _mm_nt(seg1(dh16), w_hout1, wm_sq1, D, D, tm, D, D, "hgrn_out_dx")
    do, dgate, dgain = _row_call(_onorm_bwd_fn, [(o, 0, D), (proj, 3, D), (don, 0, D)], [out_norm],
                                 [(D, F32), (D, BF16)], 1, "onorm_bwd")
    dproj, dlb = _hgrn_bwd(proj, lb, states, do, dgate, D)
    g_hin = _mm_tn(xn0, dproj, (N_CHIPS, 1, D, cw_h), _wmap_col(cw_h, tn_h, 0), D, 4 * D, tk_w, tn_h, "hgrn_in_dw")
    dproj = reduce_start(("hgrn_w_in", 0), g_hin, dproj)
    dxn = _mm_nt(dproj, w_hin, _wmap_col(cw_h, tn_h, 0), D, 4 * D, tm, D, tn_h, "hgrn_in_dx", per_step=2)
    grad_x, _, dnm0 = rms_bwd(h0, dxn, dh, norm_mix[0:1])

    small = dict(
        loss=loss,
        norm_mix=jnp.stack([jnp.sum(dnm0, axis=0), jnp.sum(dnm1, axis=0)]),
        norm_ffn=jnp.stack([jnp.sum(dnf0, axis=0), jnp.sum(dnf1, axis=0)]),
        lb=dlb[0:1],
        out_norm=jnp.sum(dgain, axis=0)[None],
        final_norm=jnp.sum(dfinal, axis=0),
        sc_conv=dscc[:3],
        ffn_conv=jnp.stack([dconv0, dconv1]),
    )
    return grad_x, small


def kernel(x, norm_mix, norm_ffn, hgrn_w_in, hgrn_lb_table, hgrn_out_norm, hgrn_w_out, sc_w_in, sc_conv, sc_w_out, ffn_w_up, ffn_conv, ffn_w_down, final_norm, loss_target, m_norm_mix, m_norm_ffn, m_hgrn_w_in, m_hgrn_lb_table, m_hgrn_out_norm, m_hgrn_w_out, m_sc_w_in, m_sc_conv, m_sc_w_out, m_ffn_w_up, m_ffn_conv, m_ffn_w_down, m_final_norm, v_norm_mix, v_norm_ffn, v_hgrn_w_in, v_hgrn_lb_table, v_hgrn_out_norm, v_hgrn_w_out, v_sc_w_in, v_sc_conv, v_sc_w_out, v_ffn_w_up, v_ffn_conv, v_ffn_w_down, v_final_norm):
    D = x.shape[-1]
    xi, yi, ci = lax.axis_index("x"), lax.axis_index("y"), lax.axis_index("c")
    me_chip = (2 * xi + yi).astype(jnp.int32).reshape(1)
    me_core = ci.astype(jnp.int32).reshape(1)

    big_names = ["hgrn_w_in", "hgrn_w_out", "sc_w_in", "sc_w_out", "ffn_w_up", "ffn_w_down"]
    big_w = dict(hgrn_w_in=hgrn_w_in, hgrn_w_out=hgrn_w_out, sc_w_in=sc_w_in, sc_w_out=sc_w_out,
                 ffn_w_up=ffn_w_up, ffn_w_down=ffn_w_down)
    big_m = dict(hgrn_w_in=m_hgrn_w_in, hgrn_w_out=m_hgrn_w_out, sc_w_in=m_sc_w_in, sc_w_out=m_sc_w_out,
                 ffn_w_up=m_ffn_w_up, ffn_w_down=m_ffn_w_down)
    big_v = dict(hgrn_w_in=v_hgrn_w_in, hgrn_w_out=v_hgrn_w_out, sc_w_in=v_sc_w_in, sc_w_out=v_sc_w_out,
                 ffn_w_up=v_ffn_w_up, ffn_w_down=v_ffn_w_down)
    flat2 = lambda a: a.reshape(-1, a.shape[-1])

    sh = lambda a, layer=0: _to_bf16(flat2(a), layer, a.shape[0])
    in_order_of_use = [("hgrn_w_in", sh(hgrn_w_in)), ("hgrn_w_out", sh(hgrn_w_out)), ("ffn_w_up0", sh(ffn_w_up, 0)),
                       ("ffn_w_down0", sh(ffn_w_down, 0)), ("sc_w_in", sh(sc_w_in)), ("sc_w_out", sh(sc_w_out)),
                       ("ffn_w_up1", sh(ffn_w_up, 1)), ("ffn_w_down1", sh(ffn_w_down, 1))]
    scc4, fcc4 = _gather_taps([flat2(sc_conv), flat2(ffn_conv)])
    names = [n for n, _ in in_order_of_use]
    ss, rs, src, land, (scc4, norm_mix) = _gather_start([s for _, s in in_order_of_use], [scc4, norm_mix], "gather_start")
    travelling = {n: (s, l) for n, s, l in zip(names, src, land)}

    def landed(name, after, call, which=None):
        (s,), (l,) = _gather_wait(ss, rs, [travelling[name][0]], [travelling[name][1]], after, call,
                                  base=names.index(name), which=which)
        travelling[name] = (s, l)
        return l

    scc = jnp.moveaxis(scc4, 0, 1).reshape(3, D)
    f2 = ffn_conv.shape[-1] * N_CHIPS
    fcc = jnp.moveaxis(fcc4.reshape(N_CHIPS, 2, 3, -1), 0, 2).reshape(2, 3, f2)

    def first(after, matmul):
        w = landed("hgrn_w_in", after, "gather_wait_0_own", which=(3,))
        proj = matmul(w, me_chip, None)
        others = [2 * (1 - xi) + yi, 2 * xi + (1 - yi), 2 * (1 - xi) + (1 - yi)]
        for j, s in enumerate(others):
            w = landed("hgrn_w_in", proj, "gather_wait_0_%d" % j, which=(j,))
            fs, fr, w, proj = _forward_start(w, proj, "gather_forward_start_0_%d" % j, which=(j,))
            w = _forward_wait(fs, fr, w, proj, "gather_forward_wait_0_%d" % j, which=(j,))
            travelling["hgrn_w_in"] = (travelling["hgrn_w_in"][0], w)
            proj = matmul(w, s.astype(jnp.int32).reshape(1), proj)
        return proj, w

    def arrive(name, after):
        fs, fr, w, after = _forward_start(landed(name, after, "gather_wait_" + name), after, "gather_forward_start_" + name)
        return functools.partial(_forward_wait, fs, fr, w, name="gather_forward_wait_" + name), after

    pending = []
    started = []

    def reduce_start(slot, grad, thru):
        t = sum(len(b[0]) for b in pending) + len(started)
        halves = grad.reshape(N_CHIPS, 2, -1, grad.shape[-1])
        ss, rs, src, land, thru = _sibling_start([halves], True, thru, "grad_pair_start_%d" % t)
        started.append((slot, t, ss, rs, src, land))
        return thru

    def reduce_finish(thru):
        k = len(pending)
        pair = []
        for slot, t, ss, rs, src, land in started:
            src, recv = _sibling_wait(ss, rs, src, land, True, thru, "grad_pair_wait_%d" % t)
            pair.append(_add_pair(src[0], recv[0], me_core, "grad_add_pair"))
        ss, rs, pair, land, thru = _chip_start(pair, thru, "grad_chip_start_%d" % k)
        pending.append(([s[0] for s in started], ss, rs, pair, land))
        started.clear()
        return thru

    lb8 = _lb_softmax(hgrn_lb_table)
    grad_x, small = _local_step(
        x[0], loss_target[0], norm_mix, norm_ffn, lb8, hgrn_out_norm, final_norm[None], scc, fcc, first, arrive,
        reduce_start, reduce_finish)

    small_names = ["loss", "norm_mix", "norm_ffn", "lb", "out_norm", "final_norm", "sc_conv", "ffn_conv"]
    parts = [small[n].astype(F32) for n in small_names]
    shapes = [p.shape for p in parts]
    tot = dict(zip(small_names, _unpack(reduce_finish(_all_sum(_pack(parts))), shapes)))
    loss = tot["loss"].reshape(())
    g_lb_table = _lb_table_grad(lb8, tot["lb"], hgrn_lb_table.shape[0])
    cw = sc_conv.shape[-1]
    g_sc_conv = lax.dynamic_slice_in_dim(tot["sc_conv"], me_chip[0] * cw, cw, axis=1)[None]
    cf = ffn_conv.shape[-1]
    g_ffn_conv = lax.dynamic_slice_in_dim(tot["ffn_conv"], me_chip[0] * cf, cf, axis=2)
    g_small = dict(norm_mix=tot["norm_mix"], norm_ffn=tot["norm_ffn"], hgrn_lb_table=g_lb_table,
                   hgrn_out_norm=tot["out_norm"], sc_conv=g_sc_conv, ffn_conv=g_ffn_conv, final_norm=tot["final_norm"])
    w_small = dict(norm_mix=norm_mix, norm_ffn=norm_ffn, hgrn_lb_table=hgrn_lb_table, hgrn_out_norm=hgrn_out_norm,
                   sc_conv=sc_conv, ffn_conv=ffn_conv, final_norm=final_norm)
    m_small = dict(norm_mix=m_norm_mix, norm_ffn=m_norm_ffn, hgrn_lb_table=m_hgrn_lb_table, hgrn_out_norm=m_hgrn_out_norm,
                   sc_conv=m_sc_conv, ffn_conv=m_ffn_conv, final_norm=m_final_norm)
    v_small = dict(norm_mix=v_norm_mix, norm_ffn=v_norm_ffn, hgrn_lb_table=v_hgrn_lb_table, hgrn_out_norm=v_hgrn_out_norm,
                   sc_conv=v_sc_conv, ffn_conv=v_ffn_conv, final_norm=v_final_norm)
    sm_names = list(g_small)
    sm_shapes = [w_small[n].shape for n in sm_names]
    d_s, m_s, v_s = _adamw(_pack([w_small[n] for n in sm_names]), _pack([g_small[n] for n in sm_names]),
                           _pack([m_small[n] for n in sm_names]), _pack([v_small[n] for n in sm_names]), "adamw_small")
    out_g, out_d, out_m, out_v = dict(g_small), {}, {}, {}
    for n, d_, m_, v_ in zip(sm_names, _unpack(d_s, sm_shapes), _unpack(m_s, sm_shapes), _unpack(v_s, sm_shapes)):
        out_d[n], out_m[n], out_v[n] = d_, m_, v_

    done = {}
    after = grad_x

    def add_and_share(k, after):
        slots, ss, rs, pair, land = pending[k]
        pair, recv = _chip_wait(ss, rs, pair, land, after, "grad_chip_wait_%d" % k)
        mine = [_add_chips(p, r, me_chip, "grad_add_chips") for p, r in zip(pair, recv)]
        ss, rs, mine, land, _ = _sibling_start(mine, False, None, "grad_share_start_%d" % k)
        return slots, ss, rs, mine, land

    def update(k, share, after):
        slots, ss, rs, mine, land = share
        mine, theirs = _sibling_wait(ss, rs, mine, land, False, after, "grad_share_wait_%d" % k)
        for (n, layer), gm, gr in zip(slots, mine, theirs):
            done[n] = _adamw_halves(flat2(big_w[n]), flat2(big_m[n]), flat2(big_v[n]), gm, gr, me_core, "adamw_" + n,
                                    layer=layer, prev=done.get(n))
        return done[slots[-1][0]][0]

    shares = []
    for k in range(len(pending) - 1):
        shares.append(add_and_share(k, after))
        after = shares[-1][3][0]
    for k, share in enumerate(shares):
        after = update(k, share, after)
    last = len(pending) - 1
    share = add_and_share(last, after)
    update(last, share, share[3][0])
    for n in big_names:
        out_g[n], out_d[n], out_m[n], out_v[n] = (a.reshape(big_w[n].shape) for a in done[n])

    order = ["norm_mix", "norm_ffn", "hgrn_w_in", "hgrn_lb_table", "hgrn_out_norm", "hgrn_w_out", "sc_w_in", "sc_conv",
             "sc_w_out", "ffn_w_up", "ffn_conv", "ffn_w_down", "final_norm"]
    return (loss, grad_x[None], *[out_g[n] for n in order], *[out_d[n] for n in order],
            *[out_m[n] for n in order], *[out_v[n] for n in order])
```
